```python
import math
import jax, jax.numpy as jnp
from jax import lax
import numpy as np

D_MODEL = 1024
BATCH = 4
SEQ = 8192
DEPTH = 4

GRID_W = 64
CTX_LEN = 256
BLOCK = 128
ROPE_BASE = 10000.0
EPS = 1e-6
NEG_INF = -1e30

DA_HEADS = 4
DA_DIM = 32
DA_VDIM = 2 * DA_DIM
HY_WIDTH = 256
HY_SHORT = 3
HY_BANDS = 16
HY_EMB = 1 + 2 * HY_BANDS
HY_FF = 64
HY_SHIFT = 0.05
HY_FAST_DECAY = 0.3
HY_SLOW_DECAY = 1.5
HY_TARGET = 1e-2
WA_HEADS = 4
WA_KV_HEADS = 2
WA_GROUP = WA_HEADS // WA_KV_HEADS
WA_DIM = 64
WINDOW = 128
CF_WIDTH = 256
CF_TAPS = 31
N_BRANCH = 4
BRANCH_W = 256
N_GROUPS = 4
EXP_PER_GROUP = 4
N_EXPERTS = N_GROUPS * EXP_PER_GROUP
TOP_K = 2
D_EXPERT = 512
MOE_BLOCK = 256

OFF_A = 0
W_A = 4 * DA_HEADS * DA_DIM + DA_HEADS * DA_VDIM
OFF_B = OFF_A + W_A
W_B = 3 * HY_WIDTH
OFF_C = OFF_B + W_B
W_C = (WA_HEADS + 2 * WA_KV_HEADS) * WA_DIM
OFF_D = OFF_C + W_C
W_D = 2 * CF_WIDTH
OFF_G = OFF_D + W_D
D_IN = OFF_G + N_BRANCH * D_MODEL

kernel_name = 'hybrid_diffusion_trunk'


def rms_norm(x, g):
    xf = x.astype(jnp.float32)
    y = xf * lax.rsqrt(jnp.mean(xf * xf, axis=-1, keepdims=True) + EPS)
    return (y * g.astype(jnp.float32)).astype(x.dtype)


def layer_norm(x, g, b):
    xf = x.astype(jnp.float32)
    xc = xf - jnp.mean(xf, axis=-1, keepdims=True)
    y = xc * lax.rsqrt(jnp.mean(xc * xc, axis=-1, keepdims=True) + EPS)
    return (y * g.astype(jnp.float32) + b.astype(jnp.float32)).astype(x.dtype)


def modulate(x, g, shift, scale):
    return rms_norm(x, g) * (1.0 + scale) + shift


def heads(z, off, n, d):
    return z[..., off:off + n * d].reshape(z.shape[:-1] + (n, d))


def grid_rope(x):
    L, d = x.shape[1], x.shape[-1]
    rows = L // GRID_W
    row = jnp.repeat(jnp.arange(rows, dtype=jnp.float32), GRID_W)
    col = jnp.tile(jnp.arange(GRID_W, dtype=jnp.float32), rows)
    qd = d // 4
    inv = ROPE_BASE ** (-jnp.arange(qd, dtype=jnp.float32) / qd)

    def rot(xa, pos):
        ang = pos[:, None] * inv[None, :]
        cos = jnp.cos(ang)[None, :, None, :]
        sin = jnp.sin(ang)[None, :, None, :]
        x1 = xa[..., :qd].astype(jnp.float32)
        x2 = xa[..., qd:].astype(jnp.float32)
        return jnp.concatenate([x1 * cos - x2 * sin, x2 * cos + x1 * sin], axis=-1)

    out = jnp.concatenate([rot(x[..., :d // 2], row), rot(x[..., d // 2:], col)], axis=-1)
    return out.astype(x.dtype)


def dw_conv(x, w, b):
    taps, ch = w.shape
    y = lax.conv_general_dilated(x, w.astype(x.dtype)[:, None, :], window_strides=(1,),
                                 padding=[(taps // 2, taps // 2)],
                                 dimension_numbers=('NWC', 'WIO', 'NWC'), feature_group_count=ch)
    return y + b.astype(x.dtype)


def da_queries(z, g):
    w = DA_HEADS * DA_DIM
    q1 = rms_norm(heads(z, OFF_A, DA_HEADS, DA_DIM), g)
    q2 = rms_norm(heads(z, OFF_A + w, DA_HEADS, DA_DIM), g)
    return q1, q2


def da_keys_values(z, g):
    w = DA_HEADS * DA_DIM
    k1 = rms_norm(heads(z, OFF_A + 2 * w, DA_HEADS, DA_DIM), g)
    k2 = rms_norm(heads(z, OFF_A + 3 * w, DA_HEADS, DA_DIM), g)
    v = heads(z, OFF_A + 4 * w, DA_HEADS, DA_VDIM)
    return k1, k2, v


def diff_attend(q1, q2, k1, k2, v, lam):
    scale = DA_DIM ** -0.5
    s1 = jnp.einsum('bhqd,bhkd->bhqk', q1, k1).astype(jnp.float32) * scale
    s2 = jnp.einsum('bhqd,bhkd->bhqk', q2, k2).astype(jnp.float32) * scale
    p = jax.nn.softmax(s1, axis=-1) - lam * jax.nn.softmax(s2, axis=-1)
    return jnp.einsum('bhqk,bhkd->bhqd', p.astype(v.dtype), v)


def hyena_filter(L, w1, b1, w2, b2, w3, b3, w4, freq):
    f32 = jnp.float32
    t = jnp.linspace(0.0, 1.0, L, dtype=f32)[:, None]
    w = (2.0 * math.pi / L) * jnp.arange(L, dtype=f32)[:, None]
    bands = jnp.linspace(1e-4, HY_BANDS - 1, HY_BANDS, dtype=f32)[None, :]
    feat = jnp.concatenate([t, jnp.cos(w * bands), jnp.sin(w * bands)], axis=-1)
    fr = freq.astype(f32)
    a = jnp.sin(fr * (feat @ w1.astype(f32) + b1.astype(f32)))
    a = jnp.sin(fr * (a @ w2.astype(f32) + b2.astype(f32)))
    a = jnp.sin(fr * (a @ w3.astype(f32) + b3.astype(f32)))
    coef = a @ w4.astype(f32)
    deltas = jnp.abs(jnp.linspace(math.log(HY_TARGET) / HY_FAST_DECAY,
                                  math.log(HY_TARGET) / HY_SLOW_DECAY, HY_WIDTH, dtype=f32))
    window = jnp.exp(-t * deltas[None, :]) + HY_SHIFT
    hf = coef[:, :HY_WIDTH] * window
    hb = coef[:, HY_WIDTH:] * window
    filt = jnp.concatenate([hf, jnp.zeros((1, HY_WIDTH), f32), hb[:0:-1]], axis=0)
    return filt * lax.rsqrt(jnp.sum(filt * filt, axis=0, keepdims=True) + EPS)


def fft_long_conv(u, filt, bias):
    L = u.shape[1]
    uf = u.astype(jnp.float32)
    spec = jnp.fft.rfft(uf, n=2 * L, axis=1) * jnp.fft.rfft(filt, axis=0)[None]
    y = jnp.fft.irfft(spec, n=2 * L, axis=1)[:, :L]
    return (y + uf * bias.astype(jnp.float32)).astype(u.dtype)


def hyena(zb, conv_w, conv_b, filt, bias):
    u = dw_conv(zb, conv_w, conv_b)
    x0, x1, v = jnp.split(u, 3, axis=-1)
    return x0 * fft_long_conv(v * x1, filt, bias)


def wa_queries(z, g):
    return rms_norm(heads(z, OFF_C, WA_HEADS, WA_DIM), g)


def wa_keys_values(z, g):
    k = rms_norm(heads(z, OFF_C + WA_HEADS * WA_DIM, WA_KV_HEADS, WA_DIM), g)
    v = heads(z, OFF_C + (WA_HEADS + WA_KV_HEADS) * WA_DIM, WA_KV_HEADS, WA_DIM)
    return k, v


def window_gqa_latent(q, k, v, kc, vc, sink):
    B, S = q.shape[:2]
    nb = S // BLOCK
    qb = q.reshape(B, nb, BLOCK, WA_KV_HEADS, WA_GROUP, WA_DIM)

    def band(a):
        ap = jnp.pad(a, ((0, 0), (BLOCK, BLOCK), (0, 0), (0, 0))).reshape(B, nb + 2, BLOCK, WA_KV_HEADS, WA_DIM)
        return jnp.concatenate([ap[:, :-2], ap[:, 1:-1], ap[:, 2:]], axis=2)

    kw, vw = band(k), band(v)
    blk = jnp.arange(nb)[:, None, None]
    qpos = blk * BLOCK + jnp.arange(BLOCK)[None, :, None]
    kpos = (blk - 1) * BLOCK + jnp.arange(3 * BLOCK)[None, None, :]
    valid = (jnp.abs(qpos - kpos) <= WINDOW) & (kpos >= 0) & (kpos < S)
    scale = WA_DIM ** -0.5
    sw = jnp.einsum('bnqhgd,bnkhd->bnhgqk', qb, kw).astype(jnp.float32) * scale
    sw = jnp.where(valid[None, :, None, None], sw, NEG_INF)
    sc = jnp.einsum('bnqhgd,bchd->bnhgqc', qb, kc).astype(jnp.float32) * scale
    snk = jnp.broadcast_to(sink.astype(jnp.float32).reshape(WA_KV_HEADS, WA_GROUP)[None, None, :, :, None, None],
                           sw.shape[:-1] + (1,))
    p = jax.nn.softmax(jnp.concatenate([sw, sc, snk], axis=-1), axis=-1)
    nw = 3 * BLOCK
    nc = kc.shape[1]
    o = (jnp.einsum('bnhgqk,bnkhd->bnqhgd', p[..., :nw].astype(v.dtype), vw)
         + jnp.einsum('bnhgqc,bchd->bnqhgd', p[..., nw:nw + nc].astype(v.dtype), vc))
    return o.reshape(B, S, WA_HEADS * WA_DIM)


def sink_attention_ctx(qc, kc, vc, sink):
    B, C = qc.shape[:2]
    qg = qc.reshape(B, C, WA_KV_HEADS, WA_GROUP, WA_DIM)
    s = jnp.einsum('bqhgd,bkhd->bhgqk', qg, kc).astype(jnp.float32) * (WA_DIM ** -0.5)
    snk = jnp.broadcast_to(sink.astype(jnp.float32).reshape(WA_KV_HEADS, WA_GROUP)[None, :, :, None, None],
                           s.shape[:-1] + (1,))
    p = jax.nn.softmax(jnp.concatenate([s, snk], axis=-1), axis=-1)[..., :C]
    o = jnp.einsum('bhgqk,bkhd->bqhgd', p.astype(vc.dtype), vc)
    return o.reshape(B, C, WA_HEADS * WA_DIM)


def conformer_conv(zd, dw_w, dw_b, ln_g, ln_b):
    a, b = jnp.split(zd, 2, axis=-1)
    u = dw_conv(a * jax.nn.sigmoid(b), dw_w, dw_b)
    return jax.nn.silu(layer_norm(u, ln_g, ln_b))


def token_mixers(h, hc, layer, need_ctx, p):
    B, S, _ = h.shape
    z = h @ p['w_in']
    zc = hc @ p['w_in']
    tr = lambda a: a.transpose(0, 2, 1, 3)
    hf = (p['hf_w1'], p['hf_b1'], p['hf_w2'], p['hf_b2'], p['hf_w3'], p['hf_b3'], p['hf_w4'], p['hf_freq'])

    lam_init = 0.8 - 0.6 * math.exp(-0.3 * layer)
    lv = p['da_lam'].astype(jnp.float32)
    lam = jnp.exp(jnp.sum(lv[0] * lv[1])) - jnp.exp(jnp.sum(lv[2] * lv[3])) + lam_init
    q1, q2 = da_queries(z, p['da_qn'])
    k1, k2, v = da_keys_values(z, p['da_kn'])
    q1, q2, k1, k2 = grid_rope(q1), grid_rope(q2), grid_rope(k1), grid_rope(k2)
    k1c, k2c, vc = da_keys_values(zc, p['da_kn'])
    kk1 = tr(jnp.concatenate([k1, k1c], axis=1))
    kk2 = tr(jnp.concatenate([k2, k2c], axis=1))
    vv = tr(jnp.concatenate([v, vc], axis=1))
    nb = S // BLOCK
    to_blocks = lambda q: q.reshape(B, nb, BLOCK, DA_HEADS, DA_DIM).transpose(1, 0, 3, 2, 4)
    ob = lax.map(lambda qs: diff_attend(qs[0], qs[1], kk1, kk2, vv, lam), (to_blocks(q1), to_blocks(q2)))
    ya = ob.transpose(1, 0, 3, 2, 4).reshape(B, S, DA_HEADS, DA_VDIM)
    ya = (rms_norm(ya, p['da_subln']) * (1.0 - lam_init)).reshape(B, S, BRANCH_W)

    yb = hyena(z[..., OFF_B:OFF_B + W_B], p['hy_conv_w'], p['hy_conv_b'], hyena_filter(S, *hf), p['hy_bias'])

    qw = grid_rope(wa_queries(z, p['wa_qn']))
    kw, vw = wa_keys_values(z, p['wa_kn'])
    kw = grid_rope(kw)
    kwc, vwc = wa_keys_values(zc, p['wa_kn'])
    yc_ = window_gqa_latent(qw, kw, vw, kwc, vwc, p['wa_sink'])

    yd = conformer_conv(z[..., OFF_D:OFF_D + W_D], p['cf_dw_w'], p['cf_dw_b'], p['cf_ln_g'], p['cf_ln_b'])

    def merge(t, ys):
        acc = 0.0
        for bi, yi in enumerate(ys):
            g = jax.nn.sigmoid(t[..., OFF_G + bi * D_MODEL:OFF_G + (bi + 1) * D_MODEL]
                               + p['b_gate'][bi * D_MODEL:(bi + 1) * D_MODEL])
            acc = acc + g * (yi @ p['w_branch'][bi])
        return acc @ p['w_out']

    y = merge(z, (ya, yb, yc_, yd))
    if not need_ctx:
        return y, None

    C = hc.shape[1]
    q1c, q2c = da_queries(zc, p['da_qn'])
    yca = tr(diff_attend(tr(q1c), tr(q2c), tr(k1c), tr(k2c), tr(vc), lam))
    yca = (rms_norm(yca, p['da_subln']) * (1.0 - lam_init)).reshape(B, C, BRANCH_W)
    ycb = hyena(zc[..., OFF_B:OFF_B + W_B], p['hy_conv_w'], p['hy_conv_b'], hyena_filter(C, *hf), p['hy_bias'])
    ycc = sink_attention_ctx(wa_queries(zc, p['wa_qn']), kwc, vwc, p['wa_sink'])
    ycd = conformer_conv(zc[..., OFF_D:OFF_D + W_D], p['cf_dw_w'], p['cf_dw_b'], p['cf_ln_g'], p['cf_ln_b'])
    yc = merge(zc, (yca, ycb, ycc, ycd))
    return y, yc


def hier_moe(h, w_rg, b_rg, w_re, b_re, w1, w3, w2):
    shp = h.shape
    t = h.reshape(-1, D_MODEL)
    T = t.shape[0]
    lg = (t @ w_rg).astype(jnp.float32) + b_rg.astype(jnp.float32)
    pg = jax.nn.softmax(lg, axis=-1)
    gi = jnp.argmax(lg, axis=-1)
    gw = jnp.take_along_axis(pg, gi[:, None], axis=-1)
    le = ((t @ w_re).astype(jnp.float32) + b_re.astype(jnp.float32)).reshape(T, N_GROUPS, EXP_PER_GROUP)
    le = jnp.take_along_axis(le, gi[:, None, None], axis=1)[:, 0]
    top_p, top_i = lax.top_k(jax.nn.softmax(le, axis=-1), TOP_K)
    top_p = top_p / jnp.sum(top_p, axis=-1, keepdims=True) * gw
    e_a = (gi[:, None] * EXP_PER_GROUP + top_i).reshape(-1)
    w_a = top_p.reshape(-1)
    A = T * TOP_K
    tok_a = jnp.arange(A) // TOP_K
    order = jnp.argsort(e_a)
    e_s, tok_s, w_s = e_a[order], tok_a[order], w_a[order]
    counts = jnp.bincount(e_a, length=N_EXPERTS)
    padded = ((counts + MOE_BLOCK - 1) // MOE_BLOCK) * MOE_BLOCK
    pad_end = jnp.cumsum(padded)
    pad_start = pad_end - padded
    seg_start = jnp.cumsum(counts) - counts
    dest = pad_start[e_s] + jnp.arange(A) - seg_start[e_s]
    P = (-(-A // MOE_BLOCK)) * MOE_BLOCK + N_EXPERTS * MOE_BLOCK
    n_blk = P // MOE_BLOCK
    buf_tok = jnp.zeros((P,), jnp.int32).at[dest].set(tok_s.astype(jnp.int32))
    buf_w = jnp.zeros((P,), jnp.float32).at[dest].set(w_s)
    blk_exp = jnp.minimum(jnp.searchsorted(pad_end, jnp.arange(n_blk) * MOE_BLOCK, side='right'), N_EXPERTS - 1)
    xb = t[buf_tok].reshape(n_blk, MOE_BLOCK, D_MODEL)

    def expert_block(args):
        xblk, e = args
        u = jax.nn.silu(xblk @ w1[e]) * (xblk @ w3[e])
        return u @ w2[e]

    yb = lax.map(expert_block, (xb, blk_exp)).reshape(P, D_MODEL)
    y = jax.ops.segment_sum(yb.astype(jnp.float32) * buf_w[:, None], buf_tok, num_segments=T)
    return y.astype(h.dtype).reshape(shp)


def setup_inputs(seed: int = 0) -> dict:
    key = jax.random.key(seed)
    ks = iter(jax.random.split(key, 48))
    f32 = jnp.float32
    D = D_MODEL

    def nrm(shape, scale=1.0):
        return jax.random.normal(next(ks), shape, f32) * scale

    def gain(shape):
        return 1.0 + nrm(shape, 0.02)

    return {
        'x': nrm((BATCH, SEQ, D)),
        'c': nrm((BATCH, D)),
        'ctx': nrm((BATCH, CTX_LEN, D)),
        'c_ctx': nrm((D,)),
        'w_mod': nrm((DEPTH, D, 6 * D), 0.5 * D ** -0.5),
        'b_mod': nrm((DEPTH, 6 * D), 0.02),
        'norm1_g': gain((DEPTH, D)),
        'norm2_g': gain((DEPTH, D)),
        'w_in': nrm((DEPTH, D, D_IN), D ** -0.5),
        'b_gate': nrm((DEPTH, N_BRANCH * D), 0.02),
        'da_qn': gain((DEPTH, DA_DIM)),
        'da_kn': gain((DEPTH, DA_DIM)),
        'da_lam': nrm((DEPTH, 4, DA_DIM), 0.1),
        'da_subln': gain((DEPTH, DA_VDIM)),
        'hy_conv_w': nrm((DEPTH, HY_SHORT, W_B), HY_SHORT ** -0.5),
        'hy_conv_b': nrm((DEPTH, W_B), 0.02),
        'hf_w1': nrm((DEPTH, HY_EMB, HY_FF), HY_EMB ** -0.5),
        'hf_b1': nrm((DEPTH, HY_FF), 0.02),
        'hf_w2': nrm((DEPTH, HY_FF, HY_FF), HY_FF ** -0.5),
        'hf_b2': nrm((DEPTH, HY_FF), 0.02),
        'hf_w3': nrm((DEPTH, HY_FF, HY_FF), HY_FF ** -0.5),
        'hf_b3': nrm((DEPTH, HY_FF), 0.02),
        'hf_w4': nrm((DEPTH, HY_FF, 2 * HY_WIDTH), HY_FF ** -0.5),
        'hf_freq': gain((DEPTH, HY_FF)),
        'hy_bias': nrm((DEPTH, HY_WIDTH)),
        'wa_qn': gain((DEPTH, WA_DIM)),
        'wa_kn': gain((DEPTH, WA_DIM)),
        'wa_sink': nrm((DEPTH, WA_HEADS), 0.5),
        'cf_dw_w': nrm((DEPTH, CF_TAPS, CF_WIDTH), CF_TAPS ** -0.5),
        'cf_dw_b': nrm((DEPTH, CF_WIDTH), 0.02),
        'cf_ln_g': gain((DEPTH, CF_WIDTH)),
        'cf_ln_b': nrm((DEPTH, CF_WIDTH), 0.02),
        'w_branch': nrm((DEPTH, N_BRANCH, BRANCH_W, D), BRANCH_W ** -0.5),
        'w_out': nrm((DEPTH, D, D), D ** -0.5),
        'w_rg': nrm((DEPTH, D, N_GROUPS), D ** -0.5),
        'b_rg': nrm((DEPTH, N_GROUPS), 0.01),
        'w_re': nrm((DEPTH, D, N_EXPERTS), D ** -0.5),
        'b_re': nrm((DEPTH, N_EXPERTS), 0.01),
        'w1': nrm((DEPTH, N_EXPERTS, D, D_EXPERT), D ** -0.5),
        'w3': nrm((DEPTH, N_EXPERTS, D, D_EXPERT), D ** -0.5),
        'w2': nrm((DEPTH, N_EXPERTS, D_EXPERT, D), D_EXPERT ** -0.5),
    }


def reference(x, c, ctx, c_ctx, w_mod, b_mod, norm1_g, norm2_g, w_in, b_gate, da_qn, da_kn, da_lam,
              da_subln, hy_conv_w, hy_conv_b, hf_w1, hf_b1, hf_w2, hf_b2, hf_w3, hf_b3, hf_w4, hf_freq,
              hy_bias, wa_qn, wa_kn, wa_sink, cf_dw_w, cf_dw_b, cf_ln_g, cf_ln_b, w_branch, w_out,
              w_rg, b_rg, w_re, b_re, w1, w3, w2):
    xc = ctx
    sc = jax.nn.silu(c)
    scc = jax.nn.silu(c_ctx)
    for l in range(DEPTH):
        last = l == DEPTH - 1
        mod = (sc @ w_mod[l] + b_mod[l])[:, None, :]
        modc = scc @ w_mod[l] + b_mod[l]
        sh1, s1, g1, sh2, s2, g2 = jnp.split(mod, 6, axis=-1)
        csh1, cs1, cg1, csh2, cs2, cg2 = jnp.split(modc, 6, axis=-1)
        p = dict(w_in=w_in[l], b_gate=b_gate[l], da_qn=da_qn[l], da_kn=da_kn[l], da_lam=da_lam[l],
                 da_subln=da_subln[l], hy_conv_w=hy_conv_w[l], hy_conv_b=hy_conv_b[l],
                 hf_w1=hf_w1[l], hf_b1=hf_b1[l], hf_w2=hf_w2[l], hf_b2=hf_b2[l], hf_w3=hf_w3[l],
                 hf_b3=hf_b3[l], hf_w4=hf_w4[l], hf_freq=hf_freq[l], hy_bias=hy_bias[l],
                 wa_qn=wa_qn[l], wa_kn=wa_kn[l], wa_sink=wa_sink[l], cf_dw_w=cf_dw_w[l],
                 cf_dw_b=cf_dw_b[l], cf_ln_g=cf_ln_g[l], cf_ln_b=cf_ln_b[l], w_branch=w_branch[l],
                 w_out=w_out[l])
        h = modulate(x, norm1_g[l], sh1, s1)
        hc = modulate(xc, norm1_g[l], csh1, cs1)
        y, yc = token_mixers(h, hc, l, not last, p)
        x = x + g1 * y
        if not last:
            xc = xc + cg1 * yc
            hc2 = modulate(xc, norm2_g[l], csh2, cs2)
            xc = xc + cg2 * hier_moe(hc2, w_rg[l], b_rg[l], w_re[l], b_re[l], w1[l], w3[l], w2[l])
        h2 = modulate(x, norm2_g[l], sh2, s2)
        x = x + g2 * hier_moe(h2, w_rg[l], b_rg[l], w_re[l], b_re[l], w1[l], w3[l], w2[l])
    return x
```

```python
import functools
import math

import jax
import jax.numpy as jnp
from jax import lax
from jax.experimental import pallas as pl
from jax.experimental.pallas import tpu as pltpu

F32 = jnp.float32
BF16 = jnp.bfloat16
HI = lax.Precision.HIGHEST

GRID_W = 64
BLOCK = 128
ROPE_BASE = 10000.0
EPS = 1e-6
NEG_INF = -1e30

DA_HEADS = 4
DA_DIM = 32
DA_VDIM = 64
HY_WIDTH = 256
HY_BANDS = 16
HY_FF = 64
HY_SHIFT = 0.05
HY_FAST_DECAY = 0.3
HY_SLOW_DECAY = 1.5
HY_TARGET = 1e-2
WA_HEADS = 4
WA_KV_HEADS = 2
WA_GROUP = 2
WA_DIM = 64
CF_WIDTH = 256
CF_TAPS = 31
N_BRANCH = 4
BRANCH_W = 256
N_GROUPS = 4
EXP_PER_GROUP = 4
N_EXPERTS = 16

W_A = 4 * DA_HEADS * DA_DIM + DA_HEADS * DA_VDIM
W_B = 3 * HY_WIDTH
W_C = (WA_HEADS + 2 * WA_KV_HEADS) * WA_DIM
W_D = 2 * CF_WIDTH
OFF_B = W_A
OFF_C = OFF_B + W_B
OFF_D = OFF_C + W_C
OFF_G = OFF_D + W_D

LANES = 128
SUBLANES = 8
VMEM_LIMIT = 56 * 1024 * 1024


def _cp(*sem):
    return pltpu.CompilerParams(dimension_semantics=sem, vmem_limit_bytes=VMEM_LIMIT)


def _rms(xf):
    return xf * lax.rsqrt(jnp.mean(xf * xf, axis=-1, keepdims=True) + EPS)


def _silu(x):
    return x * jax.nn.sigmoid(x)


def _mod_kernel(c_ref, w_ref, b_ref, o_ref):
    s = _silu(c_ref[...])
    o_ref[0] = jnp.dot(s, w_ref[0], precision=HI, preferred_element_type=F32) + b_ref[0]


def _mod_vectors(crows, w_mod, b_mod):
    depth, d, n = w_mod.shape
    r = crows.shape[0]
    tn = 1536
    return pl.pallas_call(
        _mod_kernel,
        grid=(depth, n // tn),
        in_specs=[pl.BlockSpec((r, d), lambda l, j: (0, 0)),
                  pl.BlockSpec((1, d, tn), lambda l, j: (l, 0, j)),
                  pl.BlockSpec((1, 1, tn), lambda l, j: (l, 0, j))],
        out_specs=pl.BlockSpec((1, r, tn), lambda l, j: (l, 0, j)),
        out_shape=jax.ShapeDtypeStruct((depth, r, n), F32),
        compiler_params=_cp("arbitrary", "arbitrary"),
        name="mod_vectors",
    )(crows, w_mod, b_mod.reshape(depth, 1, n))


def _inproj_kernel(x_ref, m_ref, g_ref, wa, wb, wc, wd, wg, oa, ob, oc, od, og):
    d = x_ref.shape[-1]
    x = x_ref[0]
    shift = m_ref[0, :, 0:d]
    scale = m_ref[0, :, d:2 * d]
    h = (_rms(x) * g_ref[...] * (1.0 + scale) + shift).astype(BF16)
    for w, o in ((wa, oa), (wb, ob), (wc, oc), (wd, od)):
        o[0] = jnp.dot(h, w[...], preferred_element_type=F32)
    for k in range(N_BRANCH):
        og[0, :, k * d:(k + 1) * d] = jnp.dot(h, wg[:, k * d:(k + 1) * d], preferred_element_type=F32)


def _in_proj(x, mod, g, ws):
    bx, lx, d = x.shape
    tm = min(256, lx)
    widths = [w.shape[1] for w in ws]
    const = lambda b, i: (0, 0)
    return pl.pallas_call(
        _inproj_kernel,
        grid=(bx, lx // tm),
        in_specs=[pl.BlockSpec((1, tm, d), lambda b, i: (b, i, 0)),
                  pl.BlockSpec((1, 1, mod.shape[-1]), lambda b, i: (b, 0, 0)),
                  pl.BlockSpec((1, d), const)]
                 + [pl.BlockSpec((d, w), const) for w in widths],
        out_specs=[pl.BlockSpec((1, tm, w), lambda b, i: (b, i, 0)) for w in widths],
        out_shape=[jax.ShapeDtypeStruct((bx, lx, w), F32) for w in widths],
        compiler_params=_cp("parallel", "parallel"),
        name="in_proj",
    )(x, mod, g.reshape(1, d), *ws)


def _rope_tables(s, d, reps):
    rows = s // GRID_W
    row = jnp.repeat(jnp.arange(rows, dtype=F32), GRID_W)
    col = jnp.tile(jnp.arange(GRID_W, dtype=F32), rows)
    qd = d // 4
    inv = ROPE_BASE ** (-jnp.arange(qd, dtype=F32) / qd)
    ar = row[:, None] * inv[None, :]
    ac = col[:, None] * inv[None, :]
    z = jnp.zeros_like(ar)
    cos = jnp.concatenate([jnp.cos(ar), jnp.cos(ar), jnp.cos(ac), jnp.cos(ac)], axis=-1)
    sin_up = jnp.concatenate([-jnp.sin(ar), z, -jnp.sin(ac), z], axis=-1)
    sin_dn = jnp.concatenate([z, jnp.sin(ar), z, jnp.sin(ac)], axis=-1)
    t = lambda a: jnp.tile(a, (1, reps))
    return t(cos), t(sin_up), t(sin_dn)


def _unit_tables(s, w):
    return jnp.ones((s, w), F32), jnp.zeros((s, w), F32), jnp.zeros((s, w), F32)


def _group_ones(width, group):
    i = jnp.arange(width) // group
    return (i[:, None] == i[None, :]).astype(F32)


def _norm_rope(x, gmat, gain, cos, sup, sdn, group, qd):
    w = x.shape[-1]
    ss = jnp.dot(x * x, gmat, precision=HI, preferred_element_type=F32) * (1.0 / group)
    xn = x * lax.rsqrt(ss + EPS) * gain
    return xn * cos + pltpu.roll(xn, w - qd, 1) * sup + pltpu.roll(xn, qd, 1) * sdn


def _prep_a_kernel(z_ref, cos_ref, sup_ref, sdn_ref, gm_ref, qn_ref, kn_ref, q1, q2, k1, k2, v):
    hw = DA_HEADS * DA_DIM
    cos, sup, sdn, gm = cos_ref[...], sup_ref[...], sdn_ref[...], gm_ref[...]
    scale = DA_DIM ** -0.5
    for t, (o, gain, sc) in enumerate(((q1, qn_ref, scale), (q2, qn_ref, scale), (k1, kn_ref, 1.0), (k2, kn_ref, 1.0))):
        x = z_ref[0, :, t * hw:(t + 1) * hw]
        y = _norm_rope(x, gm, gain[...], cos, sup, sdn, DA_DIM, DA_DIM // 4)
        o[0] = (y * sc).astype(BF16)
    v[0] = z_ref[0, :, 4 * hw:].astype(BF16)


def _prep_a(za, tables, qn, kn):
    bx, lx, _ = za.shape
    tl = min(256, lx)
    hw = DA_HEADS * DA_DIM
    gm = _group_ones(hw, DA_DIM)
    row = lambda b, i: (b, i, 0)
    tab = pl.BlockSpec((tl, hw), lambda b, i: (i, 0))
    const = lambda b, i: (0, 0)
    outs = [jax.ShapeDtypeStruct((bx, lx, hw), BF16)] * 4 + [jax.ShapeDtypeStruct((bx, lx, DA_HEADS * DA_VDIM), BF16)]
    return pl.pallas_call(
        _prep_a_kernel,
        grid=(bx, lx // tl),
        in_specs=[pl.BlockSpec((1, tl, W_A), row), tab, tab, tab,
                  pl.BlockSpec((hw, hw), const), pl.BlockSpec((1, hw), const), pl.BlockSpec((1, hw), const)],
        out_specs=[pl.BlockSpec((1, tl, hw), row)] * 4 + [pl.BlockSpec((1, tl, DA_HEADS * DA_VDIM), row)],
        out_shape=outs,
        compiler_params=_cp("parallel", "parallel"),
        name="prep_a",
    )(za, *tables, gm, jnp.tile(qn, DA_HEADS).reshape(1, hw), jnp.tile(kn, DA_HEADS).reshape(1, hw))


def _dattn_kernel(lam_ref, sg_ref, q1_ref, q2_ref, k1_ref, k2_ref, v_ref, o_ref, *, tk, lam_init):
    lv = lam_ref[...]
    lam = (jnp.exp(jnp.sum(lv[0:1] * lv[1:2], keepdims=True)) - jnp.exp(jnp.sum(lv[2:3] * lv[3:4], keepdims=True))
           + lam_init)
    q1 = q1_ref[0, 0]
    q2 = q2_ref[0, 0]
    tq = q1.shape[0]
    nk = k1_ref.shape[2] // tk

    def update(q, kc, vc, m, l, a):
        s = lax.dot_general(q, kc, (((1,), (1,)), ((), ())), preferred_element_type=F32)
        mn = jnp.maximum(m, jnp.max(s, axis=-1, keepdims=True))
        al = jnp.exp(m - mn)
        p = jnp.exp(s - mn)
        l = al * l + jnp.sum(p, axis=-1, keepdims=True)
        a = al * a + jnp.dot(p.astype(BF16), vc, preferred_element_type=F32)
        return mn, l, a

    def body(j, c):
        m1, l1, a1, m2, l2, a2 = c
        off = pl.multiple_of(j * tk, tk)
        vc = v_ref[0, 0, pl.ds(off, tk), :]
        m1, l1, a1 = update(q1, k1_ref[0, 0, pl.ds(off, tk), :], vc, m1, l1, a1)
        m2, l2, a2 = update(q2, k2_ref[0, 0, pl.ds(off, tk), :], vc, m2, l2, a2)
        return m1, l1, a1, m2, l2, a2

    m0 = jnp.full((tq, 1), NEG_INF, F32)
    l0 = jnp.zeros((tq, 1), F32)
    a0 = jnp.zeros((tq, DA_VDIM), F32)
    m1, l1, a1, m2, l2, a2 = lax.fori_loop(0, nk, body, (m0, l0, a0, m0, l0, a0))
    o = a1 / l1 - lam * (a2 / l2)
    o_ref[0, 0] = _rms(o) * sg_ref[...] * (1.0 - lam_init)


def _diff_attention(q1, q2, k1, k2, v, lam_p, subln, lam_init):
    b, h, lq, dd = q1.shape
    lk = k1.shape[2]
    tq = min(256, lq)
    tk = min(256, lk)
    qs = pl.BlockSpec((1, 1, tq, dd), lambda b_, h_, i: (b_, h_, i, 0))
    ks = pl.BlockSpec((1, 1, lk, dd), lambda b_, h_, i: (b_, h_, 0, 0))
    const = lambda b_, h_, i: (0, 0)
    return pl.pallas_call(
        functools.partial(_dattn_kernel, tk=tk, lam_init=lam_init),
        grid=(b, h, lq // tq),
        in_specs=[pl.BlockSpec(lam_p.shape, const), pl.BlockSpec((1, DA_VDIM), const), qs, qs, ks, ks,
                  pl.BlockSpec((1, 1, lk, DA_VDIM), lambda b_, h_, i: (b_, h_, 0, 0))],
        out_specs=pl.BlockSpec((1, 1, tq, DA_VDIM), lambda b_, h_, i: (b_, h_, i, 0)),
        out_shape=jax.ShapeDtypeStruct((b, h, lq, DA_VDIM), F32),
        compiler_params=_cp("parallel", "parallel", "arbitrary"),
        name="diff_attention",
    )(lam_p, subln.reshape(1, DA_VDIM), q1, q2, k1, k2, v)


def _split_heads(a, n, d):
    b, l, _ = a.shape
    return a.reshape(b, l, n, d).transpose(0, 2, 1, 3)


def _merge_heads(a):
    b, n, l, d = a.shape
    return a.transpose(0, 2, 1, 3).reshape(b, l, n * d)


def _hy_prep_kernel(zp_ref, zc_ref, zn_ref, w_ref, b_ref, p_ref, x0_ref, ext):
    i = pl.program_id(1)
    last = pl.num_programs(1) - 1
    tl = zc_ref.shape[1]
    h = SUBLANES
    ext[0:h] = jnp.where(i == 0, 0.0, zp_ref[0])
    ext[h:h + tl] = zc_ref[0]
    ext[h + tl:] = jnp.where(i == last, 0.0, zn_ref[0])
    w = w_ref[...]
    u = (ext[pl.ds(h - 1, tl), :] * w[0:1] + ext[pl.ds(h, tl), :] * w[1:2] + ext[pl.ds(h + 1, tl), :] * w[2:3]
         + b_ref[...])
    hw = HY_WIDTH
    x0_ref[0] = u[:, 0:hw]
    p_ref[0] = u[:, 2 * hw:3 * hw] * u[:, hw:2 * hw]


def _hy_prep(zb, conv_w, conv_b):
    bx, lx, w = zb.shape
    tl = min(256, lx)
    h = SUBLANES
    nh = lx // h
    per = tl // h
    out = jax.ShapeDtypeStruct((bx, lx, HY_WIDTH), F32)
    return pl.pallas_call(
        _hy_prep_kernel,
        grid=(bx, lx // tl),
        in_specs=[pl.BlockSpec((1, h, w), lambda b, i: (b, jnp.maximum(i * per - 1, 0), 0)),
                  pl.BlockSpec((1, tl, w), lambda b, i: (b, i, 0)),
                  pl.BlockSpec((1, h, w), lambda b, i: (b, jnp.minimum((i + 1) * per, nh - 1), 0)),
                  pl.BlockSpec(conv_w.shape, lambda b, i: (0, 0)),
                  pl.BlockSpec((1, w), lambda b, i: (0, 0))],
        out_specs=[pl.BlockSpec((1, tl, HY_WIDTH), lambda b, i: (b, i, 0))] * 2,
        out_shape=[out, out],
        scratch_shapes=[pltpu.VMEM((tl + 2 * h, w), F32)],
        compiler_params=_cp("parallel", "parallel"),
        name="hyena_prep",
    )(zb, zb, zb, conv_w, conv_b.reshape(1, w))


def _hy_filter_kernel(feat_ref, w1, b1, w2, b2, w3, b3, w4, fr_ref, dl_ref, filt_ref, ssq_ref, *, s):
    i = pl.program_id(0)
    tr = feat_ref.shape[0]
    feat = feat_ref[...]
    fr = fr_ref[...]
    dot = lambda a, w: jnp.dot(a, w[...], precision=HI, preferred_element_type=F32)
    a = jnp.sin(fr * (dot(feat, w1) + b1[...]))
    a = jnp.sin(fr * (dot(a, w2) + b2[...]))
    a = jnp.sin(fr * (dot(a, w3) + b3[...]))
    coef = dot(a, w4)
    n = i * tr + lax.broadcasted_iota(jnp.int32, (tr, 1), 0)
    window = jnp.exp(-feat[:, 0:1] * dl_ref[...]) + HY_SHIFT
    half = jnp.where(n < s, coef[:, :HY_WIDTH], coef[:, HY_WIDTH:])
    filt = jnp.where(n == s, 0.0, half * window)
    filt_ref[...] = filt

    @pl.when(i == 0)
    def _():
        ssq_ref[...] = jnp.zeros_like(ssq_ref)

    ssq_ref[...] += jnp.sum(filt * filt, axis=0, keepdims=True)


def _hy_feat(s):
    t = jnp.linspace(0.0, 1.0, s, dtype=F32)[:, None]
    w = (2.0 * math.pi / s) * jnp.arange(s, dtype=F32)[:, None]
    bands = jnp.linspace(1e-4, HY_BANDS - 1, HY_BANDS, dtype=F32)[None, :]
    feat = jnp.concatenate([t, jnp.cos(w * bands), jnp.sin(w * bands)], axis=-1)
    feat = jnp.concatenate([feat, feat[:1], feat[:0:-1]], axis=0)
    return jnp.pad(feat, ((0, 0), (0, LANES - feat.shape[1])))


def _hy_filter(s, feat, w1, b1, w2, b2, w3, b3, w4, freq):
    n = 2 * s
    tr = min(512, n)
    deltas = jnp.abs(jnp.linspace(math.log(HY_TARGET) / HY_FAST_DECAY, math.log(HY_TARGET) / HY_SLOW_DECAY,
                                  HY_WIDTH, dtype=F32)).reshape(1, HY_WIDTH)
    w1p = jnp.pad(w1, ((0, LANES - w1.shape[0]), (0, 0)))
    row = lambda a: a.reshape(1, -1)
    args = (feat, w1p, row(b1), w2, row(b2), w3, row(b3), w4, row(freq), deltas)
    const = lambda i: (0, 0)
    return pl.pallas_call(
        functools.partial(_hy_filter_kernel, s=s),
        grid=(n // tr,),
        in_specs=[pl.BlockSpec((tr, LANES), lambda i: (i, 0))] + [pl.BlockSpec(a.shape, const) for a in args[1:]],
        out_specs=[pl.BlockSpec((tr, HY_WIDTH), lambda i: (i, 0)), pl.BlockSpec((1, HY_WIDTH), const)],
        out_shape=[jax.ShapeDtypeStruct((n, HY_WIDTH), F32), jax.ShapeDtypeStruct((1, HY_WIDTH), F32)],
        compiler_params=_cp("arbitrary"),
        name="hyena_filter",
    )(*args)


def _dft_factors(n):
    lg = n.bit_length() - 1
    assert 1 << lg == n
    n1 = 1 << (lg // 2)
    return n1, n // n1


def _dft_tables(n):
    n1, n2 = _dft_factors(n)
    ia = jnp.arange(n1, dtype=jnp.int32)
    ang1 = (2.0 * math.pi / n1) * ((ia[:, None] * ia[None, :]) % n1).astype(F32)
    f1 = jnp.concatenate([jnp.cos(ang1), -jnp.sin(ang1)], axis=0)
    c = jnp.arange(n1, dtype=jnp.int32)[:, None, None]
    d = jnp.arange(n2, dtype=jnp.int32)[None, :, None]
    b = jnp.arange(n2, dtype=jnp.int32)[None, None, :]
    ang = (2.0 * math.pi / n) * ((b * (c + n1 * d)) % n).astype(F32)
    re, im = jnp.cos(ang), -jnp.sin(ang)
    m1 = jnp.concatenate([jnp.concatenate([re, -im], axis=2), jnp.concatenate([im, re], axis=2)], axis=1)
    m2 = jnp.swapaxes(m1, 1, 2)
    ang4 = ang1[: n1 // 2]
    f4 = jnp.concatenate([jnp.cos(ang4), -jnp.sin(ang4)], axis=1) * (1.0 / n)
    return f1, m1, m2, f4


def _hy_stage1_kernel(f_ref, x_ref, re_ref, im_ref):
    n1 = re_ref.shape[1]
    y = jnp.dot(f_ref[...], x_ref[0], precision=HI, preferred_element_type=F32)
    re_ref[0] = y[:n1]
    im_ref[0] = y[n1:]


def _hy_stage1(f, x2d):
    bx, k, lanes = x2d.shape
    n1 = f.shape[0] // 2
    tl = min(2048, lanes)
    out = jax.ShapeDtypeStruct((bx, n1, lanes), F32)
    return pl.pallas_call(
        _hy_stage1_kernel,
        grid=(bx, lanes // tl),
        in_specs=[pl.BlockSpec(f.shape, lambda b, j: (0, 0)), pl.BlockSpec((1, k, tl), lambda b, j: (b, 0, j))],
        out_specs=[pl.BlockSpec((1, n1, tl), lambda b, j: (b, 0, j))] * 2,
        out_shape=[out, out],
        compiler_params=_cp("parallel", "parallel"),
        name="hyena_dft_stage1",
    )(f, x2d)


def _hy_spec_kernel(m1_ref, re_ref, im_ref, rs_ref, ore_ref, oim_ref):
    n2 = re_ref.shape[2]
    a = jnp.concatenate([re_ref[0, 0], im_ref[0, 0]], axis=0)
    x = jnp.dot(m1_ref[0], a, precision=HI, preferred_element_type=F32) * rs_ref[...]
    ore_ref[0] = x[:n2]
    oim_ref[0] = x[n2:]


def _hy_filter_spectrum(m1, are, aim, rs):
    _, n1, n2, w = are.shape
    blk = pl.BlockSpec((1, 1, n2, w), lambda c: (0, c, 0, 0))
    oblk = pl.BlockSpec((1, n2, w), lambda c: (c, 0, 0))
    out = jax.ShapeDtypeStruct((n1, n2, w), F32)
    return pl.pallas_call(
        _hy_spec_kernel,
        grid=(n1,),
        in_specs=[pl.BlockSpec((1, 2 * n2, 2 * n2), lambda c: (c, 0, 0)), blk, blk, pl.BlockSpec((1, w), lambda c: (0, 0))],
        out_specs=[oblk, oblk],
        out_shape=[out, out],
        compiler_params=_cp("parallel"),
        name="hyena_filter_spectrum",
    )(m1, are, aim, rs)


def _hy_mid_kernel(m1_ref, m2_ref, re_ref, im_ref, hre_ref, him_ref, ore_ref, oim_ref):
    n2 = re_ref.shape[2]
    a = jnp.concatenate([re_ref[0, 0], im_ref[0, 0]], axis=0)
    x = jnp.dot(m1_ref[0], a, precision=HI, preferred_element_type=F32)
    xre, xim = x[:n2], x[n2:]
    hre, him = hre_ref[0], him_ref[0]
    y = jnp.concatenate([xre * hre - xim * him, xre * him + xim * hre], axis=0)
    bb = jnp.dot(m2_ref[0], y, precision=HI, preferred_element_type=F32)
    ore_ref[0, 0] = bb[:n2]
    oim_ref[0, 0] = bb[n2:]


def _hy_mid(m1, m2, are, aim, hre, him):
    bx, n1, n2, w = are.shape
    blk = pl.BlockSpec((1, 1, n2, w), lambda b, c: (b, c, 0, 0))
    mblk = pl.BlockSpec((1, 2 * n2, 2 * n2), lambda b, c: (c, 0, 0))
    hblk = pl.BlockSpec((1, n2, w), lambda b, c: (c, 0, 0))
    out = jax.ShapeDtypeStruct((bx, n1, n2, w), F32)
    return pl.pallas_call(
        _hy_mid_kernel,
        grid=(bx, n1),
        in_specs=[mblk, mblk, blk, blk, hblk, hblk],
        out_specs=[blk, blk],
        out_shape=[out, out],
        compiler_params=_cp("parallel", "parallel"),
        name="hyena_dft_mid",
    )(m1, m2, are, aim, hre, him)


def _hy_last_kernel(f_ref, re_ref, im_ref, p_ref, x0_ref, bias_ref, o_ref):
    bb = jnp.concatenate([re_ref[0], im_ref[0]], axis=0)
    y = jnp.dot(f_ref[...], bb, precision=HI, preferred_element_type=F32)
    o_ref[0] = x0_ref[0] * (y + p_ref[0] * bias_ref[...])


def _hy_last(f4, bre, bim, p2d, x02d, bias2d):
    bx, n1, lanes = bre.shape
    k = n1 // 2
    tl = min(2048, lanes)
    big = pl.BlockSpec((1, n1, tl), lambda b, j: (b, 0, j))
    small = pl.BlockSpec((1, k, tl), lambda b, j: (b, 0, j))
    return pl.pallas_call(
        _hy_last_kernel,
        grid=(bx, lanes // tl),
        in_specs=[pl.BlockSpec(f4.shape, lambda b, j: (0, 0)), big, big, small, small,
                  pl.BlockSpec((1, tl), lambda b, j: (0, j))],
        out_specs=small,
        out_shape=jax.ShapeDtypeStruct((bx, k, lanes), F32),
        compiler_params=_cp("parallel", "parallel"),
        name="hyena_dft_last",
    )(f4, bre, bim, p2d, x02d, bias2d)


def _hyena(zb, conv_w, conv_b, filt_params, hy_bias, consts):
    bx, lx, _ = zb.shape
    feat, (f1, m1, m2, f4) = consts
    n = 2 * lx
    n1, n2 = _dft_factors(n)
    w = HY_WIDTH
    filt, ssq = _hy_filter(lx, feat, *filt_params)
    rs = lax.rsqrt(ssq + EPS)
    fre, fim = _hy_stage1(f1, filt.reshape(1, n1, n2 * w))
    hre, him = _hy_filter_spectrum(m1, fre.reshape(1, n1, n2, w), fim.reshape(1, n1, n2, w), rs)
    p, x0 = _hy_prep(zb, conv_w, conv_b)
    k = n1 // 2
    are, aim = _hy_stage1(f1[:, :k], p.reshape(bx, k, n2 * w))
    bre, bim = _hy_mid(m1, m2, are.reshape(bx, n1, n2, w), aim.reshape(bx, n1, n2, w), hre, him)
    bias2d = jnp.tile(hy_bias, n2).reshape(1, n2 * w)
    y = _hy_last(f4, bre.reshape(bx, n1, n2 * w), bim.reshape(bx, n1, n2 * w), p.reshape(bx, k, n2 * w),
                 x0.reshape(bx, k, n2 * w), bias2d)
    return y.reshape(bx, lx, w)


def _prep_c_kernel(z_ref, cq, uq, dq, ck, uk, dk, gq_ref, gk_ref, qn_ref, kn_ref, q, k, v):
    qw = WA_HEADS * WA_DIM
    kw = WA_KV_HEADS * WA_DIM
    y = _norm_rope(z_ref[0, :, 0:qw], gq_ref[...], qn_ref[...], cq[...], uq[...], dq[...], WA_DIM, WA_DIM // 4)
    q[0] = (y * WA_DIM ** -0.5).astype(BF16)
    y = _norm_rope(z_ref[0, :, qw:qw + kw], gk_ref[...], kn_ref[...], ck[...], uk[...], dk[...], WA_DIM, WA_DIM // 4)
    k[0] = y.astype(BF16)
    v[0] = z_ref[0, :, qw + kw:].astype(BF16)


def _prep_c(zc, tq_tables, tk_tables, qn, kn):
    bx, lx, _ = zc.shape
    tl = min(256, lx)
    qw = WA_HEADS * WA_DIM
    kw = WA_KV_HEADS * WA_DIM
    row = lambda b, i: (b, i, 0)
    const = lambda b, i: (0, 0)
    tabq = pl.BlockSpec((tl, qw), lambda b, i: (i, 0))
    tabk = pl.BlockSpec((tl, kw), lambda b, i: (i, 0))
    return pl.pallas_call(
        _prep_c_kernel,
        grid=(bx, lx // tl),
        in_specs=[pl.BlockSpec((1, tl, W_C), row), tabq, tabq, tabq, tabk, tabk, tabk,
                  pl.BlockSpec((qw, qw), const), pl.BlockSpec((kw, kw), const),
                  pl.BlockSpec((1, qw), const), pl.BlockSpec((1, kw), const)],
        out_specs=[pl.BlockSpec((1, tl, qw), row), pl.BlockSpec((1, tl, kw), row), pl.BlockSpec((1, tl, kw), row)],
        out_shape=[jax.ShapeDtypeStruct((bx, lx, qw), BF16), jax.ShapeDtypeStruct((bx, lx, kw), BF16),
                   jax.ShapeDtypeStruct((bx, lx, kw), BF16)],
        compiler_params=_cp("parallel", "parallel"),
        name="prep_c",
    )(zc, *tq_tables, *tk_tables, _group_ones(qw, WA_DIM), _group_ones(kw, WA_DIM),
      jnp.tile(qn, WA_HEADS).reshape(1, qw), jnp.tile(kn, WA_KV_HEADS).reshape(1, kw))


def _wattn_kernel(sink_ref, q_ref, kp_ref, kc_ref, kn_ref, kx_ref, vp_ref, vc_ref, vn_ref, vx_ref, o_ref, *, banded):
    h = pl.program_id(1)
    n = pl.program_id(2)
    last = pl.num_programs(2) - 1
    gq = q_ref.shape[2]
    q = q_ref[0, 0].reshape(gq * BLOCK, WA_DIM)
    dn = (((1,), (1,)), ((), ()))
    parts = []
    if banded:
        r = lax.broadcasted_iota(jnp.int32, (gq * BLOCK, BLOCK), 0) % BLOCK
        c = lax.broadcasted_iota(jnp.int32, (gq * BLOCK, BLOCK), 1)
        sp = lax.dot_general(q, kp_ref[0, 0], dn, preferred_element_type=F32)
        parts.append((jnp.where((c >= r) & (n > 0), sp, NEG_INF), vp_ref))
        parts.append((lax.dot_general(q, kc_ref[0, 0], dn, preferred_element_type=F32), vc_ref))
        sn = lax.dot_general(q, kn_ref[0, 0], dn, preferred_element_type=F32)
        parts.append((jnp.where((c <= r) & (n < last), sn, NEG_INF), vn_ref))
    parts.append((lax.dot_general(q, kx_ref[0, 0], dn, preferred_element_type=F32), vx_ref))
    rowh = lax.broadcasted_iota(jnp.int32, (gq * BLOCK, 1), 0) // BLOCK
    snk = jnp.zeros((gq * BLOCK, 1), F32)
    for g in range(gq):
        snk = jnp.where(rowh == g, sink_ref[h * gq + g], snk)
    m = snk
    for s, _ in parts:
        m = jnp.maximum(m, jnp.max(s, axis=-1, keepdims=True))
    den = jnp.exp(snk - m)
    acc = jnp.zeros((gq * BLOCK, WA_DIM), F32)
    for s, v_ref in parts:
        p = jnp.exp(s - m)
        den = den + jnp.sum(p, axis=-1, keepdims=True)
        acc = acc + jnp.dot(p.astype(BF16), v_ref[0, 0], preferred_element_type=F32)
    o_ref[0, 0] = (acc / den).reshape(gq, BLOCK, WA_DIM)


def _window_attention(q, k, v, kx, vx, sink, banded):
    b, kv, g, lq, d = q.shape
    cx = kx.shape[2]
    nb = lq // BLOCK
    blk = lambda f: pl.BlockSpec((1, 1, BLOCK, d), f)
    prev = blk(lambda b_, h, n: (b_, h, jnp.maximum(n - 1, 0), 0))
    cur = blk(lambda b_, h, n: (b_, h, n, 0))
    nxt = blk(lambda b_, h, n: (b_, h, jnp.minimum(n + 1, nb - 1), 0))
    ctx = pl.BlockSpec((1, 1, cx, d), lambda b_, h, n: (b_, h, 0, 0))
    qspec = pl.BlockSpec((1, 1, g, BLOCK, d), lambda b_, h, n: (b_, h, 0, n, 0))
    return pl.pallas_call(
        functools.partial(_wattn_kernel, banded=banded),
        grid=(b, kv, nb),
        in_specs=[pl.BlockSpec(memory_space=pltpu.SMEM), qspec, prev, cur, nxt, ctx, prev, cur, nxt, ctx],
        out_specs=qspec,
        out_shape=jax.ShapeDtypeStruct((b, kv, g, lq, d), F32),
        compiler_params=_cp("parallel", "parallel", "arbitrary"),
        name="window_attention",
    )(sink, q, k, k, k, kx, v, v, v, vx)


def _conf_kernel(zp_ref, zc_ref, zn_ref, w_ref, b_ref, lg_ref, lb_ref, o_ref, ext, *, halo):
    i = pl.program_id(1)
    last = pl.num_programs(1) - 1
    tl = zc_ref.shape[1]
    cw = CF_WIDTH
    glu = lambda z: z[:, :cw] * jax.nn.sigmoid(z[:, cw:])
    ext[0:halo] = jnp.where(i == 0, 0.0, glu(zp_ref[0]))
    ext[halo:halo + tl] = glu(zc_ref[0])
    ext[halo + tl:] = jnp.where(i == last, 0.0, glu(zn_ref[0]))
    w = w_ref[...]
    u = jnp.zeros((tl, cw), F32) + b_ref[...]
    for j in range(CF_TAPS):
        u = u + ext[pl.ds(halo - CF_TAPS // 2 + j, tl), :] * w[j:j + 1]
    uc = u - jnp.mean(u, axis=-1, keepdims=True)
    y = uc * lax.rsqrt(jnp.mean(uc * uc, axis=-1, keepdims=True) + EPS) * lg_ref[...] + lb_ref[...]
    o_ref[0] = _silu(y)


def _conformer(zd, dw_w, dw_b, ln_g, ln_b):
    bx, lx, w = zd.shape
    tl = min(256, lx)
    halo = 2 * SUBLANES
    nh = lx // halo
    per = tl // halo
    row = lambda a: a.reshape(1, -1)
    const = lambda b, i: (0, 0)
    return pl.pallas_call(
        functools.partial(_conf_kernel, halo=halo),
        grid=(bx, lx // tl),
        in_specs=[pl.BlockSpec((1, halo, w), lambda b, i: (b, jnp.maximum(i * per - 1, 0), 0)),
                  pl.BlockSpec((1, tl, w), lambda b, i: (b, i, 0)),
                  pl.BlockSpec((1, halo, w), lambda b, i: (b, jnp.minimum((i + 1) * per, nh - 1), 0)),
                  pl.BlockSpec(dw_w.shape, const)] + [pl.BlockSpec((1, CF_WIDTH), const)] * 3,
        out_specs=pl.BlockSpec((1, tl, CF_WIDTH), lambda b, i: (b, i, 0)),
        out_shape=jax.ShapeDtypeStruct((bx, lx, CF_WIDTH), F32),
        scratch_shapes=[pltpu.VMEM((tl + 2 * halo, CF_WIDTH), F32)],
        compiler_params=_cp("parallel", "parallel"),
        name="conformer_conv",
    )(zd, zd, zd, dw_w, row(dw_b), row(ln_g), row(ln_b))


def _merge_kernel(x_ref, m_ref, ya, yb, yc, yd, zg_ref, bg_ref, wb_ref, wo_ref, o_ref, *, goff):
    d = x_ref.shape[-1]
    acc = jnp.zeros(x_ref.shape[1:], F32)
    for i, y in enumerate((ya, yb, yc, yd)):
        gate = jax.nn.sigmoid(zg_ref[0, :, i * d:(i + 1) * d] + bg_ref[:, i * d:(i + 1) * d])
        acc = acc + gate * jnp.dot(y[0].astype(BF16), wb_ref[i], preferred_element_type=F32)
    out = jnp.dot(acc.astype(BF16), wo_ref[...], preferred_element_type=F32)
    o_ref[0] = x_ref[0] + m_ref[0, :, goff:goff + d] * out


def _merge(x, mod, ys, zg, b_gate, w_branch, w_out):
    bx, lx, d = x.shape
    tm = min(256, lx)
    row = lambda b, i: (b, i, 0)
    ysp = pl.BlockSpec((1, tm, BRANCH_W), row)
    return pl.pallas_call(
        functools.partial(_merge_kernel, goff=2 * d),
        grid=(bx, lx // tm),
        in_specs=[pl.BlockSpec((1, tm, d), row), pl.BlockSpec((1, 1, mod.shape[-1]), lambda b, i: (b, 0, 0)),
                  ysp, ysp, ysp, ysp, pl.BlockSpec((1, tm, N_BRANCH * d), row),
                  pl.BlockSpec((1, N_BRANCH * d), lambda b, i: (0, 0)),
                  pl.BlockSpec(w_branch.shape, lambda b, i: (0, 0, 0)), pl.BlockSpec(w_out.shape, lambda b, i: (0, 0))],
        out_specs=pl.BlockSpec((1, tm, d), row),
        out_shape=jax.ShapeDtypeStruct(x.shape, F32),
        input_output_aliases={0: 0},
        compiler_params=_cp("parallel", "parallel"),
        name="merge",
    )(x, mod, *ys, zg, b_gate.reshape(1, -1), w_branch, w_out)


def _moe_kernel(x_ref, m_ref, g_ref, wr_ref, br_ref, w1_ref, w3_ref, w2_ref, o_ref, h_sc, gate_sc, acc_sc):
    e = pl.program_id(2)
    d = x_ref.shape[-1]
    tm = x_ref.shape[1]
    lane = lax.broadcasted_iota(jnp.int32, (tm, LANES), 1).astype(F32)

    @pl.when(e == 0)
    def _():
        h = _rms(x_ref[0]) * g_ref[...] * (1.0 + m_ref[0, :, 4 * d:5 * d]) + m_ref[0, :, 3 * d:4 * d]
        h_sc[...] = h.astype(BF16)
        lg = jnp.dot(h, wr_ref[...], precision=HI, preferred_element_type=F32) + br_ref[...]
        isg = lane < N_GROUPS
        gmax = jnp.max(jnp.where(isg, lg, NEG_INF), axis=-1, keepdims=True)
        gi = jnp.min(jnp.where(isg & (lg == gmax), lane, LANES), axis=-1, keepdims=True)
        gw = 1.0 / jnp.sum(jnp.where(isg, jnp.exp(lg - gmax), 0.0), axis=-1, keepdims=True)
        lo = N_GROUPS + gi * EXP_PER_GROUP
        ise = (lane >= lo) & (lane < lo + EXP_PER_GROUP)
        le = jnp.where(ise, lg, NEG_INF)
        m1 = jnp.max(le, axis=-1, keepdims=True)
        i1 = jnp.min(jnp.where(ise & (le == m1), lane, LANES), axis=-1, keepdims=True)
        ise2 = ise & (lane != i1)
        le2 = jnp.where(ise2, lg, NEG_INF)
        m2 = jnp.max(le2, axis=-1, keepdims=True)
        i2 = jnp.min(jnp.where(ise2 & (le2 == m2), lane, LANES), axis=-1, keepdims=True)
        r = jnp.exp(m2 - m1)
        wa = gw / (1.0 + r)
        gate_sc[...] = jnp.where(lane == i1, wa, 0.0) + jnp.where(lane == i2, wa * r, 0.0)
        acc_sc[...] = jnp.zeros_like(acc_sc)

    hb = h_sc[...]
    u = _silu(jnp.dot(hb, w1_ref[0], preferred_element_type=F32)) * jnp.dot(hb, w3_ref[0], preferred_element_type=F32)
    ge = jnp.sum(jnp.where(lane == (e + N_GROUPS).astype(F32), gate_sc[...], 0.0), axis=-1, keepdims=True)
    acc_sc[...] += ge * jnp.dot(u.astype(BF16), w2_ref[0], preferred_element_type=F32)

    @pl.when(e == pl.num_programs(2) - 1)
    def _():
        o_ref[0] = x_ref[0] + m_ref[0, :, 5 * d:6 * d] * acc_sc[...]


def _moe(x, mod, g, w_router, b_router, w1, w3, w2):
    bx, lx, d = x.shape
    tm = min(1024, lx)
    ne, _, f = w1.shape
    row = lambda b, i, e: (b, i, 0)
    const = lambda b, i, e: (0, 0)
    return pl.pallas_call(
        _moe_kernel,
        grid=(bx, lx // tm, ne),
        in_specs=[pl.BlockSpec((1, tm, d), row), pl.BlockSpec((1, 1, mod.shape[-1]), lambda b, i, e: (b, 0, 0)),
                  pl.BlockSpec((1, d), const), pl.BlockSpec((d, LANES), const), pl.BlockSpec((1, LANES), const),
                  pl.BlockSpec((1, d, f), lambda b, i, e: (e, 0, 0)), pl.BlockSpec((1, d, f), lambda b, i, e: (e, 0, 0)),
                  pl.BlockSpec((1, f, d), lambda b, i, e: (e, 0, 0))],
        out_specs=pl.BlockSpec((1, tm, d), row),
        out_shape=jax.ShapeDtypeStruct(x.shape, F32),
        scratch_shapes=[pltpu.VMEM((tm, d), BF16), pltpu.VMEM((tm, LANES), F32), pltpu.VMEM((tm, d), F32)],
        input_output_aliases={0: 0},
        compiler_params=_cp("parallel", "parallel", "arbitrary"),
        name="moe",
    )(x, mod, g.reshape(1, d), w_router, b_router, w1, w3, w2)


def kernel(x, c, ctx, c_ctx, w_mod, b_mod, norm1_g, norm2_g, w_in, b_gate, da_qn, da_kn, da_lam, da_subln, hy_conv_w, hy_conv_b, hf_w1, hf_b1, hf_w2, hf_b2, hf_w3, hf_b3, hf_w4, hf_freq, hy_bias, wa_qn, wa_kn, wa_sink, cf_dw_w, cf_dw_b, cf_ln_g, cf_ln_b, w_branch, w_out, w_rg, b_rg, w_re, b_re, w1, w3, w2):
    b, s, d = x.shape
    cl = ctx.shape[1]
    depth = w_mod.shape[0]
    assert s % 256 == 0 and cl % 256 == 0 and s % GRID_W == 0

    nrow = -(-(b + 1) // SUBLANES) * SUBLANES
    crows = jnp.zeros((nrow, d), F32).at[:b].set(c).at[b].set(c_ctx)
    mods = _mod_vectors(crows, w_mod, b_mod)

    rope_a = _rope_tables(s, DA_DIM, DA_HEADS)
    unit_a = _unit_tables(cl, DA_HEADS * DA_DIM)
    rope_cq = _rope_tables(s, WA_DIM, WA_HEADS)
    rope_ck = _rope_tables(s, WA_DIM, WA_KV_HEADS)
    unit_cq = _unit_tables(cl, WA_HEADS * WA_DIM)
    unit_ck = _unit_tables(cl, WA_KV_HEADS * WA_DIM)
    hy_lat = (_hy_feat(s), _dft_tables(2 * s))
    hy_ctx = (_hy_feat(cl), _dft_tables(2 * cl))

    xc = ctx
    for l in range(depth):
        last = l == depth - 1
        lam_init = 0.8 - 0.6 * math.exp(-0.3 * l)
        mod_x = mods[l, :b][:, None, :]
        mod_c = jnp.broadcast_to(mods[l, b][None, None, :], (b, 1, mods.shape[-1]))
        wl = w_in[l]
        ws = [wl[:, 0:OFF_B].astype(BF16), wl[:, OFF_B:OFF_C].astype(BF16), wl[:, OFF_C:OFF_D].astype(BF16),
              wl[:, OFF_D:OFF_G].astype(BF16), wl[:, OFF_G:].astype(BF16)]
        za, zb, zc_, zd, zg = _in_proj(x, mod_x, norm1_g[l], ws)
        ca, cb, cc, cd, cg = _in_proj(xc, mod_c, norm1_g[l], ws)
        filt_params = (hf_w1[l], hf_b1[l], hf_w2[l], hf_b2[l], hf_w3[l], hf_b3[l], hf_w4[l], hf_freq[l])
        wbr = w_branch[l].astype(BF16)
        wo = w_out[l].astype(BF16)

        q1, q2, k1, k2, v = _prep_a(za, rope_a, da_qn[l], da_kn[l])
        q1c, q2c, k1c, k2c, vc = _prep_a(ca, unit_a, da_qn[l], da_kn[l])
        sh = lambda a, dd: _split_heads(a, DA_HEADS, dd)
        cat = lambda a, a_c, dd: jnp.concatenate([sh(a, dd), sh(a_c, dd)], axis=2)
        ya = _merge_heads(_diff_attention(sh(q1, DA_DIM), sh(q2, DA_DIM), cat(k1, k1c, DA_DIM), cat(k2, k2c, DA_DIM),
                                          cat(v, vc, DA_VDIM), da_lam[l], da_subln[l], lam_init))
        yb = _hyena(zb, hy_conv_w[l], hy_conv_b[l], filt_params, hy_bias[l], hy_lat)
        qw, kw, vw = _prep_c(zc_, rope_cq, rope_ck, wa_qn[l], wa_kn[l])
        qwc, kwc, vwc = _prep_c(cc, unit_cq, unit_ck, wa_qn[l], wa_kn[l])
        hq = lambda a: _split_heads(a, WA_HEADS, WA_DIM).reshape(a.shape[0], WA_KV_HEADS, WA_GROUP, a.shape[1], WA_DIM)
        hk = lambda a: _split_heads(a, WA_KV_HEADS, WA_DIM)
        unq = lambda o: _merge_heads(o.reshape(o.shape[0], WA_HEADS, o.shape[3], WA_DIM))
        yc_ = unq(_window_attention(hq(qw), hk(kw), hk(vw), hk(kwc), hk(vwc), wa_sink[l], True))
        yd = _conformer(zd, cf_dw_w[l], cf_dw_b[l], cf_ln_g[l], cf_ln_b[l])
        x = _merge(x, mod_x, (ya, yb, yc_, yd), zg, b_gate[l], wbr, wo)

        w_router = jnp.pad(jnp.concatenate([w_rg[l], w_re[l]], axis=1), ((0, 0), (0, LANES - N_GROUPS - N_EXPERTS)))
        b_router = jnp.pad(jnp.concatenate([b_rg[l], b_re[l]]), (0, LANES - N_GROUPS - N_EXPERTS)).reshape(1, LANES)
        ew = (w1[l].astype(BF16), w3[l].astype(BF16), w2[l].astype(BF16))

        if not last:
            yca = _merge_heads(_diff_attention(sh(q1c, DA_DIM), sh(q2c, DA_DIM), sh(k1c, DA_DIM), sh(k2c, DA_DIM),
                                               sh(vc, DA_VDIM), da_lam[l], da_subln[l], lam_init))
            ycb = _hyena(cb, hy_conv_w[l], hy_conv_b[l], filt_params, hy_bias[l], hy_ctx)
            ycc = unq(_window_attention(hq(qwc), hk(kwc), hk(vwc), hk(kwc), hk(vwc), wa_sink[l], False))
            ycd = _conformer(cd, cf_dw_w[l], cf_dw_b[l], cf_ln_g[l], cf_ln_b[l])
            xc = _merge(xc, mod_c, (yca, ycb, ycc, ycd), cg, b_gate[l], wbr, wo)
            xc = _moe(xc.reshape(1, b * cl, d), mod_c[:1], norm2_g[l], w_router, b_router, *ew).reshape(b, cl, d)
        x = _moe(x, mod_x, norm2_g[l], w_router, b_router, *ew)
    return x
```

```python
import functools
import math

import jax
import jax.numpy as jnp
from jax import lax
from jax.experimental import pallas as pl
from jax.experimental.pallas import tpu as pltpu

F32 = jnp.float32
BF16 = jnp.bfloat16
HI = lax.Precision.HIGHEST

GRID_W = 64
BLOCK = 128
ROPE_BASE = 10000.0
EPS = 1e-6
NEG_INF = -1e30

DA_HEADS = 4
DA_DIM = 32
DA_VDIM = 64
HY_WIDTH = 256
HY_BANDS = 16
HY_FF = 64
HY_SHIFT = 0.05
HY_FAST_DECAY = 0.3
HY_SLOW_DECAY = 1.5
HY_TARGET = 1e-2
WA_HEADS = 4
WA_KV_HEADS = 2
WA_GROUP = 2
WA_DIM = 64
CF_WIDTH = 256
CF_TAPS = 31
N_BRANCH = 4
BRANCH_W = 256
N_GROUPS = 4
EXP_PER_GROUP = 4
N_EXPERTS = 16

W_A = 4 * DA_HEADS * DA_DIM + DA_HEADS * DA_VDIM
W_B = 3 * HY_WIDTH
W_C = (WA_HEADS + 2 * WA_KV_HEADS) * WA_DIM
W_D = 2 * CF_WIDTH
OFF_B = W_A
OFF_C = OFF_B + W_B
OFF_D = OFF_C + W_C
OFF_G = OFF_D + W_D

LOG2E = math.log2(math.e)
LANES = 128
SUBLANES = 8
VMEM_LIMIT = 56 * 1024 * 1024


def _cp(*sem):
    return pltpu.CompilerParams(dimension_semantics=sem, vmem_limit_bytes=VMEM_LIMIT)


def _rms(xf):
    return xf * lax.rsqrt(jnp.mean(xf * xf, axis=-1, keepdims=True) + EPS)


def _silu(x):
    return x * jax.nn.sigmoid(x)


def _mod_kernel(c_ref, w_ref, b_ref, o_ref):
    s = _silu(c_ref[...])
    o_ref[0] = jnp.dot(s, w_ref[0], precision=HI, preferred_element_type=F32) + b_ref[0]


def _mod_vectors(crows, w_mod, b_mod):
    depth, d, n = w_mod.shape
    r = crows.shape[0]
    tn = 1536
    return pl.pallas_call(
        _mod_kernel,
        grid=(depth, n // tn),
        in_specs=[pl.BlockSpec((r, d), lambda l, j: (0, 0)),
                  pl.BlockSpec((1, d, tn), lambda l, j: (l, 0, j)),
                  pl.BlockSpec((1, 1, tn), lambda l, j: (l, 0, j))],
        out_specs=pl.BlockSpec((1, r, tn), lambda l, j: (l, 0, j)),
        out_shape=jax.ShapeDtypeStruct((depth, r, n), F32),
        compiler_params=_cp("arbitrary", "arbitrary"),
        name="mod_vectors",
    )(crows, w_mod, b_mod.reshape(depth, 1, n))


def _inproj_kernel(x_ref, m_ref, g_ref, wa, wb, wc, wd, wg, oa, ob, oc, od, og):
    d = x_ref.shape[-1]
    x = x_ref[0]
    shift = m_ref[0, :, 0:d]
    scale = m_ref[0, :, d:2 * d]
    h = (_rms(x) * g_ref[...] * (1.0 + scale) + shift).astype(BF16)
    for w, o in ((wa, oa), (wb, ob), (wc, oc), (wd, od)):
        o[0] = jnp.dot(h, w[...], preferred_element_type=F32)
    for k in range(N_BRANCH):
        og[0, :, k * d:(k + 1) * d] = jnp.dot(h, wg[:, k * d:(k + 1) * d], preferred_element_type=F32)


def _in_proj(x, mod, g, ws):
    bx, lx, d = x.shape
    tm = min(256, lx)
    widths = [w.shape[1] for w in ws]
    const = lambda b, i: (0, 0)
    return pl.pallas_call(
        _inproj_kernel,
        grid=(bx, lx // tm),
        in_specs=[pl.BlockSpec((1, tm, d), lambda b, i: (b, i, 0)),
                  pl.BlockSpec((1, 1, mod.shape[-1]), lambda b, i: (b, 0, 0)),
                  pl.BlockSpec((1, d), const)]
                 + [pl.BlockSpec((d, w), const) for w in widths],
        out_specs=[pl.BlockSpec((1, tm, w), lambda b, i: (b, i, 0)) for w in widths],
        out_shape=[jax.ShapeDtypeStruct((bx, lx, w), F32) for w in widths],
        compiler_params=_cp("parallel", "parallel"),
        name="in_proj",
    )(x, mod, g.reshape(1, d), *ws)


def _rope_tables(s, d, reps):
    rows = s // GRID_W
    row = jnp.repeat(jnp.arange(rows, dtype=F32), GRID_W)
    col = jnp.tile(jnp.arange(GRID_W, dtype=F32), rows)
    qd = d // 4
    inv = ROPE_BASE ** (-jnp.arange(qd, dtype=F32) / qd)
    ar = row[:, None] * inv[None, :]
    ac = col[:, None] * inv[None, :]
    z = jnp.zeros_like(ar)
    cos = jnp.concatenate([jnp.cos(ar), jnp.cos(ar), jnp.cos(ac), jnp.cos(ac)], axis=-1)
    sin_up = jnp.concatenate([-jnp.sin(ar), z, -jnp.sin(ac), z], axis=-1)
    sin_dn = jnp.concatenate([z, jnp.sin(ar), z, jnp.sin(ac)], axis=-1)
    t = lambda a: jnp.tile(a, (1, reps))
    return t(cos), t(sin_up), t(sin_dn)


def _unit_tables(s, w):
    return jnp.ones((s, w), F32), jnp.zeros((s, w), F32), jnp.zeros((s, w), F32)


def _group_ones(width, group):
    i = jnp.arange(width) // group
    return (i[:, None] == i[None, :]).astype(F32)


def _norm_rope(x, gmat, gain, cos, sup, sdn, group, qd):
    w = x.shape[-1]
    ss = jnp.dot(x * x, gmat, precision=HI, preferred_element_type=F32) * (1.0 / group)
    xn = x * lax.rsqrt(ss + EPS) * gain
    return xn * cos + pltpu.roll(xn, w - qd, 1) * sup + pltpu.roll(xn, qd, 1) * sdn


def _prep_a_kernel(z_ref, cos_ref, sup_ref, sdn_ref, gm_ref, qn_ref, kn_ref, q1, q2, k1, k2, v):
    hw = DA_HEADS * DA_DIM
    cos, sup, sdn, gm = cos_ref[...], sup_ref[...], sdn_ref[...], gm_ref[...]
    scale = DA_DIM ** -0.5 * LOG2E
    for t, (o, gain, sc) in enumerate(((q1, qn_ref, scale), (q2, qn_ref, scale), (k1, kn_ref, 1.0), (k2, kn_ref, 1.0))):
        x = z_ref[0, :, t * hw:(t + 1) * hw]
        y = _norm_rope(x, gm, gain[...], cos, sup, sdn, DA_DIM, DA_DIM // 4)
        o[0] = (y * sc).astype(BF16)
    v[0] = z_ref[0, :, 4 * hw:].astype(BF16)


def _prep_a(za, tables, qn, kn):
    bx, lx, _ = za.shape
    tl = min(256, lx)
    hw = DA_HEADS * DA_DIM
    gm = _group_ones(hw, DA_DIM)
    row = lambda b, i: (b, i, 0)
    tab = pl.BlockSpec((tl, hw), lambda b, i: (i, 0))
    const = lambda b, i: (0, 0)
    outs = [jax.ShapeDtypeStruct((bx, lx, hw), BF16)] * 4 + [jax.ShapeDtypeStruct((bx, lx, DA_HEADS * DA_VDIM), BF16)]
    return pl.pallas_call(
        _prep_a_kernel,
        grid=(bx, lx // tl),
        in_specs=[pl.BlockSpec((1, tl, W_A), row), tab, tab, tab,
                  pl.BlockSpec((hw, hw), const), pl.BlockSpec((1, hw), const), pl.BlockSpec((1, hw), const)],
        out_specs=[pl.BlockSpec((1, tl, hw), row)] * 4 + [pl.BlockSpec((1, tl, DA_HEADS * DA_VDIM), row)],
        out_shape=outs,
        compiler_params=_cp("parallel", "parallel"),
        name="prep_a",
    )(za, *tables, gm, jnp.tile(qn, DA_HEADS).reshape(1, hw), jnp.tile(kn, DA_HEADS).reshape(1, hw))


DA_AUG = 2 * DA_DIM
DA_VEXT = DA_VDIM + 16
DA_SHIFT_MAX = 50.0


def _da_lambda(lam_ref, lam_init):
    lv = lam_ref[...]
    return (jnp.exp(jnp.sum(lv[0:1] * lv[1:2], keepdims=True)) - jnp.exp(jnp.sum(lv[2:3] * lv[3:4], keepdims=True))
            + lam_init)


def _dattn_fixed_kernel(lam_ref, sg_ref, q1_ref, q2_ref, k1_ref, k2_ref, v_ref, o_ref, acc1, acc2, *, lam_init):
    nk = k1_ref.shape[2]
    dn = (((1,), (1,)), ((), ()))
    q1 = q1_ref[0, 0]
    q2 = q2_ref[0, 0]
    acc1[...] = jnp.zeros_like(acc1)
    acc2[...] = jnp.zeros_like(acc2)

    def body(j, carry):
        vc = v_ref[0, 0, j]
        p1 = jnp.exp2(lax.dot_general(q1, k1_ref[0, 0, j], dn, preferred_element_type=F32)).astype(BF16)
        acc1[...] += jnp.dot(p1, vc, preferred_element_type=F32)
        p2 = jnp.exp2(lax.dot_general(q2, k2_ref[0, 0, j], dn, preferred_element_type=F32)).astype(BF16)
        acc2[...] += jnp.dot(p2, vc, preferred_element_type=F32)
        return carry

    lax.fori_loop(0, nk, body, 0)
    dv = DA_VDIM
    a1 = acc1[...]
    a2 = acc2[...]
    lam = _da_lambda(lam_ref, lam_init)
    o = a1[:, :dv] * (1.0 / a1[:, dv:dv + 1]) - a2[:, :dv] * (lam / a2[:, dv:dv + 1])
    o_ref[0, 0] = _rms(o) * (sg_ref[...] * (1.0 - lam_init))


def _dattn_online_kernel(lam_ref, sg_ref, q1_ref, q2_ref, k1_ref, k2_ref, vt_ref, o_ref, *, lam_init):
    q1 = q1_ref[0, 0]
    q2 = q2_ref[0, 0]
    tq = q1.shape[0]
    nk = k1_ref.shape[2]
    dn = (((1,), (1,)), ((), ()))

    def update(q, kc, vtc, m, a):
        s = lax.dot_general(kc, q, dn, preferred_element_type=F32)
        mn = jnp.maximum(m, jnp.max(s, axis=0, keepdims=True))
        p = jnp.exp2(s - mn).astype(BF16)
        return mn, jnp.exp2(m - mn) * a + jnp.dot(vtc, p, preferred_element_type=F32)

    def body(j, c):
        m1, a1, m2, a2 = c
        vtc = vt_ref[0, 0, j]
        m1, a1 = update(q1, k1_ref[0, 0, j], vtc, m1, a1)
        m2, a2 = update(q2, k2_ref[0, 0, j], vtc, m2, a2)
        return m1, a1, m2, a2

    m0 = jnp.full((1, tq), NEG_INF, F32)
    a0 = jnp.zeros((DA_VEXT, tq), F32)
    _, a1, _, a2 = lax.fori_loop(0, nk, body, (m0, a0, m0, a0))
    dv = DA_VDIM
    lam = _da_lambda(lam_ref, lam_init)
    o = a1[:dv] * (1.0 / a1[dv:dv + 1]) - a2[:dv] * (lam / a2[dv:dv + 1])
    o = o * lax.rsqrt(jnp.mean(o * o, axis=0, keepdims=True) + EPS)
    o_ref[0, 0] = o * (sg_ref[...] * (1.0 - lam_init))


def _diff_attention(q1, q2, k1, k2, v, qn, kn, lam_p, subln, lam_init):
    b, lq, _ = q1.shape
    lk = k1.shape[1]
    h, dd, dv = DA_HEADS, DA_DIM, DA_VDIM
    tk = min(256, lk)
    nk = lk // tk
    shift = 1.02 * LOG2E * DA_DIM ** 0.5 * jnp.max(jnp.abs(qn)) * jnp.max(jnp.abs(kn))
    fixed = shift <= DA_SHIFT_MAX
    pad = lambda a, col: jnp.concatenate([a, col, jnp.zeros(a.shape[:-1] + (DA_AUG - dd - 1,), BF16)], axis=-1)
    qh = pad(jnp.stack([q1, q2]).reshape(2, b, lq, h, dd), jnp.ones((2, b, lq, h, 1), BF16)).transpose(0, 1, 3, 2, 4)
    kcol = jnp.broadcast_to(jnp.where(fixed, -shift, 0.0), (2, b, nk, tk, h, 1)).astype(BF16)
    kh = pad(jnp.stack([k1, k2]).reshape(2, b, nk, tk, h, dd), kcol).transpose(0, 1, 4, 2, 3, 5)
    vx = jnp.concatenate([v.reshape(b, nk, tk, h, dv), jnp.ones((b, nk, tk, h, DA_VEXT - dv), BF16)], axis=-1)
    const = lambda b_, h_, i: (0, 0)
    ks = pl.BlockSpec((1, 1, nk, tk, DA_AUG), lambda b_, h_, i: (b_, h_, 0, 0, 0))

    def fixed_call(qh, kh, vx):
        tq = min(2048, lq)
        qs = pl.BlockSpec((1, 1, tq, DA_AUG), lambda b_, h_, i: (b_, h_, i, 0))
        out = pl.pallas_call(
            functools.partial(_dattn_fixed_kernel, lam_init=lam_init),
            grid=(b, h, lq // tq),
            in_specs=[pl.BlockSpec(lam_p.shape, const), pl.BlockSpec((1, dv), const), qs, qs, ks, ks,
                      pl.BlockSpec((1, 1, nk, tk, DA_VEXT), lambda b_, h_, i: (b_, h_, 0, 0, 0))],
            out_specs=pl.BlockSpec((1, 1, tq, dv), lambda b_, h_, i: (b_, h_, i, 0)),
            out_shape=jax.ShapeDtypeStruct((b, h, lq, dv), F32),
            scratch_shapes=[pltpu.VMEM((tq, DA_VEXT), F32)] * 2,
            compiler_params=_cp("parallel", "parallel", "arbitrary"),
            name="diff_attention",
        )(lam_p, subln.reshape(1, dv), qh[0], qh[1], kh[0], kh[1], vx.transpose(0, 3, 1, 2, 4))
        return out.transpose(0, 2, 1, 3).reshape(b, lq, h * dv)

    def online_call(qh, kh, vx):
        tq = min(256, lq)
        qs = pl.BlockSpec((1, 1, tq, DA_AUG), lambda b_, h_, i: (b_, h_, i, 0))
        out = pl.pallas_call(
            functools.partial(_dattn_online_kernel, lam_init=lam_init),
            grid=(b, h, lq // tq),
            in_specs=[pl.BlockSpec(lam_p.shape, const), pl.BlockSpec((dv, 1), const), qs, qs, ks, ks,
                      pl.BlockSpec((1, 1, nk, DA_VEXT, tk), lambda b_, h_, i: (b_, h_, 0, 0, 0))],
            out_specs=pl.BlockSpec((1, 1, dv, tq), lambda b_, h_, i: (b_, h_, 0, i)),
            out_shape=jax.ShapeDtypeStruct((b, h, dv, lq), F32),
            compiler_params=_cp("parallel", "parallel", "arbitrary"),
            name="diff_attention_online",
        )(lam_p, subln.reshape(dv, 1), qh[0], qh[1], kh[0], kh[1], vx.transpose(0, 3, 1, 4, 2))
        return out.transpose(0, 3, 1, 2).reshape(b, lq, h * dv)

    return lax.cond(fixed, fixed_call, online_call, qh, kh, vx)


def _split_heads(a, n, d):
    b, l, _ = a.shape
    return a.reshape(b, l, n, d).transpose(0, 2, 1, 3)


def _merge_heads(a):
    b, n, l, d = a.shape
    return a.transpose(0, 2, 1, 3).reshape(b, l, n * d)


def _hy_prep_kernel(zp_ref, zc_ref, zn_ref, w_ref, b_ref, p_ref, x0_ref, ext):
    i = pl.program_id(1)
    last = pl.num_programs(1) - 1
    tl = zc_ref.shape[1]
    h = SUBLANES
    ext[0:h] = jnp.where(i == 0, 0.0, zp_ref[0])
    ext[h:h + tl] = zc_ref[0]
    ext[h + tl:] = jnp.where(i == last, 0.0, zn_ref[0])
    w = w_ref[...]
    u = (ext[pl.ds(h - 1, tl), :] * w[0:1] + ext[pl.ds(h, tl), :] * w[1:2] + ext[pl.ds(h + 1, tl), :] * w[2:3]
         + b_ref[...])
    hw = HY_WIDTH
    x0_ref[0] = u[:, 0:hw]
    p_ref[0] = u[:, 2 * hw:3 * hw] * u[:, hw:2 * hw]


def _hy_prep(zb, conv_w, conv_b):
    bx, lx, w = zb.shape
    tl = min(256, lx)
    h = SUBLANES
    nh = lx // h
    per = tl // h
    out = jax.ShapeDtypeStruct((bx, lx, HY_WIDTH), F32)
    return pl.pallas_call(
        _hy_prep_kernel,
        grid=(bx, lx // tl),
        in_specs=[pl.BlockSpec((1, h, w), lambda b, i: (b, jnp.maximum(i * per - 1, 0), 0)),
                  pl.BlockSpec((1, tl, w), lambda b, i: (b, i, 0)),
                  pl.BlockSpec((1, h, w), lambda b, i: (b, jnp.minimum((i + 1) * per, nh - 1), 0)),
                  pl.BlockSpec(conv_w.shape, lambda b, i: (0, 0)),
                  pl.BlockSpec((1, w), lambda b, i: (0, 0))],
        out_specs=[pl.BlockSpec((1, tl, HY_WIDTH), lambda b, i: (b, i, 0))] * 2,
        out_shape=[out, out],
        scratch_shapes=[pltpu.VMEM((tl + 2 * h, w), F32)],
        compiler_params=_cp("parallel", "parallel"),
        name="hyena_prep",
    )(zb, zb, zb, conv_w, conv_b.reshape(1, w))


def _hy_filter_kernel(feat_ref, w1, b1, w2, b2, w3, b3, w4, fr_ref, dl_ref, filt_ref, ssq_ref, *, s):
    i = pl.program_id(0)
    tr = feat_ref.shape[0]
    feat = feat_ref[...]
    fr = fr_ref[...]
    dot = lambda a, w: jnp.dot(a, w[...], precision=HI, preferred_element_type=F32)
    a = jnp.sin(fr * (dot(feat, w1) + b1[...]))
    a = jnp.sin(fr * (dot(a, w2) + b2[...]))
    a = jnp.sin(fr * (dot(a, w3) + b3[...]))
    coef = dot(a, w4)
    n = i * tr + lax.broadcasted_iota(jnp.int32, (tr, 1), 0)
    window = jnp.exp(-feat[:, 0:1] * dl_ref[...]) + HY_SHIFT
    half = jnp.where(n < s, coef[:, :HY_WIDTH], coef[:, HY_WIDTH:])
    filt = jnp.where(n == s, 0.0, half * window)
    filt_ref[...] = filt

    @pl.when(i == 0)
    def _():
        ssq_ref[...] = jnp.zeros_like(ssq_ref)

    ssq_ref[...] += jnp.sum(filt * filt, axis=0, keepdims=True)


def _hy_feat(s):
    t = jnp.linspace(0.0, 1.0, s, dtype=F32)[:, None]
    w = (2.0 * math.pi / s) * jnp.arange(s, dtype=F32)[:, None]
    bands = jnp.linspace(1e-4, HY_BANDS - 1, HY_BANDS, dtype=F32)[None, :]
    feat = jnp.concatenate([t, jnp.cos(w * bands), jnp.sin(w * bands)], axis=-1)
    feat = jnp.concatenate([feat, feat[:1], feat[:0:-1]], axis=0)
    return jnp.pad(feat, ((0, 0), (0, LANES - feat.shape[1])))


def _hy_filter(s, feat, w1, b1, w2, b2, w3, b3, w4, freq):
    n = 2 * s
    tr = min(512, n)
    deltas = jnp.abs(jnp.linspace(math.log(HY_TARGET) / HY_FAST_DECAY, math.log(HY_TARGET) / HY_SLOW_DECAY,
                                  HY_WIDTH, dtype=F32)).reshape(1, HY_WIDTH)
    w1p = jnp.pad(w1, ((0, LANES - w1.shape[0]), (0, 0)))
    row = lambda a: a.reshape(1, -1)
    args = (feat, w1p, row(b1), w2, row(b2), w3, row(b3), w4, row(freq), deltas)
    const = lambda i: (0, 0)
    return pl.pallas_call(
        functools.partial(_hy_filter_kernel, s=s),
        grid=(n // tr,),
        in_specs=[pl.BlockSpec((tr, LANES), lambda i: (i, 0))] + [pl.BlockSpec(a.shape, const) for a in args[1:]],
        out_specs=[pl.BlockSpec((tr, HY_WIDTH), lambda i: (i, 0)), pl.BlockSpec((1, HY_WIDTH), const)],
        out_shape=[jax.ShapeDtypeStruct((n, HY_WIDTH), F32), jax.ShapeDtypeStruct((1, HY_WIDTH), F32)],
        compiler_params=_cp("arbitrary"),
        name="hyena_filter",
    )(*args)


def _dft_factors(n):
    lg = n.bit_length() - 1
    assert 1 << lg == n
    n1 = 1 << (lg // 2)
    return n1, n // n1


def _dft_tables(n):
    n1, n2 = _dft_factors(n)
    ia = jnp.arange(n1, dtype=jnp.int32)
    ang1 = (2.0 * math.pi / n1) * ((ia[:, None] * ia[None, :]) % n1).astype(F32)
    f1 = jnp.concatenate([jnp.cos(ang1), -jnp.sin(ang1)], axis=0)
    c = jnp.arange(n1, dtype=jnp.int32)[:, None, None]
    d = jnp.arange(n2, dtype=jnp.int32)[None, :, None]
    b = jnp.arange(n2, dtype=jnp.int32)[None, None, :]
    ang = (2.0 * math.pi / n) * ((b * (c + n1 * d)) % n).astype(F32)
    re, im = jnp.cos(ang), -jnp.sin(ang)
    m1 = jnp.concatenate([jnp.concatenate([re, -im], axis=2), jnp.concatenate([im, re], axis=2)], axis=1)
    m2 = jnp.swapaxes(m1, 1, 2)
    ang4 = ang1[: n1 // 2]
    f4 = jnp.concatenate([jnp.cos(ang4), -jnp.sin(ang4)], axis=1) * (1.0 / n)
    return f1, m1, m2, f4


def _hy_stage1_kernel(f_ref, x_ref, re_ref, im_ref):
    n1 = re_ref.shape[1]
    y = jnp.dot(f_ref[...], x_ref[0], precision=HI, preferred_element_type=F32)
    re_ref[0] = y[:n1]
    im_ref[0] = y[n1:]


def _hy_stage1(f, x2d):
    bx, k, lanes = x2d.shape
    n1 = f.shape[0] // 2
    tl = min(2048, lanes)
    out = jax.ShapeDtypeStruct((bx, n1, lanes), F32)
    return pl.pallas_call(
        _hy_stage1_kernel,
        grid=(bx, lanes // tl),
        in_specs=[pl.BlockSpec(f.shape, lambda b, j: (0, 0)), pl.BlockSpec((1, k, tl), lambda b, j: (b, 0, j))],
        out_specs=[pl.BlockSpec((1, n1, tl), lambda b, j: (b, 0, j))] * 2,
        out_shape=[out, out],
        compiler_params=_cp("parallel", "parallel"),
        name="hyena_dft_stage1",
    )(f, x2d)


def _hy_spec_kernel(m1_ref, re_ref, im_ref, rs_ref, ore_ref, oim_ref):
    n2 = re_ref.shape[2]
    a = jnp.concatenate([re_ref[0, 0], im_ref[0, 0]], axis=0)
    x = jnp.dot(m1_ref[0], a, precision=HI, preferred_element_type=F32) * rs_ref[...]
    ore_ref[0] = x[:n2]
    oim_ref[0] = x[n2:]


def _hy_filter_spectrum(m1, are, aim, rs):
    _, n1, n2, w = are.shape
    blk = pl.BlockSpec((1, 1, n2, w), lambda c: (0, c, 0, 0))
    oblk = pl.BlockSpec((1, n2, w), lambda c: (c, 0, 0))
    out = jax.ShapeDtypeStruct((n1, n2, w), F32)
    return pl.pallas_call(
        _hy_spec_kernel,
        grid=(n1,),
        in_specs=[pl.BlockSpec((1, 2 * n2, 2 * n2), lambda c: (c, 0, 0)), blk, blk, pl.BlockSpec((1, w), lambda c: (0, 0))],
        out_specs=[oblk, oblk],
        out_shape=[out, out],
        compiler_params=_cp("parallel"),
        name="hyena_filter_spectrum",
    )(m1, are, aim, rs)


def _hy_mid_kernel(m1_ref, m2_ref, re_ref, im_ref, hre_ref, him_ref, ore_ref, oim_ref):
    n2 = re_ref.shape[2]
    a = jnp.concatenate([re_ref[0, 0], im_ref[0, 0]], axis=0)
    x = jnp.dot(m1_ref[0], a, precision=HI, preferred_element_type=F32)
    xre, xim = x[:n2], x[n2:]
    hre, him = hre_ref[0], him_ref[0]
    y = jnp.concatenate([xre * hre - xim * him, xre * him + xim * hre], axis=0)
    bb = jnp.dot(m2_ref[0], y, precision=HI, preferred_element_type=F32)
    ore_ref[0, 0] = bb[:n2]
    oim_ref[0, 0] = bb[n2:]


def _hy_mid(m1, m2, are, aim, hre, him):
    bx, n1, n2, w = are.shape
    blk = pl.BlockSpec((1, 1, n2, w), lambda b, c: (b, c, 0, 0))
    mblk = pl.BlockSpec((1, 2 * n2, 2 * n2), lambda b, c: (c, 0, 0))
    hblk = pl.BlockSpec((1, n2, w), lambda b, c: (c, 0, 0))
    out = jax.ShapeDtypeStruct((bx, n1, n2, w), F32)
    return pl.pallas_call(
        _hy_mid_kernel,
        grid=(bx, n1),
        in_specs=[mblk, mblk, blk, blk, hblk, hblk],
        out_specs=[blk, blk],
        out_shape=[out, out],
        compiler_params=_cp("parallel", "parallel"),
        name="hyena_dft_mid",
    )(m1, m2, are, aim, hre, him)


def _hy_last_kernel(f_ref, re_ref, im_ref, p_ref, x0_ref, bias_ref, o_ref):
    bb = jnp.concatenate([re_ref[0], im_ref[0]], axis=0)
    y = jnp.dot(f_ref[...], bb, precision=HI, preferred_element_type=F32)
    o_ref[0] = x0_ref[0] * (y + p_ref[0] * bias_ref[...])


def _hy_last(f4, bre, bim, p2d, x02d, bias2d):
    bx, n1, lanes = bre.shape
    k = n1 // 2
    tl = min(2048, lanes)
    big = pl.BlockSpec((1, n1, tl), lambda b, j: (b, 0, j))
    small = pl.BlockSpec((1, k, tl), lambda b, j: (b, 0, j))
    return pl.pallas_call(
        _hy_last_kernel,
        grid=(bx, lanes // tl),
        in_specs=[pl.BlockSpec(f4.shape, lambda b, j: (0, 0)), big, big, small, small,
                  pl.BlockSpec((1, tl), lambda b, j: (0, j))],
        out_specs=small,
        out_shape=jax.ShapeDtypeStruct((bx, k, lanes), F32),
        compiler_params=_cp("parallel", "parallel"),
        name="hyena_dft_last",
    )(f4, bre, bim, p2d, x02d, bias2d)


def _hyena(zb, conv_w, conv_b, filt_params, hy_bias, consts):
    bx, lx, _ = zb.shape
    feat, (f1, m1, m2, f4) = consts
    n = 2 * lx
    n1, n2 = _dft_factors(n)
    w = HY_WIDTH
    filt, ssq = _hy_filter(lx, feat, *filt_params)
    rs = lax.rsqrt(ssq + EPS)
    fre, fim = _hy_stage1(f1, filt.reshape(1, n1, n2 * w))
    hre, him = _hy_filter_spectrum(m1, fre.reshape(1, n1, n2, w), fim.reshape(1, n1, n2, w), rs)
    p, x0 = _hy_prep(zb, conv_w, conv_b)
    k = n1 // 2
    are, aim = _hy_stage1(f1[:, :k], p.reshape(bx, k, n2 * w))
    bre, bim = _hy_mid(m1, m2, are.reshape(bx, n1, n2, w), aim.reshape(bx, n1, n2, w), hre, him)
    bias2d = jnp.tile(hy_bias, n2).reshape(1, n2 * w)
    y = _hy_last(f4, bre.reshape(bx, n1, n2 * w), bim.reshape(bx, n1, n2 * w), p.reshape(bx, k, n2 * w),
                 x0.reshape(bx, k, n2 * w), bias2d)
    return y.reshape(bx, lx, w)


def _prep_c_kernel(z_ref, cq, uq, dq, ck, uk, dk, gq_ref, gk_ref, qn_ref, kn_ref, q, k, v):
    qw = WA_HEADS * WA_DIM
    kw = WA_KV_HEADS * WA_DIM
    y = _norm_rope(z_ref[0, :, 0:qw], gq_ref[...], qn_ref[...], cq[...], uq[...], dq[...], WA_DIM, WA_DIM // 4)
    q[0] = (y * WA_DIM ** -0.5).astype(BF16)
    y = _norm_rope(z_ref[0, :, qw:qw + kw], gk_ref[...], kn_ref[...], ck[...], uk[...], dk[...], WA_DIM, WA_DIM // 4)
    k[0] = y.astype(BF16)
    v[0] = z_ref[0, :, qw + kw:].astype(BF16)


def _prep_c(zc, tq_tables, tk_tables, qn, kn):
    bx, lx, _ = zc.shape
    tl = min(256, lx)
    qw = WA_HEADS * WA_DIM
    kw = WA_KV_HEADS * WA_DIM
    row = lambda b, i: (b, i, 0)
    const = lambda b, i: (0, 0)
    tabq = pl.BlockSpec((tl, qw), lambda b, i: (i, 0))
    tabk = pl.BlockSpec((tl, kw), lambda b, i: (i, 0))
    return pl.pallas_call(
        _prep_c_kernel,
        grid=(bx, lx // tl),
        in_specs=[pl.BlockSpec((1, tl, W_C), row), tabq, tabq, tabq, tabk, tabk, tabk,
                  pl.BlockSpec((qw, qw), const), pl.BlockSpec((kw, kw), const),
                  pl.BlockSpec((1, qw), const), pl.BlockSpec((1, kw), const)],
        out_specs=[pl.BlockSpec((1, tl, qw), row), pl.BlockSpec((1, tl, kw), row), pl.BlockSpec((1, tl, kw), row)],
        out_shape=[jax.ShapeDtypeStruct((bx, lx, qw), BF16), jax.ShapeDtypeStruct((bx, lx, kw), BF16),
                   jax.ShapeDtypeStruct((bx, lx, kw), BF16)],
        compiler_params=_cp("parallel", "parallel"),
        name="prep_c",
    )(zc, *tq_tables, *tk_tables, _group_ones(qw, WA_DIM), _group_ones(kw, WA_DIM),
      jnp.tile(qn, WA_HEADS).reshape(1, qw), jnp.tile(kn, WA_KV_HEADS).reshape(1, kw))


def _wattn_kernel(sink_ref, q_ref, kp_ref, kc_ref, kn_ref, kx_ref, vp_ref, vc_ref, vn_ref, vx_ref, o_ref, *, banded):
    h = pl.program_id(1)
    n = pl.program_id(2)
    last = pl.num_programs(2) - 1
    gq = q_ref.shape[2]
    q = q_ref[0, 0].reshape(gq * BLOCK, WA_DIM)
    dn = (((1,), (1,)), ((), ()))
    parts = []
    if banded:
        r = lax.broadcasted_iota(jnp.int32, (gq * BLOCK, BLOCK), 0) % BLOCK
        c = lax.broadcasted_iota(jnp.int32, (gq * BLOCK, BLOCK), 1)
        sp = lax.dot_general(q, kp_ref[0, 0], dn, preferred_element_type=F32)
        parts.append((jnp.where((c >= r) & (n > 0), sp, NEG_INF), vp_ref))
        parts.append((lax.dot_general(q, kc_ref[0, 0], dn, preferred_element_type=F32), vc_ref))
        sn = lax.dot_general(q, kn_ref[0, 0], dn, preferred_element_type=F32)
        parts.append((jnp.where((c <= r) & (n < last), sn, NEG_INF), vn_ref))
    parts.append((lax.dot_general(q, kx_ref[0, 0], dn, preferred_element_type=F32), vx_ref))
    rowh = lax.broadcasted_iota(jnp.int32, (gq * BLOCK, 1), 0) // BLOCK
    snk = jnp.zeros((gq * BLOCK, 1), F32)
    for g in range(gq):
        snk = jnp.where(rowh == g, sink_ref[h * gq + g], snk)
    m = snk
    for s, _ in parts:
        m = jnp.maximum(m, jnp.max(s, axis=-1, keepdims=True))
    den = jnp.exp(snk - m)
    acc = jnp.zeros((gq * BLOCK, WA_DIM), F32)
    for s, v_ref in parts:
        p = jnp.exp(s - m)
        den = den + jnp.sum(p, axis=-1, keepdims=True)
        acc = acc + jnp.dot(p.astype(BF16), v_ref[0, 0], preferred_element_type=F32)
    o_ref[0, 0] = (acc / den).reshape(gq, BLOCK, WA_DIM)


def _window_attention(q, k, v, kx, vx, sink, banded):
    b, kv, g, lq, d = q.shape
    cx = kx.shape[2]
    nb = lq // BLOCK
    blk = lambda f: pl.BlockSpec((1, 1, BLOCK, d), f)
    prev = blk(lambda b_, h, n: (b_, h, jnp.maximum(n - 1, 0), 0))
    cur = blk(lambda b_, h, n: (b_, h, n, 0))
    nxt = blk(lambda b_, h, n: (b_, h, jnp.minimum(n + 1, nb - 1), 0))
    ctx = pl.BlockSpec((1, 1, cx, d), lambda b_, h, n: (b_, h, 0, 0))
    qspec = pl.BlockSpec((1, 1, g, BLOCK, d), lambda b_, h, n: (b_, h, 0, n, 0))
    return pl.pallas_call(
        functools.partial(_wattn_kernel, banded=banded),
        grid=(b, kv, nb),
        in_specs=[pl.BlockSpec(memory_space=pltpu.SMEM), qspec, prev, cur, nxt, ctx, prev, cur, nxt, ctx],
        out_specs=qspec,
        out_shape=jax.ShapeDtypeStruct((b, kv, g, lq, d), F32),
        compiler_params=_cp("parallel", "parallel", "arbitrary"),
        name="window_attention",
    )(sink, q, k, k, k, kx, v, v, v, vx)


def _conf_kernel(zp_ref, zc_ref, zn_ref, w_ref, b_ref, lg_ref, lb_ref, o_ref, ext, *, halo):
    i = pl.program_id(1)
    last = pl.num_programs(1) - 1
    tl = zc_ref.shape[1]
    cw = CF_WIDTH
    glu = lambda z: z[:, :cw] * jax.nn.sigmoid(z[:, cw:])
    ext[0:halo] = jnp.where(i == 0, 0.0, glu(zp_ref[0]))
    ext[halo:halo + tl] = glu(zc_ref[0])
    ext[halo + tl:] = jnp.where(i == last, 0.0, glu(zn_ref[0]))
    w = w_ref[...]
    u = jnp.zeros((tl, cw), F32) + b_ref[...]
    for j in range(CF_TAPS):
        u = u + ext[pl.ds(halo - CF_TAPS // 2 + j, tl), :] * w[j:j + 1]
    uc = u - jnp.mean(u, axis=-1, keepdims=True)
    y = uc * lax.rsqrt(jnp.mean(uc * uc, axis=-1, keepdims=True) + EPS) * lg_ref[...] + lb_ref[...]
    o_ref[0] = _silu(y)


def _conformer(zd, dw_w, dw_b, ln_g, ln_b):
    bx, lx, w = zd.shape
    tl = min(256, lx)
    halo = 2 * SUBLANES
    nh = lx // halo
    per = tl // halo
    row = lambda a: a.reshape(1, -1)
    const = lambda b, i: (0, 0)
    return pl.pallas_call(
        functools.partial(_conf_kernel, halo=halo),
        grid=(bx, lx // tl),
        in_specs=[pl.BlockSpec((1, halo, w), lambda b, i: (b, jnp.maximum(i * per - 1, 0), 0)),
                  pl.BlockSpec((1, tl, w), lambda b, i: (b, i, 0)),
                  pl.BlockSpec((1, halo, w), lambda b, i: (b, jnp.minimum((i + 1) * per, nh - 1), 0)),
                  pl.BlockSpec(dw_w.shape, const)] + [pl.BlockSpec((1, CF_WIDTH), const)] * 3,
        out_specs=pl.BlockSpec((1, tl, CF_WIDTH), lambda b, i: (b, i, 0)),
        out_shape=jax.ShapeDtypeStruct((bx, lx, CF_WIDTH), F32),
        scratch_shapes=[pltpu.VMEM((tl + 2 * halo, CF_WIDTH), F32)],
        compiler_params=_cp("parallel", "parallel"),
        name="conformer_conv",
    )(zd, zd, zd, dw_w, row(dw_b), row(ln_g), row(ln_b))


def _merge_kernel(x_ref, m_ref, ya, yb, yc, yd, zg_ref, bg_ref, wb_ref, wo_ref, o_ref, *, goff):
    d = x_ref.shape[-1]
    acc = jnp.zeros(x_ref.shape[1:], F32)
    for i, y in enumerate((ya, yb, yc, yd)):
        gate = jax.nn.sigmoid(zg_ref[0, :, i * d:(i + 1) * d] + bg_ref[:, i * d:(i + 1) * d])
        acc = acc + gate * jnp.dot(y[0].astype(BF16), wb_ref[i], preferred_element_type=F32)
    out = jnp.dot(acc.astype(BF16), wo_ref[...], preferred_element_type=F32)
    o_ref[0] = x_ref[0] + m_ref[0, :, goff:goff + d] * out


def _merge(x, mod, ys, zg, b_gate, w_branch, w_out):
    bx, lx, d = x.shape
    tm = min(256, lx)
    row = lambda b, i: (b, i, 0)
    ysp = pl.BlockSpec((1, tm, BRANCH_W), row)
    return pl.pallas_call(
        functools.partial(_merge_kernel, goff=2 * d),
        grid=(bx, lx // tm),
        in_specs=[pl.BlockSpec((1, tm, d), row), pl.BlockSpec((1, 1, mod.shape[-1]), lambda b, i: (b, 0, 0)),
                  ysp, ysp, ysp, ysp, pl.BlockSpec((1, tm, N_BRANCH * d), row),
                  pl.BlockSpec((1, N_BRANCH * d), lambda b, i: (0, 0)),
                  pl.BlockSpec(w_branch.shape, lambda b, i: (0, 0, 0)), pl.BlockSpec(w_out.shape, lambda b, i: (0, 0))],
        out_specs=pl.BlockSpec((1, tm, d), row),
        out_shape=jax.ShapeDtypeStruct(x.shape, F32),
        input_output_aliases={0: 0},
        compiler_params=_cp("parallel", "parallel"),
        name="merge",
    )(x, mod, *ys, zg, b_gate.reshape(1, -1), w_branch, w_out)


def _moe_kernel(x_ref, m_ref, g_ref, wr_ref, br_ref, w1_ref, w3_ref, w2_ref, o_ref, h_sc, gate_sc, acc_sc):
    e = pl.program_id(2)
    d = x_ref.shape[-1]
    tm = x_ref.shape[1]
    lane = lax.broadcasted_iota(jnp.int32, (tm, LANES), 1).astype(F32)

    @pl.when(e == 0)
    def _():
        h = _rms(x_ref[0]) * g_ref[...] * (1.0 + m_ref[0, :, 4 * d:5 * d]) + m_ref[0, :, 3 * d:4 * d]
        h_sc[...] = h.astype(BF16)
        lg = jnp.dot(h, wr_ref[...], precision=HI, preferred_element_type=F32) + br_ref[...]
        isg = lane < N_GROUPS
        gmax = jnp.max(jnp.where(isg, lg, NEG_INF), axis=-1, keepdims=True)
        gi = jnp.min(jnp.where(isg & (lg == gmax), lane, LANES), axis=-1, keepdims=True)
        gw = 1.0 / jnp.sum(jnp.where(isg, jnp.exp(lg - gmax), 0.0), axis=-1, keepdims=True)
        lo = N_GROUPS + gi * EXP_PER_GROUP
        ise = (lane >= lo) & (lane < lo + EXP_PER_GROUP)
        le = jnp.where(ise, lg, NEG_INF)
        m1 = jnp.max(le, axis=-1, keepdims=True)
        i1 = jnp.min(jnp.where(ise & (le == m1), lane, LANES), axis=-1, keepdims=True)
        ise2 = ise & (lane != i1)
        le2 = jnp.where(ise2, lg, NEG_INF)
        m2 = jnp.max(le2, axis=-1, keepdims=True)
        i2 = jnp.min(jnp.where(ise2 & (le2 == m2), lane, LANES), axis=-1, keepdims=True)
        r = jnp.exp(m2 - m1)
        wa = gw / (1.0 + r)
        gate_sc[...] = jnp.where(lane == i1, wa, 0.0) + jnp.where(lane == i2, wa * r, 0.0)
        acc_sc[...] = jnp.zeros_like(acc_sc)

    hb = h_sc[...]
    u = _silu(jnp.dot(hb, w1_ref[0], preferred_element_type=F32)) * jnp.dot(hb, w3_ref[0], preferred_element_type=F32)
    ge = jnp.sum(jnp.where(lane == (e + N_GROUPS).astype(F32), gate_sc[...], 0.0), axis=-1, keepdims=True)
    acc_sc[...] += ge * jnp.dot(u.astype(BF16), w2_ref[0], preferred_element_type=F32)

    @pl.when(e == pl.num_programs(2) - 1)
    def _():
        o_ref[0] = x_ref[0] + m_ref[0, :, 5 * d:6 * d] * acc_sc[...]


def _moe(x, mod, g, w_router, b_router, w1, w3, w2):
    bx, lx, d = x.shape
    tm = min(1024, lx)
    ne, _, f = w1.shape
    row = lambda b, i, e: (b, i, 0)
    const = lambda b, i, e: (0, 0)
    return pl.pallas_call(
        _moe_kernel,
        grid=(bx, lx // tm, ne),
        in_specs=[pl.BlockSpec((1, tm, d), row), pl.BlockSpec((1, 1, mod.shape[-1]), lambda b, i, e: (b, 0, 0)),
                  pl.BlockSpec((1, d), const), pl.BlockSpec((d, LANES), const), pl.BlockSpec((1, LANES), const),
                  pl.BlockSpec((1, d, f), lambda b, i, e: (e, 0, 0)), pl.BlockSpec((1, d, f), lambda b, i, e: (e, 0, 0)),
                  pl.BlockSpec((1, f, d), lambda b, i, e: (e, 0, 0))],
        out_specs=pl.BlockSpec((1, tm, d), row),
        out_shape=jax.ShapeDtypeStruct(x.shape, F32),
        scratch_shapes=[pltpu.VMEM((tm, d), BF16), pltpu.VMEM((tm, LANES), F32), pltpu.VMEM((tm, d), F32)],
        input_output_aliases={0: 0},
        compiler_params=_cp("parallel", "parallel", "arbitrary"),
        name="moe",
    )(x, mod, g.reshape(1, d), w_router, b_router, w1, w3, w2)


def kernel(x, c, ctx, c_ctx, w_mod, b_mod, norm1_g, norm2_g, w_in, b_gate, da_qn, da_kn, da_lam, da_subln, hy_conv_w, hy_conv_b, hf_w1, hf_b1, hf_w2, hf_b2, hf_w3, hf_b3, hf_w4, hf_freq, hy_bias, wa_qn, wa_kn, wa_sink, cf_dw_w, cf_dw_b, cf_ln_g, cf_ln_b, w_branch, w_out, w_rg, b_rg, w_re, b_re, w1, w3, w2):
    b, s, d = x.shape
    cl = ctx.shape[1]
    depth = w_mod.shape[0]
    assert s % 256 == 0 and cl % 256 == 0 and s % GRID_W == 0

    nrow = -(-(b + 1) // SUBLANES) * SUBLANES
    crows = jnp.zeros((nrow, d), F32).at[:b].set(c).at[b].set(c_ctx)
    mods = _mod_vectors(crows, w_mod, b_mod)

    rope_a = _rope_tables(s, DA_DIM, DA_HEADS)
    unit_a = _unit_tables(cl, DA_HEADS * DA_DIM)
    rope_cq = _rope_tables(s, WA_DIM, WA_HEADS)
    rope_ck = _rope_tables(s, WA_DIM, WA_KV_HEADS)
    unit_cq = _unit_tables(cl, WA_HEADS * WA_DIM)
    unit_ck = _unit_tables(cl, WA_KV_HEADS * WA_DIM)
    hy_lat = (_hy_feat(s), _dft_tables(2 * s))
    hy_ctx = (_hy_feat(cl), _dft_tables(2 * cl))

    xc = ctx
    for l in range(depth):
        last = l == depth - 1
        lam_init = 0.8 - 0.6 * math.exp(-0.3 * l)
        mod_x = mods[l, :b][:, None, :]
        mod_c = jnp.broadcast_to(mods[l, b][None, None, :], (b, 1, mods.shape[-1]))
        wl = w_in[l]
        ws = [wl[:, 0:OFF_B].astype(BF16), wl[:, OFF_B:OFF_C].astype(BF16), wl[:, OFF_C:OFF_D].astype(BF16),
              wl[:, OFF_D:OFF_G].astype(BF16), wl[:, OFF_G:].astype(BF16)]
        za, zb, zc_, zd, zg = _in_proj(x, mod_x, norm1_g[l], ws)
        ca, cb, cc, cd, cg = _in_proj(xc, mod_c, norm1_g[l], ws)
        filt_params = (hf_w1[l], hf_b1[l], hf_w2[l], hf_b2[l], hf_w3[l], hf_b3[l], hf_w4[l], hf_freq[l])
        wbr = w_branch[l].astype(BF16)
        wo = w_out[l].astype(BF16)

        q1, q2, k1, k2, v = _prep_a(za, rope_a, da_qn[l], da_kn[l])
        q1c, q2c, k1c, k2c, vc = _prep_a(ca, unit_a, da_qn[l], da_kn[l])
        cat = lambda a, a_c: jnp.concatenate([a, a_c], axis=1)
        ya = _diff_attention(q1, q2, cat(k1, k1c), cat(k2, k2c), cat(v, vc), da_qn[l], da_kn[l], da_lam[l],
                             da_subln[l], lam_init)
        yb = _hyena(zb, hy_conv_w[l], hy_conv_b[l], filt_params, hy_bias[l], hy_lat)
        qw, kw, vw = _prep_c(zc_, rope_cq, rope_ck, wa_qn[l], wa_kn[l])
        qwc, kwc, vwc = _prep_c(cc, unit_cq, unit_ck, wa_qn[l], wa_kn[l])
        hq = lambda a: _split_heads(a, WA_HEADS, WA_DIM).reshape(a.shape[0], WA_KV_HEADS, WA_GROUP, a.shape[1], WA_DIM)
        hk = lambda a: _split_heads(a, WA_KV_HEADS, WA_DIM)
        unq = lambda o: _merge_heads(o.reshape(o.shape[0], WA_HEADS, o.shape[3], WA_DIM))
        yc_ = unq(_window_attention(hq(qw), hk(kw), hk(vw), hk(kwc), hk(vwc), wa_sink[l], True))
        yd = _conformer(zd, cf_dw_w[l], cf_dw_b[l], cf_ln_g[l], cf_ln_b[l])
        x = _merge(x, mod_x, (ya, yb, yc_, yd), zg, b_gate[l], wbr, wo)

        w_router = jnp.pad(jnp.concatenate([w_rg[l], w_re[l]], axis=1), ((0, 0), (0, LANES - N_GROUPS - N_EXPERTS)))
        b_router = jnp.pad(jnp.concatenate([b_rg[l], b_re[l]]), (0, LANES - N_GROUPS - N_EXPERTS)).reshape(1, LANES)
        ew = (w1[l].astype(BF16), w3[l].astype(BF16), w2[l].astype(BF16))

        if not last:
            yca = _diff_attention(q1c, q2c, k1c, k2c, vc, da_qn[l], da_kn[l], da_lam[l], da_subln[l], lam_init)
            ycb = _hyena(cb, hy_conv_w[l], hy_conv_b[l], filt_params, hy_bias[l], hy_ctx)
            ycc = unq(_window_attention(hq(qwc), hk(kwc), hk(vwc), hk(kwc), hk(vwc), wa_sink[l], False))
            ycd = _conformer(cd, cf_dw_w[l], cf_dw_b[l], cf_ln_g[l], cf_ln_b[l])
            xc = _merge(xc, mod_c, (yca, ycb, ycc, ycd), cg, b_gate[l], wbr, wo)
            xc = _moe(xc.reshape(1, b * cl, d), mod_c[:1], norm2_g[l], w_router, b_router, *ew).reshape(b, cl, d)
        x = _moe(x, mod_x, norm2_g[l], w_router, b_router, *ew)
    return x
```

```python
import functools
import math

import jax
import jax.numpy as jnp
from jax import lax
from jax.experimental import pallas as pl
from jax.experimental.pallas import tpu as pltpu

F32 = jnp.float32
BF16 = jnp.bfloat16
HI = lax.Precision.HIGHEST

GRID_W = 64
BLOCK = 128
ROPE_BASE = 10000.0
EPS = 1e-6
NEG_INF = -1e30

DA_HEADS = 4
DA_DIM = 32
DA_VDIM = 64
HY_WIDTH = 256
HY_BANDS = 16
HY_FF = 64
HY_SHIFT = 0.05
HY_FAST_DECAY = 0.3
HY_SLOW_DECAY = 1.5
HY_TARGET = 1e-2
WA_HEADS = 4
WA_KV_HEADS = 2
WA_GROUP = 2
WA_DIM = 64
CF_WIDTH = 256
CF_TAPS = 31
N_BRANCH = 4
BRANCH_W = 256
N_GROUPS = 4
EXP_PER_GROUP = 4
N_EXPERTS = 16

W_A = 4 * DA_HEADS * DA_DIM + DA_HEADS * DA_VDIM
W_B = 3 * HY_WIDTH
W_C = (WA_HEADS + 2 * WA_KV_HEADS) * WA_DIM
W_D = 2 * CF_WIDTH
OFF_B = W_A
OFF_C = OFF_B + W_B
OFF_D = OFF_C + W_C
OFF_G = OFF_D + W_D

LOG2E = math.log2(math.e)
LANES = 128
SUBLANES = 8
VMEM_LIMIT = 56 * 1024 * 1024

DA_ONES = 16
DA_SHIFT_MAX = 50.0


def _cp(*sem):
    return pltpu.CompilerParams(dimension_semantics=sem, vmem_limit_bytes=VMEM_LIMIT)


def _rms(xf):
    return xf * lax.rsqrt(jnp.mean(xf * xf, axis=-1, keepdims=True) + EPS)


def _silu(x):
    return x * jax.nn.sigmoid(x)


def _mod_kernel(c_ref, w_ref, b_ref, o_ref):
    s = _silu(c_ref[...])
    o_ref[0] = jnp.dot(s, w_ref[0], precision=HI, preferred_element_type=F32) + b_ref[0]


def _mod_vectors(crows, w_mod, b_mod):
    depth, d, n = w_mod.shape
    r = crows.shape[0]
    tn = 1536
    return pl.pallas_call(
        _mod_kernel,
        grid=(depth, n // tn),
        in_specs=[pl.BlockSpec((r, d), lambda l, j: (0, 0)),
                  pl.BlockSpec((1, d, tn), lambda l, j: (l, 0, j)),
                  pl.BlockSpec((1, 1, tn), lambda l, j: (l, 0, j))],
        out_specs=pl.BlockSpec((1, r, tn), lambda l, j: (l, 0, j)),
        out_shape=jax.ShapeDtypeStruct((depth, r, n), F32),
        compiler_params=_cp("arbitrary", "arbitrary"),
        name="mod_vectors",
    )(crows, w_mod, b_mod.reshape(depth, 1, n))


def _rope_tables(s, d, reps):
    rows = s // GRID_W
    row = jnp.repeat(jnp.arange(rows, dtype=F32), GRID_W)
    col = jnp.tile(jnp.arange(GRID_W, dtype=F32), rows)
    qd = d // 4
    inv = ROPE_BASE ** (-jnp.arange(qd, dtype=F32) / qd)
    ar = row[:, None] * inv[None, :]
    ac = col[:, None] * inv[None, :]
    z = jnp.zeros_like(ar)
    cos = jnp.concatenate([jnp.cos(ar), jnp.cos(ar), jnp.cos(ac), jnp.cos(ac)], axis=-1)
    sin_up = jnp.concatenate([-jnp.sin(ar), z, -jnp.sin(ac), z], axis=-1)
    sin_dn = jnp.concatenate([z, jnp.sin(ar), z, jnp.sin(ac)], axis=-1)
    t = lambda a: jnp.tile(a, (1, reps))
    return t(cos), t(sin_up), t(sin_dn)


def _unit_tables(s, w):
    return jnp.ones((s, w), F32), jnp.zeros((s, w), F32), jnp.zeros((s, w), F32)


def _group_ones(width, group):
    i = jnp.arange(width) // group
    return (i[:, None] == i[None, :]).astype(F32)


def _slot_select(width, group):
    i = jnp.arange(width)
    dst = (i // group) * LANES + i % group
    return (dst[:, None] == jnp.arange((width // group) * LANES)[None, :]).astype(BF16)


def _slot_vector(n, lo, hi, value):
    j = jnp.arange(n * LANES) % LANES
    return jnp.where((j >= lo) & (j < hi), value, 0.0).astype(F32).reshape(1, n * LANES)


def _norm_rope(x, gmat, gain, cos, sup, sdn, group, qd):
    w = x.shape[-1]
    ss = jnp.dot(x * x, gmat, precision=HI, preferred_element_type=F32) * (1.0 / group)
    xn = x * lax.rsqrt(ss + EPS) * gain
    return xn * cos + pltpu.roll(xn, w - qd, 1) * sup + pltpu.roll(xn, qd, 1) * sdn


def _inproj_kernel(x_ref, m_ref, g_ref, wa, wb, wc, wd, wg,
                   ca, ua, da, cq, uq, dq, ck, uk, dk, gma, gmq, gmk, qna, kna, qnc, knc,
                   sela, selv, selq, selk, qvec, kvec, vvec,
                   q1o, q2o, k1o, k2o, vo, zbo, qco, kco, vco, zdo, zgo):
    d = x_ref.shape[-1]
    x = x_ref[0]
    shift = m_ref[0, :, 0:d]
    scale = m_ref[0, :, d:2 * d]
    h = (_rms(x) * g_ref[...] * (1.0 + scale) + shift).astype(BF16)
    place = lambda y, sel: jnp.dot(y.astype(BF16), sel[...], preferred_element_type=F32)

    za = jnp.dot(h, wa[...], preferred_element_type=F32)
    hw = DA_HEADS * DA_DIM
    cos, sup, sdn, gm = ca[...], ua[...], da[...], gma[...]
    qscale = DA_DIM ** -0.5 * LOG2E
    for t, (o, gain, sc, vec) in enumerate(((q1o, qna, qscale, qvec), (q2o, qna, qscale, qvec),
                                            (k1o, kna, 1.0, kvec), (k2o, kna, 1.0, kvec))):
        y = _norm_rope(za[:, t * hw:(t + 1) * hw], gm, gain[...], cos, sup, sdn, DA_DIM, DA_DIM // 4) * sc
        o[0] = (place(y, sela) + vec[...]).astype(BF16)
    vo[0] = (place(za[:, 4 * hw:], selv) + vvec[...]).astype(BF16)

    zbo[0] = jnp.dot(h, wb[...], preferred_element_type=F32).astype(BF16)

    zc = jnp.dot(h, wc[...], preferred_element_type=F32)
    qw = WA_HEADS * WA_DIM
    kw = WA_KV_HEADS * WA_DIM
    y = _norm_rope(zc[:, 0:qw], gmq[...], qnc[...], cq[...], uq[...], dq[...], WA_DIM, WA_DIM // 4) * WA_DIM ** -0.5
    qco[0] = place(y, selq).astype(BF16)
    y = _norm_rope(zc[:, qw:qw + kw], gmk[...], knc[...], ck[...], uk[...], dk[...], WA_DIM, WA_DIM // 4)
    kco[0] = place(y, selk).astype(BF16)
    vco[0] = place(zc[:, qw + kw:], selk).astype(BF16)

    zdo[0] = jnp.dot(h, wd[...], preferred_element_type=F32).astype(BF16)
    for k in range(N_BRANCH):
        zgo[0, :, k * d:(k + 1) * d] = jnp.dot(h, wg[:, k * d:(k + 1) * d], preferred_element_type=F32).astype(BF16)


def _in_proj(x, mod, g, ws, tables, consts):
    bx, lx, d = x.shape
    tm = min(256, lx)
    const = lambda b, i: (0, 0)
    row = lambda b, i: (b, i, 0)
    tab = lambda i_: (lambda b, i: (i, 0))
    widths = [DA_HEADS * LANES] * 5 + [W_B, WA_HEADS * LANES, WA_KV_HEADS * LANES, WA_KV_HEADS * LANES, W_D, N_BRANCH * d]
    return pl.pallas_call(
        _inproj_kernel,
        grid=(bx, lx // tm),
        in_specs=[pl.BlockSpec((1, tm, d), row), pl.BlockSpec((1, 1, mod.shape[-1]), lambda b, i: (b, 0, 0)),
                  pl.BlockSpec((1, d), const)]
                 + [pl.BlockSpec(w.shape, const) for w in ws]
                 + [pl.BlockSpec((tm, t.shape[1]), tab(0)) for t in tables]
                 + [pl.BlockSpec(c.shape, const) for c in consts],
        out_specs=[pl.BlockSpec((1, tm, w), row) for w in widths],
        out_shape=[jax.ShapeDtypeStruct((bx, lx, w), BF16) for w in widths],
        compiler_params=_cp("parallel", "parallel"),
        name="in_proj",
    )(x, mod, g.reshape(1, d), *ws, *tables, *consts)


def _da_lambda(lam_ref, lam_init):
    lv = lam_ref[...]
    return (jnp.exp(jnp.sum(lv[0:1] * lv[1:2], keepdims=True)) - jnp.exp(jnp.sum(lv[2:3] * lv[3:4], keepdims=True))
            + lam_init)


def _dattn_kernel(lam_ref, sg_ref, q1_ref, q2_ref, *rest, lam_init, online, nsrc, tk):
    srcs = [rest[3 * s:3 * s + 3] for s in range(nsrc)]
    o_ref, acc1, acc2 = rest[3 * nsrc:]
    dn = (((1,), (1,)), ((), ()))
    q1 = q1_ref[0]
    q2 = q2_ref[0]
    tq = q1.shape[0]
    acc1[...] = jnp.zeros_like(acc1)
    acc2[...] = jnp.zeros_like(acc2)
    carry = (jnp.full((tq, 1), NEG_INF, F32),) * 2 if online else 0

    for k1_ref, k2_ref, v_ref in srcs:
        def body(j, c, k1_ref=k1_ref, k2_ref=k2_ref, v_ref=v_ref):
            rows = pl.ds(pl.multiple_of(j * tk, tk), tk)
            vc = v_ref[0, rows, :]
            s1 = lax.dot_general(q1, k1_ref[0, rows, :], dn, preferred_element_type=F32)
            s2 = lax.dot_general(q2, k2_ref[0, rows, :], dn, preferred_element_type=F32)
            if online:
                m1, m2 = c
                n1 = jnp.maximum(m1, jnp.max(s1, axis=-1, keepdims=True))
                n2 = jnp.maximum(m2, jnp.max(s2, axis=-1, keepdims=True))
                acc1[...] = jnp.exp2(m1 - n1) * acc1[...] + jnp.dot(jnp.exp2(s1 - n1).astype(BF16), vc,
                                                                    preferred_element_type=F32)
                acc2[...] = jnp.exp2(m2 - n2) * acc2[...] + jnp.dot(jnp.exp2(s2 - n2).astype(BF16), vc,
                                                                    preferred_element_type=F32)
                return n1, n2
            acc1[...] += jnp.dot(jnp.exp2(s1).astype(BF16), vc, preferred_element_type=F32)
            acc2[...] += jnp.dot(jnp.exp2(s2).astype(BF16), vc, preferred_element_type=F32)
            return c

        carry = lax.fori_loop(0, k1_ref.shape[1] // tk, body, carry)

    dv = DA_VDIM
    a1 = acc1[...]
    a2 = acc2[...]
    lam = _da_lambda(lam_ref, lam_init)
    o = a1 * (1.0 / a1[:, dv:dv + 1]) - a2 * (lam / a2[:, dv:dv + 1])
    o = jnp.where(lax.broadcasted_iota(jnp.int32, o.shape, 1) < dv, o, 0.0)
    o = o * lax.rsqrt(jnp.sum(o * o, axis=-1, keepdims=True) * (1.0 / dv) + EPS)
    o_ref[0] = (o * (sg_ref[...] * (1.0 - lam_init))).astype(BF16)


def _diff_attention(q1, q2, srcs, fixed, lam_p, subln, lam_init):
    b, lq, _ = q1.shape
    h = DA_HEADS
    tk = 256
    const = lambda b_, h_, i: (0, 0)
    sg = jnp.pad(subln, (0, LANES - DA_VDIM)).reshape(1, LANES)
    flat = [a for src in srcs for a in src]

    def call(online, *args):
        tq = min(256 if online else 2048, lq)
        qs = pl.BlockSpec((1, tq, LANES), lambda b_, h_, i: (b_, i, h_))
        return pl.pallas_call(
            functools.partial(_dattn_kernel, lam_init=lam_init, online=online, nsrc=len(srcs), tk=tk),
            grid=(b, h, lq // tq),
            in_specs=[pl.BlockSpec(lam_p.shape, const), pl.BlockSpec((1, LANES), const), qs, qs]
                     + [pl.BlockSpec((1, a.shape[1], LANES), lambda b_, h_, i: (b_, 0, h_)) for a in flat],
            out_specs=qs,
            out_shape=jax.ShapeDtypeStruct((b, lq, h * LANES), BF16),
            scratch_shapes=[pltpu.VMEM((tq, LANES), F32)] * 2,
            compiler_params=_cp("parallel", "parallel", "arbitrary"),
            name="diff_attention_online" if online else "diff_attention",
        )(*args)

    return lax.cond(fixed, functools.partial(call, False), functools.partial(call, True),
                    lam_p, sg, q1, q2, *flat)


def _hy_prep_kernel(zp_ref, zc_ref, zn_ref, w_ref, b_ref, p_ref, x0_ref, ext):
    i = pl.program_id(1)
    last = pl.num_programs(1) - 1
    tl = zc_ref.shape[1]
    h = 2 * SUBLANES
    ext[0:h] = jnp.where(i == 0, 0.0, zp_ref[0].astype(F32))
    ext[h:h + tl] = zc_ref[0].astype(F32)
    ext[h + tl:] = jnp.where(i == last, 0.0, zn_ref[0].astype(F32))
    w = w_ref[...]
    u = (ext[pl.ds(h - 1, tl), :] * w[0:1] + ext[pl.ds(h, tl), :] * w[1:2] + ext[pl.ds(h + 1, tl), :] * w[2:3]
         + b_ref[...])
    hw = HY_WIDTH
    x0_ref[0] = u[:, 0:hw]
    p_ref[0] = u[:, 2 * hw:3 * hw] * u[:, hw:2 * hw]


def _hy_prep(zb, conv_w, conv_b):
    bx, lx, w = zb.shape
    tl = min(256, lx)
    h = 2 * SUBLANES
    nh = lx // h
    per = tl // h
    out = jax.ShapeDtypeStruct((bx, lx, HY_WIDTH), F32)
    return pl.pallas_call(
        _hy_prep_kernel,
        grid=(bx, lx // tl),
        in_specs=[pl.BlockSpec((1, h, w), lambda b, i: (b, jnp.maximum(i * per - 1, 0), 0)),
                  pl.BlockSpec((1, tl, w), lambda b, i: (b, i, 0)),
                  pl.BlockSpec((1, h, w), lambda b, i: (b, jnp.minimum((i + 1) * per, nh - 1), 0)),
                  pl.BlockSpec(conv_w.shape, lambda b, i: (0, 0)),
                  pl.BlockSpec((1, w), lambda b, i: (0, 0))],
        out_specs=[pl.BlockSpec((1, tl, HY_WIDTH), lambda b, i: (b, i, 0))] * 2,
        out_shape=[out, out],
        scratch_shapes=[pltpu.VMEM((tl + 2 * h, w), F32)],
        compiler_params=_cp("parallel", "parallel"),
        name="hyena_prep",
    )(zb, zb, zb, conv_w, conv_b.reshape(1, w))


def _hy_filter_kernel(feat_ref, w1, b1, w2, b2, w3, b3, w4, fr_ref, dl_ref, filt_ref, ssq_ref, *, s):
    i = pl.program_id(0)
    tr = feat_ref.shape[0]
    feat = feat_ref[...]
    fr = fr_ref[...]
    dot = lambda a, w: jnp.dot(a, w[...], precision=HI, preferred_element_type=F32)
    a = jnp.sin(fr * (dot(feat, w1) + b1[...]))
    a = jnp.sin(fr * (dot(a, w2) + b2[...]))
    a = jnp.sin(fr * (dot(a, w3) + b3[...]))
    coef = dot(a, w4)
    n = i * tr + lax.broadcasted_iota(jnp.int32, (tr, 1), 0)
    window = jnp.exp(-feat[:, 0:1] * dl_ref[...]) + HY_SHIFT
    half = jnp.where(n < s, coef[:, :HY_WIDTH], coef[:, HY_WIDTH:])
    filt = jnp.where(n == s, 0.0, half * window)
    filt_ref[...] = filt

    @pl.when(i == 0)
    def _():
        ssq_ref[...] = jnp.zeros_like(ssq_ref)

    ssq_ref[...] += jnp.sum(filt * filt, axis=0, keepdims=True)


def _hy_feat(s):
    t = jnp.linspace(0.0, 1.0, s, dtype=F32)[:, None]
    w = (2.0 * math.pi / s) * jnp.arange(s, dtype=F32)[:, None]
    bands = jnp.linspace(1e-4, HY_BANDS - 1, HY_BANDS, dtype=F32)[None, :]
    feat = jnp.concatenate([t, jnp.cos(w * bands), jnp.sin(w * bands)], axis=-1)
    feat = jnp.concatenate([feat, feat[:1], feat[:0:-1]], axis=0)
    return jnp.pad(feat, ((0, 0), (0, LANES - feat.shape[1])))


def _hy_filter(s, feat, w1, b1, w2, b2, w3, b3, w4, freq):
    n = 2 * s
    tr = min(512, n)
    deltas = jnp.abs(jnp.linspace(math.log(HY_TARGET) / HY_FAST_DECAY, math.log(HY_TARGET) / HY_SLOW_DECAY,
                                  HY_WIDTH, dtype=F32)).reshape(1, HY_WIDTH)
    w1p = jnp.pad(w1, ((0, LANES - w1.shape[0]), (0, 0)))
    row = lambda a: a.reshape(1, -1)
    args = (feat, w1p, row(b1), w2, row(b2), w3, row(b3), w4, row(freq), deltas)
    const = lambda i: (0, 0)
    return pl.pallas_call(
        functools.partial(_hy_filter_kernel, s=s),
        grid=(n // tr,),
        in_specs=[pl.BlockSpec((tr, LANES), lambda i: (i, 0))] + [pl.BlockSpec(a.shape, const) for a in args[1:]],
        out_specs=[pl.BlockSpec((tr, HY_WIDTH), lambda i: (i, 0)), pl.BlockSpec((1, HY_WIDTH), const)],
        out_shape=[jax.ShapeDtypeStruct((n, HY_WIDTH), F32), jax.ShapeDtypeStruct((1, HY_WIDTH), F32)],
        compiler_params=_cp("arbitrary"),
        name="hyena_filter",
    )(*args)


def _dft_factors(n):
    lg = n.bit_length() - 1
    assert 1 << lg == n
    n1 = 1 << (lg // 2)
    return n1, n // n1


def _hi_lo(a):
    hi = a.astype(BF16)
    return hi, (a - hi.astype(F32)).astype(BF16)


def _dft_tables(n):
    n1, n2 = _dft_factors(n)
    ia = jnp.arange(n1, dtype=jnp.int32)
    ang1 = (2.0 * math.pi / n1) * ((ia[:, None] * ia[None, :]) % n1).astype(F32)
    f1 = jnp.concatenate([jnp.cos(ang1), -jnp.sin(ang1)], axis=0)
    c = jnp.arange(n1, dtype=jnp.int32)[:, None, None]
    d = jnp.arange(n2, dtype=jnp.int32)[None, :, None]
    b = jnp.arange(n2, dtype=jnp.int32)[None, None, :]
    ang = (2.0 * math.pi / n) * ((b * (c + n1 * d)) % n).astype(F32)
    re, im = jnp.cos(ang), -jnp.sin(ang)
    m1 = jnp.concatenate([jnp.concatenate([re, -im], axis=2), jnp.concatenate([im, re], axis=2)], axis=1)
    m2 = jnp.swapaxes(m1, 1, 2)
    ang4 = ang1[: n1 // 2]
    f4 = jnp.concatenate([jnp.cos(ang4), -jnp.sin(ang4)], axis=1) * (1.0 / n)
    return f1, _hi_lo(m1), _hi_lo(m2), f4


def _dot3(mh, ml, a):
    ah, al = _hi_lo(a)
    return (jnp.dot(mh, ah, preferred_element_type=F32) + jnp.dot(mh, al, preferred_element_type=F32)
            + jnp.dot(ml, ah, preferred_element_type=F32))


def _hy_stage1_kernel(f_ref, x_ref, re_ref, im_ref):
    n1 = re_ref.shape[1]
    y = jnp.dot(f_ref[...], x_ref[0], precision=HI, preferred_element_type=F32)
    re_ref[0] = y[:n1]
    im_ref[0] = y[n1:]


def _hy_stage1(f, x2d):
    bx, k, lanes = x2d.shape
    n1 = f.shape[0] // 2
    tl = min(2048, lanes)
    out = jax.ShapeDtypeStruct((bx, n1, lanes), F32)
    return pl.pallas_call(
        _hy_stage1_kernel,
        grid=(bx, lanes // tl),
        in_specs=[pl.BlockSpec(f.shape, lambda b, j: (0, 0)), pl.BlockSpec((1, k, tl), lambda b, j: (b, 0, j))],
        out_specs=[pl.BlockSpec((1, n1, tl), lambda b, j: (b, 0, j))] * 2,
        out_shape=[out, out],
        compiler_params=_cp("parallel", "parallel"),
        name="hyena_dft_stage1",
    )(f, x2d)


def _hy_mid_kernel(m1h_ref, m1l_ref, m2h_ref, m2l_ref, re_ref, im_ref, hre_ref, him_ref, ore_ref, oim_ref):
    n2 = re_ref.shape[2]
    for c in range(re_ref.shape[1]):
        a = jnp.concatenate([re_ref[0, c], im_ref[0, c]], axis=0)
        x = _dot3(m1h_ref[c], m1l_ref[c], a)
        xre, xim = x[:n2], x[n2:]
        hre, him = hre_ref[c], him_ref[c]
        y = jnp.concatenate([xre * hre - xim * him, xre * him + xim * hre], axis=0)
        bb = _dot3(m2h_ref[c], m2l_ref[c], y)
        ore_ref[0, c] = bb[:n2]
        oim_ref[0, c] = bb[n2:]


def _hy_spec_kernel(m1h_ref, m1l_ref, re_ref, im_ref, rs_ref, ore_ref, oim_ref):
    n2 = re_ref.shape[2]
    for c in range(re_ref.shape[1]):
        a = jnp.concatenate([re_ref[0, c], im_ref[0, c]], axis=0)
        x = _dot3(m1h_ref[c], m1l_ref[c], a) * rs_ref[...]
        ore_ref[c] = x[:n2]
        oim_ref[c] = x[n2:]


def _hy_filter_spectrum(m1, are, aim, rs):
    _, n1, n2, w = are.shape
    cb = min(4, n1)
    blk = pl.BlockSpec((1, cb, n2, w), lambda c: (0, c, 0, 0))
    mblk = pl.BlockSpec((cb, 2 * n2, 2 * n2), lambda c: (c, 0, 0))
    oblk = pl.BlockSpec((cb, n2, w), lambda c: (c, 0, 0))
    out = jax.ShapeDtypeStruct((n1, n2, w), F32)
    return pl.pallas_call(
        _hy_spec_kernel,
        grid=(n1 // cb,),
        in_specs=[mblk, mblk, blk, blk, pl.BlockSpec((1, w), lambda c: (0, 0))],
        out_specs=[oblk, oblk],
        out_shape=[out, out],
        compiler_params=_cp("parallel"),
        name="hyena_filter_spectrum",
    )(*m1, are, aim, rs)


def _hy_mid(m1, m2, are, aim, hre, him):
    bx, n1, n2, w = are.shape
    cb = min(4, n1)
    blk = pl.BlockSpec((1, cb, n2, w), lambda c, b: (b, c, 0, 0))
    mblk = pl.BlockSpec((cb, 2 * n2, 2 * n2), lambda c, b: (c, 0, 0))
    hblk = pl.BlockSpec((cb, n2, w), lambda c, b: (c, 0, 0))
    out = jax.ShapeDtypeStruct((bx, n1, n2, w), F32)
    return pl.pallas_call(
        _hy_mid_kernel,
        grid=(n1 // cb, bx),
        in_specs=[mblk, mblk, mblk, mblk, blk, blk, hblk, hblk],
        out_specs=[blk, blk],
        out_shape=[out, out],
        compiler_params=_cp("parallel", "parallel"),
        name="hyena_dft_mid",
    )(*m1, *m2, are, aim, hre, him)


def _hy_last_kernel(f_ref, re_ref, im_ref, p_ref, x0_ref, bias_ref, o_ref):
    bb = jnp.concatenate([re_ref[0], im_ref[0]], axis=0)
    y = jnp.dot(f_ref[...], bb, precision=HI, preferred_element_type=F32)
    o_ref[0] = (x0_ref[0] * (y + p_ref[0] * bias_ref[...])).astype(BF16)


def _hy_last(f4, bre, bim, p2d, x02d, bias2d):
    bx, n1, lanes = bre.shape
    k = n1 // 2
    tl = min(2048, lanes)
    big = pl.BlockSpec((1, n1, tl), lambda b, j: (b, 0, j))
    small = pl.BlockSpec((1, k, tl), lambda b, j: (b, 0, j))
    return pl.pallas_call(
        _hy_last_kernel,
        grid=(bx, lanes // tl),
        in_specs=[pl.BlockSpec(f4.shape, lambda b, j: (0, 0)), big, big, small, small,
                  pl.BlockSpec((1, tl), lambda b, j: (0, j))],
        out_specs=small,
        out_shape=jax.ShapeDtypeStruct((bx, k, lanes), BF16),
        compiler_params=_cp("parallel", "parallel"),
        name="hyena_dft_last",
    )(f4, bre, bim, p2d, x02d, bias2d)


def _hyena(zb, conv_w, conv_b, filt_params, hy_bias, consts):
    bx, lx, _ = zb.shape
    feat, (f1, m1, m2, f4) = consts
    n = 2 * lx
    n1, n2 = _dft_factors(n)
    w = HY_WIDTH
    filt, ssq = _hy_filter(lx, feat, *filt_params)
    rs = lax.rsqrt(ssq + EPS)
    fre, fim = _hy_stage1(f1, filt.reshape(1, n1, n2 * w))
    hre, him = _hy_filter_spectrum(m1, fre.reshape(1, n1, n2, w), fim.reshape(1, n1, n2, w), rs)
    p, x0 = _hy_prep(zb, conv_w, conv_b)
    k = n1 // 2
    are, aim = _hy_stage1(f1[:, :k], p.reshape(bx, k, n2 * w))
    bre, bim = _hy_mid(m1, m2, are.reshape(bx, n1, n2, w), aim.reshape(bx, n1, n2, w), hre, him)
    bias2d = jnp.tile(hy_bias, n2).reshape(1, n2 * w)
    y = _hy_last(f4, bre.reshape(bx, n1, n2 * w), bim.reshape(bx, n1, n2 * w), p.reshape(bx, k, n2 * w),
                 x0.reshape(bx, k, n2 * w), bias2d)
    return y.reshape(bx, lx, w)


def _wattn_kernel(sink_ref, q_ref, *rest, banded):
    if banded:
        kp_ref, kc_ref, kn_ref, kx_ref, vp_ref, vc_ref, vn_ref, vx_ref, o_ref = rest
    else:
        kx_ref, vx_ref, o_ref = rest
    h = pl.program_id(1)
    i = pl.program_id(2)
    last = pl.num_programs(2) - 1
    qb = q_ref.shape[1]
    q = jnp.concatenate([q_ref[0, :, :LANES], q_ref[0, :, LANES:]], axis=0)
    dn = (((1,), (1,)), ((), ()))
    parts = []
    if banded:
        kk = jnp.concatenate([kp_ref[0], kc_ref[0], kn_ref[0]], axis=0)
        vv = jnp.concatenate([vp_ref[0], vc_ref[0], vn_ref[0]], axis=0)
        s = lax.dot_general(q, kk, dn, preferred_element_type=F32)
        r = lax.broadcasted_iota(jnp.int32, s.shape, 0) % qb
        c = lax.broadcasted_iota(jnp.int32, s.shape, 1) - BLOCK
        valid = ((jnp.abs(r - c) <= BLOCK) & ((c >= 0) | (i > 0)) & ((c < qb) | (i < last)))
        parts.append((jnp.where(valid, s, NEG_INF), vv))
    parts.append((lax.dot_general(q, kx_ref[0], dn, preferred_element_type=F32), vx_ref[0]))
    top = lax.broadcasted_iota(jnp.int32, (2 * qb, 1), 0) < qb
    snk = jnp.where(top, sink_ref[h * WA_GROUP], sink_ref[h * WA_GROUP + 1])
    m = snk
    for s, _ in parts:
        m = jnp.maximum(m, jnp.max(s, axis=-1, keepdims=True))
    den = jnp.exp(snk - m)
    acc = jnp.zeros((2 * qb, LANES), F32)
    for s, v in parts:
        p = jnp.exp(s - m)
        den = den + jnp.sum(p, axis=-1, keepdims=True)
        acc = acc + jnp.dot(p.astype(BF16), v, preferred_element_type=F32)
    o = (acc / den).astype(BF16)
    o_ref[0] = jnp.concatenate([o[:qb], o[qb:]], axis=1)


def _window_attention(q, k, v, kx, vx, sink, banded):
    b, lq, _ = q.shape
    cx = kx.shape[1]
    qb = min(512, lq)
    per = qb // BLOCK
    nblk = lq // BLOCK
    side = lambda f: pl.BlockSpec((1, BLOCK, LANES), f)
    prev = side(lambda b_, h, i: (b_, jnp.maximum(i * per - 1, 0), h))
    nxt = side(lambda b_, h, i: (b_, jnp.minimum((i + 1) * per, nblk - 1), h))
    cur = pl.BlockSpec((1, qb, LANES), lambda b_, h, i: (b_, i, h))
    ctx = pl.BlockSpec((1, cx, LANES), lambda b_, h, i: (b_, 0, h))
    qspec = pl.BlockSpec((1, qb, WA_GROUP * LANES), lambda b_, h, i: (b_, i, h))
    if banded:
        specs, args = [prev, cur, nxt, ctx, prev, cur, nxt, ctx], (k, k, k, kx, v, v, v, vx)
    else:
        specs, args = [ctx, ctx], (kx, vx)
    return pl.pallas_call(
        functools.partial(_wattn_kernel, banded=banded),
        grid=(b, WA_KV_HEADS, lq // qb),
        in_specs=[pl.BlockSpec(memory_space=pltpu.SMEM), qspec] + specs,
        out_specs=qspec,
        out_shape=jax.ShapeDtypeStruct((b, lq, WA_HEADS * LANES), BF16),
        compiler_params=_cp("parallel", "parallel", "arbitrary"),
        name="window_attention",
    )(sink, q, *args)


def _conf_kernel(zp_ref, zc_ref, zn_ref, w_ref, b_ref, lg_ref, lb_ref, o_ref, ext, *, halo):
    i = pl.program_id(1)
    last = pl.num_programs(1) - 1
    tl = zc_ref.shape[1]
    cw = CF_WIDTH

    def glu(z_ref):
        z = z_ref[0].astype(F32)
        return z[:, :cw] * jax.nn.sigmoid(z[:, cw:])

    ext[0:halo] = jnp.where(i == 0, 0.0, glu(zp_ref))
    ext[halo:halo + tl] = glu(zc_ref)
    ext[halo + tl:] = jnp.where(i == last, 0.0, glu(zn_ref))
    w = w_ref[...]
    u = jnp.zeros((tl, cw), F32) + b_ref[...]
    for j in range(CF_TAPS):
        u = u + ext[pl.ds(halo - CF_TAPS // 2 + j, tl), :] * w[j:j + 1]
    uc = u - jnp.mean(u, axis=-1, keepdims=True)
    y = uc * lax.rsqrt(jnp.mean(uc * uc, axis=-1, keepdims=True) + EPS) * lg_ref[...] + lb_ref[...]
    o_ref[0] = _silu(y).astype(BF16)


def _conformer(zd, dw_w, dw_b, ln_g, ln_b):
    bx, lx, w = zd.shape
    tl = min(256, lx)
    halo = 2 * SUBLANES
    nh = lx // halo
    per = tl // halo
    row = lambda a: a.reshape(1, -1)
    const = lambda b, i: (0, 0)
    return pl.pallas_call(
        functools.partial(_conf_kernel, halo=halo),
        grid=(bx, lx // tl),
        in_specs=[pl.BlockSpec((1, halo, w), lambda b, i: (b, jnp.maximum(i * per - 1, 0), 0)),
                  pl.BlockSpec((1, tl, w), lambda b, i: (b, i, 0)),
                  pl.BlockSpec((1, halo, w), lambda b, i: (b, jnp.minimum((i + 1) * per, nh - 1), 0)),
                  pl.BlockSpec(dw_w.shape, const)] + [pl.BlockSpec((1, CF_WIDTH), const)] * 3,
        out_specs=pl.BlockSpec((1, tl, CF_WIDTH), lambda b, i: (b, i, 0)),
        out_shape=jax.ShapeDtypeStruct((bx, lx, CF_WIDTH), BF16),
        scratch_shapes=[pltpu.VMEM((tl + 2 * halo, CF_WIDTH), F32)],
        compiler_params=_cp("parallel", "parallel"),
        name="conformer_conv",
    )(zd, zd, zd, dw_w, row(dw_b), row(ln_g), row(ln_b))


def _merge_kernel(x_ref, m_ref, ya, yb, yc, yd, zg_ref, bg_ref, wa, wb, wc, wd, wo_ref, o_ref, *, goff):
    d = x_ref.shape[-1]
    acc = jnp.zeros(x_ref.shape[1:], F32)
    for i, (y, w) in enumerate(((ya, wa), (yb, wb), (yc, wc), (yd, wd))):
        gate = jax.nn.sigmoid(zg_ref[0, :, i * d:(i + 1) * d].astype(F32) + bg_ref[:, i * d:(i + 1) * d])
        acc = acc + gate * jnp.dot(y[0], w[...], preferred_element_type=F32)
    out = jnp.dot(acc.astype(BF16), wo_ref[...], preferred_element_type=F32)
    o_ref[0] = x_ref[0] + m_ref[0, :, goff:goff + d] * out


def _merge(x, mod, ys, zg, b_gate, wbs, w_out):
    bx, lx, d = x.shape
    tm = min(256, lx)
    row = lambda b, i: (b, i, 0)
    const = lambda b, i: (0, 0)
    return pl.pallas_call(
        functools.partial(_merge_kernel, goff=2 * d),
        grid=(bx, lx // tm),
        in_specs=[pl.BlockSpec((1, tm, d), row), pl.BlockSpec((1, 1, mod.shape[-1]), lambda b, i: (b, 0, 0))]
                 + [pl.BlockSpec((1, tm, y.shape[-1]), row) for y in ys]
                 + [pl.BlockSpec((1, tm, N_BRANCH * d), row), pl.BlockSpec((1, N_BRANCH * d), const)]
                 + [pl.BlockSpec(w.shape, const) for w in wbs] + [pl.BlockSpec(w_out.shape, const)],
        out_specs=pl.BlockSpec((1, tm, d), row),
        out_shape=jax.ShapeDtypeStruct(x.shape, F32),
        input_output_aliases={0: 0},
        compiler_params=_cp("parallel", "parallel"),
        name="merge",
    )(x, mod, *ys, zg, b_gate.reshape(1, -1), *wbs, w_out)


def _slot_rows(w, group):
    n = w.shape[0] // group
    return jnp.pad(w.reshape(n, group, -1), ((0, 0), (0, LANES - group), (0, 0))).reshape(n * LANES, -1)


def _moe_kernel(x_ref, m_ref, g_ref, wr_ref, br_ref, w1_ref, w3_ref, w2_ref, o_ref, h_sc, gate_sc, acc_sc):
    e = pl.program_id(2)
    d = x_ref.shape[-1]
    tm = x_ref.shape[1]
    lane = lax.broadcasted_iota(jnp.int32, (tm, LANES), 1).astype(F32)

    @pl.when(e == 0)
    def _():
        h = _rms(x_ref[0]) * g_ref[...] * (1.0 + m_ref[0, :, 4 * d:5 * d]) + m_ref[0, :, 3 * d:4 * d]
        h_sc[...] = h.astype(BF16)
        lg = jnp.dot(h, wr_ref[...], precision=HI, preferred_element_type=F32) + br_ref[...]
        isg = lane < N_GROUPS
        gmax = jnp.max(jnp.where(isg, lg, NEG_INF), axis=-1, keepdims=True)
        gi = jnp.min(jnp.where(isg & (lg == gmax), lane, LANES), axis=-1, keepdims=True)
        gw = 1.0 / jnp.sum(jnp.where(isg, jnp.exp(lg - gmax), 0.0), axis=-1, keepdims=True)
        lo = N_GROUPS + gi * EXP_PER_GROUP
        ise = (lane >= lo) & (lane < lo + EXP_PER_GROUP)
        le = jnp.where(ise, lg, NEG_INF)
        m1 = jnp.max(le, axis=-1, keepdims=True)
        i1 = jnp.min(jnp.where(ise & (le == m1), lane, LANES), axis=-1, keepdims=True)
        ise2 = ise & (lane != i1)
        le2 = jnp.where(ise2, lg, NEG_INF)
        m2 = jnp.max(le2, axis=-1, keepdims=True)
        i2 = jnp.min(jnp.where(ise2 & (le2 == m2), lane, LANES), axis=-1, keepdims=True)
        r = jnp.exp(m2 - m1)
        wa = gw / (1.0 + r)
        gate_sc[...] = jnp.where(lane == i1, wa, 0.0) + jnp.where(lane == i2, wa * r, 0.0)
        acc_sc[...] = jnp.zeros_like(acc_sc)

    hb = h_sc[...]
    u = _silu(jnp.dot(hb, w1_ref[0], preferred_element_type=F32)) * jnp.dot(hb, w3_ref[0], preferred_element_type=F32)
    ge = jnp.sum(jnp.where(lane == (e + N_GROUPS).astype(F32), gate_sc[...], 0.0), axis=-1, keepdims=True)
    acc_sc[...] += ge * jnp.dot(u.astype(BF16), w2_ref[0], preferred_element_type=F32)

    @pl.when(e == pl.num_programs(2) - 1)
    def _():
        o_ref[0] = x_ref[0] + m_ref[0, :, 5 * d:6 * d] * acc_sc[...]


def _moe(x, mod, g, w_router, b_router, w1, w3, w2):
    bx, lx, d = x.shape
    tm = min(1024, lx)
    ne, _, f = w1.shape
    row = lambda b, i, e: (b, i, 0)
    const = lambda b, i, e: (0, 0)
    return pl.pallas_call(
        _moe_kernel,
        grid=(bx, lx // tm, ne),
        in_specs=[pl.BlockSpec((1, tm, d), row), pl.BlockSpec((1, 1, mod.shape[-1]), lambda b, i, e: (b, 0, 0)),
                  pl.BlockSpec((1, d), const), pl.BlockSpec((d, LANES), const), pl.BlockSpec((1, LANES), const),
                  pl.BlockSpec((1, d, f), lambda b, i, e: (e, 0, 0)), pl.BlockSpec((1, d, f), lambda b, i, e: (e, 0, 0)),
                  pl.BlockSpec((1, f, d), lambda b, i, e: (e, 0, 0))],
        out_specs=pl.BlockSpec((1, tm, d), row),
        out_shape=jax.ShapeDtypeStruct(x.shape, F32),
        scratch_shapes=[pltpu.VMEM((tm, d), BF16), pltpu.VMEM((tm, LANES), F32), pltpu.VMEM((tm, d), F32)],
        input_output_aliases={0: 0},
        compiler_params=_cp("parallel", "parallel", "arbitrary"),
        name="moe",
    )(x, mod, g.reshape(1, d), w_router, b_router, w1, w3, w2)


def kernel(x, c, ctx, c_ctx, w_mod, b_mod, norm1_g, norm2_g, w_in, b_gate, da_qn, da_kn, da_lam, da_subln, hy_conv_w, hy_conv_b, hf_w1, hf_b1, hf_w2, hf_b2, hf_w3, hf_b3, hf_w4, hf_freq, hy_bias, wa_qn, wa_kn, wa_sink, cf_dw_w, cf_dw_b, cf_ln_g, cf_ln_b, w_branch, w_out, w_rg, b_rg, w_re, b_re, w1, w3, w2):
    b, s, d = x.shape
    cl = ctx.shape[1]
    depth = w_mod.shape[0]
    assert s % 256 == 0 and cl % 256 == 0 and s % GRID_W == 0

    nrow = -(-(b + 1) // SUBLANES) * SUBLANES
    crows = jnp.zeros((nrow, d), F32).at[:b].set(c).at[b].set(c_ctx)
    mods = _mod_vectors(crows, w_mod, b_mod)

    aw, qw, kw = DA_HEADS * DA_DIM, WA_HEADS * WA_DIM, WA_KV_HEADS * WA_DIM
    tab_lat = (*_rope_tables(s, DA_DIM, DA_HEADS), *_rope_tables(s, WA_DIM, WA_HEADS), *_rope_tables(s, WA_DIM, WA_KV_HEADS))
    tab_ctx = (*_unit_tables(cl, aw), *_unit_tables(cl, qw), *_unit_tables(cl, kw))
    hy_lat = (_hy_feat(s), _dft_tables(2 * s))
    hy_ctx = (_hy_feat(cl), _dft_tables(2 * cl))
    gms = (_group_ones(aw, DA_DIM), _group_ones(qw, WA_DIM), _group_ones(kw, WA_DIM))
    sels = (_slot_select(aw, DA_DIM), _slot_select(DA_HEADS * DA_VDIM, DA_VDIM), _slot_select(qw, WA_DIM),
            _slot_select(kw, WA_DIM))
    qvec = _slot_vector(DA_HEADS, DA_DIM, DA_DIM + 1, 1.0)
    vvec = _slot_vector(DA_HEADS, DA_VDIM, DA_VDIM + DA_ONES, 1.0)

    xc = ctx
    for l in range(depth):
        last = l == depth - 1
        lam_init = 0.8 - 0.6 * math.exp(-0.3 * l)
        mod_x = mods[l, :b][:, None, :]
        mod_c = jnp.broadcast_to(mods[l, b][None, None, :], (b, 1, mods.shape[-1]))
        wl = w_in[l]
        ws = [wl[:, 0:OFF_B].astype(BF16), wl[:, OFF_B:OFF_C].astype(BF16), wl[:, OFF_C:OFF_D].astype(BF16),
              wl[:, OFF_D:OFF_G].astype(BF16), wl[:, OFF_G:].astype(BF16)]
        shift = 1.02 * LOG2E * DA_DIM ** 0.5 * jnp.max(jnp.abs(da_qn[l])) * jnp.max(jnp.abs(da_kn[l]))
        fixed = shift <= DA_SHIFT_MAX
        kvec = _slot_vector(DA_HEADS, DA_DIM, DA_DIM + 1, jnp.where(fixed, -shift, 0.0))
        tile = lambda a, n: jnp.tile(a, n).reshape(1, -1)
        consts = (*gms, tile(da_qn[l], DA_HEADS), tile(da_kn[l], DA_HEADS), tile(wa_qn[l], WA_HEADS),
                  tile(wa_kn[l], WA_KV_HEADS), *sels, qvec, kvec, vvec)
        q1, q2, k1, k2, v, zb, qc, kc, vc, zd, zg = _in_proj(x, mod_x, norm1_g[l], ws, tab_lat, consts)
        q1x, q2x, k1x, k2x, vx, zbx, qcx, kcx, vcx, zdx, zgx = _in_proj(xc, mod_c, norm1_g[l], ws, tab_ctx, consts)
        filt_params = (hf_w1[l], hf_b1[l], hf_w2[l], hf_b2[l], hf_w3[l], hf_b3[l], hf_w4[l], hf_freq[l])
        wb4 = w_branch[l].astype(BF16)
        wbs = (_slot_rows(wb4[0], DA_VDIM), wb4[1], _slot_rows(wb4[2], WA_DIM), wb4[3])
        wo = w_out[l].astype(BF16)

        ya = _diff_attention(q1, q2, [(k1, k2, v), (k1x, k2x, vx)], fixed, da_lam[l], da_subln[l], lam_init)
        yb = _hyena(zb, hy_conv_w[l], hy_conv_b[l], filt_params, hy_bias[l], hy_lat)
        yc_ = _window_attention(qc, kc, vc, kcx, vcx, wa_sink[l], True)
        yd = _conformer(zd, cf_dw_w[l], cf_dw_b[l], cf_ln_g[l], cf_ln_b[l])
        x = _merge(x, mod_x, (ya, yb, yc_, yd), zg, b_gate[l], wbs, wo)

        w_router = jnp.pad(jnp.concatenate([w_rg[l], w_re[l]], axis=1), ((0, 0), (0, LANES - N_GROUPS - N_EXPERTS)))
        b_router = jnp.pad(jnp.concatenate([b_rg[l], b_re[l]]), (0, LANES - N_GROUPS - N_EXPERTS)).reshape(1, LANES)
        ew = (w1[l].astype(BF16), w3[l].astype(BF16), w2[l].astype(BF16))

        if not last:
            yca = _diff_attention(q1x, q2x, [(k1x, k2x, vx)], fixed, da_lam[l], da_subln[l], lam_init)
            ycb = _hyena(zbx, hy_conv_w[l], hy_conv_b[l], filt_params, hy_bias[l], hy_ctx)
            ycc = _window_attention(qcx, kcx, vcx, kcx, vcx, wa_sink[l], False)
            ycd = _conformer(zdx, cf_dw_w[l], cf_dw_b[l], cf_ln_g[l], cf_ln_b[l])
            xc = _merge(xc, mod_c, (yca, ycb, ycc, ycd), zgx, b_gate[l], wbs, wo)
            xc = _moe(xc.reshape(1, b * cl, d), mod_c[:1], norm2_g[l], w_router, b_router, *ew).reshape(b, cl, d)
        x = _moe(x, mod_x, norm2_g[l], w_router, b_router, *ew)
    return x
```

```python
import functools
import math

import jax
import jax.numpy as jnp
from jax import lax
from jax.experimental import pallas as pl
from jax.experimental.pallas import tpu as pltpu

F32 = jnp.float32
BF16 = jnp.bfloat16
HI = lax.Precision.HIGHEST

GRID_W = 64
BLOCK = 128
ROPE_BASE = 10000.0
EPS = 1e-6
NEG_INF = -1e30

DA_HEADS = 4
DA_DIM = 32
DA_VDIM = 64
HY_WIDTH = 256
HY_BANDS = 16
HY_FF = 64
HY_SHIFT = 0.05
HY_FAST_DECAY = 0.3
HY_SLOW_DECAY = 1.5
HY_TARGET = 1e-2
WA_HEADS = 4
WA_KV_HEADS = 2
WA_GROUP = 2
WA_DIM = 64
CF_WIDTH = 256
CF_TAPS = 31
N_BRANCH = 4
BRANCH_W = 256
N_GROUPS = 4
EXP_PER_GROUP = 4
N_EXPERTS = 16

W_A = 4 * DA_HEADS * DA_DIM + DA_HEADS * DA_VDIM
W_B = 3 * HY_WIDTH
W_C = (WA_HEADS + 2 * WA_KV_HEADS) * WA_DIM
W_D = 2 * CF_WIDTH
OFF_B = W_A
OFF_C = OFF_B + W_B
OFF_D = OFF_C + W_C
OFF_G = OFF_D + W_D

LOG2E = math.log2(math.e)
LANES = 128
SUBLANES = 8
VMEM_LIMIT = 56 * 1024 * 1024

DA_ONES = 16
DA_SHIFT_MAX = 50.0


def _cp(*sem):
    return pltpu.CompilerParams(dimension_semantics=sem, vmem_limit_bytes=VMEM_LIMIT)


def _rms(xf):
    return xf * lax.rsqrt(jnp.mean(xf * xf, axis=-1, keepdims=True) + EPS)


def _silu(x):
    return x * jax.nn.sigmoid(x)


def _mod_kernel(c_ref, w_ref, b_ref, o_ref):
    s = _silu(c_ref[...])
    o_ref[0] = jnp.dot(s, w_ref[0], precision=HI, preferred_element_type=F32) + b_ref[0]


def _mod_vectors(crows, w_mod, b_mod):
    depth, d, n = w_mod.shape
    r = crows.shape[0]
    tn = 1536
    return pl.pallas_call(
        _mod_kernel,
        grid=(depth, n // tn),
        in_specs=[pl.BlockSpec((r, d), lambda l, j: (0, 0)),
                  pl.BlockSpec((1, d, tn), lambda l, j: (l, 0, j)),
                  pl.BlockSpec((1, 1, tn), lambda l, j: (l, 0, j))],
        out_specs=pl.BlockSpec((1, r, tn), lambda l, j: (l, 0, j)),
        out_shape=jax.ShapeDtypeStruct((depth, r, n), F32),
        compiler_params=_cp("arbitrary", "arbitrary"),
        name="mod_vectors",
    )(crows, w_mod, b_mod.reshape(depth, 1, n))


def _rope_tables(s, d, reps):
    rows = s // GRID_W
    row = jnp.repeat(jnp.arange(rows, dtype=F32), GRID_W)
    col = jnp.tile(jnp.arange(GRID_W, dtype=F32), rows)
    qd = d // 4
    inv = ROPE_BASE ** (-jnp.arange(qd, dtype=F32) / qd)
    ar = row[:, None] * inv[None, :]
    ac = col[:, None] * inv[None, :]
    z = jnp.zeros_like(ar)
    cos = jnp.concatenate([jnp.cos(ar), jnp.cos(ar), jnp.cos(ac), jnp.cos(ac)], axis=-1)
    sin_up = jnp.concatenate([-jnp.sin(ar), z, -jnp.sin(ac), z], axis=-1)
    sin_dn = jnp.concatenate([z, jnp.sin(ar), z, jnp.sin(ac)], axis=-1)
    t = lambda a: jnp.tile(a, (1, reps))
    return t(cos), t(sin_up), t(sin_dn)


def _unit_tables(s, w):
    return jnp.ones((s, w), F32), jnp.zeros((s, w), F32), jnp.zeros((s, w), F32)


def _group_ones(width, group):
    i = jnp.arange(width) // group
    return (i[:, None] == i[None, :]).astype(BF16)


def _hi_lo(a):
    hi = a.astype(BF16)
    return hi, (a - hi.astype(F32)).astype(BF16)


def _slot_select(width, group):
    i = jnp.arange(width)
    dst = (i // group) * LANES + i % group
    return (dst[:, None] == jnp.arange((width // group) * LANES)[None, :]).astype(BF16)


def _slot_vector(n, lo, hi, value):
    j = jnp.arange(n * LANES) % LANES
    return jnp.where((j >= lo) & (j < hi), value, 0.0).astype(F32).reshape(1, n * LANES)


def _norm_rope(x, gmat, gain, cos, sup, sdn, group, qd):
    w = x.shape[-1]
    sh, sl = _hi_lo(x * x)
    ss = (jnp.dot(sh, gmat, preferred_element_type=F32) + jnp.dot(sl, gmat, preferred_element_type=F32)) * (1.0 / group)
    xn = x * lax.rsqrt(ss + EPS) * gain
    return xn * cos + pltpu.roll(xn, w - qd, 1) * sup + pltpu.roll(xn, qd, 1) * sdn


def _inproj_kernel(x_ref, m_ref, g_ref, wa, wb, wc, wd, wg,
                   ca, ua, da, cq, uq, dq, ck, uk, dk, gma, gmq, gmk, qna, kna, qnc, knc,
                   sela, selv, selq, selk, qvec, kvec, vvec,
                   q1o, q2o, k1o, k2o, vo, zbo, qco, kco, vco, zdo, zgo):
    d = x_ref.shape[-1]
    x = x_ref[0]
    shift = m_ref[0, :, 0:d]
    scale = m_ref[0, :, d:2 * d]
    h = (_rms(x) * g_ref[...] * (1.0 + scale) + shift).astype(BF16)
    place = lambda y, sel: jnp.dot(y.astype(BF16), sel[...], preferred_element_type=F32)

    za = jnp.dot(h, wa[...], preferred_element_type=F32)
    hw = DA_HEADS * DA_DIM
    cos, sup, sdn, gm = ca[...], ua[...], da[...], gma[...]
    qscale = DA_DIM ** -0.5 * LOG2E
    for t, (o, gain, sc, vec) in enumerate(((q1o, qna, qscale, qvec), (q2o, qna, qscale, qvec),
                                            (k1o, kna, 1.0, kvec), (k2o, kna, 1.0, kvec))):
        y = _norm_rope(za[:, t * hw:(t + 1) * hw], gm, gain[...], cos, sup, sdn, DA_DIM, DA_DIM // 4) * sc
        o[0] = (place(y, sela) + vec[...]).astype(BF16)
    vo[0] = (place(za[:, 4 * hw:], selv) + vvec[...]).astype(BF16)

    zbo[0] = jnp.dot(h, wb[...], preferred_element_type=F32).astype(BF16)

    zc = jnp.dot(h, wc[...], preferred_element_type=F32)
    qw = WA_HEADS * WA_DIM
    kw = WA_KV_HEADS * WA_DIM
    y = _norm_rope(zc[:, 0:qw], gmq[...], qnc[...], cq[...], uq[...], dq[...], WA_DIM, WA_DIM // 4) * WA_DIM ** -0.5
    qco[0] = place(y, selq).astype(BF16)
    y = _norm_rope(zc[:, qw:qw + kw], gmk[...], knc[...], ck[...], uk[...], dk[...], WA_DIM, WA_DIM // 4)
    kco[0] = place(y, selk).astype(BF16)
    vco[0] = place(zc[:, qw + kw:], selk).astype(BF16)

    zdo[0] = jnp.dot(h, wd[...], preferred_element_type=F32).astype(BF16)
    for k in range(N_BRANCH):
        zgo[0, :, k * d:(k + 1) * d] = jnp.dot(h, wg[:, k * d:(k + 1) * d], preferred_element_type=F32).astype(BF16)


def _in_proj(x, mod, g, ws, tables, consts):
    bx, lx, d = x.shape
    tm = min(512, lx)
    const = lambda b, i: (0, 0)
    row = lambda b, i: (b, i, 0)
    once = lambda a: pl.BlockSpec(a.shape, const, pipeline_mode=pl.Buffered(1))
    widths = [DA_HEADS * LANES] * 5 + [W_B, WA_HEADS * LANES, WA_KV_HEADS * LANES, WA_KV_HEADS * LANES, W_D, N_BRANCH * d]
    return pl.pallas_call(
        _inproj_kernel,
        grid=(bx, lx // tm),
        in_specs=[pl.BlockSpec((1, tm, d), row), pl.BlockSpec((1, 1, mod.shape[-1]), lambda b, i: (b, 0, 0)),
                  pl.BlockSpec((1, d), const)]
                 + [once(w) for w in ws]
                 + [pl.BlockSpec((tm, t.shape[1]), lambda b, i: (i, 0)) for t in tables]
                 + [once(c) for c in consts],
        out_specs=[pl.BlockSpec((1, tm, w), row) for w in widths],
        out_shape=[jax.ShapeDtypeStruct((bx, lx, w), BF16) for w in widths],
        compiler_params=_cp("parallel", "parallel"),
        name="in_proj",
    )(x, mod, g.reshape(1, d), *ws, *tables, *consts)


def _da_lambda(lam_ref, lam_init):
    lv = lam_ref[...]
    return (jnp.exp(jnp.sum(lv[0:1] * lv[1:2], keepdims=True)) - jnp.exp(jnp.sum(lv[2:3] * lv[3:4], keepdims=True))
            + lam_init)


def _dattn_kernel(lam_ref, sg_ref, q1_ref, q2_ref, *rest, lam_init, online, nsrc, tk_max):
    srcs = [rest[3 * s:3 * s + 3] for s in range(nsrc)]
    o_ref, acc1, acc2 = rest[3 * nsrc:]
    dn = (((1,), (1,)), ((), ()))
    q1 = q1_ref[0]
    q2 = q2_ref[0]
    tq = q1.shape[0]
    acc1[...] = jnp.zeros_like(acc1)
    acc2[...] = jnp.zeros_like(acc2)
    carry = (jnp.full((tq, 1), NEG_INF, F32),) * 2 if online else 0

    for k1_ref, k2_ref, v_ref in srcs:
        tk = min(tk_max, k1_ref.shape[1])

        def body(j, c, k1_ref=k1_ref, k2_ref=k2_ref, v_ref=v_ref, tk=tk):
            rows = pl.ds(pl.multiple_of(j * tk, tk), tk)
            vc = v_ref[0, rows, :]
            s1 = lax.dot_general(q1, k1_ref[0, rows, :], dn, preferred_element_type=F32)
            s2 = lax.dot_general(q2, k2_ref[0, rows, :], dn, preferred_element_type=F32)
            if online:
                m1, m2 = c
                n1 = jnp.maximum(m1, jnp.max(s1, axis=-1, keepdims=True))
                n2 = jnp.maximum(m2, jnp.max(s2, axis=-1, keepdims=True))
                acc1[...] = jnp.exp2(m1 - n1) * acc1[...] + jnp.dot(jnp.exp2(s1 - n1).astype(BF16), vc,
                                                                    preferred_element_type=F32)
                acc2[...] = jnp.exp2(m2 - n2) * acc2[...] + jnp.dot(jnp.exp2(s2 - n2).astype(BF16), vc,
                                                                    preferred_element_type=F32)
                return n1, n2
            acc1[...] += jnp.dot(jnp.exp2(s1).astype(BF16), vc, preferred_element_type=F32)
            acc2[...] += jnp.dot(jnp.exp2(s2).astype(BF16), vc, preferred_element_type=F32)
            return c

        carry = lax.fori_loop(0, k1_ref.shape[1] // tk, body, carry)

    dv = DA_VDIM
    a1 = acc1[...]
    a2 = acc2[...]
    lam = _da_lambda(lam_ref, lam_init)
    o = a1 * (1.0 / a1[:, dv:dv + 1]) - a2 * (lam / a2[:, dv:dv + 1])
    o = jnp.where(lax.broadcasted_iota(jnp.int32, o.shape, 1) < dv, o, 0.0)
    o = o * lax.rsqrt(jnp.sum(o * o, axis=-1, keepdims=True) * (1.0 / dv) + EPS)
    o_ref[0] = (o * (sg_ref[...] * (1.0 - lam_init))).astype(BF16)


def _diff_attention(q1, q2, srcs, fixed, lam_p, subln, lam_init):
    b, lq, _ = q1.shape
    h = DA_HEADS
    const = lambda b_, h_, i: (0, 0)
    sg = jnp.pad(subln, (0, LANES - DA_VDIM)).reshape(1, LANES)
    flat = [a for src in srcs for a in src]

    def call(online, *args):
        tq = min(256 if online else 2048, lq)
        qs = pl.BlockSpec((1, tq, LANES), lambda b_, h_, i: (b_, i, h_))
        return pl.pallas_call(
            functools.partial(_dattn_kernel, lam_init=lam_init, online=online, nsrc=len(srcs),
                              tk_max=256 if online else 512),
            grid=(b, h, lq // tq),
            in_specs=[pl.BlockSpec(lam_p.shape, const), pl.BlockSpec((1, LANES), const), qs, qs]
                     + [pl.BlockSpec((1, a.shape[1], LANES), lambda b_, h_, i: (b_, 0, h_)) for a in flat],
            out_specs=qs,
            out_shape=jax.ShapeDtypeStruct((b, lq, h * LANES), BF16),
            scratch_shapes=[pltpu.VMEM((tq, LANES), F32)] * 2,
            compiler_params=_cp("parallel", "parallel", "arbitrary"),
            name="diff_attention_online" if online else "diff_attention",
        )(*args)

    return lax.cond(fixed, functools.partial(call, False), functools.partial(call, True),
                    lam_p, sg, q1, q2, *flat)


def _hy_prep_kernel(zp_ref, zc_ref, zn_ref, w_ref, b_ref, p_ref, x0_ref, ext):
    i = pl.program_id(1)
    last = pl.num_programs(1) - 1
    tl = zc_ref.shape[1]
    h = 2 * SUBLANES
    ext[0:h] = jnp.where(i == 0, 0.0, zp_ref[0].astype(F32))
    ext[h:h + tl] = zc_ref[0].astype(F32)
    ext[h + tl:] = jnp.where(i == last, 0.0, zn_ref[0].astype(F32))
    w = w_ref[...]
    u = (ext[pl.ds(h - 1, tl), :] * w[0:1] + ext[pl.ds(h, tl), :] * w[1:2] + ext[pl.ds(h + 1, tl), :] * w[2:3]
         + b_ref[...])
    hw = HY_WIDTH
    x0_ref[0] = u[:, 0:hw]
    p_ref[0] = u[:, 2 * hw:3 * hw] * u[:, hw:2 * hw]


def _hy_prep(zb, conv_w, conv_b):
    bx, lx, w = zb.shape
    tl = min(256, lx)
    h = 2 * SUBLANES
    nh = lx // h
    per = tl // h
    out = jax.ShapeDtypeStruct((bx, lx, HY_WIDTH), F32)
    return pl.pallas_call(
        _hy_prep_kernel,
        grid=(bx, lx // tl),
        in_specs=[pl.BlockSpec((1, h, w), lambda b, i: (b, jnp.maximum(i * per - 1, 0), 0)),
                  pl.BlockSpec((1, tl, w), lambda b, i: (b, i, 0)),
                  pl.BlockSpec((1, h, w), lambda b, i: (b, jnp.minimum((i + 1) * per, nh - 1), 0)),
                  pl.BlockSpec(conv_w.shape, lambda b, i: (0, 0)),
                  pl.BlockSpec((1, w), lambda b, i: (0, 0))],
        out_specs=[pl.BlockSpec((1, tl, HY_WIDTH), lambda b, i: (b, i, 0))] * 2,
        out_shape=[out, out],
        scratch_shapes=[pltpu.VMEM((tl + 2 * h, w), F32)],
        compiler_params=_cp("parallel", "parallel"),
        name="hyena_prep",
    )(zb, zb, zb, conv_w, conv_b.reshape(1, w))


def _hy_filter_kernel(feat_ref, w1, b1, w2, b2, w3, b3, w4, fr_ref, dl_ref, filt_ref, ssq_ref, *, s):
    i = pl.program_id(0)
    tr = feat_ref.shape[0]
    feat = feat_ref[...]
    fr = fr_ref[...]
    dot = lambda a, w: jnp.dot(a, w[...], precision=HI, preferred_element_type=F32)
    a = jnp.sin(fr * (dot(feat, w1) + b1[...]))
    a = jnp.sin(fr * (dot(a, w2) + b2[...]))
    a = jnp.sin(fr * (dot(a, w3) + b3[...]))
    coef = dot(a, w4)
    n = i * tr + lax.broadcasted_iota(jnp.int32, (tr, 1), 0)
    window = jnp.exp(-feat[:, 0:1] * dl_ref[...]) + HY_SHIFT
    half = jnp.where(n < s, coef[:, :HY_WIDTH], coef[:, HY_WIDTH:])
    filt = jnp.where(n == s, 0.0, half * window)
    filt_ref[...] = filt

    @pl.when(i == 0)
    def _():
        ssq_ref[...] = jnp.zeros_like(ssq_ref)

    ssq_ref[...] += jnp.sum(filt * filt, axis=0, keepdims=True)


def _hy_feat(s):
    t = jnp.linspace(0.0, 1.0, s, dtype=F32)[:, None]
    w = (2.0 * math.pi / s) * jnp.arange(s, dtype=F32)[:, None]
    bands = jnp.linspace(1e-4, HY_BANDS - 1, HY_BANDS, dtype=F32)[None, :]
    feat = jnp.concatenate([t, jnp.cos(w * bands), jnp.sin(w * bands)], axis=-1)
    feat = jnp.concatenate([feat, feat[:1], feat[:0:-1]], axis=0)
    return jnp.pad(feat, ((0, 0), (0, LANES - feat.shape[1])))


def _hy_filter(s, feat, w1, b1, w2, b2, w3, b3, w4, freq):
    n = 2 * s
    tr = min(512, n)
    deltas = jnp.abs(jnp.linspace(math.log(HY_TARGET) / HY_FAST_DECAY, math.log(HY_TARGET) / HY_SLOW_DECAY,
                                  HY_WIDTH, dtype=F32)).reshape(1, HY_WIDTH)
    w1p = jnp.pad(w1, ((0, LANES - w1.shape[0]), (0, 0)))
    row = lambda a: a.reshape(1, -1)
    args = (feat, w1p, row(b1), w2, row(b2), w3, row(b3), w4, row(freq), deltas)
    const = lambda i: (0, 0)
    return pl.pallas_call(
        functools.partial(_hy_filter_kernel, s=s),
        grid=(n // tr,),
        in_specs=[pl.BlockSpec((tr, LANES), lambda i: (i, 0))] + [pl.BlockSpec(a.shape, const) for a in args[1:]],
        out_specs=[pl.BlockSpec((tr, HY_WIDTH), lambda i: (i, 0)), pl.BlockSpec((1, HY_WIDTH), const)],
        out_shape=[jax.ShapeDtypeStruct((n, HY_WIDTH), F32), jax.ShapeDtypeStruct((1, HY_WIDTH), F32)],
        compiler_params=_cp("arbitrary"),
        name="hyena_filter",
    )(*args)


def _dft_factors(n):
    lg = n.bit_length() - 1
    assert 1 << lg == n
    n1 = 1 << (lg // 2)
    return n1, n // n1


def _dft_tables(n):
    n1, n2 = _dft_factors(n)
    ia = jnp.arange(n1, dtype=jnp.int32)
    ang1 = (2.0 * math.pi / n1) * ((ia[:, None] * ia[None, :]) % n1).astype(F32)
    f1 = jnp.concatenate([jnp.cos(ang1), -jnp.sin(ang1)], axis=0)
    c = jnp.arange(n1, dtype=jnp.int32)[:, None, None]
    d = jnp.arange(n2, dtype=jnp.int32)[None, :, None]
    b = jnp.arange(n2, dtype=jnp.int32)[None, None, :]
    ang = (2.0 * math.pi / n) * ((b * (c + n1 * d)) % n).astype(F32)
    re, im = jnp.cos(ang), -jnp.sin(ang)
    m1 = jnp.concatenate([jnp.concatenate([re, -im], axis=2), jnp.concatenate([im, re], axis=2)], axis=1)
    m2 = jnp.swapaxes(m1, 1, 2)
    ang4 = ang1[: n1 // 2]
    f4 = jnp.concatenate([jnp.cos(ang4), -jnp.sin(ang4)], axis=1) * (1.0 / n)
    return f1, _hi_lo(m1), _hi_lo(m2), _hi_lo(f4)


def _dot3(mh, ml, a):
    ah, al = _hi_lo(a)
    return (jnp.dot(mh, ah, preferred_element_type=F32) + jnp.dot(mh, al, preferred_element_type=F32)
            + jnp.dot(ml, ah, preferred_element_type=F32))


def _hy_stage1_kernel(fh_ref, fl_ref, x_ref, re_ref, im_ref):
    n1 = re_ref.shape[1]
    y = _dot3(fh_ref[...], fl_ref[...], x_ref[0])
    re_ref[0] = y[:n1]
    im_ref[0] = y[n1:]


def _hy_stage1(f, x2d):
    bx, k, lanes = x2d.shape
    n1 = f[0].shape[0] // 2
    tl = min(2048, lanes)
    out = jax.ShapeDtypeStruct((bx, n1, lanes), F32)
    return pl.pallas_call(
        _hy_stage1_kernel,
        grid=(bx, lanes // tl),
        in_specs=[pl.BlockSpec(f[0].shape, lambda b, j: (0, 0))] * 2 + [pl.BlockSpec((1, k, tl), lambda b, j: (b, 0, j))],
        out_specs=[pl.BlockSpec((1, n1, tl), lambda b, j: (b, 0, j))] * 2,
        out_shape=[out, out],
        compiler_params=_cp("parallel", "parallel"),
        name="hyena_dft_stage1",
    )(*f, x2d)


def _hy_mid_kernel(m1h_ref, m1l_ref, m2h_ref, m2l_ref, re_ref, im_ref, hre_ref, him_ref, ore_ref, oim_ref):
    n2 = re_ref.shape[2]
    for c in range(re_ref.shape[1]):
        a = jnp.concatenate([re_ref[0, c], im_ref[0, c]], axis=0)
        x = _dot3(m1h_ref[c], m1l_ref[c], a)
        xre, xim = x[:n2], x[n2:]
        hre, him = hre_ref[c], him_ref[c]
        y = jnp.concatenate([xre * hre - xim * him, xre * him + xim * hre], axis=0)
        bb = _dot3(m2h_ref[c], m2l_ref[c], y)
        ore_ref[0, c] = bb[:n2]
        oim_ref[0, c] = bb[n2:]


def _hy_spec_kernel(m1h_ref, m1l_ref, re_ref, im_ref, rs_ref, ore_ref, oim_ref):
    n2 = re_ref.shape[2]
    for c in range(re_ref.shape[1]):
        a = jnp.concatenate([re_ref[0, c], im_ref[0, c]], axis=0)
        x = _dot3(m1h_ref[c], m1l_ref[c], a) * rs_ref[...]
        ore_ref[c] = x[:n2]
        oim_ref[c] = x[n2:]


def _hy_filter_spectrum(m1, are, aim, rs):
    _, n1, n2, w = are.shape
    cb = min(4, n1)
    blk = pl.BlockSpec((1, cb, n2, w), lambda c: (0, c, 0, 0))
    mblk = pl.BlockSpec((cb, 2 * n2, 2 * n2), lambda c: (c, 0, 0))
    oblk = pl.BlockSpec((cb, n2, w), lambda c: (c, 0, 0))
    out = jax.ShapeDtypeStruct((n1, n2, w), F32)
    return pl.pallas_call(
        _hy_spec_kernel,
        grid=(n1 // cb,),
        in_specs=[mblk, mblk, blk, blk, pl.BlockSpec((1, w), lambda c: (0, 0))],
        out_specs=[oblk, oblk],
        out_shape=[out, out],
        compiler_params=_cp("parallel"),
        name="hyena_filter_spectrum",
    )(*m1, are, aim, rs)


def _hy_mid(m1, m2, are, aim, hre, him):
    bx, n1, n2, w = are.shape
    cb = min(4, n1)
    blk = pl.BlockSpec((1, cb, n2, w), lambda c, b: (b, c, 0, 0))
    mblk = pl.BlockSpec((cb, 2 * n2, 2 * n2), lambda c, b: (c, 0, 0))
    hblk = pl.BlockSpec((cb, n2, w), lambda c, b: (c, 0, 0))
    out = jax.ShapeDtypeStruct((bx, n1, n2, w), F32)
    return pl.pallas_call(
        _hy_mid_kernel,
        grid=(n1 // cb, bx),
        in_specs=[mblk, mblk, mblk, mblk, blk, blk, hblk, hblk],
        out_specs=[blk, blk],
        out_shape=[out, out],
        compiler_params=_cp("parallel", "parallel"),
        name="hyena_dft_mid",
    )(*m1, *m2, are, aim, hre, him)


def _hy_last_kernel(fh_ref, fl_ref, re_ref, im_ref, p_ref, x0_ref, bias_ref, o_ref):
    bb = jnp.concatenate([re_ref[0], im_ref[0]], axis=0)
    y = _dot3(fh_ref[...], fl_ref[...], bb)
    o_ref[0] = (x0_ref[0] * (y + p_ref[0] * bias_ref[...])).astype(BF16)


def _hy_last(f4, bre, bim, p2d, x02d, bias2d):
    bx, n1, lanes = bre.shape
    k = n1 // 2
    tl = min(2048, lanes)
    big = pl.BlockSpec((1, n1, tl), lambda b, j: (b, 0, j))
    small = pl.BlockSpec((1, k, tl), lambda b, j: (b, 0, j))
    return pl.pallas_call(
        _hy_last_kernel,
        grid=(bx, lanes // tl),
        in_specs=[pl.BlockSpec(f4[0].shape, lambda b, j: (0, 0))] * 2 + [big, big, small, small,
                  pl.BlockSpec((1, tl), lambda b, j: (0, j))],
        out_specs=small,
        out_shape=jax.ShapeDtypeStruct((bx, k, lanes), BF16),
        compiler_params=_cp("parallel", "parallel"),
        name="hyena_dft_last",
    )(*f4, bre, bim, p2d, x02d, bias2d)


def _hyena(zb, conv_w, conv_b, filt_params, hy_bias, consts):
    bx, lx, _ = zb.shape
    feat, (f1, m1, m2, f4) = consts
    n = 2 * lx
    n1, n2 = _dft_factors(n)
    w = HY_WIDTH
    filt, ssq = _hy_filter(lx, feat, *filt_params)
    rs = lax.rsqrt(ssq + EPS)
    fre, fim = _hy_stage1(_hi_lo(f1), filt.reshape(1, n1, n2 * w))
    hre, him = _hy_filter_spectrum(m1, fre.reshape(1, n1, n2, w), fim.reshape(1, n1, n2, w), rs)
    p, x0 = _hy_prep(zb, conv_w, conv_b)
    k = n1 // 2
    are, aim = _hy_stage1(_hi_lo(f1[:, :k]), p.reshape(bx, k, n2 * w))
    bre, bim = _hy_mid(m1, m2, are.reshape(bx, n1, n2, w), aim.reshape(bx, n1, n2, w), hre, him)
    bias2d = jnp.tile(hy_bias, n2).reshape(1, n2 * w)
    y = _hy_last(f4, bre.reshape(bx, n1, n2 * w), bim.reshape(bx, n1, n2 * w), p.reshape(bx, k, n2 * w),
                 x0.reshape(bx, k, n2 * w), bias2d)
    return y.reshape(bx, lx, w)


def _wattn_kernel(sink_ref, q_ref, *rest, banded):
    if banded:
        bias_ref, kp_ref, kc_ref, kn_ref, kx_ref, vp_ref, vc_ref, vn_ref, vx_ref, o_ref = rest
    else:
        kx_ref, vx_ref, o_ref = rest
    h = pl.program_id(1)
    i = pl.program_id(2)
    last = pl.num_programs(2) - 1
    qb = q_ref.shape[1]
    q = jnp.concatenate([q_ref[0, :, :LANES], q_ref[0, :, LANES:]], axis=0)
    dn = (((1,), (1,)), ((), ()))
    parts = []
    if banded:
        kk = jnp.concatenate([kp_ref[0], kc_ref[0], kn_ref[0]], axis=0)
        vv = jnp.concatenate([vp_ref[0], vc_ref[0], vn_ref[0]], axis=0)
        s = lax.dot_general(q, kk, dn, preferred_element_type=F32)
        c = lax.broadcasted_iota(jnp.int32, (1, qb + 2 * BLOCK), 1) - BLOCK
        edge = jnp.where(((c < 0) & (i == 0)) | ((c >= qb) & (i == last)), NEG_INF, 0.0)
        parts.append((s + bias_ref[...] + edge, vv))
    parts.append((lax.dot_general(q, kx_ref[0], dn, preferred_element_type=F32), vx_ref[0]))
    top = lax.broadcasted_iota(jnp.int32, (2 * qb, 1), 0) < qb
    snk = jnp.where(top, sink_ref[h * WA_GROUP], sink_ref[h * WA_GROUP + 1])
    m = snk
    for s, _ in parts:
        m = jnp.maximum(m, jnp.max(s, axis=-1, keepdims=True))
    den = jnp.exp(snk - m)
    acc = jnp.zeros((2 * qb, LANES), F32)
    for s, v in parts:
        p = jnp.exp(s - m)
        den = den + jnp.sum(p, axis=-1, keepdims=True)
        acc = acc + jnp.dot(p.astype(BF16), v, preferred_element_type=F32)
    o = (acc / den).astype(BF16)
    o_ref[0] = jnp.concatenate([o[:qb], o[qb:]], axis=1)


def _window_attention(q, k, v, kx, vx, sink, banded):
    b, lq, _ = q.shape
    cx = kx.shape[1]
    qb = min(512, lq)
    per = qb // BLOCK
    nblk = lq // BLOCK
    side = lambda f: pl.BlockSpec((1, BLOCK, LANES), f)
    prev = side(lambda b_, h, i: (b_, jnp.maximum(i * per - 1, 0), h))
    nxt = side(lambda b_, h, i: (b_, jnp.minimum((i + 1) * per, nblk - 1), h))
    cur = pl.BlockSpec((1, qb, LANES), lambda b_, h, i: (b_, i, h))
    ctx = pl.BlockSpec((1, cx, LANES), lambda b_, h, i: (b_, 0, h))
    qspec = pl.BlockSpec((1, qb, WA_GROUP * LANES), lambda b_, h, i: (b_, i, h))
    if banded:
        r = jnp.arange(WA_GROUP * qb)[:, None] % qb
        c = jnp.arange(qb + 2 * BLOCK)[None, :] - BLOCK
        bias = jnp.where(jnp.abs(r - c) <= BLOCK, 0.0, NEG_INF).astype(F32)
        specs = [pl.BlockSpec(bias.shape, lambda b_, h, i: (0, 0)), prev, cur, nxt, ctx, prev, cur, nxt, ctx]
        args = (bias, k, k, k, kx, v, v, v, vx)
    else:
        specs, args = [ctx, ctx], (kx, vx)
    return pl.pallas_call(
        functools.partial(_wattn_kernel, banded=banded),
        grid=(b, WA_KV_HEADS, lq // qb),
        in_specs=[pl.BlockSpec(memory_space=pltpu.SMEM), qspec] + specs,
        out_specs=qspec,
        out_shape=jax.ShapeDtypeStruct((b, lq, WA_HEADS * LANES), BF16),
        compiler_params=_cp("parallel", "parallel", "arbitrary"),
        name="window_attention",
    )(sink, q, *args)


def _conf_kernel(zp_ref, zc_ref, zn_ref, w_ref, b_ref, lg_ref, lb_ref, o_ref, ext, *, halo):
    i = pl.program_id(1)
    last = pl.num_programs(1) - 1
    tl = zc_ref.shape[1]
    cw = CF_WIDTH

    def glu(z_ref):
        z = z_ref[0].astype(F32)
        return z[:, :cw] * jax.nn.sigmoid(z[:, cw:])

    ext[0:halo] = jnp.where(i == 0, 0.0, glu(zp_ref))
    ext[halo:halo + tl] = glu(zc_ref)
    ext[halo + tl:] = jnp.where(i == last, 0.0, glu(zn_ref))
    w = w_ref[...]
    u = jnp.zeros((tl, cw), F32) + b_ref[...]
    for j in range(CF_TAPS):
        u = u + ext[pl.ds(halo - CF_TAPS // 2 + j, tl), :] * w[j:j + 1]
    uc = u - jnp.mean(u, axis=-1, keepdims=True)
    y = uc * lax.rsqrt(jnp.mean(uc * uc, axis=-1, keepdims=True) + EPS) * lg_ref[...] + lb_ref[...]
    o_ref[0] = _silu(y).astype(BF16)


def _conformer(zd, dw_w, dw_b, ln_g, ln_b):
    bx, lx, w = zd.shape
    tl = min(256, lx)
    halo = 2 * SUBLANES
    nh = lx // halo
    per = tl // halo
    row = lambda a: a.reshape(1, -1)
    const = lambda b, i: (0, 0)
    return pl.pallas_call(
        functools.partial(_conf_kernel, halo=halo),
        grid=(bx, lx // tl),
        in_specs=[pl.BlockSpec((1, halo, w), lambda b, i: (b, jnp.maximum(i * per - 1, 0), 0)),
                  pl.BlockSpec((1, tl, w), lambda b, i: (b, i, 0)),
                  pl.BlockSpec((1, halo, w), lambda b, i: (b, jnp.minimum((i + 1) * per, nh - 1), 0)),
                  pl.BlockSpec(dw_w.shape, const)] + [pl.BlockSpec((1, CF_WIDTH), const)] * 3,
        out_specs=pl.BlockSpec((1, tl, CF_WIDTH), lambda b, i: (b, i, 0)),
        out_shape=jax.ShapeDtypeStruct((bx, lx, CF_WIDTH), BF16),
        scratch_shapes=[pltpu.VMEM((tl + 2 * halo, CF_WIDTH), F32)],
        compiler_params=_cp("parallel", "parallel"),
        name="conformer_conv",
    )(zd, zd, zd, dw_w, row(dw_b), row(ln_g), row(ln_b))


def _merge_kernel(x_ref, m_ref, ya, yb, yc, yd, zg_ref, bg_ref, wa, wb, wc, wd, wo_ref, o_ref, *, goff):
    d = x_ref.shape[-1]
    acc = jnp.zeros(x_ref.shape[1:], F32)
    for i, (y, w) in enumerate(((ya, wa), (yb, wb), (yc, wc), (yd, wd))):
        gate = jax.nn.sigmoid(zg_ref[0, :, i * d:(i + 1) * d].astype(F32) + bg_ref[:, i * d:(i + 1) * d])
        acc = acc + gate * jnp.dot(y[0], w[...], preferred_element_type=F32)
    out = jnp.dot(acc.astype(BF16), wo_ref[...], preferred_element_type=F32)
    o_ref[0] = x_ref[0] + m_ref[0, :, goff:goff + d] * out


def _merge(x, mod, ys, zg, b_gate, wbs, w_out):
    bx, lx, d = x.shape
    tm = min(256, lx)
    row = lambda b, i: (b, i, 0)
    const = lambda b, i: (0, 0)
    return pl.pallas_call(
        functools.partial(_merge_kernel, goff=2 * d),
        grid=(bx, lx // tm),
        in_specs=[pl.BlockSpec((1, tm, d), row), pl.BlockSpec((1, 1, mod.shape[-1]), lambda b, i: (b, 0, 0))]
                 + [pl.BlockSpec((1, tm, y.shape[-1]), row) for y in ys]
                 + [pl.BlockSpec((1, tm, N_BRANCH * d), row), pl.BlockSpec((1, N_BRANCH * d), const)]
                 + [pl.BlockSpec(w.shape, const) for w in wbs] + [pl.BlockSpec(w_out.shape, const)],
        out_specs=pl.BlockSpec((1, tm, d), row),
        out_shape=jax.ShapeDtypeStruct(x.shape, F32),
        input_output_aliases={0: 0},
        compiler_params=_cp("parallel", "parallel"),
        name="merge",
    )(x, mod, *ys, zg, b_gate.reshape(1, -1), *wbs, w_out)


def _slot_rows(w, group):
    n = w.shape[0] // group
    return jnp.pad(w.reshape(n, group, -1), ((0, 0), (0, LANES - group), (0, 0))).reshape(n * LANES, -1)


def _moe_kernel(x_ref, m_ref, g_ref, wr_ref, br_ref, w1_ref, w3_ref, w2_ref, o_ref, h_sc, gate_sc, acc_sc):
    e = pl.program_id(2)
    d = x_ref.shape[-1]
    tm = x_ref.shape[1]
    lane = lax.broadcasted_iota(jnp.int32, (tm, LANES), 1).astype(F32)

    @pl.when(e == 0)
    def _():
        h = _rms(x_ref[0]) * g_ref[...] * (1.0 + m_ref[0, :, 4 * d:5 * d]) + m_ref[0, :, 3 * d:4 * d]
        h_sc[...] = h.astype(BF16)
        lg = jnp.dot(h, wr_ref[...], precision=HI, preferred_element_type=F32) + br_ref[...]
        isg = lane < N_GROUPS
        gmax = jnp.max(jnp.where(isg, lg, NEG_INF), axis=-1, keepdims=True)
        gi = jnp.min(jnp.where(isg & (lg == gmax), lane, LANES), axis=-1, keepdims=True)
        gw = 1.0 / jnp.sum(jnp.where(isg, jnp.exp(lg - gmax), 0.0), axis=-1, keepdims=True)
        lo = N_GROUPS + gi * EXP_PER_GROUP
        ise = (lane >= lo) & (lane < lo + EXP_PER_GROUP)
        le = jnp.where(ise, lg, NEG_INF)
        m1 = jnp.max(le, axis=-1, keepdims=True)
        i1 = jnp.min(jnp.where(ise & (le == m1), lane, LANES), axis=-1, keepdims=True)
        ise2 = ise & (lane != i1)
        le2 = jnp.where(ise2, lg, NEG_INF)
        m2 = jnp.max(le2, axis=-1, keepdims=True)
        i2 = jnp.min(jnp.where(ise2 & (le2 == m2), lane, LANES), axis=-1, keepdims=True)
        r = jnp.exp(m2 - m1)
        wa = gw / (1.0 + r)
        gate_sc[...] = jnp.where(lane == i1, wa, 0.0) + jnp.where(lane == i2, wa * r, 0.0)
        acc_sc[...] = jnp.zeros_like(acc_sc)

    hb = h_sc[...]
    u = _silu(jnp.dot(hb, w1_ref[0], preferred_element_type=F32)) * jnp.dot(hb, w3_ref[0], preferred_element_type=F32)
    ge = jnp.sum(jnp.where(lane == (e + N_GROUPS).astype(F32), gate_sc[...], 0.0), axis=-1, keepdims=True)
    acc_sc[...] += ge * jnp.dot(u.astype(BF16), w2_ref[0], preferred_element_type=F32)

    @pl.when(e == pl.num_programs(2) - 1)
    def _():
        o_ref[0] = x_ref[0] + m_ref[0, :, 5 * d:6 * d] * acc_sc[...]


def _moe(x, mod, g, w_router, b_router, w1, w3, w2):
    bx, lx, d = x.shape
    tm = min(1024, lx)
    ne, _, f = w1.shape
    row = lambda b, i, e: (b, i, 0)
    const = lambda b, i, e: (0, 0)
    return pl.pallas_call(
        _moe_kernel,
        grid=(bx, lx // tm, ne),
        in_specs=[pl.BlockSpec((1, tm, d), row), pl.BlockSpec((1, 1, mod.shape[-1]), lambda b, i, e: (b, 0, 0)),
                  pl.BlockSpec((1, d), const), pl.BlockSpec((d, LANES), const), pl.BlockSpec((1, LANES), const),
                  pl.BlockSpec((1, d, f), lambda b, i, e: (e, 0, 0)), pl.BlockSpec((1, d, f), lambda b, i, e: (e, 0, 0)),
                  pl.BlockSpec((1, f, d), lambda b, i, e: (e, 0, 0))],
        out_specs=pl.BlockSpec((1, tm, d), row),
        out_shape=jax.ShapeDtypeStruct(x.shape, F32),
        scratch_shapes=[pltpu.VMEM((tm, d), BF16), pltpu.VMEM((tm, LANES), F32), pltpu.VMEM((tm, d), F32)],
        input_output_aliases={0: 0},
        compiler_params=_cp("parallel", "parallel", "arbitrary"),
        name="moe",
    )(x, mod, g.reshape(1, d), w_router, b_router, w1, w3, w2)


def kernel(x, c, ctx, c_ctx, w_mod, b_mod, norm1_g, norm2_g, w_in, b_gate, da_qn, da_kn, da_lam, da_subln, hy_conv_w, hy_conv_b, hf_w1, hf_b1, hf_w2, hf_b2, hf_w3, hf_b3, hf_w4, hf_freq, hy_bias, wa_qn, wa_kn, wa_sink, cf_dw_w, cf_dw_b, cf_ln_g, cf_ln_b, w_branch, w_out, w_rg, b_rg, w_re, b_re, w1, w3, w2):
    b, s, d = x.shape
    cl = ctx.shape[1]
    depth = w_mod.shape[0]
    assert s % 256 == 0 and cl % 256 == 0 and s % GRID_W == 0

    nrow = -(-(b + 1) // SUBLANES) * SUBLANES
    crows = jnp.zeros((nrow, d), F32).at[:b].set(c).at[b].set(c_ctx)
    mods = _mod_vectors(crows, w_mod, b_mod)

    aw, qw, kw = DA_HEADS * DA_DIM, WA_HEADS * WA_DIM, WA_KV_HEADS * WA_DIM
    tab_lat = (*_rope_tables(s, DA_DIM, DA_HEADS), *_rope_tables(s, WA_DIM, WA_HEADS), *_rope_tables(s, WA_DIM, WA_KV_HEADS))
    tab_ctx = (*_unit_tables(cl, aw), *_unit_tables(cl, qw), *_unit_tables(cl, kw))
    hy_lat = (_hy_feat(s), _dft_tables(2 * s))
    hy_ctx = (_hy_feat(cl), _dft_tables(2 * cl))
    gms = (_group_ones(aw, DA_DIM), _group_ones(qw, WA_DIM), _group_ones(kw, WA_DIM))
    sels = (_slot_select(aw, DA_DIM), _slot_select(DA_HEADS * DA_VDIM, DA_VDIM), _slot_select(qw, WA_DIM),
            _slot_select(kw, WA_DIM))
    qvec = _slot_vector(DA_HEADS, DA_DIM, DA_DIM + 1, 1.0)
    vvec = _slot_vector(DA_HEADS, DA_VDIM, DA_VDIM + DA_ONES, 1.0)

    xc = ctx
    for l in range(depth):
        last = l == depth - 1
        lam_init = 0.8 - 0.6 * math.exp(-0.3 * l)
        mod_x = mods[l, :b][:, None, :]
        mod_c = jnp.broadcast_to(mods[l, b][None, None, :], (b, 1, mods.shape[-1]))
        wl = w_in[l]
        ws = [wl[:, 0:OFF_B].astype(BF16), wl[:, OFF_B:OFF_C].astype(BF16), wl[:, OFF_C:OFF_D].astype(BF16),
              wl[:, OFF_D:OFF_G].astype(BF16), wl[:, OFF_G:].astype(BF16)]
        shift = 1.02 * LOG2E * DA_DIM ** 0.5 * jnp.max(jnp.abs(da_qn[l])) * jnp.max(jnp.abs(da_kn[l]))
        fixed = shift <= DA_SHIFT_MAX
        kvec = _slot_vector(DA_HEADS, DA_DIM, DA_DIM + 1, jnp.where(fixed, -shift, 0.0))
        tile = lambda a, n: jnp.tile(a, n).reshape(1, -1)
        consts = (*gms, tile(da_qn[l], DA_HEADS), tile(da_kn[l], DA_HEADS), tile(wa_qn[l], WA_HEADS),
                  tile(wa_kn[l], WA_KV_HEADS), *sels, qvec, kvec, vvec)
        q1, q2, k1, k2, v, zb, qc, kc, vc, zd, zg = _in_proj(x, mod_x, norm1_g[l], ws, tab_lat, consts)
        q1x, q2x, k1x, k2x, vx, zbx, qcx, kcx, vcx, zdx, zgx = _in_proj(xc, mod_c, norm1_g[l], ws, tab_ctx, consts)
        filt_params = (hf_w1[l], hf_b1[l], hf_w2[l], hf_b2[l], hf_w3[l], hf_b3[l], hf_w4[l], hf_freq[l])
        wb4 = w_branch[l].astype(BF16)
        wbs = (_slot_rows(wb4[0], DA_VDIM), wb4[1], _slot_rows(wb4[2], WA_DIM), wb4[3])
        wo = w_out[l].astype(BF16)

        ya = _diff_attention(q1, q2, [(k1, k2, v), (k1x, k2x, vx)], fixed, da_lam[l], da_subln[l], lam_init)
        yb = _hyena(zb, hy_conv_w[l], hy_conv_b[l], filt_params, hy_bias[l], hy_lat)
        yc_ = _window_attention(qc, kc, vc, kcx, vcx, wa_sink[l], True)
        yd = _conformer(zd, cf_dw_w[l], cf_dw_b[l], cf_ln_g[l], cf_ln_b[l])
        x = _merge(x, mod_x, (ya, yb, yc_, yd), zg, b_gate[l], wbs, wo)

        w_router = jnp.pad(jnp.concatenate([w_rg[l], w_re[l]], axis=1), ((0, 0), (0, LANES - N_GROUPS - N_EXPERTS)))
        b_router = jnp.pad(jnp.concatenate([b_rg[l], b_re[l]]), (0, LANES - N_GROUPS - N_EXPERTS)).reshape(1, LANES)
        ew = (w1[l].astype(BF16), w3[l].astype(BF16), w2[l].astype(BF16))

        if not last:
            yca = _diff_attention(q1x, q2x, [(k1x, k2x, vx)], fixed, da_lam[l], da_subln[l], lam_init)
            ycb = _hyena(zbx, hy_conv_w[l], hy_conv_b[l], filt_params, hy_bias[l], hy_ctx)
            ycc = _window_attention(qcx, kcx, vcx, kcx, vcx, wa_sink[l], False)
            ycd = _conformer(zdx, cf_dw_w[l], cf_dw_b[l], cf_ln_g[l], cf_ln_b[l])
            xc = _merge(xc, mod_c, (yca, ycb, ycc, ycd), zgx, b_gate[l], wbs, wo)
            xc = _moe(xc.reshape(1, b * cl, d), mod_c[:1], norm2_g[l], w_router, b_router, *ew).reshape(b, cl, d)
        x = _moe(x, mod_x, norm2_g[l], w_router, b_router, *ew)
    return x
```

```python
import functools
import math

import jax
import jax.numpy as jnp
from jax import lax
from jax.experimental import pallas as pl
from jax.experimental.pallas import tpu as pltpu
from jax.experimental.pallas import tpu_sc as plsc

F32 = jnp.float32
BF16 = jnp.bfloat16
HI = lax.Precision.HIGHEST

GRID_W = 64
BLOCK = 128
ROPE_BASE = 10000.0
EPS = 1e-6
NEG_INF = -1e30

DA_HEADS = 4
DA_DIM = 32
DA_VDIM = 64
HY_WIDTH = 256
HY_BANDS = 16
HY_FF = 64
HY_SHIFT = 0.05
HY_FAST_DECAY = 0.3
HY_SLOW_DECAY = 1.5
HY_TARGET = 1e-2
WA_HEADS = 4
WA_KV_HEADS = 2
WA_GROUP = 2
WA_DIM = 64
CF_WIDTH = 256
CF_TAPS = 31
N_BRANCH = 4
BRANCH_W = 256
N_GROUPS = 4
EXP_PER_GROUP = 4
N_EXPERTS = 16

W_A = 4 * DA_HEADS * DA_DIM + DA_HEADS * DA_VDIM
W_B = 3 * HY_WIDTH
W_C = (WA_HEADS + 2 * WA_KV_HEADS) * WA_DIM
W_D = 2 * CF_WIDTH
OFF_B = W_A
OFF_C = OFF_B + W_B
OFF_D = OFF_C + W_C
OFF_G = OFF_D + W_D

LOG2E = math.log2(math.e)
LANES = 128
SUBLANES = 8
VMEM_LIMIT = 56 * 1024 * 1024

DA_ONES = 16
DA_SHIFT_MAX = 50.0


def _cp(*sem):
    return pltpu.CompilerParams(dimension_semantics=sem, vmem_limit_bytes=VMEM_LIMIT)


def _rms(xf):
    return xf * lax.rsqrt(jnp.mean(xf * xf, axis=-1, keepdims=True) + EPS)


def _silu(x):
    return x * jax.nn.sigmoid(x)


def _mod_kernel(c_ref, w_ref, b_ref, o_ref):
    s = _silu(c_ref[...])
    o_ref[0] = jnp.dot(s, w_ref[0], precision=HI, preferred_element_type=F32) + b_ref[0]


def _mod_vectors(crows, w_mod, b_mod):
    depth, d, n = w_mod.shape
    r = crows.shape[0]
    tn = 1536
    return pl.pallas_call(
        _mod_kernel,
        grid=(depth, n // tn),
        in_specs=[pl.BlockSpec((r, d), lambda l, j: (0, 0)),
                  pl.BlockSpec((1, d, tn), lambda l, j: (l, 0, j)),
                  pl.BlockSpec((1, 1, tn), lambda l, j: (l, 0, j))],
        out_specs=pl.BlockSpec((1, r, tn), lambda l, j: (l, 0, j)),
        out_shape=jax.ShapeDtypeStruct((depth, r, n), F32),
        compiler_params=_cp("arbitrary", "arbitrary"),
        name="mod_vectors",
    )(crows, w_mod, b_mod.reshape(depth, 1, n))


def _rope_tables(s, d, reps):
    rows = s // GRID_W
    row = jnp.repeat(jnp.arange(rows, dtype=F32), GRID_W)
    col = jnp.tile(jnp.arange(GRID_W, dtype=F32), rows)
    qd = d // 4
    inv = ROPE_BASE ** (-jnp.arange(qd, dtype=F32) / qd)
    ar = row[:, None] * inv[None, :]
    ac = col[:, None] * inv[None, :]
    z = jnp.zeros_like(ar)
    cos = jnp.concatenate([jnp.cos(ar), jnp.cos(ar), jnp.cos(ac), jnp.cos(ac)], axis=-1)
    sin_up = jnp.concatenate([-jnp.sin(ar), z, -jnp.sin(ac), z], axis=-1)
    sin_dn = jnp.concatenate([z, jnp.sin(ar), z, jnp.sin(ac)], axis=-1)
    t = lambda a: jnp.tile(a, (1, reps))
    return t(cos), t(sin_up), t(sin_dn)


def _unit_tables(s, w):
    return jnp.ones((s, w), F32), jnp.zeros((s, w), F32), jnp.zeros((s, w), F32)


def _group_ones(width, group):
    i = jnp.arange(width) // group
    return (i[:, None] == i[None, :]).astype(BF16)


def _hi_lo(a):
    hi = a.astype(BF16)
    return hi, (a - hi.astype(F32)).astype(BF16)


def _slot_select(width, group):
    i = jnp.arange(width)
    dst = (i // group) * LANES + i % group
    return (dst[:, None] == jnp.arange((width // group) * LANES)[None, :]).astype(BF16)


def _slot_vector(n, lo, hi, value):
    j = jnp.arange(n * LANES) % LANES
    return jnp.where((j >= lo) & (j < hi), value, 0.0).astype(F32).reshape(1, n * LANES)


def _norm_rope(x, gmat, gain, cos, sup, sdn, group, qd):
    w = x.shape[-1]
    sh, sl = _hi_lo(x * x)
    ss = (jnp.dot(sh, gmat, preferred_element_type=F32) + jnp.dot(sl, gmat, preferred_element_type=F32)) * (1.0 / group)
    xn = x * lax.rsqrt(ss + EPS) * gain
    return xn * cos + pltpu.roll(xn, w - qd, 1) * sup + pltpu.roll(xn, qd, 1) * sdn


def _inproj_kernel(x_ref, m_ref, g_ref, wa, wb, wc, wd, wg,
                   ca, ua, da, cq, uq, dq, ck, uk, dk, gma, gmq, gmk, qna, kna, qnc, knc,
                   sela, selv, selq, selk, qvec, kvec, vvec,
                   q1o, q2o, k1o, k2o, vo, zbo, qco, kco, vco, zdo, zgo):
    d = x_ref.shape[-1]
    x = x_ref[0]
    shift = m_ref[0, :, 0:d]
    scale = m_ref[0, :, d:2 * d]
    h = (_rms(x) * g_ref[...] * (1.0 + scale) + shift).astype(BF16)
    place = lambda y, sel: jnp.dot(y.astype(BF16), sel[...], preferred_element_type=F32)

    za = jnp.dot(h, wa[...], preferred_element_type=F32)
    hw = DA_HEADS * DA_DIM
    cos, sup, sdn, gm = ca[...], ua[...], da[...], gma[...]
    qscale = DA_DIM ** -0.5 * LOG2E
    for t, (o, gain, sc, vec) in enumerate(((q1o, qna, qscale, qvec), (q2o, qna, qscale, qvec),
                                            (k1o, kna, 1.0, kvec), (k2o, kna, 1.0, kvec))):
        y = _norm_rope(za[:, t * hw:(t + 1) * hw], gm, gain[...], cos, sup, sdn, DA_DIM, DA_DIM // 4) * sc
        o[0] = (place(y, sela) + vec[...]).astype(BF16)
    vo[0] = (place(za[:, 4 * hw:], selv) + vvec[...]).astype(BF16)

    zbo[0] = jnp.dot(h, wb[...], preferred_element_type=F32).astype(BF16)

    zc = jnp.dot(h, wc[...], preferred_element_type=F32)
    qw = WA_HEADS * WA_DIM
    kw = WA_KV_HEADS * WA_DIM
    y = _norm_rope(zc[:, 0:qw], gmq[...], qnc[...], cq[...], uq[...], dq[...], WA_DIM, WA_DIM // 4) * WA_DIM ** -0.5
    qco[0] = place(y, selq).astype(BF16)
    y = _norm_rope(zc[:, qw:qw + kw], gmk[...], knc[...], ck[...], uk[...], dk[...], WA_DIM, WA_DIM // 4)
    kco[0] = place(y, selk).astype(BF16)
    vco[0] = place(zc[:, qw + kw:], selk).astype(BF16)

    zdo[0] = jnp.dot(h, wd[...], preferred_element_type=F32).astype(BF16)
    for k in range(N_BRANCH):
        zgo[0, :, k * d:(k + 1) * d] = jnp.dot(h, wg[:, k * d:(k + 1) * d], preferred_element_type=F32).astype(BF16)


def _in_proj(x, mod, g, ws, tables, consts):
    bx, lx, d = x.shape
    tm = min(512, lx)
    const = lambda b, i: (0, 0)
    row = lambda b, i: (b, i, 0)
    once = lambda a: pl.BlockSpec(a.shape, const, pipeline_mode=pl.Buffered(1))
    widths = [DA_HEADS * LANES] * 5 + [W_B, WA_HEADS * LANES, WA_KV_HEADS * LANES, WA_KV_HEADS * LANES, W_D, N_BRANCH * d]
    return pl.pallas_call(
        _inproj_kernel,
        grid=(bx, lx // tm),
        in_specs=[pl.BlockSpec((1, tm, d), row), pl.BlockSpec((1, 1, mod.shape[-1]), lambda b, i: (b, 0, 0)),
                  pl.BlockSpec((1, d), const)]
                 + [once(w) for w in ws]
                 + [pl.BlockSpec((tm, t.shape[1]), lambda b, i: (i, 0)) for t in tables]
                 + [once(c) for c in consts],
        out_specs=[pl.BlockSpec((1, tm, w), row) for w in widths],
        out_shape=[jax.ShapeDtypeStruct((bx, lx, w), BF16) for w in widths],
        compiler_params=_cp("parallel", "parallel"),
        name="in_proj",
    )(x, mod, g.reshape(1, d), *ws, *tables, *consts)


def _da_lambda(lam_ref, lam_init):
    lv = lam_ref[...]
    return (jnp.exp(jnp.sum(lv[0:1] * lv[1:2], keepdims=True)) - jnp.exp(jnp.sum(lv[2:3] * lv[3:4], keepdims=True))
            + lam_init)


def _dattn_kernel(lam_ref, sg_ref, q1_ref, q2_ref, *rest, lam_init, online, nsrc, tk_max):
    srcs = [rest[3 * s:3 * s + 3] for s in range(nsrc)]
    o_ref, acc1, acc2 = rest[3 * nsrc:]
    dn = (((1,), (1,)), ((), ()))
    q1 = q1_ref[0]
    q2 = q2_ref[0]
    tq = q1.shape[0]
    acc1[...] = jnp.zeros_like(acc1)
    acc2[...] = jnp.zeros_like(acc2)
    carry = (jnp.full((tq, 1), NEG_INF, F32),) * 2 if online else 0

    for k1_ref, k2_ref, v_ref in srcs:
        tk = min(tk_max, k1_ref.shape[1])

        def body(j, c, k1_ref=k1_ref, k2_ref=k2_ref, v_ref=v_ref, tk=tk):
            rows = pl.ds(pl.multiple_of(j * tk, tk), tk)
            vc = v_ref[0, rows, :]
            s1 = lax.dot_general(q1, k1_ref[0, rows, :], dn, preferred_element_type=F32)
            s2 = lax.dot_general(q2, k2_ref[0, rows, :], dn, preferred_element_type=F32)
            if online:
                m1, m2 = c
                n1 = jnp.maximum(m1, jnp.max(s1, axis=-1, keepdims=True))
                n2 = jnp.maximum(m2, jnp.max(s2, axis=-1, keepdims=True))
                acc1[...] = jnp.exp2(m1 - n1) * acc1[...] + jnp.dot(jnp.exp2(s1 - n1).astype(BF16), vc,
                                                                    preferred_element_type=F32)
                acc2[...] = jnp.exp2(m2 - n2) * acc2[...] + jnp.dot(jnp.exp2(s2 - n2).astype(BF16), vc,
                                                                    preferred_element_type=F32)
                return n1, n2
            acc1[...] += jnp.dot(jnp.exp2(s1).astype(BF16), vc, preferred_element_type=F32)
            acc2[...] += jnp.dot(jnp.exp2(s2).astype(BF16), vc, preferred_element_type=F32)
            return c

        carry = lax.fori_loop(0, k1_ref.shape[1] // tk, body, carry)

    dv = DA_VDIM
    a1 = acc1[...]
    a2 = acc2[...]
    lam = _da_lambda(lam_ref, lam_init)
    o = a1 * (1.0 / a1[:, dv:dv + 1]) - a2 * (lam / a2[:, dv:dv + 1])
    o = jnp.where(lax.broadcasted_iota(jnp.int32, o.shape, 1) < dv, o, 0.0)
    o = o * lax.rsqrt(jnp.sum(o * o, axis=-1, keepdims=True) * (1.0 / dv) + EPS)
    o_ref[0] = (o * (sg_ref[...] * (1.0 - lam_init))).astype(BF16)


def _diff_attention(q1, q2, srcs, fixed, lam_p, subln, lam_init):
    b, lq, _ = q1.shape
    h = DA_HEADS
    const = lambda b_, h_, i: (0, 0)
    sg = jnp.pad(subln, (0, LANES - DA_VDIM)).reshape(1, LANES)
    flat = [a for src in srcs for a in src]

    def call(online, *args):
        tq = min(256 if online else 2048, lq)
        qs = pl.BlockSpec((1, tq, LANES), lambda b_, h_, i: (b_, i, h_))
        return pl.pallas_call(
            functools.partial(_dattn_kernel, lam_init=lam_init, online=online, nsrc=len(srcs),
                              tk_max=256 if online else 512),
            grid=(b, h, lq // tq),
            in_specs=[pl.BlockSpec(lam_p.shape, const), pl.BlockSpec((1, LANES), const), qs, qs]
                     + [pl.BlockSpec((1, a.shape[1], LANES), lambda b_, h_, i: (b_, 0, h_)) for a in flat],
            out_specs=qs,
            out_shape=jax.ShapeDtypeStruct((b, lq, h * LANES), BF16),
            scratch_shapes=[pltpu.VMEM((tq, LANES), F32)] * 2,
            compiler_params=_cp("parallel", "parallel", "arbitrary"),
            name="diff_attention_online" if online else "diff_attention",
        )(*args)

    return lax.cond(fixed, functools.partial(call, False), functools.partial(call, True),
                    lam_p, sg, q1, q2, *flat)


def _hy_prep_kernel(zp_ref, zc_ref, zn_ref, w_ref, b_ref, p_ref, x0_ref, ext):
    i = pl.program_id(1)
    last = pl.num_programs(1) - 1
    tl = zc_ref.shape[1]
    h = 2 * SUBLANES
    ext[0:h] = jnp.where(i == 0, 0.0, zp_ref[0].astype(F32))
    ext[h:h + tl] = zc_ref[0].astype(F32)
    ext[h + tl:] = jnp.where(i == last, 0.0, zn_ref[0].astype(F32))
    w = w_ref[...]
    u = (ext[pl.ds(h - 1, tl), :] * w[0:1] + ext[pl.ds(h, tl), :] * w[1:2] + ext[pl.ds(h + 1, tl), :] * w[2:3]
         + b_ref[...])
    hw = HY_WIDTH
    x0_ref[0] = u[:, 0:hw]
    p_ref[0] = u[:, 2 * hw:3 * hw] * u[:, hw:2 * hw]


def _hy_prep(zb, conv_w, conv_b):
    bx, lx, w = zb.shape
    tl = min(256, lx)
    h = 2 * SUBLANES
    nh = lx // h
    per = tl // h
    out = jax.ShapeDtypeStruct((bx, lx, HY_WIDTH), F32)
    return pl.pallas_call(
        _hy_prep_kernel,
        grid=(bx, lx // tl),
        in_specs=[pl.BlockSpec((1, h, w), lambda b, i: (b, jnp.maximum(i * per - 1, 0), 0)),
                  pl.BlockSpec((1, tl, w), lambda b, i: (b, i, 0)),
                  pl.BlockSpec((1, h, w), lambda b, i: (b, jnp.minimum((i + 1) * per, nh - 1), 0)),
                  pl.BlockSpec(conv_w.shape, lambda b, i: (0, 0)),
                  pl.BlockSpec((1, w), lambda b, i: (0, 0))],
        out_specs=[pl.BlockSpec((1, tl, HY_WIDTH), lambda b, i: (b, i, 0))] * 2,
        out_shape=[out, out],
        scratch_shapes=[pltpu.VMEM((tl + 2 * h, w), F32)],
        compiler_params=_cp("parallel", "parallel"),
        name="hyena_prep",
    )(zb, zb, zb, conv_w, conv_b.reshape(1, w))


def _hy_filter_kernel(feat_ref, w1, b1, w2, b2, w3, b3, w4, fr_ref, dl_ref, filt_ref, ssq_ref, *, s):
    i = pl.program_id(0)
    tr = feat_ref.shape[0]
    feat = feat_ref[...]
    fr = fr_ref[...]
    dot = lambda a, w: jnp.dot(a, w[...], precision=HI, preferred_element_type=F32)
    a = jnp.sin(fr * (dot(feat, w1) + b1[...]))
    a = jnp.sin(fr * (dot(a, w2) + b2[...]))
    a = jnp.sin(fr * (dot(a, w3) + b3[...]))
    coef = dot(a, w4)
    n = i * tr + lax.broadcasted_iota(jnp.int32, (tr, 1), 0)
    window = jnp.exp(-feat[:, 0:1] * dl_ref[...]) + HY_SHIFT
    half = jnp.where(n < s, coef[:, :HY_WIDTH], coef[:, HY_WIDTH:])
    filt = jnp.where(n == s, 0.0, half * window)
    filt_ref[...] = filt

    @pl.when(i == 0)
    def _():
        ssq_ref[...] = jnp.zeros_like(ssq_ref)

    ssq_ref[...] += jnp.sum(filt * filt, axis=0, keepdims=True)


def _hy_feat(s):
    t = jnp.linspace(0.0, 1.0, s, dtype=F32)[:, None]
    w = (2.0 * math.pi / s) * jnp.arange(s, dtype=F32)[:, None]
    bands = jnp.linspace(1e-4, HY_BANDS - 1, HY_BANDS, dtype=F32)[None, :]
    feat = jnp.concatenate([t, jnp.cos(w * bands), jnp.sin(w * bands)], axis=-1)
    feat = jnp.concatenate([feat, feat[:1], feat[:0:-1]], axis=0)
    return jnp.pad(feat, ((0, 0), (0, LANES - feat.shape[1])))


def _hy_filter(s, feat, w1, b1, w2, b2, w3, b3, w4, freq):
    n = 2 * s
    tr = min(512, n)
    deltas = jnp.abs(jnp.linspace(math.log(HY_TARGET) / HY_FAST_DECAY, math.log(HY_TARGET) / HY_SLOW_DECAY,
                                  HY_WIDTH, dtype=F32)).reshape(1, HY_WIDTH)
    w1p = jnp.pad(w1, ((0, LANES - w1.shape[0]), (0, 0)))
    row = lambda a: a.reshape(1, -1)
    args = (feat, w1p, row(b1), w2, row(b2), w3, row(b3), w4, row(freq), deltas)
    const = lambda i: (0, 0)
    return pl.pallas_call(
        functools.partial(_hy_filter_kernel, s=s),
        grid=(n // tr,),
        in_specs=[pl.BlockSpec((tr, LANES), lambda i: (i, 0))] + [pl.BlockSpec(a.shape, const) for a in args[1:]],
        out_specs=[pl.BlockSpec((tr, HY_WIDTH), lambda i: (i, 0)), pl.BlockSpec((1, HY_WIDTH), const)],
        out_shape=[jax.ShapeDtypeStruct((n, HY_WIDTH), F32), jax.ShapeDtypeStruct((1, HY_WIDTH), F32)],
        compiler_params=_cp("arbitrary"),
        name="hyena_filter",
    )(*args)


def _dft_factors(n):
    lg = n.bit_length() - 1
    assert 1 << lg == n
    n1 = 1 << (lg // 2)
    return n1, n // n1


def _dft_tables(n):
    n1, n2 = _dft_factors(n)
    ia = jnp.arange(n1, dtype=jnp.int32)
    ang1 = (2.0 * math.pi / n1) * ((ia[:, None] * ia[None, :]) % n1).astype(F32)
    f1 = jnp.concatenate([jnp.cos(ang1), -jnp.sin(ang1)], axis=0)
    c = jnp.arange(n1, dtype=jnp.int32)[:, None, None]
    d = jnp.arange(n2, dtype=jnp.int32)[None, :, None]
    b = jnp.arange(n2, dtype=jnp.int32)[None, None, :]
    ang = (2.0 * math.pi / n) * ((b * (c + n1 * d)) % n).astype(F32)
    re, im = jnp.cos(ang), -jnp.sin(ang)
    m1 = jnp.concatenate([jnp.concatenate([re, -im], axis=2), jnp.concatenate([im, re], axis=2)], axis=1)
    m2 = jnp.swapaxes(m1, 1, 2)
    ang4 = ang1[: n1 // 2]
    f4 = jnp.concatenate([jnp.cos(ang4), -jnp.sin(ang4)], axis=1) * (1.0 / n)
    return f1, _hi_lo(m1), _hi_lo(m2), _hi_lo(f4)


def _dot3(mh, ml, a):
    ah, al = _hi_lo(a)
    return (jnp.dot(mh, ah, preferred_element_type=F32) + jnp.dot(mh, al, preferred_element_type=F32)
            + jnp.dot(ml, ah, preferred_element_type=F32))


def _hy_stage1_kernel(fh_ref, fl_ref, x_ref, re_ref, im_ref):
    n1 = re_ref.shape[1]
    y = _dot3(fh_ref[...], fl_ref[...], x_ref[0])
    re_ref[0] = y[:n1]
    im_ref[0] = y[n1:]


def _hy_stage1(f, x2d):
    bx, k, lanes = x2d.shape
    n1 = f[0].shape[0] // 2
    tl = min(2048, lanes)
    out = jax.ShapeDtypeStruct((bx, n1, lanes), F32)
    return pl.pallas_call(
        _hy_stage1_kernel,
        grid=(bx, lanes // tl),
        in_specs=[pl.BlockSpec(f[0].shape, lambda b, j: (0, 0))] * 2 + [pl.BlockSpec((1, k, tl), lambda b, j: (b, 0, j))],
        out_specs=[pl.BlockSpec((1, n1, tl), lambda b, j: (b, 0, j))] * 2,
        out_shape=[out, out],
        compiler_params=_cp("parallel", "parallel"),
        name="hyena_dft_stage1",
    )(*f, x2d)


def _hy_mid_kernel(m1h_ref, m1l_ref, m2h_ref, m2l_ref, re_ref, im_ref, hre_ref, him_ref, ore_ref, oim_ref):
    n2 = re_ref.shape[2]
    for c in range(re_ref.shape[1]):
        a = jnp.concatenate([re_ref[0, c], im_ref[0, c]], axis=0)
        x = _dot3(m1h_ref[c], m1l_ref[c], a)
        xre, xim = x[:n2], x[n2:]
        hre, him = hre_ref[c], him_ref[c]
        y = jnp.concatenate([xre * hre - xim * him, xre * him + xim * hre], axis=0)
        bb = _dot3(m2h_ref[c], m2l_ref[c], y)
        ore_ref[0, c] = bb[:n2]
        oim_ref[0, c] = bb[n2:]


def _hy_spec_kernel(m1h_ref, m1l_ref, re_ref, im_ref, rs_ref, ore_ref, oim_ref):
    n2 = re_ref.shape[2]
    for c in range(re_ref.shape[1]):
        a = jnp.concatenate([re_ref[0, c], im_ref[0, c]], axis=0)
        x = _dot3(m1h_ref[c], m1l_ref[c], a) * rs_ref[...]
        ore_ref[c] = x[:n2]
        oim_ref[c] = x[n2:]


def _hy_filter_spectrum(m1, are, aim, rs):
    _, n1, n2, w = are.shape
    cb = min(4, n1)
    blk = pl.BlockSpec((1, cb, n2, w), lambda c: (0, c, 0, 0))
    mblk = pl.BlockSpec((cb, 2 * n2, 2 * n2), lambda c: (c, 0, 0))
    oblk = pl.BlockSpec((cb, n2, w), lambda c: (c, 0, 0))
    out = jax.ShapeDtypeStruct((n1, n2, w), F32)
    return pl.pallas_call(
        _hy_spec_kernel,
        grid=(n1 // cb,),
        in_specs=[mblk, mblk, blk, blk, pl.BlockSpec((1, w), lambda c: (0, 0))],
        out_specs=[oblk, oblk],
        out_shape=[out, out],
        compiler_params=_cp("parallel"),
        name="hyena_filter_spectrum",
    )(*m1, are, aim, rs)


def _hy_mid(m1, m2, are, aim, hre, him):
    bx, n1, n2, w = are.shape
    cb = min(4, n1)
    blk = pl.BlockSpec((1, cb, n2, w), lambda c, b: (b, c, 0, 0))
    mblk = pl.BlockSpec((cb, 2 * n2, 2 * n2), lambda c, b: (c, 0, 0))
    hblk = pl.BlockSpec((cb, n2, w), lambda c, b: (c, 0, 0))
    out = jax.ShapeDtypeStruct((bx, n1, n2, w), F32)
    return pl.pallas_call(
        _hy_mid_kernel,
        grid=(n1 // cb, bx),
        in_specs=[mblk, mblk, mblk, mblk, blk, blk, hblk, hblk],
        out_specs=[blk, blk],
        out_shape=[out, out],
        compiler_params=_cp("parallel", "parallel"),
        name="hyena_dft_mid",
    )(*m1, *m2, are, aim, hre, him)


def _hy_last_kernel(fh_ref, fl_ref, re_ref, im_ref, p_ref, x0_ref, bias_ref, o_ref):
    bb = jnp.concatenate([re_ref[0], im_ref[0]], axis=0)
    y = _dot3(fh_ref[...], fl_ref[...], bb)
    o_ref[0] = (x0_ref[0] * (y + p_ref[0] * bias_ref[...])).astype(BF16)


def _hy_last(f4, bre, bim, p2d, x02d, bias2d):
    bx, n1, lanes = bre.shape
    k = n1 // 2
    tl = min(2048, lanes)
    big = pl.BlockSpec((1, n1, tl), lambda b, j: (b, 0, j))
    small = pl.BlockSpec((1, k, tl), lambda b, j: (b, 0, j))
    return pl.pallas_call(
        _hy_last_kernel,
        grid=(bx, lanes // tl),
        in_specs=[pl.BlockSpec(f4[0].shape, lambda b, j: (0, 0))] * 2 + [big, big, small, small,
                  pl.BlockSpec((1, tl), lambda b, j: (0, j))],
        out_specs=small,
        out_shape=jax.ShapeDtypeStruct((bx, k, lanes), BF16),
        compiler_params=_cp("parallel", "parallel"),
        name="hyena_dft_last",
    )(*f4, bre, bim, p2d, x02d, bias2d)


def _hyena(zb, conv_w, conv_b, filt_params, hy_bias, consts):
    bx, lx, _ = zb.shape
    feat, (f1, m1, m2, f4) = consts
    n = 2 * lx
    n1, n2 = _dft_factors(n)
    w = HY_WIDTH
    filt, ssq = _hy_filter(lx, feat, *filt_params)
    rs = lax.rsqrt(ssq + EPS)
    fre, fim = _hy_stage1(_hi_lo(f1), filt.reshape(1, n1, n2 * w))
    hre, him = _hy_filter_spectrum(m1, fre.reshape(1, n1, n2, w), fim.reshape(1, n1, n2, w), rs)
    p, x0 = _hy_prep(zb, conv_w, conv_b)
    k = n1 // 2
    are, aim = _hy_stage1(_hi_lo(f1[:, :k]), p.reshape(bx, k, n2 * w))
    bre, bim = _hy_mid(m1, m2, are.reshape(bx, n1, n2, w), aim.reshape(bx, n1, n2, w), hre, him)
    bias2d = jnp.tile(hy_bias, n2).reshape(1, n2 * w)
    y = _hy_last(f4, bre.reshape(bx, n1, n2 * w), bim.reshape(bx, n1, n2 * w), p.reshape(bx, k, n2 * w),
                 x0.reshape(bx, k, n2 * w), bias2d)
    return y.reshape(bx, lx, w)


def _wattn_kernel(sink_ref, q_ref, *rest, banded):
    if banded:
        bias_ref, kp_ref, kc_ref, kn_ref, kx_ref, vp_ref, vc_ref, vn_ref, vx_ref, o_ref = rest
    else:
        kx_ref, vx_ref, o_ref = rest
    h = pl.program_id(1)
    i = pl.program_id(2)
    last = pl.num_programs(2) - 1
    qb = q_ref.shape[1]
    q = jnp.concatenate([q_ref[0, :, :LANES], q_ref[0, :, LANES:]], axis=0)
    dn = (((1,), (1,)), ((), ()))
    parts = []
    if banded:
        kk = jnp.concatenate([kp_ref[0], kc_ref[0], kn_ref[0]], axis=0)
        vv = jnp.concatenate([vp_ref[0], vc_ref[0], vn_ref[0]], axis=0)
        s = lax.dot_general(q, kk, dn, preferred_element_type=F32)
        c = lax.broadcasted_iota(jnp.int32, (1, qb + 2 * BLOCK), 1) - BLOCK
        edge = jnp.where(((c < 0) & (i == 0)) | ((c >= qb) & (i == last)), NEG_INF, 0.0)
        parts.append((s + bias_ref[...] + edge, vv))
    parts.append((lax.dot_general(q, kx_ref[0], dn, preferred_element_type=F32), vx_ref[0]))
    top = lax.broadcasted_iota(jnp.int32, (2 * qb, 1), 0) < qb
    snk = jnp.where(top, sink_ref[h * WA_GROUP], sink_ref[h * WA_GROUP + 1])
    m = snk
    for s, _ in parts:
        m = jnp.maximum(m, jnp.max(s, axis=-1, keepdims=True))
    den = jnp.exp(snk - m)
    acc = jnp.zeros((2 * qb, LANES), F32)
    for s, v in parts:
        p = jnp.exp(s - m)
        den = den + jnp.sum(p, axis=-1, keepdims=True)
        acc = acc + jnp.dot(p.astype(BF16), v, preferred_element_type=F32)
    o = (acc / den).astype(BF16)
    o_ref[0] = jnp.concatenate([o[:qb], o[qb:]], axis=1)


def _window_attention(q, k, v, kx, vx, sink, banded):
    b, lq, _ = q.shape
    cx = kx.shape[1]
    qb = min(512, lq)
    per = qb // BLOCK
    nblk = lq // BLOCK
    side = lambda f: pl.BlockSpec((1, BLOCK, LANES), f)
    prev = side(lambda b_, h, i: (b_, jnp.maximum(i * per - 1, 0), h))
    nxt = side(lambda b_, h, i: (b_, jnp.minimum((i + 1) * per, nblk - 1), h))
    cur = pl.BlockSpec((1, qb, LANES), lambda b_, h, i: (b_, i, h))
    ctx = pl.BlockSpec((1, cx, LANES), lambda b_, h, i: (b_, 0, h))
    qspec = pl.BlockSpec((1, qb, WA_GROUP * LANES), lambda b_, h, i: (b_, i, h))
    if banded:
        r = jnp.arange(WA_GROUP * qb)[:, None] % qb
        c = jnp.arange(qb + 2 * BLOCK)[None, :] - BLOCK
        bias = jnp.where(jnp.abs(r - c) <= BLOCK, 0.0, NEG_INF).astype(F32)
        specs = [pl.BlockSpec(bias.shape, lambda b_, h, i: (0, 0)), prev, cur, nxt, ctx, prev, cur, nxt, ctx]
        args = (bias, k, k, k, kx, v, v, v, vx)
    else:
        specs, args = [ctx, ctx], (kx, vx)
    return pl.pallas_call(
        functools.partial(_wattn_kernel, banded=banded),
        grid=(b, WA_KV_HEADS, lq // qb),
        in_specs=[pl.BlockSpec(memory_space=pltpu.SMEM), qspec] + specs,
        out_specs=qspec,
        out_shape=jax.ShapeDtypeStruct((b, lq, WA_HEADS * LANES), BF16),
        compiler_params=_cp("parallel", "parallel", "arbitrary"),
        name="window_attention",
    )(sink, q, *args)


def _conf_kernel(zp_ref, zc_ref, zn_ref, w_ref, b_ref, lg_ref, lb_ref, o_ref, ext, *, halo):
    i = pl.program_id(1)
    last = pl.num_programs(1) - 1
    tl = zc_ref.shape[1]
    cw = CF_WIDTH

    def glu(z_ref):
        z = z_ref[0].astype(F32)
        return z[:, :cw] * jax.nn.sigmoid(z[:, cw:])

    ext[0:halo] = jnp.where(i == 0, 0.0, glu(zp_ref))
    ext[halo:halo + tl] = glu(zc_ref)
    ext[halo + tl:] = jnp.where(i == last, 0.0, glu(zn_ref))
    w = w_ref[...]
    u = jnp.zeros((tl, cw), F32) + b_ref[...]
    for j in range(CF_TAPS):
        u = u + ext[pl.ds(halo - CF_TAPS // 2 + j, tl), :] * w[j:j + 1]
    uc = u - jnp.mean(u, axis=-1, keepdims=True)
    y = uc * lax.rsqrt(jnp.mean(uc * uc, axis=-1, keepdims=True) + EPS) * lg_ref[...] + lb_ref[...]
    o_ref[0] = _silu(y).astype(BF16)


def _conformer(zd, dw_w, dw_b, ln_g, ln_b):
    bx, lx, w = zd.shape
    tl = min(256, lx)
    halo = 2 * SUBLANES
    nh = lx // halo
    per = tl // halo
    row = lambda a: a.reshape(1, -1)
    const = lambda b, i: (0, 0)
    return pl.pallas_call(
        functools.partial(_conf_kernel, halo=halo),
        grid=(bx, lx // tl),
        in_specs=[pl.BlockSpec((1, halo, w), lambda b, i: (b, jnp.maximum(i * per - 1, 0), 0)),
                  pl.BlockSpec((1, tl, w), lambda b, i: (b, i, 0)),
                  pl.BlockSpec((1, halo, w), lambda b, i: (b, jnp.minimum((i + 1) * per, nh - 1), 0)),
                  pl.BlockSpec(dw_w.shape, const)] + [pl.BlockSpec((1, CF_WIDTH), const)] * 3,
        out_specs=pl.BlockSpec((1, tl, CF_WIDTH), lambda b, i: (b, i, 0)),
        out_shape=jax.ShapeDtypeStruct((bx, lx, CF_WIDTH), BF16),
        scratch_shapes=[pltpu.VMEM((tl + 2 * halo, CF_WIDTH), F32)],
        compiler_params=_cp("parallel", "parallel"),
        name="conformer_conv",
    )(zd, zd, zd, dw_w, row(dw_b), row(ln_g), row(ln_b))


def _merge_kernel(x_ref, m_ref, ya, yb, yc, yd, zg_ref, bg_ref, wa, wb, wc, wd, wo_ref, o_ref, *, goff):
    d = x_ref.shape[-1]
    acc = jnp.zeros(x_ref.shape[1:], F32)
    for i, (y, w) in enumerate(((ya, wa), (yb, wb), (yc, wc), (yd, wd))):
        gate = jax.nn.sigmoid(zg_ref[0, :, i * d:(i + 1) * d].astype(F32) + bg_ref[:, i * d:(i + 1) * d])
        acc = acc + gate * jnp.dot(y[0], w[...], preferred_element_type=F32)
    out = jnp.dot(acc.astype(BF16), wo_ref[...], preferred_element_type=F32)
    o_ref[0] = x_ref[0] + m_ref[0, :, goff:goff + d] * out


def _merge(x, mod, ys, zg, b_gate, wbs, w_out):
    bx, lx, d = x.shape
    tm = min(256, lx)
    row = lambda b, i: (b, i, 0)
    const = lambda b, i: (0, 0)
    return pl.pallas_call(
        functools.partial(_merge_kernel, goff=2 * d),
        grid=(bx, lx // tm),
        in_specs=[pl.BlockSpec((1, tm, d), row), pl.BlockSpec((1, 1, mod.shape[-1]), lambda b, i: (b, 0, 0))]
                 + [pl.BlockSpec((1, tm, y.shape[-1]), row) for y in ys]
                 + [pl.BlockSpec((1, tm, N_BRANCH * d), row), pl.BlockSpec((1, N_BRANCH * d), const)]
                 + [pl.BlockSpec(w.shape, const) for w in wbs] + [pl.BlockSpec(w_out.shape, const)],
        out_specs=pl.BlockSpec((1, tm, d), row),
        out_shape=jax.ShapeDtypeStruct(x.shape, F32),
        input_output_aliases={0: 0},
        compiler_params=_cp("parallel", "parallel"),
        name="merge",
    )(x, mod, *ys, zg, b_gate.reshape(1, -1), *wbs, w_out)


def _slot_rows(w, group):
    n = w.shape[0] // group
    return jnp.pad(w.reshape(n, group, -1), ((0, 0), (0, LANES - group), (0, 0))).reshape(n * LANES, -1)


def _moe_kernel(x_ref, m_ref, g_ref, wr_ref, br_ref, w1_ref, w3_ref, w2_ref, o_ref, h_sc, gate_sc, acc_sc):
    e = pl.program_id(2)
    d = x_ref.shape[-1]
    tm = x_ref.shape[1]
    lane = lax.broadcasted_iota(jnp.int32, (tm, LANES), 1).astype(F32)

    @pl.when(e == 0)
    def _():
        h = _rms(x_ref[0]) * g_ref[...] * (1.0 + m_ref[0, :, 4 * d:5 * d]) + m_ref[0, :, 3 * d:4 * d]
        h_sc[...] = h.astype(BF16)
        lg = jnp.dot(h, wr_ref[...], precision=HI, preferred_element_type=F32) + br_ref[...]
        isg = lane < N_GROUPS
        gmax = jnp.max(jnp.where(isg, lg, NEG_INF), axis=-1, keepdims=True)
        gi = jnp.min(jnp.where(isg & (lg == gmax), lane, LANES), axis=-1, keepdims=True)
        gw = 1.0 / jnp.sum(jnp.where(isg, jnp.exp(lg - gmax), 0.0), axis=-1, keepdims=True)
        lo = N_GROUPS + gi * EXP_PER_GROUP
        ise = (lane >= lo) & (lane < lo + EXP_PER_GROUP)
        le = jnp.where(ise, lg, NEG_INF)
        m1 = jnp.max(le, axis=-1, keepdims=True)
        i1 = jnp.min(jnp.where(ise & (le == m1), lane, LANES), axis=-1, keepdims=True)
        ise2 = ise & (lane != i1)
        le2 = jnp.where(ise2, lg, NEG_INF)
        m2 = jnp.max(le2, axis=-1, keepdims=True)
        i2 = jnp.min(jnp.where(ise2 & (le2 == m2), lane, LANES), axis=-1, keepdims=True)
        r = jnp.exp(m2 - m1)
        wa = gw / (1.0 + r)
        gate_sc[...] = jnp.where(lane == i1, wa, 0.0) + jnp.where(lane == i2, wa * r, 0.0)
        acc_sc[...] = jnp.zeros_like(acc_sc)

    hb = h_sc[...]
    u = _silu(jnp.dot(hb, w1_ref[0], preferred_element_type=F32)) * jnp.dot(hb, w3_ref[0], preferred_element_type=F32)
    ge = jnp.sum(jnp.where(lane == (e + N_GROUPS).astype(F32), gate_sc[...], 0.0), axis=-1, keepdims=True)
    acc_sc[...] += ge * jnp.dot(u.astype(BF16), w2_ref[0], preferred_element_type=F32)

    @pl.when(e == pl.num_programs(2) - 1)
    def _():
        o_ref[0] = x_ref[0] + m_ref[0, :, 5 * d:6 * d] * acc_sc[...]


def _moe(x, mod, g, w_router, b_router, w1, w3, w2):
    bx, lx, d = x.shape
    tm = min(1024, lx)
    ne, _, f = w1.shape
    row = lambda b, i, e: (b, i, 0)
    const = lambda b, i, e: (0, 0)
    return pl.pallas_call(
        _moe_kernel,
        grid=(bx, lx // tm, ne),
        in_specs=[pl.BlockSpec((1, tm, d), row), pl.BlockSpec((1, 1, mod.shape[-1]), lambda b, i, e: (b, 0, 0)),
                  pl.BlockSpec((1, d), const), pl.BlockSpec((d, LANES), const), pl.BlockSpec((1, LANES), const),
                  pl.BlockSpec((1, d, f), lambda b, i, e: (e, 0, 0)), pl.BlockSpec((1, d, f), lambda b, i, e: (e, 0, 0)),
                  pl.BlockSpec((1, f, d), lambda b, i, e: (e, 0, 0))],
        out_specs=pl.BlockSpec((1, tm, d), row),
        out_shape=jax.ShapeDtypeStruct(x.shape, F32),
        scratch_shapes=[pltpu.VMEM((tm, d), BF16), pltpu.VMEM((tm, LANES), F32), pltpu.VMEM((tm, d), F32)],
        input_output_aliases={0: 0},
        compiler_params=_cp("parallel", "parallel", "arbitrary"),
        name="moe",
    )(x, mod, g.reshape(1, d), w_router, b_router, w1, w3, w2)


SC_CORES = 2
SC_SUBCORES = 16
SC_CHUNK = 64
MOE_ROWS = 1024


def _sc_gather(table, idx):
    n = idx.shape[0]
    w = table.shape[1]
    per = n // (SC_CORES * SC_SUBCORES)
    assert per * SC_CORES * SC_SUBCORES == n and per % SC_CHUNK == 0
    mesh = plsc.VectorSubcoreMesh(core_axis_name="c", subcore_axis_name="s")

    @functools.partial(
        pl.kernel, mesh=mesh, out_type=jax.ShapeDtypeStruct((n, w), table.dtype),
        scratch_types=[pltpu.VMEM((SC_CHUNK,), jnp.int32), pltpu.VMEM((SC_CHUNK, w), table.dtype),
                       pltpu.SemaphoreType.DMA],
        name="sc_row_gather")
    def gather(table_hbm, idx_hbm, out_hbm, idx_v, rows_v, sem):
        base = (lax.axis_index("s") * SC_CORES + lax.axis_index("c")) * per

        @pl.loop(0, per // SC_CHUNK)
        def _(j):
            off = pl.multiple_of(base + j * SC_CHUNK, SC_CHUNK)
            pltpu.sync_copy(idx_hbm.at[pl.ds(off, SC_CHUNK)], idx_v)
            pltpu.async_copy(table_hbm.at[idx_v], rows_v, sem).wait()
            pltpu.sync_copy(rows_v, out_hbm.at[pl.ds(off, SC_CHUNK)])

    return gather(table, idx)


def _route_kernel(x_ref, m_ref, g_ref, wr_ref, br_ref, h_ref, gate_ref, gi_ref):
    d = x_ref.shape[-1]
    tm = x_ref.shape[1]
    lane = lax.broadcasted_iota(jnp.int32, (tm, LANES), 1).astype(F32)
    h = _rms(x_ref[0]) * g_ref[...] * (1.0 + m_ref[0, :, 4 * d:5 * d]) + m_ref[0, :, 3 * d:4 * d]
    h_ref[0] = h
    lg = jnp.dot(h, wr_ref[...], precision=HI, preferred_element_type=F32) + br_ref[...]
    isg = lane < N_GROUPS
    gmax = jnp.max(jnp.where(isg, lg, NEG_INF), axis=-1, keepdims=True)
    gi = jnp.min(jnp.where(isg & (lg == gmax), lane, LANES), axis=-1, keepdims=True)
    gw = 1.0 / jnp.sum(jnp.where(isg, jnp.exp(lg - gmax), 0.0), axis=-1, keepdims=True)
    lo = N_GROUPS + gi * EXP_PER_GROUP
    ise = (lane >= lo) & (lane < lo + EXP_PER_GROUP)
    le = jnp.where(ise, lg, NEG_INF)
    m1 = jnp.max(le, axis=-1, keepdims=True)
    i1 = jnp.min(jnp.where(ise & (le == m1), lane, LANES), axis=-1, keepdims=True)
    ise2 = ise & (lane != i1)
    le2 = jnp.where(ise2, lg, NEG_INF)
    m2 = jnp.max(le2, axis=-1, keepdims=True)
    i2 = jnp.min(jnp.where(ise2 & (le2 == m2), lane, LANES), axis=-1, keepdims=True)
    r = jnp.exp(m2 - m1)
    wa = gw / (1.0 + r)
    gate_ref[0] = jnp.where(lane == i1 - lo, wa, 0.0) + jnp.where(lane == i2 - lo, wa * r, 0.0)
    gi_ref[0] = gi.astype(jnp.int32)


def _route(x, mod, g, w_router, b_router):
    bx, lx, d = x.shape
    tm = min(512, lx)
    row = lambda b, i: (b, i, 0)
    const = lambda b, i: (0, 0)
    return pl.pallas_call(
        _route_kernel,
        grid=(bx, lx // tm),
        in_specs=[pl.BlockSpec((1, tm, d), row), pl.BlockSpec((1, 1, mod.shape[-1]), lambda b, i: (b, 0, 0)),
                  pl.BlockSpec((1, d), const), pl.BlockSpec((d, LANES), const), pl.BlockSpec((1, LANES), const)],
        out_specs=[pl.BlockSpec((1, tm, d), row), pl.BlockSpec((1, tm, LANES), row), pl.BlockSpec((1, tm, 1), row)],
        out_shape=[jax.ShapeDtypeStruct((bx, lx, d), F32), jax.ShapeDtypeStruct((bx, lx, LANES), F32),
                   jax.ShapeDtypeStruct((bx, lx, 1), jnp.int32)],
        compiler_params=_cp("parallel", "parallel"),
        name="moe_route",
    )(x, mod, g.reshape(1, d), w_router, b_router)


def _gmoe_kernel(grp_ref, nv_ref, xs_ref, gs_ref, w1_ref, w3_ref, w2_ref, o_ref, h_sc, acc_sc):
    i = pl.program_id(0)
    e = pl.program_id(1)
    tm = xs_ref.shape[0]

    @pl.when(e == 0)
    def _():
        h_sc[...] = xs_ref[...].astype(BF16)
        acc_sc[...] = jnp.zeros_like(acc_sc)

    hb = h_sc[...]
    u = _silu(jnp.dot(hb, w1_ref[0], preferred_element_type=F32)) * jnp.dot(hb, w3_ref[0], preferred_element_type=F32)
    lane = lax.broadcasted_iota(jnp.int32, (tm, LANES), 1)
    ge = jnp.sum(jnp.where(lane == e, gs_ref[...], 0.0), axis=-1, keepdims=True)
    ge = jnp.where(lax.broadcasted_iota(jnp.int32, (tm, 1), 0) < nv_ref[i], ge, 0.0)
    acc_sc[...] += ge * jnp.dot(u.astype(BF16), w2_ref[0], preferred_element_type=F32)

    @pl.when(e == pl.num_programs(1) - 1)
    def _():
        o_ref[...] = acc_sc[...]


def _grouped_moe(grp, nv, xs, gs, w1, w3, w2):
    p, d = xs.shape
    _, _, f = w1.shape
    wmap = lambda i, e, grp, nv: (grp[i] * EXP_PER_GROUP + e, 0, 0)
    return pl.pallas_call(
        _gmoe_kernel,
        grid_spec=pltpu.PrefetchScalarGridSpec(
            num_scalar_prefetch=2,
            grid=(p // MOE_ROWS, EXP_PER_GROUP),
            in_specs=[pl.BlockSpec((MOE_ROWS, d), lambda i, e, grp, nv: (i, 0)),
                      pl.BlockSpec((MOE_ROWS, LANES), lambda i, e, grp, nv: (i, 0)),
                      pl.BlockSpec((1, d, f), wmap), pl.BlockSpec((1, d, f), wmap), pl.BlockSpec((1, f, d), wmap)],
            out_specs=pl.BlockSpec((MOE_ROWS, d), lambda i, e, grp, nv: (i, 0)),
            scratch_shapes=[pltpu.VMEM((MOE_ROWS, d), BF16), pltpu.VMEM((MOE_ROWS, d), F32)]),
        out_shape=jax.ShapeDtypeStruct((p, d), F32),
        compiler_params=_cp("arbitrary", "arbitrary"),
        name="moe_experts",
    )(grp, nv, xs, gs, w1, w3, w2)


def _residual_kernel(x_ref, m_ref, y_ref, o_ref):
    d = x_ref.shape[-1]
    o_ref[0] = x_ref[0] + m_ref[0, :, 5 * d:6 * d] * y_ref[0]


def _residual(x, mod, y):
    bx, lx, d = x.shape
    tm = min(1024, lx)
    row = lambda b, i: (b, i, 0)
    return pl.pallas_call(
        _residual_kernel,
        grid=(bx, lx // tm),
        in_specs=[pl.BlockSpec((1, tm, d), row), pl.BlockSpec((1, 1, mod.shape[-1]), lambda b, i: (b, 0, 0)),
                  pl.BlockSpec((1, tm, d), row)],
        out_specs=pl.BlockSpec((1, tm, d), row),
        out_shape=jax.ShapeDtypeStruct(x.shape, F32),
        input_output_aliases={0: 0},
        compiler_params=_cp("parallel", "parallel"),
        name="moe_residual",
    )(x, mod, y)


def _moe_sorted(x, mod, g, w_router, b_router, w1, w3, w2):
    bx, lx, d = x.shape
    t = bx * lx
    h, gate, gi = _route(x, mod, g, w_router, b_router)
    gi = gi.reshape(t)
    onehot = (gi[:, None] == jnp.arange(N_GROUPS, dtype=jnp.int32)[None, :]).astype(jnp.int32)
    csum = jnp.cumsum(onehot, axis=0)
    counts = csum[-1]
    rank = jnp.take_along_axis(csum, gi[:, None], axis=1)[:, 0] - 1
    padded = (counts + MOE_ROWS - 1) // MOE_ROWS * MOE_ROWS
    pend = jnp.cumsum(padded)
    pstart = pend - padded
    pos = pstart[gi] + rank
    p = t + N_GROUPS * MOE_ROWS
    src = jnp.zeros((p,), jnp.int32).at[pos].set(jnp.arange(t, dtype=jnp.int32))
    bstart = jnp.arange(p // MOE_ROWS, dtype=jnp.int32) * MOE_ROWS
    grp = jnp.minimum(jnp.searchsorted(pend, bstart, side="right"), N_GROUPS - 1).astype(jnp.int32)
    nv = jnp.clip(pstart[grp] + counts[grp] - bstart, 0, MOE_ROWS).astype(jnp.int32)
    xs = _sc_gather(h.reshape(t, d), src)
    gs = _sc_gather(gate.reshape(t, LANES), src)
    ys = _grouped_moe(grp, nv, xs, gs, w1, w3, w2)
    yt = _sc_gather(ys, pos)
    return _residual(x, mod, yt.reshape(bx, lx, d))


def kernel(x, c, ctx, c_ctx, w_mod, b_mod, norm1_g, norm2_g, w_in, b_gate, da_qn, da_kn, da_lam, da_subln, hy_conv_w, hy_conv_b, hf_w1, hf_b1, hf_w2, hf_b2, hf_w3, hf_b3, hf_w4, hf_freq, hy_bias, wa_qn, wa_kn, wa_sink, cf_dw_w, cf_dw_b, cf_ln_g, cf_ln_b, w_branch, w_out, w_rg, b_rg, w_re, b_re, w1, w3, w2):
    b, s, d = x.shape
    cl = ctx.shape[1]
    depth = w_mod.shape[0]
    assert s % 256 == 0 and cl % 256 == 0 and s % GRID_W == 0

    nrow = -(-(b + 1) // SUBLANES) * SUBLANES
    crows = jnp.zeros((nrow, d), F32).at[:b].set(c).at[b].set(c_ctx)
    mods = _mod_vectors(crows, w_mod, b_mod)

    aw, qw, kw = DA_HEADS * DA_DIM, WA_HEADS * WA_DIM, WA_KV_HEADS * WA_DIM
    tab_lat = (*_rope_tables(s, DA_DIM, DA_HEADS), *_rope_tables(s, WA_DIM, WA_HEADS), *_rope_tables(s, WA_DIM, WA_KV_HEADS))
    tab_ctx = (*_unit_tables(cl, aw), *_unit_tables(cl, qw), *_unit_tables(cl, kw))
    hy_lat = (_hy_feat(s), _dft_tables(2 * s))
    hy_ctx = (_hy_feat(cl), _dft_tables(2 * cl))
    gms = (_group_ones(aw, DA_DIM), _group_ones(qw, WA_DIM), _group_ones(kw, WA_DIM))
    sels = (_slot_select(aw, DA_DIM), _slot_select(DA_HEADS * DA_VDIM, DA_VDIM), _slot_select(qw, WA_DIM),
            _slot_select(kw, WA_DIM))
    qvec = _slot_vector(DA_HEADS, DA_DIM, DA_DIM + 1, 1.0)
    vvec = _slot_vector(DA_HEADS, DA_VDIM, DA_VDIM + DA_ONES, 1.0)

    xc = ctx
    for l in range(depth):
        last = l == depth - 1
        lam_init = 0.8 - 0.6 * math.exp(-0.3 * l)
        mod_x = mods[l, :b][:, None, :]
        mod_c = jnp.broadcast_to(mods[l, b][None, None, :], (b, 1, mods.shape[-1]))
        wl = w_in[l]
        ws = [wl[:, 0:OFF_B].astype(BF16), wl[:, OFF_B:OFF_C].astype(BF16), wl[:, OFF_C:OFF_D].astype(BF16),
              wl[:, OFF_D:OFF_G].astype(BF16), wl[:, OFF_G:].astype(BF16)]
        shift = 1.02 * LOG2E * DA_DIM ** 0.5 * jnp.max(jnp.abs(da_qn[l])) * jnp.max(jnp.abs(da_kn[l]))
        fixed = shift <= DA_SHIFT_MAX
        kvec = _slot_vector(DA_HEADS, DA_DIM, DA_DIM + 1, jnp.where(fixed, -shift, 0.0))
        tile = lambda a, n: jnp.tile(a, n).reshape(1, -1)
        consts = (*gms, tile(da_qn[l], DA_HEADS), tile(da_kn[l], DA_HEADS), tile(wa_qn[l], WA_HEADS),
                  tile(wa_kn[l], WA_KV_HEADS), *sels, qvec, kvec, vvec)
        q1, q2, k1, k2, v, zb, qc, kc, vc, zd, zg = _in_proj(x, mod_x, norm1_g[l], ws, tab_lat, consts)
        q1x, q2x, k1x, k2x, vx, zbx, qcx, kcx, vcx, zdx, zgx = _in_proj(xc, mod_c, norm1_g[l], ws, tab_ctx, consts)
        filt_params = (hf_w1[l], hf_b1[l], hf_w2[l], hf_b2[l], hf_w3[l], hf_b3[l], hf_w4[l], hf_freq[l])
        wb4 = w_branch[l].astype(BF16)
        wbs = (_slot_rows(wb4[0], DA_VDIM), wb4[1], _slot_rows(wb4[2], WA_DIM), wb4[3])
        wo = w_out[l].astype(BF16)

        ya = _diff_attention(q1, q2, [(k1, k2, v), (k1x, k2x, vx)], fixed, da_lam[l], da_subln[l], lam_init)
        yb = _hyena(zb, hy_conv_w[l], hy_conv_b[l], filt_params, hy_bias[l], hy_lat)
        yc_ = _window_attention(qc, kc, vc, kcx, vcx, wa_sink[l], True)
        yd = _conformer(zd, cf_dw_w[l], cf_dw_b[l], cf_ln_g[l], cf_ln_b[l])
        x = _merge(x, mod_x, (ya, yb, yc_, yd), zg, b_gate[l], wbs, wo)

        w_router = jnp.pad(jnp.concatenate([w_rg[l], w_re[l]], axis=1), ((0, 0), (0, LANES - N_GROUPS - N_EXPERTS)))
        b_router = jnp.pad(jnp.concatenate([b_rg[l], b_re[l]]), (0, LANES - N_GROUPS - N_EXPERTS)).reshape(1, LANES)
        ew = (w1[l].astype(BF16), w3[l].astype(BF16), w2[l].astype(BF16))

        if not last:
            yca = _diff_attention(q1x, q2x, [(k1x, k2x, vx)], fixed, da_lam[l], da_subln[l], lam_init)
            ycb = _hyena(zbx, hy_conv_w[l], hy_conv_b[l], filt_params, hy_bias[l], hy_ctx)
            ycc = _window_attention(qcx, kcx, vcx, kcx, vcx, wa_sink[l], False)
            ycd = _conformer(zdx, cf_dw_w[l], cf_dw_b[l], cf_ln_g[l], cf_ln_b[l])
            xc = _merge(xc, mod_c, (yca, ycb, ycc, ycd), zgx, b_gate[l], wbs, wo)
            xc = _moe(xc.reshape(1, b * cl, d), mod_c[:1], norm2_g[l], w_router, b_router, *ew).reshape(b, cl, d)
        x = _moe_sorted(x, mod_x, norm2_g[l], w_router, b_router, *ew)
    return x
```

```python
import functools
import math

import jax
import jax.numpy as jnp
from jax import lax
from jax.experimental import pallas as pl
from jax.experimental.pallas import tpu as pltpu
from jax.experimental.pallas import tpu_sc as plsc

F32 = jnp.float32
BF16 = jnp.bfloat16
HI = lax.Precision.HIGHEST

GRID_W = 64
BLOCK = 128
ROPE_BASE = 10000.0
EPS = 1e-6
NEG_INF = -1e30

DA_HEADS = 4
DA_DIM = 32
DA_VDIM = 64
HY_WIDTH = 256
HY_BANDS = 16
HY_FF = 64
HY_SHIFT = 0.05
HY_FAST_DECAY = 0.3
HY_SLOW_DECAY = 1.5
HY_TARGET = 1e-2
WA_HEADS = 4
WA_KV_HEADS = 2
WA_GROUP = 2
WA_DIM = 64
CF_WIDTH = 256
CF_TAPS = 31
N_BRANCH = 4
BRANCH_W = 256
N_GROUPS = 4
EXP_PER_GROUP = 4
N_EXPERTS = 16

W_A = 4 * DA_HEADS * DA_DIM + DA_HEADS * DA_VDIM
W_B = 3 * HY_WIDTH
W_C = (WA_HEADS + 2 * WA_KV_HEADS) * WA_DIM
W_D = 2 * CF_WIDTH
OFF_B = W_A
OFF_C = OFF_B + W_B
OFF_D = OFF_C + W_C
OFF_G = OFF_D + W_D

LOG2E = math.log2(math.e)
LANES = 128
SUBLANES = 8
VMEM_LIMIT = 56 * 1024 * 1024

DA_ONES = 16
DA_SHIFT_MAX = 50.0


def _cp(*sem):
    return pltpu.CompilerParams(dimension_semantics=sem, vmem_limit_bytes=VMEM_LIMIT)


def _rms(xf):
    return xf * lax.rsqrt(jnp.mean(xf * xf, axis=-1, keepdims=True) + EPS)


def _silu(x):
    return x * jax.nn.sigmoid(x)


def _mod_kernel(c_ref, w_ref, b_ref, o_ref):
    s = _silu(c_ref[...])
    o_ref[0] = jnp.dot(s, w_ref[0], precision=HI, preferred_element_type=F32) + b_ref[0]


def _mod_vectors(crows, w_mod, b_mod):
    depth, d, n = w_mod.shape
    r = crows.shape[0]
    tn = 1536
    return pl.pallas_call(
        _mod_kernel,
        grid=(depth, n // tn),
        in_specs=[pl.BlockSpec((r, d), lambda l, j: (0, 0)),
                  pl.BlockSpec((1, d, tn), lambda l, j: (l, 0, j)),
                  pl.BlockSpec((1, 1, tn), lambda l, j: (l, 0, j))],
        out_specs=pl.BlockSpec((1, r, tn), lambda l, j: (l, 0, j)),
        out_shape=jax.ShapeDtypeStruct((depth, r, n), F32),
        compiler_params=_cp("arbitrary", "arbitrary"),
        name="mod_vectors",
    )(crows, w_mod, b_mod.reshape(depth, 1, n))


def _rope_tables(s, d, reps):
    rows = s // GRID_W
    row = jnp.repeat(jnp.arange(rows, dtype=F32), GRID_W)
    col = jnp.tile(jnp.arange(GRID_W, dtype=F32), rows)
    qd = d // 4
    inv = ROPE_BASE ** (-jnp.arange(qd, dtype=F32) / qd)
    ar = row[:, None] * inv[None, :]
    ac = col[:, None] * inv[None, :]
    z = jnp.zeros_like(ar)
    cos = jnp.concatenate([jnp.cos(ar), jnp.cos(ar), jnp.cos(ac), jnp.cos(ac)], axis=-1)
    sin_up = jnp.concatenate([-jnp.sin(ar), z, -jnp.sin(ac), z], axis=-1)
    sin_dn = jnp.concatenate([z, jnp.sin(ar), z, jnp.sin(ac)], axis=-1)
    t = lambda a: jnp.tile(a, (1, reps))
    return t(cos), t(sin_up), t(sin_dn)


def _unit_tables(s, w):
    return jnp.ones((s, w), F32), jnp.zeros((s, w), F32), jnp.zeros((s, w), F32)


def _group_ones(width, group):
    i = jnp.arange(width) // group
    return (i[:, None] == i[None, :]).astype(BF16)


def _hi_lo(a):
    hi = a.astype(BF16)
    return hi, (a - hi.astype(F32)).astype(BF16)


def _slot_select(width, group):
    i = jnp.arange(width)
    dst = (i // group) * LANES + i % group
    return (dst[:, None] == jnp.arange((width // group) * LANES)[None, :]).astype(BF16)


def _slot_vector(n, lo, hi, value):
    j = jnp.arange(n * LANES) % LANES
    return jnp.where((j >= lo) & (j < hi), value, 0.0).astype(F32).reshape(1, n * LANES)


def _norm_rope(x, gmat, gain, cos, sup, sdn, group, qd):
    w = x.shape[-1]
    sh, sl = _hi_lo(x * x)
    ss = (jnp.dot(sh, gmat, preferred_element_type=F32) + jnp.dot(sl, gmat, preferred_element_type=F32)) * (1.0 / group)
    xn = x * lax.rsqrt(ss + EPS) * gain
    return xn * cos + pltpu.roll(xn, w - qd, 1) * sup + pltpu.roll(xn, qd, 1) * sdn


def _inproj_kernel(x_ref, m_ref, g_ref, wa, wb, wc, wd, wg,
                   ca, ua, da, cq, uq, dq, ck, uk, dk, gma, gmq, gmk, qna, kna, qnc, knc,
                   sela, selv, selq, selk, qvec, kvec, vvec,
                   q1o, q2o, k1o, k2o, vo, zbo, qco, kco, vco, zdo, zgo):
    d = x_ref.shape[-1]
    x = x_ref[0]
    shift = m_ref[0, :, 0:d]
    scale = m_ref[0, :, d:2 * d]
    h = (_rms(x) * g_ref[...] * (1.0 + scale) + shift).astype(BF16)
    place = lambda y, sel: jnp.dot(y.astype(BF16), sel[...], preferred_element_type=F32)

    za = jnp.dot(h, wa[...], preferred_element_type=F32)
    hw = DA_HEADS * DA_DIM
    cos, sup, sdn, gm = ca[...], ua[...], da[...], gma[...]
    qscale = DA_DIM ** -0.5 * LOG2E
    for t, (o, gain, sc, vec) in enumerate(((q1o, qna, qscale, qvec), (q2o, qna, qscale, qvec),
                                            (k1o, kna, 1.0, kvec), (k2o, kna, 1.0, kvec))):
        y = _norm_rope(za[:, t * hw:(t + 1) * hw], gm, gain[...], cos, sup, sdn, DA_DIM, DA_DIM // 4) * sc
        o[0] = (place(y, sela) + vec[...]).astype(BF16)
    vo[0] = (place(za[:, 4 * hw:], selv) + vvec[...]).astype(BF16)

    zbo[0] = jnp.dot(h, wb[...], preferred_element_type=F32).astype(BF16)

    zc = jnp.dot(h, wc[...], preferred_element_type=F32)
    qw = WA_HEADS * WA_DIM
    kw = WA_KV_HEADS * WA_DIM
    y = _norm_rope(zc[:, 0:qw], gmq[...], qnc[...], cq[...], uq[...], dq[...], WA_DIM, WA_DIM // 4) * WA_DIM ** -0.5
    qco[0] = place(y, selq).astype(BF16)
    y = _norm_rope(zc[:, qw:qw + kw], gmk[...], knc[...], ck[...], uk[...], dk[...], WA_DIM, WA_DIM // 4)
    kco[0] = place(y, selk).astype(BF16)
    vco[0] = place(zc[:, qw + kw:], selk).astype(BF16)

    zdo[0] = jnp.dot(h, wd[...], preferred_element_type=F32).astype(BF16)
    for k in range(N_BRANCH):
        zgo[0, :, k * d:(k + 1) * d] = jnp.dot(h, wg[:, k * d:(k + 1) * d], preferred_element_type=F32).astype(BF16)


def _in_proj(x, mod, g, ws, tables, consts):
    bx, lx, d = x.shape
    tm = min(512, lx)
    const = lambda b, i: (0, 0)
    row = lambda b, i: (b, i, 0)
    once = lambda a: pl.BlockSpec(a.shape, const, pipeline_mode=pl.Buffered(1))
    widths = [DA_HEADS * LANES] * 5 + [W_B, WA_HEADS * LANES, WA_KV_HEADS * LANES, WA_KV_HEADS * LANES, W_D, N_BRANCH * d]
    return pl.pallas_call(
        _inproj_kernel,
        grid=(bx, lx // tm),
        in_specs=[pl.BlockSpec((1, tm, d), row), pl.BlockSpec((1, 1, mod.shape[-1]), lambda b, i: (b, 0, 0)),
                  pl.BlockSpec((1, d), const)]
                 + [once(w) for w in ws]
                 + [pl.BlockSpec((tm, t.shape[1]), lambda b, i: (i, 0)) for t in tables]
                 + [once(c) for c in consts],
        out_specs=[pl.BlockSpec((1, tm, w), row) for w in widths],
        out_shape=[jax.ShapeDtypeStruct((bx, lx, w), BF16) for w in widths],
        compiler_params=_cp("parallel", "parallel"),
        name="in_proj",
    )(x, mod, g.reshape(1, d), *ws, *tables, *consts)


def _da_lambda(lam_ref, lam_init):
    lv = lam_ref[...]
    return (jnp.exp(jnp.sum(lv[0:1] * lv[1:2], keepdims=True)) - jnp.exp(jnp.sum(lv[2:3] * lv[3:4], keepdims=True))
            + lam_init)


def _dattn_kernel(lam_ref, sg_ref, q1_ref, q2_ref, *rest, lam_init, online, nsrc, tk_max):
    srcs = [rest[3 * s:3 * s + 3] for s in range(nsrc)]
    o_ref, acc1, acc2 = rest[3 * nsrc:]
    dn = (((1,), (1,)), ((), ()))
    q1 = q1_ref[0]
    q2 = q2_ref[0]
    tq = q1.shape[0]
    acc1[...] = jnp.zeros_like(acc1)
    acc2[...] = jnp.zeros_like(acc2)
    carry = (jnp.full((tq, 1), NEG_INF, F32),) * 2 if online else 0

    for k1_ref, k2_ref, v_ref in srcs:
        tk = min(tk_max, k1_ref.shape[1])

        def body(j, c, k1_ref=k1_ref, k2_ref=k2_ref, v_ref=v_ref, tk=tk):
            rows = pl.ds(pl.multiple_of(j * tk, tk), tk)
            vc = v_ref[0, rows, :]
            s1 = lax.dot_general(q1, k1_ref[0, rows, :], dn, preferred_element_type=F32)
            s2 = lax.dot_general(q2, k2_ref[0, rows, :], dn, preferred_element_type=F32)
            if online:
                m1, m2 = c
                n1 = jnp.maximum(m1, jnp.max(s1, axis=-1, keepdims=True))
                n2 = jnp.maximum(m2, jnp.max(s2, axis=-1, keepdims=True))
                acc1[...] = jnp.exp2(m1 - n1) * acc1[...] + jnp.dot(jnp.exp2(s1 - n1).astype(BF16), vc,
                                                                    preferred_element_type=F32)
                acc2[...] = jnp.exp2(m2 - n2) * acc2[...] + jnp.dot(jnp.exp2(s2 - n2).astype(BF16), vc,
                                                                    preferred_element_type=F32)
                return n1, n2
            acc1[...] += jnp.dot(jnp.exp2(s1).astype(BF16), vc, preferred_element_type=F32)
            acc2[...] += jnp.dot(jnp.exp2(s2).astype(BF16), vc, preferred_element_type=F32)
            return c

        carry = lax.fori_loop(0, k1_ref.shape[1] // tk, body, carry)

    dv = DA_VDIM
    a1 = acc1[...]
    a2 = acc2[...]
    lam = _da_lambda(lam_ref, lam_init)
    o = a1 * (1.0 / a1[:, dv:dv + 1]) - a2 * (lam / a2[:, dv:dv + 1])
    o = jnp.where(lax.broadcasted_iota(jnp.int32, o.shape, 1) < dv, o, 0.0)
    o = o * lax.rsqrt(jnp.sum(o * o, axis=-1, keepdims=True) * (1.0 / dv) + EPS)
    o_ref[0] = (o * (sg_ref[...] * (1.0 - lam_init))).astype(BF16)


def _diff_attention(q1, q2, srcs, fixed, lam_p, subln, lam_init):
    b, lq, _ = q1.shape
    h = DA_HEADS
    const = lambda b_, h_, i: (0, 0)
    sg = jnp.pad(subln, (0, LANES - DA_VDIM)).reshape(1, LANES)
    flat = [a for src in srcs for a in src]

    def call(online, *args):
        tq = min(256 if online else 2048, lq)
        qs = pl.BlockSpec((1, tq, LANES), lambda b_, h_, i: (b_, i, h_))
        return pl.pallas_call(
            functools.partial(_dattn_kernel, lam_init=lam_init, online=online, nsrc=len(srcs),
                              tk_max=256 if online else 512),
            grid=(b, h, lq // tq),
            in_specs=[pl.BlockSpec(lam_p.shape, const), pl.BlockSpec((1, LANES), const), qs, qs]
                     + [pl.BlockSpec((1, a.shape[1], LANES), lambda b_, h_, i: (b_, 0, h_)) for a in flat],
            out_specs=qs,
            out_shape=jax.ShapeDtypeStruct((b, lq, h * LANES), BF16),
            scratch_shapes=[pltpu.VMEM((tq, LANES), F32)] * 2,
            compiler_params=_cp("parallel", "parallel", "arbitrary"),
            name="diff_attention_online" if online else "diff_attention",
        )(*args)

    return lax.cond(fixed, functools.partial(call, False), functools.partial(call, True),
                    lam_p, sg, q1, q2, *flat)


def _hy_prep_kernel(zp_ref, zc_ref, zn_ref, w_ref, b_ref, p_ref, x0_ref, ext):
    i = pl.program_id(1)
    last = pl.num_programs(1) - 1
    tl = zc_ref.shape[1]
    h = 2 * SUBLANES
    ext[0:h] = jnp.where(i == 0, 0.0, zp_ref[0].astype(F32))
    ext[h:h + tl] = zc_ref[0].astype(F32)
    ext[h + tl:] = jnp.where(i == last, 0.0, zn_ref[0].astype(F32))
    w = w_ref[...]
    u = (ext[pl.ds(h - 1, tl), :] * w[0:1] + ext[pl.ds(h, tl), :] * w[1:2] + ext[pl.ds(h + 1, tl), :] * w[2:3]
         + b_ref[...])
    hw = HY_WIDTH
    x0_ref[0] = u[:, 0:hw]
    p_ref[0] = u[:, 2 * hw:3 * hw] * u[:, hw:2 * hw]


def _hy_prep(zb, conv_w, conv_b):
    bx, lx, w = zb.shape
    tl = min(256, lx)
    h = 2 * SUBLANES
    nh = lx // h
    per = tl // h
    out = jax.ShapeDtypeStruct((bx, lx, HY_WIDTH), F32)
    return pl.pallas_call(
        _hy_prep_kernel,
        grid=(bx, lx // tl),
        in_specs=[pl.BlockSpec((1, h, w), lambda b, i: (b, jnp.maximum(i * per - 1, 0), 0)),
                  pl.BlockSpec((1, tl, w), lambda b, i: (b, i, 0)),
                  pl.BlockSpec((1, h, w), lambda b, i: (b, jnp.minimum((i + 1) * per, nh - 1), 0)),
                  pl.BlockSpec(conv_w.shape, lambda b, i: (0, 0)),
                  pl.BlockSpec((1, w), lambda b, i: (0, 0))],
        out_specs=[pl.BlockSpec((1, tl, HY_WIDTH), lambda b, i: (b, i, 0))] * 2,
        out_shape=[out, out],
        scratch_shapes=[pltpu.VMEM((tl + 2 * h, w), F32)],
        compiler_params=_cp("parallel", "parallel"),
        name="hyena_prep",
    )(zb, zb, zb, conv_w, conv_b.reshape(1, w))


def _hy_filter_kernel(feat_ref, w1, b1, w2, b2, w3, b3, w4, fr_ref, dl_ref, filt_ref, ssq_ref, *, s):
    i = pl.program_id(0)
    tr = feat_ref.shape[0]
    feat = feat_ref[...]
    fr = fr_ref[...]
    dot = lambda a, w: jnp.dot(a, w[...], precision=HI, preferred_element_type=F32)
    a = jnp.sin(fr * (dot(feat, w1) + b1[...]))
    a = jnp.sin(fr * (dot(a, w2) + b2[...]))
    a = jnp.sin(fr * (dot(a, w3) + b3[...]))
    coef = dot(a, w4)
    n = i * tr + lax.broadcasted_iota(jnp.int32, (tr, 1), 0)
    window = jnp.exp(-feat[:, 0:1] * dl_ref[...]) + HY_SHIFT
    half = jnp.where(n < s, coef[:, :HY_WIDTH], coef[:, HY_WIDTH:])
    filt = jnp.where(n == s, 0.0, half * window)
    filt_ref[...] = filt

    @pl.when(i == 0)
    def _():
        ssq_ref[...] = jnp.zeros_like(ssq_ref)

    ssq_ref[...] += jnp.sum(filt * filt, axis=0, keepdims=True)


def _hy_feat(s):
    t = jnp.linspace(0.0, 1.0, s, dtype=F32)[:, None]
    w = (2.0 * math.pi / s) * jnp.arange(s, dtype=F32)[:, None]
    bands = jnp.linspace(1e-4, HY_BANDS - 1, HY_BANDS, dtype=F32)[None, :]
    feat = jnp.concatenate([t, jnp.cos(w * bands), jnp.sin(w * bands)], axis=-1)
    feat = jnp.concatenate([feat, feat[:1], feat[:0:-1]], axis=0)
    return jnp.pad(feat, ((0, 0), (0, LANES - feat.shape[1])))


def _hy_filter(s, feat, w1, b1, w2, b2, w3, b3, w4, freq):
    n = 2 * s
    tr = min(512, n)
    deltas = jnp.abs(jnp.linspace(math.log(HY_TARGET) / HY_FAST_DECAY, math.log(HY_TARGET) / HY_SLOW_DECAY,
                                  HY_WIDTH, dtype=F32)).reshape(1, HY_WIDTH)
    w1p = jnp.pad(w1, ((0, LANES - w1.shape[0]), (0, 0)))
    row = lambda a: a.reshape(1, -1)
    args = (feat, w1p, row(b1), w2, row(b2), w3, row(b3), w4, row(freq), deltas)
    const = lambda i: (0, 0)
    return pl.pallas_call(
        functools.partial(_hy_filter_kernel, s=s),
        grid=(n // tr,),
        in_specs=[pl.BlockSpec((tr, LANES), lambda i: (i, 0))] + [pl.BlockSpec(a.shape, const) for a in args[1:]],
        out_specs=[pl.BlockSpec((tr, HY_WIDTH), lambda i: (i, 0)), pl.BlockSpec((1, HY_WIDTH), const)],
        out_shape=[jax.ShapeDtypeStruct((n, HY_WIDTH), F32), jax.ShapeDtypeStruct((1, HY_WIDTH), F32)],
        compiler_params=_cp("arbitrary"),
        name="hyena_filter",
    )(*args)


def _dft_factors(n):
    lg = n.bit_length() - 1
    assert 1 << lg == n
    n1 = 1 << (lg // 2)
    return n1, n // n1


def _dft_tables(n):
    n1, n2 = _dft_factors(n)
    ia = jnp.arange(n1, dtype=jnp.int32)
    ang1 = (2.0 * math.pi / n1) * ((ia[:, None] * ia[None, :]) % n1).astype(F32)
    f1 = jnp.concatenate([jnp.cos(ang1), -jnp.sin(ang1)], axis=0)
    c = jnp.arange(n1, dtype=jnp.int32)[:, None, None]
    d = jnp.arange(n2, dtype=jnp.int32)[None, :, None]
    b = jnp.arange(n2, dtype=jnp.int32)[None, None, :]
    ang = (2.0 * math.pi / n) * ((b * (c + n1 * d)) % n).astype(F32)
    re, im = jnp.cos(ang), -jnp.sin(ang)
    m1 = jnp.concatenate([jnp.concatenate([re, -im], axis=2), jnp.concatenate([im, re], axis=2)], axis=1)
    m2 = jnp.swapaxes(m1, 1, 2)
    ang4 = ang1[: n1 // 2]
    f4 = jnp.concatenate([jnp.cos(ang4), -jnp.sin(ang4)], axis=1) * (1.0 / n)
    return f1, _hi_lo(m1), _hi_lo(m2), _hi_lo(f4)


def _dot3(mh, ml, a):
    ah, al = _hi_lo(a)
    return (jnp.dot(mh, ah, preferred_element_type=F32) + jnp.dot(mh, al, preferred_element_type=F32)
            + jnp.dot(ml, ah, preferred_element_type=F32))


def _hy_stage1_kernel(fh_ref, fl_ref, x_ref, re_ref, im_ref):
    n1 = re_ref.shape[1]
    y = _dot3(fh_ref[...], fl_ref[...], x_ref[0])
    re_ref[0] = y[:n1]
    im_ref[0] = y[n1:]


def _hy_stage1(f, x2d):
    bx, k, lanes = x2d.shape
    n1 = f[0].shape[0] // 2
    tl = min(2048, lanes)
    out = jax.ShapeDtypeStruct((bx, n1, lanes), F32)
    return pl.pallas_call(
        _hy_stage1_kernel,
        grid=(bx, lanes // tl),
        in_specs=[pl.BlockSpec(f[0].shape, lambda b, j: (0, 0))] * 2 + [pl.BlockSpec((1, k, tl), lambda b, j: (b, 0, j))],
        out_specs=[pl.BlockSpec((1, n1, tl), lambda b, j: (b, 0, j))] * 2,
        out_shape=[out, out],
        compiler_params=_cp("parallel", "parallel"),
        name="hyena_dft_stage1",
    )(*f, x2d)


def _hy_mid_kernel(m1h_ref, m1l_ref, m2h_ref, m2l_ref, re_ref, im_ref, hre_ref, him_ref, ore_ref, oim_ref):
    n2 = re_ref.shape[2]
    for c in range(re_ref.shape[1]):
        a = jnp.concatenate([re_ref[0, c], im_ref[0, c]], axis=0)
        x = _dot3(m1h_ref[c], m1l_ref[c], a)
        xre, xim = x[:n2], x[n2:]
        hre, him = hre_ref[c], him_ref[c]
        y = jnp.concatenate([xre * hre - xim * him, xre * him + xim * hre], axis=0)
        bb = _dot3(m2h_ref[c], m2l_ref[c], y)
        ore_ref[0, c] = bb[:n2]
        oim_ref[0, c] = bb[n2:]


def _hy_spec_kernel(m1h_ref, m1l_ref, re_ref, im_ref, rs_ref, ore_ref, oim_ref):
    n2 = re_ref.shape[2]
    for c in range(re_ref.shape[1]):
        a = jnp.concatenate([re_ref[0, c], im_ref[0, c]], axis=0)
        x = _dot3(m1h_ref[c], m1l_ref[c], a) * rs_ref[...]
        ore_ref[c] = x[:n2]
        oim_ref[c] = x[n2:]


def _hy_filter_spectrum(m1, are, aim, rs):
    _, n1, n2, w = are.shape
    cb = min(4, n1)
    blk = pl.BlockSpec((1, cb, n2, w), lambda c: (0, c, 0, 0))
    mblk = pl.BlockSpec((cb, 2 * n2, 2 * n2), lambda c: (c, 0, 0))
    oblk = pl.BlockSpec((cb, n2, w), lambda c: (c, 0, 0))
    out = jax.ShapeDtypeStruct((n1, n2, w), F32)
    return pl.pallas_call(
        _hy_spec_kernel,
        grid=(n1 // cb,),
        in_specs=[mblk, mblk, blk, blk, pl.BlockSpec((1, w), lambda c: (0, 0))],
        out_specs=[oblk, oblk],
        out_shape=[out, out],
        compiler_params=_cp("parallel"),
        name="hyena_filter_spectrum",
    )(*m1, are, aim, rs)


def _hy_mid(m1, m2, are, aim, hre, him):
    bx, n1, n2, w = are.shape
    cb = min(4, n1)
    blk = pl.BlockSpec((1, cb, n2, w), lambda c, b: (b, c, 0, 0))
    mblk = pl.BlockSpec((cb, 2 * n2, 2 * n2), lambda c, b: (c, 0, 0))
    hblk = pl.BlockSpec((cb, n2, w), lambda c, b: (c, 0, 0))
    out = jax.ShapeDtypeStruct((bx, n1, n2, w), F32)
    return pl.pallas_call(
        _hy_mid_kernel,
        grid=(n1 // cb, bx),
        in_specs=[mblk, mblk, mblk, mblk, blk, blk, hblk, hblk],
        out_specs=[blk, blk],
        out_shape=[out, out],
        compiler_params=_cp("parallel", "parallel"),
        name="hyena_dft_mid",
    )(*m1, *m2, are, aim, hre, him)


def _hy_last_kernel(fh_ref, fl_ref, re_ref, im_ref, p_ref, x0_ref, bias_ref, o_ref):
    bb = jnp.concatenate([re_ref[0], im_ref[0]], axis=0)
    y = _dot3(fh_ref[...], fl_ref[...], bb)
    o_ref[0] = (x0_ref[0] * (y + p_ref[0] * bias_ref[...])).astype(BF16)


def _hy_last(f4, bre, bim, p2d, x02d, bias2d):
    bx, n1, lanes = bre.shape
    k = n1 // 2
    tl = min(2048, lanes)
    big = pl.BlockSpec((1, n1, tl), lambda b, j: (b, 0, j))
    small = pl.BlockSpec((1, k, tl), lambda b, j: (b, 0, j))
    return pl.pallas_call(
        _hy_last_kernel,
        grid=(bx, lanes // tl),
        in_specs=[pl.BlockSpec(f4[0].shape, lambda b, j: (0, 0))] * 2 + [big, big, small, small,
                  pl.BlockSpec((1, tl), lambda b, j: (0, j))],
        out_specs=small,
        out_shape=jax.ShapeDtypeStruct((bx, k, lanes), BF16),
        compiler_params=_cp("parallel", "parallel"),
        name="hyena_dft_last",
    )(*f4, bre, bim, p2d, x02d, bias2d)


def _hyena(zb, conv_w, conv_b, filt_params, hy_bias, consts):
    bx, lx, _ = zb.shape
    feat, (f1, m1, m2, f4) = consts
    n = 2 * lx
    n1, n2 = _dft_factors(n)
    w = HY_WIDTH
    filt, ssq = _hy_filter(lx, feat, *filt_params)
    rs = lax.rsqrt(ssq + EPS)
    fre, fim = _hy_stage1(_hi_lo(f1), filt.reshape(1, n1, n2 * w))
    hre, him = _hy_filter_spectrum(m1, fre.reshape(1, n1, n2, w), fim.reshape(1, n1, n2, w), rs)
    p, x0 = _hy_prep(zb, conv_w, conv_b)
    k = n1 // 2
    are, aim = _hy_stage1(_hi_lo(f1[:, :k]), p.reshape(bx, k, n2 * w))
    bre, bim = _hy_mid(m1, m2, are.reshape(bx, n1, n2, w), aim.reshape(bx, n1, n2, w), hre, him)
    bias2d = jnp.tile(hy_bias, n2).reshape(1, n2 * w)
    y = _hy_last(f4, bre.reshape(bx, n1, n2 * w), bim.reshape(bx, n1, n2 * w), p.reshape(bx, k, n2 * w),
                 x0.reshape(bx, k, n2 * w), bias2d)
    return y.reshape(bx, lx, w)


def _wattn_kernel(sink_ref, q_ref, *rest, banded):
    if banded:
        bias_ref, kp_ref, kc_ref, kn_ref, kx_ref, vp_ref, vc_ref, vn_ref, vx_ref, o_ref = rest
    else:
        kx_ref, vx_ref, o_ref = rest
    h = pl.program_id(1)
    i = pl.program_id(2)
    last = pl.num_programs(2) - 1
    qb = q_ref.shape[1]
    q = jnp.concatenate([q_ref[0, :, :LANES], q_ref[0, :, LANES:]], axis=0)
    dn = (((1,), (1,)), ((), ()))
    parts = []
    if banded:
        kk = jnp.concatenate([kp_ref[0], kc_ref[0], kn_ref[0]], axis=0)
        vv = jnp.concatenate([vp_ref[0], vc_ref[0], vn_ref[0]], axis=0)
        s = lax.dot_general(q, kk, dn, preferred_element_type=F32)
        c = lax.broadcasted_iota(jnp.int32, (1, qb + 2 * BLOCK), 1) - BLOCK
        edge = jnp.where(((c < 0) & (i == 0)) | ((c >= qb) & (i == last)), NEG_INF, 0.0)
        parts.append((s + bias_ref[...] + edge, vv))
    parts.append((lax.dot_general(q, kx_ref[0], dn, preferred_element_type=F32), vx_ref[0]))
    top = lax.broadcasted_iota(jnp.int32, (2 * qb, 1), 0) < qb
    snk = jnp.where(top, sink_ref[h * WA_GROUP], sink_ref[h * WA_GROUP + 1])
    m = snk
    for s, _ in parts:
        m = jnp.maximum(m, jnp.max(s, axis=-1, keepdims=True))
    den = jnp.exp(snk - m)
    acc = jnp.zeros((2 * qb, LANES), F32)
    for s, v in parts:
        p = jnp.exp(s - m)
        den = den + jnp.sum(p, axis=-1, keepdims=True)
        acc = acc + jnp.dot(p.astype(BF16), v, preferred_element_type=F32)
    o = (acc / den).astype(BF16)
    o_ref[0] = jnp.concatenate([o[:qb], o[qb:]], axis=1)


def _window_attention(q, k, v, kx, vx, sink, banded):
    b, lq, _ = q.shape
    cx = kx.shape[1]
    qb = min(512, lq)
    per = qb // BLOCK
    nblk = lq // BLOCK
    side = lambda f: pl.BlockSpec((1, BLOCK, LANES), f)
    prev = side(lambda b_, h, i: (b_, jnp.maximum(i * per - 1, 0), h))
    nxt = side(lambda b_, h, i: (b_, jnp.minimum((i + 1) * per, nblk - 1), h))
    cur = pl.BlockSpec((1, qb, LANES), lambda b_, h, i: (b_, i, h))
    ctx = pl.BlockSpec((1, cx, LANES), lambda b_, h, i: (b_, 0, h))
    qspec = pl.BlockSpec((1, qb, WA_GROUP * LANES), lambda b_, h, i: (b_, i, h))
    if banded:
        r = jnp.arange(WA_GROUP * qb)[:, None] % qb
        c = jnp.arange(qb + 2 * BLOCK)[None, :] - BLOCK
        bias = jnp.where(jnp.abs(r - c) <= BLOCK, 0.0, NEG_INF).astype(F32)
        specs = [pl.BlockSpec(bias.shape, lambda b_, h, i: (0, 0)), prev, cur, nxt, ctx, prev, cur, nxt, ctx]
        args = (bias, k, k, k, kx, v, v, v, vx)
    else:
        specs, args = [ctx, ctx], (kx, vx)
    return pl.pallas_call(
        functools.partial(_wattn_kernel, banded=banded),
        grid=(b, WA_KV_HEADS, lq // qb),
        in_specs=[pl.BlockSpec(memory_space=pltpu.SMEM), qspec] + specs,
        out_specs=qspec,
        out_shape=jax.ShapeDtypeStruct((b, lq, WA_HEADS * LANES), BF16),
        compiler_params=_cp("parallel", "parallel", "arbitrary"),
        name="window_attention",
    )(sink, q, *args)


def _conf_kernel(zp_ref, zc_ref, zn_ref, w_ref, b_ref, lg_ref, lb_ref, o_ref, ext, *, halo):
    i = pl.program_id(1)
    last = pl.num_programs(1) - 1
    tl = zc_ref.shape[1]
    cw = CF_WIDTH

    def glu(z_ref):
        z = z_ref[0].astype(F32)
        return z[:, :cw] * jax.nn.sigmoid(z[:, cw:])

    ext[0:halo] = jnp.where(i == 0, 0.0, glu(zp_ref))
    ext[halo:halo + tl] = glu(zc_ref)
    ext[halo + tl:] = jnp.where(i == last, 0.0, glu(zn_ref))
    w = w_ref[...]
    u = jnp.zeros((tl, cw), F32) + b_ref[...]
    for j in range(CF_TAPS):
        u = u + ext[pl.ds(halo - CF_TAPS // 2 + j, tl), :] * w[j:j + 1]
    uc = u - jnp.mean(u, axis=-1, keepdims=True)
    y = uc * lax.rsqrt(jnp.mean(uc * uc, axis=-1, keepdims=True) + EPS) * lg_ref[...] + lb_ref[...]
    o_ref[0] = _silu(y).astype(BF16)


def _conformer(zd, dw_w, dw_b, ln_g, ln_b):
    bx, lx, w = zd.shape
    tl = min(256, lx)
    halo = 2 * SUBLANES
    nh = lx // halo
    per = tl // halo
    row = lambda a: a.reshape(1, -1)
    const = lambda b, i: (0, 0)
    return pl.pallas_call(
        functools.partial(_conf_kernel, halo=halo),
        grid=(bx, lx // tl),
        in_specs=[pl.BlockSpec((1, halo, w), lambda b, i: (b, jnp.maximum(i * per - 1, 0), 0)),
                  pl.BlockSpec((1, tl, w), lambda b, i: (b, i, 0)),
                  pl.BlockSpec((1, halo, w), lambda b, i: (b, jnp.minimum((i + 1) * per, nh - 1), 0)),
                  pl.BlockSpec(dw_w.shape, const)] + [pl.BlockSpec((1, CF_WIDTH), const)] * 3,
        out_specs=pl.BlockSpec((1, tl, CF_WIDTH), lambda b, i: (b, i, 0)),
        out_shape=jax.ShapeDtypeStruct((bx, lx, CF_WIDTH), BF16),
        scratch_shapes=[pltpu.VMEM((tl + 2 * halo, CF_WIDTH), F32)],
        compiler_params=_cp("parallel", "parallel"),
        name="conformer_conv",
    )(zd, zd, zd, dw_w, row(dw_b), row(ln_g), row(ln_b))


def _merge_kernel(x_ref, m_ref, ya, yb, yc, yd, zg_ref, bg_ref, wa, wb, wc, wd, wo_ref, o_ref, *, goff):
    d = x_ref.shape[-1]
    acc = jnp.zeros(x_ref.shape[1:], F32)
    for i, (y, w) in enumerate(((ya, wa), (yb, wb), (yc, wc), (yd, wd))):
        gate = jax.nn.sigmoid(zg_ref[0, :, i * d:(i + 1) * d].astype(F32) + bg_ref[:, i * d:(i + 1) * d])
        acc = acc + gate * jnp.dot(y[0], w[...], preferred_element_type=F32)
    out = jnp.dot(acc.astype(BF16), wo_ref[...], preferred_element_type=F32)
    o_ref[0] = x_ref[0] + m_ref[0, :, goff:goff + d] * out


def _merge(x, mod, ys, zg, b_gate, wbs, w_out):
    bx, lx, d = x.shape
    tm = min(256, lx)
    row = lambda b, i: (b, i, 0)
    const = lambda b, i: (0, 0)
    return pl.pallas_call(
        functools.partial(_merge_kernel, goff=2 * d),
        grid=(bx, lx // tm),
        in_specs=[pl.BlockSpec((1, tm, d), row), pl.BlockSpec((1, 1, mod.shape[-1]), lambda b, i: (b, 0, 0))]
                 + [pl.BlockSpec((1, tm, y.shape[-1]), row) for y in ys]
                 + [pl.BlockSpec((1, tm, N_BRANCH * d), row), pl.BlockSpec((1, N_BRANCH * d), const)]
                 + [pl.BlockSpec(w.shape, const) for w in wbs] + [pl.BlockSpec(w_out.shape, const)],
        out_specs=pl.BlockSpec((1, tm, d), row),
        out_shape=jax.ShapeDtypeStruct(x.shape, F32),
        input_output_aliases={0: 0},
        compiler_params=_cp("parallel", "parallel"),
        name="merge",
    )(x, mod, *ys, zg, b_gate.reshape(1, -1), *wbs, w_out)


def _slot_rows(w, group):
    n = w.shape[0] // group
    return jnp.pad(w.reshape(n, group, -1), ((0, 0), (0, LANES - group), (0, 0))).reshape(n * LANES, -1)


def _moe_kernel(x_ref, m_ref, g_ref, wr_ref, br_ref, w1_ref, w3_ref, w2_ref, o_ref, h_sc, gate_sc, acc_sc):
    e = pl.program_id(2)
    d = x_ref.shape[-1]
    tm = x_ref.shape[1]
    lane = lax.broadcasted_iota(jnp.int32, (tm, LANES), 1).astype(F32)

    @pl.when(e == 0)
    def _():
        h = _rms(x_ref[0]) * g_ref[...] * (1.0 + m_ref[0, :, 4 * d:5 * d]) + m_ref[0, :, 3 * d:4 * d]
        h_sc[...] = h.astype(BF16)
        lg = jnp.dot(h, wr_ref[...], precision=HI, preferred_element_type=F32) + br_ref[...]
        isg = lane < N_GROUPS
        gmax = jnp.max(jnp.where(isg, lg, NEG_INF), axis=-1, keepdims=True)
        gi = jnp.min(jnp.where(isg & (lg == gmax), lane, LANES), axis=-1, keepdims=True)
        gw = 1.0 / jnp.sum(jnp.where(isg, jnp.exp(lg - gmax), 0.0), axis=-1, keepdims=True)
        lo = N_GROUPS + gi * EXP_PER_GROUP
        ise = (lane >= lo) & (lane < lo + EXP_PER_GROUP)
        le = jnp.where(ise, lg, NEG_INF)
        m1 = jnp.max(le, axis=-1, keepdims=True)
        i1 = jnp.min(jnp.where(ise & (le == m1), lane, LANES), axis=-1, keepdims=True)
        ise2 = ise & (lane != i1)
        le2 = jnp.where(ise2, lg, NEG_INF)
        m2 = jnp.max(le2, axis=-1, keepdims=True)
        i2 = jnp.min(jnp.where(ise2 & (le2 == m2), lane, LANES), axis=-1, keepdims=True)
        r = jnp.exp(m2 - m1)
        wa = gw / (1.0 + r)
        gate_sc[...] = jnp.where(lane == i1, wa, 0.0) + jnp.where(lane == i2, wa * r, 0.0)
        acc_sc[...] = jnp.zeros_like(acc_sc)

    hb = h_sc[...]
    u = (_silu(jnp.dot(hb, w1_ref[0].astype(BF16), preferred_element_type=F32))
         * jnp.dot(hb, w3_ref[0].astype(BF16), preferred_element_type=F32))
    ge = jnp.sum(jnp.where(lane == (e + N_GROUPS).astype(F32), gate_sc[...], 0.0), axis=-1, keepdims=True)
    acc_sc[...] += ge * jnp.dot(u.astype(BF16), w2_ref[0].astype(BF16), preferred_element_type=F32)

    @pl.when(e == pl.num_programs(2) - 1)
    def _():
        o_ref[0] = x_ref[0] + m_ref[0, :, 5 * d:6 * d] * acc_sc[...]


def _moe(x, mod, g, w_router, b_router, w1, w3, w2):
    bx, lx, d = x.shape
    tm = min(1024, lx)
    ne, _, f = w1.shape
    row = lambda b, i, e: (b, i, 0)
    const = lambda b, i, e: (0, 0)
    return pl.pallas_call(
        _moe_kernel,
        grid=(bx, lx // tm, ne),
        in_specs=[pl.BlockSpec((1, tm, d), row), pl.BlockSpec((1, 1, mod.shape[-1]), lambda b, i, e: (b, 0, 0)),
                  pl.BlockSpec((1, d), const), pl.BlockSpec((d, LANES), const), pl.BlockSpec((1, LANES), const),
                  pl.BlockSpec((1, d, f), lambda b, i, e: (e, 0, 0)), pl.BlockSpec((1, d, f), lambda b, i, e: (e, 0, 0)),
                  pl.BlockSpec((1, f, d), lambda b, i, e: (e, 0, 0))],
        out_specs=pl.BlockSpec((1, tm, d), row),
        out_shape=jax.ShapeDtypeStruct(x.shape, F32),
        scratch_shapes=[pltpu.VMEM((tm, d), BF16), pltpu.VMEM((tm, LANES), F32), pltpu.VMEM((tm, d), F32)],
        input_output_aliases={0: 0},
        compiler_params=_cp("parallel", "parallel", "arbitrary"),
        name="moe",
    )(x, mod, g.reshape(1, d), w_router, b_router, w1, w3, w2)


SC_CORES = 2
SC_SUBCORES = 16
SC_CHUNK = 64
MOE_ROWS = 1024


def _sc_gather(table, idx):
    n = idx.shape[0]
    w = table.shape[1]
    per = n // (SC_CORES * SC_SUBCORES)
    assert per * SC_CORES * SC_SUBCORES == n and per % SC_CHUNK == 0
    mesh = plsc.VectorSubcoreMesh(core_axis_name="c", subcore_axis_name="s")

    @functools.partial(
        pl.kernel, mesh=mesh, out_type=jax.ShapeDtypeStruct((n, w), table.dtype),
        scratch_types=[pltpu.VMEM((SC_CHUNK,), jnp.int32), pltpu.VMEM((SC_CHUNK, w), table.dtype),
                       pltpu.SemaphoreType.DMA],
        name="sc_row_gather")
    def gather(table_hbm, idx_hbm, out_hbm, idx_v, rows_v, sem):
        base = (lax.axis_index("s") * SC_CORES + lax.axis_index("c")) * per

        @pl.loop(0, per // SC_CHUNK)
        def _(j):
            off = pl.multiple_of(base + j * SC_CHUNK, SC_CHUNK)
            pltpu.sync_copy(idx_hbm.at[pl.ds(off, SC_CHUNK)], idx_v)
            pltpu.async_copy(table_hbm.at[idx_v], rows_v, sem).wait()
            pltpu.sync_copy(rows_v, out_hbm.at[pl.ds(off, SC_CHUNK)])

    return gather(table, idx)


def _sc_scatter(rows, idx, n_out):
    n, w = rows.shape
    per = n // (SC_CORES * SC_SUBCORES)
    assert per * SC_CORES * SC_SUBCORES == n and per % SC_CHUNK == 0
    mesh = plsc.VectorSubcoreMesh(core_axis_name="c", subcore_axis_name="s")

    @functools.partial(
        pl.kernel, mesh=mesh, out_type=jax.ShapeDtypeStruct((n_out, w), rows.dtype),
        scratch_types=[pltpu.VMEM((SC_CHUNK,), jnp.int32), pltpu.VMEM((SC_CHUNK, w), rows.dtype),
                       pltpu.SemaphoreType.DMA],
        name="sc_row_scatter")
    def scatter(rows_hbm, idx_hbm, out_hbm, idx_v, rows_v, sem):
        base = (lax.axis_index("s") * SC_CORES + lax.axis_index("c")) * per

        @pl.loop(0, per // SC_CHUNK)
        def _(j):
            off = pl.multiple_of(base + j * SC_CHUNK, SC_CHUNK)
            pltpu.sync_copy(idx_hbm.at[pl.ds(off, SC_CHUNK)], idx_v)
            pltpu.sync_copy(rows_hbm.at[pl.ds(off, SC_CHUNK)], rows_v)
            pltpu.async_copy(rows_v, out_hbm.at[idx_v], sem).wait()

    return scatter(rows, idx)


def _route_kernel(x_ref, m_ref, g_ref, wr_ref, br_ref, rows_ref, gi_ref):
    d = x_ref.shape[-1]
    tm = x_ref.shape[1]
    lane = lax.broadcasted_iota(jnp.int32, (tm, LANES), 1).astype(F32)
    h = _rms(x_ref[0]) * g_ref[...] * (1.0 + m_ref[0, :, 4 * d:5 * d]) + m_ref[0, :, 3 * d:4 * d]
    rows_ref[0, :, :d] = h
    lg = jnp.dot(h, wr_ref[...], precision=HI, preferred_element_type=F32) + br_ref[...]
    isg = lane < N_GROUPS
    gmax = jnp.max(jnp.where(isg, lg, NEG_INF), axis=-1, keepdims=True)
    gi = jnp.min(jnp.where(isg & (lg == gmax), lane, LANES), axis=-1, keepdims=True)
    gw = 1.0 / jnp.sum(jnp.where(isg, jnp.exp(lg - gmax), 0.0), axis=-1, keepdims=True)
    lo = N_GROUPS + gi * EXP_PER_GROUP
    ise = (lane >= lo) & (lane < lo + EXP_PER_GROUP)
    le = jnp.where(ise, lg, NEG_INF)
    m1 = jnp.max(le, axis=-1, keepdims=True)
    i1 = jnp.min(jnp.where(ise & (le == m1), lane, LANES), axis=-1, keepdims=True)
    ise2 = ise & (lane != i1)
    le2 = jnp.where(ise2, lg, NEG_INF)
    m2 = jnp.max(le2, axis=-1, keepdims=True)
    i2 = jnp.min(jnp.where(ise2 & (le2 == m2), lane, LANES), axis=-1, keepdims=True)
    r = jnp.exp(m2 - m1)
    wa = gw / (1.0 + r)
    rows_ref[0, :, d:] = jnp.where(lane == i1 - lo, wa, 0.0) + jnp.where(lane == i2 - lo, wa * r, 0.0)
    gi_ref[0] = gi.astype(jnp.int32)


def _route(x, mod, g, w_router, b_router):
    bx, lx, d = x.shape
    tm = min(512, lx)
    row = lambda b, i: (b, i, 0)
    const = lambda b, i: (0, 0)
    return pl.pallas_call(
        _route_kernel,
        grid=(bx, lx // tm),
        in_specs=[pl.BlockSpec((1, tm, d), row), pl.BlockSpec((1, 1, mod.shape[-1]), lambda b, i: (b, 0, 0)),
                  pl.BlockSpec((1, d), const), pl.BlockSpec((d, LANES), const), pl.BlockSpec((1, LANES), const)],
        out_specs=[pl.BlockSpec((1, tm, d + LANES), row), pl.BlockSpec((1, tm, 1), row)],
        out_shape=[jax.ShapeDtypeStruct((bx, lx, d + LANES), F32), jax.ShapeDtypeStruct((bx, lx, 1), jnp.int32)],
        compiler_params=_cp("parallel", "parallel"),
        name="moe_route",
    )(x, mod, g.reshape(1, d), w_router, b_router)


def _gmoe_kernel(grp_ref, nv_ref, xs_ref, w1_ref, w3_ref, w2_ref, o_ref, h_sc, acc_sc):
    i = pl.program_id(0)
    e = pl.program_id(1)
    tm, d = h_sc.shape
    valid = lax.broadcasted_iota(jnp.int32, (tm, 1), 0) < nv_ref[i]

    @pl.when(nv_ref[i] > 0)
    def _():
        @pl.when(e == 0)
        def _():
            h_sc[...] = jnp.where(valid, xs_ref[:, :d], 0.0).astype(BF16)
            acc_sc[...] = jnp.zeros_like(acc_sc)

        hb = h_sc[...]
        u = (_silu(jnp.dot(hb, w1_ref[0].astype(BF16), preferred_element_type=F32))
             * jnp.dot(hb, w3_ref[0].astype(BF16), preferred_element_type=F32))
        lane = lax.broadcasted_iota(jnp.int32, (tm, LANES), 1)
        ge = jnp.sum(jnp.where(valid & (lane == e), xs_ref[:, d:], 0.0), axis=-1, keepdims=True)
        acc_sc[...] += ge * jnp.dot(u.astype(BF16), w2_ref[0].astype(BF16), preferred_element_type=F32)

    @pl.when(e == pl.num_programs(1) - 1)
    def _():
        o_ref[...] = jnp.where(nv_ref[i] > 0, acc_sc[...], 0.0)


def _grouped_moe(grp, nv, xs, w1, w3, w2):
    p, dw = xs.shape
    d = dw - LANES
    _, _, f = w1.shape
    wmap = lambda i, e, grp, nv: (grp[i] * EXP_PER_GROUP + e, 0, 0)
    rows = lambda i, e, grp, nv: (i, 0)
    return pl.pallas_call(
        _gmoe_kernel,
        grid_spec=pltpu.PrefetchScalarGridSpec(
            num_scalar_prefetch=2,
            grid=(p // MOE_ROWS, EXP_PER_GROUP),
            in_specs=[pl.BlockSpec((MOE_ROWS, dw), rows),
                      pl.BlockSpec((1, d, f), wmap), pl.BlockSpec((1, d, f), wmap), pl.BlockSpec((1, f, d), wmap)],
            out_specs=pl.BlockSpec((MOE_ROWS, d), rows),
            scratch_shapes=[pltpu.VMEM((MOE_ROWS, d), BF16), pltpu.VMEM((MOE_ROWS, d), F32)]),
        out_shape=jax.ShapeDtypeStruct((p, d), F32),
        compiler_params=_cp("arbitrary", "arbitrary"),
        name="moe_experts",
    )(grp, nv, xs, w1, w3, w2)


def _residual_kernel(x_ref, m_ref, y_ref, o_ref):
    d = x_ref.shape[-1]
    o_ref[0] = x_ref[0] + m_ref[0, :, 5 * d:6 * d] * y_ref[0]


def _residual(x, mod, y):
    bx, lx, d = x.shape
    tm = min(1024, lx)
    row = lambda b, i: (b, i, 0)
    return pl.pallas_call(
        _residual_kernel,
        grid=(bx, lx // tm),
        in_specs=[pl.BlockSpec((1, tm, d), row), pl.BlockSpec((1, 1, mod.shape[-1]), lambda b, i: (b, 0, 0)),
                  pl.BlockSpec((1, tm, d), row)],
        out_specs=pl.BlockSpec((1, tm, d), row),
        out_shape=jax.ShapeDtypeStruct(x.shape, F32),
        input_output_aliases={0: 0},
        compiler_params=_cp("parallel", "parallel"),
        name="moe_residual",
    )(x, mod, y)


def _moe_sorted(x, mod, g, w_router, b_router, w1, w3, w2):
    bx, lx, d = x.shape
    t = bx * lx
    rows, gi = _route(x, mod, g, w_router, b_router)
    gi = gi.reshape(t)
    onehot = (gi[:, None] == jnp.arange(N_GROUPS, dtype=jnp.int32)[None, :]).astype(jnp.int32)
    csum = jnp.cumsum(onehot, axis=0)
    counts = csum[-1]
    rank = jnp.take_along_axis(csum, gi[:, None], axis=1)[:, 0] - 1
    padded = (counts + MOE_ROWS - 1) // MOE_ROWS * MOE_ROWS
    pend = jnp.cumsum(padded)
    pstart = pend - padded
    pos = (pstart[gi] + rank).astype(jnp.int32)
    p = t + N_GROUPS * MOE_ROWS
    bstart = jnp.arange(p // MOE_ROWS, dtype=jnp.int32) * MOE_ROWS
    grp = jnp.minimum(jnp.searchsorted(pend, bstart, side="right"), N_GROUPS - 1).astype(jnp.int32)
    nv = jnp.clip(pstart[grp] + counts[grp] - bstart, 0, MOE_ROWS).astype(jnp.int32)
    xs = _sc_scatter(rows.reshape(t, d + LANES), pos, p)
    ys = _grouped_moe(grp, nv, xs, w1, w3, w2)
    yt = _sc_gather(ys, pos)
    return _residual(x, mod, yt.reshape(bx, lx, d))


def kernel(x, c, ctx, c_ctx, w_mod, b_mod, norm1_g, norm2_g, w_in, b_gate, da_qn, da_kn, da_lam, da_subln, hy_conv_w, hy_conv_b, hf_w1, hf_b1, hf_w2, hf_b2, hf_w3, hf_b3, hf_w4, hf_freq, hy_bias, wa_qn, wa_kn, wa_sink, cf_dw_w, cf_dw_b, cf_ln_g, cf_ln_b, w_branch, w_out, w_rg, b_rg, w_re, b_re, w1, w3, w2):
    b, s, d = x.shape
    cl = ctx.shape[1]
    depth = w_mod.shape[0]
    assert s % 256 == 0 and cl % 256 == 0 and s % GRID_W == 0

    nrow = -(-(b + 1) // SUBLANES) * SUBLANES
    crows = jnp.zeros((nrow, d), F32).at[:b].set(c).at[b].set(c_ctx)
    mods = _mod_vectors(crows, w_mod, b_mod)

    aw, qw, kw = DA_HEADS * DA_DIM, WA_HEADS * WA_DIM, WA_KV_HEADS * WA_DIM
    tab_lat = (*_rope_tables(s, DA_DIM, DA_HEADS), *_rope_tables(s, WA_DIM, WA_HEADS), *_rope_tables(s, WA_DIM, WA_KV_HEADS))
    tab_ctx = (*_unit_tables(cl, aw), *_unit_tables(cl, qw), *_unit_tables(cl, kw))
    hy_lat = (_hy_feat(s), _dft_tables(2 * s))
    hy_ctx = (_hy_feat(cl), _dft_tables(2 * cl))
    gms = (_group_ones(aw, DA_DIM), _group_ones(qw, WA_DIM), _group_ones(kw, WA_DIM))
    sels = (_slot_select(aw, DA_DIM), _slot_select(DA_HEADS * DA_VDIM, DA_VDIM), _slot_select(qw, WA_DIM),
            _slot_select(kw, WA_DIM))
    qvec = _slot_vector(DA_HEADS, DA_DIM, DA_DIM + 1, 1.0)
    vvec = _slot_vector(DA_HEADS, DA_VDIM, DA_VDIM + DA_ONES, 1.0)

    xc = ctx
    for l in range(depth):
        last = l == depth - 1
        lam_init = 0.8 - 0.6 * math.exp(-0.3 * l)
        mod_x = mods[l, :b][:, None, :]
        mod_c = jnp.broadcast_to(mods[l, b][None, None, :], (b, 1, mods.shape[-1]))
        wl = w_in[l]
        ws = [wl[:, 0:OFF_B].astype(BF16), wl[:, OFF_B:OFF_C].astype(BF16), wl[:, OFF_C:OFF_D].astype(BF16),
              wl[:, OFF_D:OFF_G].astype(BF16), wl[:, OFF_G:].astype(BF16)]
        shift = 1.02 * LOG2E * DA_DIM ** 0.5 * jnp.max(jnp.abs(da_qn[l])) * jnp.max(jnp.abs(da_kn[l]))
        fixed = shift <= DA_SHIFT_MAX
        kvec = _slot_vector(DA_HEADS, DA_DIM, DA_DIM + 1, jnp.where(fixed, -shift, 0.0))
        tile = lambda a, n: jnp.tile(a, n).reshape(1, -1)
        consts = (*gms, tile(da_qn[l], DA_HEADS), tile(da_kn[l], DA_HEADS), tile(wa_qn[l], WA_HEADS),
                  tile(wa_kn[l], WA_KV_HEADS), *sels, qvec, kvec, vvec)
        q1, q2, k1, k2, v, zb, qc, kc, vc, zd, zg = _in_proj(x, mod_x, norm1_g[l], ws, tab_lat, consts)
        q1x, q2x, k1x, k2x, vx, zbx, qcx, kcx, vcx, zdx, zgx = _in_proj(xc, mod_c, norm1_g[l], ws, tab_ctx, consts)
        filt_params = (hf_w1[l], hf_b1[l], hf_w2[l], hf_b2[l], hf_w3[l], hf_b3[l], hf_w4[l], hf_freq[l])
        wb4 = w_branch[l].astype(BF16)
        wbs = (_slot_rows(wb4[0], DA_VDIM), wb4[1], _slot_rows(wb4[2], WA_DIM), wb4[3])
        wo = w_out[l].astype(BF16)

        ya = _diff_attention(q1, q2, [(k1, k2, v), (k1x, k2x, vx)], fixed, da_lam[l], da_subln[l], lam_init)
        yb = _hyena(zb, hy_conv_w[l], hy_conv_b[l], filt_params, hy_bias[l], hy_lat)
        yc_ = _window_attention(qc, kc, vc, kcx, vcx, wa_sink[l], True)
        yd = _conformer(zd, cf_dw_w[l], cf_dw_b[l], cf_ln_g[l], cf_ln_b[l])
        x = _merge(x, mod_x, (ya, yb, yc_, yd), zg, b_gate[l], wbs, wo)

        w_router = jnp.pad(jnp.concatenate([w_rg[l], w_re[l]], axis=1), ((0, 0), (0, LANES - N_GROUPS - N_EXPERTS)))
        b_router = jnp.pad(jnp.concatenate([b_rg[l], b_re[l]]), (0, LANES - N_GROUPS - N_EXPERTS)).reshape(1, LANES)
        ew = (w1[l], w3[l], w2[l])

        if not last:
            yca = _diff_attention(q1x, q2x, [(k1x, k2x, vx)], fixed, da_lam[l], da_subln[l], lam_init)
            ycb = _hyena(zbx, hy_conv_w[l], hy_conv_b[l], filt_params, hy_bias[l], hy_ctx)
            ycc = _window_attention(qcx, kcx, vcx, kcx, vcx, wa_sink[l], False)
            ycd = _conformer(zdx, cf_dw_w[l], cf_dw_b[l], cf_ln_g[l], cf_ln_b[l])
            xc = _merge(xc, mod_c, (yca, ycb, ycc, ycd), zgx, b_gate[l], wbs, wo)
            xc = _moe(xc.reshape(1, b * cl, d), mod_c[:1], norm2_g[l], w_router, b_router, *ew).reshape(b, cl, d)
        x = _moe_sorted(x, mod_x, norm2_g[l], w_router, b_router, *ew)
    return x
```

```python
import functools
import math

import jax
import jax.numpy as jnp
from jax import lax
from jax.experimental import pallas as pl
from jax.experimental.pallas import tpu as pltpu
from jax.experimental.pallas import tpu_sc as plsc

F32 = jnp.float32
BF16 = jnp.bfloat16
HI = lax.Precision.HIGHEST

GRID_W = 64
BLOCK = 128
ROPE_BASE = 10000.0
EPS = 1e-6
NEG_INF = -1e30

DA_HEADS = 4
DA_DIM = 32
DA_VDIM = 64
HY_WIDTH = 256
HY_BANDS = 16
HY_FF = 64
HY_SHIFT = 0.05
HY_FAST_DECAY = 0.3
HY_SLOW_DECAY = 1.5
HY_TARGET = 1e-2
WA_HEADS = 4
WA_KV_HEADS = 2
WA_GROUP = 2
WA_DIM = 64
CF_WIDTH = 256
CF_TAPS = 31
N_BRANCH = 4
BRANCH_W = 256
N_GROUPS = 4
EXP_PER_GROUP = 4
N_EXPERTS = 16

W_A = 4 * DA_HEADS * DA_DIM + DA_HEADS * DA_VDIM
W_B = 3 * HY_WIDTH
W_C = (WA_HEADS + 2 * WA_KV_HEADS) * WA_DIM
W_D = 2 * CF_WIDTH
OFF_B = W_A
OFF_C = OFF_B + W_B
OFF_D = OFF_C + W_C
OFF_G = OFF_D + W_D

LOG2E = math.log2(math.e)
LANES = 128
SUBLANES = 8
VMEM_LIMIT = 56 * 1024 * 1024

DA_ONES = 16
DA_SHIFT_MAX = 50.0


def _cp(*sem):
    return pltpu.CompilerParams(dimension_semantics=sem, vmem_limit_bytes=VMEM_LIMIT)


def _rms(xf):
    return xf * lax.rsqrt(jnp.mean(xf * xf, axis=-1, keepdims=True) + EPS)


def _silu(x):
    return x * jax.nn.sigmoid(x)


def _mod_kernel(c_ref, w_ref, b_ref, o_ref):
    s = _silu(c_ref[...])
    o_ref[0] = jnp.dot(s, w_ref[0], precision=HI, preferred_element_type=F32) + b_ref[0]


def _mod_vectors(crows, w_mod, b_mod):
    depth, d, n = w_mod.shape
    r = crows.shape[0]
    tn = 1536
    return pl.pallas_call(
        _mod_kernel,
        grid=(depth, n // tn),
        in_specs=[pl.BlockSpec((r, d), lambda l, j: (0, 0)),
                  pl.BlockSpec((1, d, tn), lambda l, j: (l, 0, j)),
                  pl.BlockSpec((1, 1, tn), lambda l, j: (l, 0, j))],
        out_specs=pl.BlockSpec((1, r, tn), lambda l, j: (l, 0, j)),
        out_shape=jax.ShapeDtypeStruct((depth, r, n), F32),
        compiler_params=_cp("arbitrary", "arbitrary"),
        name="mod_vectors",
    )(crows, w_mod, b_mod.reshape(depth, 1, n))


def _rope_tables(s, d, reps):
    rows = s // GRID_W
    row = jnp.repeat(jnp.arange(rows, dtype=F32), GRID_W)
    col = jnp.tile(jnp.arange(GRID_W, dtype=F32), rows)
    qd = d // 4
    inv = ROPE_BASE ** (-jnp.arange(qd, dtype=F32) / qd)
    ar = row[:, None] * inv[None, :]
    ac = col[:, None] * inv[None, :]
    z = jnp.zeros_like(ar)
    cos = jnp.concatenate([jnp.cos(ar), jnp.cos(ar), jnp.cos(ac), jnp.cos(ac)], axis=-1)
    sin_up = jnp.concatenate([-jnp.sin(ar), z, -jnp.sin(ac), z], axis=-1)
    sin_dn = jnp.concatenate([z, jnp.sin(ar), z, jnp.sin(ac)], axis=-1)
    t = lambda a: jnp.tile(a, (1, reps))
    return t(cos), t(sin_up), t(sin_dn)


def _unit_tables(s, w):
    return jnp.ones((s, w), F32), jnp.zeros((s, w), F32), jnp.zeros((s, w), F32)


def _group_ones(width, group):
    i = jnp.arange(width) // group
    return (i[:, None] == i[None, :]).astype(BF16)


def _hi_lo(a):
    hi = a.astype(BF16)
    return hi, (a - hi.astype(F32)).astype(BF16)


def _slot_select(width, group):
    i = jnp.arange(width)
    dst = (i // group) * LANES + i % group
    return (dst[:, None] == jnp.arange((width // group) * LANES)[None, :]).astype(BF16)


def _slot_vector(n, lo, hi, value):
    j = jnp.arange(n * LANES) % LANES
    return jnp.where((j >= lo) & (j < hi), value, 0.0).astype(F32).reshape(1, n * LANES)


def _norm_rope(x, gmat, gain, cos, sup, sdn, group, qd):
    w = x.shape[-1]
    sh, sl = _hi_lo(x * x)
    ss = (jnp.dot(sh, gmat, preferred_element_type=F32) + jnp.dot(sl, gmat, preferred_element_type=F32)) * (1.0 / group)
    xn = x * lax.rsqrt(ss + EPS) * gain
    return xn * cos + pltpu.roll(xn, w - qd, 1) * sup + pltpu.roll(xn, qd, 1) * sdn


def _inproj_kernel(x_ref, m_ref, g_ref, wa, wb, wc, wd, wg,
                   ca, ua, da, cq, uq, dq, ck, uk, dk, gma, gmq, gmk, qna, kna, qnc, knc,
                   sela, selv, selq, selk, qvec, kvec, vvec, vvecc,
                   q1o, q2o, k1o, k2o, vo, zbo, qco, kco, vco, zdo, zgo):
    d = x_ref.shape[-1]
    x = x_ref[0]
    shift = m_ref[0, :, 0:d]
    scale = m_ref[0, :, d:2 * d]
    h = (_rms(x) * g_ref[...] * (1.0 + scale) + shift).astype(BF16)
    place = lambda y, sel: jnp.dot(y.astype(BF16), sel[...], preferred_element_type=F32)

    za = jnp.dot(h, wa[...], preferred_element_type=F32)
    hw = DA_HEADS * DA_DIM
    cos, sup, sdn, gm = ca[...], ua[...], da[...], gma[...]
    qscale = DA_DIM ** -0.5 * LOG2E
    for t, (o, gain, sc, vec) in enumerate(((q1o, qna, qscale, qvec), (q2o, qna, qscale, qvec),
                                            (k1o, kna, 1.0, kvec), (k2o, kna, 1.0, kvec))):
        y = _norm_rope(za[:, t * hw:(t + 1) * hw], gm, gain[...], cos, sup, sdn, DA_DIM, DA_DIM // 4) * sc
        o[0] = (place(y, sela) + vec[...]).astype(BF16)
    vo[0] = (place(za[:, 4 * hw:], selv) + vvec[...]).astype(BF16)

    zbo[0] = jnp.dot(h, wb[...], preferred_element_type=F32).astype(BF16)

    zc = jnp.dot(h, wc[...], preferred_element_type=F32)
    qw = WA_HEADS * WA_DIM
    kw = WA_KV_HEADS * WA_DIM
    y = _norm_rope(zc[:, 0:qw], gmq[...], qnc[...], cq[...], uq[...], dq[...], WA_DIM, WA_DIM // 4) * (WA_DIM ** -0.5 * LOG2E)
    qco[0] = place(y, selq).astype(BF16)
    y = _norm_rope(zc[:, qw:qw + kw], gmk[...], knc[...], ck[...], uk[...], dk[...], WA_DIM, WA_DIM // 4)
    kco[0] = place(y, selk).astype(BF16)
    vco[0] = (place(zc[:, qw + kw:], selk) + vvecc[...]).astype(BF16)

    zdo[0] = jnp.dot(h, wd[...], preferred_element_type=F32).astype(BF16)
    for k in range(N_BRANCH):
        zgo[0, :, k * d:(k + 1) * d] = jnp.dot(h, wg[:, k * d:(k + 1) * d], preferred_element_type=F32).astype(BF16)


def _in_proj(x, mod, g, ws, tables, consts):
    bx, lx, d = x.shape
    tm = min(512, lx)
    const = lambda b, i: (0, 0)
    row = lambda b, i: (b, i, 0)
    once = lambda a: pl.BlockSpec(a.shape, const, pipeline_mode=pl.Buffered(1))
    widths = [DA_HEADS * LANES] * 5 + [W_B, WA_HEADS * LANES, WA_KV_HEADS * LANES, WA_KV_HEADS * LANES, W_D, N_BRANCH * d]
    return pl.pallas_call(
        _inproj_kernel,
        grid=(bx, lx // tm),
        in_specs=[pl.BlockSpec((1, tm, d), row), pl.BlockSpec((1, 1, mod.shape[-1]), lambda b, i: (b, 0, 0)),
                  pl.BlockSpec((1, d), const)]
                 + [once(w) for w in ws]
                 + [pl.BlockSpec((tm, t.shape[1]), lambda b, i: (i, 0)) for t in tables]
                 + [once(c) for c in consts],
        out_specs=[pl.BlockSpec((1, tm, w), row) for w in widths],
        out_shape=[jax.ShapeDtypeStruct((bx, lx, w), BF16) for w in widths],
        compiler_params=_cp("parallel", "parallel"),
        name="in_proj",
    )(x, mod, g.reshape(1, d), *ws, *tables, *consts)


def _da_lambda(lam_ref, lam_init):
    lv = lam_ref[...]
    return (jnp.exp(jnp.sum(lv[0:1] * lv[1:2], keepdims=True)) - jnp.exp(jnp.sum(lv[2:3] * lv[3:4], keepdims=True))
            + lam_init)


def _dattn_kernel(lam_ref, sg_ref, q1_ref, q2_ref, *rest, lam_init, online, nsrc, tk_max):
    srcs = [rest[3 * s:3 * s + 3] for s in range(nsrc)]
    o_ref, acc1, acc2 = rest[3 * nsrc:]
    dn = (((1,), (1,)), ((), ()))
    q1 = q1_ref[0]
    q2 = q2_ref[0]
    tq = q1.shape[0]
    acc1[...] = jnp.zeros_like(acc1)
    acc2[...] = jnp.zeros_like(acc2)
    carry = (jnp.full((tq, 1), NEG_INF, F32),) * 2 if online else 0

    for k1_ref, k2_ref, v_ref in srcs:
        tk = min(tk_max, k1_ref.shape[1])

        def body(j, c, k1_ref=k1_ref, k2_ref=k2_ref, v_ref=v_ref, tk=tk):
            rows = pl.ds(pl.multiple_of(j * tk, tk), tk)
            vc = v_ref[0, rows, :]
            s1 = lax.dot_general(q1, k1_ref[0, rows, :], dn, preferred_element_type=F32)
            s2 = lax.dot_general(q2, k2_ref[0, rows, :], dn, preferred_element_type=F32)
            if online:
                m1, m2 = c
                n1 = jnp.maximum(m1, jnp.max(s1, axis=-1, keepdims=True))
                n2 = jnp.maximum(m2, jnp.max(s2, axis=-1, keepdims=True))
                acc1[...] = jnp.exp2(m1 - n1) * acc1[...] + jnp.dot(jnp.exp2(s1 - n1).astype(BF16), vc,
                                                                    preferred_element_type=F32)
                acc2[...] = jnp.exp2(m2 - n2) * acc2[...] + jnp.dot(jnp.exp2(s2 - n2).astype(BF16), vc,
                                                                    preferred_element_type=F32)
                return n1, n2
            acc1[...] += jnp.dot(jnp.exp2(s1).astype(BF16), vc, preferred_element_type=F32)
            acc2[...] += jnp.dot(jnp.exp2(s2).astype(BF16), vc, preferred_element_type=F32)
            return c

        carry = lax.fori_loop(0, k1_ref.shape[1] // tk, body, carry)

    dv = DA_VDIM
    a1 = acc1[...]
    a2 = acc2[...]
    lam = _da_lambda(lam_ref, lam_init)
    o = a1 * (1.0 / a1[:, dv:dv + 1]) - a2 * (lam / a2[:, dv:dv + 1])
    o = jnp.where(lax.broadcasted_iota(jnp.int32, o.shape, 1) < dv, o, 0.0)
    o = o * lax.rsqrt(jnp.sum(o * o, axis=-1, keepdims=True) * (1.0 / dv) + EPS)
    o_ref[0] = (o * (sg_ref[...] * (1.0 - lam_init))).astype(BF16)


def _diff_attention(q1, q2, srcs, fixed, lam_p, subln, lam_init):
    b, lq, _ = q1.shape
    h = DA_HEADS
    const = lambda b_, h_, i: (0, 0)
    sg = jnp.pad(subln, (0, LANES - DA_VDIM)).reshape(1, LANES)
    flat = [a for src in srcs for a in src]

    def call(online, *args):
        tq = min(256 if online else 2048, lq)
        qs = pl.BlockSpec((1, tq, LANES), lambda b_, h_, i: (b_, i, h_))
        return pl.pallas_call(
            functools.partial(_dattn_kernel, lam_init=lam_init, online=online, nsrc=len(srcs),
                              tk_max=256 if online else 512),
            grid=(b, h, lq // tq),
            in_specs=[pl.BlockSpec(lam_p.shape, const), pl.BlockSpec((1, LANES), const), qs, qs]
                     + [pl.BlockSpec((1, a.shape[1], LANES), lambda b_, h_, i: (b_, 0, h_)) for a in flat],
            out_specs=qs,
            out_shape=jax.ShapeDtypeStruct((b, lq, h * LANES), BF16),
            scratch_shapes=[pltpu.VMEM((tq, LANES), F32)] * 2,
            compiler_params=_cp("parallel", "parallel", "arbitrary"),
            name="diff_attention_online" if online else "diff_attention",
        )(*args)

    return lax.cond(fixed, functools.partial(call, False), functools.partial(call, True),
                    lam_p, sg, q1, q2, *flat)


def _hy_prep_kernel(zp_ref, zc_ref, zn_ref, w_ref, b_ref, p_ref, x0_ref, ext):
    i = pl.program_id(1)
    last = pl.num_programs(1) - 1
    tl = zc_ref.shape[1]
    h = 2 * SUBLANES
    ext[0:h] = jnp.where(i == 0, 0.0, zp_ref[0].astype(F32))
    ext[h:h + tl] = zc_ref[0].astype(F32)
    ext[h + tl:] = jnp.where(i == last, 0.0, zn_ref[0].astype(F32))
    w = w_ref[...]
    u = (ext[pl.ds(h - 1, tl), :] * w[0:1] + ext[pl.ds(h, tl), :] * w[1:2] + ext[pl.ds(h + 1, tl), :] * w[2:3]
         + b_ref[...])
    hw = HY_WIDTH
    x0_ref[0] = u[:, 0:hw]
    p_ref[0] = u[:, 2 * hw:3 * hw] * u[:, hw:2 * hw]


def _hy_prep(zb, conv_w, conv_b):
    bx, lx, w = zb.shape
    tl = min(256, lx)
    h = 2 * SUBLANES
    nh = lx // h
    per = tl // h
    out = jax.ShapeDtypeStruct((bx, lx, HY_WIDTH), F32)
    return pl.pallas_call(
        _hy_prep_kernel,
        grid=(bx, lx // tl),
        in_specs=[pl.BlockSpec((1, h, w), lambda b, i: (b, jnp.maximum(i * per - 1, 0), 0)),
                  pl.BlockSpec((1, tl, w), lambda b, i: (b, i, 0)),
                  pl.BlockSpec((1, h, w), lambda b, i: (b, jnp.minimum((i + 1) * per, nh - 1), 0)),
                  pl.BlockSpec(conv_w.shape, lambda b, i: (0, 0)),
                  pl.BlockSpec((1, w), lambda b, i: (0, 0))],
        out_specs=[pl.BlockSpec((1, tl, HY_WIDTH), lambda b, i: (b, i, 0))] * 2,
        out_shape=[out, out],
        scratch_shapes=[pltpu.VMEM((tl + 2 * h, w), F32)],
        compiler_params=_cp("parallel", "parallel"),
        name="hyena_prep",
    )(zb, zb, zb, conv_w, conv_b.reshape(1, w))


def _hy_filter_kernel(feat_ref, w1, b1, w2, b2, w3, b3, w4, fr_ref, dl_ref, filt_ref, ssq_ref, *, s):
    i = pl.program_id(0)
    tr = feat_ref.shape[0]
    feat = feat_ref[...]
    fr = fr_ref[...]
    dot = lambda a, w: _dot3(*_hi_lo(a), w[...])
    a = jnp.sin(fr * (dot(feat, w1) + b1[...]))
    a = jnp.sin(fr * (dot(a, w2) + b2[...]))
    a = jnp.sin(fr * (dot(a, w3) + b3[...]))
    coef = dot(a, w4)
    n = i * tr + lax.broadcasted_iota(jnp.int32, (tr, 1), 0)
    window = jnp.exp(-feat[:, 0:1] * dl_ref[...]) + HY_SHIFT
    half = jnp.where(n < s, coef[:, :HY_WIDTH], coef[:, HY_WIDTH:])
    filt = jnp.where(n == s, 0.0, half * window)
    filt_ref[...] = filt

    @pl.when(i == 0)
    def _():
        ssq_ref[...] = jnp.zeros_like(ssq_ref)

    ssq_ref[...] += jnp.sum(filt * filt, axis=0, keepdims=True)


def _hy_feat(s):
    t = jnp.linspace(0.0, 1.0, s, dtype=F32)[:, None]
    w = (2.0 * math.pi / s) * jnp.arange(s, dtype=F32)[:, None]
    bands = jnp.linspace(1e-4, HY_BANDS - 1, HY_BANDS, dtype=F32)[None, :]
    feat = jnp.concatenate([t, jnp.cos(w * bands), jnp.sin(w * bands)], axis=-1)
    feat = jnp.concatenate([feat, feat[:1], feat[:0:-1]], axis=0)
    return jnp.pad(feat, ((0, 0), (0, LANES - feat.shape[1])))


def _hy_filter(s, feat, w1, b1, w2, b2, w3, b3, w4, freq):
    n = 2 * s
    tr = min(512, n)
    deltas = jnp.abs(jnp.linspace(math.log(HY_TARGET) / HY_FAST_DECAY, math.log(HY_TARGET) / HY_SLOW_DECAY,
                                  HY_WIDTH, dtype=F32)).reshape(1, HY_WIDTH)
    w1p = jnp.pad(w1, ((0, LANES - w1.shape[0]), (0, 0)))
    row = lambda a: a.reshape(1, -1)
    args = (feat, w1p, row(b1), w2, row(b2), w3, row(b3), w4, row(freq), deltas)
    const = lambda i: (0, 0)
    return pl.pallas_call(
        functools.partial(_hy_filter_kernel, s=s),
        grid=(n // tr,),
        in_specs=[pl.BlockSpec((tr, LANES), lambda i: (i, 0))] + [pl.BlockSpec(a.shape, const) for a in args[1:]],
        out_specs=[pl.BlockSpec((tr, HY_WIDTH), lambda i: (i, 0)), pl.BlockSpec((1, HY_WIDTH), const)],
        out_shape=[jax.ShapeDtypeStruct((n, HY_WIDTH), F32), jax.ShapeDtypeStruct((1, HY_WIDTH), F32)],
        compiler_params=_cp("arbitrary"),
        name="hyena_filter",
    )(*args)


def _dft_factors(n):
    lg = n.bit_length() - 1
    assert 1 << lg == n
    n1 = 1 << (lg // 2)
    return n1, n // n1


def _dft_tables(n):
    n1, n2 = _dft_factors(n)
    ia = jnp.arange(n1, dtype=jnp.int32)
    ang1 = (2.0 * math.pi / n1) * ((ia[:, None] * ia[None, :]) % n1).astype(F32)
    f1 = jnp.concatenate([jnp.cos(ang1), -jnp.sin(ang1)], axis=0)
    c = jnp.arange(n1, dtype=jnp.int32)[:, None, None]
    d = jnp.arange(n2, dtype=jnp.int32)[None, :, None]
    b = jnp.arange(n2, dtype=jnp.int32)[None, None, :]
    ang = (2.0 * math.pi / n) * ((b * (c + n1 * d)) % n).astype(F32)
    re, im = jnp.cos(ang), -jnp.sin(ang)
    m1 = jnp.concatenate([jnp.concatenate([re, -im], axis=2), jnp.concatenate([im, re], axis=2)], axis=1)
    m2 = jnp.swapaxes(m1, 1, 2)
    ang4 = ang1[: n1 // 2]
    f4 = jnp.concatenate([jnp.cos(ang4), -jnp.sin(ang4)], axis=1) * (1.0 / n)
    return f1, _hi_lo(m1), _hi_lo(m2), _hi_lo(f4)


def _dot3(mh, ml, a):
    ah, al = _hi_lo(a)
    return (jnp.dot(mh, ah, preferred_element_type=F32) + jnp.dot(mh, al, preferred_element_type=F32)
            + jnp.dot(ml, ah, preferred_element_type=F32))


def _hy_stage1_kernel(fh_ref, fl_ref, x_ref, re_ref, im_ref):
    n1 = re_ref.shape[1]
    y = _dot3(fh_ref[...], fl_ref[...], x_ref[0])
    re_ref[0] = y[:n1]
    im_ref[0] = y[n1:]


def _hy_stage1(f, x2d):
    bx, k, lanes = x2d.shape
    n1 = f[0].shape[0] // 2
    tl = min(2048, lanes)
    out = jax.ShapeDtypeStruct((bx, n1, lanes), F32)
    return pl.pallas_call(
        _hy_stage1_kernel,
        grid=(bx, lanes // tl),
        in_specs=[pl.BlockSpec(f[0].shape, lambda b, j: (0, 0))] * 2 + [pl.BlockSpec((1, k, tl), lambda b, j: (b, 0, j))],
        out_specs=[pl.BlockSpec((1, n1, tl), lambda b, j: (b, 0, j))] * 2,
        out_shape=[out, out],
        compiler_params=_cp("parallel", "parallel"),
        name="hyena_dft_stage1",
    )(*f, x2d)


def _hy_mid_kernel(m1h_ref, m1l_ref, m2h_ref, m2l_ref, re_ref, im_ref, hre_ref, him_ref, ore_ref, oim_ref):
    n2 = re_ref.shape[2]
    for c in range(re_ref.shape[1]):
        a = jnp.concatenate([re_ref[0, c], im_ref[0, c]], axis=0)
        x = _dot3(m1h_ref[c], m1l_ref[c], a)
        xre, xim = x[:n2], x[n2:]
        hre, him = hre_ref[c], him_ref[c]
        y = jnp.concatenate([xre * hre - xim * him, xre * him + xim * hre], axis=0)
        bb = _dot3(m2h_ref[c], m2l_ref[c], y)
        ore_ref[0, c] = bb[:n2]
        oim_ref[0, c] = bb[n2:]


def _hy_spec_kernel(m1h_ref, m1l_ref, re_ref, im_ref, rs_ref, ore_ref, oim_ref):
    n2 = re_ref.shape[2]
    for c in range(re_ref.shape[1]):
        a = jnp.concatenate([re_ref[0, c], im_ref[0, c]], axis=0)
        x = _dot3(m1h_ref[c], m1l_ref[c], a) * rs_ref[...]
        ore_ref[c] = x[:n2]
        oim_ref[c] = x[n2:]


def _hy_filter_spectrum(m1, are, aim, rs):
    _, n1, n2, w = are.shape
    cb = min(4, n1)
    blk = pl.BlockSpec((1, cb, n2, w), lambda c: (0, c, 0, 0))
    mblk = pl.BlockSpec((cb, 2 * n2, 2 * n2), lambda c: (c, 0, 0))
    oblk = pl.BlockSpec((cb, n2, w), lambda c: (c, 0, 0))
    out = jax.ShapeDtypeStruct((n1, n2, w), F32)
    return pl.pallas_call(
        _hy_spec_kernel,
        grid=(n1 // cb,),
        in_specs=[mblk, mblk, blk, blk, pl.BlockSpec((1, w), lambda c: (0, 0))],
        out_specs=[oblk, oblk],
        out_shape=[out, out],
        compiler_params=_cp("parallel"),
        name="hyena_filter_spectrum",
    )(*m1, are, aim, rs)


def _hy_mid(m1, m2, are, aim, hre, him):
    bx, n1, n2, w = are.shape
    cb = min(4, n1)
    blk = pl.BlockSpec((1, cb, n2, w), lambda c, b: (b, c, 0, 0))
    mblk = pl.BlockSpec((cb, 2 * n2, 2 * n2), lambda c, b: (c, 0, 0))
    hblk = pl.BlockSpec((cb, n2, w), lambda c, b: (c, 0, 0))
    out = jax.ShapeDtypeStruct((bx, n1, n2, w), F32)
    return pl.pallas_call(
        _hy_mid_kernel,
        grid=(n1 // cb, bx),
        in_specs=[mblk, mblk, mblk, mblk, blk, blk, hblk, hblk],
        out_specs=[blk, blk],
        out_shape=[out, out],
        compiler_params=_cp("parallel", "parallel"),
        name="hyena_dft_mid",
    )(*m1, *m2, are, aim, hre, him)


def _hy_last_kernel(fh_ref, fl_ref, re_ref, im_ref, p_ref, x0_ref, bias_ref, o_ref):
    bb = jnp.concatenate([re_ref[0], im_ref[0]], axis=0)
    y = _dot3(fh_ref[...], fl_ref[...], bb)
    o_ref[0] = (x0_ref[0] * (y + p_ref[0] * bias_ref[...])).astype(BF16)


def _hy_last(f4, bre, bim, p2d, x02d, bias2d):
    bx, n1, lanes = bre.shape
    k = n1 // 2
    tl = min(2048, lanes)
    big = pl.BlockSpec((1, n1, tl), lambda b, j: (b, 0, j))
    small = pl.BlockSpec((1, k, tl), lambda b, j: (b, 0, j))
    return pl.pallas_call(
        _hy_last_kernel,
        grid=(bx, lanes // tl),
        in_specs=[pl.BlockSpec(f4[0].shape, lambda b, j: (0, 0))] * 2 + [big, big, small, small,
                  pl.BlockSpec((1, tl), lambda b, j: (0, j))],
        out_specs=small,
        out_shape=jax.ShapeDtypeStruct((bx, k, lanes), BF16),
        compiler_params=_cp("parallel", "parallel"),
        name="hyena_dft_last",
    )(*f4, bre, bim, p2d, x02d, bias2d)


def _hyena(zb, conv_w, conv_b, filt_params, hy_bias, consts):
    bx, lx, _ = zb.shape
    feat, (f1, m1, m2, f4) = consts
    n = 2 * lx
    n1, n2 = _dft_factors(n)
    w = HY_WIDTH
    filt, ssq = _hy_filter(lx, feat, *filt_params)
    rs = lax.rsqrt(ssq + EPS)
    fre, fim = _hy_stage1(_hi_lo(f1), filt.reshape(1, n1, n2 * w))
    hre, him = _hy_filter_spectrum(m1, fre.reshape(1, n1, n2, w), fim.reshape(1, n1, n2, w), rs)
    p, x0 = _hy_prep(zb, conv_w, conv_b)
    k = n1 // 2
    are, aim = _hy_stage1(_hi_lo(f1[:, :k]), p.reshape(bx, k, n2 * w))
    bre, bim = _hy_mid(m1, m2, are.reshape(bx, n1, n2, w), aim.reshape(bx, n1, n2, w), hre, him)
    bias2d = jnp.tile(hy_bias, n2).reshape(1, n2 * w)
    y = _hy_last(f4, bre.reshape(bx, n1, n2 * w), bim.reshape(bx, n1, n2 * w), p.reshape(bx, k, n2 * w),
                 x0.reshape(bx, k, n2 * w), bias2d)
    return y.reshape(bx, lx, w)


WA_SHIFT_MAX = 35.0


def _wattn_kernel(sk_ref, q_ref, bias_ref, *rest, banded, fixed):
    if banded:
        kp_ref, kc_ref, kn_ref, kx_ref, vp_ref, vc_ref, vn_ref, vx_ref, o_ref = rest
    else:
        kx_ref, vx_ref, o_ref = rest
    h = pl.program_id(1)
    i = pl.program_id(2)
    last = pl.num_programs(2) - 1
    qb = q_ref.shape[1]
    q = jnp.concatenate([q_ref[0, :, :LANES], q_ref[0, :, LANES:]], axis=0)
    if banded:
        kk = jnp.concatenate([kp_ref[0], kc_ref[0], kn_ref[0], kx_ref[0]], axis=0)
        vv = jnp.concatenate([vp_ref[0], vc_ref[0], vn_ref[0], vx_ref[0]], axis=0)
    else:
        kk, vv = kx_ref[0], vx_ref[0]
    s = lax.dot_general(q, kk, (((1,), (1,)), ((), ())), preferred_element_type=F32) + bias_ref[0]
    if banded:
        c = lax.broadcasted_iota(jnp.int32, (1, kk.shape[0]), 1) - BLOCK
        outside = ((c < 0) & (i == 0)) | ((c >= qb) & (c < qb + BLOCK) & (i == last))
        s = s + jnp.where(outside, NEG_INF, 0.0)
    top = lax.broadcasted_iota(jnp.int32, (2 * qb, 1), 0) < qb
    sk = jnp.where(top, sk_ref[h * WA_GROUP], sk_ref[h * WA_GROUP + 1])
    if fixed:
        p, sink_term = jnp.exp2(s), sk
    else:
        m = jnp.maximum(jnp.max(s, axis=-1, keepdims=True), sk)
        p, sink_term = jnp.exp2(s - m), jnp.exp2(sk - m)
    acc = jnp.dot(p.astype(BF16), vv, preferred_element_type=F32)
    o = (acc * (1.0 / (acc[:, WA_DIM:WA_DIM + 1] + sink_term))).astype(BF16)
    o_ref[0] = jnp.concatenate([o[:qb], o[qb:]], axis=1)


def _window_attention(q, k, v, kx, vx, sink, bound, banded):
    b, lq, _ = q.shape
    cx = kx.shape[1]
    qb = min(256, lq)
    per = qb // BLOCK
    nblk = lq // BLOCK
    side = lambda f: pl.BlockSpec((1, BLOCK, LANES), f)
    prev = side(lambda b_, h, i: (b_, jnp.maximum(i * per - 1, 0), h))
    nxt = side(lambda b_, h, i: (b_, jnp.minimum((i + 1) * per, nblk - 1), h))
    cur = pl.BlockSpec((1, qb, LANES), lambda b_, h, i: (b_, i, h))
    ctx = pl.BlockSpec((1, cx, LANES), lambda b_, h, i: (b_, 0, h))
    qspec = pl.BlockSpec((1, qb, WA_GROUP * LANES), lambda b_, h, i: (b_, i, h))
    mask = jnp.zeros((WA_GROUP * qb, cx), F32)
    if banded:
        r = jnp.arange(WA_GROUP * qb)[:, None] % qb
        c = jnp.arange(qb + 2 * BLOCK)[None, :] - BLOCK
        mask = jnp.concatenate([jnp.where(jnp.abs(r - c) <= BLOCK, 0.0, NEG_INF).astype(F32), mask], axis=1)
        specs, args = [prev, cur, nxt, ctx, prev, cur, nxt, ctx], (k, k, k, kx, v, v, v, vx)
    else:
        specs, args = [ctx, ctx], (kx, vx)
    fixed = bound <= WA_SHIFT_MAX
    shift = jnp.maximum(bound, sink)
    rows = jnp.repeat(shift.reshape(WA_KV_HEADS, WA_GROUP), qb, axis=1)[:, :, None]
    bias = mask[None] - jnp.where(fixed, LOG2E * rows, 0.0)
    sk = jnp.where(fixed, jnp.exp2(LOG2E * (sink - shift)), LOG2E * sink)

    def call(fixed_, *ops):
        return pl.pallas_call(
            functools.partial(_wattn_kernel, banded=banded, fixed=fixed_),
            grid=(b, WA_KV_HEADS, lq // qb),
            in_specs=[pl.BlockSpec(memory_space=pltpu.SMEM), qspec,
                      pl.BlockSpec((1,) + bias.shape[1:], lambda b_, h, i: (h, 0, 0))] + specs,
            out_specs=qspec,
            out_shape=jax.ShapeDtypeStruct((b, lq, WA_HEADS * LANES), BF16),
            compiler_params=_cp("parallel", "parallel", "arbitrary"),
            name="window_attention" if fixed_ else "window_attention_online",
        )(*ops)

    return lax.cond(fixed, functools.partial(call, True), functools.partial(call, False), sk, q, bias, *args)


def _conf_kernel(zp_ref, zc_ref, zn_ref, w_ref, b_ref, lg_ref, lb_ref, o_ref, ext, sh, *, halo):
    i = pl.program_id(1)
    last = pl.num_programs(1) - 1
    tl = zc_ref.shape[1]
    cw = CF_WIDTH

    def glu(z_ref):
        z = z_ref[0].astype(F32)
        return z[:, :cw] * jax.nn.sigmoid(z[:, cw:])

    ext[0:halo] = jnp.where(i == 0, 0.0, glu(zp_ref))
    ext[halo:halo + tl] = glu(zc_ref)
    ext[halo + tl:] = jnp.where(i == last, 0.0, glu(zn_ref))
    for r in range(1, SUBLANES):
        sh[r - 1] = ext[pl.ds(r, sh.shape[1]), :]
    w = w_ref[...]
    u = jnp.zeros((tl, cw), F32) + b_ref[...]
    for j in range(CF_TAPS):
        off = halo - CF_TAPS // 2 + j
        base, r = off // SUBLANES * SUBLANES, off % SUBLANES
        tap = ext[pl.ds(base, tl), :] if r == 0 else sh[r - 1, pl.ds(base, tl), :]
        u = u + tap * w[j:j + 1]
    uc = u - jnp.mean(u, axis=-1, keepdims=True)
    y = uc * lax.rsqrt(jnp.mean(uc * uc, axis=-1, keepdims=True) + EPS) * lg_ref[...] + lb_ref[...]
    o_ref[0] = _silu(y).astype(BF16)


def _conformer(zd, dw_w, dw_b, ln_g, ln_b):
    bx, lx, w = zd.shape
    tl = min(256, lx)
    halo = 2 * SUBLANES
    nh = lx // halo
    per = tl // halo
    row = lambda a: a.reshape(1, -1)
    const = lambda b, i: (0, 0)
    return pl.pallas_call(
        functools.partial(_conf_kernel, halo=halo),
        grid=(bx, lx // tl),
        in_specs=[pl.BlockSpec((1, halo, w), lambda b, i: (b, jnp.maximum(i * per - 1, 0), 0)),
                  pl.BlockSpec((1, tl, w), lambda b, i: (b, i, 0)),
                  pl.BlockSpec((1, halo, w), lambda b, i: (b, jnp.minimum((i + 1) * per, nh - 1), 0)),
                  pl.BlockSpec(dw_w.shape, const)] + [pl.BlockSpec((1, CF_WIDTH), const)] * 3,
        out_specs=pl.BlockSpec((1, tl, CF_WIDTH), lambda b, i: (b, i, 0)),
        out_shape=jax.ShapeDtypeStruct((bx, lx, CF_WIDTH), BF16),
        scratch_shapes=[pltpu.VMEM((tl + 2 * halo, CF_WIDTH), F32),
                        pltpu.VMEM((SUBLANES - 1, tl + 2 * halo - SUBLANES, CF_WIDTH), F32)],
        compiler_params=_cp("parallel", "parallel"),
        name="conformer_conv",
    )(zd, zd, zd, dw_w, row(dw_b), row(ln_g), row(ln_b))


def _merge_kernel(x_ref, m_ref, ya, yb, yc, yd, zg_ref, bg_ref, wa, wb, wc, wd, wo_ref, o_ref, *, goff):
    d = x_ref.shape[-1]
    acc = jnp.zeros(x_ref.shape[1:], F32)
    for i, (y, w) in enumerate(((ya, wa), (yb, wb), (yc, wc), (yd, wd))):
        gate = jax.nn.sigmoid(zg_ref[0, :, i * d:(i + 1) * d].astype(F32) + bg_ref[:, i * d:(i + 1) * d])
        acc = acc + gate * jnp.dot(y[0], w[...], preferred_element_type=F32)
    out = jnp.dot(acc.astype(BF16), wo_ref[...], preferred_element_type=F32)
    o_ref[0] = x_ref[0] + m_ref[0, :, goff:goff + d] * out


def _merge(x, mod, ys, zg, b_gate, wbs, w_out):
    bx, lx, d = x.shape
    tm = min(512, lx)
    row = lambda b, i: (b, i, 0)
    const = lambda b, i: (0, 0)
    once = lambda a: pl.BlockSpec(a.shape, const, pipeline_mode=pl.Buffered(1))
    return pl.pallas_call(
        functools.partial(_merge_kernel, goff=2 * d),
        grid=(bx, lx // tm),
        in_specs=[pl.BlockSpec((1, tm, d), row), pl.BlockSpec((1, 1, mod.shape[-1]), lambda b, i: (b, 0, 0))]
                 + [pl.BlockSpec((1, tm, y.shape[-1]), row) for y in ys]
                 + [pl.BlockSpec((1, tm, N_BRANCH * d), row), pl.BlockSpec((1, N_BRANCH * d), const)]
                 + [once(w) for w in wbs] + [once(w_out)],
        out_specs=pl.BlockSpec((1, tm, d), row),
        out_shape=jax.ShapeDtypeStruct(x.shape, F32),
        input_output_aliases={0: 0},
        compiler_params=_cp("parallel", "parallel"),
        name="merge",
    )(x, mod, *ys, zg, b_gate.reshape(1, -1), *wbs, w_out)


def _slot_rows(w, group):
    n = w.shape[0] // group
    return jnp.pad(w.reshape(n, group, -1), ((0, 0), (0, LANES - group), (0, 0))).reshape(n * LANES, -1)


def _moe_kernel(x_ref, m_ref, g_ref, wr_ref, br_ref, w1_ref, w3_ref, w2_ref, o_ref, h_sc, gate_sc, acc_sc):
    e = pl.program_id(2)
    d = x_ref.shape[-1]
    tm = x_ref.shape[1]
    lane = lax.broadcasted_iota(jnp.int32, (tm, LANES), 1).astype(F32)

    @pl.when(e == 0)
    def _():
        h = _rms(x_ref[0]) * g_ref[...] * (1.0 + m_ref[0, :, 4 * d:5 * d]) + m_ref[0, :, 3 * d:4 * d]
        h_sc[...] = h.astype(BF16)
        lg = _dot3(*_hi_lo(h), wr_ref[...]) + br_ref[...]
        isg = lane < N_GROUPS
        gmax = jnp.max(jnp.where(isg, lg, NEG_INF), axis=-1, keepdims=True)
        gi = jnp.min(jnp.where(isg & (lg == gmax), lane, LANES), axis=-1, keepdims=True)
        gw = 1.0 / jnp.sum(jnp.where(isg, jnp.exp(lg - gmax), 0.0), axis=-1, keepdims=True)
        lo = N_GROUPS + gi * EXP_PER_GROUP
        ise = (lane >= lo) & (lane < lo + EXP_PER_GROUP)
        le = jnp.where(ise, lg, NEG_INF)
        m1 = jnp.max(le, axis=-1, keepdims=True)
        i1 = jnp.min(jnp.where(ise & (le == m1), lane, LANES), axis=-1, keepdims=True)
        ise2 = ise & (lane != i1)
        le2 = jnp.where(ise2, lg, NEG_INF)
        m2 = jnp.max(le2, axis=-1, keepdims=True)
        i2 = jnp.min(jnp.where(ise2 & (le2 == m2), lane, LANES), axis=-1, keepdims=True)
        r = jnp.exp(m2 - m1)
        wa = gw / (1.0 + r)
        gate_sc[...] = jnp.where(lane == i1, wa, 0.0) + jnp.where(lane == i2, wa * r, 0.0)
        acc_sc[...] = jnp.zeros_like(acc_sc)

    hb = h_sc[...]
    u = (_silu(jnp.dot(hb, w1_ref[0].astype(BF16), preferred_element_type=F32))
         * jnp.dot(hb, w3_ref[0].astype(BF16), preferred_element_type=F32))
    ge = jnp.sum(jnp.where(lane == (e + N_GROUPS).astype(F32), gate_sc[...], 0.0), axis=-1, keepdims=True)
    acc_sc[...] += ge * jnp.dot(u.astype(BF16), w2_ref[0].astype(BF16), preferred_element_type=F32)

    @pl.when(e == pl.num_programs(2) - 1)
    def _():
        o_ref[0] = x_ref[0] + m_ref[0, :, 5 * d:6 * d] * acc_sc[...]


def _moe(x, mod, g, w_router, b_router, w1, w3, w2):
    bx, lx, d = x.shape
    tm = min(1024, lx)
    ne, _, f = w1.shape
    row = lambda b, i, e: (b, i, 0)
    const = lambda b, i, e: (0, 0)
    return pl.pallas_call(
        _moe_kernel,
        grid=(bx, lx // tm, ne),
        in_specs=[pl.BlockSpec((1, tm, d), row), pl.BlockSpec((1, 1, mod.shape[-1]), lambda b, i, e: (b, 0, 0)),
                  pl.BlockSpec((1, d), const), pl.BlockSpec((d, LANES), const), pl.BlockSpec((1, LANES), const),
                  pl.BlockSpec((1, d, f), lambda b, i, e: (e, 0, 0)), pl.BlockSpec((1, d, f), lambda b, i, e: (e, 0, 0)),
                  pl.BlockSpec((1, f, d), lambda b, i, e: (e, 0, 0))],
        out_specs=pl.BlockSpec((1, tm, d), row),
        out_shape=jax.ShapeDtypeStruct(x.shape, F32),
        scratch_shapes=[pltpu.VMEM((tm, d), BF16), pltpu.VMEM((tm, LANES), F32), pltpu.VMEM((tm, d), F32)],
        input_output_aliases={0: 0},
        compiler_params=_cp("parallel", "parallel", "arbitrary"),
        name="moe",
    )(x, mod, g.reshape(1, d), w_router, b_router, w1, w3, w2)


SC_CORES = 2
SC_SUBCORES = 16
SC_CHUNK = 64
MOE_ROWS = 1024


def _sc_gather(table, idx):
    n = idx.shape[0]
    w = table.shape[1]
    per = n // (SC_CORES * SC_SUBCORES)
    assert per * SC_CORES * SC_SUBCORES == n and per % SC_CHUNK == 0
    mesh = plsc.VectorSubcoreMesh(core_axis_name="c", subcore_axis_name="s")

    @functools.partial(
        pl.kernel, mesh=mesh, out_type=jax.ShapeDtypeStruct((n, w), table.dtype),
        scratch_types=[pltpu.VMEM((SC_CHUNK,), jnp.int32), pltpu.VMEM((SC_CHUNK, w), table.dtype),
                       pltpu.SemaphoreType.DMA],
        name="sc_row_gather")
    def gather(table_hbm, idx_hbm, out_hbm, idx_v, rows_v, sem):
        base = (lax.axis_index("s") * SC_CORES + lax.axis_index("c")) * per

        @pl.loop(0, per // SC_CHUNK)
        def _(j):
            off = pl.multiple_of(base + j * SC_CHUNK, SC_CHUNK)
            pltpu.sync_copy(idx_hbm.at[pl.ds(off, SC_CHUNK)], idx_v)
            pltpu.async_copy(table_hbm.at[idx_v], rows_v, sem).wait()
            pltpu.sync_copy(rows_v, out_hbm.at[pl.ds(off, SC_CHUNK)])

    return gather(table, idx)


def _sc_scatter(rows, idx, n_out):
    n, w = rows.shape
    per = n // (SC_CORES * SC_SUBCORES)
    assert per * SC_CORES * SC_SUBCORES == n and per % SC_CHUNK == 0
    mesh = plsc.VectorSubcoreMesh(core_axis_name="c", subcore_axis_name="s")

    @functools.partial(
        pl.kernel, mesh=mesh, out_type=jax.ShapeDtypeStruct((n_out, w), rows.dtype),
        scratch_types=[pltpu.VMEM((SC_CHUNK,), jnp.int32), pltpu.VMEM((SC_CHUNK, w), rows.dtype),
                       pltpu.SemaphoreType.DMA],
        name="sc_row_scatter")
    def scatter(rows_hbm, idx_hbm, out_hbm, idx_v, rows_v, sem):
        base = (lax.axis_index("s") * SC_CORES + lax.axis_index("c")) * per

        @pl.loop(0, per // SC_CHUNK)
        def _(j):
            off = pl.multiple_of(base + j * SC_CHUNK, SC_CHUNK)
            pltpu.sync_copy(idx_hbm.at[pl.ds(off, SC_CHUNK)], idx_v)
            pltpu.sync_copy(rows_hbm.at[pl.ds(off, SC_CHUNK)], rows_v)
            pltpu.async_copy(rows_v, out_hbm.at[idx_v], sem).wait()

    return scatter(rows, idx)


def _route_kernel(x_ref, m_ref, g_ref, wr_ref, br_ref, rows_ref, gi_ref):
    d = x_ref.shape[-1]
    tm = x_ref.shape[1]
    lane = lax.broadcasted_iota(jnp.int32, (tm, LANES), 1).astype(F32)
    h = _rms(x_ref[0]) * g_ref[...] * (1.0 + m_ref[0, :, 4 * d:5 * d]) + m_ref[0, :, 3 * d:4 * d]
    rows_ref[0, :, :d] = h
    lg = _dot3(*_hi_lo(h), wr_ref[...]) + br_ref[...]
    isg = lane < N_GROUPS
    gmax = jnp.max(jnp.where(isg, lg, NEG_INF), axis=-1, keepdims=True)
    gi = jnp.min(jnp.where(isg & (lg == gmax), lane, LANES), axis=-1, keepdims=True)
    gw = 1.0 / jnp.sum(jnp.where(isg, jnp.exp(lg - gmax), 0.0), axis=-1, keepdims=True)
    lo = N_GROUPS + gi * EXP_PER_GROUP
    ise = (lane >= lo) & (lane < lo + EXP_PER_GROUP)
    le = jnp.where(ise, lg, NEG_INF)
    m1 = jnp.max(le, axis=-1, keepdims=True)
    i1 = jnp.min(jnp.where(ise & (le == m1), lane, LANES), axis=-1, keepdims=True)
    ise2 = ise & (lane != i1)
    le2 = jnp.where(ise2, lg, NEG_INF)
    m2 = jnp.max(le2, axis=-1, keepdims=True)
    i2 = jnp.min(jnp.where(ise2 & (le2 == m2), lane, LANES), axis=-1, keepdims=True)
    r = jnp.exp(m2 - m1)
    wa = gw / (1.0 + r)
    rows_ref[0, :, d:] = jnp.where(lane == i1 - lo, wa, 0.0) + jnp.where(lane == i2 - lo, wa * r, 0.0)
    gi_ref[0] = gi.astype(jnp.int32)


def _route(x, mod, g, w_router, b_router):
    bx, lx, d = x.shape
    tm = min(512, lx)
    row = lambda b, i: (b, i, 0)
    const = lambda b, i: (0, 0)
    return pl.pallas_call(
        _route_kernel,
        grid=(bx, lx // tm),
        in_specs=[pl.BlockSpec((1, tm, d), row), pl.BlockSpec((1, 1, mod.shape[-1]), lambda b, i: (b, 0, 0)),
                  pl.BlockSpec((1, d), const), pl.BlockSpec((d, LANES), const), pl.BlockSpec((1, LANES), const)],
        out_specs=[pl.BlockSpec((1, tm, d + LANES), row), pl.BlockSpec((1, tm, 1), row)],
        out_shape=[jax.ShapeDtypeStruct((bx, lx, d + LANES), F32), jax.ShapeDtypeStruct((bx, lx, 1), jnp.int32)],
        compiler_params=_cp("parallel", "parallel"),
        name="moe_route",
    )(x, mod, g.reshape(1, d), w_router, b_router)


def _gmoe_kernel(grp_ref, nv_ref, xs_ref, w1_ref, w3_ref, w2_ref, o_ref, h_sc, acc_sc):
    i = pl.program_id(0)
    e = pl.program_id(1)
    tm, d = h_sc.shape
    valid = lax.broadcasted_iota(jnp.int32, (tm, 1), 0) < nv_ref[i]

    @pl.when(nv_ref[i] > 0)
    def _():
        @pl.when(e == 0)
        def _():
            h_sc[...] = jnp.where(valid, xs_ref[:, :d], 0.0).astype(BF16)
            acc_sc[...] = jnp.zeros_like(acc_sc)

        hb = h_sc[...]
        u = (_silu(jnp.dot(hb, w1_ref[0].astype(BF16), preferred_element_type=F32))
             * jnp.dot(hb, w3_ref[0].astype(BF16), preferred_element_type=F32))
        lane = lax.broadcasted_iota(jnp.int32, (tm, LANES), 1)
        ge = jnp.sum(jnp.where(valid & (lane == e), xs_ref[:, d:], 0.0), axis=-1, keepdims=True)
        acc_sc[...] += ge * jnp.dot(u.astype(BF16), w2_ref[0].astype(BF16), preferred_element_type=F32)

    @pl.when(e == pl.num_programs(1) - 1)
    def _():
        o_ref[...] = jnp.where(nv_ref[i] > 0, acc_sc[...], 0.0)


def _grouped_moe(grp, nv, xs, w1, w3, w2):
    p, dw = xs.shape
    d = dw - LANES
    _, _, f = w1.shape
    wmap = lambda i, e, grp, nv: (grp[i] * EXP_PER_GROUP + e, 0, 0)
    rows = lambda i, e, grp, nv: (i, 0)
    return pl.pallas_call(
        _gmoe_kernel,
        grid_spec=pltpu.PrefetchScalarGridSpec(
            num_scalar_prefetch=2,
            grid=(p // MOE_ROWS, EXP_PER_GROUP),
            in_specs=[pl.BlockSpec((MOE_ROWS, dw), rows),
                      pl.BlockSpec((1, d, f), wmap), pl.BlockSpec((1, d, f), wmap), pl.BlockSpec((1, f, d), wmap)],
            out_specs=pl.BlockSpec((MOE_ROWS, d), rows),
            scratch_shapes=[pltpu.VMEM((MOE_ROWS, d), BF16), pltpu.VMEM((MOE_ROWS, d), F32)]),
        out_shape=jax.ShapeDtypeStruct((p, d), F32),
        compiler_params=_cp("arbitrary", "arbitrary"),
        name="moe_experts",
    )(grp, nv, xs, w1, w3, w2)


def _residual_kernel(x_ref, m_ref, y_ref, o_ref):
    d = x_ref.shape[-1]
    o_ref[0] = x_ref[0] + m_ref[0, :, 5 * d:6 * d] * y_ref[0]


def _residual(x, mod, y):
    bx, lx, d = x.shape
    tm = min(1024, lx)
    row = lambda b, i: (b, i, 0)
    return pl.pallas_call(
        _residual_kernel,
        grid=(bx, lx // tm),
        in_specs=[pl.BlockSpec((1, tm, d), row), pl.BlockSpec((1, 1, mod.shape[-1]), lambda b, i: (b, 0, 0)),
                  pl.BlockSpec((1, tm, d), row)],
        out_specs=pl.BlockSpec((1, tm, d), row),
        out_shape=jax.ShapeDtypeStruct(x.shape, F32),
        input_output_aliases={0: 0},
        compiler_params=_cp("parallel", "parallel"),
        name="moe_residual",
    )(x, mod, y)


def _moe_sorted(x, mod, g, w_router, b_router, w1, w3, w2):
    bx, lx, d = x.shape
    t = bx * lx
    rows, gi = _route(x, mod, g, w_router, b_router)
    gi = gi.reshape(t)
    onehot = (gi[:, None] == jnp.arange(N_GROUPS, dtype=jnp.int32)[None, :]).astype(jnp.int32)
    csum = jnp.cumsum(onehot, axis=0)
    counts = csum[-1]
    rank = jnp.take_along_axis(csum, gi[:, None], axis=1)[:, 0] - 1
    padded = (counts + MOE_ROWS - 1) // MOE_ROWS * MOE_ROWS
    pend = jnp.cumsum(padded)
    pstart = pend - padded
    pos = (pstart[gi] + rank).astype(jnp.int32)
    p = t + N_GROUPS * MOE_ROWS
    bstart = jnp.arange(p // MOE_ROWS, dtype=jnp.int32) * MOE_ROWS
    grp = jnp.minimum(jnp.searchsorted(pend, bstart, side="right"), N_GROUPS - 1).astype(jnp.int32)
    nv = jnp.clip(pstart[grp] + counts[grp] - bstart, 0, MOE_ROWS).astype(jnp.int32)
    xs = _sc_scatter(rows.reshape(t, d + LANES), pos, p)
    ys = _grouped_moe(grp, nv, xs, w1, w3, w2)
    yt = _sc_gather(ys, pos)
    return _residual(x, mod, yt.reshape(bx, lx, d))


def kernel(x, c, ctx, c_ctx, w_mod, b_mod, norm1_g, norm2_g, w_in, b_gate, da_qn, da_kn, da_lam, da_subln, hy_conv_w, hy_conv_b, hf_w1, hf_b1, hf_w2, hf_b2, hf_w3, hf_b3, hf_w4, hf_freq, hy_bias, wa_qn, wa_kn, wa_sink, cf_dw_w, cf_dw_b, cf_ln_g, cf_ln_b, w_branch, w_out, w_rg, b_rg, w_re, b_re, w1, w3, w2):
    b, s, d = x.shape
    cl = ctx.shape[1]
    depth = w_mod.shape[0]
    assert s % 256 == 0 and cl % 256 == 0 and s % GRID_W == 0

    nrow = -(-(b + 1) // SUBLANES) * SUBLANES
    crows = jnp.zeros((nrow, d), F32).at[:b].set(c).at[b].set(c_ctx)
    mods = _mod_vectors(crows, w_mod, b_mod)

    aw, qw, kw = DA_HEADS * DA_DIM, WA_HEADS * WA_DIM, WA_KV_HEADS * WA_DIM
    tab_lat = (*_rope_tables(s, DA_DIM, DA_HEADS), *_rope_tables(s, WA_DIM, WA_HEADS), *_rope_tables(s, WA_DIM, WA_KV_HEADS))
    tab_ctx = (*_unit_tables(cl, aw), *_unit_tables(cl, qw), *_unit_tables(cl, kw))
    hy_lat = (_hy_feat(s), _dft_tables(2 * s))
    hy_ctx = (_hy_feat(cl), _dft_tables(2 * cl))
    gms = (_group_ones(aw, DA_DIM), _group_ones(qw, WA_DIM), _group_ones(kw, WA_DIM))
    sels = (_slot_select(aw, DA_DIM), _slot_select(DA_HEADS * DA_VDIM, DA_VDIM), _slot_select(qw, WA_DIM),
            _slot_select(kw, WA_DIM))
    qvec = _slot_vector(DA_HEADS, DA_DIM, DA_DIM + 1, 1.0)
    vvec = _slot_vector(DA_HEADS, DA_VDIM, DA_VDIM + DA_ONES, 1.0)
    vvecc = _slot_vector(WA_KV_HEADS, WA_DIM, WA_DIM + DA_ONES, 1.0)

    xc = ctx
    for l in range(depth):
        last = l == depth - 1
        lam_init = 0.8 - 0.6 * math.exp(-0.3 * l)
        mod_x = mods[l, :b][:, None, :]
        mod_c = jnp.broadcast_to(mods[l, b][None, None, :], (b, 1, mods.shape[-1]))
        wl = w_in[l]
        ws = [wl[:, 0:OFF_B].astype(BF16), wl[:, OFF_B:OFF_C].astype(BF16), wl[:, OFF_C:OFF_D].astype(BF16),
              wl[:, OFF_D:OFF_G].astype(BF16), wl[:, OFF_G:].astype(BF16)]
        shift = 1.02 * LOG2E * DA_DIM ** 0.5 * jnp.max(jnp.abs(da_qn[l])) * jnp.max(jnp.abs(da_kn[l]))
        fixed = shift <= DA_SHIFT_MAX
        kvec = _slot_vector(DA_HEADS, DA_DIM, DA_DIM + 1, jnp.where(fixed, -shift, 0.0))
        tile = lambda a, n: jnp.tile(a, n).reshape(1, -1)
        consts = (*gms, tile(da_qn[l], DA_HEADS), tile(da_kn[l], DA_HEADS), tile(wa_qn[l], WA_HEADS),
                  tile(wa_kn[l], WA_KV_HEADS), *sels, qvec, kvec, vvec, vvecc)
        q1, q2, k1, k2, v, zb, qc, kc, vc, zd, zg = _in_proj(x, mod_x, norm1_g[l], ws, tab_lat, consts)
        q1x, q2x, k1x, k2x, vx, zbx, qcx, kcx, vcx, zdx, zgx = _in_proj(xc, mod_c, norm1_g[l], ws, tab_ctx, consts)
        filt_params = (hf_w1[l], hf_b1[l], hf_w2[l], hf_b2[l], hf_w3[l], hf_b3[l], hf_w4[l], hf_freq[l])
        wb4 = w_branch[l].astype(BF16)
        wbs = (_slot_rows(wb4[0], DA_VDIM), wb4[1], _slot_rows(wb4[2], WA_DIM), wb4[3])
        wo = w_out[l].astype(BF16)

        ya = _diff_attention(q1, q2, [(k1, k2, v), (k1x, k2x, vx)], fixed, da_lam[l], da_subln[l], lam_init)
        yb = _hyena(zb, hy_conv_w[l], hy_conv_b[l], filt_params, hy_bias[l], hy_lat)
        wbound = 1.02 * WA_DIM ** 0.5 * jnp.max(jnp.abs(wa_qn[l])) * jnp.max(jnp.abs(wa_kn[l]))
        yc_ = _window_attention(qc, kc, vc, kcx, vcx, wa_sink[l], wbound, True)
        yd = _conformer(zd, cf_dw_w[l], cf_dw_b[l], cf_ln_g[l], cf_ln_b[l])
        x = _merge(x, mod_x, (ya, yb, yc_, yd), zg, b_gate[l], wbs, wo)

        w_router = jnp.pad(jnp.concatenate([w_rg[l], w_re[l]], axis=1), ((0, 0), (0, LANES - N_GROUPS - N_EXPERTS)))
        b_router = jnp.pad(jnp.concatenate([b_rg[l], b_re[l]]), (0, LANES - N_GROUPS - N_EXPERTS)).reshape(1, LANES)
        ew = (w1[l], w3[l], w2[l])

        if not last:
            yca = _diff_attention(q1x, q2x, [(k1x, k2x, vx)], fixed, da_lam[l], da_subln[l], lam_init)
            ycb = _hyena(zbx, hy_conv_w[l], hy_conv_b[l], filt_params, hy_bias[l], hy_ctx)
            ycc = _window_attention(qcx, kcx, vcx, kcx, vcx, wa_sink[l], wbound, False)
            ycd = _conformer(zdx, cf_dw_w[l], cf_dw_b[l], cf_ln_g[l], cf_ln_b[l])
            xc = _merge(xc, mod_c, (yca, ycb, ycc, ycd), zgx, b_gate[l], wbs, wo)
            xc = _moe(xc.reshape(1, b * cl, d), mod_c[:1], norm2_g[l], w_router, b_router, *ew).reshape(b, cl, d)
        x = _moe_sorted(x, mod_x, norm2_g[l], w_router, b_router, *ew)
    return x
```

```python
import functools
import math

import jax
import jax.numpy as jnp
from jax import lax
from jax.experimental import pallas as pl
from jax.experimental.pallas import tpu as pltpu
from jax.experimental.pallas import tpu_sc as plsc

F32 = jnp.float32
BF16 = jnp.bfloat16
HI = lax.Precision.HIGHEST

GRID_W = 64
BLOCK = 128
ROPE_BASE = 10000.0
EPS = 1e-6
NEG_INF = -1e30

DA_HEADS = 4
DA_DIM = 32
DA_VDIM = 64
HY_WIDTH = 256
HY_BANDS = 16
HY_FF = 64
HY_SHIFT = 0.05
HY_FAST_DECAY = 0.3
HY_SLOW_DECAY = 1.5
HY_TARGET = 1e-2
WA_HEADS = 4
WA_KV_HEADS = 2
WA_GROUP = 2
WA_DIM = 64
CF_WIDTH = 256
CF_TAPS = 31
N_BRANCH = 4
BRANCH_W = 256
N_GROUPS = 4
EXP_PER_GROUP = 4
N_EXPERTS = 16

W_A = 4 * DA_HEADS * DA_DIM + DA_HEADS * DA_VDIM
W_B = 3 * HY_WIDTH
W_C = (WA_HEADS + 2 * WA_KV_HEADS) * WA_DIM
W_D = 2 * CF_WIDTH
OFF_B = W_A
OFF_C = OFF_B + W_B
OFF_D = OFF_C + W_C
OFF_G = OFF_D + W_D

LOG2E = math.log2(math.e)
LANES = 128
SUBLANES = 8
VMEM_LIMIT = 56 * 1024 * 1024

DA_ONES = 16
DA_SHIFT_MAX = 50.0


def _cp(*sem):
    return pltpu.CompilerParams(dimension_semantics=sem, vmem_limit_bytes=VMEM_LIMIT)


def _rms(xf):
    return xf * lax.rsqrt(jnp.mean(xf * xf, axis=-1, keepdims=True) + EPS)


def _silu(x):
    return x * jax.nn.sigmoid(x)


def _mod_kernel(c_ref, w_ref, b_ref, o_ref):
    s = _silu(c_ref[...])
    o_ref[0] = jnp.dot(s, w_ref[0], precision=HI, preferred_element_type=F32) + b_ref[0]


def _mod_vectors(crows, w_mod, b_mod):
    depth, d, n = w_mod.shape
    r = crows.shape[0]
    tn = 1536
    return pl.pallas_call(
        _mod_kernel,
        grid=(depth, n // tn),
        in_specs=[pl.BlockSpec((r, d), lambda l, j: (0, 0)),
                  pl.BlockSpec((1, d, tn), lambda l, j: (l, 0, j)),
                  pl.BlockSpec((1, 1, tn), lambda l, j: (l, 0, j))],
        out_specs=pl.BlockSpec((1, r, tn), lambda l, j: (l, 0, j)),
        out_shape=jax.ShapeDtypeStruct((depth, r, n), F32),
        compiler_params=_cp("arbitrary", "arbitrary"),
        name="mod_vectors",
    )(crows, w_mod, b_mod.reshape(depth, 1, n))


def _rope_tables(s, d, reps):
    rows = s // GRID_W
    row = jnp.repeat(jnp.arange(rows, dtype=F32), GRID_W)
    col = jnp.tile(jnp.arange(GRID_W, dtype=F32), rows)
    qd = d // 4
    inv = ROPE_BASE ** (-jnp.arange(qd, dtype=F32) / qd)
    ar = row[:, None] * inv[None, :]
    ac = col[:, None] * inv[None, :]
    z = jnp.zeros_like(ar)
    cos = jnp.concatenate([jnp.cos(ar), jnp.cos(ar), jnp.cos(ac), jnp.cos(ac)], axis=-1)
    sin_up = jnp.concatenate([-jnp.sin(ar), z, -jnp.sin(ac), z], axis=-1)
    sin_dn = jnp.concatenate([z, jnp.sin(ar), z, jnp.sin(ac)], axis=-1)
    t = lambda a: jnp.tile(a, (1, reps))
    return t(cos), t(sin_up), t(sin_dn)


def _unit_tables(s, w):
    return jnp.ones((s, w), F32), jnp.zeros((s, w), F32), jnp.zeros((s, w), F32)


def _group_ones(width, group):
    i = jnp.arange(width) // group
    return (i[:, None] == i[None, :]).astype(BF16)


def _hi_lo(a):
    hi = a.astype(BF16)
    return hi, (a - hi.astype(F32)).astype(BF16)


def _slot_select(width, group):
    i = jnp.arange(width)
    dst = (i // group) * LANES + i % group
    return (dst[:, None] == jnp.arange((width // group) * LANES)[None, :]).astype(BF16)


def _slot_vector(n, lo, hi, value):
    j = jnp.arange(n * LANES) % LANES
    return jnp.where((j >= lo) & (j < hi), value, 0.0).astype(F32).reshape(1, n * LANES)


def _norm_rope(x, gmat, gain, cos, sup, sdn, group, qd):
    w = x.shape[-1]
    sh, sl = _hi_lo(x * x)
    ss = (jnp.dot(sh, gmat, preferred_element_type=F32) + jnp.dot(sl, gmat, preferred_element_type=F32)) * (1.0 / group)
    xn = x * lax.rsqrt(ss + EPS) * gain
    return xn * cos + pltpu.roll(xn, w - qd, 1) * sup + pltpu.roll(xn, qd, 1) * sdn


def _inproj_kernel(x_ref, m_ref, g_ref, wa, wb, wc, wd, wg,
                   ca, ua, da, cq, uq, dq, ck, uk, dk, gma, gmq, gmk, qna, kna, qnc, knc,
                   sela, selv, selq, selk, qvec, kvec, vvec, vvecc,
                   q1o, q2o, k1o, k2o, vo, zbo, qco, kco, vco, zdo, zgo):
    d = x_ref.shape[-1]
    x = x_ref[0]
    shift = m_ref[0, :, 0:d]
    scale = m_ref[0, :, d:2 * d]
    h = (_rms(x) * g_ref[...] * (1.0 + scale) + shift).astype(BF16)
    place = lambda y, sel: jnp.dot(y.astype(BF16), sel[...], preferred_element_type=F32)

    za = jnp.dot(h, wa[...], preferred_element_type=F32)
    hw = DA_HEADS * DA_DIM
    cos, sup, sdn, gm = ca[...], ua[...], da[...], gma[...]
    qscale = DA_DIM ** -0.5 * LOG2E
    for t, (o, gain, sc, vec) in enumerate(((q1o, qna, qscale, qvec), (q2o, qna, qscale, qvec),
                                            (k1o, kna, 1.0, kvec), (k2o, kna, 1.0, kvec))):
        y = _norm_rope(za[:, t * hw:(t + 1) * hw], gm, gain[...], cos, sup, sdn, DA_DIM, DA_DIM // 4) * sc
        o[0] = (place(y, sela) + vec[...]).astype(BF16)
    vo[0] = (place(za[:, 4 * hw:], selv) + vvec[...]).astype(BF16)

    zbo[0] = jnp.dot(h, wb[...], preferred_element_type=F32).astype(BF16)

    zc = jnp.dot(h, wc[...], preferred_element_type=F32)
    qw = WA_HEADS * WA_DIM
    kw = WA_KV_HEADS * WA_DIM
    y = _norm_rope(zc[:, 0:qw], gmq[...], qnc[...], cq[...], uq[...], dq[...], WA_DIM, WA_DIM // 4) * (WA_DIM ** -0.5 * LOG2E)
    qco[0] = place(y, selq).astype(BF16)
    y = _norm_rope(zc[:, qw:qw + kw], gmk[...], knc[...], ck[...], uk[...], dk[...], WA_DIM, WA_DIM // 4)
    kco[0] = place(y, selk).astype(BF16)
    vco[0] = (place(zc[:, qw + kw:], selk) + vvecc[...]).astype(BF16)

    zdo[0] = jnp.dot(h, wd[...], preferred_element_type=F32).astype(BF16)
    for k in range(N_BRANCH):
        zgo[0, :, k * d:(k + 1) * d] = jnp.dot(h, wg[:, k * d:(k + 1) * d], preferred_element_type=F32).astype(BF16)


def _in_proj(x, mod, g, ws, tables, consts):
    bx, lx, d = x.shape
    tm = min(512, lx)
    const = lambda b, i: (0, 0)
    row = lambda b, i: (b, i, 0)
    once = lambda a: pl.BlockSpec(a.shape, const, pipeline_mode=pl.Buffered(1))
    widths = [DA_HEADS * LANES] * 5 + [W_B, WA_HEADS * LANES, WA_KV_HEADS * LANES, WA_KV_HEADS * LANES, W_D, N_BRANCH * d]
    return pl.pallas_call(
        _inproj_kernel,
        grid=(bx, lx // tm),
        in_specs=[pl.BlockSpec((1, tm, d), row), pl.BlockSpec((1, 1, mod.shape[-1]), lambda b, i: (b, 0, 0)),
                  pl.BlockSpec((1, d), const)]
                 + [once(w) for w in ws]
                 + [pl.BlockSpec((tm, t.shape[1]), lambda b, i: (i, 0)) for t in tables]
                 + [once(c) for c in consts],
        out_specs=[pl.BlockSpec((1, tm, w), row) for w in widths],
        out_shape=[jax.ShapeDtypeStruct((bx, lx, w), BF16) for w in widths],
        compiler_params=_cp("parallel", "parallel"),
        name="in_proj",
    )(x, mod, g.reshape(1, d), *ws, *tables, *consts)


def _da_lambda(lam_ref, lam_init):
    lv = lam_ref[...]
    return (jnp.exp(jnp.sum(lv[0:1] * lv[1:2], keepdims=True)) - jnp.exp(jnp.sum(lv[2:3] * lv[3:4], keepdims=True))
            + lam_init)


def _dattn_kernel(lam_ref, sg_ref, q1_ref, q2_ref, *rest, lam_init, online, nsrc, tk_max):
    srcs = [rest[3 * s:3 * s + 3] for s in range(nsrc)]
    o_ref, acc1, acc2 = rest[3 * nsrc:]
    dn = (((1,), (1,)), ((), ()))
    q1 = q1_ref[0]
    q2 = q2_ref[0]
    tq = q1.shape[0]
    acc1[...] = jnp.zeros_like(acc1)
    acc2[...] = jnp.zeros_like(acc2)
    carry = (jnp.full((tq, 1), NEG_INF, F32),) * 2 if online else 0

    for k1_ref, k2_ref, v_ref in srcs:
        tk = min(tk_max, k1_ref.shape[1])

        def body(j, c, k1_ref=k1_ref, k2_ref=k2_ref, v_ref=v_ref, tk=tk):
            rows = pl.ds(pl.multiple_of(j * tk, tk), tk)
            vc = v_ref[0, rows, :]
            s1 = lax.dot_general(q1, k1_ref[0, rows, :], dn, preferred_element_type=F32)
            s2 = lax.dot_general(q2, k2_ref[0, rows, :], dn, preferred_element_type=F32)
            if online:
                m1, m2 = c
                n1 = jnp.maximum(m1, jnp.max(s1, axis=-1, keepdims=True))
                n2 = jnp.maximum(m2, jnp.max(s2, axis=-1, keepdims=True))
                acc1[...] = jnp.exp2(m1 - n1) * acc1[...] + jnp.dot(jnp.exp2(s1 - n1).astype(BF16), vc,
                                                                    preferred_element_type=F32)
                acc2[...] = jnp.exp2(m2 - n2) * acc2[...] + jnp.dot(jnp.exp2(s2 - n2).astype(BF16), vc,
                                                                    preferred_element_type=F32)
                return n1, n2
            acc1[...] += jnp.dot(jnp.exp2(s1).astype(BF16), vc, preferred_element_type=F32)
            acc2[...] += jnp.dot(jnp.exp2(s2).astype(BF16), vc, preferred_element_type=F32)
            return c

        carry = lax.fori_loop(0, k1_ref.shape[1] // tk, body, carry)

    dv = DA_VDIM
    a1 = acc1[...]
    a2 = acc2[...]
    lam = _da_lambda(lam_ref, lam_init)
    o = a1 * (1.0 / a1[:, dv:dv + 1]) - a2 * (lam / a2[:, dv:dv + 1])
    o = jnp.where(lax.broadcasted_iota(jnp.int32, o.shape, 1) < dv, o, 0.0)
    o = o * lax.rsqrt(jnp.sum(o * o, axis=-1, keepdims=True) * (1.0 / dv) + EPS)
    o_ref[0] = (o * (sg_ref[...] * (1.0 - lam_init))).astype(BF16)


def _diff_attention(q1, q2, srcs, fixed, lam_p, subln, lam_init):
    b, lq, _ = q1.shape
    h = DA_HEADS
    const = lambda b_, h_, i: (0, 0)
    sg = jnp.pad(subln, (0, LANES - DA_VDIM)).reshape(1, LANES)
    flat = [a for src in srcs for a in src]

    def call(online, *args):
        tq = min(256 if online else 2048, lq)
        qs = pl.BlockSpec((1, tq, LANES), lambda b_, h_, i: (b_, i, h_))
        return pl.pallas_call(
            functools.partial(_dattn_kernel, lam_init=lam_init, online=online, nsrc=len(srcs),
                              tk_max=256 if online else 512),
            grid=(b, h, lq // tq),
            in_specs=[pl.BlockSpec(lam_p.shape, const), pl.BlockSpec((1, LANES), const), qs, qs]
                     + [pl.BlockSpec((1, a.shape[1], LANES), lambda b_, h_, i: (b_, 0, h_)) for a in flat],
            out_specs=qs,
            out_shape=jax.ShapeDtypeStruct((b, lq, h * LANES), BF16),
            scratch_shapes=[pltpu.VMEM((tq, LANES), F32)] * 2,
            compiler_params=_cp("parallel", "parallel", "arbitrary"),
            name="diff_attention_online" if online else "diff_attention",
        )(*args)

    return lax.cond(fixed, functools.partial(call, False), functools.partial(call, True),
                    lam_p, sg, q1, q2, *flat)


def _hy_prep_kernel(zp_ref, zc_ref, zn_ref, w_ref, b_ref, p_ref, x0_ref, ext):
    i = pl.program_id(1)
    last = pl.num_programs(1) - 1
    tl = zc_ref.shape[1]
    h = 2 * SUBLANES
    ext[0:h] = jnp.where(i == 0, 0.0, zp_ref[0].astype(F32))
    ext[h:h + tl] = zc_ref[0].astype(F32)
    ext[h + tl:] = jnp.where(i == last, 0.0, zn_ref[0].astype(F32))
    w = w_ref[...]
    u = (ext[pl.ds(h - 1, tl), :] * w[0:1] + ext[pl.ds(h, tl), :] * w[1:2] + ext[pl.ds(h + 1, tl), :] * w[2:3]
         + b_ref[...])
    hw = HY_WIDTH
    x0_ref[0] = u[:, 0:hw]
    p_ref[0] = u[:, 2 * hw:3 * hw] * u[:, hw:2 * hw]


def _hy_prep(zb, conv_w, conv_b):
    bx, lx, w = zb.shape
    tl = min(256, lx)
    h = 2 * SUBLANES
    nh = lx // h
    per = tl // h
    out = jax.ShapeDtypeStruct((bx, lx, HY_WIDTH), F32)
    return pl.pallas_call(
        _hy_prep_kernel,
        grid=(bx, lx // tl),
        in_specs=[pl.BlockSpec((1, h, w), lambda b, i: (b, jnp.maximum(i * per - 1, 0), 0)),
                  pl.BlockSpec((1, tl, w), lambda b, i: (b, i, 0)),
                  pl.BlockSpec((1, h, w), lambda b, i: (b, jnp.minimum((i + 1) * per, nh - 1), 0)),
                  pl.BlockSpec(conv_w.shape, lambda b, i: (0, 0)),
                  pl.BlockSpec((1, w), lambda b, i: (0, 0))],
        out_specs=[pl.BlockSpec((1, tl, HY_WIDTH), lambda b, i: (b, i, 0))] * 2,
        out_shape=[out, out],
        scratch_shapes=[pltpu.VMEM((tl + 2 * h, w), F32)],
        compiler_params=_cp("parallel", "parallel"),
        name="hyena_prep",
    )(zb, zb, zb, conv_w, conv_b.reshape(1, w))


def _hy_filter_kernel(feat_ref, w1, b1, w2, b2, w3, b3, w4, fr_ref, dl_ref, filt_ref, ssq_ref, *, s):
    i = pl.program_id(0)
    tr = feat_ref.shape[0]
    feat = feat_ref[...]
    fr = fr_ref[...]
    dot = lambda a, w: _dot3(*_hi_lo(a), w[...])
    a = jnp.sin(fr * (dot(feat, w1) + b1[...]))
    a = jnp.sin(fr * (dot(a, w2) + b2[...]))
    a = jnp.sin(fr * (dot(a, w3) + b3[...]))
    coef = dot(a, w4)
    n = i * tr + lax.broadcasted_iota(jnp.int32, (tr, 1), 0)
    window = jnp.exp(-feat[:, 0:1] * dl_ref[...]) + HY_SHIFT
    half = jnp.where(n < s, coef[:, :HY_WIDTH], coef[:, HY_WIDTH:])
    filt = jnp.where(n == s, 0.0, half * window)
    filt_ref[...] = filt

    @pl.when(i == 0)
    def _():
        ssq_ref[...] = jnp.zeros_like(ssq_ref)

    ssq_ref[...] += jnp.sum(filt * filt, axis=0, keepdims=True)


def _hy_feat(s):
    t = jnp.linspace(0.0, 1.0, s, dtype=F32)[:, None]
    w = (2.0 * math.pi / s) * jnp.arange(s, dtype=F32)[:, None]
    bands = jnp.linspace(1e-4, HY_BANDS - 1, HY_BANDS, dtype=F32)[None, :]
    feat = jnp.concatenate([t, jnp.cos(w * bands), jnp.sin(w * bands)], axis=-1)
    feat = jnp.concatenate([feat, feat[:1], feat[:0:-1]], axis=0)
    return jnp.pad(feat, ((0, 0), (0, LANES - feat.shape[1])))


def _hy_filter(s, feat, w1, b1, w2, b2, w3, b3, w4, freq):
    n = 2 * s
    tr = min(512, n)
    deltas = jnp.abs(jnp.linspace(math.log(HY_TARGET) / HY_FAST_DECAY, math.log(HY_TARGET) / HY_SLOW_DECAY,
                                  HY_WIDTH, dtype=F32)).reshape(1, HY_WIDTH)
    w1p = jnp.pad(w1, ((0, LANES - w1.shape[0]), (0, 0)))
    row = lambda a: a.reshape(1, -1)
    args = (feat, w1p, row(b1), w2, row(b2), w3, row(b3), w4, row(freq), deltas)
    const = lambda i: (0, 0)
    return pl.pallas_call(
        functools.partial(_hy_filter_kernel, s=s),
        grid=(n // tr,),
        in_specs=[pl.BlockSpec((tr, LANES), lambda i: (i, 0))] + [pl.BlockSpec(a.shape, const) for a in args[1:]],
        out_specs=[pl.BlockSpec((tr, HY_WIDTH), lambda i: (i, 0)), pl.BlockSpec((1, HY_WIDTH), const)],
        out_shape=[jax.ShapeDtypeStruct((n, HY_WIDTH), F32), jax.ShapeDtypeStruct((1, HY_WIDTH), F32)],
        compiler_params=_cp("arbitrary"),
        name="hyena_filter",
    )(*args)


def _dft_factors(n):
    lg = n.bit_length() - 1
    assert 1 << lg == n
    n1 = 1 << (lg // 2)
    return n1, n // n1


def _dft_tables(n):
    n1, n2 = _dft_factors(n)
    ia = jnp.arange(n1, dtype=jnp.int32)
    ang1 = (2.0 * math.pi / n1) * ((ia[:, None] * ia[None, :]) % n1).astype(F32)
    f1 = jnp.concatenate([jnp.cos(ang1), -jnp.sin(ang1)], axis=0)
    c = jnp.arange(n1, dtype=jnp.int32)[:, None, None]
    d = jnp.arange(n2, dtype=jnp.int32)[None, :, None]
    b = jnp.arange(n2, dtype=jnp.int32)[None, None, :]
    ang = (2.0 * math.pi / n) * ((b * (c + n1 * d)) % n).astype(F32)
    re, im = jnp.cos(ang), -jnp.sin(ang)
    m1 = jnp.concatenate([jnp.concatenate([re, -im], axis=2), jnp.concatenate([im, re], axis=2)], axis=1)
    m2 = jnp.swapaxes(m1, 1, 2)
    ang4 = ang1[: n1 // 2]
    f4 = jnp.concatenate([jnp.cos(ang4), -jnp.sin(ang4)], axis=1) * (1.0 / n)
    return f1, _hi_lo(m1), _hi_lo(m2), _hi_lo(f4)


def _dot3(mh, ml, a):
    ah, al = _hi_lo(a)
    return (jnp.dot(mh, ah, preferred_element_type=F32) + jnp.dot(mh, al, preferred_element_type=F32)
            + jnp.dot(ml, ah, preferred_element_type=F32))


HY_TB = SUBLANES


def _hy_stage1_kernel(fh_ref, fl_ref, x_ref, re_ref, im_ref):
    n1 = re_ref.shape[1]
    for bb in range(x_ref.shape[2]):
        y = _dot3(fh_ref[...], fl_ref[...], x_ref[0, :, bb, :])
        re_ref[0, :, bb, :] = y[:n1]
        im_ref[0, :, bb, :] = y[n1:]


def _hy_stage1(f, x4):
    bx, k, n2, w = x4.shape
    n1 = f[0].shape[0] // 2
    tb = HY_TB
    out = jax.ShapeDtypeStruct((bx, n1, n2, w), F32)
    return pl.pallas_call(
        _hy_stage1_kernel,
        grid=(bx, n2 // tb),
        in_specs=[pl.BlockSpec(f[0].shape, lambda b, j: (0, 0))] * 2
                 + [pl.BlockSpec((1, k, tb, w), lambda b, j: (b, 0, j, 0))],
        out_specs=[pl.BlockSpec((1, n1, tb, w), lambda b, j: (b, 0, j, 0))] * 2,
        out_shape=[out, out],
        compiler_params=_cp("parallel", "parallel"),
        name="hyena_dft_stage1",
    )(*f, x4)


def _hy_mid_kernel(m1h_ref, m1l_ref, m2h_ref, m2l_ref, re_ref, im_ref, hre_ref, him_ref, ore_ref, oim_ref):
    n2 = re_ref.shape[2]
    for c in range(re_ref.shape[1]):
        a = jnp.concatenate([re_ref[0, c], im_ref[0, c]], axis=0)
        x = _dot3(m1h_ref[c], m1l_ref[c], a)
        xre, xim = x[:n2], x[n2:]
        hre, him = hre_ref[c], him_ref[c]
        y = jnp.concatenate([xre * hre - xim * him, xre * him + xim * hre], axis=0)
        bb = _dot3(m2h_ref[c], m2l_ref[c], y)
        ore_ref[0, c] = bb[:n2]
        oim_ref[0, c] = bb[n2:]


def _hy_spec_kernel(m1h_ref, m1l_ref, re_ref, im_ref, rs_ref, ore_ref, oim_ref):
    n2 = re_ref.shape[2]
    for c in range(re_ref.shape[1]):
        a = jnp.concatenate([re_ref[0, c], im_ref[0, c]], axis=0)
        x = _dot3(m1h_ref[c], m1l_ref[c], a) * rs_ref[...]
        ore_ref[c] = x[:n2]
        oim_ref[c] = x[n2:]


def _hy_filter_spectrum(m1, are, aim, rs):
    _, n1, n2, w = are.shape
    cb = min(4, n1)
    blk = pl.BlockSpec((1, cb, n2, w), lambda c: (0, c, 0, 0))
    mblk = pl.BlockSpec((cb, 2 * n2, 2 * n2), lambda c: (c, 0, 0))
    oblk = pl.BlockSpec((cb, n2, w), lambda c: (c, 0, 0))
    out = jax.ShapeDtypeStruct((n1, n2, w), F32)
    return pl.pallas_call(
        _hy_spec_kernel,
        grid=(n1 // cb,),
        in_specs=[mblk, mblk, blk, blk, pl.BlockSpec((1, w), lambda c: (0, 0))],
        out_specs=[oblk, oblk],
        out_shape=[out, out],
        compiler_params=_cp("parallel"),
        name="hyena_filter_spectrum",
    )(*m1, are, aim, rs)


def _hy_mid(m1, m2, are, aim, hre, him):
    bx, n1, n2, w = are.shape
    cb = min(4, n1)
    blk = pl.BlockSpec((1, cb, n2, w), lambda c, b: (b, c, 0, 0))
    mblk = pl.BlockSpec((cb, 2 * n2, 2 * n2), lambda c, b: (c, 0, 0))
    hblk = pl.BlockSpec((cb, n2, w), lambda c, b: (c, 0, 0))
    out = jax.ShapeDtypeStruct((bx, n1, n2, w), F32)
    return pl.pallas_call(
        _hy_mid_kernel,
        grid=(n1 // cb, bx),
        in_specs=[mblk, mblk, mblk, mblk, blk, blk, hblk, hblk],
        out_specs=[blk, blk],
        out_shape=[out, out],
        compiler_params=_cp("parallel", "parallel"),
        name="hyena_dft_mid",
    )(*m1, *m2, are, aim, hre, him)


def _hy_last_kernel(fh_ref, fl_ref, re_ref, im_ref, p_ref, bias_ref, o_ref):
    for bb in range(p_ref.shape[2]):
        spec = jnp.concatenate([re_ref[0, :, bb, :], im_ref[0, :, bb, :]], axis=0)
        y = _dot3(fh_ref[...], fl_ref[...], spec)
        o_ref[0, :, bb, :] = y + p_ref[0, :, bb, :] * bias_ref[...]


def _hy_last(f4, bre, bim, p4, bias):
    bx, n1, n2, w = bre.shape
    k = n1 // 2
    tb = HY_TB
    big = pl.BlockSpec((1, n1, tb, w), lambda b, j: (b, 0, j, 0))
    small = pl.BlockSpec((1, k, tb, w), lambda b, j: (b, 0, j, 0))
    return pl.pallas_call(
        _hy_last_kernel,
        grid=(bx, n2 // tb),
        in_specs=[pl.BlockSpec(f4[0].shape, lambda b, j: (0, 0))] * 2 + [big, big, small,
                  pl.BlockSpec((1, w), lambda b, j: (0, 0))],
        out_specs=small,
        out_shape=jax.ShapeDtypeStruct((bx, k, n2, w), F32),
        compiler_params=_cp("parallel", "parallel"),
        name="hyena_dft_last",
    )(*f4, bre, bim, p4, bias)


def _hyena(zb, conv_w, conv_b, filt_params, hy_bias, consts):
    bx, lx, _ = zb.shape
    feat, (f1, m1, m2, f4) = consts
    n = 2 * lx
    n1, n2 = _dft_factors(n)
    w = HY_WIDTH
    filt, ssq = _hy_filter(lx, feat, *filt_params)
    rs = lax.rsqrt(ssq + EPS)
    fre, fim = _hy_stage1(_hi_lo(f1), filt.reshape(1, n1, n2, w))
    hre, him = _hy_filter_spectrum(m1, fre, fim, rs)
    p, x0 = _hy_prep(zb, conv_w, conv_b)
    k = n1 // 2
    p4 = p.reshape(bx, k, n2, w)
    are, aim = _hy_stage1(_hi_lo(f1[:, :k]), p4)
    bre, bim = _hy_mid(m1, m2, are, aim, hre, him)
    t = _hy_last(f4, bre, bim, p4, hy_bias.reshape(1, w))
    return x0, t.reshape(bx, lx, w)


WA_SHIFT_MAX = 35.0


def _wattn_kernel(sk_ref, q_ref, bias_ref, *rest, banded, fixed):
    if banded:
        kp_ref, kc_ref, kn_ref, kx_ref, vp_ref, vc_ref, vn_ref, vx_ref, o_ref = rest
    else:
        kx_ref, vx_ref, o_ref = rest
    h = pl.program_id(1)
    i = pl.program_id(2)
    last = pl.num_programs(2) - 1
    qb = q_ref.shape[1]
    q = jnp.concatenate([q_ref[0, :, :LANES], q_ref[0, :, LANES:]], axis=0)
    if banded:
        kk = jnp.concatenate([kp_ref[0], kc_ref[0], kn_ref[0], kx_ref[0]], axis=0)
        vv = jnp.concatenate([vp_ref[0], vc_ref[0], vn_ref[0], vx_ref[0]], axis=0)
    else:
        kk, vv = kx_ref[0], vx_ref[0]
    s = lax.dot_general(q, kk, (((1,), (1,)), ((), ())), preferred_element_type=F32) + bias_ref[0]
    if banded:
        c = lax.broadcasted_iota(jnp.int32, (1, kk.shape[0]), 1) - BLOCK
        outside = ((c < 0) & (i == 0)) | ((c >= qb) & (c < qb + BLOCK) & (i == last))
        s = s + jnp.where(outside, NEG_INF, 0.0)
    top = lax.broadcasted_iota(jnp.int32, (2 * qb, 1), 0) < qb
    sk = jnp.where(top, sk_ref[h * WA_GROUP], sk_ref[h * WA_GROUP + 1])
    if fixed:
        p, sink_term = jnp.exp2(s), sk
    else:
        m = jnp.maximum(jnp.max(s, axis=-1, keepdims=True), sk)
        p, sink_term = jnp.exp2(s - m), jnp.exp2(sk - m)
    acc = jnp.dot(p.astype(BF16), vv, preferred_element_type=F32)
    o = (acc * (1.0 / (acc[:, WA_DIM:WA_DIM + 1] + sink_term))).astype(BF16)
    o_ref[0] = jnp.concatenate([o[:qb], o[qb:]], axis=1)


def _window_attention(q, k, v, kx, vx, sink, bound, banded):
    b, lq, _ = q.shape
    cx = kx.shape[1]
    qb = min(256, lq)
    per = qb // BLOCK
    nblk = lq // BLOCK
    side = lambda f: pl.BlockSpec((1, BLOCK, LANES), f)
    prev = side(lambda b_, h, i: (b_, jnp.maximum(i * per - 1, 0), h))
    nxt = side(lambda b_, h, i: (b_, jnp.minimum((i + 1) * per, nblk - 1), h))
    cur = pl.BlockSpec((1, qb, LANES), lambda b_, h, i: (b_, i, h))
    ctx = pl.BlockSpec((1, cx, LANES), lambda b_, h, i: (b_, 0, h))
    qspec = pl.BlockSpec((1, qb, WA_GROUP * LANES), lambda b_, h, i: (b_, i, h))
    mask = jnp.zeros((WA_GROUP * qb, cx), F32)
    if banded:
        r = jnp.arange(WA_GROUP * qb)[:, None] % qb
        c = jnp.arange(qb + 2 * BLOCK)[None, :] - BLOCK
        mask = jnp.concatenate([jnp.where(jnp.abs(r - c) <= BLOCK, 0.0, NEG_INF).astype(F32), mask], axis=1)
        specs, args = [prev, cur, nxt, ctx, prev, cur, nxt, ctx], (k, k, k, kx, v, v, v, vx)
    else:
        specs, args = [ctx, ctx], (kx, vx)
    fixed = bound <= WA_SHIFT_MAX
    shift = jnp.maximum(bound, sink)
    rows = jnp.repeat(shift.reshape(WA_KV_HEADS, WA_GROUP), qb, axis=1)[:, :, None]
    bias = mask[None] - jnp.where(fixed, LOG2E * rows, 0.0)
    sk = jnp.where(fixed, jnp.exp2(LOG2E * (sink - shift)), LOG2E * sink)

    def call(fixed_, *ops):
        return pl.pallas_call(
            functools.partial(_wattn_kernel, banded=banded, fixed=fixed_),
            grid=(b, WA_KV_HEADS, lq // qb),
            in_specs=[pl.BlockSpec(memory_space=pltpu.SMEM), qspec,
                      pl.BlockSpec((1,) + bias.shape[1:], lambda b_, h, i: (h, 0, 0))] + specs,
            out_specs=qspec,
            out_shape=jax.ShapeDtypeStruct((b, lq, WA_HEADS * LANES), BF16),
            compiler_params=_cp("parallel", "parallel", "arbitrary"),
            name="window_attention" if fixed_ else "window_attention_online",
        )(*ops)

    return lax.cond(fixed, functools.partial(call, True), functools.partial(call, False), sk, q, bias, *args)


def _conf_kernel(zp_ref, zc_ref, zn_ref, w_ref, b_ref, lg_ref, lb_ref, o_ref, ext, sh, *, halo):
    i = pl.program_id(1)
    last = pl.num_programs(1) - 1
    tl = zc_ref.shape[1]
    cw = CF_WIDTH

    def glu(z_ref):
        z = z_ref[0].astype(F32)
        return z[:, :cw] * jax.nn.sigmoid(z[:, cw:])

    ext[0:halo] = jnp.where(i == 0, 0.0, glu(zp_ref))
    ext[halo:halo + tl] = glu(zc_ref)
    ext[halo + tl:] = jnp.where(i == last, 0.0, glu(zn_ref))
    for r in range(1, SUBLANES):
        sh[r - 1] = ext[pl.ds(r, sh.shape[1]), :]
    w = w_ref[...]
    u = jnp.zeros((tl, cw), F32) + b_ref[...]
    for j in range(CF_TAPS):
        off = halo - CF_TAPS // 2 + j
        base, r = off // SUBLANES * SUBLANES, off % SUBLANES
        tap = ext[pl.ds(base, tl), :] if r == 0 else sh[r - 1, pl.ds(base, tl), :]
        u = u + tap * w[j:j + 1]
    uc = u - jnp.mean(u, axis=-1, keepdims=True)
    y = uc * lax.rsqrt(jnp.mean(uc * uc, axis=-1, keepdims=True) + EPS) * lg_ref[...] + lb_ref[...]
    o_ref[0] = _silu(y).astype(BF16)


def _conformer(zd, dw_w, dw_b, ln_g, ln_b):
    bx, lx, w = zd.shape
    tl = min(256, lx)
    halo = 2 * SUBLANES
    nh = lx // halo
    per = tl // halo
    row = lambda a: a.reshape(1, -1)
    const = lambda b, i: (0, 0)
    return pl.pallas_call(
        functools.partial(_conf_kernel, halo=halo),
        grid=(bx, lx // tl),
        in_specs=[pl.BlockSpec((1, halo, w), lambda b, i: (b, jnp.maximum(i * per - 1, 0), 0)),
                  pl.BlockSpec((1, tl, w), lambda b, i: (b, i, 0)),
                  pl.BlockSpec((1, halo, w), lambda b, i: (b, jnp.minimum((i + 1) * per, nh - 1), 0)),
                  pl.BlockSpec(dw_w.shape, const)] + [pl.BlockSpec((1, CF_WIDTH), const)] * 3,
        out_specs=pl.BlockSpec((1, tl, CF_WIDTH), lambda b, i: (b, i, 0)),
        out_shape=jax.ShapeDtypeStruct((bx, lx, CF_WIDTH), BF16),
        scratch_shapes=[pltpu.VMEM((tl + 2 * halo, CF_WIDTH), F32),
                        pltpu.VMEM((SUBLANES - 1, tl + 2 * halo - SUBLANES, CF_WIDTH), F32)],
        compiler_params=_cp("parallel", "parallel"),
        name="conformer_conv",
    )(zd, zd, zd, dw_w, row(dw_b), row(ln_g), row(ln_b))


def _merge_kernel(x_ref, m_ref, ya, yb0, ybt, yc, yd, zg_ref, bg_ref, wa, wb, wc, wd, wo_ref, o_ref, *, goff):
    d = x_ref.shape[-1]
    acc = jnp.zeros(x_ref.shape[1:], F32)
    ys = (ya[0], (yb0[0] * ybt[0]).astype(BF16), yc[0], yd[0])
    for i, (y, w) in enumerate(zip(ys, (wa, wb, wc, wd))):
        gate = jax.nn.sigmoid(zg_ref[0, :, i * d:(i + 1) * d].astype(F32) + bg_ref[:, i * d:(i + 1) * d])
        acc = acc + gate * jnp.dot(y, w[...], preferred_element_type=F32)
    out = jnp.dot(acc.astype(BF16), wo_ref[...], preferred_element_type=F32)
    o_ref[0] = x_ref[0] + m_ref[0, :, goff:goff + d] * out


def _merge(x, mod, ys, zg, b_gate, wbs, w_out):
    bx, lx, d = x.shape
    tm = min(512, lx)
    row = lambda b, i: (b, i, 0)
    const = lambda b, i: (0, 0)
    once = lambda a: pl.BlockSpec(a.shape, const, pipeline_mode=pl.Buffered(1))
    return pl.pallas_call(
        functools.partial(_merge_kernel, goff=2 * d),
        grid=(bx, lx // tm),
        in_specs=[pl.BlockSpec((1, tm, d), row), pl.BlockSpec((1, 1, mod.shape[-1]), lambda b, i: (b, 0, 0))]
                 + [pl.BlockSpec((1, tm, y.shape[-1]), row) for y in ys]
                 + [pl.BlockSpec((1, tm, N_BRANCH * d), row), pl.BlockSpec((1, N_BRANCH * d), const)]
                 + [once(w) for w in wbs] + [once(w_out)],
        out_specs=pl.BlockSpec((1, tm, d), row),
        out_shape=jax.ShapeDtypeStruct(x.shape, F32),
        input_output_aliases={0: 0},
        compiler_params=_cp("parallel", "parallel"),
        name="merge",
    )(x, mod, *ys, zg, b_gate.reshape(1, -1), *wbs, w_out)


def _slot_rows(w, group):
    n = w.shape[0] // group
    return jnp.pad(w.reshape(n, group, -1), ((0, 0), (0, LANES - group), (0, 0))).reshape(n * LANES, -1)


def _moe_kernel(x_ref, m_ref, g_ref, wr_ref, br_ref, w1_ref, w3_ref, w2_ref, o_ref, h_sc, gate_sc, acc_sc):
    e = pl.program_id(2)
    d = x_ref.shape[-1]
    tm = x_ref.shape[1]
    lane = lax.broadcasted_iota(jnp.int32, (tm, LANES), 1).astype(F32)

    @pl.when(e == 0)
    def _():
        h = _rms(x_ref[0]) * g_ref[...] * (1.0 + m_ref[0, :, 4 * d:5 * d]) + m_ref[0, :, 3 * d:4 * d]
        h_sc[...] = h.astype(BF16)
        lg = _dot3(*_hi_lo(h), wr_ref[...]) + br_ref[...]
        isg = lane < N_GROUPS
        gmax = jnp.max(jnp.where(isg, lg, NEG_INF), axis=-1, keepdims=True)
        gi = jnp.min(jnp.where(isg & (lg == gmax), lane, LANES), axis=-1, keepdims=True)
        gw = 1.0 / jnp.sum(jnp.where(isg, jnp.exp(lg - gmax), 0.0), axis=-1, keepdims=True)
        lo = N_GROUPS + gi * EXP_PER_GROUP
        ise = (lane >= lo) & (lane < lo + EXP_PER_GROUP)
        le = jnp.where(ise, lg, NEG_INF)
        m1 = jnp.max(le, axis=-1, keepdims=True)
        i1 = jnp.min(jnp.where(ise & (le == m1), lane, LANES), axis=-1, keepdims=True)
        ise2 = ise & (lane != i1)
        le2 = jnp.where(ise2, lg, NEG_INF)
        m2 = jnp.max(le2, axis=-1, keepdims=True)
        i2 = jnp.min(jnp.where(ise2 & (le2 == m2), lane, LANES), axis=-1, keepdims=True)
        r = jnp.exp(m2 - m1)
        wa = gw / (1.0 + r)
        gate_sc[...] = jnp.where(lane == i1, wa, 0.0) + jnp.where(lane == i2, wa * r, 0.0)
        acc_sc[...] = jnp.zeros_like(acc_sc)

    hb = h_sc[...]
    u = (_silu(jnp.dot(hb, w1_ref[0].astype(BF16), preferred_element_type=F32))
         * jnp.dot(hb, w3_ref[0].astype(BF16), preferred_element_type=F32))
    ge = jnp.sum(jnp.where(lane == (e + N_GROUPS).astype(F32), gate_sc[...], 0.0), axis=-1, keepdims=True)
    acc_sc[...] += ge * jnp.dot(u.astype(BF16), w2_ref[0].astype(BF16), preferred_element_type=F32)

    @pl.when(e == pl.num_programs(2) - 1)
    def _():
        o_ref[0] = x_ref[0] + m_ref[0, :, 5 * d:6 * d] * acc_sc[...]


def _moe(x, mod, g, w_router, b_router, w1, w3, w2):
    bx, lx, d = x.shape
    tm = min(1024, lx)
    ne, _, f = w1.shape
    row = lambda b, i, e: (b, i, 0)
    const = lambda b, i, e: (0, 0)
    return pl.pallas_call(
        _moe_kernel,
        grid=(bx, lx // tm, ne),
        in_specs=[pl.BlockSpec((1, tm, d), row), pl.BlockSpec((1, 1, mod.shape[-1]), lambda b, i, e: (b, 0, 0)),
                  pl.BlockSpec((1, d), const), pl.BlockSpec((d, LANES), const), pl.BlockSpec((1, LANES), const),
                  pl.BlockSpec((1, d, f), lambda b, i, e: (e, 0, 0)), pl.BlockSpec((1, d, f), lambda b, i, e: (e, 0, 0)),
                  pl.BlockSpec((1, f, d), lambda b, i, e: (e, 0, 0))],
        out_specs=pl.BlockSpec((1, tm, d), row),
        out_shape=jax.ShapeDtypeStruct(x.shape, F32),
        scratch_shapes=[pltpu.VMEM((tm, d), BF16), pltpu.VMEM((tm, LANES), F32), pltpu.VMEM((tm, d), F32)],
        input_output_aliases={0: 0},
        compiler_params=_cp("parallel", "parallel", "arbitrary"),
        name="moe",
    )(x, mod, g.reshape(1, d), w_router, b_router, w1, w3, w2)


SC_CORES = 2
SC_SUBCORES = 16
SC_CHUNK = 64
MOE_ROWS = 1024


def _sc_gather(table, idx):
    n = idx.shape[0]
    w = table.shape[1]
    per = n // (SC_CORES * SC_SUBCORES)
    assert per * SC_CORES * SC_SUBCORES == n and per % SC_CHUNK == 0
    mesh = plsc.VectorSubcoreMesh(core_axis_name="c", subcore_axis_name="s")

    @functools.partial(
        pl.kernel, mesh=mesh, out_type=jax.ShapeDtypeStruct((n, w), table.dtype),
        scratch_types=[pltpu.VMEM((SC_CHUNK,), jnp.int32), pltpu.VMEM((SC_CHUNK, w), table.dtype),
                       pltpu.SemaphoreType.DMA],
        name="sc_row_gather")
    def gather(table_hbm, idx_hbm, out_hbm, idx_v, rows_v, sem):
        base = (lax.axis_index("s") * SC_CORES + lax.axis_index("c")) * per

        @pl.loop(0, per // SC_CHUNK)
        def _(j):
            off = pl.multiple_of(base + j * SC_CHUNK, SC_CHUNK)
            pltpu.sync_copy(idx_hbm.at[pl.ds(off, SC_CHUNK)], idx_v)
            pltpu.async_copy(table_hbm.at[idx_v], rows_v, sem).wait()
            pltpu.sync_copy(rows_v, out_hbm.at[pl.ds(off, SC_CHUNK)])

    return gather(table, idx)


def _sc_scatter(rows, idx, n_out):
    n, w = rows.shape
    per = n // (SC_CORES * SC_SUBCORES)
    assert per * SC_CORES * SC_SUBCORES == n and per % SC_CHUNK == 0
    mesh = plsc.VectorSubcoreMesh(core_axis_name="c", subcore_axis_name="s")

    @functools.partial(
        pl.kernel, mesh=mesh, out_type=jax.ShapeDtypeStruct((n_out, w), rows.dtype),
        scratch_types=[pltpu.VMEM((SC_CHUNK,), jnp.int32), pltpu.VMEM((SC_CHUNK, w), rows.dtype),
                       pltpu.SemaphoreType.DMA],
        name="sc_row_scatter")
    def scatter(rows_hbm, idx_hbm, out_hbm, idx_v, rows_v, sem):
        base = (lax.axis_index("s") * SC_CORES + lax.axis_index("c")) * per

        @pl.loop(0, per // SC_CHUNK)
        def _(j):
            off = pl.multiple_of(base + j * SC_CHUNK, SC_CHUNK)
            pltpu.sync_copy(idx_hbm.at[pl.ds(off, SC_CHUNK)], idx_v)
            pltpu.sync_copy(rows_hbm.at[pl.ds(off, SC_CHUNK)], rows_v)
            pltpu.async_copy(rows_v, out_hbm.at[idx_v], sem).wait()

    return scatter(rows, idx)


def _route_kernel(x_ref, m_ref, g_ref, wr_ref, br_ref, rows_ref, gi_ref):
    d = x_ref.shape[-1]
    tm = x_ref.shape[1]
    lane = lax.broadcasted_iota(jnp.int32, (tm, LANES), 1).astype(F32)
    h = _rms(x_ref[0]) * g_ref[...] * (1.0 + m_ref[0, :, 4 * d:5 * d]) + m_ref[0, :, 3 * d:4 * d]
    rows_ref[0, :, :d] = h
    lg = _dot3(*_hi_lo(h), wr_ref[...]) + br_ref[...]
    isg = lane < N_GROUPS
    gmax = jnp.max(jnp.where(isg, lg, NEG_INF), axis=-1, keepdims=True)
    gi = jnp.min(jnp.where(isg & (lg == gmax), lane, LANES), axis=-1, keepdims=True)
    gw = 1.0 / jnp.sum(jnp.where(isg, jnp.exp(lg - gmax), 0.0), axis=-1, keepdims=True)
    lo = N_GROUPS + gi * EXP_PER_GROUP
    ise = (lane >= lo) & (lane < lo + EXP_PER_GROUP)
    le = jnp.where(ise, lg, NEG_INF)
    m1 = jnp.max(le, axis=-1, keepdims=True)
    i1 = jnp.min(jnp.where(ise & (le == m1), lane, LANES), axis=-1, keepdims=True)
    ise2 = ise & (lane != i1)
    le2 = jnp.where(ise2, lg, NEG_INF)
    m2 = jnp.max(le2, axis=-1, keepdims=True)
    i2 = jnp.min(jnp.where(ise2 & (le2 == m2), lane, LANES), axis=-1, keepdims=True)
    r = jnp.exp(m2 - m1)
    wa = gw / (1.0 + r)
    rows_ref[0, :, d:] = jnp.where(lane == i1 - lo, wa, 0.0) + jnp.where(lane == i2 - lo, wa * r, 0.0)
    gi_ref[0] = gi.astype(jnp.int32)


def _route(x, mod, g, w_router, b_router):
    bx, lx, d = x.shape
    tm = min(512, lx)
    row = lambda b, i: (b, i, 0)
    const = lambda b, i: (0, 0)
    return pl.pallas_call(
        _route_kernel,
        grid=(bx, lx // tm),
        in_specs=[pl.BlockSpec((1, tm, d), row), pl.BlockSpec((1, 1, mod.shape[-1]), lambda b, i: (b, 0, 0)),
                  pl.BlockSpec((1, d), const), pl.BlockSpec((d, LANES), const), pl.BlockSpec((1, LANES), const)],
        out_specs=[pl.BlockSpec((1, tm, d + LANES), row), pl.BlockSpec((1, tm, 1), row)],
        out_shape=[jax.ShapeDtypeStruct((bx, lx, d + LANES), F32), jax.ShapeDtypeStruct((bx, lx, 1), jnp.int32)],
        compiler_params=_cp("parallel", "parallel"),
        name="moe_route",
    )(x, mod, g.reshape(1, d), w_router, b_router)


def _gmoe_kernel(grp_ref, nv_ref, xs_ref, w1_ref, w3_ref, w2_ref, o_ref, h_sc, acc_sc):
    i = pl.program_id(0)
    e = pl.program_id(1)
    tm, d = h_sc.shape
    valid = lax.broadcasted_iota(jnp.int32, (tm, 1), 0) < nv_ref[i]

    @pl.when(nv_ref[i] > 0)
    def _():
        @pl.when(e == 0)
        def _():
            h_sc[...] = jnp.where(valid, xs_ref[:, :d], 0.0).astype(BF16)
            acc_sc[...] = jnp.zeros_like(acc_sc)

        hb = h_sc[...]
        u = (_silu(jnp.dot(hb, w1_ref[0].astype(BF16), preferred_element_type=F32))
             * jnp.dot(hb, w3_ref[0].astype(BF16), preferred_element_type=F32))
        lane = lax.broadcasted_iota(jnp.int32, (tm, LANES), 1)
        ge = jnp.sum(jnp.where(valid & (lane == e), xs_ref[:, d:], 0.0), axis=-1, keepdims=True)
        acc_sc[...] += ge * jnp.dot(u.astype(BF16), w2_ref[0].astype(BF16), preferred_element_type=F32)

    @pl.when(e == pl.num_programs(1) - 1)
    def _():
        o_ref[...] = jnp.where(nv_ref[i] > 0, acc_sc[...], 0.0)


def _grouped_moe(grp, nv, xs, w1, w3, w2):
    p, dw = xs.shape
    d = dw - LANES
    _, _, f = w1.shape
    wmap = lambda i, e, grp, nv: (grp[i] * EXP_PER_GROUP + e, 0, 0)
    rows = lambda i, e, grp, nv: (i, 0)
    return pl.pallas_call(
        _gmoe_kernel,
        grid_spec=pltpu.PrefetchScalarGridSpec(
            num_scalar_prefetch=2,
            grid=(p // MOE_ROWS, EXP_PER_GROUP),
            in_specs=[pl.BlockSpec((MOE_ROWS, dw), rows),
                      pl.BlockSpec((1, d, f), wmap), pl.BlockSpec((1, d, f), wmap), pl.BlockSpec((1, f, d), wmap)],
            out_specs=pl.BlockSpec((MOE_ROWS, d), rows),
            scratch_shapes=[pltpu.VMEM((MOE_ROWS, d), BF16), pltpu.VMEM((MOE_ROWS, d), F32)]),
        out_shape=jax.ShapeDtypeStruct((p, d), F32),
        compiler_params=_cp("arbitrary", "arbitrary"),
        name="moe_experts",
    )(grp, nv, xs, w1, w3, w2)


def _residual_kernel(x_ref, m_ref, y_ref, o_ref):
    d = x_ref.shape[-1]
    o_ref[0] = x_ref[0] + m_ref[0, :, 5 * d:6 * d] * y_ref[0]


def _residual(x, mod, y):
    bx, lx, d = x.shape
    tm = min(1024, lx)
    row = lambda b, i: (b, i, 0)
    return pl.pallas_call(
        _residual_kernel,
        grid=(bx, lx // tm),
        in_specs=[pl.BlockSpec((1, tm, d), row), pl.BlockSpec((1, 1, mod.shape[-1]), lambda b, i: (b, 0, 0)),
                  pl.BlockSpec((1, tm, d), row)],
        out_specs=pl.BlockSpec((1, tm, d), row),
        out_shape=jax.ShapeDtypeStruct(x.shape, F32),
        input_output_aliases={0: 0},
        compiler_params=_cp("parallel", "parallel"),
        name="moe_residual",
    )(x, mod, y)


def _moe_sorted(x, mod, g, w_router, b_router, w1, w3, w2):
    bx, lx, d = x.shape
    t = bx * lx
    rows, gi = _route(x, mod, g, w_router, b_router)
    gi = gi.reshape(t)
    onehot = (gi[:, None] == jnp.arange(N_GROUPS, dtype=jnp.int32)[None, :]).astype(jnp.int32)
    csum = jnp.cumsum(onehot, axis=0)
    counts = csum[-1]
    rank = jnp.take_along_axis(csum, gi[:, None], axis=1)[:, 0] - 1
    padded = (counts + MOE_ROWS - 1) // MOE_ROWS * MOE_ROWS
    pend = jnp.cumsum(padded)
    pstart = pend - padded
    pos = (pstart[gi] + rank).astype(jnp.int32)
    p = t + N_GROUPS * MOE_ROWS
    bstart = jnp.arange(p // MOE_ROWS, dtype=jnp.int32) * MOE_ROWS
    grp = jnp.minimum(jnp.searchsorted(pend, bstart, side="right"), N_GROUPS - 1).astype(jnp.int32)
    nv = jnp.clip(pstart[grp] + counts[grp] - bstart, 0, MOE_ROWS).astype(jnp.int32)
    xs = _sc_scatter(rows.reshape(t, d + LANES), pos, p)
    ys = _grouped_moe(grp, nv, xs, w1, w3, w2)
    yt = _sc_gather(ys, pos)
    return _residual(x, mod, yt.reshape(bx, lx, d))


def kernel(x, c, ctx, c_ctx, w_mod, b_mod, norm1_g, norm2_g, w_in, b_gate, da_qn, da_kn, da_lam, da_subln, hy_conv_w, hy_conv_b, hf_w1, hf_b1, hf_w2, hf_b2, hf_w3, hf_b3, hf_w4, hf_freq, hy_bias, wa_qn, wa_kn, wa_sink, cf_dw_w, cf_dw_b, cf_ln_g, cf_ln_b, w_branch, w_out, w_rg, b_rg, w_re, b_re, w1, w3, w2):
    b, s, d = x.shape
    cl = ctx.shape[1]
    depth = w_mod.shape[0]
    assert s % 256 == 0 and cl % 256 == 0 and s % GRID_W == 0

    nrow = -(-(b + 1) // SUBLANES) * SUBLANES
    crows = jnp.zeros((nrow, d), F32).at[:b].set(c).at[b].set(c_ctx)
    mods = _mod_vectors(crows, w_mod, b_mod)

    aw, qw, kw = DA_HEADS * DA_DIM, WA_HEADS * WA_DIM, WA_KV_HEADS * WA_DIM
    tab_lat = (*_rope_tables(s, DA_DIM, DA_HEADS), *_rope_tables(s, WA_DIM, WA_HEADS), *_rope_tables(s, WA_DIM, WA_KV_HEADS))
    tab_ctx = (*_unit_tables(cl, aw), *_unit_tables(cl, qw), *_unit_tables(cl, kw))
    hy_lat = (_hy_feat(s), _dft_tables(2 * s))
    hy_ctx = (_hy_feat(cl), _dft_tables(2 * cl))
    gms = (_group_ones(aw, DA_DIM), _group_ones(qw, WA_DIM), _group_ones(kw, WA_DIM))
    sels = (_slot_select(aw, DA_DIM), _slot_select(DA_HEADS * DA_VDIM, DA_VDIM), _slot_select(qw, WA_DIM),
            _slot_select(kw, WA_DIM))
    qvec = _slot_vector(DA_HEADS, DA_DIM, DA_DIM + 1, 1.0)
    vvec = _slot_vector(DA_HEADS, DA_VDIM, DA_VDIM + DA_ONES, 1.0)
    vvecc = _slot_vector(WA_KV_HEADS, WA_DIM, WA_DIM + DA_ONES, 1.0)

    xc = ctx
    for l in range(depth):
        last = l == depth - 1
        lam_init = 0.8 - 0.6 * math.exp(-0.3 * l)
        mod_x = mods[l, :b][:, None, :]
        mod_c = jnp.broadcast_to(mods[l, b][None, None, :], (b, 1, mods.shape[-1]))
        wl = w_in[l]
        ws = [wl[:, 0:OFF_B].astype(BF16), wl[:, OFF_B:OFF_C].astype(BF16), wl[:, OFF_C:OFF_D].astype(BF16),
              wl[:, OFF_D:OFF_G].astype(BF16), wl[:, OFF_G:].astype(BF16)]
        shift = 1.02 * LOG2E * DA_DIM ** 0.5 * jnp.max(jnp.abs(da_qn[l])) * jnp.max(jnp.abs(da_kn[l]))
        fixed = shift <= DA_SHIFT_MAX
        kvec = _slot_vector(DA_HEADS, DA_DIM, DA_DIM + 1, jnp.where(fixed, -shift, 0.0))
        tile = lambda a, n: jnp.tile(a, n).reshape(1, -1)
        consts = (*gms, tile(da_qn[l], DA_HEADS), tile(da_kn[l], DA_HEADS), tile(wa_qn[l], WA_HEADS),
                  tile(wa_kn[l], WA_KV_HEADS), *sels, qvec, kvec, vvec, vvecc)
        q1, q2, k1, k2, v, zb, qc, kc, vc, zd, zg = _in_proj(x, mod_x, norm1_g[l], ws, tab_lat, consts)
        q1x, q2x, k1x, k2x, vx, zbx, qcx, kcx, vcx, zdx, zgx = _in_proj(xc, mod_c, norm1_g[l], ws, tab_ctx, consts)
        filt_params = (hf_w1[l], hf_b1[l], hf_w2[l], hf_b2[l], hf_w3[l], hf_b3[l], hf_w4[l], hf_freq[l])
        wb4 = w_branch[l].astype(BF16)
        wbs = (_slot_rows(wb4[0], DA_VDIM), wb4[1], _slot_rows(wb4[2], WA_DIM), wb4[3])
        wo = w_out[l].astype(BF16)

        ya = _diff_attention(q1, q2, [(k1, k2, v), (k1x, k2x, vx)], fixed, da_lam[l], da_subln[l], lam_init)
        yb = _hyena(zb, hy_conv_w[l], hy_conv_b[l], filt_params, hy_bias[l], hy_lat)
        wbound = 1.02 * WA_DIM ** 0.5 * jnp.max(jnp.abs(wa_qn[l])) * jnp.max(jnp.abs(wa_kn[l]))
        yc_ = _window_attention(qc, kc, vc, kcx, vcx, wa_sink[l], wbound, True)
        yd = _conformer(zd, cf_dw_w[l], cf_dw_b[l], cf_ln_g[l], cf_ln_b[l])
        x = _merge(x, mod_x, (ya, *yb, yc_, yd), zg, b_gate[l], wbs, wo)

        w_router = jnp.pad(jnp.concatenate([w_rg[l], w_re[l]], axis=1), ((0, 0), (0, LANES - N_GROUPS - N_EXPERTS)))
        b_router = jnp.pad(jnp.concatenate([b_rg[l], b_re[l]]), (0, LANES - N_GROUPS - N_EXPERTS)).reshape(1, LANES)
        ew = (w1[l], w3[l], w2[l])

        if not last:
            yca = _diff_attention(q1x, q2x, [(k1x, k2x, vx)], fixed, da_lam[l], da_subln[l], lam_init)
            ycb = _hyena(zbx, hy_conv_w[l], hy_conv_b[l], filt_params, hy_bias[l], hy_ctx)
            ycc = _window_attention(qcx, kcx, vcx, kcx, vcx, wa_sink[l], wbound, False)
            ycd = _conformer(zdx, cf_dw_w[l], cf_dw_b[l], cf_ln_g[l], cf_ln_b[l])
            xc = _merge(xc, mod_c, (yca, *ycb, ycc, ycd), zgx, b_gate[l], wbs, wo)
            xc = _moe(xc.reshape(1, b * cl, d), mod_c[:1], norm2_g[l], w_router, b_router, *ew).reshape(b, cl, d)
        x = _moe_sorted(x, mod_x, norm2_g[l], w_router, b_router, *ew)
    return x
```

```python
import functools
import math

import jax
import jax.numpy as jnp
from jax import lax
from jax.experimental import pallas as pl
from jax.experimental.pallas import tpu as pltpu
from jax.experimental.pallas import tpu_sc as plsc

F32 = jnp.float32
BF16 = jnp.bfloat16
HI = lax.Precision.HIGHEST

GRID_W = 64
BLOCK = 128
ROPE_BASE = 10000.0
EPS = 1e-6
NEG_INF = -1e30

DA_HEADS = 4
DA_DIM = 32
DA_VDIM = 64
HY_WIDTH = 256
HY_BANDS = 16
HY_FF = 64
HY_SHIFT = 0.05
HY_FAST_DECAY = 0.3
HY_SLOW_DECAY = 1.5
HY_TARGET = 1e-2
WA_HEADS = 4
WA_KV_HEADS = 2
WA_GROUP = 2
WA_DIM = 64
CF_WIDTH = 256
CF_TAPS = 31
N_BRANCH = 4
BRANCH_W = 256
N_GROUPS = 4
EXP_PER_GROUP = 4
N_EXPERTS = 16

W_A = 4 * DA_HEADS * DA_DIM + DA_HEADS * DA_VDIM
W_B = 3 * HY_WIDTH
W_C = (WA_HEADS + 2 * WA_KV_HEADS) * WA_DIM
W_D = 2 * CF_WIDTH
OFF_B = W_A
OFF_C = OFF_B + W_B
OFF_D = OFF_C + W_C
OFF_G = OFF_D + W_D

LOG2E = math.log2(math.e)
LANES = 128
SUBLANES = 8
VMEM_LIMIT = 56 * 1024 * 1024

DA_ONES = 16
DA_SHIFT_MAX = 50.0


def _cp(*sem):
    return pltpu.CompilerParams(dimension_semantics=sem, vmem_limit_bytes=VMEM_LIMIT)


def _rms(xf):
    return xf * lax.rsqrt(jnp.mean(xf * xf, axis=-1, keepdims=True) + EPS)


def _silu(x):
    return x * jax.nn.sigmoid(x)


def _mod_kernel(c_ref, w_ref, b_ref, o_ref):
    s = _silu(c_ref[...])
    o_ref[0] = jnp.dot(s, w_ref[0], precision=HI, preferred_element_type=F32) + b_ref[0]


def _mod_vectors(crows, w_mod, b_mod):
    depth, d, n = w_mod.shape
    r = crows.shape[0]
    tn = 1536
    return pl.pallas_call(
        _mod_kernel,
        grid=(depth, n // tn),
        in_specs=[pl.BlockSpec((r, d), lambda l, j: (0, 0)),
                  pl.BlockSpec((1, d, tn), lambda l, j: (l, 0, j)),
                  pl.BlockSpec((1, 1, tn), lambda l, j: (l, 0, j))],
        out_specs=pl.BlockSpec((1, r, tn), lambda l, j: (l, 0, j)),
        out_shape=jax.ShapeDtypeStruct((depth, r, n), F32),
        compiler_params=_cp("arbitrary", "arbitrary"),
        name="mod_vectors",
    )(crows, w_mod, b_mod.reshape(depth, 1, n))


def _rope_tables(s, d, reps):
    rows = s // GRID_W
    row = jnp.repeat(jnp.arange(rows, dtype=F32), GRID_W)
    col = jnp.tile(jnp.arange(GRID_W, dtype=F32), rows)
    qd = d // 4
    inv = ROPE_BASE ** (-jnp.arange(qd, dtype=F32) / qd)
    ar = row[:, None] * inv[None, :]
    ac = col[:, None] * inv[None, :]
    z = jnp.zeros_like(ar)
    cos = jnp.concatenate([jnp.cos(ar), jnp.cos(ar), jnp.cos(ac), jnp.cos(ac)], axis=-1)
    sin_up = jnp.concatenate([-jnp.sin(ar), z, -jnp.sin(ac), z], axis=-1)
    sin_dn = jnp.concatenate([z, jnp.sin(ar), z, jnp.sin(ac)], axis=-1)
    t = lambda a: jnp.tile(a, (1, reps))
    return t(cos), t(sin_up), t(sin_dn)


def _unit_tables(s, w):
    return jnp.ones((s, w), F32), jnp.zeros((s, w), F32), jnp.zeros((s, w), F32)


def _group_ones(width, group):
    i = jnp.arange(width) // group
    return (i[:, None] == i[None, :]).astype(BF16)


def _hi_lo(a):
    hi = a.astype(BF16)
    return hi, (a - hi.astype(F32)).astype(BF16)


def _slot_select(width, group):
    i = jnp.arange(width)
    dst = (i // group) * LANES + i % group
    return (dst[:, None] == jnp.arange((width // group) * LANES)[None, :]).astype(BF16)


def _slot_vector(n, lo, hi, value):
    j = jnp.arange(n * LANES) % LANES
    return jnp.where((j >= lo) & (j < hi), value, 0.0).astype(F32).reshape(1, n * LANES)


def _norm_rope(x, gmat, gain, cos, sup, sdn, group, qd):
    w = x.shape[-1]
    sh, sl = _hi_lo(x * x)
    ss = (jnp.dot(sh, gmat, preferred_element_type=F32) + jnp.dot(sl, gmat, preferred_element_type=F32)) * (1.0 / group)
    xn = x * lax.rsqrt(ss + EPS) * gain
    return xn * cos + pltpu.roll(xn, w - qd, 1) * sup + pltpu.roll(xn, qd, 1) * sdn


def _inproj_kernel(x_ref, m_ref, g_ref, wa, wb, wc, wd, wg,
                   ca, ua, da, cq, uq, dq, ck, uk, dk, gma, gmq, gmk, qna, kna, qnc, knc,
                   sela, selv, selq, selk, qvec, kvec, vvec, vvecc,
                   q1o, q2o, k1o, k2o, vo, zbo, qco, kco, vco, zdo, zgo):
    d = x_ref.shape[-1]
    x = x_ref[0]
    shift = m_ref[0, :, 0:d]
    scale = m_ref[0, :, d:2 * d]
    h = (_rms(x) * g_ref[...] * (1.0 + scale) + shift).astype(BF16)
    place = lambda y, sel: jnp.dot(y.astype(BF16), sel[...], preferred_element_type=F32)

    za = jnp.dot(h, wa[...], preferred_element_type=F32)
    hw = DA_HEADS * DA_DIM
    cos, sup, sdn, gm = ca[...], ua[...], da[...], gma[...]
    qscale = DA_DIM ** -0.5 * LOG2E
    for t, (o, gain, sc, vec) in enumerate(((q1o, qna, qscale, qvec), (q2o, qna, qscale, qvec),
                                            (k1o, kna, 1.0, kvec), (k2o, kna, 1.0, kvec))):
        y = _norm_rope(za[:, t * hw:(t + 1) * hw], gm, gain[...], cos, sup, sdn, DA_DIM, DA_DIM // 4) * sc
        o[0] = (place(y, sela) + vec[...]).astype(BF16)
    vo[0] = (place(za[:, 4 * hw:], selv) + vvec[...]).astype(BF16)

    zbo[0] = jnp.dot(h, wb[...], preferred_element_type=F32).astype(BF16)

    zc = jnp.dot(h, wc[...], preferred_element_type=F32)
    qw = WA_HEADS * WA_DIM
    kw = WA_KV_HEADS * WA_DIM
    y = _norm_rope(zc[:, 0:qw], gmq[...], qnc[...], cq[...], uq[...], dq[...], WA_DIM, WA_DIM // 4) * (WA_DIM ** -0.5 * LOG2E)
    qco[0] = place(y, selq).astype(BF16)
    y = _norm_rope(zc[:, qw:qw + kw], gmk[...], knc[...], ck[...], uk[...], dk[...], WA_DIM, WA_DIM // 4)
    kco[0] = place(y, selk).astype(BF16)
    vco[0] = (place(zc[:, qw + kw:], selk) + vvecc[...]).astype(BF16)

    zdo[0] = jnp.dot(h, wd[...], preferred_element_type=F32).astype(BF16)
    for k in range(N_BRANCH):
        zgo[0, :, k * d:(k + 1) * d] = jnp.dot(h, wg[:, k * d:(k + 1) * d], preferred_element_type=F32).astype(BF16)


def _in_proj(x, mod, g, ws, tables, consts):
    bx, lx, d = x.shape
    tm = min(512, lx)
    const = lambda b, i: (0, 0)
    row = lambda b, i: (b, i, 0)
    once = lambda a: pl.BlockSpec(a.shape, const, pipeline_mode=pl.Buffered(1))
    widths = [DA_HEADS * LANES] * 5 + [W_B, WA_HEADS * LANES, WA_KV_HEADS * LANES, WA_KV_HEADS * LANES, W_D, N_BRANCH * d]
    return pl.pallas_call(
        _inproj_kernel,
        grid=(bx, lx // tm),
        in_specs=[pl.BlockSpec((1, tm, d), row), pl.BlockSpec((1, 1, mod.shape[-1]), lambda b, i: (b, 0, 0)),
                  pl.BlockSpec((1, d), const)]
                 + [once(w) for w in ws]
                 + [pl.BlockSpec((tm, t.shape[1]), lambda b, i: (i, 0)) for t in tables]
                 + [once(c) for c in consts],
        out_specs=[pl.BlockSpec((1, tm, w), row) for w in widths],
        out_shape=[jax.ShapeDtypeStruct((bx, lx, w), BF16) for w in widths],
        compiler_params=_cp("parallel", "parallel"),
        name="in_proj",
    )(x, mod, g.reshape(1, d), *ws, *tables, *consts)


def _da_lambda(lam_ref, lam_init):
    lv = lam_ref[...]
    return (jnp.exp(jnp.sum(lv[0:1] * lv[1:2], keepdims=True)) - jnp.exp(jnp.sum(lv[2:3] * lv[3:4], keepdims=True))
            + lam_init)


def _dattn_kernel(lam_ref, sg_ref, q1_ref, q2_ref, *rest, lam_init, online, nsrc, tk_max):
    srcs = [rest[3 * s:3 * s + 3] for s in range(nsrc)]
    o_ref, acc1, acc2 = rest[3 * nsrc:]
    dn = (((1,), (1,)), ((), ()))
    q1 = q1_ref[0]
    q2 = q2_ref[0]
    tq = q1.shape[0]
    acc1[...] = jnp.zeros_like(acc1)
    acc2[...] = jnp.zeros_like(acc2)
    carry = (jnp.full((tq, 1), NEG_INF, F32),) * 2 if online else 0

    for k1_ref, k2_ref, v_ref in srcs:
        tk = min(tk_max, k1_ref.shape[1])

        def body(j, c, k1_ref=k1_ref, k2_ref=k2_ref, v_ref=v_ref, tk=tk):
            rows = pl.ds(pl.multiple_of(j * tk, tk), tk)
            vc = v_ref[0, rows, :]
            s1 = lax.dot_general(q1, k1_ref[0, rows, :], dn, preferred_element_type=F32)
            s2 = lax.dot_general(q2, k2_ref[0, rows, :], dn, preferred_element_type=F32)
            if online:
                m1, m2 = c
                n1 = jnp.maximum(m1, jnp.max(s1, axis=-1, keepdims=True))
                n2 = jnp.maximum(m2, jnp.max(s2, axis=-1, keepdims=True))
                acc1[...] = jnp.exp2(m1 - n1) * acc1[...] + jnp.dot(jnp.exp2(s1 - n1).astype(BF16), vc,
                                                                    preferred_element_type=F32)
                acc2[...] = jnp.exp2(m2 - n2) * acc2[...] + jnp.dot(jnp.exp2(s2 - n2).astype(BF16), vc,
                                                                    preferred_element_type=F32)
                return n1, n2
            acc1[...] += jnp.dot(jnp.exp2(s1).astype(BF16), vc, preferred_element_type=F32)
            acc2[...] += jnp.dot(jnp.exp2(s2).astype(BF16), vc, preferred_element_type=F32)
            return c

        carry = lax.fori_loop(0, k1_ref.shape[1] // tk, body, carry)

    dv = DA_VDIM
    a1 = acc1[...]
    a2 = acc2[...]
    lam = _da_lambda(lam_ref, lam_init)
    o = a1 * (1.0 / a1[:, dv:dv + 1]) - a2 * (lam / a2[:, dv:dv + 1])
    o = jnp.where(lax.broadcasted_iota(jnp.int32, o.shape, 1) < dv, o, 0.0)
    o = o * lax.rsqrt(jnp.sum(o * o, axis=-1, keepdims=True) * (1.0 / dv) + EPS)
    o_ref[0] = (o * (sg_ref[...] * (1.0 - lam_init))).astype(BF16)


def _diff_attention(q1, q2, srcs, fixed, lam_p, subln, lam_init):
    b, lq, _ = q1.shape
    h = DA_HEADS
    const = lambda b_, h_, i: (0, 0)
    sg = jnp.pad(subln, (0, LANES - DA_VDIM)).reshape(1, LANES)
    flat = [a for src in srcs for a in src]

    def call(online, *args):
        tq = min(256 if online else 2048, lq)
        qs = pl.BlockSpec((1, tq, LANES), lambda b_, h_, i: (b_, i, h_))
        return pl.pallas_call(
            functools.partial(_dattn_kernel, lam_init=lam_init, online=online, nsrc=len(srcs),
                              tk_max=256 if online else 512),
            grid=(b, h, lq // tq),
            in_specs=[pl.BlockSpec(lam_p.shape, const), pl.BlockSpec((1, LANES), const), qs, qs]
                     + [pl.BlockSpec((1, a.shape[1], LANES), lambda b_, h_, i: (b_, 0, h_)) for a in flat],
            out_specs=qs,
            out_shape=jax.ShapeDtypeStruct((b, lq, h * LANES), BF16),
            scratch_shapes=[pltpu.VMEM((tq, LANES), F32)] * 2,
            compiler_params=_cp("parallel", "parallel", "arbitrary"),
            name="diff_attention_online" if online else "diff_attention",
        )(*args)

    return lax.cond(fixed, functools.partial(call, False), functools.partial(call, True),
                    lam_p, sg, q1, q2, *flat)


def _hy_prep_kernel(zp_ref, zc_ref, zn_ref, w_ref, b_ref, p_ref, x0_ref, ext):
    i = pl.program_id(1)
    last = pl.num_programs(1) - 1
    tl = zc_ref.shape[1]
    h = 2 * SUBLANES
    ext[0:h] = jnp.where(i == 0, 0.0, zp_ref[0].astype(F32))
    ext[h:h + tl] = zc_ref[0].astype(F32)
    ext[h + tl:] = jnp.where(i == last, 0.0, zn_ref[0].astype(F32))
    w = w_ref[...]
    u = (ext[pl.ds(h - 1, tl), :] * w[0:1] + ext[pl.ds(h, tl), :] * w[1:2] + ext[pl.ds(h + 1, tl), :] * w[2:3]
         + b_ref[...])
    hw = HY_WIDTH
    x0_ref[0] = u[:, 0:hw]
    p_ref[0] = u[:, 2 * hw:3 * hw] * u[:, hw:2 * hw]


def _hy_prep(zb, conv_w, conv_b):
    bx, lx, w = zb.shape
    tl = min(256, lx)
    h = 2 * SUBLANES
    nh = lx // h
    per = tl // h
    out = jax.ShapeDtypeStruct((bx, lx, HY_WIDTH), F32)
    return pl.pallas_call(
        _hy_prep_kernel,
        grid=(bx, lx // tl),
        in_specs=[pl.BlockSpec((1, h, w), lambda b, i: (b, jnp.maximum(i * per - 1, 0), 0)),
                  pl.BlockSpec((1, tl, w), lambda b, i: (b, i, 0)),
                  pl.BlockSpec((1, h, w), lambda b, i: (b, jnp.minimum((i + 1) * per, nh - 1), 0)),
                  pl.BlockSpec(conv_w.shape, lambda b, i: (0, 0)),
                  pl.BlockSpec((1, w), lambda b, i: (0, 0))],
        out_specs=[pl.BlockSpec((1, tl, HY_WIDTH), lambda b, i: (b, i, 0))] * 2,
        out_shape=[out, out],
        scratch_shapes=[pltpu.VMEM((tl + 2 * h, w), F32)],
        compiler_params=_cp("parallel", "parallel"),
        name="hyena_prep",
    )(zb, zb, zb, conv_w, conv_b.reshape(1, w))


def _hy_filter_kernel(feat_ref, w1, b1, w2, b2, w3, b3, w4, fr_ref, dl_ref, filt_ref, ssq_ref, *, s):
    i = pl.program_id(0)
    tr = feat_ref.shape[0]
    feat = feat_ref[...]
    fr = fr_ref[...]
    dot = lambda a, w: _dot3(*_hi_lo(a), w[...])
    a = jnp.sin(fr * (dot(feat, w1) + b1[...]))
    a = jnp.sin(fr * (dot(a, w2) + b2[...]))
    a = jnp.sin(fr * (dot(a, w3) + b3[...]))
    coef = dot(a, w4)
    n = i * tr + lax.broadcasted_iota(jnp.int32, (tr, 1), 0)
    window = jnp.exp(-feat[:, 0:1] * dl_ref[...]) + HY_SHIFT
    half = jnp.where(n < s, coef[:, :HY_WIDTH], coef[:, HY_WIDTH:])
    filt = jnp.where(n == s, 0.0, half * window)
    filt_ref[...] = filt

    @pl.when(i == 0)
    def _():
        ssq_ref[...] = jnp.zeros_like(ssq_ref)

    ssq_ref[...] += jnp.sum(filt * filt, axis=0, keepdims=True)


def _hy_feat(s):
    t = jnp.linspace(0.0, 1.0, s, dtype=F32)[:, None]
    w = (2.0 * math.pi / s) * jnp.arange(s, dtype=F32)[:, None]
    bands = jnp.linspace(1e-4, HY_BANDS - 1, HY_BANDS, dtype=F32)[None, :]
    feat = jnp.concatenate([t, jnp.cos(w * bands), jnp.sin(w * bands)], axis=-1)
    feat = jnp.concatenate([feat, feat[:1], feat[:0:-1]], axis=0)
    return jnp.pad(feat, ((0, 0), (0, LANES - feat.shape[1])))


def _hy_filter(s, feat, w1, b1, w2, b2, w3, b3, w4, freq):
    n = 2 * s
    tr = min(512, n)
    deltas = jnp.abs(jnp.linspace(math.log(HY_TARGET) / HY_FAST_DECAY, math.log(HY_TARGET) / HY_SLOW_DECAY,
                                  HY_WIDTH, dtype=F32)).reshape(1, HY_WIDTH)
    w1p = jnp.pad(w1, ((0, LANES - w1.shape[0]), (0, 0)))
    row = lambda a: a.reshape(1, -1)
    args = (feat, w1p, row(b1), w2, row(b2), w3, row(b3), w4, row(freq), deltas)
    const = lambda i: (0, 0)
    return pl.pallas_call(
        functools.partial(_hy_filter_kernel, s=s),
        grid=(n // tr,),
        in_specs=[pl.BlockSpec((tr, LANES), lambda i: (i, 0))] + [pl.BlockSpec(a.shape, const) for a in args[1:]],
        out_specs=[pl.BlockSpec((tr, HY_WIDTH), lambda i: (i, 0)), pl.BlockSpec((1, HY_WIDTH), const)],
        out_shape=[jax.ShapeDtypeStruct((n, HY_WIDTH), F32), jax.ShapeDtypeStruct((1, HY_WIDTH), F32)],
        compiler_params=_cp("arbitrary"),
        name="hyena_filter",
    )(*args)


def _dft_factors(n):
    lg = n.bit_length() - 1
    assert 1 << lg == n
    n1 = 1 << (lg // 2)
    return n1, n // n1


def _dft_tables(n):
    n1, n2 = _dft_factors(n)
    ia = jnp.arange(n1, dtype=jnp.int32)
    ang1 = (2.0 * math.pi / n1) * ((ia[:, None] * ia[None, :]) % n1).astype(F32)
    f1 = jnp.concatenate([jnp.cos(ang1), -jnp.sin(ang1)], axis=0)
    c = jnp.arange(n1, dtype=jnp.int32)[:, None, None]
    d = jnp.arange(n2, dtype=jnp.int32)[None, :, None]
    b = jnp.arange(n2, dtype=jnp.int32)[None, None, :]
    ang = (2.0 * math.pi / n) * ((b * (c + n1 * d)) % n).astype(F32)
    re, im = jnp.cos(ang), -jnp.sin(ang)
    m1 = jnp.concatenate([jnp.concatenate([re, -im], axis=2), jnp.concatenate([im, re], axis=2)], axis=1)
    m2 = jnp.swapaxes(m1, 1, 2)
    ang4 = ang1[: n1 // 2]
    f4 = jnp.concatenate([jnp.cos(ang4), -jnp.sin(ang4)], axis=1) * (1.0 / n)
    return f1, _hi_lo(m1), _hi_lo(m2), _hi_lo(_per_offset(f4))


def _dot3(mh, ml, a):
    ah, al = _hi_lo(a)
    return (jnp.dot(mh, ah, preferred_element_type=F32) + jnp.dot(mh, al, preferred_element_type=F32)
            + jnp.dot(ml, ah, preferred_element_type=F32))


HY_TB = SUBLANES


def _per_offset(f):
    return jnp.kron(f, jnp.eye(HY_TB, dtype=f.dtype))


def _hy_stage1_kernel(fh_ref, fl_ref, x_ref, re_ref, im_ref):
    _, n1, tb, w = re_ref.shape
    y = _dot3(fh_ref[...], fl_ref[...], x_ref[0].reshape(-1, w))
    re_ref[0] = y[:n1 * tb].reshape(n1, tb, w)
    im_ref[0] = y[n1 * tb:].reshape(n1, tb, w)


def _hy_stage1(f, x4):
    bx, k, n2, w = x4.shape
    tb = HY_TB
    n1 = f[0].shape[0] // (2 * tb)
    out = jax.ShapeDtypeStruct((bx, n1, n2, w), F32)
    return pl.pallas_call(
        _hy_stage1_kernel,
        grid=(bx, n2 // tb),
        in_specs=[pl.BlockSpec(f[0].shape, lambda b, j: (0, 0))] * 2
                 + [pl.BlockSpec((1, k, tb, w), lambda b, j: (b, 0, j, 0))],
        out_specs=[pl.BlockSpec((1, n1, tb, w), lambda b, j: (b, 0, j, 0))] * 2,
        out_shape=[out, out],
        compiler_params=_cp("parallel", "parallel"),
        name="hyena_dft_stage1",
    )(*f, x4)


def _hy_mid_kernel(m1h_ref, m1l_ref, m2h_ref, m2l_ref, re_ref, im_ref, hre_ref, him_ref, ore_ref, oim_ref):
    n2 = re_ref.shape[2]
    for c in range(re_ref.shape[1]):
        a = jnp.concatenate([re_ref[0, c], im_ref[0, c]], axis=0)
        x = _dot3(m1h_ref[c], m1l_ref[c], a)
        xre, xim = x[:n2], x[n2:]
        hre, him = hre_ref[c], him_ref[c]
        y = jnp.concatenate([xre * hre - xim * him, xre * him + xim * hre], axis=0)
        bb = _dot3(m2h_ref[c], m2l_ref[c], y)
        ore_ref[0, c] = bb[:n2]
        oim_ref[0, c] = bb[n2:]


def _hy_spec_kernel(m1h_ref, m1l_ref, re_ref, im_ref, rs_ref, ore_ref, oim_ref):
    n2 = re_ref.shape[2]
    for c in range(re_ref.shape[1]):
        a = jnp.concatenate([re_ref[0, c], im_ref[0, c]], axis=0)
        x = _dot3(m1h_ref[c], m1l_ref[c], a) * rs_ref[...]
        ore_ref[c] = x[:n2]
        oim_ref[c] = x[n2:]


def _hy_filter_spectrum(m1, are, aim, rs):
    _, n1, n2, w = are.shape
    cb = min(4, n1)
    blk = pl.BlockSpec((1, cb, n2, w), lambda c: (0, c, 0, 0))
    mblk = pl.BlockSpec((cb, 2 * n2, 2 * n2), lambda c: (c, 0, 0))
    oblk = pl.BlockSpec((cb, n2, w), lambda c: (c, 0, 0))
    out = jax.ShapeDtypeStruct((n1, n2, w), F32)
    return pl.pallas_call(
        _hy_spec_kernel,
        grid=(n1 // cb,),
        in_specs=[mblk, mblk, blk, blk, pl.BlockSpec((1, w), lambda c: (0, 0))],
        out_specs=[oblk, oblk],
        out_shape=[out, out],
        compiler_params=_cp("parallel"),
        name="hyena_filter_spectrum",
    )(*m1, are, aim, rs)


def _hy_mid(m1, m2, are, aim, hre, him):
    bx, n1, n2, w = are.shape
    cb = min(4, n1)
    blk = pl.BlockSpec((1, cb, n2, w), lambda c, b: (b, c, 0, 0))
    mblk = pl.BlockSpec((cb, 2 * n2, 2 * n2), lambda c, b: (c, 0, 0))
    hblk = pl.BlockSpec((cb, n2, w), lambda c, b: (c, 0, 0))
    out = jax.ShapeDtypeStruct((bx, n1, n2, w), F32)
    return pl.pallas_call(
        _hy_mid_kernel,
        grid=(n1 // cb, bx),
        in_specs=[mblk, mblk, mblk, mblk, blk, blk, hblk, hblk],
        out_specs=[blk, blk],
        out_shape=[out, out],
        compiler_params=_cp("parallel", "parallel"),
        name="hyena_dft_mid",
    )(*m1, *m2, are, aim, hre, him)


def _hy_last_kernel(fh_ref, fl_ref, re_ref, im_ref, p_ref, bias_ref, o_ref):
    _, k, tb, w = p_ref.shape
    spec = jnp.concatenate([re_ref[0].reshape(-1, w), im_ref[0].reshape(-1, w)], axis=0)
    y = _dot3(fh_ref[...], fl_ref[...], spec)
    o_ref[0] = y.reshape(k, tb, w) + p_ref[0] * bias_ref[...]


def _hy_last(f4, bre, bim, p4, bias):
    bx, n1, n2, w = bre.shape
    k = n1 // 2
    tb = HY_TB
    big = pl.BlockSpec((1, n1, tb, w), lambda b, j: (b, 0, j, 0))
    small = pl.BlockSpec((1, k, tb, w), lambda b, j: (b, 0, j, 0))
    return pl.pallas_call(
        _hy_last_kernel,
        grid=(bx, n2 // tb),
        in_specs=[pl.BlockSpec(f4[0].shape, lambda b, j: (0, 0))] * 2 + [big, big, small,
                  pl.BlockSpec((1, w), lambda b, j: (0, 0))],
        out_specs=small,
        out_shape=jax.ShapeDtypeStruct((bx, k, n2, w), F32),
        compiler_params=_cp("parallel", "parallel"),
        name="hyena_dft_last",
    )(*f4, bre, bim, p4, bias)


def _hyena(zb, conv_w, conv_b, filt_params, hy_bias, consts):
    bx, lx, _ = zb.shape
    feat, (f1, m1, m2, f4) = consts
    n = 2 * lx
    n1, n2 = _dft_factors(n)
    w = HY_WIDTH
    filt, ssq = _hy_filter(lx, feat, *filt_params)
    rs = lax.rsqrt(ssq + EPS)
    fre, fim = _hy_stage1(_hi_lo(_per_offset(f1)), filt.reshape(1, n1, n2, w))
    hre, him = _hy_filter_spectrum(m1, fre, fim, rs)
    p, x0 = _hy_prep(zb, conv_w, conv_b)
    k = n1 // 2
    p4 = p.reshape(bx, k, n2, w)
    are, aim = _hy_stage1(_hi_lo(_per_offset(f1[:, :k])), p4)
    bre, bim = _hy_mid(m1, m2, are, aim, hre, him)
    t = _hy_last(f4, bre, bim, p4, hy_bias.reshape(1, w))
    return x0, t.reshape(bx, lx, w)


WA_SHIFT_MAX = 35.0


def _wattn_kernel(sk_ref, q_ref, bias_ref, *rest, banded, fixed):
    if banded:
        kp_ref, kc_ref, kn_ref, kx_ref, vp_ref, vc_ref, vn_ref, vx_ref, o_ref = rest
    else:
        kx_ref, vx_ref, o_ref = rest
    h = pl.program_id(1)
    i = pl.program_id(2)
    last = pl.num_programs(2) - 1
    qb = q_ref.shape[1]
    q = jnp.concatenate([q_ref[0, :, :LANES], q_ref[0, :, LANES:]], axis=0)
    if banded:
        kk = jnp.concatenate([kp_ref[0], kc_ref[0], kn_ref[0], kx_ref[0]], axis=0)
        vv = jnp.concatenate([vp_ref[0], vc_ref[0], vn_ref[0], vx_ref[0]], axis=0)
    else:
        kk, vv = kx_ref[0], vx_ref[0]
    s = lax.dot_general(q, kk, (((1,), (1,)), ((), ())), preferred_element_type=F32) + bias_ref[0]
    if banded:
        c = lax.broadcasted_iota(jnp.int32, (1, kk.shape[0]), 1) - BLOCK
        outside = ((c < 0) & (i == 0)) | ((c >= qb) & (c < qb + BLOCK) & (i == last))
        s = s + jnp.where(outside, NEG_INF, 0.0)
    top = lax.broadcasted_iota(jnp.int32, (2 * qb, 1), 0) < qb
    sk = jnp.where(top, sk_ref[h * WA_GROUP], sk_ref[h * WA_GROUP + 1])
    if fixed:
        p, sink_term = jnp.exp2(s), sk
    else:
        m = jnp.maximum(jnp.max(s, axis=-1, keepdims=True), sk)
        p, sink_term = jnp.exp2(s - m), jnp.exp2(sk - m)
    acc = jnp.dot(p.astype(BF16), vv, preferred_element_type=F32)
    o = (acc * (1.0 / (acc[:, WA_DIM:WA_DIM + 1] + sink_term))).astype(BF16)
    o_ref[0] = jnp.concatenate([o[:qb], o[qb:]], axis=1)


def _window_attention(q, k, v, kx, vx, sink, bound, banded):
    b, lq, _ = q.shape
    cx = kx.shape[1]
    qb = min(256, lq)
    per = qb // BLOCK
    nblk = lq // BLOCK
    side = lambda f: pl.BlockSpec((1, BLOCK, LANES), f)
    prev = side(lambda b_, h, i: (b_, jnp.maximum(i * per - 1, 0), h))
    nxt = side(lambda b_, h, i: (b_, jnp.minimum((i + 1) * per, nblk - 1), h))
    cur = pl.BlockSpec((1, qb, LANES), lambda b_, h, i: (b_, i, h))
    ctx = pl.BlockSpec((1, cx, LANES), lambda b_, h, i: (b_, 0, h))
    qspec = pl.BlockSpec((1, qb, WA_GROUP * LANES), lambda b_, h, i: (b_, i, h))
    mask = jnp.zeros((WA_GROUP * qb, cx), F32)
    if banded:
        r = jnp.arange(WA_GROUP * qb)[:, None] % qb
        c = jnp.arange(qb + 2 * BLOCK)[None, :] - BLOCK
        mask = jnp.concatenate([jnp.where(jnp.abs(r - c) <= BLOCK, 0.0, NEG_INF).astype(F32), mask], axis=1)
        specs, args = [prev, cur, nxt, ctx, prev, cur, nxt, ctx], (k, k, k, kx, v, v, v, vx)
    else:
        specs, args = [ctx, ctx], (kx, vx)
    fixed = bound <= WA_SHIFT_MAX
    shift = jnp.maximum(bound, sink)
    rows = jnp.repeat(shift.reshape(WA_KV_HEADS, WA_GROUP), qb, axis=1)[:, :, None]
    bias = mask[None] - jnp.where(fixed, LOG2E * rows, 0.0)
    sk = jnp.where(fixed, jnp.exp2(LOG2E * (sink - shift)), LOG2E * sink)

    def call(fixed_, *ops):
        return pl.pallas_call(
            functools.partial(_wattn_kernel, banded=banded, fixed=fixed_),
            grid=(b, WA_KV_HEADS, lq // qb),
            in_specs=[pl.BlockSpec(memory_space=pltpu.SMEM), qspec,
                      pl.BlockSpec((1,) + bias.shape[1:], lambda b_, h, i: (h, 0, 0))] + specs,
            out_specs=qspec,
            out_shape=jax.ShapeDtypeStruct((b, lq, WA_HEADS * LANES), BF16),
            compiler_params=_cp("parallel", "parallel", "arbitrary"),
            name="window_attention" if fixed_ else "window_attention_online",
        )(*ops)

    return lax.cond(fixed, functools.partial(call, True), functools.partial(call, False), sk, q, bias, *args)


def _conf_kernel(zp_ref, zc_ref, zn_ref, w_ref, b_ref, lg_ref, lb_ref, o_ref, ext, sh, *, halo):
    i = pl.program_id(1)
    last = pl.num_programs(1) - 1
    tl = zc_ref.shape[1]
    cw = CF_WIDTH

    def glu(z_ref):
        z = z_ref[0].astype(F32)
        return z[:, :cw] * jax.nn.sigmoid(z[:, cw:])

    ext[0:halo] = jnp.where(i == 0, 0.0, glu(zp_ref))
    ext[halo:halo + tl] = glu(zc_ref)
    ext[halo + tl:] = jnp.where(i == last, 0.0, glu(zn_ref))
    for r in range(1, SUBLANES):
        sh[r - 1] = ext[pl.ds(r, sh.shape[1]), :]
    w = w_ref[...]
    u = jnp.zeros((tl, cw), F32) + b_ref[...]
    for j in range(CF_TAPS):
        off = halo - CF_TAPS // 2 + j
        base, r = off // SUBLANES * SUBLANES, off % SUBLANES
        tap = ext[pl.ds(base, tl), :] if r == 0 else sh[r - 1, pl.ds(base, tl), :]
        u = u + tap * w[j:j + 1]
    uc = u - jnp.mean(u, axis=-1, keepdims=True)
    y = uc * lax.rsqrt(jnp.mean(uc * uc, axis=-1, keepdims=True) + EPS) * lg_ref[...] + lb_ref[...]
    o_ref[0] = _silu(y).astype(BF16)


def _conformer(zd, dw_w, dw_b, ln_g, ln_b):
    bx, lx, w = zd.shape
    tl = min(256, lx)
    halo = 2 * SUBLANES
    nh = lx // halo
    per = tl // halo
    row = lambda a: a.reshape(1, -1)
    const = lambda b, i: (0, 0)
    return pl.pallas_call(
        functools.partial(_conf_kernel, halo=halo),
        grid=(bx, lx // tl),
        in_specs=[pl.BlockSpec((1, halo, w), lambda b, i: (b, jnp.maximum(i * per - 1, 0), 0)),
                  pl.BlockSpec((1, tl, w), lambda b, i: (b, i, 0)),
                  pl.BlockSpec((1, halo, w), lambda b, i: (b, jnp.minimum((i + 1) * per, nh - 1), 0)),
                  pl.BlockSpec(dw_w.shape, const)] + [pl.BlockSpec((1, CF_WIDTH), const)] * 3,
        out_specs=pl.BlockSpec((1, tl, CF_WIDTH), lambda b, i: (b, i, 0)),
        out_shape=jax.ShapeDtypeStruct((bx, lx, CF_WIDTH), BF16),
        scratch_shapes=[pltpu.VMEM((tl + 2 * halo, CF_WIDTH), F32),
                        pltpu.VMEM((SUBLANES - 1, tl + 2 * halo - SUBLANES, CF_WIDTH), F32)],
        compiler_params=_cp("parallel", "parallel"),
        name="conformer_conv",
    )(zd, zd, zd, dw_w, row(dw_b), row(ln_g), row(ln_b))


def _merge_kernel(x_ref, m_ref, ya, yb0, ybt, yc, yd, zg_ref, bg_ref, wa, wb, wc, wd, wo_ref, o_ref, *, goff):
    d = x_ref.shape[-1]
    acc = jnp.zeros(x_ref.shape[1:], F32)
    ys = (ya[0], (yb0[0] * ybt[0]).astype(BF16), yc[0], yd[0])
    for i, (y, w) in enumerate(zip(ys, (wa, wb, wc, wd))):
        gate = jax.nn.sigmoid(zg_ref[0, :, i * d:(i + 1) * d].astype(F32) + bg_ref[:, i * d:(i + 1) * d])
        acc = acc + gate * jnp.dot(y, w[...], preferred_element_type=F32)
    out = jnp.dot(acc.astype(BF16), wo_ref[...], preferred_element_type=F32)
    o_ref[0] = x_ref[0] + m_ref[0, :, goff:goff + d] * out


def _merge(x, mod, ys, zg, b_gate, wbs, w_out):
    bx, lx, d = x.shape
    tm = min(512, lx)
    row = lambda b, i: (b, i, 0)
    const = lambda b, i: (0, 0)
    once = lambda a: pl.BlockSpec(a.shape, const, pipeline_mode=pl.Buffered(1))
    return pl.pallas_call(
        functools.partial(_merge_kernel, goff=2 * d),
        grid=(bx, lx // tm),
        in_specs=[pl.BlockSpec((1, tm, d), row), pl.BlockSpec((1, 1, mod.shape[-1]), lambda b, i: (b, 0, 0))]
                 + [pl.BlockSpec((1, tm, y.shape[-1]), row) for y in ys]
                 + [pl.BlockSpec((1, tm, N_BRANCH * d), row), pl.BlockSpec((1, N_BRANCH * d), const)]
                 + [once(w) for w in wbs] + [once(w_out)],
        out_specs=pl.BlockSpec((1, tm, d), row),
        out_shape=jax.ShapeDtypeStruct(x.shape, F32),
        input_output_aliases={0: 0},
        compiler_params=_cp("parallel", "parallel"),
        name="merge",
    )(x, mod, *ys, zg, b_gate.reshape(1, -1), *wbs, w_out)


def _slot_rows(w, group):
    n = w.shape[0] // group
    return jnp.pad(w.reshape(n, group, -1), ((0, 0), (0, LANES - group), (0, 0))).reshape(n * LANES, -1)


def _moe_kernel(x_ref, m_ref, g_ref, wr_ref, br_ref, w1_ref, w3_ref, w2_ref, o_ref, h_sc, gate_sc, acc_sc):
    e = pl.program_id(2)
    d = x_ref.shape[-1]
    tm = x_ref.shape[1]
    lane = lax.broadcasted_iota(jnp.int32, (tm, LANES), 1).astype(F32)

    @pl.when(e == 0)
    def _():
        h = _rms(x_ref[0]) * g_ref[...] * (1.0 + m_ref[0, :, 4 * d:5 * d]) + m_ref[0, :, 3 * d:4 * d]
        h_sc[...] = h.astype(BF16)
        lg = _dot3(*_hi_lo(h), wr_ref[...]) + br_ref[...]
        isg = lane < N_GROUPS
        gmax = jnp.max(jnp.where(isg, lg, NEG_INF), axis=-1, keepdims=True)
        gi = jnp.min(jnp.where(isg & (lg == gmax), lane, LANES), axis=-1, keepdims=True)
        gw = 1.0 / jnp.sum(jnp.where(isg, jnp.exp(lg - gmax), 0.0), axis=-1, keepdims=True)
        lo = N_GROUPS + gi * EXP_PER_GROUP
        ise = (lane >= lo) & (lane < lo + EXP_PER_GROUP)
        le = jnp.where(ise, lg, NEG_INF)
        m1 = jnp.max(le, axis=-1, keepdims=True)
        i1 = jnp.min(jnp.where(ise & (le == m1), lane, LANES), axis=-1, keepdims=True)
        ise2 = ise & (lane != i1)
        le2 = jnp.where(ise2, lg, NEG_INF)
        m2 = jnp.max(le2, axis=-1, keepdims=True)
        i2 = jnp.min(jnp.where(ise2 & (le2 == m2), lane, LANES), axis=-1, keepdims=True)
        r = jnp.exp(m2 - m1)
        wa = gw / (1.0 + r)
        gate_sc[...] = jnp.where(lane == i1, wa, 0.0) + jnp.where(lane == i2, wa * r, 0.0)
        acc_sc[...] = jnp.zeros_like(acc_sc)

    hb = h_sc[...]
    u = (_silu(jnp.dot(hb, w1_ref[0].astype(BF16), preferred_element_type=F32))
         * jnp.dot(hb, w3_ref[0].astype(BF16), preferred_element_type=F32))
    ge = jnp.sum(jnp.where(lane == (e + N_GROUPS).astype(F32), gate_sc[...], 0.0), axis=-1, keepdims=True)
    acc_sc[...] += ge * jnp.dot(u.astype(BF16), w2_ref[0].astype(BF16), preferred_element_type=F32)

    @pl.when(e == pl.num_programs(2) - 1)
    def _():
        o_ref[0] = x_ref[0] + m_ref[0, :, 5 * d:6 * d] * acc_sc[...]


def _moe(x, mod, g, w_router, b_router, w1, w3, w2):
    bx, lx, d = x.shape
    tm = min(1024, lx)
    ne, _, f = w1.shape
    row = lambda b, i, e: (b, i, 0)
    const = lambda b, i, e: (0, 0)
    return pl.pallas_call(
        _moe_kernel,
        grid=(bx, lx // tm, ne),
        in_specs=[pl.BlockSpec((1, tm, d), row), pl.BlockSpec((1, 1, mod.shape[-1]), lambda b, i, e: (b, 0, 0)),
                  pl.BlockSpec((1, d), const), pl.BlockSpec((d, LANES), const), pl.BlockSpec((1, LANES), const),
                  pl.BlockSpec((1, d, f), lambda b, i, e: (e, 0, 0)), pl.BlockSpec((1, d, f), lambda b, i, e: (e, 0, 0)),
                  pl.BlockSpec((1, f, d), lambda b, i, e: (e, 0, 0))],
        out_specs=pl.BlockSpec((1, tm, d), row),
        out_shape=jax.ShapeDtypeStruct(x.shape, F32),
        scratch_shapes=[pltpu.VMEM((tm, d), BF16), pltpu.VMEM((tm, LANES), F32), pltpu.VMEM((tm, d), F32)],
        input_output_aliases={0: 0},
        compiler_params=_cp("parallel", "parallel", "arbitrary"),
        name="moe",
    )(x, mod, g.reshape(1, d), w_router, b_router, w1, w3, w2)


SC_CORES = 2
SC_SUBCORES = 16
SC_CHUNK = 64
MOE_ROWS = 1024


def _sc_gather(table, idx):
    n = idx.shape[0]
    w = table.shape[1]
    per = n // (SC_CORES * SC_SUBCORES)
    assert per * SC_CORES * SC_SUBCORES == n and per % SC_CHUNK == 0
    mesh = plsc.VectorSubcoreMesh(core_axis_name="c", subcore_axis_name="s")

    @functools.partial(
        pl.kernel, mesh=mesh, out_type=jax.ShapeDtypeStruct((n, w), table.dtype),
        scratch_types=[pltpu.VMEM((SC_CHUNK,), jnp.int32), pltpu.VMEM((SC_CHUNK, w), table.dtype),
                       pltpu.SemaphoreType.DMA],
        name="sc_row_gather")
    def gather(table_hbm, idx_hbm, out_hbm, idx_v, rows_v, sem):
        base = (lax.axis_index("s") * SC_CORES + lax.axis_index("c")) * per

        @pl.loop(0, per // SC_CHUNK)
        def _(j):
            off = pl.multiple_of(base + j * SC_CHUNK, SC_CHUNK)
            pltpu.sync_copy(idx_hbm.at[pl.ds(off, SC_CHUNK)], idx_v)
            pltpu.async_copy(table_hbm.at[idx_v], rows_v, sem).wait()
            pltpu.sync_copy(rows_v, out_hbm.at[pl.ds(off, SC_CHUNK)])

    return gather(table, idx)


def _sc_scatter(rows, idx, n_out):
    n, w = rows.shape
    per = n // (SC_CORES * SC_SUBCORES)
    assert per * SC_CORES * SC_SUBCORES == n and per % SC_CHUNK == 0
    mesh = plsc.VectorSubcoreMesh(core_axis_name="c", subcore_axis_name="s")

    @functools.partial(
        pl.kernel, mesh=mesh, out_type=jax.ShapeDtypeStruct((n_out, w), rows.dtype),
        scratch_types=[pltpu.VMEM((SC_CHUNK,), jnp.int32), pltpu.VMEM((SC_CHUNK, w), rows.dtype),
                       pltpu.SemaphoreType.DMA],
        name="sc_row_scatter")
    def scatter(rows_hbm, idx_hbm, out_hbm, idx_v, rows_v, sem):
        base = (lax.axis_index("s") * SC_CORES + lax.axis_index("c")) * per

        @pl.loop(0, per // SC_CHUNK)
        def _(j):
            off = pl.multiple_of(base + j * SC_CHUNK, SC_CHUNK)
            pltpu.sync_copy(idx_hbm.at[pl.ds(off, SC_CHUNK)], idx_v)
            pltpu.sync_copy(rows_hbm.at[pl.ds(off, SC_CHUNK)], rows_v)
            pltpu.async_copy(rows_v, out_hbm.at[idx_v], sem).wait()

    return scatter(rows, idx)


def _route_kernel(x_ref, m_ref, g_ref, wr_ref, br_ref, rows_ref, gi_ref):
    d = x_ref.shape[-1]
    tm = x_ref.shape[1]
    lane = lax.broadcasted_iota(jnp.int32, (tm, LANES), 1).astype(F32)
    h = _rms(x_ref[0]) * g_ref[...] * (1.0 + m_ref[0, :, 4 * d:5 * d]) + m_ref[0, :, 3 * d:4 * d]
    rows_ref[0, :, :d] = h
    lg = _dot3(*_hi_lo(h), wr_ref[...]) + br_ref[...]
    isg = lane < N_GROUPS
    gmax = jnp.max(jnp.where(isg, lg, NEG_INF), axis=-1, keepdims=True)
    gi = jnp.min(jnp.where(isg & (lg == gmax), lane, LANES), axis=-1, keepdims=True)
    gw = 1.0 / jnp.sum(jnp.where(isg, jnp.exp(lg - gmax), 0.0), axis=-1, keepdims=True)
    lo = N_GROUPS + gi * EXP_PER_GROUP
    ise = (lane >= lo) & (lane < lo + EXP_PER_GROUP)
    le = jnp.where(ise, lg, NEG_INF)
    m1 = jnp.max(le, axis=-1, keepdims=True)
    i1 = jnp.min(jnp.where(ise & (le == m1), lane, LANES), axis=-1, keepdims=True)
    ise2 = ise & (lane != i1)
    le2 = jnp.where(ise2, lg, NEG_INF)
    m2 = jnp.max(le2, axis=-1, keepdims=True)
    i2 = jnp.min(jnp.where(ise2 & (le2 == m2), lane, LANES), axis=-1, keepdims=True)
    r = jnp.exp(m2 - m1)
    wa = gw / (1.0 + r)
    rows_ref[0, :, d:] = jnp.where(lane == i1 - lo, wa, 0.0) + jnp.where(lane == i2 - lo, wa * r, 0.0)
    gi_ref[0] = gi.astype(jnp.int32)


def _route(x, mod, g, w_router, b_router):
    bx, lx, d = x.shape
    tm = min(512, lx)
    row = lambda b, i: (b, i, 0)
    const = lambda b, i: (0, 0)
    return pl.pallas_call(
        _route_kernel,
        grid=(bx, lx // tm),
        in_specs=[pl.BlockSpec((1, tm, d), row), pl.BlockSpec((1, 1, mod.shape[-1]), lambda b, i: (b, 0, 0)),
                  pl.BlockSpec((1, d), const), pl.BlockSpec((d, LANES), const), pl.BlockSpec((1, LANES), const)],
        out_specs=[pl.BlockSpec((1, tm, d + LANES), row), pl.BlockSpec((1, tm, 1), row)],
        out_shape=[jax.ShapeDtypeStruct((bx, lx, d + LANES), F32), jax.ShapeDtypeStruct((bx, lx, 1), jnp.int32)],
        compiler_params=_cp("parallel", "parallel"),
        name="moe_route",
    )(x, mod, g.reshape(1, d), w_router, b_router)


def _gmoe_kernel(grp_ref, nv_ref, xs_ref, w1_ref, w3_ref, w2_ref, o_ref, h_sc, acc_sc):
    i = pl.program_id(0)
    e = pl.program_id(1)
    tm, d = h_sc.shape
    valid = lax.broadcasted_iota(jnp.int32, (tm, 1), 0) < nv_ref[i]

    @pl.when(nv_ref[i] > 0)
    def _():
        @pl.when(e == 0)
        def _():
            h_sc[...] = jnp.where(valid, xs_ref[:, :d], 0.0).astype(BF16)
            acc_sc[...] = jnp.zeros_like(acc_sc)

        hb = h_sc[...]
        u = (_silu(jnp.dot(hb, w1_ref[0].astype(BF16), preferred_element_type=F32))
             * jnp.dot(hb, w3_ref[0].astype(BF16), preferred_element_type=F32))
        lane = lax.broadcasted_iota(jnp.int32, (tm, LANES), 1)
        ge = jnp.sum(jnp.where(valid & (lane == e), xs_ref[:, d:], 0.0), axis=-1, keepdims=True)
        acc_sc[...] += ge * jnp.dot(u.astype(BF16), w2_ref[0].astype(BF16), preferred_element_type=F32)

    @pl.when(e == pl.num_programs(1) - 1)
    def _():
        o_ref[...] = jnp.where(nv_ref[i] > 0, acc_sc[...], 0.0)


def _grouped_moe(grp, nv, xs, w1, w3, w2):
    p, dw = xs.shape
    d = dw - LANES
    _, _, f = w1.shape
    wmap = lambda i, e, grp, nv: (grp[i] * EXP_PER_GROUP + e, 0, 0)
    rows = lambda i, e, grp, nv: (i, 0)
    return pl.pallas_call(
        _gmoe_kernel,
        grid_spec=pltpu.PrefetchScalarGridSpec(
            num_scalar_prefetch=2,
            grid=(p // MOE_ROWS, EXP_PER_GROUP),
            in_specs=[pl.BlockSpec((MOE_ROWS, dw), rows),
                      pl.BlockSpec((1, d, f), wmap), pl.BlockSpec((1, d, f), wmap), pl.BlockSpec((1, f, d), wmap)],
            out_specs=pl.BlockSpec((MOE_ROWS, d), rows),
            scratch_shapes=[pltpu.VMEM((MOE_ROWS, d), BF16), pltpu.VMEM((MOE_ROWS, d), F32)]),
        out_shape=jax.ShapeDtypeStruct((p, d), F32),
        compiler_params=_cp("arbitrary", "arbitrary"),
        name="moe_experts",
    )(grp, nv, xs, w1, w3, w2)


def _residual_kernel(x_ref, m_ref, y_ref, o_ref):
    d = x_ref.shape[-1]
    o_ref[0] = x_ref[0] + m_ref[0, :, 5 * d:6 * d] * y_ref[0]


def _residual(x, mod, y):
    bx, lx, d = x.shape
    tm = min(1024, lx)
    row = lambda b, i: (b, i, 0)
    return pl.pallas_call(
        _residual_kernel,
        grid=(bx, lx // tm),
        in_specs=[pl.BlockSpec((1, tm, d), row), pl.BlockSpec((1, 1, mod.shape[-1]), lambda b, i: (b, 0, 0)),
                  pl.BlockSpec((1, tm, d), row)],
        out_specs=pl.BlockSpec((1, tm, d), row),
        out_shape=jax.ShapeDtypeStruct(x.shape, F32),
        input_output_aliases={0: 0},
        compiler_params=_cp("parallel", "parallel"),
        name="moe_residual",
    )(x, mod, y)


def _moe_sorted(x, mod, g, w_router, b_router, w1, w3, w2):
    bx, lx, d = x.shape
    t = bx * lx
    rows, gi = _route(x, mod, g, w_router, b_router)
    gi = gi.reshape(t)
    onehot = (gi[:, None] == jnp.arange(N_GROUPS, dtype=jnp.int32)[None, :]).astype(jnp.int32)
    csum = jnp.cumsum(onehot, axis=0)
    counts = csum[-1]
    rank = jnp.take_along_axis(csum, gi[:, None], axis=1)[:, 0] - 1
    padded = (counts + MOE_ROWS - 1) // MOE_ROWS * MOE_ROWS
    pend = jnp.cumsum(padded)
    pstart = pend - padded
    pos = (pstart[gi] + rank).astype(jnp.int32)
    p = t + N_GROUPS * MOE_ROWS
    bstart = jnp.arange(p // MOE_ROWS, dtype=jnp.int32) * MOE_ROWS
    grp = jnp.minimum(jnp.searchsorted(pend, bstart, side="right"), N_GROUPS - 1).astype(jnp.int32)
    nv = jnp.clip(pstart[grp] + counts[grp] - bstart, 0, MOE_ROWS).astype(jnp.int32)
    xs = _sc_scatter(rows.reshape(t, d + LANES), pos, p)
    ys = _grouped_moe(grp, nv, xs, w1, w3, w2)
    yt = _sc_gather(ys, pos)
    return _residual(x, mod, yt.reshape(bx, lx, d))


def kernel(x, c, ctx, c_ctx, w_mod, b_mod, norm1_g, norm2_g, w_in, b_gate, da_qn, da_kn, da_lam, da_subln, hy_conv_w, hy_conv_b, hf_w1, hf_b1, hf_w2, hf_b2, hf_w3, hf_b3, hf_w4, hf_freq, hy_bias, wa_qn, wa_kn, wa_sink, cf_dw_w, cf_dw_b, cf_ln_g, cf_ln_b, w_branch, w_out, w_rg, b_rg, w_re, b_re, w1, w3, w2):
    b, s, d = x.shape
    cl = ctx.shape[1]
    depth = w_mod.shape[0]
    assert s % 256 == 0 and cl % 256 == 0 and s % GRID_W == 0

    nrow = -(-(b + 1) // SUBLANES) * SUBLANES
    crows = jnp.zeros((nrow, d), F32).at[:b].set(c).at[b].set(c_ctx)
    mods = _mod_vectors(crows, w_mod, b_mod)

    aw, qw, kw = DA_HEADS * DA_DIM, WA_HEADS * WA_DIM, WA_KV_HEADS * WA_DIM
    tab_lat = (*_rope_tables(s, DA_DIM, DA_HEADS), *_rope_tables(s, WA_DIM, WA_HEADS), *_rope_tables(s, WA_DIM, WA_KV_HEADS))
    tab_ctx = (*_unit_tables(cl, aw), *_unit_tables(cl, qw), *_unit_tables(cl, kw))
    hy_lat = (_hy_feat(s), _dft_tables(2 * s))
    hy_ctx = (_hy_feat(cl), _dft_tables(2 * cl))
    gms = (_group_ones(aw, DA_DIM), _group_ones(qw, WA_DIM), _group_ones(kw, WA_DIM))
    sels = (_slot_select(aw, DA_DIM), _slot_select(DA_HEADS * DA_VDIM, DA_VDIM), _slot_select(qw, WA_DIM),
            _slot_select(kw, WA_DIM))
    qvec = _slot_vector(DA_HEADS, DA_DIM, DA_DIM + 1, 1.0)
    vvec = _slot_vector(DA_HEADS, DA_VDIM, DA_VDIM + DA_ONES, 1.0)
    vvecc = _slot_vector(WA_KV_HEADS, WA_DIM, WA_DIM + DA_ONES, 1.0)

    xc = ctx
    for l in range(depth):
        last = l == depth - 1
        lam_init = 0.8 - 0.6 * math.exp(-0.3 * l)
        mod_x = mods[l, :b][:, None, :]
        mod_c = jnp.broadcast_to(mods[l, b][None, None, :], (b, 1, mods.shape[-1]))
        wl = w_in[l]
        ws = [wl[:, 0:OFF_B].astype(BF16), wl[:, OFF_B:OFF_C].astype(BF16), wl[:, OFF_C:OFF_D].astype(BF16),
              wl[:, OFF_D:OFF_G].astype(BF16), wl[:, OFF_G:].astype(BF16)]
        shift = 1.02 * LOG2E * DA_DIM ** 0.5 * jnp.max(jnp.abs(da_qn[l])) * jnp.max(jnp.abs(da_kn[l]))
        fixed = shift <= DA_SHIFT_MAX
        kvec = _slot_vector(DA_HEADS, DA_DIM, DA_DIM + 1, jnp.where(fixed, -shift, 0.0))
        tile = lambda a, n: jnp.tile(a, n).reshape(1, -1)
        consts = (*gms, tile(da_qn[l], DA_HEADS), tile(da_kn[l], DA_HEADS), tile(wa_qn[l], WA_HEADS),
                  tile(wa_kn[l], WA_KV_HEADS), *sels, qvec, kvec, vvec, vvecc)
        q1, q2, k1, k2, v, zb, qc, kc, vc, zd, zg = _in_proj(x, mod_x, norm1_g[l], ws, tab_lat, consts)
        q1x, q2x, k1x, k2x, vx, zbx, qcx, kcx, vcx, zdx, zgx = _in_proj(xc, mod_c, norm1_g[l], ws, tab_ctx, consts)
        filt_params = (hf_w1[l], hf_b1[l], hf_w2[l], hf_b2[l], hf_w3[l], hf_b3[l], hf_w4[l], hf_freq[l])
        wb4 = w_branch[l].astype(BF16)
        wbs = (_slot_rows(wb4[0], DA_VDIM), wb4[1], _slot_rows(wb4[2], WA_DIM), wb4[3])
        wo = w_out[l].astype(BF16)

        ya = _diff_attention(q1, q2, [(k1, k2, v), (k1x, k2x, vx)], fixed, da_lam[l], da_subln[l], lam_init)
        yb = _hyena(zb, hy_conv_w[l], hy_conv_b[l], filt_params, hy_bias[l], hy_lat)
        wbound = 1.02 * WA_DIM ** 0.5 * jnp.max(jnp.abs(wa_qn[l])) * jnp.max(jnp.abs(wa_kn[l]))
        yc_ = _window_attention(qc, kc, vc, kcx, vcx, wa_sink[l], wbound, True)
        yd = _conformer(zd, cf_dw_w[l], cf_dw_b[l], cf_ln_g[l], cf_ln_b[l])
        x = _merge(x, mod_x, (ya, *yb, yc_, yd), zg, b_gate[l], wbs, wo)

        w_router = jnp.pad(jnp.concatenate([w_rg[l], w_re[l]], axis=1), ((0, 0), (0, LANES - N_GROUPS - N_EXPERTS)))
        b_router = jnp.pad(jnp.concatenate([b_rg[l], b_re[l]]), (0, LANES - N_GROUPS - N_EXPERTS)).reshape(1, LANES)
        ew = (w1[l], w3[l], w2[l])

        if not last:
            yca = _diff_attention(q1x, q2x, [(k1x, k2x, vx)], fixed, da_lam[l], da_subln[l], lam_init)
            ycb = _hyena(zbx, hy_conv_w[l], hy_conv_b[l], filt_params, hy_bias[l], hy_ctx)
            ycc = _window_attention(qcx, kcx, vcx, kcx, vcx, wa_sink[l], wbound, False)
            ycd = _conformer(zdx, cf_dw_w[l], cf_dw_b[l], cf_ln_g[l], cf_ln_b[l])
            xc = _merge(xc, mod_c, (yca, *ycb, ycc, ycd), zgx, b_gate[l], wbs, wo)
            xc = _moe(xc.reshape(1, b * cl, d), mod_c[:1], norm2_g[l], w_router, b_router, *ew).reshape(b, cl, d)
        x = _moe_sorted(x, mod_x, norm2_g[l], w_router, b_router, *ew)
    return x
```

```python
import functools
import math

import jax
import jax.numpy as jnp
from jax import lax
from jax.experimental import pallas as pl
from jax.experimental.pallas import tpu as pltpu
from jax.experimental.pallas import tpu_sc as plsc

F32 = jnp.float32
BF16 = jnp.bfloat16
HI = lax.Precision.HIGHEST

GRID_W = 64
BLOCK = 128
ROPE_BASE = 10000.0
EPS = 1e-6
NEG_INF = -1e30

DA_HEADS = 4
DA_DIM = 32
DA_VDIM = 64
HY_WIDTH = 256
HY_BANDS = 16
HY_FF = 64
HY_SHIFT = 0.05
HY_FAST_DECAY = 0.3
HY_SLOW_DECAY = 1.5
HY_TARGET = 1e-2
WA_HEADS = 4
WA_KV_HEADS = 2
WA_GROUP = 2
WA_DIM = 64
CF_WIDTH = 256
CF_TAPS = 31
N_BRANCH = 4
BRANCH_W = 256
N_GROUPS = 4
EXP_PER_GROUP = 4
N_EXPERTS = 16

W_A = 4 * DA_HEADS * DA_DIM + DA_HEADS * DA_VDIM
W_B = 3 * HY_WIDTH
W_C = (WA_HEADS + 2 * WA_KV_HEADS) * WA_DIM
W_D = 2 * CF_WIDTH
OFF_B = W_A
OFF_C = OFF_B + W_B
OFF_D = OFF_C + W_C
OFF_G = OFF_D + W_D

LOG2E = math.log2(math.e)
LANES = 128
SUBLANES = 8
VMEM_LIMIT = 56 * 1024 * 1024

DA_ONES = 16
DA_SHIFT_MAX = 50.0


def _cp(*sem):
    return pltpu.CompilerParams(dimension_semantics=sem, vmem_limit_bytes=VMEM_LIMIT)


def _rms(xf):
    return xf * lax.rsqrt(jnp.mean(xf * xf, axis=-1, keepdims=True) + EPS)


def _silu(x):
    return x * jax.nn.sigmoid(x)


def _mod_kernel(c_ref, w_ref, b_ref, o_ref):
    s = _silu(c_ref[...])
    o_ref[0] = jnp.dot(s, w_ref[0], precision=HI, preferred_element_type=F32) + b_ref[0]


def _mod_vectors(crows, w_mod, b_mod):
    depth, d, n = w_mod.shape
    r = crows.shape[0]
    tn = 1536
    return pl.pallas_call(
        _mod_kernel,
        grid=(depth, n // tn),
        in_specs=[pl.BlockSpec((r, d), lambda l, j: (0, 0)),
                  pl.BlockSpec((1, d, tn), lambda l, j: (l, 0, j)),
                  pl.BlockSpec((1, 1, tn), lambda l, j: (l, 0, j))],
        out_specs=pl.BlockSpec((1, r, tn), lambda l, j: (l, 0, j)),
        out_shape=jax.ShapeDtypeStruct((depth, r, n), F32),
        compiler_params=_cp("arbitrary", "arbitrary"),
        name="mod_vectors",
    )(crows, w_mod, b_mod.reshape(depth, 1, n))


def _rope_tables(s, d, reps):
    rows = s // GRID_W
    row = jnp.repeat(jnp.arange(rows, dtype=F32), GRID_W)
    col = jnp.tile(jnp.arange(GRID_W, dtype=F32), rows)
    qd = d // 4
    inv = ROPE_BASE ** (-jnp.arange(qd, dtype=F32) / qd)
    ar = row[:, None] * inv[None, :]
    ac = col[:, None] * inv[None, :]
    z = jnp.zeros_like(ar)
    cos = jnp.concatenate([jnp.cos(ar), jnp.cos(ar), jnp.cos(ac), jnp.cos(ac)], axis=-1)
    sin_up = jnp.concatenate([-jnp.sin(ar), z, -jnp.sin(ac), z], axis=-1)
    sin_dn = jnp.concatenate([z, jnp.sin(ar), z, jnp.sin(ac)], axis=-1)
    t = lambda a: jnp.tile(a, (1, reps))
    return t(cos), t(sin_up), t(sin_dn)


def _unit_tables(s, w):
    return jnp.ones((s, w), F32), jnp.zeros((s, w), F32), jnp.zeros((s, w), F32)


def _group_ones(width, group):
    i = jnp.arange(width) // group
    return (i[:, None] == i[None, :]).astype(BF16)


def _hi_lo(a):
    hi = a.astype(BF16)
    return hi, (a - hi.astype(F32)).astype(BF16)


def _slot_select(width, group):
    i = jnp.arange(width)
    dst = (i // group) * LANES + i % group
    return (dst[:, None] == jnp.arange((width // group) * LANES)[None, :]).astype(BF16)


def _slot_vector(n, lo, hi, value):
    j = jnp.arange(n * LANES) % LANES
    return jnp.where((j >= lo) & (j < hi), value, 0.0).astype(F32).reshape(1, n * LANES)


def _norm_rope(x, gmat, gain, cos, sup, sdn, group, qd):
    w = x.shape[-1]
    sh, sl = _hi_lo(x * x)
    ss = (jnp.dot(sh, gmat, preferred_element_type=F32) + jnp.dot(sl, gmat, preferred_element_type=F32)) * (1.0 / group)
    xn = x * lax.rsqrt(ss + EPS) * gain
    return xn * cos + pltpu.roll(xn, w - qd, 1) * sup + pltpu.roll(xn, qd, 1) * sdn


def _inproj_kernel(x_ref, m_ref, g_ref, wa, wb, wc, wd, wg,
                   ca, ua, da, cq, uq, dq, ck, uk, dk, gma, gmq, gmk, qna, kna, qnc, knc,
                   sela, selv, selq, selk, qvec, kvec, vvec, vvecc,
                   q1o, q2o, k1o, k2o, vo, zbo, qco, kco, vco, zdo, zgo):
    d = x_ref.shape[-1]
    x = x_ref[0]
    shift = m_ref[0, :, 0:d]
    scale = m_ref[0, :, d:2 * d]
    h = (_rms(x) * g_ref[...] * (1.0 + scale) + shift).astype(BF16)
    place = lambda y, sel: jnp.dot(y.astype(BF16), sel[...], preferred_element_type=F32)

    za = jnp.dot(h, wa[...], preferred_element_type=F32)
    hw = DA_HEADS * DA_DIM
    cos, sup, sdn, gm = ca[...], ua[...], da[...], gma[...]
    qscale = DA_DIM ** -0.5 * LOG2E
    for t, (o, gain, sc, vec) in enumerate(((q1o, qna, qscale, qvec), (q2o, qna, qscale, qvec),
                                            (k1o, kna, 1.0, kvec), (k2o, kna, 1.0, kvec))):
        y = _norm_rope(za[:, t * hw:(t + 1) * hw], gm, gain[...], cos, sup, sdn, DA_DIM, DA_DIM // 4) * sc
        o[0] = (place(y, sela) + vec[...]).astype(BF16)
    vo[0] = (place(za[:, 4 * hw:], selv) + vvec[...]).astype(BF16)

    zbo[0] = jnp.dot(h, wb[...], preferred_element_type=F32).astype(BF16)

    zc = jnp.dot(h, wc[...], preferred_element_type=F32)
    qw = WA_HEADS * WA_DIM
    kw = WA_KV_HEADS * WA_DIM
    y = _norm_rope(zc[:, 0:qw], gmq[...], qnc[...], cq[...], uq[...], dq[...], WA_DIM, WA_DIM // 4) * (WA_DIM ** -0.5 * LOG2E)
    qco[0] = place(y, selq).astype(BF16)
    y = _norm_rope(zc[:, qw:qw + kw], gmk[...], knc[...], ck[...], uk[...], dk[...], WA_DIM, WA_DIM // 4)
    kco[0] = place(y, selk).astype(BF16)
    vco[0] = (place(zc[:, qw + kw:], selk) + vvecc[...]).astype(BF16)

    zdo[0] = jnp.dot(h, wd[...], preferred_element_type=F32).astype(BF16)
    for k in range(N_BRANCH):
        zgo[0, :, k * d:(k + 1) * d] = jnp.dot(h, wg[:, k * d:(k + 1) * d], preferred_element_type=F32).astype(BF16)


def _in_proj(x, mod, g, ws, tables, consts):
    bx, lx, d = x.shape
    tm = min(512, lx)
    const = lambda b, i: (0, 0)
    row = lambda b, i: (b, i, 0)
    once = lambda a: pl.BlockSpec(a.shape, const, pipeline_mode=pl.Buffered(1))
    widths = [DA_HEADS * LANES] * 5 + [W_B, WA_HEADS * LANES, WA_KV_HEADS * LANES, WA_KV_HEADS * LANES, W_D, N_BRANCH * d]
    return pl.pallas_call(
        _inproj_kernel,
        grid=(bx, lx // tm),
        in_specs=[pl.BlockSpec((1, tm, d), row), pl.BlockSpec((1, 1, mod.shape[-1]), lambda b, i: (b, 0, 0)),
                  pl.BlockSpec((1, d), const)]
                 + [once(w) for w in ws]
                 + [pl.BlockSpec((tm, t.shape[1]), lambda b, i: (i, 0)) for t in tables]
                 + [once(c) for c in consts],
        out_specs=[pl.BlockSpec((1, tm, w), row) for w in widths],
        out_shape=[jax.ShapeDtypeStruct((bx, lx, w), BF16) for w in widths],
        compiler_params=_cp("parallel", "parallel"),
        name="in_proj",
    )(x, mod, g.reshape(1, d), *ws, *tables, *consts)


def _da_lambda(lam_ref, lam_init):
    lv = lam_ref[...]
    return (jnp.exp(jnp.sum(lv[0:1] * lv[1:2], keepdims=True)) - jnp.exp(jnp.sum(lv[2:3] * lv[3:4], keepdims=True))
            + lam_init)


def _dattn_kernel(lam_ref, sg_ref, q1_ref, q2_ref, *rest, lam_init, online, nsrc, tk_max):
    srcs = [rest[3 * s:3 * s + 3] for s in range(nsrc)]
    o_ref, acc1, acc2 = rest[3 * nsrc:]
    dn = (((1,), (1,)), ((), ()))
    q1 = q1_ref[0]
    q2 = q2_ref[0]
    tq = q1.shape[0]
    acc1[...] = jnp.zeros_like(acc1)
    acc2[...] = jnp.zeros_like(acc2)
    carry = (jnp.full((tq, 1), NEG_INF, F32),) * 2 if online else 0

    for k1_ref, k2_ref, v_ref in srcs:
        tk = min(tk_max, k1_ref.shape[1])

        def body(j, c, k1_ref=k1_ref, k2_ref=k2_ref, v_ref=v_ref, tk=tk):
            rows = pl.ds(pl.multiple_of(j * tk, tk), tk)
            vc = v_ref[0, rows, :]
            s1 = lax.dot_general(q1, k1_ref[0, rows, :], dn, preferred_element_type=F32)
            s2 = lax.dot_general(q2, k2_ref[0, rows, :], dn, preferred_element_type=F32)
            if online:
                m1, m2 = c
                n1 = jnp.maximum(m1, jnp.max(s1, axis=-1, keepdims=True))
                n2 = jnp.maximum(m2, jnp.max(s2, axis=-1, keepdims=True))
                acc1[...] = jnp.exp2(m1 - n1) * acc1[...] + jnp.dot(jnp.exp2(s1 - n1).astype(BF16), vc,
                                                                    preferred_element_type=F32)
                acc2[...] = jnp.exp2(m2 - n2) * acc2[...] + jnp.dot(jnp.exp2(s2 - n2).astype(BF16), vc,
                                                                    preferred_element_type=F32)
                return n1, n2
            acc1[...] += jnp.dot(jnp.exp2(s1).astype(BF16), vc, preferred_element_type=F32)
            acc2[...] += jnp.dot(jnp.exp2(s2).astype(BF16), vc, preferred_element_type=F32)
            return c

        carry = lax.fori_loop(0, k1_ref.shape[1] // tk, body, carry)

    dv = DA_VDIM
    a1 = acc1[...]
    a2 = acc2[...]
    lam = _da_lambda(lam_ref, lam_init)
    o = a1 * (1.0 / a1[:, dv:dv + 1]) - a2 * (lam / a2[:, dv:dv + 1])
    o = jnp.where(lax.broadcasted_iota(jnp.int32, o.shape, 1) < dv, o, 0.0)
    o = o * lax.rsqrt(jnp.sum(o * o, axis=-1, keepdims=True) * (1.0 / dv) + EPS)
    o_ref[0] = (o * (sg_ref[...] * (1.0 - lam_init))).astype(BF16)


def _diff_attention(q1, q2, srcs, fixed, lam_p, subln, lam_init):
    b, lq, _ = q1.shape
    h = DA_HEADS
    const = lambda b_, h_, i: (0, 0)
    sg = jnp.pad(subln, (0, LANES - DA_VDIM)).reshape(1, LANES)
    flat = [a for src in srcs for a in src]

    def call(online, *args):
        tq = min(256 if online else 2048, lq)
        qs = pl.BlockSpec((1, tq, LANES), lambda b_, h_, i: (b_, i, h_))
        return pl.pallas_call(
            functools.partial(_dattn_kernel, lam_init=lam_init, online=online, nsrc=len(srcs),
                              tk_max=256 if online else 512),
            grid=(b, h, lq // tq),
            in_specs=[pl.BlockSpec(lam_p.shape, const), pl.BlockSpec((1, LANES), const), qs, qs]
                     + [pl.BlockSpec((1, a.shape[1], LANES), lambda b_, h_, i: (b_, 0, h_)) for a in flat],
            out_specs=qs,
            out_shape=jax.ShapeDtypeStruct((b, lq, h * LANES), BF16),
            scratch_shapes=[pltpu.VMEM((tq, LANES), F32)] * 2,
            compiler_params=_cp("parallel", "parallel", "arbitrary"),
            name="diff_attention_online" if online else "diff_attention",
        )(*args)

    return lax.cond(fixed, functools.partial(call, False), functools.partial(call, True),
                    lam_p, sg, q1, q2, *flat)


def _hy_prep_kernel(zp_ref, zc_ref, zn_ref, w_ref, b_ref, p_ref, x0_ref, ext):
    i = pl.program_id(1)
    last = pl.num_programs(1) - 1
    tl = zc_ref.shape[1]
    h = 2 * SUBLANES
    ext[0:h] = jnp.where(i == 0, 0.0, zp_ref[0].astype(F32))
    ext[h:h + tl] = zc_ref[0].astype(F32)
    ext[h + tl:] = jnp.where(i == last, 0.0, zn_ref[0].astype(F32))
    w = w_ref[...]
    u = (ext[pl.ds(h - 1, tl), :] * w[0:1] + ext[pl.ds(h, tl), :] * w[1:2] + ext[pl.ds(h + 1, tl), :] * w[2:3]
         + b_ref[...])
    hw = HY_WIDTH
    x0_ref[0] = u[:, 0:hw]
    p_ref[0] = u[:, 2 * hw:3 * hw] * u[:, hw:2 * hw]


def _hy_prep(zb, conv_w, conv_b):
    bx, lx, w = zb.shape
    tl = min(256, lx)
    h = 2 * SUBLANES
    nh = lx // h
    per = tl // h
    out = jax.ShapeDtypeStruct((bx, lx, HY_WIDTH), F32)
    return pl.pallas_call(
        _hy_prep_kernel,
        grid=(bx, lx // tl),
        in_specs=[pl.BlockSpec((1, h, w), lambda b, i: (b, jnp.maximum(i * per - 1, 0), 0)),
                  pl.BlockSpec((1, tl, w), lambda b, i: (b, i, 0)),
                  pl.BlockSpec((1, h, w), lambda b, i: (b, jnp.minimum((i + 1) * per, nh - 1), 0)),
                  pl.BlockSpec(conv_w.shape, lambda b, i: (0, 0)),
                  pl.BlockSpec((1, w), lambda b, i: (0, 0))],
        out_specs=[pl.BlockSpec((1, tl, HY_WIDTH), lambda b, i: (b, i, 0))] * 2,
        out_shape=[out, out],
        scratch_shapes=[pltpu.VMEM((tl + 2 * h, w), F32)],
        compiler_params=_cp("parallel", "parallel"),
        name="hyena_prep",
    )(zb, zb, zb, conv_w, conv_b.reshape(1, w))


def _hy_filter_kernel(feat_ref, w1, b1, w2, b2, w3, b3, w4, fr_ref, dl_ref, filt_ref, ssq_ref, *, s):
    i = pl.program_id(0)
    tr = feat_ref.shape[0]
    feat = feat_ref[...]
    fr = fr_ref[...]
    dot = lambda a, w: _dot3(*_hi_lo(a), w[...])
    a = jnp.sin(fr * (dot(feat, w1) + b1[...]))
    a = jnp.sin(fr * (dot(a, w2) + b2[...]))
    a = jnp.sin(fr * (dot(a, w3) + b3[...]))
    coef = dot(a, w4)
    n = i * tr + lax.broadcasted_iota(jnp.int32, (tr, 1), 0)
    window = jnp.exp(-feat[:, 0:1] * dl_ref[...]) + HY_SHIFT
    half = jnp.where(n < s, coef[:, :HY_WIDTH], coef[:, HY_WIDTH:])
    filt = jnp.where(n == s, 0.0, half * window)
    filt_ref[...] = filt

    @pl.when(i == 0)
    def _():
        ssq_ref[...] = jnp.zeros_like(ssq_ref)

    ssq_ref[...] += jnp.sum(filt * filt, axis=0, keepdims=True)


def _hy_feat(s):
    t = jnp.linspace(0.0, 1.0, s, dtype=F32)[:, None]
    w = (2.0 * math.pi / s) * jnp.arange(s, dtype=F32)[:, None]
    bands = jnp.linspace(1e-4, HY_BANDS - 1, HY_BANDS, dtype=F32)[None, :]
    feat = jnp.concatenate([t, jnp.cos(w * bands), jnp.sin(w * bands)], axis=-1)
    feat = jnp.concatenate([feat, feat[:1], feat[:0:-1]], axis=0)
    return jnp.pad(feat, ((0, 0), (0, LANES - feat.shape[1])))


def _hy_filter(s, feat, w1, b1, w2, b2, w3, b3, w4, freq):
    n = 2 * s
    tr = min(512, n)
    deltas = jnp.abs(jnp.linspace(math.log(HY_TARGET) / HY_FAST_DECAY, math.log(HY_TARGET) / HY_SLOW_DECAY,
                                  HY_WIDTH, dtype=F32)).reshape(1, HY_WIDTH)
    w1p = jnp.pad(w1, ((0, LANES - w1.shape[0]), (0, 0)))
    row = lambda a: a.reshape(1, -1)
    args = (feat, w1p, row(b1), w2, row(b2), w3, row(b3), w4, row(freq), deltas)
    const = lambda i: (0, 0)
    return pl.pallas_call(
        functools.partial(_hy_filter_kernel, s=s),
        grid=(n // tr,),
        in_specs=[pl.BlockSpec((tr, LANES), lambda i: (i, 0))] + [pl.BlockSpec(a.shape, const) for a in args[1:]],
        out_specs=[pl.BlockSpec((tr, HY_WIDTH), lambda i: (i, 0)), pl.BlockSpec((1, HY_WIDTH), const)],
        out_shape=[jax.ShapeDtypeStruct((n, HY_WIDTH), F32), jax.ShapeDtypeStruct((1, HY_WIDTH), F32)],
        compiler_params=_cp("arbitrary"),
        name="hyena_filter",
    )(*args)


def _dft_factors(n):
    lg = n.bit_length() - 1
    assert 1 << lg == n
    n1 = 1 << (lg // 2)
    return n1, n // n1


def _dft_tables(n):
    n1, n2 = _dft_factors(n)
    ia = jnp.arange(n1, dtype=jnp.int32)
    ang1 = (2.0 * math.pi / n1) * ((ia[:, None] * ia[None, :]) % n1).astype(F32)
    f1 = jnp.concatenate([jnp.cos(ang1), -jnp.sin(ang1)], axis=0)
    c = jnp.arange(n1, dtype=jnp.int32)[:, None, None]
    d = jnp.arange(n2, dtype=jnp.int32)[None, :, None]
    b = jnp.arange(n2, dtype=jnp.int32)[None, None, :]
    ang = (2.0 * math.pi / n) * ((b * (c + n1 * d)) % n).astype(F32)
    re, im = jnp.cos(ang), -jnp.sin(ang)
    m1 = jnp.concatenate([jnp.concatenate([re, -im], axis=2), jnp.concatenate([im, re], axis=2)], axis=1)
    m2 = jnp.swapaxes(m1, 1, 2)
    ang4 = ang1[: n1 // 2]
    f4 = jnp.concatenate([jnp.cos(ang4), -jnp.sin(ang4)], axis=1) * (1.0 / n)
    return f1, _hi_lo(m1), _hi_lo(m2), _hi_lo(_per_offset(f4))


def _dot3(mh, ml, a):
    ah, al = _hi_lo(a)
    return (jnp.dot(mh, ah, preferred_element_type=F32) + jnp.dot(mh, al, preferred_element_type=F32)
            + jnp.dot(ml, ah, preferred_element_type=F32))


HY_TB = SUBLANES


def _per_offset(f):
    return jnp.kron(f, jnp.eye(HY_TB, dtype=f.dtype))


def _hy_stage1_kernel(fh_ref, fl_ref, x_ref, re_ref, im_ref):
    _, n1, tb, w = re_ref.shape
    y = _dot3(fh_ref[...], fl_ref[...], x_ref[0].reshape(-1, w))
    re_ref[0] = y[:n1 * tb].reshape(n1, tb, w)
    im_ref[0] = y[n1 * tb:].reshape(n1, tb, w)


def _hy_stage1(f, x4):
    bx, k, n2, w = x4.shape
    tb = HY_TB
    n1 = f[0].shape[0] // (2 * tb)
    out = jax.ShapeDtypeStruct((bx, n1, n2, w), F32)
    return pl.pallas_call(
        _hy_stage1_kernel,
        grid=(bx, n2 // tb),
        in_specs=[pl.BlockSpec(f[0].shape, lambda b, j: (0, 0))] * 2
                 + [pl.BlockSpec((1, k, tb, w), lambda b, j: (b, 0, j, 0))],
        out_specs=[pl.BlockSpec((1, n1, tb, w), lambda b, j: (b, 0, j, 0))] * 2,
        out_shape=[out, out],
        compiler_params=_cp("parallel", "parallel"),
        name="hyena_dft_stage1",
    )(*f, x4)


def _hy_mid_kernel(m1h_ref, m1l_ref, m2h_ref, m2l_ref, re_ref, im_ref, hre_ref, him_ref, ore_ref, oim_ref):
    n2 = re_ref.shape[2]
    for c in range(re_ref.shape[1]):
        a = jnp.concatenate([re_ref[0, c], im_ref[0, c]], axis=0)
        x = _dot3(m1h_ref[c], m1l_ref[c], a)
        xre, xim = x[:n2], x[n2:]
        hre, him = hre_ref[c], him_ref[c]
        y = jnp.concatenate([xre * hre - xim * him, xre * him + xim * hre], axis=0)
        bb = _dot3(m2h_ref[c], m2l_ref[c], y)
        ore_ref[0, c] = bb[:n2]
        oim_ref[0, c] = bb[n2:]


def _hy_spec_kernel(m1h_ref, m1l_ref, re_ref, im_ref, rs_ref, ore_ref, oim_ref):
    n2 = re_ref.shape[2]
    for c in range(re_ref.shape[1]):
        a = jnp.concatenate([re_ref[0, c], im_ref[0, c]], axis=0)
        x = _dot3(m1h_ref[c], m1l_ref[c], a) * rs_ref[...]
        ore_ref[c] = x[:n2]
        oim_ref[c] = x[n2:]


def _hy_filter_spectrum(m1, are, aim, rs):
    _, n1, n2, w = are.shape
    cb = min(4, n1)
    blk = pl.BlockSpec((1, cb, n2, w), lambda c: (0, c, 0, 0))
    mblk = pl.BlockSpec((cb, 2 * n2, 2 * n2), lambda c: (c, 0, 0))
    oblk = pl.BlockSpec((cb, n2, w), lambda c: (c, 0, 0))
    out = jax.ShapeDtypeStruct((n1, n2, w), F32)
    return pl.pallas_call(
        _hy_spec_kernel,
        grid=(n1 // cb,),
        in_specs=[mblk, mblk, blk, blk, pl.BlockSpec((1, w), lambda c: (0, 0))],
        out_specs=[oblk, oblk],
        out_shape=[out, out],
        compiler_params=_cp("parallel"),
        name="hyena_filter_spectrum",
    )(*m1, are, aim, rs)


def _hy_mid(m1, m2, are, aim, hre, him):
    bx, n1, n2, w = are.shape
    cb = min(4, n1)
    blk = pl.BlockSpec((1, cb, n2, w), lambda c, b: (b, c, 0, 0))
    mblk = pl.BlockSpec((cb, 2 * n2, 2 * n2), lambda c, b: (c, 0, 0))
    hblk = pl.BlockSpec((cb, n2, w), lambda c, b: (c, 0, 0))
    out = jax.ShapeDtypeStruct((bx, n1, n2, w), F32)
    return pl.pallas_call(
        _hy_mid_kernel,
        grid=(n1 // cb, bx),
        in_specs=[mblk, mblk, mblk, mblk, blk, blk, hblk, hblk],
        out_specs=[blk, blk],
        out_shape=[out, out],
        compiler_params=_cp("parallel", "parallel"),
        name="hyena_dft_mid",
    )(*m1, *m2, are, aim, hre, him)


def _hy_last_kernel(fh_ref, fl_ref, re_ref, im_ref, p_ref, bias_ref, o_ref):
    _, k, tb, w = p_ref.shape
    spec = jnp.concatenate([re_ref[0].reshape(-1, w), im_ref[0].reshape(-1, w)], axis=0)
    y = _dot3(fh_ref[...], fl_ref[...], spec)
    o_ref[0] = y.reshape(k, tb, w) + p_ref[0] * bias_ref[...]


def _hy_last(f4, bre, bim, p4, bias):
    bx, n1, n2, w = bre.shape
    k = n1 // 2
    tb = HY_TB
    big = pl.BlockSpec((1, n1, tb, w), lambda b, j: (b, 0, j, 0))
    small = pl.BlockSpec((1, k, tb, w), lambda b, j: (b, 0, j, 0))
    return pl.pallas_call(
        _hy_last_kernel,
        grid=(bx, n2 // tb),
        in_specs=[pl.BlockSpec(f4[0].shape, lambda b, j: (0, 0))] * 2 + [big, big, small,
                  pl.BlockSpec((1, w), lambda b, j: (0, 0))],
        out_specs=small,
        out_shape=jax.ShapeDtypeStruct((bx, k, n2, w), F32),
        compiler_params=_cp("parallel", "parallel"),
        name="hyena_dft_last",
    )(*f4, bre, bim, p4, bias)


def _hyena(zb, conv_w, conv_b, filt_params, hy_bias, consts):
    bx, lx, _ = zb.shape
    feat, (f1, m1, m2, f4) = consts
    n = 2 * lx
    n1, n2 = _dft_factors(n)
    w = HY_WIDTH
    filt, ssq = _hy_filter(lx, feat, *filt_params)
    rs = lax.rsqrt(ssq + EPS)
    fre, fim = _hy_stage1(_hi_lo(_per_offset(f1)), filt.reshape(1, n1, n2, w))
    hre, him = _hy_filter_spectrum(m1, fre, fim, rs)
    p, x0 = _hy_prep(zb, conv_w, conv_b)
    k = n1 // 2
    p4 = p.reshape(bx, k, n2, w)
    are, aim = _hy_stage1(_hi_lo(_per_offset(f1[:, :k])), p4)
    bre, bim = _hy_mid(m1, m2, are, aim, hre, him)
    t = _hy_last(f4, bre, bim, p4, hy_bias.reshape(1, w))
    return x0, t.reshape(bx, lx, w)


WA_SHIFT_MAX = 35.0


def _wattn_kernel(sk_ref, q_ref, bias_ref, *rest, banded, fixed):
    if banded:
        kp_ref, kc_ref, kn_ref, kx_ref, vp_ref, vc_ref, vn_ref, vx_ref, o_ref = rest
    else:
        kx_ref, vx_ref, o_ref = rest
    h = pl.program_id(1)
    i = pl.program_id(2)
    last = pl.num_programs(2) - 1
    qb = q_ref.shape[1]
    q = jnp.concatenate([q_ref[0, :, :LANES], q_ref[0, :, LANES:]], axis=0)
    if banded:
        kk = jnp.concatenate([kp_ref[0], kc_ref[0], kn_ref[0], kx_ref[0]], axis=0)
        vv = jnp.concatenate([vp_ref[0], vc_ref[0], vn_ref[0], vx_ref[0]], axis=0)
    else:
        kk, vv = kx_ref[0], vx_ref[0]
    s = lax.dot_general(q, kk, (((1,), (1,)), ((), ())), preferred_element_type=F32) + bias_ref[0]
    if banded:
        c = lax.broadcasted_iota(jnp.int32, (1, kk.shape[0]), 1) - BLOCK
        outside = ((c < 0) & (i == 0)) | ((c >= qb) & (c < qb + BLOCK) & (i == last))
        s = s + jnp.where(outside, NEG_INF, 0.0)
    top = lax.broadcasted_iota(jnp.int32, (2 * qb, 1), 0) < qb
    sk = jnp.where(top, sk_ref[h * WA_GROUP], sk_ref[h * WA_GROUP + 1])
    if fixed:
        p, sink_term = jnp.exp2(s), sk
    else:
        m = jnp.maximum(jnp.max(s, axis=-1, keepdims=True), sk)
        p, sink_term = jnp.exp2(s - m), jnp.exp2(sk - m)
    acc = jnp.dot(p.astype(BF16), vv, preferred_element_type=F32)
    o = (acc * (1.0 / (acc[:, WA_DIM:WA_DIM + 1] + sink_term))).astype(BF16)
    o_ref[0] = jnp.concatenate([o[:qb], o[qb:]], axis=1)


def _window_attention(q, k, v, kx, vx, sink, bound, banded):
    b, lq, _ = q.shape
    cx = kx.shape[1]
    qb = min(256, lq)
    per = qb // BLOCK
    nblk = lq // BLOCK
    side = lambda f: pl.BlockSpec((1, BLOCK, LANES), f)
    prev = side(lambda b_, h, i: (b_, jnp.maximum(i * per - 1, 0), h))
    nxt = side(lambda b_, h, i: (b_, jnp.minimum((i + 1) * per, nblk - 1), h))
    cur = pl.BlockSpec((1, qb, LANES), lambda b_, h, i: (b_, i, h))
    ctx = pl.BlockSpec((1, cx, LANES), lambda b_, h, i: (b_, 0, h))
    qspec = pl.BlockSpec((1, qb, WA_GROUP * LANES), lambda b_, h, i: (b_, i, h))
    mask = jnp.zeros((WA_GROUP * qb, cx), F32)
    if banded:
        r = jnp.arange(WA_GROUP * qb)[:, None] % qb
        c = jnp.arange(qb + 2 * BLOCK)[None, :] - BLOCK
        mask = jnp.concatenate([jnp.where(jnp.abs(r - c) <= BLOCK, 0.0, NEG_INF).astype(F32), mask], axis=1)
        specs, args = [prev, cur, nxt, ctx, prev, cur, nxt, ctx], (k, k, k, kx, v, v, v, vx)
    else:
        specs, args = [ctx, ctx], (kx, vx)
    fixed = bound <= WA_SHIFT_MAX
    shift = jnp.maximum(bound, sink)
    rows = jnp.repeat(shift.reshape(WA_KV_HEADS, WA_GROUP), qb, axis=1)[:, :, None]
    bias = mask[None] - jnp.where(fixed, LOG2E * rows, 0.0)
    sk = jnp.where(fixed, jnp.exp2(LOG2E * (sink - shift)), LOG2E * sink)

    def call(fixed_, *ops):
        return pl.pallas_call(
            functools.partial(_wattn_kernel, banded=banded, fixed=fixed_),
            grid=(b, WA_KV_HEADS, lq // qb),
            in_specs=[pl.BlockSpec(memory_space=pltpu.SMEM), qspec,
                      pl.BlockSpec((1,) + bias.shape[1:], lambda b_, h, i: (h, 0, 0))] + specs,
            out_specs=qspec,
            out_shape=jax.ShapeDtypeStruct((b, lq, WA_HEADS * LANES), BF16),
            compiler_params=_cp("parallel", "parallel", "arbitrary"),
            name="window_attention" if fixed_ else "window_attention_online",
        )(*ops)

    return lax.cond(fixed, functools.partial(call, True), functools.partial(call, False), sk, q, bias, *args)


def _conf_kernel(zp_ref, zc_ref, zn_ref, w_ref, b_ref, lg_ref, lb_ref, o_ref, ext, sh, *, halo):
    i = pl.program_id(1)
    last = pl.num_programs(1) - 1
    tl = zc_ref.shape[1]
    cw = CF_WIDTH

    def glu(z_ref):
        z = z_ref[0].astype(F32)
        return z[:, :cw] * jax.nn.sigmoid(z[:, cw:])

    ext[0:halo] = jnp.where(i == 0, 0.0, glu(zp_ref))
    ext[halo:halo + tl] = glu(zc_ref)
    ext[halo + tl:] = jnp.where(i == last, 0.0, glu(zn_ref))
    for r in range(1, SUBLANES):
        sh[r - 1] = ext[pl.ds(r, sh.shape[1]), :]
    w = w_ref[...]
    u = jnp.zeros((tl, cw), F32) + b_ref[...]
    for j in range(CF_TAPS):
        off = halo - CF_TAPS // 2 + j
        base, r = off // SUBLANES * SUBLANES, off % SUBLANES
        tap = ext[pl.ds(base, tl), :] if r == 0 else sh[r - 1, pl.ds(base, tl), :]
        u = u + tap * w[j:j + 1]
    uc = u - jnp.mean(u, axis=-1, keepdims=True)
    y = uc * lax.rsqrt(jnp.mean(uc * uc, axis=-1, keepdims=True) + EPS) * lg_ref[...] + lb_ref[...]
    o_ref[0] = _silu(y).astype(BF16)


def _conformer(zd, dw_w, dw_b, ln_g, ln_b):
    bx, lx, w = zd.shape
    tl = min(256, lx)
    halo = 2 * SUBLANES
    nh = lx // halo
    per = tl // halo
    row = lambda a: a.reshape(1, -1)
    const = lambda b, i: (0, 0)
    return pl.pallas_call(
        functools.partial(_conf_kernel, halo=halo),
        grid=(bx, lx // tl),
        in_specs=[pl.BlockSpec((1, halo, w), lambda b, i: (b, jnp.maximum(i * per - 1, 0), 0)),
                  pl.BlockSpec((1, tl, w), lambda b, i: (b, i, 0)),
                  pl.BlockSpec((1, halo, w), lambda b, i: (b, jnp.minimum((i + 1) * per, nh - 1), 0)),
                  pl.BlockSpec(dw_w.shape, const)] + [pl.BlockSpec((1, CF_WIDTH), const)] * 3,
        out_specs=pl.BlockSpec((1, tl, CF_WIDTH), lambda b, i: (b, i, 0)),
        out_shape=jax.ShapeDtypeStruct((bx, lx, CF_WIDTH), BF16),
        scratch_shapes=[pltpu.VMEM((tl + 2 * halo, CF_WIDTH), F32),
                        pltpu.VMEM((SUBLANES - 1, tl + 2 * halo - SUBLANES, CF_WIDTH), F32)],
        compiler_params=_cp("parallel", "parallel"),
        name="conformer_conv",
    )(zd, zd, zd, dw_w, row(dw_b), row(ln_g), row(ln_b))


def _merge_kernel(x_ref, m_ref, ya, yb0, ybt, yc, yd, zg_ref, bg_ref, wa, wb, wc, wd, wo_ref, o_ref, *, goff):
    d = x_ref.shape[-1]
    acc = jnp.zeros(x_ref.shape[1:], F32)
    ys = (ya[0], (yb0[0] * ybt[0]).astype(BF16), yc[0], yd[0])
    for i, (y, w) in enumerate(zip(ys, (wa, wb, wc, wd))):
        gate = jax.nn.sigmoid(zg_ref[0, :, i * d:(i + 1) * d].astype(F32) + bg_ref[:, i * d:(i + 1) * d])
        acc = acc + gate * jnp.dot(y, w[...], preferred_element_type=F32)
    out = jnp.dot(acc.astype(BF16), wo_ref[...], preferred_element_type=F32)
    o_ref[0] = x_ref[0] + m_ref[0, :, goff:goff + d] * out


def _merge(x, mod, ys, zg, b_gate, wbs, w_out):
    bx, lx, d = x.shape
    tm = min(512, lx)
    row = lambda b, i: (b, i, 0)
    const = lambda b, i: (0, 0)
    once = lambda a: pl.BlockSpec(a.shape, const, pipeline_mode=pl.Buffered(1))
    return pl.pallas_call(
        functools.partial(_merge_kernel, goff=2 * d),
        grid=(bx, lx // tm),
        in_specs=[pl.BlockSpec((1, tm, d), row), pl.BlockSpec((1, 1, mod.shape[-1]), lambda b, i: (b, 0, 0))]
                 + [pl.BlockSpec((1, tm, y.shape[-1]), row) for y in ys]
                 + [pl.BlockSpec((1, tm, N_BRANCH * d), row), pl.BlockSpec((1, N_BRANCH * d), const)]
                 + [once(w) for w in wbs] + [once(w_out)],
        out_specs=pl.BlockSpec((1, tm, d), row),
        out_shape=jax.ShapeDtypeStruct(x.shape, F32),
        input_output_aliases={0: 0},
        compiler_params=_cp("parallel", "parallel"),
        name="merge",
    )(x, mod, *ys, zg, b_gate.reshape(1, -1), *wbs, w_out)


def _slot_rows(w, group):
    n = w.shape[0] // group
    return jnp.pad(w.reshape(n, group, -1), ((0, 0), (0, LANES - group), (0, 0))).reshape(n * LANES, -1)


def _moe_kernel(x_ref, m_ref, g_ref, wr_ref, br_ref, w1_ref, w3_ref, w2_ref, o_ref, h_sc, gate_sc, acc_sc):
    e = pl.program_id(2)
    d = x_ref.shape[-1]
    tm = x_ref.shape[1]
    lane = lax.broadcasted_iota(jnp.int32, (tm, LANES), 1).astype(F32)

    @pl.when(e == 0)
    def _():
        h = _rms(x_ref[0]) * g_ref[...] * (1.0 + m_ref[0, :, 4 * d:5 * d]) + m_ref[0, :, 3 * d:4 * d]
        h_sc[...] = h.astype(BF16)
        lg = _dot3(*_hi_lo(h), wr_ref[...]) + br_ref[...]
        isg = lane < N_GROUPS
        gmax = jnp.max(jnp.where(isg, lg, NEG_INF), axis=-1, keepdims=True)
        gi = jnp.min(jnp.where(isg & (lg == gmax), lane, LANES), axis=-1, keepdims=True)
        gw = 1.0 / jnp.sum(jnp.where(isg, jnp.exp(lg - gmax), 0.0), axis=-1, keepdims=True)
        lo = N_GROUPS + gi * EXP_PER_GROUP
        ise = (lane >= lo) & (lane < lo + EXP_PER_GROUP)
        le = jnp.where(ise, lg, NEG_INF)
        m1 = jnp.max(le, axis=-1, keepdims=True)
        i1 = jnp.min(jnp.where(ise & (le == m1), lane, LANES), axis=-1, keepdims=True)
        ise2 = ise & (lane != i1)
        le2 = jnp.where(ise2, lg, NEG_INF)
        m2 = jnp.max(le2, axis=-1, keepdims=True)
        i2 = jnp.min(jnp.where(ise2 & (le2 == m2), lane, LANES), axis=-1, keepdims=True)
        r = jnp.exp(m2 - m1)
        wa = gw / (1.0 + r)
        gate_sc[...] = jnp.where(lane == i1, wa, 0.0) + jnp.where(lane == i2, wa * r, 0.0)
        acc_sc[...] = jnp.zeros_like(acc_sc)

    hb = h_sc[...]
    u = (_silu(jnp.dot(hb, w1_ref[0].astype(BF16), preferred_element_type=F32))
         * jnp.dot(hb, w3_ref[0].astype(BF16), preferred_element_type=F32))
    ge = jnp.sum(jnp.where(lane == (e + N_GROUPS).astype(F32), gate_sc[...], 0.0), axis=-1, keepdims=True)
    acc_sc[...] += ge * jnp.dot(u.astype(BF16), w2_ref[0].astype(BF16), preferred_element_type=F32)

    @pl.when(e == pl.num_programs(2) - 1)
    def _():
        o_ref[0] = x_ref[0] + m_ref[0, :, 5 * d:6 * d] * acc_sc[...]


def _moe(x, mod, g, w_router, b_router, w1, w3, w2):
    bx, lx, d = x.shape
    tm = min(1024, lx)
    ne, _, f = w1.shape
    row = lambda b, i, e: (b, i, 0)
    const = lambda b, i, e: (0, 0)
    return pl.pallas_call(
        _moe_kernel,
        grid=(bx, lx // tm, ne),
        in_specs=[pl.BlockSpec((1, tm, d), row), pl.BlockSpec((1, 1, mod.shape[-1]), lambda b, i, e: (b, 0, 0)),
                  pl.BlockSpec((1, d), const), pl.BlockSpec((d, LANES), const), pl.BlockSpec((1, LANES), const),
                  pl.BlockSpec((1, d, f), lambda b, i, e: (e, 0, 0)), pl.BlockSpec((1, d, f), lambda b, i, e: (e, 0, 0)),
                  pl.BlockSpec((1, f, d), lambda b, i, e: (e, 0, 0))],
        out_specs=pl.BlockSpec((1, tm, d), row),
        out_shape=jax.ShapeDtypeStruct(x.shape, F32),
        scratch_shapes=[pltpu.VMEM((tm, d), BF16), pltpu.VMEM((tm, LANES), F32), pltpu.VMEM((tm, d), F32)],
        input_output_aliases={0: 0},
        compiler_params=_cp("parallel", "parallel", "arbitrary"),
        name="moe",
    )(x, mod, g.reshape(1, d), w_router, b_router, w1, w3, w2)


SC_CORES = 2
SC_SUBCORES = 16
SC_CHUNK = 64
MOE_ROWS = 1024


def _sc_gather(table, idx):
    n = idx.shape[0]
    w = table.shape[1]
    per = n // (SC_CORES * SC_SUBCORES)
    assert per * SC_CORES * SC_SUBCORES == n and per % SC_CHUNK == 0
    mesh = plsc.VectorSubcoreMesh(core_axis_name="c", subcore_axis_name="s")

    @functools.partial(
        pl.kernel, mesh=mesh, out_type=jax.ShapeDtypeStruct((n, w), table.dtype),
        scratch_types=[pltpu.VMEM((SC_CHUNK,), jnp.int32), pltpu.VMEM((SC_CHUNK, w), table.dtype),
                       pltpu.SemaphoreType.DMA],
        name="sc_row_gather")
    def gather(table_hbm, idx_hbm, out_hbm, idx_v, rows_v, sem):
        base = (lax.axis_index("s") * SC_CORES + lax.axis_index("c")) * per

        @pl.loop(0, per // SC_CHUNK)
        def _(j):
            off = pl.multiple_of(base + j * SC_CHUNK, SC_CHUNK)
            pltpu.sync_copy(idx_hbm.at[pl.ds(off, SC_CHUNK)], idx_v)
            pltpu.async_copy(table_hbm.at[idx_v], rows_v, sem).wait()
            pltpu.sync_copy(rows_v, out_hbm.at[pl.ds(off, SC_CHUNK)])

    return gather(table, idx)


def _sc_scatter(rows, idx, n_out):
    n, w = rows.shape
    per = n // (SC_CORES * SC_SUBCORES)
    assert per * SC_CORES * SC_SUBCORES == n and per % SC_CHUNK == 0
    mesh = plsc.VectorSubcoreMesh(core_axis_name="c", subcore_axis_name="s")

    @functools.partial(
        pl.kernel, mesh=mesh, out_type=jax.ShapeDtypeStruct((n_out, w), rows.dtype),
        scratch_types=[pltpu.VMEM((SC_CHUNK,), jnp.int32), pltpu.VMEM((SC_CHUNK, w), rows.dtype),
                       pltpu.SemaphoreType.DMA],
        name="sc_row_scatter")
    def scatter(rows_hbm, idx_hbm, out_hbm, idx_v, rows_v, sem):
        base = (lax.axis_index("s") * SC_CORES + lax.axis_index("c")) * per

        @pl.loop(0, per // SC_CHUNK)
        def _(j):
            off = pl.multiple_of(base + j * SC_CHUNK, SC_CHUNK)
            pltpu.sync_copy(idx_hbm.at[pl.ds(off, SC_CHUNK)], idx_v)
            pltpu.sync_copy(rows_hbm.at[pl.ds(off, SC_CHUNK)], rows_v)
            pltpu.async_copy(rows_v, out_hbm.at[idx_v], sem).wait()

    return scatter(rows, idx)


def _route_kernel(x_ref, m_ref, g_ref, wr_ref, br_ref, rows_ref, gi_ref):
    d = x_ref.shape[-1]
    tm = x_ref.shape[1]
    lane = lax.broadcasted_iota(jnp.int32, (tm, LANES), 1).astype(F32)
    h = _rms(x_ref[0]) * g_ref[...] * (1.0 + m_ref[0, :, 4 * d:5 * d]) + m_ref[0, :, 3 * d:4 * d]
    rows_ref[0, :, :d] = h
    lg = _dot3(*_hi_lo(h), wr_ref[...]) + br_ref[...]
    isg = lane < N_GROUPS
    gmax = jnp.max(jnp.where(isg, lg, NEG_INF), axis=-1, keepdims=True)
    gi = jnp.min(jnp.where(isg & (lg == gmax), lane, LANES), axis=-1, keepdims=True)
    gw = 1.0 / jnp.sum(jnp.where(isg, jnp.exp(lg - gmax), 0.0), axis=-1, keepdims=True)
    lo = N_GROUPS + gi * EXP_PER_GROUP
    ise = (lane >= lo) & (lane < lo + EXP_PER_GROUP)
    le = jnp.where(ise, lg, NEG_INF)
    m1 = jnp.max(le, axis=-1, keepdims=True)
    i1 = jnp.min(jnp.where(ise & (le == m1), lane, LANES), axis=-1, keepdims=True)
    ise2 = ise & (lane != i1)
    le2 = jnp.where(ise2, lg, NEG_INF)
    m2 = jnp.max(le2, axis=-1, keepdims=True)
    i2 = jnp.min(jnp.where(ise2 & (le2 == m2), lane, LANES), axis=-1, keepdims=True)
    r = jnp.exp(m2 - m1)
    wa = gw / (1.0 + r)
    rows_ref[0, :, d:] = jnp.where(lane == i1 - lo, wa, 0.0) + jnp.where(lane == i2 - lo, wa * r, 0.0)
    gi_ref[0] = gi.astype(jnp.int32)


def _route(x, mod, g, w_router, b_router, b0, nb):
    _, lx, d = x.shape
    tm = min(512, lx)
    row = lambda b, i: (b, i, 0)
    const = lambda b, i: (0, 0)
    return pl.pallas_call(
        _route_kernel,
        grid=(nb, lx // tm),
        in_specs=[pl.BlockSpec((1, tm, d), lambda b, i: (b0 + b, i, 0)),
                  pl.BlockSpec((1, 1, mod.shape[-1]), lambda b, i: (b0 + b, 0, 0)),
                  pl.BlockSpec((1, d), const), pl.BlockSpec((d, LANES), const), pl.BlockSpec((1, LANES), const)],
        out_specs=[pl.BlockSpec((1, tm, d + LANES), row), pl.BlockSpec((1, tm, 1), row)],
        out_shape=[jax.ShapeDtypeStruct((nb, lx, d + LANES), F32), jax.ShapeDtypeStruct((nb, lx, 1), jnp.int32)],
        compiler_params=_cp("parallel", "parallel"),
        name="moe_route",
    )(x, mod, g.reshape(1, d), w_router, b_router)


def _gmoe_kernel(grp_ref, nv_ref, xs_ref, w1_ref, w3_ref, w2_ref, o_ref, h_sc, acc_sc):
    i = pl.program_id(0)
    e = pl.program_id(1)
    tm, d = h_sc.shape
    valid = lax.broadcasted_iota(jnp.int32, (tm, 1), 0) < nv_ref[i]

    @pl.when(nv_ref[i] > 0)
    def _():
        @pl.when(e == 0)
        def _():
            h_sc[...] = jnp.where(valid, xs_ref[:, :d], 0.0).astype(BF16)
            acc_sc[...] = jnp.zeros_like(acc_sc)

        hb = h_sc[...]
        u = (_silu(jnp.dot(hb, w1_ref[0].astype(BF16), preferred_element_type=F32))
             * jnp.dot(hb, w3_ref[0].astype(BF16), preferred_element_type=F32))
        lane = lax.broadcasted_iota(jnp.int32, (tm, LANES), 1)
        ge = jnp.sum(jnp.where(valid & (lane == e), xs_ref[:, d:], 0.0), axis=-1, keepdims=True)
        acc_sc[...] += ge * jnp.dot(u.astype(BF16), w2_ref[0].astype(BF16), preferred_element_type=F32)

    @pl.when(e == pl.num_programs(1) - 1)
    def _():
        o_ref[...] = jnp.where(nv_ref[i] > 0, acc_sc[...], 0.0)


def _grouped_moe(grp, nv, xs, w1, w3, w2):
    p, dw = xs.shape
    d = dw - LANES
    _, _, f = w1.shape
    wmap = lambda i, e, grp, nv: (grp[i] * EXP_PER_GROUP + e, 0, 0)
    rows = lambda i, e, grp, nv: (i, 0)
    return pl.pallas_call(
        _gmoe_kernel,
        grid_spec=pltpu.PrefetchScalarGridSpec(
            num_scalar_prefetch=2,
            grid=(p // MOE_ROWS, EXP_PER_GROUP),
            in_specs=[pl.BlockSpec((MOE_ROWS, dw), rows),
                      pl.BlockSpec((1, d, f), wmap), pl.BlockSpec((1, d, f), wmap), pl.BlockSpec((1, f, d), wmap)],
            out_specs=pl.BlockSpec((MOE_ROWS, d), rows),
            scratch_shapes=[pltpu.VMEM((MOE_ROWS, d), BF16), pltpu.VMEM((MOE_ROWS, d), F32)]),
        out_shape=jax.ShapeDtypeStruct((p, d), F32),
        compiler_params=_cp("arbitrary", "arbitrary"),
        name="moe_experts",
    )(grp, nv, xs, w1, w3, w2)


def _residual_kernel(x_ref, m_ref, y_ref, o_ref):
    d = x_ref.shape[-1]
    o_ref[0] = x_ref[0] + m_ref[0, :, 5 * d:6 * d] * y_ref[0]


def _residual(x, mod, y, b0):
    _, lx, d = x.shape
    tm = min(1024, lx)
    row = lambda b, i: (b0 + b, i, 0)
    return pl.pallas_call(
        _residual_kernel,
        grid=(y.shape[0], lx // tm),
        in_specs=[pl.BlockSpec((1, tm, d), row), pl.BlockSpec((1, 1, mod.shape[-1]), lambda b, i: (b0 + b, 0, 0)),
                  pl.BlockSpec((1, tm, d), lambda b, i: (b, i, 0))],
        out_specs=pl.BlockSpec((1, tm, d), row),
        out_shape=jax.ShapeDtypeStruct(x.shape, F32),
        input_output_aliases={0: 0},
        compiler_params=_cp("parallel", "parallel"),
        name="moe_residual",
    )(x, mod, y)


def _moe_sorted(x, mod, g, w_router, b_router, w1, w3, w2):
    bx, lx, d = x.shape
    unit = SC_CORES * SC_SUBCORES * SC_CHUNK
    parts = 2 if bx % 2 == 0 and (bx // 2 * lx) % unit == 0 else 1
    nb = bx // parts
    t = nb * lx
    p = t + N_GROUPS * MOE_ROWS
    routed = [_route(x, mod, g, w_router, b_router, i * nb, nb) for i in range(parts)]
    outs = []
    for rows, gi in routed:
        gi = gi.reshape(t)
        onehot = (gi[:, None] == jnp.arange(N_GROUPS, dtype=jnp.int32)[None, :]).astype(jnp.int32)
        csum = jnp.cumsum(onehot, axis=0)
        counts = csum[-1]
        rank = jnp.take_along_axis(csum, gi[:, None], axis=1)[:, 0] - 1
        padded = (counts + MOE_ROWS - 1) // MOE_ROWS * MOE_ROWS
        pend = jnp.cumsum(padded)
        pstart = pend - padded
        pos = (pstart[gi] + rank).astype(jnp.int32)
        bstart = jnp.arange(p // MOE_ROWS, dtype=jnp.int32) * MOE_ROWS
        grp = jnp.minimum(jnp.searchsorted(pend, bstart, side="right"), N_GROUPS - 1).astype(jnp.int32)
        nv = jnp.clip(pstart[grp] + counts[grp] - bstart, 0, MOE_ROWS).astype(jnp.int32)
        xs = _sc_scatter(rows.reshape(t, d + LANES), pos, p)
        ys = _grouped_moe(grp, nv, xs, w1, w3, w2)
        outs.append(_sc_gather(ys, pos).reshape(nb, lx, d))
    for i, yt in enumerate(outs):
        x = _residual(x, mod, yt, i * nb)
    return x


def kernel(x, c, ctx, c_ctx, w_mod, b_mod, norm1_g, norm2_g, w_in, b_gate, da_qn, da_kn, da_lam, da_subln, hy_conv_w, hy_conv_b, hf_w1, hf_b1, hf_w2, hf_b2, hf_w3, hf_b3, hf_w4, hf_freq, hy_bias, wa_qn, wa_kn, wa_sink, cf_dw_w, cf_dw_b, cf_ln_g, cf_ln_b, w_branch, w_out, w_rg, b_rg, w_re, b_re, w1, w3, w2):
    b, s, d = x.shape
    cl = ctx.shape[1]
    depth = w_mod.shape[0]
    assert s % 256 == 0 and cl % 256 == 0 and s % GRID_W == 0

    nrow = -(-(b + 1) // SUBLANES) * SUBLANES
    crows = jnp.zeros((nrow, d), F32).at[:b].set(c).at[b].set(c_ctx)
    mods = _mod_vectors(crows, w_mod, b_mod)

    aw, qw, kw = DA_HEADS * DA_DIM, WA_HEADS * WA_DIM, WA_KV_HEADS * WA_DIM
    tab_lat = (*_rope_tables(s, DA_DIM, DA_HEADS), *_rope_tables(s, WA_DIM, WA_HEADS), *_rope_tables(s, WA_DIM, WA_KV_HEADS))
    tab_ctx = (*_unit_tables(cl, aw), *_unit_tables(cl, qw), *_unit_tables(cl, kw))
    hy_lat = (_hy_feat(s), _dft_tables(2 * s))
    hy_ctx = (_hy_feat(cl), _dft_tables(2 * cl))
    gms = (_group_ones(aw, DA_DIM), _group_ones(qw, WA_DIM), _group_ones(kw, WA_DIM))
    sels = (_slot_select(aw, DA_DIM), _slot_select(DA_HEADS * DA_VDIM, DA_VDIM), _slot_select(qw, WA_DIM),
            _slot_select(kw, WA_DIM))
    qvec = _slot_vector(DA_HEADS, DA_DIM, DA_DIM + 1, 1.0)
    vvec = _slot_vector(DA_HEADS, DA_VDIM, DA_VDIM + DA_ONES, 1.0)
    vvecc = _slot_vector(WA_KV_HEADS, WA_DIM, WA_DIM + DA_ONES, 1.0)

    xc = ctx
    for l in range(depth):
        last = l == depth - 1
        lam_init = 0.8 - 0.6 * math.exp(-0.3 * l)
        mod_x = mods[l, :b][:, None, :]
        mod_c = jnp.broadcast_to(mods[l, b][None, None, :], (b, 1, mods.shape[-1]))
        wl = w_in[l]
        ws = [wl[:, 0:OFF_B].astype(BF16), wl[:, OFF_B:OFF_C].astype(BF16), wl[:, OFF_C:OFF_D].astype(BF16),
              wl[:, OFF_D:OFF_G].astype(BF16), wl[:, OFF_G:].astype(BF16)]
        shift = 1.02 * LOG2E * DA_DIM ** 0.5 * jnp.max(jnp.abs(da_qn[l])) * jnp.max(jnp.abs(da_kn[l]))
        fixed = shift <= DA_SHIFT_MAX
        kvec = _slot_vector(DA_HEADS, DA_DIM, DA_DIM + 1, jnp.where(fixed, -shift, 0.0))
        tile = lambda a, n: jnp.tile(a, n).reshape(1, -1)
        consts = (*gms, tile(da_qn[l], DA_HEADS), tile(da_kn[l], DA_HEADS), tile(wa_qn[l], WA_HEADS),
                  tile(wa_kn[l], WA_KV_HEADS), *sels, qvec, kvec, vvec, vvecc)
        q1, q2, k1, k2, v, zb, qc, kc, vc, zd, zg = _in_proj(x, mod_x, norm1_g[l], ws, tab_lat, consts)
        q1x, q2x, k1x, k2x, vx, zbx, qcx, kcx, vcx, zdx, zgx = _in_proj(xc, mod_c, norm1_g[l], ws, tab_ctx, consts)
        filt_params = (hf_w1[l], hf_b1[l], hf_w2[l], hf_b2[l], hf_w3[l], hf_b3[l], hf_w4[l], hf_freq[l])
        wb4 = w_branch[l].astype(BF16)
        wbs = (_slot_rows(wb4[0], DA_VDIM), wb4[1], _slot_rows(wb4[2], WA_DIM), wb4[3])
        wo = w_out[l].astype(BF16)

        ya = _diff_attention(q1, q2, [(k1, k2, v), (k1x, k2x, vx)], fixed, da_lam[l], da_subln[l], lam_init)
        yb = _hyena(zb, hy_conv_w[l], hy_conv_b[l], filt_params, hy_bias[l], hy_lat)
        wbound = 1.02 * WA_DIM ** 0.5 * jnp.max(jnp.abs(wa_qn[l])) * jnp.max(jnp.abs(wa_kn[l]))
        yc_ = _window_attention(qc, kc, vc, kcx, vcx, wa_sink[l], wbound, True)
        yd = _conformer(zd, cf_dw_w[l], cf_dw_b[l], cf_ln_g[l], cf_ln_b[l])
        x = _merge(x, mod_x, (ya, *yb, yc_, yd), zg, b_gate[l], wbs, wo)

        w_router = jnp.pad(jnp.concatenate([w_rg[l], w_re[l]], axis=1), ((0, 0), (0, LANES - N_GROUPS - N_EXPERTS)))
        b_router = jnp.pad(jnp.concatenate([b_rg[l], b_re[l]]), (0, LANES - N_GROUPS - N_EXPERTS)).reshape(1, LANES)
        ew = (w1[l], w3[l], w2[l])

        if not last:
            yca = _diff_attention(q1x, q2x, [(k1x, k2x, vx)], fixed, da_lam[l], da_subln[l], lam_init)
            ycb = _hyena(zbx, hy_conv_w[l], hy_conv_b[l], filt_params, hy_bias[l], hy_ctx)
            ycc = _window_attention(qcx, kcx, vcx, kcx, vcx, wa_sink[l], wbound, False)
            ycd = _conformer(zdx, cf_dw_w[l], cf_dw_b[l], cf_ln_g[l], cf_ln_b[l])
            xc = _merge(xc, mod_c, (yca, *ycb, ycc, ycd), zgx, b_gate[l], wbs, wo)
            xc = _moe(xc.reshape(1, b * cl, d), mod_c[:1], norm2_g[l], w_router, b_router, *ew).reshape(b, cl, d)
        x = _moe_sorted(x, mod_x, norm2_g[l], w_router, b_router, *ew)
    return x
```

```python
import functools
import math

import jax
import jax.numpy as jnp
from jax import lax
from jax.experimental import pallas as pl
from jax.experimental.pallas import tpu as pltpu
from jax.experimental.pallas import tpu_sc as plsc

F32 = jnp.float32
BF16 = jnp.bfloat16
HI = lax.Precision.HIGHEST

GRID_W = 64
BLOCK = 128
ROPE_BASE = 10000.0
EPS = 1e-6
NEG_INF = -1e30

DA_HEADS = 4
DA_DIM = 32
DA_VDIM = 64
HY_WIDTH = 256
HY_BANDS = 16
HY_FF = 64
HY_SHIFT = 0.05
HY_FAST_DECAY = 0.3
HY_SLOW_DECAY = 1.5
HY_TARGET = 1e-2
WA_HEADS = 4
WA_KV_HEADS = 2
WA_GROUP = 2
WA_DIM = 64
CF_WIDTH = 256
CF_TAPS = 31
N_BRANCH = 4
BRANCH_W = 256
N_GROUPS = 4
EXP_PER_GROUP = 4
N_EXPERTS = 16

W_A = 4 * DA_HEADS * DA_DIM + DA_HEADS * DA_VDIM
W_B = 3 * HY_WIDTH
W_C = (WA_HEADS + 2 * WA_KV_HEADS) * WA_DIM
W_D = 2 * CF_WIDTH
OFF_B = W_A
OFF_C = OFF_B + W_B
OFF_D = OFF_C + W_C
OFF_G = OFF_D + W_D

LOG2E = math.log2(math.e)
LANES = 128
SUBLANES = 8
VMEM_LIMIT = 56 * 1024 * 1024

DA_ONES = 16
DA_SHIFT_MAX = 50.0


def _cp(*sem):
    return pltpu.CompilerParams(dimension_semantics=sem, vmem_limit_bytes=VMEM_LIMIT)


def _rms(xf):
    return xf * lax.rsqrt(jnp.mean(xf * xf, axis=-1, keepdims=True) + EPS)


def _silu(x):
    return x * jax.nn.sigmoid(x)


def _mod_kernel(c_ref, w_ref, b_ref, o_ref):
    s = _silu(c_ref[...])
    o_ref[0] = jnp.dot(s, w_ref[0], precision=HI, preferred_element_type=F32) + b_ref[0]


def _mod_vectors(crows, w_mod, b_mod):
    depth, d, n = w_mod.shape
    r = crows.shape[0]
    tn = 1536
    return pl.pallas_call(
        _mod_kernel,
        grid=(depth, n // tn),
        in_specs=[pl.BlockSpec((r, d), lambda l, j: (0, 0)),
                  pl.BlockSpec((1, d, tn), lambda l, j: (l, 0, j)),
                  pl.BlockSpec((1, 1, tn), lambda l, j: (l, 0, j))],
        out_specs=pl.BlockSpec((1, r, tn), lambda l, j: (l, 0, j)),
        out_shape=jax.ShapeDtypeStruct((depth, r, n), F32),
        compiler_params=_cp("arbitrary", "arbitrary"),
        name="mod_vectors",
    )(crows, w_mod, b_mod.reshape(depth, 1, n))


def _rope_tables(s, d, reps):
    rows = s // GRID_W
    row = jnp.repeat(jnp.arange(rows, dtype=F32), GRID_W)
    col = jnp.tile(jnp.arange(GRID_W, dtype=F32), rows)
    qd = d // 4
    inv = ROPE_BASE ** (-jnp.arange(qd, dtype=F32) / qd)
    ar = row[:, None] * inv[None, :]
    ac = col[:, None] * inv[None, :]
    z = jnp.zeros_like(ar)
    cos = jnp.concatenate([jnp.cos(ar), jnp.cos(ar), jnp.cos(ac), jnp.cos(ac)], axis=-1)
    sin_up = jnp.concatenate([-jnp.sin(ar), z, -jnp.sin(ac), z], axis=-1)
    sin_dn = jnp.concatenate([z, jnp.sin(ar), z, jnp.sin(ac)], axis=-1)
    t = lambda a: jnp.tile(a, (1, reps))
    return t(cos), t(sin_up), t(sin_dn)


def _unit_tables(s, w):
    return jnp.ones((s, w), F32), jnp.zeros((s, w), F32), jnp.zeros((s, w), F32)


def _group_ones(width, group):
    i = jnp.arange(width) // group
    return (i[:, None] == i[None, :]).astype(BF16)


def _hi_lo(a):
    hi = a.astype(BF16)
    return hi, (a - hi.astype(F32)).astype(BF16)


def _slot_select(width, group):
    i = jnp.arange(width)
    dst = (i // group) * LANES + i % group
    return (dst[:, None] == jnp.arange((width // group) * LANES)[None, :]).astype(BF16)


def _slot_vector(n, lo, hi, value):
    j = jnp.arange(n * LANES) % LANES
    return jnp.where((j >= lo) & (j < hi), value, 0.0).astype(F32).reshape(1, n * LANES)


def _norm_rope(x, gmat, gain, cos, sup, sdn, group, qd):
    w = x.shape[-1]
    sh, sl = _hi_lo(x * x)
    ss = (jnp.dot(sh, gmat, preferred_element_type=F32) + jnp.dot(sl, gmat, preferred_element_type=F32)) * (1.0 / group)
    xn = x * lax.rsqrt(ss + EPS) * gain
    return xn * cos + pltpu.roll(xn, w - qd, 1) * sup + pltpu.roll(xn, qd, 1) * sdn


def _inproj_kernel(x_ref, m_ref, g_ref, w_ref,
                   ca, ua, da, cq, uq, dq, ck, uk, dk, gma, gmq, gmk, qna, kna, qnc, knc,
                   sela, selv, selq, selk, qvec, kvec, vvec, vvecc,
                   q1o, q2o, k1o, k2o, vo, zbo, qco, kco, vco, zdo, zgo):
    d = x_ref.shape[-1]
    x = x_ref[0]
    shift = m_ref[0, :, 0:d]
    scale = m_ref[0, :, d:2 * d]
    h = (_rms(x) * g_ref[...] * (1.0 + scale) + shift).astype(BF16)
    place = lambda y, sel: jnp.dot(y.astype(BF16), sel[...], preferred_element_type=F32)

    za = jnp.dot(h, w_ref[0, :, 0:OFF_B], preferred_element_type=F32)
    hw = DA_HEADS * DA_DIM
    cos, sup, sdn, gm = ca[...], ua[...], da[...], gma[...]
    qscale = DA_DIM ** -0.5 * LOG2E
    for t, (o, gain, sc, vec) in enumerate(((q1o, qna, qscale, qvec), (q2o, qna, qscale, qvec),
                                            (k1o, kna, 1.0, kvec), (k2o, kna, 1.0, kvec))):
        y = _norm_rope(za[:, t * hw:(t + 1) * hw], gm, gain[...], cos, sup, sdn, DA_DIM, DA_DIM // 4) * sc
        o[0] = (place(y, sela) + vec[...]).astype(BF16)
    vo[0] = (place(za[:, 4 * hw:], selv) + vvec[...]).astype(BF16)

    zbo[0] = jnp.dot(h, w_ref[0, :, OFF_B:OFF_C], preferred_element_type=F32).astype(BF16)

    zc = jnp.dot(h, w_ref[0, :, OFF_C:OFF_D], preferred_element_type=F32)
    qw = WA_HEADS * WA_DIM
    kw = WA_KV_HEADS * WA_DIM
    y = _norm_rope(zc[:, 0:qw], gmq[...], qnc[...], cq[...], uq[...], dq[...], WA_DIM, WA_DIM // 4) * (WA_DIM ** -0.5 * LOG2E)
    qco[0] = place(y, selq).astype(BF16)
    y = _norm_rope(zc[:, qw:qw + kw], gmk[...], knc[...], ck[...], uk[...], dk[...], WA_DIM, WA_DIM // 4)
    kco[0] = place(y, selk).astype(BF16)
    vco[0] = (place(zc[:, qw + kw:], selk) + vvecc[...]).astype(BF16)

    zdo[0] = jnp.dot(h, w_ref[0, :, OFF_D:OFF_G], preferred_element_type=F32).astype(BF16)
    for k in range(N_BRANCH):
        zgo[0, :, k * d:(k + 1) * d] = jnp.dot(h, w_ref[0, :, OFF_G + k * d:OFF_G + (k + 1) * d],
                                               preferred_element_type=F32).astype(BF16)


def _in_proj(x, mod, g, w_all, layer, tables, consts):
    bx, lx, d = x.shape
    tm = min(512, lx)
    const = lambda b, i: (0, 0)
    row = lambda b, i: (b, i, 0)
    once = lambda a: pl.BlockSpec(a.shape, const, pipeline_mode=pl.Buffered(1))
    widths = [DA_HEADS * LANES] * 5 + [W_B, WA_HEADS * LANES, WA_KV_HEADS * LANES, WA_KV_HEADS * LANES, W_D, N_BRANCH * d]
    return pl.pallas_call(
        _inproj_kernel,
        grid=(bx, lx // tm),
        in_specs=[pl.BlockSpec((1, tm, d), row), pl.BlockSpec((1, 1, mod.shape[-1]), lambda b, i: (b, 0, 0)),
                  pl.BlockSpec((1, d), const)]
                 + [pl.BlockSpec((1,) + w_all.shape[1:], lambda b, i: (layer, 0, 0), pipeline_mode=pl.Buffered(1))]
                 + [pl.BlockSpec((tm, t.shape[1]), lambda b, i: (i, 0)) for t in tables]
                 + [once(c) for c in consts],
        out_specs=[pl.BlockSpec((1, tm, w), row) for w in widths],
        out_shape=[jax.ShapeDtypeStruct((bx, lx, w), BF16) for w in widths],
        compiler_params=_cp("parallel", "parallel"),
        name="in_proj",
    )(x, mod, g.reshape(1, d), w_all, *tables, *consts)


def _da_lambda(lam_ref, lam_init):
    lv = lam_ref[...]
    return (jnp.exp(jnp.sum(lv[0:1] * lv[1:2], keepdims=True)) - jnp.exp(jnp.sum(lv[2:3] * lv[3:4], keepdims=True))
            + lam_init)


def _dattn_kernel(lam_ref, sg_ref, q1_ref, q2_ref, *rest, lam_init, online, nsrc, tk_max):
    srcs = [rest[3 * s:3 * s + 3] for s in range(nsrc)]
    o_ref, acc1, acc2 = rest[3 * nsrc:]
    dn = (((1,), (1,)), ((), ()))
    q1 = q1_ref[0]
    q2 = q2_ref[0]
    tq = q1.shape[0]
    acc1[...] = jnp.zeros_like(acc1)
    acc2[...] = jnp.zeros_like(acc2)
    carry = (jnp.full((tq, 1), NEG_INF, F32),) * 2 if online else 0

    for k1_ref, k2_ref, v_ref in srcs:
        tk = min(tk_max, k1_ref.shape[1])

        def body(j, c, k1_ref=k1_ref, k2_ref=k2_ref, v_ref=v_ref, tk=tk):
            rows = pl.ds(pl.multiple_of(j * tk, tk), tk)
            vc = v_ref[0, rows, :]
            s1 = lax.dot_general(q1, k1_ref[0, rows, :], dn, preferred_element_type=F32)
            s2 = lax.dot_general(q2, k2_ref[0, rows, :], dn, preferred_element_type=F32)
            if online:
                m1, m2 = c
                n1 = jnp.maximum(m1, jnp.max(s1, axis=-1, keepdims=True))
                n2 = jnp.maximum(m2, jnp.max(s2, axis=-1, keepdims=True))
                acc1[...] = jnp.exp2(m1 - n1) * acc1[...] + jnp.dot(jnp.exp2(s1 - n1).astype(BF16), vc,
                                                                    preferred_element_type=F32)
                acc2[...] = jnp.exp2(m2 - n2) * acc2[...] + jnp.dot(jnp.exp2(s2 - n2).astype(BF16), vc,
                                                                    preferred_element_type=F32)
                return n1, n2
            acc1[...] += jnp.dot(jnp.exp2(s1).astype(BF16), vc, preferred_element_type=F32)
            acc2[...] += jnp.dot(jnp.exp2(s2).astype(BF16), vc, preferred_element_type=F32)
            return c

        carry = lax.fori_loop(0, k1_ref.shape[1] // tk, body, carry)

    dv = DA_VDIM
    a1 = acc1[...]
    a2 = acc2[...]
    lam = _da_lambda(lam_ref, lam_init)
    o = a1 * (1.0 / a1[:, dv:dv + 1]) - a2 * (lam / a2[:, dv:dv + 1])
    o = jnp.where(lax.broadcasted_iota(jnp.int32, o.shape, 1) < dv, o, 0.0)
    o = o * lax.rsqrt(jnp.sum(o * o, axis=-1, keepdims=True) * (1.0 / dv) + EPS)
    o_ref[0] = (o * (sg_ref[...] * (1.0 - lam_init))).astype(BF16)


def _diff_attention(q1, q2, srcs, fixed, lam_p, subln, lam_init):
    b, lq, _ = q1.shape
    h = DA_HEADS
    const = lambda b_, h_, i: (0, 0)
    sg = jnp.pad(subln, (0, LANES - DA_VDIM)).reshape(1, LANES)
    flat = [a for src in srcs for a in src]

    def call(online, *args):
        tq = min(256 if online else 2048, lq)
        qs = pl.BlockSpec((1, tq, LANES), lambda b_, h_, i: (b_, i, h_))
        return pl.pallas_call(
            functools.partial(_dattn_kernel, lam_init=lam_init, online=online, nsrc=len(srcs),
                              tk_max=256 if online else 512),
            grid=(b, h, lq // tq),
            in_specs=[pl.BlockSpec(lam_p.shape, const), pl.BlockSpec((1, LANES), const), qs, qs]
                     + [pl.BlockSpec((1, a.shape[1], LANES), lambda b_, h_, i: (b_, 0, h_)) for a in flat],
            out_specs=qs,
            out_shape=jax.ShapeDtypeStruct((b, lq, h * LANES), BF16),
            scratch_shapes=[pltpu.VMEM((tq, LANES), F32)] * 2,
            compiler_params=_cp("parallel", "parallel", "arbitrary"),
            name="diff_attention_online" if online else "diff_attention",
        )(*args)

    return lax.cond(fixed, functools.partial(call, False), functools.partial(call, True),
                    lam_p, sg, q1, q2, *flat)


def _hy_prep_kernel(zp_ref, zc_ref, zn_ref, w_ref, b_ref, p_ref, x0_ref, ext):
    i = pl.program_id(1)
    last = pl.num_programs(1) - 1
    tl = zc_ref.shape[1]
    h = 2 * SUBLANES
    ext[0:h] = jnp.where(i == 0, 0.0, zp_ref[0].astype(F32))
    ext[h:h + tl] = zc_ref[0].astype(F32)
    ext[h + tl:] = jnp.where(i == last, 0.0, zn_ref[0].astype(F32))
    w = w_ref[...]
    u = (ext[pl.ds(h - 1, tl), :] * w[0:1] + ext[pl.ds(h, tl), :] * w[1:2] + ext[pl.ds(h + 1, tl), :] * w[2:3]
         + b_ref[...])
    hw = HY_WIDTH
    x0_ref[0] = u[:, 0:hw]
    p_ref[0] = u[:, 2 * hw:3 * hw] * u[:, hw:2 * hw]


def _hy_prep(zb, conv_w, conv_b):
    bx, lx, w = zb.shape
    tl = min(256, lx)
    h = 2 * SUBLANES
    nh = lx // h
    per = tl // h
    out = jax.ShapeDtypeStruct((bx, lx, HY_WIDTH), F32)
    return pl.pallas_call(
        _hy_prep_kernel,
        grid=(bx, lx // tl),
        in_specs=[pl.BlockSpec((1, h, w), lambda b, i: (b, jnp.maximum(i * per - 1, 0), 0)),
                  pl.BlockSpec((1, tl, w), lambda b, i: (b, i, 0)),
                  pl.BlockSpec((1, h, w), lambda b, i: (b, jnp.minimum((i + 1) * per, nh - 1), 0)),
                  pl.BlockSpec(conv_w.shape, lambda b, i: (0, 0)),
                  pl.BlockSpec((1, w), lambda b, i: (0, 0))],
        out_specs=[pl.BlockSpec((1, tl, HY_WIDTH), lambda b, i: (b, i, 0))] * 2,
        out_shape=[out, out],
        scratch_shapes=[pltpu.VMEM((tl + 2 * h, w), F32)],
        compiler_params=_cp("parallel", "parallel"),
        name="hyena_prep",
    )(zb, zb, zb, conv_w, conv_b.reshape(1, w))


def _hy_filter_kernel(feat_ref, w1, b1, w2, b2, w3, b3, w4, fr_ref, dl_ref, filt_ref, ssq_ref, *, s):
    i = pl.program_id(0)
    tr = feat_ref.shape[0]
    feat = feat_ref[...]
    fr = fr_ref[...]
    dot = lambda a, w: _dot3(*_hi_lo(a), w[...])
    a = jnp.sin(fr * (dot(feat, w1) + b1[...]))
    a = jnp.sin(fr * (dot(a, w2) + b2[...]))
    a = jnp.sin(fr * (dot(a, w3) + b3[...]))
    coef = dot(a, w4)
    n = i * tr + lax.broadcasted_iota(jnp.int32, (tr, 1), 0)
    window = jnp.exp(-feat[:, 0:1] * dl_ref[...]) + HY_SHIFT
    half = jnp.where(n < s, coef[:, :HY_WIDTH], coef[:, HY_WIDTH:])
    filt = jnp.where(n == s, 0.0, half * window)
    filt_ref[...] = filt

    @pl.when(i == 0)
    def _():
        ssq_ref[...] = jnp.zeros_like(ssq_ref)

    ssq_ref[...] += jnp.sum(filt * filt, axis=0, keepdims=True)


def _hy_feat(s):
    t = jnp.linspace(0.0, 1.0, s, dtype=F32)[:, None]
    w = (2.0 * math.pi / s) * jnp.arange(s, dtype=F32)[:, None]
    bands = jnp.linspace(1e-4, HY_BANDS - 1, HY_BANDS, dtype=F32)[None, :]
    feat = jnp.concatenate([t, jnp.cos(w * bands), jnp.sin(w * bands)], axis=-1)
    feat = jnp.concatenate([feat, feat[:1], feat[:0:-1]], axis=0)
    return jnp.pad(feat, ((0, 0), (0, LANES - feat.shape[1])))


def _hy_filter(s, feat, w1, b1, w2, b2, w3, b3, w4, freq):
    n = 2 * s
    tr = min(512, n)
    deltas = jnp.abs(jnp.linspace(math.log(HY_TARGET) / HY_FAST_DECAY, math.log(HY_TARGET) / HY_SLOW_DECAY,
                                  HY_WIDTH, dtype=F32)).reshape(1, HY_WIDTH)
    w1p = jnp.pad(w1, ((0, LANES - w1.shape[0]), (0, 0)))
    row = lambda a: a.reshape(1, -1)
    args = (feat, w1p, row(b1), w2, row(b2), w3, row(b3), w4, row(freq), deltas)
    const = lambda i: (0, 0)
    return pl.pallas_call(
        functools.partial(_hy_filter_kernel, s=s),
        grid=(n // tr,),
        in_specs=[pl.BlockSpec((tr, LANES), lambda i: (i, 0))] + [pl.BlockSpec(a.shape, const) for a in args[1:]],
        out_specs=[pl.BlockSpec((tr, HY_WIDTH), lambda i: (i, 0)), pl.BlockSpec((1, HY_WIDTH), const)],
        out_shape=[jax.ShapeDtypeStruct((n, HY_WIDTH), F32), jax.ShapeDtypeStruct((1, HY_WIDTH), F32)],
        compiler_params=_cp("arbitrary"),
        name="hyena_filter",
    )(*args)


def _dft_factors(n):
    lg = n.bit_length() - 1
    assert 1 << lg == n
    n1 = 1 << (lg // 2)
    return n1, n // n1


def _dft_tables(n):
    n1, n2 = _dft_factors(n)
    ia = jnp.arange(n1, dtype=jnp.int32)
    ang1 = (2.0 * math.pi / n1) * ((ia[:, None] * ia[None, :]) % n1).astype(F32)
    f1 = jnp.concatenate([jnp.cos(ang1), -jnp.sin(ang1)], axis=0)
    c = jnp.arange(n1, dtype=jnp.int32)[:, None, None]
    d = jnp.arange(n2, dtype=jnp.int32)[None, :, None]
    b = jnp.arange(n2, dtype=jnp.int32)[None, None, :]
    ang = (2.0 * math.pi / n) * ((b * (c + n1 * d)) % n).astype(F32)
    re, im = jnp.cos(ang), -jnp.sin(ang)
    m1 = jnp.concatenate([jnp.concatenate([re, -im], axis=2), jnp.concatenate([im, re], axis=2)], axis=1)
    m2 = jnp.swapaxes(m1, 1, 2)
    ang4 = ang1[: n1 // 2]
    f4 = jnp.concatenate([jnp.cos(ang4), -jnp.sin(ang4)], axis=1) * (1.0 / n)
    return f1, _hi_lo(m1), _hi_lo(m2), _hi_lo(_per_offset(f4))


def _dot3(mh, ml, a):
    ah, al = _hi_lo(a)
    return (jnp.dot(mh, ah, preferred_element_type=F32) + jnp.dot(mh, al, preferred_element_type=F32)
            + jnp.dot(ml, ah, preferred_element_type=F32))


HY_TB = SUBLANES


def _per_offset(f):
    return jnp.kron(f, jnp.eye(HY_TB, dtype=f.dtype))


def _hy_stage1_kernel(fh_ref, fl_ref, x_ref, re_ref, im_ref):
    _, n1, tb, w = re_ref.shape
    y = _dot3(fh_ref[...], fl_ref[...], x_ref[0].reshape(-1, w))
    re_ref[0] = y[:n1 * tb].reshape(n1, tb, w)
    im_ref[0] = y[n1 * tb:].reshape(n1, tb, w)


def _hy_stage1(f, x4):
    bx, k, n2, w = x4.shape
    tb = HY_TB
    n1 = f[0].shape[0] // (2 * tb)
    out = jax.ShapeDtypeStruct((bx, n1, n2, w), F32)
    return pl.pallas_call(
        _hy_stage1_kernel,
        grid=(bx, n2 // tb),
        in_specs=[pl.BlockSpec(f[0].shape, lambda b, j: (0, 0))] * 2
                 + [pl.BlockSpec((1, k, tb, w), lambda b, j: (b, 0, j, 0))],
        out_specs=[pl.BlockSpec((1, n1, tb, w), lambda b, j: (b, 0, j, 0))] * 2,
        out_shape=[out, out],
        compiler_params=_cp("parallel", "parallel"),
        name="hyena_dft_stage1",
    )(*f, x4)


def _hy_mid_kernel(m1h_ref, m1l_ref, m2h_ref, m2l_ref, re_ref, im_ref, hre_ref, him_ref, ore_ref, oim_ref):
    n2 = re_ref.shape[2]
    for c in range(re_ref.shape[1]):
        a = jnp.concatenate([re_ref[0, c], im_ref[0, c]], axis=0)
        x = _dot3(m1h_ref[c], m1l_ref[c], a)
        xre, xim = x[:n2], x[n2:]
        hre, him = hre_ref[c], him_ref[c]
        y = jnp.concatenate([xre * hre - xim * him, xre * him + xim * hre], axis=0)
        bb = _dot3(m2h_ref[c], m2l_ref[c], y)
        ore_ref[0, c] = bb[:n2]
        oim_ref[0, c] = bb[n2:]


def _hy_spec_kernel(m1h_ref, m1l_ref, re_ref, im_ref, rs_ref, ore_ref, oim_ref):
    n2 = re_ref.shape[2]
    for c in range(re_ref.shape[1]):
        a = jnp.concatenate([re_ref[0, c], im_ref[0, c]], axis=0)
        x = _dot3(m1h_ref[c], m1l_ref[c], a) * rs_ref[...]
        ore_ref[c] = x[:n2]
        oim_ref[c] = x[n2:]


def _hy_filter_spectrum(m1, are, aim, rs):
    _, n1, n2, w = are.shape
    cb = min(4, n1)
    blk = pl.BlockSpec((1, cb, n2, w), lambda c: (0, c, 0, 0))
    mblk = pl.BlockSpec((cb, 2 * n2, 2 * n2), lambda c: (c, 0, 0))
    oblk = pl.BlockSpec((cb, n2, w), lambda c: (c, 0, 0))
    out = jax.ShapeDtypeStruct((n1, n2, w), F32)
    return pl.pallas_call(
        _hy_spec_kernel,
        grid=(n1 // cb,),
        in_specs=[mblk, mblk, blk, blk, pl.BlockSpec((1, w), lambda c: (0, 0))],
        out_specs=[oblk, oblk],
        out_shape=[out, out],
        compiler_params=_cp("parallel"),
        name="hyena_filter_spectrum",
    )(*m1, are, aim, rs)


def _hy_mid(m1, m2, are, aim, hre, him):
    bx, n1, n2, w = are.shape
    cb = min(4, n1)
    blk = pl.BlockSpec((1, cb, n2, w), lambda c, b: (b, c, 0, 0))
    mblk = pl.BlockSpec((cb, 2 * n2, 2 * n2), lambda c, b: (c, 0, 0))
    hblk = pl.BlockSpec((cb, n2, w), lambda c, b: (c, 0, 0))
    out = jax.ShapeDtypeStruct((bx, n1, n2, w), F32)
    return pl.pallas_call(
        _hy_mid_kernel,
        grid=(n1 // cb, bx),
        in_specs=[mblk, mblk, mblk, mblk, blk, blk, hblk, hblk],
        out_specs=[blk, blk],
        out_shape=[out, out],
        compiler_params=_cp("parallel", "parallel"),
        name="hyena_dft_mid",
    )(*m1, *m2, are, aim, hre, him)


def _hy_last_kernel(fh_ref, fl_ref, re_ref, im_ref, p_ref, bias_ref, o_ref):
    _, k, tb, w = p_ref.shape
    spec = jnp.concatenate([re_ref[0].reshape(-1, w), im_ref[0].reshape(-1, w)], axis=0)
    y = _dot3(fh_ref[...], fl_ref[...], spec)
    o_ref[0] = y.reshape(k, tb, w) + p_ref[0] * bias_ref[...]


def _hy_last(f4, bre, bim, p4, bias):
    bx, n1, n2, w = bre.shape
    k = n1 // 2
    tb = HY_TB
    big = pl.BlockSpec((1, n1, tb, w), lambda b, j: (b, 0, j, 0))
    small = pl.BlockSpec((1, k, tb, w), lambda b, j: (b, 0, j, 0))
    return pl.pallas_call(
        _hy_last_kernel,
        grid=(bx, n2 // tb),
        in_specs=[pl.BlockSpec(f4[0].shape, lambda b, j: (0, 0))] * 2 + [big, big, small,
                  pl.BlockSpec((1, w), lambda b, j: (0, 0))],
        out_specs=small,
        out_shape=jax.ShapeDtypeStruct((bx, k, n2, w), F32),
        compiler_params=_cp("parallel", "parallel"),
        name="hyena_dft_last",
    )(*f4, bre, bim, p4, bias)


def _hyena(zb, conv_w, conv_b, filt_params, hy_bias, consts):
    bx, lx, _ = zb.shape
    feat, (f1, m1, m2, f4) = consts
    n = 2 * lx
    n1, n2 = _dft_factors(n)
    w = HY_WIDTH
    filt, ssq = _hy_filter(lx, feat, *filt_params)
    rs = lax.rsqrt(ssq + EPS)
    fre, fim = _hy_stage1(_hi_lo(_per_offset(f1)), filt.reshape(1, n1, n2, w))
    hre, him = _hy_filter_spectrum(m1, fre, fim, rs)
    p, x0 = _hy_prep(zb, conv_w, conv_b)
    k = n1 // 2
    p4 = p.reshape(bx, k, n2, w)
    are, aim = _hy_stage1(_hi_lo(_per_offset(f1[:, :k])), p4)
    bre, bim = _hy_mid(m1, m2, are, aim, hre, him)
    t = _hy_last(f4, bre, bim, p4, hy_bias.reshape(1, w))
    return x0, t.reshape(bx, lx, w)


WA_SHIFT_MAX = 35.0


def _wattn_kernel(sk_ref, q_ref, bias_ref, *rest, banded, fixed):
    if banded:
        kp_ref, kc_ref, kn_ref, kx_ref, vp_ref, vc_ref, vn_ref, vx_ref, o_ref = rest
    else:
        kx_ref, vx_ref, o_ref = rest
    h = pl.program_id(1)
    i = pl.program_id(2)
    last = pl.num_programs(2) - 1
    qb = q_ref.shape[1]
    q = jnp.concatenate([q_ref[0, :, :LANES], q_ref[0, :, LANES:]], axis=0)
    if banded:
        kk = jnp.concatenate([kp_ref[0], kc_ref[0], kn_ref[0], kx_ref[0]], axis=0)
        vv = jnp.concatenate([vp_ref[0], vc_ref[0], vn_ref[0], vx_ref[0]], axis=0)
    else:
        kk, vv = kx_ref[0], vx_ref[0]
    s = lax.dot_general(q, kk, (((1,), (1,)), ((), ())), preferred_element_type=F32) + bias_ref[0]
    if banded:
        c = lax.broadcasted_iota(jnp.int32, (1, kk.shape[0]), 1) - BLOCK
        outside = ((c < 0) & (i == 0)) | ((c >= qb) & (c < qb + BLOCK) & (i == last))
        s = s + jnp.where(outside, NEG_INF, 0.0)
    top = lax.broadcasted_iota(jnp.int32, (2 * qb, 1), 0) < qb
    sk = jnp.where(top, sk_ref[h * WA_GROUP], sk_ref[h * WA_GROUP + 1])
    if fixed:
        p, sink_term = jnp.exp2(s), sk
    else:
        m = jnp.maximum(jnp.max(s, axis=-1, keepdims=True), sk)
        p, sink_term = jnp.exp2(s - m), jnp.exp2(sk - m)
    acc = jnp.dot(p.astype(BF16), vv, preferred_element_type=F32)
    o = (acc * (1.0 / (acc[:, WA_DIM:WA_DIM + 1] + sink_term))).astype(BF16)
    o_ref[0] = jnp.concatenate([o[:qb], o[qb:]], axis=1)


def _window_attention(q, k, v, kx, vx, sink, bound, banded):
    b, lq, _ = q.shape
    cx = kx.shape[1]
    qb = min(256, lq)
    per = qb // BLOCK
    nblk = lq // BLOCK
    side = lambda f: pl.BlockSpec((1, BLOCK, LANES), f)
    prev = side(lambda b_, h, i: (b_, jnp.maximum(i * per - 1, 0), h))
    nxt = side(lambda b_, h, i: (b_, jnp.minimum((i + 1) * per, nblk - 1), h))
    cur = pl.BlockSpec((1, qb, LANES), lambda b_, h, i: (b_, i, h))
    ctx = pl.BlockSpec((1, cx, LANES), lambda b_, h, i: (b_, 0, h))
    qspec = pl.BlockSpec((1, qb, WA_GROUP * LANES), lambda b_, h, i: (b_, i, h))
    mask = jnp.zeros((WA_GROUP * qb, cx), F32)
    if banded:
        r = jnp.arange(WA_GROUP * qb)[:, None] % qb
        c = jnp.arange(qb + 2 * BLOCK)[None, :] - BLOCK
        mask = jnp.concatenate([jnp.where(jnp.abs(r - c) <= BLOCK, 0.0, NEG_INF).astype(F32), mask], axis=1)
        specs, args = [prev, cur, nxt, ctx, prev, cur, nxt, ctx], (k, k, k, kx, v, v, v, vx)
    else:
        specs, args = [ctx, ctx], (kx, vx)
    fixed = bound <= WA_SHIFT_MAX
    shift = jnp.maximum(bound, sink)
    rows = jnp.repeat(shift.reshape(WA_KV_HEADS, WA_GROUP), qb, axis=1)[:, :, None]
    bias = mask[None] - jnp.where(fixed, LOG2E * rows, 0.0)
    sk = jnp.where(fixed, jnp.exp2(LOG2E * (sink - shift)), LOG2E * sink)

    def call(fixed_, *ops):
        return pl.pallas_call(
            functools.partial(_wattn_kernel, banded=banded, fixed=fixed_),
            grid=(b, WA_KV_HEADS, lq // qb),
            in_specs=[pl.BlockSpec(memory_space=pltpu.SMEM), qspec,
                      pl.BlockSpec((1,) + bias.shape[1:], lambda b_, h, i: (h, 0, 0))] + specs,
            out_specs=qspec,
            out_shape=jax.ShapeDtypeStruct((b, lq, WA_HEADS * LANES), BF16),
            compiler_params=_cp("parallel", "parallel", "arbitrary"),
            name="window_attention" if fixed_ else "window_attention_online",
        )(*ops)

    return lax.cond(fixed, functools.partial(call, True), functools.partial(call, False), sk, q, bias, *args)


def _conf_kernel(zp_ref, zc_ref, zn_ref, w_ref, b_ref, lg_ref, lb_ref, o_ref, ext, sh, *, halo):
    i = pl.program_id(1)
    last = pl.num_programs(1) - 1
    tl = zc_ref.shape[1]
    cw = CF_WIDTH

    def glu(z_ref):
        z = z_ref[0].astype(F32)
        return z[:, :cw] * jax.nn.sigmoid(z[:, cw:])

    ext[0:halo] = jnp.where(i == 0, 0.0, glu(zp_ref))
    ext[halo:halo + tl] = glu(zc_ref)
    ext[halo + tl:] = jnp.where(i == last, 0.0, glu(zn_ref))
    for r in range(1, SUBLANES):
        sh[r - 1] = ext[pl.ds(r, sh.shape[1]), :]
    w = w_ref[...]
    u = jnp.zeros((tl, cw), F32) + b_ref[...]
    for j in range(CF_TAPS):
        off = halo - CF_TAPS // 2 + j
        base, r = off // SUBLANES * SUBLANES, off % SUBLANES
        tap = ext[pl.ds(base, tl), :] if r == 0 else sh[r - 1, pl.ds(base, tl), :]
        u = u + tap * w[j:j + 1]
    uc = u - jnp.mean(u, axis=-1, keepdims=True)
    y = uc * lax.rsqrt(jnp.mean(uc * uc, axis=-1, keepdims=True) + EPS) * lg_ref[...] + lb_ref[...]
    o_ref[0] = _silu(y).astype(BF16)


def _conformer(zd, dw_w, dw_b, ln_g, ln_b):
    bx, lx, w = zd.shape
    tl = min(256, lx)
    halo = 2 * SUBLANES
    nh = lx // halo
    per = tl // halo
    row = lambda a: a.reshape(1, -1)
    const = lambda b, i: (0, 0)
    return pl.pallas_call(
        functools.partial(_conf_kernel, halo=halo),
        grid=(bx, lx // tl),
        in_specs=[pl.BlockSpec((1, halo, w), lambda b, i: (b, jnp.maximum(i * per - 1, 0), 0)),
                  pl.BlockSpec((1, tl, w), lambda b, i: (b, i, 0)),
                  pl.BlockSpec((1, halo, w), lambda b, i: (b, jnp.minimum((i + 1) * per, nh - 1), 0)),
                  pl.BlockSpec(dw_w.shape, const)] + [pl.BlockSpec((1, CF_WIDTH), const)] * 3,
        out_specs=pl.BlockSpec((1, tl, CF_WIDTH), lambda b, i: (b, i, 0)),
        out_shape=jax.ShapeDtypeStruct((bx, lx, CF_WIDTH), BF16),
        scratch_shapes=[pltpu.VMEM((tl + 2 * halo, CF_WIDTH), F32),
                        pltpu.VMEM((SUBLANES - 1, tl + 2 * halo - SUBLANES, CF_WIDTH), F32)],
        compiler_params=_cp("parallel", "parallel"),
        name="conformer_conv",
    )(zd, zd, zd, dw_w, row(dw_b), row(ln_g), row(ln_b))


def _merge_kernel(x_ref, m_ref, ya, yb0, ybt, yc, yd, zg_ref, bg_ref, wa, wb, wc, wd, wo_ref, o_ref, *, goff):
    d = x_ref.shape[-1]
    acc = jnp.zeros(x_ref.shape[1:], F32)
    ys = (ya[0], (yb0[0] * ybt[0]).astype(BF16), yc[0], yd[0])
    for i, (y, w) in enumerate(zip(ys, (wa, wb, wc, wd))):
        gate = jax.nn.sigmoid(zg_ref[0, :, i * d:(i + 1) * d].astype(F32) + bg_ref[:, i * d:(i + 1) * d])
        acc = acc + gate * jnp.dot(y, w[...], preferred_element_type=F32)
    out = jnp.dot(acc.astype(BF16), wo_ref[...], preferred_element_type=F32)
    o_ref[0] = x_ref[0] + m_ref[0, :, goff:goff + d] * out


def _merge(x, mod, ys, zg, b_gate, wbs, w_out):
    bx, lx, d = x.shape
    tm = min(512, lx)
    row = lambda b, i: (b, i, 0)
    const = lambda b, i: (0, 0)
    once = lambda a: pl.BlockSpec(a.shape, const, pipeline_mode=pl.Buffered(1))
    return pl.pallas_call(
        functools.partial(_merge_kernel, goff=2 * d),
        grid=(bx, lx // tm),
        in_specs=[pl.BlockSpec((1, tm, d), row), pl.BlockSpec((1, 1, mod.shape[-1]), lambda b, i: (b, 0, 0))]
                 + [pl.BlockSpec((1, tm, y.shape[-1]), row) for y in ys]
                 + [pl.BlockSpec((1, tm, N_BRANCH * d), row), pl.BlockSpec((1, N_BRANCH * d), const)]
                 + [once(w) for w in wbs] + [once(w_out)],
        out_specs=pl.BlockSpec((1, tm, d), row),
        out_shape=jax.ShapeDtypeStruct(x.shape, F32),
        input_output_aliases={0: 0},
        compiler_params=_cp("parallel", "parallel"),
        name="merge",
    )(x, mod, *ys, zg, b_gate.reshape(1, -1), *wbs, w_out)


def _slot_rows(w, group):
    n = w.shape[0] // group
    return jnp.pad(w.reshape(n, group, -1), ((0, 0), (0, LANES - group), (0, 0))).reshape(n * LANES, -1)


def _moe_kernel(x_ref, m_ref, g_ref, wr_ref, br_ref, w1_ref, w3_ref, w2_ref, o_ref, h_sc, gate_sc, acc_sc):
    e = pl.program_id(2)
    d = x_ref.shape[-1]
    tm = x_ref.shape[1]
    lane = lax.broadcasted_iota(jnp.int32, (tm, LANES), 1).astype(F32)

    @pl.when(e == 0)
    def _():
        h = _rms(x_ref[0]) * g_ref[...] * (1.0 + m_ref[0, :, 4 * d:5 * d]) + m_ref[0, :, 3 * d:4 * d]
        h_sc[...] = h.astype(BF16)
        lg = _dot3(*_hi_lo(h), wr_ref[...]) + br_ref[...]
        isg = lane < N_GROUPS
        gmax = jnp.max(jnp.where(isg, lg, NEG_INF), axis=-1, keepdims=True)
        gi = jnp.min(jnp.where(isg & (lg == gmax), lane, LANES), axis=-1, keepdims=True)
        gw = 1.0 / jnp.sum(jnp.where(isg, jnp.exp(lg - gmax), 0.0), axis=-1, keepdims=True)
        lo = N_GROUPS + gi * EXP_PER_GROUP
        ise = (lane >= lo) & (lane < lo + EXP_PER_GROUP)
        le = jnp.where(ise, lg, NEG_INF)
        m1 = jnp.max(le, axis=-1, keepdims=True)
        i1 = jnp.min(jnp.where(ise & (le == m1), lane, LANES), axis=-1, keepdims=True)
        ise2 = ise & (lane != i1)
        le2 = jnp.where(ise2, lg, NEG_INF)
        m2 = jnp.max(le2, axis=-1, keepdims=True)
        i2 = jnp.min(jnp.where(ise2 & (le2 == m2), lane, LANES), axis=-1, keepdims=True)
        r = jnp.exp(m2 - m1)
        wa = gw / (1.0 + r)
        gate_sc[...] = jnp.where(lane == i1, wa, 0.0) + jnp.where(lane == i2, wa * r, 0.0)
        acc_sc[...] = jnp.zeros_like(acc_sc)

    hb = h_sc[...]
    u = (_silu(jnp.dot(hb, w1_ref[0].astype(BF16), preferred_element_type=F32))
         * jnp.dot(hb, w3_ref[0].astype(BF16), preferred_element_type=F32))
    ge = jnp.sum(jnp.where(lane == (e + N_GROUPS).astype(F32), gate_sc[...], 0.0), axis=-1, keepdims=True)
    acc_sc[...] += ge * jnp.dot(u.astype(BF16), w2_ref[0].astype(BF16), preferred_element_type=F32)

    @pl.when(e == pl.num_programs(2) - 1)
    def _():
        o_ref[0] = x_ref[0] + m_ref[0, :, 5 * d:6 * d] * acc_sc[...]


def _moe(x, mod, g, w_router, b_router, w1, w3, w2):
    bx, lx, d = x.shape
    tm = min(1024, lx)
    ne, _, f = w1.shape
    row = lambda b, i, e: (b, i, 0)
    const = lambda b, i, e: (0, 0)
    return pl.pallas_call(
        _moe_kernel,
        grid=(bx, lx // tm, ne),
        in_specs=[pl.BlockSpec((1, tm, d), row), pl.BlockSpec((1, 1, mod.shape[-1]), lambda b, i, e: (b, 0, 0)),
                  pl.BlockSpec((1, d), const), pl.BlockSpec((d, LANES), const), pl.BlockSpec((1, LANES), const),
                  pl.BlockSpec((1, d, f), lambda b, i, e: (e, 0, 0)), pl.BlockSpec((1, d, f), lambda b, i, e: (e, 0, 0)),
                  pl.BlockSpec((1, f, d), lambda b, i, e: (e, 0, 0))],
        out_specs=pl.BlockSpec((1, tm, d), row),
        out_shape=jax.ShapeDtypeStruct(x.shape, F32),
        scratch_shapes=[pltpu.VMEM((tm, d), BF16), pltpu.VMEM((tm, LANES), F32), pltpu.VMEM((tm, d), F32)],
        input_output_aliases={0: 0},
        compiler_params=_cp("parallel", "parallel", "arbitrary"),
        name="moe",
    )(x, mod, g.reshape(1, d), w_router, b_router, w1, w3, w2)


SC_CORES = 2
SC_SUBCORES = 16
SC_CHUNK = 64
MOE_ROWS = 1024


def _sc_gather(table, idx):
    n = idx.shape[0]
    w = table.shape[1]
    per = n // (SC_CORES * SC_SUBCORES)
    assert per * SC_CORES * SC_SUBCORES == n and per % SC_CHUNK == 0
    mesh = plsc.VectorSubcoreMesh(core_axis_name="c", subcore_axis_name="s")

    @functools.partial(
        pl.kernel, mesh=mesh, out_type=jax.ShapeDtypeStruct((n, w), table.dtype),
        scratch_types=[pltpu.VMEM((SC_CHUNK,), jnp.int32), pltpu.VMEM((SC_CHUNK, w), table.dtype),
                       pltpu.SemaphoreType.DMA],
        name="sc_row_gather")
    def gather(table_hbm, idx_hbm, out_hbm, idx_v, rows_v, sem):
        base = (lax.axis_index("s") * SC_CORES + lax.axis_index("c")) * per

        @pl.loop(0, per // SC_CHUNK)
        def _(j):
            off = pl.multiple_of(base + j * SC_CHUNK, SC_CHUNK)
            pltpu.sync_copy(idx_hbm.at[pl.ds(off, SC_CHUNK)], idx_v)
            pltpu.async_copy(table_hbm.at[idx_v], rows_v, sem).wait()
            pltpu.sync_copy(rows_v, out_hbm.at[pl.ds(off, SC_CHUNK)])

    return gather(table, idx)


def _sc_scatter(rows, idx, n_out):
    n, w = rows.shape
    per = n // (SC_CORES * SC_SUBCORES)
    assert per * SC_CORES * SC_SUBCORES == n and per % SC_CHUNK == 0
    mesh = plsc.VectorSubcoreMesh(core_axis_name="c", subcore_axis_name="s")

    @functools.partial(
        pl.kernel, mesh=mesh, out_type=jax.ShapeDtypeStruct((n_out, w), rows.dtype),
        scratch_types=[pltpu.VMEM((SC_CHUNK,), jnp.int32), pltpu.VMEM((SC_CHUNK, w), rows.dtype),
                       pltpu.SemaphoreType.DMA],
        name="sc_row_scatter")
    def scatter(rows_hbm, idx_hbm, out_hbm, idx_v, rows_v, sem):
        base = (lax.axis_index("s") * SC_CORES + lax.axis_index("c")) * per

        @pl.loop(0, per // SC_CHUNK)
        def _(j):
            off = pl.multiple_of(base + j * SC_CHUNK, SC_CHUNK)
            pltpu.sync_copy(idx_hbm.at[pl.ds(off, SC_CHUNK)], idx_v)
            pltpu.sync_copy(rows_hbm.at[pl.ds(off, SC_CHUNK)], rows_v)
            pltpu.async_copy(rows_v, out_hbm.at[idx_v], sem).wait()

    return scatter(rows, idx)


def _route_kernel(x_ref, m_ref, g_ref, wr_ref, br_ref, rows_ref, gi_ref):
    d = x_ref.shape[-1]
    tm = x_ref.shape[1]
    lane = lax.broadcasted_iota(jnp.int32, (tm, LANES), 1).astype(F32)
    h = _rms(x_ref[0]) * g_ref[...] * (1.0 + m_ref[0, :, 4 * d:5 * d]) + m_ref[0, :, 3 * d:4 * d]
    rows_ref[0, :, :d] = h
    lg = _dot3(*_hi_lo(h), wr_ref[...]) + br_ref[...]
    isg = lane < N_GROUPS
    gmax = jnp.max(jnp.where(isg, lg, NEG_INF), axis=-1, keepdims=True)
    gi = jnp.min(jnp.where(isg & (lg == gmax), lane, LANES), axis=-1, keepdims=True)
    gw = 1.0 / jnp.sum(jnp.where(isg, jnp.exp(lg - gmax), 0.0), axis=-1, keepdims=True)
    lo = N_GROUPS + gi * EXP_PER_GROUP
    ise = (lane >= lo) & (lane < lo + EXP_PER_GROUP)
    le = jnp.where(ise, lg, NEG_INF)
    m1 = jnp.max(le, axis=-1, keepdims=True)
    i1 = jnp.min(jnp.where(ise & (le == m1), lane, LANES), axis=-1, keepdims=True)
    ise2 = ise & (lane != i1)
    le2 = jnp.where(ise2, lg, NEG_INF)
    m2 = jnp.max(le2, axis=-1, keepdims=True)
    i2 = jnp.min(jnp.where(ise2 & (le2 == m2), lane, LANES), axis=-1, keepdims=True)
    r = jnp.exp(m2 - m1)
    wa = gw / (1.0 + r)
    rows_ref[0, :, d:] = jnp.where(lane == i1 - lo, wa, 0.0) + jnp.where(lane == i2 - lo, wa * r, 0.0)
    gi_ref[0] = gi.astype(jnp.int32)


def _route(x, mod, g, w_router, b_router):
    bx, lx, d = x.shape
    tm = min(512, lx)
    row = lambda b, i: (b, i, 0)
    const = lambda b, i: (0, 0)
    return pl.pallas_call(
        _route_kernel,
        grid=(bx, lx // tm),
        in_specs=[pl.BlockSpec((1, tm, d), row), pl.BlockSpec((1, 1, mod.shape[-1]), lambda b, i: (b, 0, 0)),
                  pl.BlockSpec((1, d), const), pl.BlockSpec((d, LANES), const), pl.BlockSpec((1, LANES), const)],
        out_specs=[pl.BlockSpec((1, tm, d + LANES), row), pl.BlockSpec((1, tm, 1), row)],
        out_shape=[jax.ShapeDtypeStruct((bx, lx, d + LANES), F32), jax.ShapeDtypeStruct((bx, lx, 1), jnp.int32)],
        compiler_params=_cp("parallel", "parallel"),
        name="moe_route",
    )(x, mod, g.reshape(1, d), w_router, b_router)


def _gmoe_kernel(grp_ref, nv_ref, xs_ref, w1_ref, w3_ref, w2_ref, o_ref, h_sc, acc_sc):
    i = pl.program_id(0)
    e = pl.program_id(1)
    tm, d = h_sc.shape
    valid = lax.broadcasted_iota(jnp.int32, (tm, 1), 0) < nv_ref[i]

    @pl.when(nv_ref[i] > 0)
    def _():
        @pl.when(e == 0)
        def _():
            h_sc[...] = jnp.where(valid, xs_ref[:, :d], 0.0).astype(BF16)
            acc_sc[...] = jnp.zeros_like(acc_sc)

        hb = h_sc[...]
        u = (_silu(jnp.dot(hb, w1_ref[0].astype(BF16), preferred_element_type=F32))
             * jnp.dot(hb, w3_ref[0].astype(BF16), preferred_element_type=F32))
        lane = lax.broadcasted_iota(jnp.int32, (tm, LANES), 1)
        ge = jnp.sum(jnp.where(valid & (lane == e), xs_ref[:, d:], 0.0), axis=-1, keepdims=True)
        acc_sc[...] += ge * jnp.dot(u.astype(BF16), w2_ref[0].astype(BF16), preferred_element_type=F32)

    @pl.when(e == pl.num_programs(1) - 1)
    def _():
        o_ref[...] = jnp.where(nv_ref[i] > 0, acc_sc[...], 0.0)


def _grouped_moe(grp, nv, xs, w1, w3, w2):
    p, dw = xs.shape
    d = dw - LANES
    _, _, f = w1.shape
    wmap = lambda i, e, grp, nv: (grp[i] * EXP_PER_GROUP + e, 0, 0)
    rows = lambda i, e, grp, nv: (i, 0)
    return pl.pallas_call(
        _gmoe_kernel,
        grid_spec=pltpu.PrefetchScalarGridSpec(
            num_scalar_prefetch=2,
            grid=(p // MOE_ROWS, EXP_PER_GROUP),
            in_specs=[pl.BlockSpec((MOE_ROWS, dw), rows),
                      pl.BlockSpec((1, d, f), wmap), pl.BlockSpec((1, d, f), wmap), pl.BlockSpec((1, f, d), wmap)],
            out_specs=pl.BlockSpec((MOE_ROWS, d), rows),
            scratch_shapes=[pltpu.VMEM((MOE_ROWS, d), BF16), pltpu.VMEM((MOE_ROWS, d), F32)]),
        out_shape=jax.ShapeDtypeStruct((p, d), F32),
        compiler_params=_cp("arbitrary", "arbitrary"),
        name="moe_experts",
    )(grp, nv, xs, w1, w3, w2)


def _residual_kernel(x_ref, m_ref, y_ref, o_ref):
    d = x_ref.shape[-1]
    o_ref[0] = x_ref[0] + m_ref[0, :, 5 * d:6 * d] * y_ref[0]


def _residual(x, mod, y):
    bx, lx, d = x.shape
    tm = min(1024, lx)
    row = lambda b, i: (b, i, 0)
    return pl.pallas_call(
        _residual_kernel,
        grid=(bx, lx // tm),
        in_specs=[pl.BlockSpec((1, tm, d), row), pl.BlockSpec((1, 1, mod.shape[-1]), lambda b, i: (b, 0, 0)),
                  pl.BlockSpec((1, tm, d), row)],
        out_specs=pl.BlockSpec((1, tm, d), row),
        out_shape=jax.ShapeDtypeStruct(x.shape, F32),
        input_output_aliases={0: 0},
        compiler_params=_cp("parallel", "parallel"),
        name="moe_residual",
    )(x, mod, y)


def _moe_sorted(x, mod, g, w_router, b_router, w1, w3, w2):
    bx, lx, d = x.shape
    t = bx * lx
    rows, gi = _route(x, mod, g, w_router, b_router)
    gi = gi.reshape(t)
    onehot = (gi[:, None] == jnp.arange(N_GROUPS, dtype=jnp.int32)[None, :]).astype(jnp.int32)
    csum = jnp.cumsum(onehot, axis=0)
    counts = csum[-1]
    rank = jnp.take_along_axis(csum, gi[:, None], axis=1)[:, 0] - 1
    padded = (counts + MOE_ROWS - 1) // MOE_ROWS * MOE_ROWS
    pend = jnp.cumsum(padded)
    pstart = pend - padded
    pos = (pstart[gi] + rank).astype(jnp.int32)
    p = t + N_GROUPS * MOE_ROWS
    bstart = jnp.arange(p // MOE_ROWS, dtype=jnp.int32) * MOE_ROWS
    grp = jnp.minimum(jnp.searchsorted(pend, bstart, side="right"), N_GROUPS - 1).astype(jnp.int32)
    nv = jnp.clip(pstart[grp] + counts[grp] - bstart, 0, MOE_ROWS).astype(jnp.int32)
    xs = _sc_scatter(rows.reshape(t, d + LANES), pos, p)
    ys = _grouped_moe(grp, nv, xs, w1, w3, w2)
    yt = _sc_gather(ys, pos)
    return _residual(x, mod, yt.reshape(bx, lx, d))


def kernel(x, c, ctx, c_ctx, w_mod, b_mod, norm1_g, norm2_g, w_in, b_gate, da_qn, da_kn, da_lam, da_subln, hy_conv_w, hy_conv_b, hf_w1, hf_b1, hf_w2, hf_b2, hf_w3, hf_b3, hf_w4, hf_freq, hy_bias, wa_qn, wa_kn, wa_sink, cf_dw_w, cf_dw_b, cf_ln_g, cf_ln_b, w_branch, w_out, w_rg, b_rg, w_re, b_re, w1, w3, w2):
    b, s, d = x.shape
    cl = ctx.shape[1]
    depth = w_mod.shape[0]
    assert s % 256 == 0 and cl % 256 == 0 and s % GRID_W == 0

    nrow = -(-(b + 1) // SUBLANES) * SUBLANES
    crows = jnp.zeros((nrow, d), F32).at[:b].set(c).at[b].set(c_ctx)
    mods = _mod_vectors(crows, w_mod, b_mod)

    aw, qw, kw = DA_HEADS * DA_DIM, WA_HEADS * WA_DIM, WA_KV_HEADS * WA_DIM
    tab_lat = (*_rope_tables(s, DA_DIM, DA_HEADS), *_rope_tables(s, WA_DIM, WA_HEADS), *_rope_tables(s, WA_DIM, WA_KV_HEADS))
    tab_ctx = (*_unit_tables(cl, aw), *_unit_tables(cl, qw), *_unit_tables(cl, kw))
    hy_lat = (_hy_feat(s), _dft_tables(2 * s))
    hy_ctx = (_hy_feat(cl), _dft_tables(2 * cl))
    gms = (_group_ones(aw, DA_DIM), _group_ones(qw, WA_DIM), _group_ones(kw, WA_DIM))
    sels = (_slot_select(aw, DA_DIM), _slot_select(DA_HEADS * DA_VDIM, DA_VDIM), _slot_select(qw, WA_DIM),
            _slot_select(kw, WA_DIM))
    qvec = _slot_vector(DA_HEADS, DA_DIM, DA_DIM + 1, 1.0)
    vvec = _slot_vector(DA_HEADS, DA_VDIM, DA_VDIM + DA_ONES, 1.0)
    vvecc = _slot_vector(WA_KV_HEADS, WA_DIM, WA_DIM + DA_ONES, 1.0)

    w_in_b = w_in.astype(BF16)
    xc = ctx
    for l in range(depth):
        last = l == depth - 1
        lam_init = 0.8 - 0.6 * math.exp(-0.3 * l)
        mod_x = mods[l, :b][:, None, :]
        mod_c = jnp.broadcast_to(mods[l, b][None, None, :], (b, 1, mods.shape[-1]))
        shift = 1.02 * LOG2E * DA_DIM ** 0.5 * jnp.max(jnp.abs(da_qn[l])) * jnp.max(jnp.abs(da_kn[l]))
        fixed = shift <= DA_SHIFT_MAX
        kvec = _slot_vector(DA_HEADS, DA_DIM, DA_DIM + 1, jnp.where(fixed, -shift, 0.0))
        tile = lambda a, n: jnp.tile(a, n).reshape(1, -1)
        consts = (*gms, tile(da_qn[l], DA_HEADS), tile(da_kn[l], DA_HEADS), tile(wa_qn[l], WA_HEADS),
                  tile(wa_kn[l], WA_KV_HEADS), *sels, qvec, kvec, vvec, vvecc)
        q1, q2, k1, k2, v, zb, qc, kc, vc, zd, zg = _in_proj(x, mod_x, norm1_g[l], w_in_b, l, tab_lat, consts)
        q1x, q2x, k1x, k2x, vx, zbx, qcx, kcx, vcx, zdx, zgx = _in_proj(xc, mod_c, norm1_g[l], w_in_b, l, tab_ctx, consts)
        filt_params = (hf_w1[l], hf_b1[l], hf_w2[l], hf_b2[l], hf_w3[l], hf_b3[l], hf_w4[l], hf_freq[l])
        wb4 = w_branch[l].astype(BF16)
        wbs = (_slot_rows(wb4[0], DA_VDIM), wb4[1], _slot_rows(wb4[2], WA_DIM), wb4[3])
        wo = w_out[l].astype(BF16)

        ya = _diff_attention(q1, q2, [(k1, k2, v), (k1x, k2x, vx)], fixed, da_lam[l], da_subln[l], lam_init)
        yb = _hyena(zb, hy_conv_w[l], hy_conv_b[l], filt_params, hy_bias[l], hy_lat)
        wbound = 1.02 * WA_DIM ** 0.5 * jnp.max(jnp.abs(wa_qn[l])) * jnp.max(jnp.abs(wa_kn[l]))
        yc_ = _window_attention(qc, kc, vc, kcx, vcx, wa_sink[l], wbound, True)
        yd = _conformer(zd, cf_dw_w[l], cf_dw_b[l], cf_ln_g[l], cf_ln_b[l])
        x = _merge(x, mod_x, (ya, *yb, yc_, yd), zg, b_gate[l], wbs, wo)

        w_router = jnp.pad(jnp.concatenate([w_rg[l], w_re[l]], axis=1), ((0, 0), (0, LANES - N_GROUPS - N_EXPERTS)))
        b_router = jnp.pad(jnp.concatenate([b_rg[l], b_re[l]]), (0, LANES - N_GROUPS - N_EXPERTS)).reshape(1, LANES)
        ew = (w1[l], w3[l], w2[l])

        if not last:
            yca = _diff_attention(q1x, q2x, [(k1x, k2x, vx)], fixed, da_lam[l], da_subln[l], lam_init)
            ycb = _hyena(zbx, hy_conv_w[l], hy_conv_b[l], filt_params, hy_bias[l], hy_ctx)
            ycc = _window_attention(qcx, kcx, vcx, kcx, vcx, wa_sink[l], wbound, False)
            ycd = _conformer(zdx, cf_dw_w[l], cf_dw_b[l], cf_ln_g[l], cf_ln_b[l])
            xc = _merge(xc, mod_c, (yca, *ycb, ycc, ycd), zgx, b_gate[l], wbs, wo)
            xc = _moe(xc.reshape(1, b * cl, d), mod_c[:1], norm2_g[l], w_router, b_router, *ew).reshape(b, cl, d)
        x = _moe_sorted(x, mod_x, norm2_g[l], w_router, b_router, *ew)
    return x
```

```python
import functools
import math

import jax
import jax.numpy as jnp
from jax import lax
from jax.experimental import pallas as pl
from jax.experimental.pallas import tpu as pltpu
from jax.experimental.pallas import tpu_sc as plsc

F32 = jnp.float32
BF16 = jnp.bfloat16
HI = lax.Precision.HIGHEST

GRID_W = 64
BLOCK = 128
ROPE_BASE = 10000.0
EPS = 1e-6
NEG_INF = -1e30

DA_HEADS = 4
DA_DIM = 32
DA_VDIM = 64
HY_WIDTH = 256
HY_BANDS = 16
HY_FF = 64
HY_SHIFT = 0.05
HY_FAST_DECAY = 0.3
HY_SLOW_DECAY = 1.5
HY_TARGET = 1e-2
WA_HEADS = 4
WA_KV_HEADS = 2
WA_GROUP = 2
WA_DIM = 64
CF_WIDTH = 256
CF_TAPS = 31
N_BRANCH = 4
BRANCH_W = 256
N_GROUPS = 4
EXP_PER_GROUP = 4
N_EXPERTS = 16

W_A = 4 * DA_HEADS * DA_DIM + DA_HEADS * DA_VDIM
W_B = 3 * HY_WIDTH
W_C = (WA_HEADS + 2 * WA_KV_HEADS) * WA_DIM
W_D = 2 * CF_WIDTH
OFF_B = W_A
OFF_C = OFF_B + W_B
OFF_D = OFF_C + W_C
OFF_G = OFF_D + W_D

LOG2E = math.log2(math.e)
LANES = 128
SUBLANES = 8
VMEM_LIMIT = 56 * 1024 * 1024

DA_ONES = 16
DA_SHIFT_MAX = 50.0


def _cp(*sem):
    return pltpu.CompilerParams(dimension_semantics=sem, vmem_limit_bytes=VMEM_LIMIT)


def _rms(xf):
    return xf * lax.rsqrt(jnp.mean(xf * xf, axis=-1, keepdims=True) + EPS)


def _silu(x):
    return x * jax.nn.sigmoid(x)


def _mod_kernel(c_ref, w_ref, b_ref, o_ref):
    s = _silu(c_ref[...])
    o_ref[0] = jnp.dot(s, w_ref[0], precision=HI, preferred_element_type=F32) + b_ref[0]


def _mod_vectors(crows, w_mod, b_mod):
    depth, d, n = w_mod.shape
    r = crows.shape[0]
    tn = 1536
    return pl.pallas_call(
        _mod_kernel,
        grid=(depth, n // tn),
        in_specs=[pl.BlockSpec((r, d), lambda l, j: (0, 0)),
                  pl.BlockSpec((1, d, tn), lambda l, j: (l, 0, j)),
                  pl.BlockSpec((1, 1, tn), lambda l, j: (l, 0, j))],
        out_specs=pl.BlockSpec((1, r, tn), lambda l, j: (l, 0, j)),
        out_shape=jax.ShapeDtypeStruct((depth, r, n), F32),
        compiler_params=_cp("arbitrary", "arbitrary"),
        name="mod_vectors",
    )(crows, w_mod, b_mod.reshape(depth, 1, n))


def _rope_tables(s, d, reps):
    rows = s // GRID_W
    row = jnp.repeat(jnp.arange(rows, dtype=F32), GRID_W)
    col = jnp.tile(jnp.arange(GRID_W, dtype=F32), rows)
    qd = d // 4
    inv = ROPE_BASE ** (-jnp.arange(qd, dtype=F32) / qd)
    ar = row[:, None] * inv[None, :]
    ac = col[:, None] * inv[None, :]
    z = jnp.zeros_like(ar)
    cos = jnp.concatenate([jnp.cos(ar), jnp.cos(ar), jnp.cos(ac), jnp.cos(ac)], axis=-1)
    sin_up = jnp.concatenate([-jnp.sin(ar), z, -jnp.sin(ac), z], axis=-1)
    sin_dn = jnp.concatenate([z, jnp.sin(ar), z, jnp.sin(ac)], axis=-1)
    t = lambda a: jnp.tile(a, (1, reps))
    return t(cos), t(sin_up), t(sin_dn)


def _unit_tables(s, w):
    return jnp.ones((s, w), F32), jnp.zeros((s, w), F32), jnp.zeros((s, w), F32)


def _group_ones(width, group):
    i = jnp.arange(width) // group
    return (i[:, None] == i[None, :]).astype(BF16)


def _hi_lo(a):
    hi = a.astype(BF16)
    return hi, (a - hi.astype(F32)).astype(BF16)


def _slot_fill(lo, hi, value):
    j = jnp.arange(LANES)
    return jnp.where((j >= lo) & (j < hi), value, 0.0).astype(F32).reshape(1, LANES)


def _store_slots(o_ref, y, group, fill):
    per = LANES // group
    lane = lax.broadcasted_iota(jnp.int32, (y.shape[0], LANES), 1)
    for h in range(y.shape[1] // group):
        blk = y[:, h // per * LANES:(h // per + 1) * LANES]
        if h % per:
            blk = pltpu.roll(blk, LANES - h % per * group, 1)
        o_ref[0, :, h * LANES:(h + 1) * LANES] = jnp.where(lane < group, blk, fill).astype(BF16)


def _norm_rope(x, gmat, gain, cos, sup, sdn, group, qd):
    w = x.shape[-1]
    sh, sl = _hi_lo(x * x)
    ss = (jnp.dot(sh, gmat, preferred_element_type=F32) + jnp.dot(sl, gmat, preferred_element_type=F32)) * (1.0 / group)
    xn = x * lax.rsqrt(ss + EPS) * gain
    return xn * cos + pltpu.roll(xn, w - qd, 1) * sup + pltpu.roll(xn, qd, 1) * sdn


def _inproj_kernel(x_ref, m_ref, g_ref, w_ref,
                   ca, ua, da, cq, uq, dq, ck, uk, dk, gma, gmq, gmk, qna, kna, qnc, knc,
                   qvec, kvec, vvec,
                   q1o, q2o, k1o, k2o, vo, zbo, qco, kco, vco, zdo, zgo):
    d = x_ref.shape[-1]
    x = x_ref[0]
    shift = m_ref[0, :, 0:d]
    scale = m_ref[0, :, d:2 * d]
    h = (_rms(x) * g_ref[...] * (1.0 + scale) + shift).astype(BF16)

    za = jnp.dot(h, w_ref[0, :, 0:OFF_B], preferred_element_type=F32)
    hw = DA_HEADS * DA_DIM
    cos, sup, sdn, gm = ca[...], ua[...], da[...], gma[...]
    qscale = DA_DIM ** -0.5 * LOG2E
    for t, (o, gain, sc, vec) in enumerate(((q1o, qna, qscale, qvec), (q2o, qna, qscale, qvec),
                                            (k1o, kna, 1.0, kvec), (k2o, kna, 1.0, kvec))):
        y = _norm_rope(za[:, t * hw:(t + 1) * hw], gm, gain[...], cos, sup, sdn, DA_DIM, DA_DIM // 4) * sc
        _store_slots(o, y, DA_DIM, vec[...])
    _store_slots(vo, za[:, 4 * hw:], DA_VDIM, vvec[...])

    zbo[0] = jnp.dot(h, w_ref[0, :, OFF_B:OFF_C], preferred_element_type=F32).astype(BF16)

    zc = jnp.dot(h, w_ref[0, :, OFF_C:OFF_D], preferred_element_type=F32)
    qw = WA_HEADS * WA_DIM
    kw = WA_KV_HEADS * WA_DIM
    y = _norm_rope(zc[:, 0:qw], gmq[...], qnc[...], cq[...], uq[...], dq[...], WA_DIM, WA_DIM // 4) * (WA_DIM ** -0.5 * LOG2E)
    _store_slots(qco, y, WA_DIM, 0.0)
    y = _norm_rope(zc[:, qw:qw + kw], gmk[...], knc[...], ck[...], uk[...], dk[...], WA_DIM, WA_DIM // 4)
    _store_slots(kco, y, WA_DIM, 0.0)
    _store_slots(vco, zc[:, qw + kw:], WA_DIM, vvec[...])

    zdo[0] = jnp.dot(h, w_ref[0, :, OFF_D:OFF_G], preferred_element_type=F32).astype(BF16)
    for k in range(N_BRANCH):
        zgo[0, :, k * d:(k + 1) * d] = jnp.dot(h, w_ref[0, :, OFF_G + k * d:OFF_G + (k + 1) * d],
                                               preferred_element_type=F32).astype(BF16)


def _in_proj(x, mod, g, w_all, layer, tables, consts):
    bx, lx, d = x.shape
    tm = min(512, lx)
    const = lambda b, i: (0, 0)
    row = lambda b, i: (b, i, 0)
    once = lambda a: pl.BlockSpec(a.shape, const, pipeline_mode=pl.Buffered(1))
    widths = [DA_HEADS * LANES] * 5 + [W_B, WA_HEADS * LANES, WA_KV_HEADS * LANES, WA_KV_HEADS * LANES, W_D, N_BRANCH * d]
    return pl.pallas_call(
        _inproj_kernel,
        grid=(bx, lx // tm),
        in_specs=[pl.BlockSpec((1, tm, d), row), pl.BlockSpec((1, 1, mod.shape[-1]), lambda b, i: (b, 0, 0)),
                  pl.BlockSpec((1, d), const)]
                 + [pl.BlockSpec((1,) + w_all.shape[1:], lambda b, i: (layer, 0, 0), pipeline_mode=pl.Buffered(1))]
                 + [pl.BlockSpec((tm, t.shape[1]), lambda b, i: (i, 0)) for t in tables]
                 + [once(c) for c in consts],
        out_specs=[pl.BlockSpec((1, tm, w), row) for w in widths],
        out_shape=[jax.ShapeDtypeStruct((bx, lx, w), BF16) for w in widths],
        compiler_params=_cp("parallel", "parallel"),
        name="in_proj",
    )(x, mod, g.reshape(1, d), w_all, *tables, *consts)


def _da_lambda(lam_ref, lam_init):
    lv = lam_ref[...]
    return (jnp.exp(jnp.sum(lv[0:1] * lv[1:2], keepdims=True)) - jnp.exp(jnp.sum(lv[2:3] * lv[3:4], keepdims=True))
            + lam_init)


def _dattn_kernel(lam_ref, sg_ref, q1_ref, q2_ref, *rest, lam_init, online, nsrc, tk_max):
    srcs = [rest[3 * s:3 * s + 3] for s in range(nsrc)]
    o_ref, acc1, acc2 = rest[3 * nsrc:]
    dn = (((1,), (1,)), ((), ()))
    q1 = q1_ref[0]
    q2 = q2_ref[0]
    tq = q1.shape[0]
    acc1[...] = jnp.zeros_like(acc1)
    acc2[...] = jnp.zeros_like(acc2)
    carry = (jnp.full((tq, 1), NEG_INF, F32),) * 2 if online else 0

    for k1_ref, k2_ref, v_ref in srcs:
        tk = min(tk_max, k1_ref.shape[1])

        def body(j, c, k1_ref=k1_ref, k2_ref=k2_ref, v_ref=v_ref, tk=tk):
            rows = pl.ds(pl.multiple_of(j * tk, tk), tk)
            vc = v_ref[0, rows, :]
            s1 = lax.dot_general(q1, k1_ref[0, rows, :], dn, preferred_element_type=F32)
            s2 = lax.dot_general(q2, k2_ref[0, rows, :], dn, preferred_element_type=F32)
            if online:
                m1, m2 = c
                n1 = jnp.maximum(m1, jnp.max(s1, axis=-1, keepdims=True))
                n2 = jnp.maximum(m2, jnp.max(s2, axis=-1, keepdims=True))
                acc1[...] = jnp.exp2(m1 - n1) * acc1[...] + jnp.dot(jnp.exp2(s1 - n1).astype(BF16), vc,
                                                                    preferred_element_type=F32)
                acc2[...] = jnp.exp2(m2 - n2) * acc2[...] + jnp.dot(jnp.exp2(s2 - n2).astype(BF16), vc,
                                                                    preferred_element_type=F32)
                return n1, n2
            acc1[...] += jnp.dot(jnp.exp2(s1).astype(BF16), vc, preferred_element_type=F32)
            acc2[...] += jnp.dot(jnp.exp2(s2).astype(BF16), vc, preferred_element_type=F32)
            return c

        carry = lax.fori_loop(0, k1_ref.shape[1] // tk, body, carry)

    dv = DA_VDIM
    a1 = acc1[...]
    a2 = acc2[...]
    lam = _da_lambda(lam_ref, lam_init)
    o = a1 * (1.0 / a1[:, dv:dv + 1]) - a2 * (lam / a2[:, dv:dv + 1])
    o = jnp.where(lax.broadcasted_iota(jnp.int32, o.shape, 1) < dv, o, 0.0)
    o = o * lax.rsqrt(jnp.sum(o * o, axis=-1, keepdims=True) * (1.0 / dv) + EPS)
    o_ref[0] = (o * (sg_ref[...] * (1.0 - lam_init))).astype(BF16)


def _diff_attention(q1, q2, srcs, fixed, lam_p, subln, lam_init):
    b, lq, _ = q1.shape
    h = DA_HEADS
    const = lambda b_, h_, i: (0, 0)
    sg = jnp.pad(subln, (0, LANES - DA_VDIM)).reshape(1, LANES)
    flat = [a for src in srcs for a in src]

    def call(online, *args):
        tq = min(256 if online else 2048, lq)
        qs = pl.BlockSpec((1, tq, LANES), lambda b_, h_, i: (b_, i, h_))
        return pl.pallas_call(
            functools.partial(_dattn_kernel, lam_init=lam_init, online=online, nsrc=len(srcs),
                              tk_max=256 if online else 512),
            grid=(b, h, lq // tq),
            in_specs=[pl.BlockSpec(lam_p.shape, const), pl.BlockSpec((1, LANES), const), qs, qs]
                     + [pl.BlockSpec((1, a.shape[1], LANES), lambda b_, h_, i: (b_, 0, h_)) for a in flat],
            out_specs=qs,
            out_shape=jax.ShapeDtypeStruct((b, lq, h * LANES), BF16),
            scratch_shapes=[pltpu.VMEM((tq, LANES), F32)] * 2,
            compiler_params=_cp("parallel", "parallel", "arbitrary"),
            name="diff_attention_online" if online else "diff_attention",
        )(*args)

    return lax.cond(fixed, functools.partial(call, False), functools.partial(call, True),
                    lam_p, sg, q1, q2, *flat)


def _hy_prep_kernel(zp_ref, zc_ref, zn_ref, w_ref, b_ref, p_ref, x0_ref, ext):
    i = pl.program_id(1)
    last = pl.num_programs(1) - 1
    tl = zc_ref.shape[1]
    h = 2 * SUBLANES
    ext[0:h] = jnp.where(i == 0, 0.0, zp_ref[0].astype(F32))
    ext[h:h + tl] = zc_ref[0].astype(F32)
    ext[h + tl:] = jnp.where(i == last, 0.0, zn_ref[0].astype(F32))
    w = w_ref[...]
    u = (ext[pl.ds(h - 1, tl), :] * w[0:1] + ext[pl.ds(h, tl), :] * w[1:2] + ext[pl.ds(h + 1, tl), :] * w[2:3]
         + b_ref[...])
    hw = HY_WIDTH
    x0_ref[0] = u[:, 0:hw]
    p_ref[0] = u[:, 2 * hw:3 * hw] * u[:, hw:2 * hw]


def _hy_prep(zb, conv_w, conv_b):
    bx, lx, w = zb.shape
    tl = min(256, lx)
    h = 2 * SUBLANES
    nh = lx // h
    per = tl // h
    out = jax.ShapeDtypeStruct((bx, lx, HY_WIDTH), F32)
    return pl.pallas_call(
        _hy_prep_kernel,
        grid=(bx, lx // tl),
        in_specs=[pl.BlockSpec((1, h, w), lambda b, i: (b, jnp.maximum(i * per - 1, 0), 0)),
                  pl.BlockSpec((1, tl, w), lambda b, i: (b, i, 0)),
                  pl.BlockSpec((1, h, w), lambda b, i: (b, jnp.minimum((i + 1) * per, nh - 1), 0)),
                  pl.BlockSpec(conv_w.shape, lambda b, i: (0, 0)),
                  pl.BlockSpec((1, w), lambda b, i: (0, 0))],
        out_specs=[pl.BlockSpec((1, tl, HY_WIDTH), lambda b, i: (b, i, 0))] * 2,
        out_shape=[out, out],
        scratch_shapes=[pltpu.VMEM((tl + 2 * h, w), F32)],
        compiler_params=_cp("parallel", "parallel"),
        name="hyena_prep",
    )(zb, zb, zb, conv_w, conv_b.reshape(1, w))


def _hy_filter_kernel(feat_ref, w1, b1, w2, b2, w3, b3, w4, fr_ref, dl_ref, filt_ref, ssq_ref, *, s):
    i = pl.program_id(0)
    tr = feat_ref.shape[0]
    feat = feat_ref[...]
    fr = fr_ref[...]
    dot = lambda a, w: _dot3(*_hi_lo(a), w[...])
    a = jnp.sin(fr * (dot(feat, w1) + b1[...]))
    a = jnp.sin(fr * (dot(a, w2) + b2[...]))
    a = jnp.sin(fr * (dot(a, w3) + b3[...]))
    coef = dot(a, w4)
    n = i * tr + lax.broadcasted_iota(jnp.int32, (tr, 1), 0)
    window = jnp.exp(-feat[:, 0:1] * dl_ref[...]) + HY_SHIFT
    half = jnp.where(n < s, coef[:, :HY_WIDTH], coef[:, HY_WIDTH:])
    filt = jnp.where(n == s, 0.0, half * window)
    filt_ref[...] = filt

    @pl.when(i == 0)
    def _():
        ssq_ref[...] = jnp.zeros_like(ssq_ref)

    ssq_ref[...] += jnp.sum(filt * filt, axis=0, keepdims=True)


def _hy_feat(s):
    t = jnp.linspace(0.0, 1.0, s, dtype=F32)[:, None]
    w = (2.0 * math.pi / s) * jnp.arange(s, dtype=F32)[:, None]
    bands = jnp.linspace(1e-4, HY_BANDS - 1, HY_BANDS, dtype=F32)[None, :]
    feat = jnp.concatenate([t, jnp.cos(w * bands), jnp.sin(w * bands)], axis=-1)
    feat = jnp.concatenate([feat, feat[:1], feat[:0:-1]], axis=0)
    return jnp.pad(feat, ((0, 0), (0, LANES - feat.shape[1])))


def _hy_filter(s, feat, w1, b1, w2, b2, w3, b3, w4, freq):
    n = 2 * s
    tr = min(512, n)
    deltas = jnp.abs(jnp.linspace(math.log(HY_TARGET) / HY_FAST_DECAY, math.log(HY_TARGET) / HY_SLOW_DECAY,
                                  HY_WIDTH, dtype=F32)).reshape(1, HY_WIDTH)
    w1p = jnp.pad(w1, ((0, LANES - w1.shape[0]), (0, 0)))
    row = lambda a: a.reshape(1, -1)
    args = (feat, w1p, row(b1), w2, row(b2), w3, row(b3), w4, row(freq), deltas)
    const = lambda i: (0, 0)
    return pl.pallas_call(
        functools.partial(_hy_filter_kernel, s=s),
        grid=(n // tr,),
        in_specs=[pl.BlockSpec((tr, LANES), lambda i: (i, 0))] + [pl.BlockSpec(a.shape, const) for a in args[1:]],
        out_specs=[pl.BlockSpec((tr, HY_WIDTH), lambda i: (i, 0)), pl.BlockSpec((1, HY_WIDTH), const)],
        out_shape=[jax.ShapeDtypeStruct((n, HY_WIDTH), F32), jax.ShapeDtypeStruct((1, HY_WIDTH), F32)],
        compiler_params=_cp("arbitrary"),
        name="hyena_filter",
    )(*args)


def _dft_factors(n):
    lg = n.bit_length() - 1
    assert 1 << lg == n
    n1 = 1 << (lg // 2)
    return n1, n // n1


def _dft_tables(n):
    n1, n2 = _dft_factors(n)
    ia = jnp.arange(n1, dtype=jnp.int32)
    ang1 = (2.0 * math.pi / n1) * ((ia[:, None] * ia[None, :]) % n1).astype(F32)
    f1 = jnp.concatenate([jnp.cos(ang1), -jnp.sin(ang1)], axis=0)
    c = jnp.arange(n1, dtype=jnp.int32)[:, None, None]
    d = jnp.arange(n2, dtype=jnp.int32)[None, :, None]
    b = jnp.arange(n2, dtype=jnp.int32)[None, None, :]
    ang = (2.0 * math.pi / n) * ((b * (c + n1 * d)) % n).astype(F32)
    re, im = jnp.cos(ang), -jnp.sin(ang)
    m1 = jnp.concatenate([jnp.concatenate([re, -im], axis=2), jnp.concatenate([im, re], axis=2)], axis=1)
    m2 = jnp.swapaxes(m1, 1, 2)
    ang4 = ang1[: n1 // 2]
    f4 = jnp.concatenate([jnp.cos(ang4), -jnp.sin(ang4)], axis=1) * (1.0 / n)
    return f1, _hi_lo(m1), _hi_lo(m2), _hi_lo(_per_offset(f4))


def _dot3(mh, ml, a):
    ah, al = _hi_lo(a)
    return (jnp.dot(mh, ah, preferred_element_type=F32) + jnp.dot(mh, al, preferred_element_type=F32)
            + jnp.dot(ml, ah, preferred_element_type=F32))


HY_TB = SUBLANES


def _per_offset(f):
    return jnp.kron(f, jnp.eye(HY_TB, dtype=f.dtype))


def _hy_stage1_kernel(fh_ref, fl_ref, x_ref, re_ref, im_ref):
    _, n1, tb, w = re_ref.shape
    y = _dot3(fh_ref[...], fl_ref[...], x_ref[0].reshape(-1, w))
    re_ref[0] = y[:n1 * tb].reshape(n1, tb, w)
    im_ref[0] = y[n1 * tb:].reshape(n1, tb, w)


def _hy_stage1(f, x4):
    bx, k, n2, w = x4.shape
    tb = HY_TB
    n1 = f[0].shape[0] // (2 * tb)
    out = jax.ShapeDtypeStruct((bx, n1, n2, w), F32)
    return pl.pallas_call(
        _hy_stage1_kernel,
        grid=(bx, n2 // tb),
        in_specs=[pl.BlockSpec(f[0].shape, lambda b, j: (0, 0))] * 2
                 + [pl.BlockSpec((1, k, tb, w), lambda b, j: (b, 0, j, 0))],
        out_specs=[pl.BlockSpec((1, n1, tb, w), lambda b, j: (b, 0, j, 0))] * 2,
        out_shape=[out, out],
        compiler_params=_cp("parallel", "parallel"),
        name="hyena_dft_stage1",
    )(*f, x4)


def _hy_mid_kernel(m1h_ref, m1l_ref, m2h_ref, m2l_ref, re_ref, im_ref, hre_ref, him_ref, ore_ref, oim_ref):
    n2 = re_ref.shape[2]
    for c in range(re_ref.shape[1]):
        a = jnp.concatenate([re_ref[0, c], im_ref[0, c]], axis=0)
        x = _dot3(m1h_ref[c], m1l_ref[c], a)
        xre, xim = x[:n2], x[n2:]
        hre, him = hre_ref[c], him_ref[c]
        y = jnp.concatenate([xre * hre - xim * him, xre * him + xim * hre], axis=0)
        bb = _dot3(m2h_ref[c], m2l_ref[c], y)
        ore_ref[0, c] = bb[:n2]
        oim_ref[0, c] = bb[n2:]


def _hy_spec_kernel(m1h_ref, m1l_ref, re_ref, im_ref, rs_ref, ore_ref, oim_ref):
    n2 = re_ref.shape[2]
    for c in range(re_ref.shape[1]):
        a = jnp.concatenate([re_ref[0, c], im_ref[0, c]], axis=0)
        x = _dot3(m1h_ref[c], m1l_ref[c], a) * rs_ref[...]
        ore_ref[c] = x[:n2]
        oim_ref[c] = x[n2:]


def _hy_filter_spectrum(m1, are, aim, rs):
    _, n1, n2, w = are.shape
    cb = min(4, n1)
    blk = pl.BlockSpec((1, cb, n2, w), lambda c: (0, c, 0, 0))
    mblk = pl.BlockSpec((cb, 2 * n2, 2 * n2), lambda c: (c, 0, 0))
    oblk = pl.BlockSpec((cb, n2, w), lambda c: (c, 0, 0))
    out = jax.ShapeDtypeStruct((n1, n2, w), F32)
    return pl.pallas_call(
        _hy_spec_kernel,
        grid=(n1 // cb,),
        in_specs=[mblk, mblk, blk, blk, pl.BlockSpec((1, w), lambda c: (0, 0))],
        out_specs=[oblk, oblk],
        out_shape=[out, out],
        compiler_params=_cp("parallel"),
        name="hyena_filter_spectrum",
    )(*m1, are, aim, rs)


def _hy_mid(m1, m2, are, aim, hre, him):
    bx, n1, n2, w = are.shape
    cb = min(4, n1)
    blk = pl.BlockSpec((1, cb, n2, w), lambda c, b: (b, c, 0, 0))
    mblk = pl.BlockSpec((cb, 2 * n2, 2 * n2), lambda c, b: (c, 0, 0))
    hblk = pl.BlockSpec((cb, n2, w), lambda c, b: (c, 0, 0))
    out = jax.ShapeDtypeStruct((bx, n1, n2, w), F32)
    return pl.pallas_call(
        _hy_mid_kernel,
        grid=(n1 // cb, bx),
        in_specs=[mblk, mblk, mblk, mblk, blk, blk, hblk, hblk],
        out_specs=[blk, blk],
        out_shape=[out, out],
        compiler_params=_cp("parallel", "parallel"),
        name="hyena_dft_mid",
    )(*m1, *m2, are, aim, hre, him)


def _hy_last_kernel(fh_ref, fl_ref, re_ref, im_ref, p_ref, bias_ref, o_ref):
    _, k, tb, w = p_ref.shape
    spec = jnp.concatenate([re_ref[0].reshape(-1, w), im_ref[0].reshape(-1, w)], axis=0)
    y = _dot3(fh_ref[...], fl_ref[...], spec)
    o_ref[0] = y.reshape(k, tb, w) + p_ref[0] * bias_ref[...]


def _hy_last(f4, bre, bim, p4, bias):
    bx, n1, n2, w = bre.shape
    k = n1 // 2
    tb = HY_TB
    big = pl.BlockSpec((1, n1, tb, w), lambda b, j: (b, 0, j, 0))
    small = pl.BlockSpec((1, k, tb, w), lambda b, j: (b, 0, j, 0))
    return pl.pallas_call(
        _hy_last_kernel,
        grid=(bx, n2 // tb),
        in_specs=[pl.BlockSpec(f4[0].shape, lambda b, j: (0, 0))] * 2 + [big, big, small,
                  pl.BlockSpec((1, w), lambda b, j: (0, 0))],
        out_specs=small,
        out_shape=jax.ShapeDtypeStruct((bx, k, n2, w), F32),
        compiler_params=_cp("parallel", "parallel"),
        name="hyena_dft_last",
    )(*f4, bre, bim, p4, bias)


def _hyena(zb, conv_w, conv_b, filt_params, hy_bias, consts):
    bx, lx, _ = zb.shape
    feat, (f1, m1, m2, f4) = consts
    n = 2 * lx
    n1, n2 = _dft_factors(n)
    w = HY_WIDTH
    filt, ssq = _hy_filter(lx, feat, *filt_params)
    rs = lax.rsqrt(ssq + EPS)
    fre, fim = _hy_stage1(_hi_lo(_per_offset(f1)), filt.reshape(1, n1, n2, w))
    hre, him = _hy_filter_spectrum(m1, fre, fim, rs)
    p, x0 = _hy_prep(zb, conv_w, conv_b)
    k = n1 // 2
    p4 = p.reshape(bx, k, n2, w)
    are, aim = _hy_stage1(_hi_lo(_per_offset(f1[:, :k])), p4)
    bre, bim = _hy_mid(m1, m2, are, aim, hre, him)
    t = _hy_last(f4, bre, bim, p4, hy_bias.reshape(1, w))
    return x0, t.reshape(bx, lx, w)


WA_SHIFT_MAX = 35.0


def _wattn_kernel(sk_ref, q_ref, bias_ref, *rest, banded, fixed):
    if banded:
        kp_ref, kc_ref, kn_ref, kx_ref, vp_ref, vc_ref, vn_ref, vx_ref, o_ref = rest
    else:
        kx_ref, vx_ref, o_ref = rest
    h = pl.program_id(1)
    i = pl.program_id(2)
    last = pl.num_programs(2) - 1
    qb = q_ref.shape[1]
    q = jnp.concatenate([q_ref[0, :, :LANES], q_ref[0, :, LANES:]], axis=0)
    if banded:
        kk = jnp.concatenate([kp_ref[0], kc_ref[0], kn_ref[0], kx_ref[0]], axis=0)
        vv = jnp.concatenate([vp_ref[0], vc_ref[0], vn_ref[0], vx_ref[0]], axis=0)
    else:
        kk, vv = kx_ref[0], vx_ref[0]
    s = lax.dot_general(q, kk, (((1,), (1,)), ((), ())), preferred_element_type=F32) + bias_ref[0]
    if banded:
        c = lax.broadcasted_iota(jnp.int32, (1, kk.shape[0]), 1) - BLOCK
        outside = ((c < 0) & (i == 0)) | ((c >= qb) & (c < qb + BLOCK) & (i == last))
        s = s + jnp.where(outside, NEG_INF, 0.0)
    top = lax.broadcasted_iota(jnp.int32, (2 * qb, 1), 0) < qb
    sk = jnp.where(top, sk_ref[h * WA_GROUP], sk_ref[h * WA_GROUP + 1])
    if fixed:
        p, sink_term = jnp.exp2(s), sk
    else:
        m = jnp.maximum(jnp.max(s, axis=-1, keepdims=True), sk)
        p, sink_term = jnp.exp2(s - m), jnp.exp2(sk - m)
    acc = jnp.dot(p.astype(BF16), vv, preferred_element_type=F32)
    o = (acc * (1.0 / (acc[:, WA_DIM:WA_DIM + 1] + sink_term))).astype(BF16)
    o_ref[0] = jnp.concatenate([o[:qb], o[qb:]], axis=1)


def _window_attention(q, k, v, kx, vx, sink, bound, banded):
    b, lq, _ = q.shape
    cx = kx.shape[1]
    qb = min(256, lq)
    per = qb // BLOCK
    nblk = lq // BLOCK
    side = lambda f: pl.BlockSpec((1, BLOCK, LANES), f)
    prev = side(lambda b_, h, i: (b_, jnp.maximum(i * per - 1, 0), h))
    nxt = side(lambda b_, h, i: (b_, jnp.minimum((i + 1) * per, nblk - 1), h))
    cur = pl.BlockSpec((1, qb, LANES), lambda b_, h, i: (b_, i, h))
    ctx = pl.BlockSpec((1, cx, LANES), lambda b_, h, i: (b_, 0, h))
    qspec = pl.BlockSpec((1, qb, WA_GROUP * LANES), lambda b_, h, i: (b_, i, h))
    mask = jnp.zeros((WA_GROUP * qb, cx), F32)
    if banded:
        r = jnp.arange(WA_GROUP * qb)[:, None] % qb
        c = jnp.arange(qb + 2 * BLOCK)[None, :] - BLOCK
        mask = jnp.concatenate([jnp.where(jnp.abs(r - c) <= BLOCK, 0.0, NEG_INF).astype(F32), mask], axis=1)
        specs, args = [prev, cur, nxt, ctx, prev, cur, nxt, ctx], (k, k, k, kx, v, v, v, vx)
    else:
        specs, args = [ctx, ctx], (kx, vx)
    fixed = bound <= WA_SHIFT_MAX
    shift = jnp.maximum(bound, sink)
    rows = jnp.repeat(shift.reshape(WA_KV_HEADS, WA_GROUP), qb, axis=1)[:, :, None]
    bias = mask[None] - jnp.where(fixed, LOG2E * rows, 0.0)
    sk = jnp.where(fixed, jnp.exp2(LOG2E * (sink - shift)), LOG2E * sink)

    def call(fixed_, *ops):
        return pl.pallas_call(
            functools.partial(_wattn_kernel, banded=banded, fixed=fixed_),
            grid=(b, WA_KV_HEADS, lq // qb),
            in_specs=[pl.BlockSpec(memory_space=pltpu.SMEM), qspec,
                      pl.BlockSpec((1,) + bias.shape[1:], lambda b_, h, i: (h, 0, 0))] + specs,
            out_specs=qspec,
            out_shape=jax.ShapeDtypeStruct((b, lq, WA_HEADS * LANES), BF16),
            compiler_params=_cp("parallel", "parallel", "arbitrary"),
            name="window_attention" if fixed_ else "window_attention_online",
        )(*ops)

    return lax.cond(fixed, functools.partial(call, True), functools.partial(call, False), sk, q, bias, *args)


def _conf_kernel(zp_ref, zc_ref, zn_ref, w_ref, b_ref, lg_ref, lb_ref, o_ref, ext, sh, *, halo):
    i = pl.program_id(1)
    last = pl.num_programs(1) - 1
    tl = zc_ref.shape[1]
    cw = CF_WIDTH

    def glu(z_ref):
        z = z_ref[0].astype(F32)
        return z[:, :cw] * jax.nn.sigmoid(z[:, cw:])

    ext[0:halo] = jnp.where(i == 0, 0.0, glu(zp_ref))
    ext[halo:halo + tl] = glu(zc_ref)
    ext[halo + tl:] = jnp.where(i == last, 0.0, glu(zn_ref))
    for r in range(1, SUBLANES):
        sh[r - 1] = ext[pl.ds(r, sh.shape[1]), :]
    w = w_ref[...]
    u = jnp.zeros((tl, cw), F32) + b_ref[...]
    for j in range(CF_TAPS):
        off = halo - CF_TAPS // 2 + j
        base, r = off // SUBLANES * SUBLANES, off % SUBLANES
        tap = ext[pl.ds(base, tl), :] if r == 0 else sh[r - 1, pl.ds(base, tl), :]
        u = u + tap * w[j:j + 1]
    uc = u - jnp.mean(u, axis=-1, keepdims=True)
    y = uc * lax.rsqrt(jnp.mean(uc * uc, axis=-1, keepdims=True) + EPS) * lg_ref[...] + lb_ref[...]
    o_ref[0] = _silu(y).astype(BF16)


def _conformer(zd, dw_w, dw_b, ln_g, ln_b):
    bx, lx, w = zd.shape
    tl = min(256, lx)
    halo = 2 * SUBLANES
    nh = lx // halo
    per = tl // halo
    row = lambda a: a.reshape(1, -1)
    const = lambda b, i: (0, 0)
    return pl.pallas_call(
        functools.partial(_conf_kernel, halo=halo),
        grid=(bx, lx // tl),
        in_specs=[pl.BlockSpec((1, halo, w), lambda b, i: (b, jnp.maximum(i * per - 1, 0), 0)),
                  pl.BlockSpec((1, tl, w), lambda b, i: (b, i, 0)),
                  pl.BlockSpec((1, halo, w), lambda b, i: (b, jnp.minimum((i + 1) * per, nh - 1), 0)),
                  pl.BlockSpec(dw_w.shape, const)] + [pl.BlockSpec((1, CF_WIDTH), const)] * 3,
        out_specs=pl.BlockSpec((1, tl, CF_WIDTH), lambda b, i: (b, i, 0)),
        out_shape=jax.ShapeDtypeStruct((bx, lx, CF_WIDTH), BF16),
        scratch_shapes=[pltpu.VMEM((tl + 2 * halo, CF_WIDTH), F32),
                        pltpu.VMEM((SUBLANES - 1, tl + 2 * halo - SUBLANES, CF_WIDTH), F32)],
        compiler_params=_cp("parallel", "parallel"),
        name="conformer_conv",
    )(zd, zd, zd, dw_w, row(dw_b), row(ln_g), row(ln_b))


def _merge_kernel(x_ref, m_ref, ya, yb0, ybt, yc, yd, zg_ref, bg_ref, wa, wb, wc, wd, wo_ref, o_ref, *, goff):
    d = x_ref.shape[-1]
    acc = jnp.zeros(x_ref.shape[1:], F32)
    ys = (ya[0], (yb0[0] * ybt[0]).astype(BF16), yc[0], yd[0])
    for i, (y, w) in enumerate(zip(ys, (wa, wb, wc, wd))):
        gate = jax.nn.sigmoid(zg_ref[0, :, i * d:(i + 1) * d].astype(F32) + bg_ref[:, i * d:(i + 1) * d])
        acc = acc + gate * jnp.dot(y, w[...], preferred_element_type=F32)
    out = jnp.dot(acc.astype(BF16), wo_ref[...], preferred_element_type=F32)
    o_ref[0] = x_ref[0] + m_ref[0, :, goff:goff + d] * out


def _merge(x, mod, ys, zg, b_gate, wbs, w_out):
    bx, lx, d = x.shape
    tm = min(512, lx)
    row = lambda b, i: (b, i, 0)
    const = lambda b, i: (0, 0)
    once = lambda a: pl.BlockSpec(a.shape, const, pipeline_mode=pl.Buffered(1))
    return pl.pallas_call(
        functools.partial(_merge_kernel, goff=2 * d),
        grid=(bx, lx // tm),
        in_specs=[pl.BlockSpec((1, tm, d), row), pl.BlockSpec((1, 1, mod.shape[-1]), lambda b, i: (b, 0, 0))]
                 + [pl.BlockSpec((1, tm, y.shape[-1]), row) for y in ys]
                 + [pl.BlockSpec((1, tm, N_BRANCH * d), row), pl.BlockSpec((1, N_BRANCH * d), const)]
                 + [once(w) for w in wbs] + [once(w_out)],
        out_specs=pl.BlockSpec((1, tm, d), row),
        out_shape=jax.ShapeDtypeStruct(x.shape, F32),
        input_output_aliases={0: 0},
        compiler_params=_cp("parallel", "parallel"),
        name="merge",
    )(x, mod, *ys, zg, b_gate.reshape(1, -1), *wbs, w_out)


def _slot_rows(w, group):
    n = w.shape[0] // group
    return jnp.pad(w.reshape(n, group, -1), ((0, 0), (0, LANES - group), (0, 0))).reshape(n * LANES, -1)


def _moe_kernel(x_ref, m_ref, g_ref, wr_ref, br_ref, w1_ref, w3_ref, w2_ref, o_ref, h_sc, gate_sc, acc_sc):
    e = pl.program_id(2)
    d = x_ref.shape[-1]
    tm = x_ref.shape[1]
    lane = lax.broadcasted_iota(jnp.int32, (tm, LANES), 1).astype(F32)

    @pl.when(e == 0)
    def _():
        h = _rms(x_ref[0]) * g_ref[...] * (1.0 + m_ref[0, :, 4 * d:5 * d]) + m_ref[0, :, 3 * d:4 * d]
        h_sc[...] = h.astype(BF16)
        lg = _dot3(*_hi_lo(h), wr_ref[...]) + br_ref[...]
        isg = lane < N_GROUPS
        gmax = jnp.max(jnp.where(isg, lg, NEG_INF), axis=-1, keepdims=True)
        gi = jnp.min(jnp.where(isg & (lg == gmax), lane, LANES), axis=-1, keepdims=True)
        gw = 1.0 / jnp.sum(jnp.where(isg, jnp.exp(lg - gmax), 0.0), axis=-1, keepdims=True)
        lo = N_GROUPS + gi * EXP_PER_GROUP
        ise = (lane >= lo) & (lane < lo + EXP_PER_GROUP)
        le = jnp.where(ise, lg, NEG_INF)
        m1 = jnp.max(le, axis=-1, keepdims=True)
        i1 = jnp.min(jnp.where(ise & (le == m1), lane, LANES), axis=-1, keepdims=True)
        ise2 = ise & (lane != i1)
        le2 = jnp.where(ise2, lg, NEG_INF)
        m2 = jnp.max(le2, axis=-1, keepdims=True)
        i2 = jnp.min(jnp.where(ise2 & (le2 == m2), lane, LANES), axis=-1, keepdims=True)
        r = jnp.exp(m2 - m1)
        wa = gw / (1.0 + r)
        gate_sc[...] = jnp.where(lane == i1, wa, 0.0) + jnp.where(lane == i2, wa * r, 0.0)
        acc_sc[...] = jnp.zeros_like(acc_sc)

    hb = h_sc[...]
    u = (_silu(jnp.dot(hb, w1_ref[0].astype(BF16), preferred_element_type=F32))
         * jnp.dot(hb, w3_ref[0].astype(BF16), preferred_element_type=F32))
    ge = jnp.sum(jnp.where(lane == (e + N_GROUPS).astype(F32), gate_sc[...], 0.0), axis=-1, keepdims=True)
    acc_sc[...] += ge * jnp.dot(u.astype(BF16), w2_ref[0].astype(BF16), preferred_element_type=F32)

    @pl.when(e == pl.num_programs(2) - 1)
    def _():
        o_ref[0] = x_ref[0] + m_ref[0, :, 5 * d:6 * d] * acc_sc[...]


def _moe(x, mod, g, w_router, b_router, w1, w3, w2):
    bx, lx, d = x.shape
    tm = min(1024, lx)
    ne, _, f = w1.shape
    row = lambda b, i, e: (b, i, 0)
    const = lambda b, i, e: (0, 0)
    return pl.pallas_call(
        _moe_kernel,
        grid=(bx, lx // tm, ne),
        in_specs=[pl.BlockSpec((1, tm, d), row), pl.BlockSpec((1, 1, mod.shape[-1]), lambda b, i, e: (b, 0, 0)),
                  pl.BlockSpec((1, d), const), pl.BlockSpec((d, LANES), const), pl.BlockSpec((1, LANES), const),
                  pl.BlockSpec((1, d, f), lambda b, i, e: (e, 0, 0)), pl.BlockSpec((1, d, f), lambda b, i, e: (e, 0, 0)),
                  pl.BlockSpec((1, f, d), lambda b, i, e: (e, 0, 0))],
        out_specs=pl.BlockSpec((1, tm, d), row),
        out_shape=jax.ShapeDtypeStruct(x.shape, F32),
        scratch_shapes=[pltpu.VMEM((tm, d), BF16), pltpu.VMEM((tm, LANES), F32), pltpu.VMEM((tm, d), F32)],
        input_output_aliases={0: 0},
        compiler_params=_cp("parallel", "parallel", "arbitrary"),
        name="moe",
    )(x, mod, g.reshape(1, d), w_router, b_router, w1, w3, w2)


SC_CORES = 2
SC_SUBCORES = 16
SC_CHUNK = 64
MOE_ROWS = 1024


def _sc_gather(table, idx):
    n = idx.shape[0]
    w = table.shape[1]
    per = n // (SC_CORES * SC_SUBCORES)
    assert per * SC_CORES * SC_SUBCORES == n and per % SC_CHUNK == 0
    mesh = plsc.VectorSubcoreMesh(core_axis_name="c", subcore_axis_name="s")

    @functools.partial(
        pl.kernel, mesh=mesh, out_type=jax.ShapeDtypeStruct((n, w), table.dtype),
        scratch_types=[pltpu.VMEM((SC_CHUNK,), jnp.int32), pltpu.VMEM((SC_CHUNK, w), table.dtype),
                       pltpu.SemaphoreType.DMA],
        name="sc_row_gather")
    def gather(table_hbm, idx_hbm, out_hbm, idx_v, rows_v, sem):
        base = (lax.axis_index("s") * SC_CORES + lax.axis_index("c")) * per

        @pl.loop(0, per // SC_CHUNK)
        def _(j):
            off = pl.multiple_of(base + j * SC_CHUNK, SC_CHUNK)
            pltpu.sync_copy(idx_hbm.at[pl.ds(off, SC_CHUNK)], idx_v)
            pltpu.async_copy(table_hbm.at[idx_v], rows_v, sem).wait()
            pltpu.sync_copy(rows_v, out_hbm.at[pl.ds(off, SC_CHUNK)])

    return gather(table, idx)


def _sc_scatter(rows, idx, n_out):
    n, w = rows.shape
    per = n // (SC_CORES * SC_SUBCORES)
    assert per * SC_CORES * SC_SUBCORES == n and per % SC_CHUNK == 0
    mesh = plsc.VectorSubcoreMesh(core_axis_name="c", subcore_axis_name="s")

    @functools.partial(
        pl.kernel, mesh=mesh, out_type=jax.ShapeDtypeStruct((n_out, w), rows.dtype),
        scratch_types=[pltpu.VMEM((SC_CHUNK,), jnp.int32), pltpu.VMEM((SC_CHUNK, w), rows.dtype),
                       pltpu.SemaphoreType.DMA],
        name="sc_row_scatter")
    def scatter(rows_hbm, idx_hbm, out_hbm, idx_v, rows_v, sem):
        base = (lax.axis_index("s") * SC_CORES + lax.axis_index("c")) * per

        @pl.loop(0, per // SC_CHUNK)
        def _(j):
            off = pl.multiple_of(base + j * SC_CHUNK, SC_CHUNK)
            pltpu.sync_copy(idx_hbm.at[pl.ds(off, SC_CHUNK)], idx_v)
            pltpu.sync_copy(rows_hbm.at[pl.ds(off, SC_CHUNK)], rows_v)
            pltpu.async_copy(rows_v, out_hbm.at[idx_v], sem).wait()

    return scatter(rows, idx)


def _route_kernel(x_ref, m_ref, g_ref, wr_ref, br_ref, rows_ref, gi_ref):
    d = x_ref.shape[-1]
    tm = x_ref.shape[1]
    lane = lax.broadcasted_iota(jnp.int32, (tm, LANES), 1).astype(F32)
    h = _rms(x_ref[0]) * g_ref[...] * (1.0 + m_ref[0, :, 4 * d:5 * d]) + m_ref[0, :, 3 * d:4 * d]
    rows_ref[0, :, :d] = h
    lg = _dot3(*_hi_lo(h), wr_ref[...]) + br_ref[...]
    isg = lane < N_GROUPS
    gmax = jnp.max(jnp.where(isg, lg, NEG_INF), axis=-1, keepdims=True)
    gi = jnp.min(jnp.where(isg & (lg == gmax), lane, LANES), axis=-1, keepdims=True)
    gw = 1.0 / jnp.sum(jnp.where(isg, jnp.exp(lg - gmax), 0.0), axis=-1, keepdims=True)
    lo = N_GROUPS + gi * EXP_PER_GROUP
    ise = (lane >= lo) & (lane < lo + EXP_PER_GROUP)
    le = jnp.where(ise, lg, NEG_INF)
    m1 = jnp.max(le, axis=-1, keepdims=True)
    i1 = jnp.min(jnp.where(ise & (le == m1), lane, LANES), axis=-1, keepdims=True)
    ise2 = ise & (lane != i1)
    le2 = jnp.where(ise2, lg, NEG_INF)
    m2 = jnp.max(le2, axis=-1, keepdims=True)
    i2 = jnp.min(jnp.where(ise2 & (le2 == m2), lane, LANES), axis=-1, keepdims=True)
    r = jnp.exp(m2 - m1)
    wa = gw / (1.0 + r)
    rows_ref[0, :, d:] = jnp.where(lane == i1 - lo, wa, 0.0) + jnp.where(lane == i2 - lo, wa * r, 0.0)
    gi_ref[0] = gi.astype(jnp.int32)


def _route(x, mod, g, w_router, b_router):
    bx, lx, d = x.shape
    tm = min(512, lx)
    row = lambda b, i: (b, i, 0)
    const = lambda b, i: (0, 0)
    return pl.pallas_call(
        _route_kernel,
        grid=(bx, lx // tm),
        in_specs=[pl.BlockSpec((1, tm, d), row), pl.BlockSpec((1, 1, mod.shape[-1]), lambda b, i: (b, 0, 0)),
                  pl.BlockSpec((1, d), const), pl.BlockSpec((d, LANES), const), pl.BlockSpec((1, LANES), const)],
        out_specs=[pl.BlockSpec((1, tm, d + LANES), row), pl.BlockSpec((1, tm, 1), row)],
        out_shape=[jax.ShapeDtypeStruct((bx, lx, d + LANES), F32), jax.ShapeDtypeStruct((bx, lx, 1), jnp.int32)],
        compiler_params=_cp("parallel", "parallel"),
        name="moe_route",
    )(x, mod, g.reshape(1, d), w_router, b_router)


def _gmoe_kernel(grp_ref, nv_ref, xs_ref, w1_ref, w3_ref, w2_ref, o_ref, h_sc, acc_sc):
    i = pl.program_id(0)
    e = pl.program_id(1)
    tm, d = h_sc.shape
    valid = lax.broadcasted_iota(jnp.int32, (tm, 1), 0) < nv_ref[i]

    @pl.when(nv_ref[i] > 0)
    def _():
        @pl.when(e == 0)
        def _():
            h_sc[...] = jnp.where(valid, xs_ref[:, :d], 0.0).astype(BF16)
            acc_sc[...] = jnp.zeros_like(acc_sc)

        hb = h_sc[...]
        u = (_silu(jnp.dot(hb, w1_ref[0].astype(BF16), preferred_element_type=F32))
             * jnp.dot(hb, w3_ref[0].astype(BF16), preferred_element_type=F32))
        lane = lax.broadcasted_iota(jnp.int32, (tm, LANES), 1)
        ge = jnp.sum(jnp.where(valid & (lane == e), xs_ref[:, d:], 0.0), axis=-1, keepdims=True)
        acc_sc[...] += ge * jnp.dot(u.astype(BF16), w2_ref[0].astype(BF16), preferred_element_type=F32)

    @pl.when(e == pl.num_programs(1) - 1)
    def _():
        o_ref[...] = jnp.where(nv_ref[i] > 0, acc_sc[...], 0.0)


def _grouped_moe(grp, nv, xs, w1, w3, w2):
    p, dw = xs.shape
    d = dw - LANES
    _, _, f = w1.shape
    wmap = lambda i, e, grp, nv: (grp[i] * EXP_PER_GROUP + e, 0, 0)
    rows = lambda i, e, grp, nv: (i, 0)
    return pl.pallas_call(
        _gmoe_kernel,
        grid_spec=pltpu.PrefetchScalarGridSpec(
            num_scalar_prefetch=2,
            grid=(p // MOE_ROWS, EXP_PER_GROUP),
            in_specs=[pl.BlockSpec((MOE_ROWS, dw), rows),
                      pl.BlockSpec((1, d, f), wmap), pl.BlockSpec((1, d, f), wmap), pl.BlockSpec((1, f, d), wmap)],
            out_specs=pl.BlockSpec((MOE_ROWS, d), rows),
            scratch_shapes=[pltpu.VMEM((MOE_ROWS, d), BF16), pltpu.VMEM((MOE_ROWS, d), F32)]),
        out_shape=jax.ShapeDtypeStruct((p, d), F32),
        compiler_params=_cp("arbitrary", "arbitrary"),
        name="moe_experts",
    )(grp, nv, xs, w1, w3, w2)


def _residual_kernel(x_ref, m_ref, y_ref, o_ref):
    d = x_ref.shape[-1]
    o_ref[0] = x_ref[0] + m_ref[0, :, 5 * d:6 * d] * y_ref[0]


def _residual(x, mod, y):
    bx, lx, d = x.shape
    tm = min(1024, lx)
    row = lambda b, i: (b, i, 0)
    return pl.pallas_call(
        _residual_kernel,
        grid=(bx, lx // tm),
        in_specs=[pl.BlockSpec((1, tm, d), row), pl.BlockSpec((1, 1, mod.shape[-1]), lambda b, i: (b, 0, 0)),
                  pl.BlockSpec((1, tm, d), row)],
        out_specs=pl.BlockSpec((1, tm, d), row),
        out_shape=jax.ShapeDtypeStruct(x.shape, F32),
        input_output_aliases={0: 0},
        compiler_params=_cp("parallel", "parallel"),
        name="moe_residual",
    )(x, mod, y)


def _moe_sorted(x, mod, g, w_router, b_router, w1, w3, w2):
    bx, lx, d = x.shape
    t = bx * lx
    rows, gi = _route(x, mod, g, w_router, b_router)
    gi = gi.reshape(t)
    onehot = (gi[:, None] == jnp.arange(N_GROUPS, dtype=jnp.int32)[None, :]).astype(jnp.int32)
    csum = jnp.cumsum(onehot, axis=0)
    counts = csum[-1]
    rank = jnp.take_along_axis(csum, gi[:, None], axis=1)[:, 0] - 1
    padded = (counts + MOE_ROWS - 1) // MOE_ROWS * MOE_ROWS
    pend = jnp.cumsum(padded)
    pstart = pend - padded
    pos = (pstart[gi] + rank).astype(jnp.int32)
    p = t + N_GROUPS * MOE_ROWS
    bstart = jnp.arange(p // MOE_ROWS, dtype=jnp.int32) * MOE_ROWS
    grp = jnp.minimum(jnp.searchsorted(pend, bstart, side="right"), N_GROUPS - 1).astype(jnp.int32)
    nv = jnp.clip(pstart[grp] + counts[grp] - bstart, 0, MOE_ROWS).astype(jnp.int32)
    xs = _sc_scatter(rows.reshape(t, d + LANES), pos, p)
    ys = _grouped_moe(grp, nv, xs, w1, w3, w2)
    yt = _sc_gather(ys, pos)
    return _residual(x, mod, yt.reshape(bx, lx, d))


def kernel(x, c, ctx, c_ctx, w_mod, b_mod, norm1_g, norm2_g, w_in, b_gate, da_qn, da_kn, da_lam, da_subln, hy_conv_w, hy_conv_b, hf_w1, hf_b1, hf_w2, hf_b2, hf_w3, hf_b3, hf_w4, hf_freq, hy_bias, wa_qn, wa_kn, wa_sink, cf_dw_w, cf_dw_b, cf_ln_g, cf_ln_b, w_branch, w_out, w_rg, b_rg, w_re, b_re, w1, w3, w2):
    b, s, d = x.shape
    cl = ctx.shape[1]
    depth = w_mod.shape[0]
    assert s % 256 == 0 and cl % 256 == 0 and s % GRID_W == 0

    nrow = -(-(b + 1) // SUBLANES) * SUBLANES
    crows = jnp.zeros((nrow, d), F32).at[:b].set(c).at[b].set(c_ctx)
    mods = _mod_vectors(crows, w_mod, b_mod)

    aw, qw, kw = DA_HEADS * DA_DIM, WA_HEADS * WA_DIM, WA_KV_HEADS * WA_DIM
    tab_lat = (*_rope_tables(s, DA_DIM, DA_HEADS), *_rope_tables(s, WA_DIM, WA_HEADS), *_rope_tables(s, WA_DIM, WA_KV_HEADS))
    tab_ctx = (*_unit_tables(cl, aw), *_unit_tables(cl, qw), *_unit_tables(cl, kw))
    hy_lat = (_hy_feat(s), _dft_tables(2 * s))
    hy_ctx = (_hy_feat(cl), _dft_tables(2 * cl))
    gms = (_group_ones(aw, DA_DIM), _group_ones(qw, WA_DIM), _group_ones(kw, WA_DIM))
    qvec = _slot_fill(DA_DIM, DA_DIM + 1, 1.0)
    vvec = _slot_fill(DA_VDIM, DA_VDIM + DA_ONES, 1.0)

    w_in_b = w_in.astype(BF16)
    xc = ctx
    for l in range(depth):
        last = l == depth - 1
        lam_init = 0.8 - 0.6 * math.exp(-0.3 * l)
        mod_x = mods[l, :b][:, None, :]
        mod_c = jnp.broadcast_to(mods[l, b][None, None, :], (b, 1, mods.shape[-1]))
        shift = 1.02 * LOG2E * DA_DIM ** 0.5 * jnp.max(jnp.abs(da_qn[l])) * jnp.max(jnp.abs(da_kn[l]))
        fixed = shift <= DA_SHIFT_MAX
        kvec = _slot_fill(DA_DIM, DA_DIM + 1, jnp.where(fixed, -shift, 0.0))
        tile = lambda a, n: jnp.tile(a, n).reshape(1, -1)
        consts = (*gms, tile(da_qn[l], DA_HEADS), tile(da_kn[l], DA_HEADS), tile(wa_qn[l], WA_HEADS),
                  tile(wa_kn[l], WA_KV_HEADS), qvec, kvec, vvec)
        q1, q2, k1, k2, v, zb, qc, kc, vc, zd, zg = _in_proj(x, mod_x, norm1_g[l], w_in_b, l, tab_lat, consts)
        q1x, q2x, k1x, k2x, vx, zbx, qcx, kcx, vcx, zdx, zgx = _in_proj(xc, mod_c, norm1_g[l], w_in_b, l, tab_ctx, consts)
        filt_params = (hf_w1[l], hf_b1[l], hf_w2[l], hf_b2[l], hf_w3[l], hf_b3[l], hf_w4[l], hf_freq[l])
        wb4 = w_branch[l].astype(BF16)
        wbs = (_slot_rows(wb4[0], DA_VDIM), wb4[1], _slot_rows(wb4[2], WA_DIM), wb4[3])
        wo = w_out[l].astype(BF16)

        ya = _diff_attention(q1, q2, [(k1, k2, v), (k1x, k2x, vx)], fixed, da_lam[l], da_subln[l], lam_init)
        yb = _hyena(zb, hy_conv_w[l], hy_conv_b[l], filt_params, hy_bias[l], hy_lat)
        wbound = 1.02 * WA_DIM ** 0.5 * jnp.max(jnp.abs(wa_qn[l])) * jnp.max(jnp.abs(wa_kn[l]))
        yc_ = _window_attention(qc, kc, vc, kcx, vcx, wa_sink[l], wbound, True)
        yd = _conformer(zd, cf_dw_w[l], cf_dw_b[l], cf_ln_g[l], cf_ln_b[l])
        x = _merge(x, mod_x, (ya, *yb, yc_, yd), zg, b_gate[l], wbs, wo)

        w_router = jnp.pad(jnp.concatenate([w_rg[l], w_re[l]], axis=1), ((0, 0), (0, LANES - N_GROUPS - N_EXPERTS)))
        b_router = jnp.pad(jnp.concatenate([b_rg[l], b_re[l]]), (0, LANES - N_GROUPS - N_EXPERTS)).reshape(1, LANES)
        ew = (w1[l], w3[l], w2[l])

        if not last:
            yca = _diff_attention(q1x, q2x, [(k1x, k2x, vx)], fixed, da_lam[l], da_subln[l], lam_init)
            ycb = _hyena(zbx, hy_conv_w[l], hy_conv_b[l], filt_params, hy_bias[l], hy_ctx)
            ycc = _window_attention(qcx, kcx, vcx, kcx, vcx, wa_sink[l], wbound, False)
            ycd = _conformer(zdx, cf_dw_w[l], cf_dw_b[l], cf_ln_g[l], cf_ln_b[l])
            xc = _merge(xc, mod_c, (yca, *ycb, ycc, ycd), zgx, b_gate[l], wbs, wo)
            xc = _moe(xc.reshape(1, b * cl, d), mod_c[:1], norm2_g[l], w_router, b_router, *ew).reshape(b, cl, d)
        x = _moe_sorted(x, mod_x, norm2_g[l], w_router, b_router, *ew)
    return x
```

```python
import functools
import math

import jax
import jax.numpy as jnp
from jax import lax
from jax.experimental import pallas as pl
from jax.experimental.pallas import tpu as pltpu
from jax.experimental.pallas import tpu_sc as plsc

F32 = jnp.float32
BF16 = jnp.bfloat16
HI = lax.Precision.HIGHEST

GRID_W = 64
BLOCK = 128
ROPE_BASE = 10000.0
EPS = 1e-6
NEG_INF = -1e30

DA_HEADS = 4
DA_DIM = 32
DA_VDIM = 64
HY_WIDTH = 256
HY_BANDS = 16
HY_FF = 64
HY_SHIFT = 0.05
HY_FAST_DECAY = 0.3
HY_SLOW_DECAY = 1.5
HY_TARGET = 1e-2
WA_HEADS = 4
WA_KV_HEADS = 2
WA_GROUP = 2
WA_DIM = 64
CF_WIDTH = 256
CF_TAPS = 31
N_BRANCH = 4
BRANCH_W = 256
N_GROUPS = 4
EXP_PER_GROUP = 4
N_EXPERTS = 16

W_A = 4 * DA_HEADS * DA_DIM + DA_HEADS * DA_VDIM
W_B = 3 * HY_WIDTH
W_C = (WA_HEADS + 2 * WA_KV_HEADS) * WA_DIM
W_D = 2 * CF_WIDTH
OFF_B = W_A
OFF_C = OFF_B + W_B
OFF_D = OFF_C + W_C
OFF_G = OFF_D + W_D

LOG2E = math.log2(math.e)
LANES = 128
SUBLANES = 8
VMEM_LIMIT = 56 * 1024 * 1024

DA_ONES = 16
DA_SHIFT_MAX = 50.0


def _cp(*sem):
    return pltpu.CompilerParams(dimension_semantics=sem, vmem_limit_bytes=VMEM_LIMIT)


def _rms(xf):
    return xf * lax.rsqrt(jnp.mean(xf * xf, axis=-1, keepdims=True) + EPS)


def _silu(x):
    return x * jax.nn.sigmoid(x)


def _mod_kernel(c_ref, w_ref, b_ref, o_ref):
    s = _silu(c_ref[...])
    o_ref[0] = jnp.dot(s, w_ref[0], precision=HI, preferred_element_type=F32) + b_ref[0]


def _mod_vectors(crows, w_mod, b_mod):
    depth, d, n = w_mod.shape
    r = crows.shape[0]
    tn = 1536
    return pl.pallas_call(
        _mod_kernel,
        grid=(depth, n // tn),
        in_specs=[pl.BlockSpec((r, d), lambda l, j: (0, 0)),
                  pl.BlockSpec((1, d, tn), lambda l, j: (l, 0, j)),
                  pl.BlockSpec((1, 1, tn), lambda l, j: (l, 0, j))],
        out_specs=pl.BlockSpec((1, r, tn), lambda l, j: (l, 0, j)),
        out_shape=jax.ShapeDtypeStruct((depth, r, n), F32),
        compiler_params=_cp("arbitrary", "arbitrary"),
        name="mod_vectors",
    )(crows, w_mod, b_mod.reshape(depth, 1, n))


def _rope_tables(s, d, reps):
    rows = s // GRID_W
    row = jnp.repeat(jnp.arange(rows, dtype=F32), GRID_W)
    col = jnp.tile(jnp.arange(GRID_W, dtype=F32), rows)
    qd = d // 4
    inv = ROPE_BASE ** (-jnp.arange(qd, dtype=F32) / qd)
    ar = row[:, None] * inv[None, :]
    ac = col[:, None] * inv[None, :]
    z = jnp.zeros_like(ar)
    cos = jnp.concatenate([jnp.cos(ar), jnp.cos(ar), jnp.cos(ac), jnp.cos(ac)], axis=-1)
    sin_up = jnp.concatenate([-jnp.sin(ar), z, -jnp.sin(ac), z], axis=-1)
    sin_dn = jnp.concatenate([z, jnp.sin(ar), z, jnp.sin(ac)], axis=-1)
    t = lambda a: jnp.tile(a, (1, reps))
    return t(cos), t(sin_up), t(sin_dn)


def _unit_tables(s, w):
    return jnp.ones((s, w), F32), jnp.zeros((s, w), F32), jnp.zeros((s, w), F32)


def _group_ones(width, group):
    i = jnp.arange(width) // group
    return (i[:, None] == i[None, :]).astype(BF16)


def _hi_lo(a):
    hi = a.astype(BF16)
    return hi, (a - hi.astype(F32)).astype(BF16)


def _slot_fill(lo, hi, value):
    j = jnp.arange(LANES)
    return jnp.where((j >= lo) & (j < hi), value, 0.0).astype(F32).reshape(1, LANES)


def _store_slots(o_ref, y, group, fill):
    per = LANES // group
    lane = lax.broadcasted_iota(jnp.int32, (y.shape[0], LANES), 1)
    for h in range(y.shape[1] // group):
        blk = y[:, h // per * LANES:(h // per + 1) * LANES]
        if h % per:
            blk = pltpu.roll(blk, LANES - h % per * group, 1)
        o_ref[0, :, h * LANES:(h + 1) * LANES] = jnp.where(lane < group, blk, fill).astype(BF16)


def _norm_rope(x, gmat, gain, cos, sup, sdn, group, qd):
    w = x.shape[-1]
    sh, sl = _hi_lo(x * x)
    ss = (jnp.dot(sh, gmat, preferred_element_type=F32) + jnp.dot(sl, gmat, preferred_element_type=F32)) * (1.0 / group)
    xn = x * lax.rsqrt(ss + EPS) * gain
    return xn * cos + pltpu.roll(xn, w - qd, 1) * sup + pltpu.roll(xn, qd, 1) * sdn


def _inproj_kernel(x_ref, m_ref, g_ref, w_ref,
                   ca, ua, da, cq, uq, dq, ck, uk, dk, gma, gmq, gmk, qna, kna, qnc, knc,
                   qvec, kvec, vvec,
                   q1o, q2o, k1o, k2o, vo, zbo, qco, kco, vco, zdo, zgo):
    d = x_ref.shape[-1]
    x = x_ref[0]
    shift = m_ref[0, :, 0:d]
    scale = m_ref[0, :, d:2 * d]
    h = (_rms(x) * g_ref[...] * (1.0 + scale) + shift).astype(BF16)

    za = jnp.dot(h, w_ref[0, :, 0:OFF_B], preferred_element_type=F32)
    hw = DA_HEADS * DA_DIM
    cos, sup, sdn, gm = ca[...], ua[...], da[...], gma[...]
    qscale = DA_DIM ** -0.5 * LOG2E
    for t, (o, gain, sc, vec) in enumerate(((q1o, qna, qscale, qvec), (q2o, qna, qscale, qvec),
                                            (k1o, kna, 1.0, kvec), (k2o, kna, 1.0, kvec))):
        y = _norm_rope(za[:, t * hw:(t + 1) * hw], gm, gain[...], cos, sup, sdn, DA_DIM, DA_DIM // 4) * sc
        _store_slots(o, y, DA_DIM, vec[...])
    _store_slots(vo, za[:, 4 * hw:], DA_VDIM, vvec[...])

    zbo[0] = jnp.dot(h, w_ref[0, :, OFF_B:OFF_C], preferred_element_type=F32).astype(BF16)

    zc = jnp.dot(h, w_ref[0, :, OFF_C:OFF_D], preferred_element_type=F32)
    qw = WA_HEADS * WA_DIM
    kw = WA_KV_HEADS * WA_DIM
    y = _norm_rope(zc[:, 0:qw], gmq[...], qnc[...], cq[...], uq[...], dq[...], WA_DIM, WA_DIM // 4) * (WA_DIM ** -0.5 * LOG2E)
    _store_slots(qco, y, WA_DIM, 0.0)
    y = _norm_rope(zc[:, qw:qw + kw], gmk[...], knc[...], ck[...], uk[...], dk[...], WA_DIM, WA_DIM // 4)
    _store_slots(kco, y, WA_DIM, 0.0)
    _store_slots(vco, zc[:, qw + kw:], WA_DIM, vvec[...])

    zdo[0] = jnp.dot(h, w_ref[0, :, OFF_D:OFF_G], preferred_element_type=F32).astype(BF16)
    for k in range(N_BRANCH):
        zgo[0, :, k * d:(k + 1) * d] = jnp.dot(h, w_ref[0, :, OFF_G + k * d:OFF_G + (k + 1) * d],
                                               preferred_element_type=F32).astype(BF16)


def _in_proj(x, mod, g, w_all, layer, tables, consts):
    bx, lx, d = x.shape
    tm = min(512, lx)
    const = lambda b, i: (0, 0)
    row = lambda b, i: (b, i, 0)
    once = lambda a: pl.BlockSpec(a.shape, const, pipeline_mode=pl.Buffered(1))
    widths = [DA_HEADS * LANES] * 5 + [W_B, WA_HEADS * LANES, WA_KV_HEADS * LANES, WA_KV_HEADS * LANES, W_D, N_BRANCH * d]
    return pl.pallas_call(
        _inproj_kernel,
        grid=(bx, lx // tm),
        in_specs=[pl.BlockSpec((1, tm, d), row), pl.BlockSpec((1, 1, mod.shape[-1]), lambda b, i: (b, 0, 0)),
                  pl.BlockSpec((1, d), const)]
                 + [pl.BlockSpec((1,) + w_all.shape[1:], lambda b, i: (layer, 0, 0), pipeline_mode=pl.Buffered(1))]
                 + [pl.BlockSpec((tm, t.shape[1]), lambda b, i: (i, 0)) for t in tables]
                 + [once(c) for c in consts],
        out_specs=[pl.BlockSpec((1, tm, w), row) for w in widths],
        out_shape=[jax.ShapeDtypeStruct((bx, lx, w), BF16) for w in widths],
        compiler_params=_cp("parallel", "parallel"),
        name="in_proj",
    )(x, mod, g.reshape(1, d), w_all, *tables, *consts)


def _da_lambda(lam_ref, lam_init):
    lv = lam_ref[...]
    return (jnp.exp(jnp.sum(lv[0:1] * lv[1:2], keepdims=True)) - jnp.exp(jnp.sum(lv[2:3] * lv[3:4], keepdims=True))
            + lam_init)


def _dattn_kernel(lam_ref, sg_ref, q1_ref, q2_ref, *rest, lam_init, online, nsrc, tk_max):
    srcs = [rest[3 * s:3 * s + 3] for s in range(nsrc)]
    o_ref, acc1, acc2 = rest[3 * nsrc:]
    dn = (((1,), (1,)), ((), ()))
    q1 = q1_ref[0]
    q2 = q2_ref[0]
    tq = q1.shape[0]
    acc1[...] = jnp.zeros_like(acc1)
    acc2[...] = jnp.zeros_like(acc2)
    carry = (jnp.full((tq, 1), NEG_INF, F32),) * 2 if online else 0

    for k1_ref, k2_ref, v_ref in srcs:
        tk = min(tk_max, k1_ref.shape[1])

        def body(j, c, k1_ref=k1_ref, k2_ref=k2_ref, v_ref=v_ref, tk=tk):
            rows = pl.ds(pl.multiple_of(j * tk, tk), tk)
            vc = v_ref[0, rows, :]
            s1 = lax.dot_general(q1, k1_ref[0, rows, :], dn, preferred_element_type=F32)
            s2 = lax.dot_general(q2, k2_ref[0, rows, :], dn, preferred_element_type=F32)
            if online:
                m1, m2 = c
                n1 = jnp.maximum(m1, jnp.max(s1, axis=-1, keepdims=True))
                n2 = jnp.maximum(m2, jnp.max(s2, axis=-1, keepdims=True))
                acc1[...] = jnp.exp2(m1 - n1) * acc1[...] + jnp.dot(jnp.exp2(s1 - n1).astype(BF16), vc,
                                                                    preferred_element_type=F32)
                acc2[...] = jnp.exp2(m2 - n2) * acc2[...] + jnp.dot(jnp.exp2(s2 - n2).astype(BF16), vc,
                                                                    preferred_element_type=F32)
                return n1, n2
            acc1[...] += jnp.dot(jnp.exp2(s1).astype(BF16), vc, preferred_element_type=F32)
            acc2[...] += jnp.dot(jnp.exp2(s2).astype(BF16), vc, preferred_element_type=F32)
            return c

        carry = lax.fori_loop(0, k1_ref.shape[1] // tk, body, carry)

    dv = DA_VDIM
    a1 = acc1[...]
    a2 = acc2[...]
    lam = _da_lambda(lam_ref, lam_init)
    o = a1 * (1.0 / a1[:, dv:dv + 1]) - a2 * (lam / a2[:, dv:dv + 1])
    o = jnp.where(lax.broadcasted_iota(jnp.int32, o.shape, 1) < dv, o, 0.0)
    o = o * lax.rsqrt(jnp.sum(o * o, axis=-1, keepdims=True) * (1.0 / dv) + EPS)
    o_ref[0] = (o * (sg_ref[...] * (1.0 - lam_init))).astype(BF16)


def _diff_attention(q1, q2, srcs, fixed, lam_p, subln, lam_init):
    b, lq, _ = q1.shape
    h = DA_HEADS
    const = lambda b_, h_, i: (0, 0)
    sg = jnp.pad(subln, (0, LANES - DA_VDIM)).reshape(1, LANES)
    flat = [a for src in srcs for a in src]

    def call(online, *args):
        tq = min(256 if online else 2048, lq)
        qs = pl.BlockSpec((1, tq, LANES), lambda b_, h_, i: (b_, i, h_))
        return pl.pallas_call(
            functools.partial(_dattn_kernel, lam_init=lam_init, online=online, nsrc=len(srcs),
                              tk_max=256 if online else 512),
            grid=(b, h, lq // tq),
            in_specs=[pl.BlockSpec(lam_p.shape, const), pl.BlockSpec((1, LANES), const), qs, qs]
                     + [pl.BlockSpec((1, a.shape[1], LANES), lambda b_, h_, i: (b_, 0, h_)) for a in flat],
            out_specs=qs,
            out_shape=jax.ShapeDtypeStruct((b, lq, h * LANES), BF16),
            scratch_shapes=[pltpu.VMEM((tq, LANES), F32)] * 2,
            compiler_params=_cp("parallel", "parallel", "arbitrary"),
            name="diff_attention_online" if online else "diff_attention",
        )(*args)

    return lax.cond(fixed, functools.partial(call, False), functools.partial(call, True),
                    lam_p, sg, q1, q2, *flat)


def _hy_prep_kernel(zp_ref, zc_ref, zn_ref, w_ref, b_ref, p_ref, x0_ref, ext):
    i = pl.program_id(1)
    last = pl.num_programs(1) - 1
    tl = zc_ref.shape[1]
    h = 2 * SUBLANES
    ext[0:h] = jnp.where(i == 0, 0.0, zp_ref[0].astype(F32))
    ext[h:h + tl] = zc_ref[0].astype(F32)
    ext[h + tl:] = jnp.where(i == last, 0.0, zn_ref[0].astype(F32))
    w = w_ref[...]
    u = (ext[pl.ds(h - 1, tl), :] * w[0:1] + ext[pl.ds(h, tl), :] * w[1:2] + ext[pl.ds(h + 1, tl), :] * w[2:3]
         + b_ref[...])
    hw = HY_WIDTH
    x0_ref[0] = u[:, 0:hw]
    p_ref[0] = u[:, 2 * hw:3 * hw] * u[:, hw:2 * hw]


def _hy_prep(zb, conv_w, conv_b):
    bx, lx, w = zb.shape
    tl = min(256, lx)
    h = 2 * SUBLANES
    nh = lx // h
    per = tl // h
    out = jax.ShapeDtypeStruct((bx, lx, HY_WIDTH), F32)
    return pl.pallas_call(
        _hy_prep_kernel,
        grid=(bx, lx // tl),
        in_specs=[pl.BlockSpec((1, h, w), lambda b, i: (b, jnp.maximum(i * per - 1, 0), 0)),
                  pl.BlockSpec((1, tl, w), lambda b, i: (b, i, 0)),
                  pl.BlockSpec((1, h, w), lambda b, i: (b, jnp.minimum((i + 1) * per, nh - 1), 0)),
                  pl.BlockSpec(conv_w.shape, lambda b, i: (0, 0)),
                  pl.BlockSpec((1, w), lambda b, i: (0, 0))],
        out_specs=[pl.BlockSpec((1, tl, HY_WIDTH), lambda b, i: (b, i, 0))] * 2,
        out_shape=[out, out],
        scratch_shapes=[pltpu.VMEM((tl + 2 * h, w), F32)],
        compiler_params=_cp("parallel", "parallel"),
        name="hyena_prep",
    )(zb, zb, zb, conv_w, conv_b.reshape(1, w))


def _hy_filter_kernel(feat_ref, w1, b1, w2, b2, w3, b3, w4, fr_ref, dl_ref, filt_ref, ssq_ref, *, s):
    i = pl.program_id(0)
    tr = feat_ref.shape[0]
    feat = feat_ref[...]
    fr = fr_ref[...]
    dot = lambda a, w: _dot3(*_hi_lo(a), w[...])
    a = jnp.sin(fr * (dot(feat, w1) + b1[...]))
    a = jnp.sin(fr * (dot(a, w2) + b2[...]))
    a = jnp.sin(fr * (dot(a, w3) + b3[...]))
    coef = dot(a, w4)
    n = i * tr + lax.broadcasted_iota(jnp.int32, (tr, 1), 0)
    window = jnp.exp(-feat[:, 0:1] * dl_ref[...]) + HY_SHIFT
    half = jnp.where(n < s, coef[:, :HY_WIDTH], coef[:, HY_WIDTH:])
    filt = jnp.where(n == s, 0.0, half * window)
    filt_ref[...] = filt

    @pl.when(i == 0)
    def _():
        ssq_ref[...] = jnp.zeros_like(ssq_ref)

    ssq_ref[...] += jnp.sum(filt * filt, axis=0, keepdims=True)


def _hy_feat(s):
    t = jnp.linspace(0.0, 1.0, s, dtype=F32)[:, None]
    w = (2.0 * math.pi / s) * jnp.arange(s, dtype=F32)[:, None]
    bands = jnp.linspace(1e-4, HY_BANDS - 1, HY_BANDS, dtype=F32)[None, :]
    feat = jnp.concatenate([t, jnp.cos(w * bands), jnp.sin(w * bands)], axis=-1)
    feat = jnp.concatenate([feat, feat[:1], feat[:0:-1]], axis=0)
    return jnp.pad(feat, ((0, 0), (0, LANES - feat.shape[1])))


def _hy_filter(s, feat, w1, b1, w2, b2, w3, b3, w4, freq):
    n = 2 * s
    tr = min(512, n)
    deltas = jnp.abs(jnp.linspace(math.log(HY_TARGET) / HY_FAST_DECAY, math.log(HY_TARGET) / HY_SLOW_DECAY,
                                  HY_WIDTH, dtype=F32)).reshape(1, HY_WIDTH)
    w1p = jnp.pad(w1, ((0, LANES - w1.shape[0]), (0, 0)))
    row = lambda a: a.reshape(1, -1)
    args = (feat, w1p, row(b1), w2, row(b2), w3, row(b3), w4, row(freq), deltas)
    const = lambda i: (0, 0)
    return pl.pallas_call(
        functools.partial(_hy_filter_kernel, s=s),
        grid=(n // tr,),
        in_specs=[pl.BlockSpec((tr, LANES), lambda i: (i, 0))] + [pl.BlockSpec(a.shape, const) for a in args[1:]],
        out_specs=[pl.BlockSpec((tr, HY_WIDTH), lambda i: (i, 0)), pl.BlockSpec((1, HY_WIDTH), const)],
        out_shape=[jax.ShapeDtypeStruct((n, HY_WIDTH), F32), jax.ShapeDtypeStruct((1, HY_WIDTH), F32)],
        compiler_params=_cp("arbitrary"),
        name="hyena_filter",
    )(*args)


def _dft_factors(n):
    lg = n.bit_length() - 1
    assert 1 << lg == n
    n1 = 1 << (lg // 2)
    return n1, n // n1


def _dft_tables(n):
    n1, n2 = _dft_factors(n)
    ia = jnp.arange(n1, dtype=jnp.int32)
    ang1 = (2.0 * math.pi / n1) * ((ia[:, None] * ia[None, :]) % n1).astype(F32)
    f1 = jnp.concatenate([jnp.cos(ang1), -jnp.sin(ang1)], axis=0)
    c = jnp.arange(n1, dtype=jnp.int32)[:, None, None]
    d = jnp.arange(n2, dtype=jnp.int32)[None, :, None]
    b = jnp.arange(n2, dtype=jnp.int32)[None, None, :]
    ang = (2.0 * math.pi / n) * ((b * (c + n1 * d)) % n).astype(F32)
    re, im = jnp.cos(ang), -jnp.sin(ang)
    m1 = jnp.concatenate([jnp.concatenate([re, -im], axis=2), jnp.concatenate([im, re], axis=2)], axis=1)
    m2 = jnp.swapaxes(m1, 1, 2)
    ang4 = ang1[: n1 // 2]
    f4 = jnp.concatenate([jnp.cos(ang4), -jnp.sin(ang4)], axis=1) * (1.0 / n)
    return f1, _hi_lo(m1), _hi_lo(m2), _hi_lo(_per_offset(f4))


def _dot3(mh, ml, a):
    ah, al = _hi_lo(a)
    return (jnp.dot(mh, ah, preferred_element_type=F32) + jnp.dot(mh, al, preferred_element_type=F32)
            + jnp.dot(ml, ah, preferred_element_type=F32))


HY_TB = SUBLANES


def _per_offset(f):
    return jnp.kron(f, jnp.eye(HY_TB, dtype=f.dtype))


def _hy_stage1_kernel(fh_ref, fl_ref, x_ref, re_ref, im_ref):
    _, n1, tb, w = re_ref.shape
    y = _dot3(fh_ref[...], fl_ref[...], x_ref[0].reshape(-1, w))
    re_ref[0] = y[:n1 * tb].reshape(n1, tb, w)
    im_ref[0] = y[n1 * tb:].reshape(n1, tb, w)


def _hy_stage1(f, x4):
    bx, k, n2, w = x4.shape
    tb = HY_TB
    n1 = f[0].shape[0] // (2 * tb)
    out = jax.ShapeDtypeStruct((bx, n1, n2, w), F32)
    return pl.pallas_call(
        _hy_stage1_kernel,
        grid=(bx, n2 // tb),
        in_specs=[pl.BlockSpec(f[0].shape, lambda b, j: (0, 0))] * 2
                 + [pl.BlockSpec((1, k, tb, w), lambda b, j: (b, 0, j, 0))],
        out_specs=[pl.BlockSpec((1, n1, tb, w), lambda b, j: (b, 0, j, 0))] * 2,
        out_shape=[out, out],
        compiler_params=_cp("parallel", "parallel"),
        name="hyena_dft_stage1",
    )(*f, x4)


def _hy_mid_kernel(m1h_ref, m1l_ref, m2h_ref, m2l_ref, re_ref, im_ref, hre_ref, him_ref, ore_ref, oim_ref):
    n2 = re_ref.shape[2]
    for c in range(re_ref.shape[1]):
        a = jnp.concatenate([re_ref[0, c], im_ref[0, c]], axis=0)
        x = _dot3(m1h_ref[c], m1l_ref[c], a)
        xre, xim = x[:n2], x[n2:]
        hre, him = hre_ref[c], him_ref[c]
        y = jnp.concatenate([xre * hre - xim * him, xre * him + xim * hre], axis=0)
        bb = _dot3(m2h_ref[c], m2l_ref[c], y)
        ore_ref[0, c] = bb[:n2]
        oim_ref[0, c] = bb[n2:]


def _hy_spec_kernel(m1h_ref, m1l_ref, re_ref, im_ref, rs_ref, ore_ref, oim_ref):
    n2 = re_ref.shape[2]
    for c in range(re_ref.shape[1]):
        a = jnp.concatenate([re_ref[0, c], im_ref[0, c]], axis=0)
        x = _dot3(m1h_ref[c], m1l_ref[c], a) * rs_ref[...]
        ore_ref[c] = x[:n2]
        oim_ref[c] = x[n2:]


def _hy_filter_spectrum(m1, are, aim, rs):
    _, n1, n2, w = are.shape
    cb = min(4, n1)
    blk = pl.BlockSpec((1, cb, n2, w), lambda c: (0, c, 0, 0))
    mblk = pl.BlockSpec((cb, 2 * n2, 2 * n2), lambda c: (c, 0, 0))
    oblk = pl.BlockSpec((cb, n2, w), lambda c: (c, 0, 0))
    out = jax.ShapeDtypeStruct((n1, n2, w), F32)
    return pl.pallas_call(
        _hy_spec_kernel,
        grid=(n1 // cb,),
        in_specs=[mblk, mblk, blk, blk, pl.BlockSpec((1, w), lambda c: (0, 0))],
        out_specs=[oblk, oblk],
        out_shape=[out, out],
        compiler_params=_cp("parallel"),
        name="hyena_filter_spectrum",
    )(*m1, are, aim, rs)


def _hy_mid(m1, m2, are, aim, hre, him):
    bx, n1, n2, w = are.shape
    cb = min(4, n1)
    blk = pl.BlockSpec((1, cb, n2, w), lambda c, b: (b, c, 0, 0))
    mblk = pl.BlockSpec((cb, 2 * n2, 2 * n2), lambda c, b: (c, 0, 0))
    hblk = pl.BlockSpec((cb, n2, w), lambda c, b: (c, 0, 0))
    out = jax.ShapeDtypeStruct((bx, n1, n2, w), F32)
    return pl.pallas_call(
        _hy_mid_kernel,
        grid=(n1 // cb, bx),
        in_specs=[mblk, mblk, mblk, mblk, blk, blk, hblk, hblk],
        out_specs=[blk, blk],
        out_shape=[out, out],
        compiler_params=_cp("parallel", "parallel"),
        name="hyena_dft_mid",
    )(*m1, *m2, are, aim, hre, him)


def _hy_last_kernel(fh_ref, fl_ref, re_ref, im_ref, p_ref, bias_ref, o_ref):
    _, k, tb, w = p_ref.shape
    spec = jnp.concatenate([re_ref[0].reshape(-1, w), im_ref[0].reshape(-1, w)], axis=0)
    y = _dot3(fh_ref[...], fl_ref[...], spec)
    o_ref[0] = y.reshape(k, tb, w) + p_ref[0] * bias_ref[...]


def _hy_last(f4, bre, bim, p4, bias):
    bx, n1, n2, w = bre.shape
    k = n1 // 2
    tb = HY_TB
    big = pl.BlockSpec((1, n1, tb, w), lambda b, j: (b, 0, j, 0))
    small = pl.BlockSpec((1, k, tb, w), lambda b, j: (b, 0, j, 0))
    return pl.pallas_call(
        _hy_last_kernel,
        grid=(bx, n2 // tb),
        in_specs=[pl.BlockSpec(f4[0].shape, lambda b, j: (0, 0))] * 2 + [big, big, small,
                  pl.BlockSpec((1, w), lambda b, j: (0, 0))],
        out_specs=small,
        out_shape=jax.ShapeDtypeStruct((bx, k, n2, w), F32),
        compiler_params=_cp("parallel", "parallel"),
        name="hyena_dft_last",
    )(*f4, bre, bim, p4, bias)


def _hyena_spectrum(lx, filt_params, consts):
    feat, (f1, m1, _, _) = consts
    n1, n2 = _dft_factors(2 * lx)
    filt, ssq = _hy_filter(lx, feat, *filt_params)
    rs = lax.rsqrt(ssq + EPS)
    fre, fim = _hy_stage1(_hi_lo(_per_offset(f1)), filt.reshape(1, n1, n2, HY_WIDTH))
    return _hy_filter_spectrum(m1, fre, fim, rs)


def _hyena(zb, conv_w, conv_b, spectrum, hy_bias, consts):
    bx, lx, _ = zb.shape
    _, (f1, m1, m2, f4) = consts
    n = 2 * lx
    n1, n2 = _dft_factors(n)
    w = HY_WIDTH
    hre, him = spectrum
    p, x0 = _hy_prep(zb, conv_w, conv_b)
    k = n1 // 2
    p4 = p.reshape(bx, k, n2, w)
    are, aim = _hy_stage1(_hi_lo(_per_offset(f1[:, :k])), p4)
    bre, bim = _hy_mid(m1, m2, are, aim, hre, him)
    t = _hy_last(f4, bre, bim, p4, hy_bias.reshape(1, w))
    return x0, t.reshape(bx, lx, w)


WA_SHIFT_MAX = 35.0


def _wattn_kernel(sk_ref, q_ref, bias_ref, *rest, banded, fixed):
    if banded:
        kp_ref, kc_ref, kn_ref, kx_ref, vp_ref, vc_ref, vn_ref, vx_ref, o_ref = rest
    else:
        kx_ref, vx_ref, o_ref = rest
    h = pl.program_id(1)
    i = pl.program_id(2)
    last = pl.num_programs(2) - 1
    qb = q_ref.shape[1]
    q = jnp.concatenate([q_ref[0, :, :LANES], q_ref[0, :, LANES:]], axis=0)
    if banded:
        kk = jnp.concatenate([kp_ref[0], kc_ref[0], kn_ref[0], kx_ref[0]], axis=0)
        vv = jnp.concatenate([vp_ref[0], vc_ref[0], vn_ref[0], vx_ref[0]], axis=0)
    else:
        kk, vv = kx_ref[0], vx_ref[0]
    s = lax.dot_general(q, kk, (((1,), (1,)), ((), ())), preferred_element_type=F32) + bias_ref[0]
    if banded:
        c = lax.broadcasted_iota(jnp.int32, (1, kk.shape[0]), 1) - BLOCK
        outside = ((c < 0) & (i == 0)) | ((c >= qb) & (c < qb + BLOCK) & (i == last))
        s = s + jnp.where(outside, NEG_INF, 0.0)
    top = lax.broadcasted_iota(jnp.int32, (2 * qb, 1), 0) < qb
    sk = jnp.where(top, sk_ref[h * WA_GROUP], sk_ref[h * WA_GROUP + 1])
    if fixed:
        p, sink_term = jnp.exp2(s), sk
    else:
        m = jnp.maximum(jnp.max(s, axis=-1, keepdims=True), sk)
        p, sink_term = jnp.exp2(s - m), jnp.exp2(sk - m)
    acc = jnp.dot(p.astype(BF16), vv, preferred_element_type=F32)
    o = (acc * (1.0 / (acc[:, WA_DIM:WA_DIM + 1] + sink_term))).astype(BF16)
    o_ref[0] = jnp.concatenate([o[:qb], o[qb:]], axis=1)


def _window_attention(q, k, v, kx, vx, sink, bound, banded):
    b, lq, _ = q.shape
    cx = kx.shape[1]
    qb = min(256, lq)
    per = qb // BLOCK
    nblk = lq // BLOCK
    side = lambda f: pl.BlockSpec((1, BLOCK, LANES), f)
    prev = side(lambda b_, h, i: (b_, jnp.maximum(i * per - 1, 0), h))
    nxt = side(lambda b_, h, i: (b_, jnp.minimum((i + 1) * per, nblk - 1), h))
    cur = pl.BlockSpec((1, qb, LANES), lambda b_, h, i: (b_, i, h))
    ctx = pl.BlockSpec((1, cx, LANES), lambda b_, h, i: (b_, 0, h))
    qspec = pl.BlockSpec((1, qb, WA_GROUP * LANES), lambda b_, h, i: (b_, i, h))
    mask = jnp.zeros((WA_GROUP * qb, cx), F32)
    if banded:
        r = jnp.arange(WA_GROUP * qb)[:, None] % qb
        c = jnp.arange(qb + 2 * BLOCK)[None, :] - BLOCK
        mask = jnp.concatenate([jnp.where(jnp.abs(r - c) <= BLOCK, 0.0, NEG_INF).astype(F32), mask], axis=1)
        specs, args = [prev, cur, nxt, ctx, prev, cur, nxt, ctx], (k, k, k, kx, v, v, v, vx)
    else:
        specs, args = [ctx, ctx], (kx, vx)
    fixed = bound <= WA_SHIFT_MAX
    shift = jnp.maximum(bound, sink)
    rows = jnp.repeat(shift.reshape(WA_KV_HEADS, WA_GROUP), qb, axis=1)[:, :, None]
    bias = mask[None] - jnp.where(fixed, LOG2E * rows, 0.0)
    sk = jnp.where(fixed, jnp.exp2(LOG2E * (sink - shift)), LOG2E * sink)

    def call(fixed_, *ops):
        return pl.pallas_call(
            functools.partial(_wattn_kernel, banded=banded, fixed=fixed_),
            grid=(b, WA_KV_HEADS, lq // qb),
            in_specs=[pl.BlockSpec(memory_space=pltpu.SMEM), qspec,
                      pl.BlockSpec((1,) + bias.shape[1:], lambda b_, h, i: (h, 0, 0))] + specs,
            out_specs=qspec,
            out_shape=jax.ShapeDtypeStruct((b, lq, WA_HEADS * LANES), BF16),
            compiler_params=_cp("parallel", "parallel", "arbitrary"),
            name="window_attention" if fixed_ else "window_attention_online",
        )(*ops)

    return lax.cond(fixed, functools.partial(call, True), functools.partial(call, False), sk, q, bias, *args)


def _conf_kernel(zp_ref, zc_ref, zn_ref, w_ref, b_ref, lg_ref, lb_ref, o_ref, ext, sh, *, halo):
    i = pl.program_id(1)
    last = pl.num_programs(1) - 1
    tl = zc_ref.shape[1]
    cw = CF_WIDTH

    def glu(z_ref):
        z = z_ref[0].astype(F32)
        return z[:, :cw] * jax.nn.sigmoid(z[:, cw:])

    ext[0:halo] = jnp.where(i == 0, 0.0, glu(zp_ref))
    ext[halo:halo + tl] = glu(zc_ref)
    ext[halo + tl:] = jnp.where(i == last, 0.0, glu(zn_ref))
    for r in range(1, SUBLANES):
        sh[r - 1] = ext[pl.ds(r, sh.shape[1]), :]
    w = w_ref[...]
    u = jnp.zeros((tl, cw), F32) + b_ref[...]
    for j in range(CF_TAPS):
        off = halo - CF_TAPS // 2 + j
        base, r = off // SUBLANES * SUBLANES, off % SUBLANES
        tap = ext[pl.ds(base, tl), :] if r == 0 else sh[r - 1, pl.ds(base, tl), :]
        u = u + tap * w[j:j + 1]
    uc = u - jnp.mean(u, axis=-1, keepdims=True)
    y = uc * lax.rsqrt(jnp.mean(uc * uc, axis=-1, keepdims=True) + EPS) * lg_ref[...] + lb_ref[...]
    o_ref[0] = _silu(y).astype(BF16)


def _conformer(zd, dw_w, dw_b, ln_g, ln_b):
    bx, lx, w = zd.shape
    tl = min(256, lx)
    halo = 2 * SUBLANES
    nh = lx // halo
    per = tl // halo
    row = lambda a: a.reshape(1, -1)
    const = lambda b, i: (0, 0)
    return pl.pallas_call(
        functools.partial(_conf_kernel, halo=halo),
        grid=(bx, lx // tl),
        in_specs=[pl.BlockSpec((1, halo, w), lambda b, i: (b, jnp.maximum(i * per - 1, 0), 0)),
                  pl.BlockSpec((1, tl, w), lambda b, i: (b, i, 0)),
                  pl.BlockSpec((1, halo, w), lambda b, i: (b, jnp.minimum((i + 1) * per, nh - 1), 0)),
                  pl.BlockSpec(dw_w.shape, const)] + [pl.BlockSpec((1, CF_WIDTH), const)] * 3,
        out_specs=pl.BlockSpec((1, tl, CF_WIDTH), lambda b, i: (b, i, 0)),
        out_shape=jax.ShapeDtypeStruct((bx, lx, CF_WIDTH), BF16),
        scratch_shapes=[pltpu.VMEM((tl + 2 * halo, CF_WIDTH), F32),
                        pltpu.VMEM((SUBLANES - 1, tl + 2 * halo - SUBLANES, CF_WIDTH), F32)],
        compiler_params=_cp("parallel", "parallel"),
        name="conformer_conv",
    )(zd, zd, zd, dw_w, row(dw_b), row(ln_g), row(ln_b))


def _merge_kernel(x_ref, m_ref, ya, yb0, ybt, yc, yd, zg_ref, bg_ref, wa, wb, wc, wd, wo_ref, o_ref, *, goff):
    d = x_ref.shape[-1]
    acc = jnp.zeros(x_ref.shape[1:], F32)
    ys = (ya[0], (yb0[0] * ybt[0]).astype(BF16), yc[0], yd[0])
    for i, (y, w) in enumerate(zip(ys, (wa, wb, wc, wd))):
        gate = jax.nn.sigmoid(zg_ref[0, :, i * d:(i + 1) * d].astype(F32) + bg_ref[:, i * d:(i + 1) * d])
        acc = acc + gate * jnp.dot(y, w[...], preferred_element_type=F32)
    out = jnp.dot(acc.astype(BF16), wo_ref[...], preferred_element_type=F32)
    o_ref[0] = x_ref[0] + m_ref[0, :, goff:goff + d] * out


def _merge(x, mod, ys, zg, b_gate, wbs, w_out):
    bx, lx, d = x.shape
    tm = min(512, lx)
    row = lambda b, i: (b, i, 0)
    const = lambda b, i: (0, 0)
    once = lambda a: pl.BlockSpec(a.shape, const, pipeline_mode=pl.Buffered(1))
    return pl.pallas_call(
        functools.partial(_merge_kernel, goff=2 * d),
        grid=(bx, lx // tm),
        in_specs=[pl.BlockSpec((1, tm, d), row), pl.BlockSpec((1, 1, mod.shape[-1]), lambda b, i: (b, 0, 0))]
                 + [pl.BlockSpec((1, tm, y.shape[-1]), row) for y in ys]
                 + [pl.BlockSpec((1, tm, N_BRANCH * d), row), pl.BlockSpec((1, N_BRANCH * d), const)]
                 + [once(w) for w in wbs] + [once(w_out)],
        out_specs=pl.BlockSpec((1, tm, d), row),
        out_shape=jax.ShapeDtypeStruct(x.shape, F32),
        input_output_aliases={0: 0},
        compiler_params=_cp("parallel", "parallel"),
        name="merge",
    )(x, mod, *ys, zg, b_gate.reshape(1, -1), *wbs, w_out)


def _slot_rows(w, group):
    n = w.shape[0] // group
    return jnp.pad(w.reshape(n, group, -1), ((0, 0), (0, LANES - group), (0, 0))).reshape(n * LANES, -1)


def _moe_kernel(x_ref, m_ref, g_ref, wr_ref, br_ref, w1_ref, w3_ref, w2_ref, o_ref, h_sc, gate_sc, acc_sc):
    e = pl.program_id(2)
    d = x_ref.shape[-1]
    tm = x_ref.shape[1]
    lane = lax.broadcasted_iota(jnp.int32, (tm, LANES), 1).astype(F32)

    @pl.when(e == 0)
    def _():
        h = _rms(x_ref[0]) * g_ref[...] * (1.0 + m_ref[0, :, 4 * d:5 * d]) + m_ref[0, :, 3 * d:4 * d]
        h_sc[...] = h.astype(BF16)
        lg = _dot3(*_hi_lo(h), wr_ref[...]) + br_ref[...]
        isg = lane < N_GROUPS
        gmax = jnp.max(jnp.where(isg, lg, NEG_INF), axis=-1, keepdims=True)
        gi = jnp.min(jnp.where(isg & (lg == gmax), lane, LANES), axis=-1, keepdims=True)
        gw = 1.0 / jnp.sum(jnp.where(isg, jnp.exp(lg - gmax), 0.0), axis=-1, keepdims=True)
        lo = N_GROUPS + gi * EXP_PER_GROUP
        ise = (lane >= lo) & (lane < lo + EXP_PER_GROUP)
        le = jnp.where(ise, lg, NEG_INF)
        m1 = jnp.max(le, axis=-1, keepdims=True)
        i1 = jnp.min(jnp.where(ise & (le == m1), lane, LANES), axis=-1, keepdims=True)
        ise2 = ise & (lane != i1)
        le2 = jnp.where(ise2, lg, NEG_INF)
        m2 = jnp.max(le2, axis=-1, keepdims=True)
        i2 = jnp.min(jnp.where(ise2 & (le2 == m2), lane, LANES), axis=-1, keepdims=True)
        r = jnp.exp(m2 - m1)
        wa = gw / (1.0 + r)
        gate_sc[...] = jnp.where(lane == i1, wa, 0.0) + jnp.where(lane == i2, wa * r, 0.0)
        acc_sc[...] = jnp.zeros_like(acc_sc)

    hb = h_sc[...]
    u = (_silu(jnp.dot(hb, w1_ref[0].astype(BF16), preferred_element_type=F32))
         * jnp.dot(hb, w3_ref[0].astype(BF16), preferred_element_type=F32))
    ge = jnp.sum(jnp.where(lane == (e + N_GROUPS).astype(F32), gate_sc[...], 0.0), axis=-1, keepdims=True)
    acc_sc[...] += ge * jnp.dot(u.astype(BF16), w2_ref[0].astype(BF16), preferred_element_type=F32)

    @pl.when(e == pl.num_programs(2) - 1)
    def _():
        o_ref[0] = x_ref[0] + m_ref[0, :, 5 * d:6 * d] * acc_sc[...]


def _moe(x, mod, g, w_router, b_router, w1, w3, w2):
    bx, lx, d = x.shape
    tm = min(1024, lx)
    ne, _, f = w1.shape
    row = lambda b, i, e: (b, i, 0)
    const = lambda b, i, e: (0, 0)
    return pl.pallas_call(
        _moe_kernel,
        grid=(bx, lx // tm, ne),
        in_specs=[pl.BlockSpec((1, tm, d), row), pl.BlockSpec((1, 1, mod.shape[-1]), lambda b, i, e: (b, 0, 0)),
                  pl.BlockSpec((1, d), const), pl.BlockSpec((d, LANES), const), pl.BlockSpec((1, LANES), const),
                  pl.BlockSpec((1, d, f), lambda b, i, e: (e, 0, 0)), pl.BlockSpec((1, d, f), lambda b, i, e: (e, 0, 0)),
                  pl.BlockSpec((1, f, d), lambda b, i, e: (e, 0, 0))],
        out_specs=pl.BlockSpec((1, tm, d), row),
        out_shape=jax.ShapeDtypeStruct(x.shape, F32),
        scratch_shapes=[pltpu.VMEM((tm, d), BF16), pltpu.VMEM((tm, LANES), F32), pltpu.VMEM((tm, d), F32)],
        input_output_aliases={0: 0},
        compiler_params=_cp("parallel", "parallel", "arbitrary"),
        name="moe",
    )(x, mod, g.reshape(1, d), w_router, b_router, w1, w3, w2)


SC_CORES = 2
SC_SUBCORES = 16
SC_CHUNK = 64
MOE_ROWS = 1024


def _sc_gather(table, idx):
    n = idx.shape[0]
    w = table.shape[1]
    per = n // (SC_CORES * SC_SUBCORES)
    assert per * SC_CORES * SC_SUBCORES == n and per % SC_CHUNK == 0
    mesh = plsc.VectorSubcoreMesh(core_axis_name="c", subcore_axis_name="s")

    @functools.partial(
        pl.kernel, mesh=mesh, out_type=jax.ShapeDtypeStruct((n, w), table.dtype),
        scratch_types=[pltpu.VMEM((SC_CHUNK,), jnp.int32), pltpu.VMEM((SC_CHUNK, w), table.dtype),
                       pltpu.SemaphoreType.DMA],
        name="sc_row_gather")
    def gather(table_hbm, idx_hbm, out_hbm, idx_v, rows_v, sem):
        base = (lax.axis_index("s") * SC_CORES + lax.axis_index("c")) * per

        @pl.loop(0, per // SC_CHUNK)
        def _(j):
            off = pl.multiple_of(base + j * SC_CHUNK, SC_CHUNK)
            pltpu.sync_copy(idx_hbm.at[pl.ds(off, SC_CHUNK)], idx_v)
            pltpu.async_copy(table_hbm.at[idx_v], rows_v, sem).wait()
            pltpu.sync_copy(rows_v, out_hbm.at[pl.ds(off, SC_CHUNK)])

    return gather(table, idx)


def _sc_scatter(rows, idx, n_out):
    n, w = rows.shape
    per = n // (SC_CORES * SC_SUBCORES)
    assert per * SC_CORES * SC_SUBCORES == n and per % SC_CHUNK == 0
    mesh = plsc.VectorSubcoreMesh(core_axis_name="c", subcore_axis_name="s")

    @functools.partial(
        pl.kernel, mesh=mesh, out_type=jax.ShapeDtypeStruct((n_out, w), rows.dtype),
        scratch_types=[pltpu.VMEM((SC_CHUNK,), jnp.int32), pltpu.VMEM((SC_CHUNK, w), rows.dtype),
                       pltpu.SemaphoreType.DMA],
        name="sc_row_scatter")
    def scatter(rows_hbm, idx_hbm, out_hbm, idx_v, rows_v, sem):
        base = (lax.axis_index("s") * SC_CORES + lax.axis_index("c")) * per

        @pl.loop(0, per // SC_CHUNK)
        def _(j):
            off = pl.multiple_of(base + j * SC_CHUNK, SC_CHUNK)
            pltpu.sync_copy(idx_hbm.at[pl.ds(off, SC_CHUNK)], idx_v)
            pltpu.sync_copy(rows_hbm.at[pl.ds(off, SC_CHUNK)], rows_v)
            pltpu.async_copy(rows_v, out_hbm.at[idx_v], sem).wait()

    return scatter(rows, idx)


def _route_kernel(x_ref, m_ref, g_ref, wr_ref, br_ref, rows_ref, gi_ref):
    d = x_ref.shape[-1]
    tm = x_ref.shape[1]
    lane = lax.broadcasted_iota(jnp.int32, (tm, LANES), 1).astype(F32)
    h = _rms(x_ref[0]) * g_ref[...] * (1.0 + m_ref[0, :, 4 * d:5 * d]) + m_ref[0, :, 3 * d:4 * d]
    rows_ref[0, :, :d] = h
    lg = _dot3(*_hi_lo(h), wr_ref[...]) + br_ref[...]
    isg = lane < N_GROUPS
    gmax = jnp.max(jnp.where(isg, lg, NEG_INF), axis=-1, keepdims=True)
    gi = jnp.min(jnp.where(isg & (lg == gmax), lane, LANES), axis=-1, keepdims=True)
    gw = 1.0 / jnp.sum(jnp.where(isg, jnp.exp(lg - gmax), 0.0), axis=-1, keepdims=True)
    lo = N_GROUPS + gi * EXP_PER_GROUP
    ise = (lane >= lo) & (lane < lo + EXP_PER_GROUP)
    le = jnp.where(ise, lg, NEG_INF)
    m1 = jnp.max(le, axis=-1, keepdims=True)
    i1 = jnp.min(jnp.where(ise & (le == m1), lane, LANES), axis=-1, keepdims=True)
    ise2 = ise & (lane != i1)
    le2 = jnp.where(ise2, lg, NEG_INF)
    m2 = jnp.max(le2, axis=-1, keepdims=True)
    i2 = jnp.min(jnp.where(ise2 & (le2 == m2), lane, LANES), axis=-1, keepdims=True)
    r = jnp.exp(m2 - m1)
    wa = gw / (1.0 + r)
    rows_ref[0, :, d:] = jnp.where(lane == i1 - lo, wa, 0.0) + jnp.where(lane == i2 - lo, wa * r, 0.0)
    gi_ref[0] = gi.astype(jnp.int32)


def _route(x, mod, g, w_router, b_router):
    bx, lx, d = x.shape
    tm = min(512, lx)
    row = lambda b, i: (b, i, 0)
    const = lambda b, i: (0, 0)
    return pl.pallas_call(
        _route_kernel,
        grid=(bx, lx // tm),
        in_specs=[pl.BlockSpec((1, tm, d), row), pl.BlockSpec((1, 1, mod.shape[-1]), lambda b, i: (b, 0, 0)),
                  pl.BlockSpec((1, d), const), pl.BlockSpec((d, LANES), const), pl.BlockSpec((1, LANES), const)],
        out_specs=[pl.BlockSpec((1, tm, d + LANES), row), pl.BlockSpec((1, tm, 1), row)],
        out_shape=[jax.ShapeDtypeStruct((bx, lx, d + LANES), F32), jax.ShapeDtypeStruct((bx, lx, 1), jnp.int32)],
        compiler_params=_cp("parallel", "parallel"),
        name="moe_route",
    )(x, mod, g.reshape(1, d), w_router, b_router)


def _gmoe_kernel(grp_ref, nv_ref, xs_ref, w1_ref, w3_ref, w2_ref, o_ref, h_sc, acc_sc):
    i = pl.program_id(0)
    e = pl.program_id(1)
    tm, d = h_sc.shape
    valid = lax.broadcasted_iota(jnp.int32, (tm, 1), 0) < nv_ref[i]

    @pl.when(nv_ref[i] > 0)
    def _():
        @pl.when(e == 0)
        def _():
            h_sc[...] = jnp.where(valid, xs_ref[:, :d], 0.0).astype(BF16)
            acc_sc[...] = jnp.zeros_like(acc_sc)

        hb = h_sc[...]
        u = (_silu(jnp.dot(hb, w1_ref[0].astype(BF16), preferred_element_type=F32))
             * jnp.dot(hb, w3_ref[0].astype(BF16), preferred_element_type=F32))
        lane = lax.broadcasted_iota(jnp.int32, (tm, LANES), 1)
        ge = jnp.sum(jnp.where(valid & (lane == e), xs_ref[:, d:], 0.0), axis=-1, keepdims=True)
        acc_sc[...] += ge * jnp.dot(u.astype(BF16), w2_ref[0].astype(BF16), preferred_element_type=F32)

    @pl.when(e == pl.num_programs(1) - 1)
    def _():
        o_ref[...] = jnp.where(nv_ref[i] > 0, acc_sc[...], 0.0)


def _grouped_moe(grp, nv, xs, w1, w3, w2):
    p, dw = xs.shape
    d = dw - LANES
    _, _, f = w1.shape
    wmap = lambda i, e, grp, nv: (grp[i] * EXP_PER_GROUP + e, 0, 0)
    rows = lambda i, e, grp, nv: (i, 0)
    return pl.pallas_call(
        _gmoe_kernel,
        grid_spec=pltpu.PrefetchScalarGridSpec(
            num_scalar_prefetch=2,
            grid=(p // MOE_ROWS, EXP_PER_GROUP),
            in_specs=[pl.BlockSpec((MOE_ROWS, dw), rows),
                      pl.BlockSpec((1, d, f), wmap), pl.BlockSpec((1, d, f), wmap), pl.BlockSpec((1, f, d), wmap)],
            out_specs=pl.BlockSpec((MOE_ROWS, d), rows),
            scratch_shapes=[pltpu.VMEM((MOE_ROWS, d), BF16), pltpu.VMEM((MOE_ROWS, d), F32)]),
        out_shape=jax.ShapeDtypeStruct((p, d), F32),
        compiler_params=_cp("arbitrary", "arbitrary"),
        name="moe_experts",
    )(grp, nv, xs, w1, w3, w2)


def _residual_kernel(x_ref, m_ref, y_ref, o_ref):
    d = x_ref.shape[-1]
    o_ref[0] = x_ref[0] + m_ref[0, :, 5 * d:6 * d] * y_ref[0]


def _residual(x, mod, y):
    bx, lx, d = x.shape
    tm = min(1024, lx)
    row = lambda b, i: (b, i, 0)
    return pl.pallas_call(
        _residual_kernel,
        grid=(bx, lx // tm),
        in_specs=[pl.BlockSpec((1, tm, d), row), pl.BlockSpec((1, 1, mod.shape[-1]), lambda b, i: (b, 0, 0)),
                  pl.BlockSpec((1, tm, d), row)],
        out_specs=pl.BlockSpec((1, tm, d), row),
        out_shape=jax.ShapeDtypeStruct(x.shape, F32),
        input_output_aliases={0: 0},
        compiler_params=_cp("parallel", "parallel"),
        name="moe_residual",
    )(x, mod, y)


def _moe_sorted(x, mod, g, w_router, b_router, w1, w3, w2, during_scatter, during_gather):
    bx, lx, d = x.shape
    t = bx * lx
    rows, gi = _route(x, mod, g, w_router, b_router)
    gi = gi.reshape(t)
    onehot = (gi[:, None] == jnp.arange(N_GROUPS, dtype=jnp.int32)[None, :]).astype(jnp.int32)
    csum = jnp.cumsum(onehot, axis=0)
    counts = csum[-1]
    rank = jnp.take_along_axis(csum, gi[:, None], axis=1)[:, 0] - 1
    padded = (counts + MOE_ROWS - 1) // MOE_ROWS * MOE_ROWS
    pend = jnp.cumsum(padded)
    pstart = pend - padded
    pos = (pstart[gi] + rank).astype(jnp.int32)
    p = t + N_GROUPS * MOE_ROWS
    bstart = jnp.arange(p // MOE_ROWS, dtype=jnp.int32) * MOE_ROWS
    grp = jnp.minimum(jnp.searchsorted(pend, bstart, side="right"), N_GROUPS - 1).astype(jnp.int32)
    nv = jnp.clip(pstart[grp] + counts[grp] - bstart, 0, MOE_ROWS).astype(jnp.int32)
    xs = _sc_scatter(rows.reshape(t, d + LANES), pos, p)
    side_a = during_scatter()
    ys = _grouped_moe(grp, nv, xs, w1, w3, w2)
    yt = _sc_gather(ys, pos)
    side_b = during_gather()
    return _residual(x, mod, yt.reshape(bx, lx, d)), side_a, side_b


def kernel(x, c, ctx, c_ctx, w_mod, b_mod, norm1_g, norm2_g, w_in, b_gate, da_qn, da_kn, da_lam, da_subln, hy_conv_w, hy_conv_b, hf_w1, hf_b1, hf_w2, hf_b2, hf_w3, hf_b3, hf_w4, hf_freq, hy_bias, wa_qn, wa_kn, wa_sink, cf_dw_w, cf_dw_b, cf_ln_g, cf_ln_b, w_branch, w_out, w_rg, b_rg, w_re, b_re, w1, w3, w2):
    b, s, d = x.shape
    cl = ctx.shape[1]
    depth = w_mod.shape[0]
    assert s % 256 == 0 and cl % 256 == 0 and s % GRID_W == 0

    nrow = -(-(b + 1) // SUBLANES) * SUBLANES
    crows = jnp.zeros((nrow, d), F32).at[:b].set(c).at[b].set(c_ctx)
    mods = _mod_vectors(crows, w_mod, b_mod)

    aw, qw, kw = DA_HEADS * DA_DIM, WA_HEADS * WA_DIM, WA_KV_HEADS * WA_DIM
    tab_lat = (*_rope_tables(s, DA_DIM, DA_HEADS), *_rope_tables(s, WA_DIM, WA_HEADS), *_rope_tables(s, WA_DIM, WA_KV_HEADS))
    tab_ctx = (*_unit_tables(cl, aw), *_unit_tables(cl, qw), *_unit_tables(cl, kw))
    hy_lat = (_hy_feat(s), _dft_tables(2 * s))
    hy_ctx = (_hy_feat(cl), _dft_tables(2 * cl))
    gms = (_group_ones(aw, DA_DIM), _group_ones(qw, WA_DIM), _group_ones(kw, WA_DIM))
    qvec = _slot_fill(DA_DIM, DA_DIM + 1, 1.0)
    vvec = _slot_fill(DA_VDIM, DA_VDIM + DA_ONES, 1.0)

    filt_params = lambda l: (hf_w1[l], hf_b1[l], hf_w2[l], hf_b2[l], hf_w3[l], hf_b3[l], hf_w4[l], hf_freq[l])
    w_in_b = w_in.astype(BF16)
    xc = ctx
    spec_lat = _hyena_spectrum(s, filt_params(0), hy_lat)
    for l in range(depth):
        last = l == depth - 1
        lam_init = 0.8 - 0.6 * math.exp(-0.3 * l)
        mod_x = mods[l, :b][:, None, :]
        mod_c = jnp.broadcast_to(mods[l, b][None, None, :], (b, 1, mods.shape[-1]))
        shift = 1.02 * LOG2E * DA_DIM ** 0.5 * jnp.max(jnp.abs(da_qn[l])) * jnp.max(jnp.abs(da_kn[l]))
        fixed = shift <= DA_SHIFT_MAX
        kvec = _slot_fill(DA_DIM, DA_DIM + 1, jnp.where(fixed, -shift, 0.0))
        tile = lambda a, n: jnp.tile(a, n).reshape(1, -1)
        consts = (*gms, tile(da_qn[l], DA_HEADS), tile(da_kn[l], DA_HEADS), tile(wa_qn[l], WA_HEADS),
                  tile(wa_kn[l], WA_KV_HEADS), qvec, kvec, vvec)
        q1, q2, k1, k2, v, zb, qc, kc, vc, zd, zg = _in_proj(x, mod_x, norm1_g[l], w_in_b, l, tab_lat, consts)
        q1x, q2x, k1x, k2x, vx, zbx, qcx, kcx, vcx, zdx, zgx = _in_proj(xc, mod_c, norm1_g[l], w_in_b, l, tab_ctx, consts)
        wb4 = w_branch[l].astype(BF16)
        wbs = (_slot_rows(wb4[0], DA_VDIM), wb4[1], _slot_rows(wb4[2], WA_DIM), wb4[3])
        wo = w_out[l].astype(BF16)

        ya = _diff_attention(q1, q2, [(k1, k2, v), (k1x, k2x, vx)], fixed, da_lam[l], da_subln[l], lam_init)
        yb = _hyena(zb, hy_conv_w[l], hy_conv_b[l], spec_lat, hy_bias[l], hy_lat)
        wbound = 1.02 * WA_DIM ** 0.5 * jnp.max(jnp.abs(wa_qn[l])) * jnp.max(jnp.abs(wa_kn[l]))
        yc_ = _window_attention(qc, kc, vc, kcx, vcx, wa_sink[l], wbound, True)
        yd = _conformer(zd, cf_dw_w[l], cf_dw_b[l], cf_ln_g[l], cf_ln_b[l])
        x = _merge(x, mod_x, (ya, *yb, yc_, yd), zg, b_gate[l], wbs, wo)

        w_router = jnp.pad(jnp.concatenate([w_rg[l], w_re[l]], axis=1), ((0, 0), (0, LANES - N_GROUPS - N_EXPERTS)))
        b_router = jnp.pad(jnp.concatenate([b_rg[l], b_re[l]]), (0, LANES - N_GROUPS - N_EXPERTS)).reshape(1, LANES)
        ew = (w1[l], w3[l], w2[l])

        def context_layer(xc=xc):
            if last:
                return xc
            yca = _diff_attention(q1x, q2x, [(k1x, k2x, vx)], fixed, da_lam[l], da_subln[l], lam_init)
            ycb = _hyena(zbx, hy_conv_w[l], hy_conv_b[l], _hyena_spectrum(cl, filt_params(l), hy_ctx), hy_bias[l],
                         hy_ctx)
            ycc = _window_attention(qcx, kcx, vcx, kcx, vcx, wa_sink[l], wbound, False)
            ycd = _conformer(zdx, cf_dw_w[l], cf_dw_b[l], cf_ln_g[l], cf_ln_b[l])
            xn = _merge(xc, mod_c, (yca, *ycb, ycc, ycd), zgx, b_gate[l], wbs, wo)
            return _moe(xn.reshape(1, b * cl, d), mod_c[:1], norm2_g[l], w_router, b_router, *ew).reshape(b, cl, d)

        next_spectrum = lambda: None if last else _hyena_spectrum(s, filt_params(l + 1), hy_lat)
        x, xc, spec_lat = _moe_sorted(x, mod_x, norm2_g[l], w_router, b_router, *ew, context_layer, next_spectrum)
    return x
```

```python
import functools
import math

import jax
import jax.numpy as jnp
from jax import lax
from jax.experimental import pallas as pl
from jax.experimental.pallas import tpu as pltpu
from jax.experimental.pallas import tpu_sc as plsc

F32 = jnp.float32
BF16 = jnp.bfloat16
HI = lax.Precision.HIGHEST

GRID_W = 64
BLOCK = 128
ROPE_BASE = 10000.0
EPS = 1e-6
NEG_INF = -1e30

DA_HEADS = 4
DA_DIM = 32
DA_VDIM = 64
HY_WIDTH = 256
HY_BANDS = 16
HY_FF = 64
HY_SHIFT = 0.05
HY_FAST_DECAY = 0.3
HY_SLOW_DECAY = 1.5
HY_TARGET = 1e-2
WA_HEADS = 4
WA_KV_HEADS = 2
WA_GROUP = 2
WA_DIM = 64
CF_WIDTH = 256
CF_TAPS = 31
N_BRANCH = 4
BRANCH_W = 256
N_GROUPS = 4
EXP_PER_GROUP = 4
N_EXPERTS = 16

W_A = 4 * DA_HEADS * DA_DIM + DA_HEADS * DA_VDIM
W_B = 3 * HY_WIDTH
W_C = (WA_HEADS + 2 * WA_KV_HEADS) * WA_DIM
W_D = 2 * CF_WIDTH
OFF_B = W_A
OFF_C = OFF_B + W_B
OFF_D = OFF_C + W_C
OFF_G = OFF_D + W_D

LOG2E = math.log2(math.e)
LANES = 128
SUBLANES = 8
VMEM_LIMIT = 56 * 1024 * 1024

DA_ONES = 16
DA_SHIFT_MAX = 50.0


def _cp(*sem):
    return pltpu.CompilerParams(dimension_semantics=sem, vmem_limit_bytes=VMEM_LIMIT)


def _rms(xf):
    return xf * lax.rsqrt(jnp.mean(xf * xf, axis=-1, keepdims=True) + EPS)


def _silu(x):
    return x * jax.nn.sigmoid(x)


def _mod_kernel(c_ref, w_ref, b_ref, o_ref):
    s = _silu(c_ref[...])
    o_ref[0] = jnp.dot(s, w_ref[0], precision=HI, preferred_element_type=F32) + b_ref[0]


def _mod_vectors(crows, w_mod, b_mod):
    depth, d, n = w_mod.shape
    r = crows.shape[0]
    tn = 1536
    return pl.pallas_call(
        _mod_kernel,
        grid=(depth, n // tn),
        in_specs=[pl.BlockSpec((r, d), lambda l, j: (0, 0)),
                  pl.BlockSpec((1, d, tn), lambda l, j: (l, 0, j)),
                  pl.BlockSpec((1, 1, tn), lambda l, j: (l, 0, j))],
        out_specs=pl.BlockSpec((1, r, tn), lambda l, j: (l, 0, j)),
        out_shape=jax.ShapeDtypeStruct((depth, r, n), F32),
        compiler_params=_cp("arbitrary", "arbitrary"),
        name="mod_vectors",
    )(crows, w_mod, b_mod.reshape(depth, 1, n))


def _rope_tables(s, d, reps):
    rows = s // GRID_W
    row = jnp.repeat(jnp.arange(rows, dtype=F32), GRID_W)
    col = jnp.tile(jnp.arange(GRID_W, dtype=F32), rows)
    qd = d // 4
    inv = ROPE_BASE ** (-jnp.arange(qd, dtype=F32) / qd)
    ar = row[:, None] * inv[None, :]
    ac = col[:, None] * inv[None, :]
    z = jnp.zeros_like(ar)
    cos = jnp.concatenate([jnp.cos(ar), jnp.cos(ar), jnp.cos(ac), jnp.cos(ac)], axis=-1)
    sin_up = jnp.concatenate([-jnp.sin(ar), z, -jnp.sin(ac), z], axis=-1)
    sin_dn = jnp.concatenate([z, jnp.sin(ar), z, jnp.sin(ac)], axis=-1)
    t = lambda a: jnp.tile(a, (1, reps))
    return t(cos), t(sin_up), t(sin_dn)


def _unit_tables(s, w):
    return jnp.ones((s, w), F32), jnp.zeros((s, w), F32), jnp.zeros((s, w), F32)


def _group_ones(width, group):
    i = jnp.arange(width) // group
    return (i[:, None] == i[None, :]).astype(BF16)


def _hi_lo(a):
    hi = a.astype(BF16)
    return hi, (a - hi.astype(F32)).astype(BF16)


def _slot_fill(lo, hi, value):
    j = jnp.arange(LANES)
    return jnp.where((j >= lo) & (j < hi), value, 0.0).astype(F32).reshape(1, LANES)


def _store_slots(o_ref, y, group, fill):
    per = LANES // group
    lane = lax.broadcasted_iota(jnp.int32, (y.shape[0], LANES), 1)
    for h in range(y.shape[1] // group):
        blk = y[:, h // per * LANES:(h // per + 1) * LANES]
        if h % per:
            blk = pltpu.roll(blk, LANES - h % per * group, 1)
        o_ref[0, :, h * LANES:(h + 1) * LANES] = jnp.where(lane < group, blk, fill).astype(BF16)


def _norm_rope(x, gmat, gain, cos, sup, sdn, group, qd):
    w = x.shape[-1]
    sh, sl = _hi_lo(x * x)
    ss = (jnp.dot(sh, gmat, preferred_element_type=F32) + jnp.dot(sl, gmat, preferred_element_type=F32)) * (1.0 / group)
    xn = x * lax.rsqrt(ss + EPS) * gain
    return xn * cos + pltpu.roll(xn, w - qd, 1) * sup + pltpu.roll(xn, qd, 1) * sdn


def _inproj_kernel(x_ref, m_ref, g_ref, w_ref,
                   ca, ua, da, cq, uq, dq, ck, uk, dk, gma, gmq, gmk, qna, kna, qnc, knc,
                   qvec, kvec, vvec,
                   q1o, q2o, k1o, k2o, vo, zbo, qco, kco, vco, zdo, zgo):
    d = x_ref.shape[-1]
    x = x_ref[0]
    shift = m_ref[0, :, 0:d]
    scale = m_ref[0, :, d:2 * d]
    h = (_rms(x) * g_ref[...] * (1.0 + scale) + shift).astype(BF16)

    za = jnp.dot(h, w_ref[0, :, 0:OFF_B], preferred_element_type=F32)
    hw = DA_HEADS * DA_DIM
    cos, sup, sdn, gm = ca[...], ua[...], da[...], gma[...]
    qscale = DA_DIM ** -0.5 * LOG2E
    for t, (o, gain, sc, vec) in enumerate(((q1o, qna, qscale, qvec), (q2o, qna, qscale, qvec),
                                            (k1o, kna, 1.0, kvec), (k2o, kna, 1.0, kvec))):
        y = _norm_rope(za[:, t * hw:(t + 1) * hw], gm, gain[...], cos, sup, sdn, DA_DIM, DA_DIM // 4) * sc
        _store_slots(o, y, DA_DIM, vec[...])
    _store_slots(vo, za[:, 4 * hw:], DA_VDIM, vvec[...])

    zbo[0] = jnp.dot(h, w_ref[0, :, OFF_B:OFF_C], preferred_element_type=F32).astype(BF16)

    zc = jnp.dot(h, w_ref[0, :, OFF_C:OFF_D], preferred_element_type=F32)
    qw = WA_HEADS * WA_DIM
    kw = WA_KV_HEADS * WA_DIM
    y = _norm_rope(zc[:, 0:qw], gmq[...], qnc[...], cq[...], uq[...], dq[...], WA_DIM, WA_DIM // 4) * (WA_DIM ** -0.5 * LOG2E)
    _store_slots(qco, y, WA_DIM, 0.0)
    y = _norm_rope(zc[:, qw:qw + kw], gmk[...], knc[...], ck[...], uk[...], dk[...], WA_DIM, WA_DIM // 4)
    _store_slots(kco, y, WA_DIM, 0.0)
    _store_slots(vco, zc[:, qw + kw:], WA_DIM, vvec[...])

    zdo[0] = jnp.dot(h, w_ref[0, :, OFF_D:OFF_G], preferred_element_type=F32).astype(BF16)
    for k in range(N_BRANCH):
        zgo[0, :, k * d:(k + 1) * d] = jnp.dot(h, w_ref[0, :, OFF_G + k * d:OFF_G + (k + 1) * d],
                                               preferred_element_type=F32).astype(BF16)


def _in_proj(x, mod, g, w_all, layer, tables, consts):
    bx, lx, d = x.shape
    tm = min(512, lx)
    const = lambda b, i: (0, 0)
    row = lambda b, i: (b, i, 0)
    once = lambda a: pl.BlockSpec(a.shape, const, pipeline_mode=pl.Buffered(1))
    widths = [DA_HEADS * LANES] * 5 + [W_B, WA_HEADS * LANES, WA_KV_HEADS * LANES, WA_KV_HEADS * LANES, W_D, N_BRANCH * d]
    return pl.pallas_call(
        _inproj_kernel,
        grid=(bx, lx // tm),
        in_specs=[pl.BlockSpec((1, tm, d), row), pl.BlockSpec((1, 1, mod.shape[-1]), lambda b, i: (b, 0, 0)),
                  pl.BlockSpec((1, d), const)]
                 + [pl.BlockSpec((1,) + w_all.shape[1:], lambda b, i: (layer, 0, 0), pipeline_mode=pl.Buffered(1))]
                 + [pl.BlockSpec((tm, t.shape[1]), lambda b, i: (i, 0)) for t in tables]
                 + [once(c) for c in consts],
        out_specs=[pl.BlockSpec((1, tm, w), row) for w in widths],
        out_shape=[jax.ShapeDtypeStruct((bx, lx, w), BF16) for w in widths],
        compiler_params=_cp("parallel", "parallel"),
        name="in_proj",
    )(x, mod, g.reshape(1, d), w_all, *tables, *consts)


def _da_lambda(lam_ref, lam_init):
    lv = lam_ref[...]
    return (jnp.exp(jnp.sum(lv[0:1] * lv[1:2], keepdims=True)) - jnp.exp(jnp.sum(lv[2:3] * lv[3:4], keepdims=True))
            + lam_init)


def _dattn_kernel(lam_ref, sg_ref, q1_ref, q2_ref, *rest, lam_init, online, nsrc, tk_max):
    srcs = [rest[3 * s:3 * s + 3] for s in range(nsrc)]
    o_ref, acc1, acc2 = rest[3 * nsrc:]
    dn = (((1,), (1,)), ((), ()))
    q1 = q1_ref[0]
    q2 = q2_ref[0]
    tq = q1.shape[0]
    acc1[...] = jnp.zeros_like(acc1)
    acc2[...] = jnp.zeros_like(acc2)
    carry = (jnp.full((tq, 1), NEG_INF, F32),) * 2 if online else 0

    for k1_ref, k2_ref, v_ref in srcs:
        tk = min(tk_max, k1_ref.shape[1])

        def body(j, c, k1_ref=k1_ref, k2_ref=k2_ref, v_ref=v_ref, tk=tk):
            rows = pl.ds(pl.multiple_of(j * tk, tk), tk)
            vc = v_ref[0, rows, :]
            s1 = lax.dot_general(q1, k1_ref[0, rows, :], dn, preferred_element_type=F32)
            s2 = lax.dot_general(q2, k2_ref[0, rows, :], dn, preferred_element_type=F32)
            if online:
                m1, m2 = c
                n1 = jnp.maximum(m1, jnp.max(s1, axis=-1, keepdims=True))
                n2 = jnp.maximum(m2, jnp.max(s2, axis=-1, keepdims=True))
                acc1[...] = jnp.exp2(m1 - n1) * acc1[...] + jnp.dot(jnp.exp2(s1 - n1).astype(BF16), vc,
                                                                    preferred_element_type=F32)
                acc2[...] = jnp.exp2(m2 - n2) * acc2[...] + jnp.dot(jnp.exp2(s2 - n2).astype(BF16), vc,
                                                                    preferred_element_type=F32)
                return n1, n2
            acc1[...] += jnp.dot(jnp.exp2(s1).astype(BF16), vc, preferred_element_type=F32)
            acc2[...] += jnp.dot(jnp.exp2(s2).astype(BF16), vc, preferred_element_type=F32)
            return c

        carry = lax.fori_loop(0, k1_ref.shape[1] // tk, body, carry)

    dv = DA_VDIM
    a1 = acc1[...]
    a2 = acc2[...]
    lam = _da_lambda(lam_ref, lam_init)
    o = a1 * (1.0 / a1[:, dv:dv + 1]) - a2 * (lam / a2[:, dv:dv + 1])
    o = jnp.where(lax.broadcasted_iota(jnp.int32, o.shape, 1) < dv, o, 0.0)
    o = o * lax.rsqrt(jnp.sum(o * o, axis=-1, keepdims=True) * (1.0 / dv) + EPS)
    o_ref[0] = (o * (sg_ref[...] * (1.0 - lam_init))).astype(BF16)


def _diff_attention(q1, q2, srcs, fixed, lam_p, subln, lam_init):
    b, lq, _ = q1.shape
    h = DA_HEADS
    const = lambda b_, h_, i: (0, 0)
    sg = jnp.pad(subln, (0, LANES - DA_VDIM)).reshape(1, LANES)
    flat = [a for src in srcs for a in src]

    def call(online, *args):
        tq = min(256 if online else 2048, lq)
        qs = pl.BlockSpec((1, tq, LANES), lambda b_, h_, i: (b_, i, h_))
        return pl.pallas_call(
            functools.partial(_dattn_kernel, lam_init=lam_init, online=online, nsrc=len(srcs),
                              tk_max=256 if online else 512),
            grid=(b, h, lq // tq),
            in_specs=[pl.BlockSpec(lam_p.shape, const), pl.BlockSpec((1, LANES), const), qs, qs]
                     + [pl.BlockSpec((1, a.shape[1], LANES), lambda b_, h_, i: (b_, 0, h_)) for a in flat],
            out_specs=qs,
            out_shape=jax.ShapeDtypeStruct((b, lq, h * LANES), BF16),
            scratch_shapes=[pltpu.VMEM((tq, LANES), F32)] * 2,
            compiler_params=_cp("parallel", "parallel", "arbitrary"),
            name="diff_attention_online" if online else "diff_attention",
        )(*args)

    return lax.cond(fixed, functools.partial(call, False), functools.partial(call, True),
                    lam_p, sg, q1, q2, *flat)


def _hy_prep_kernel(zp_ref, zc_ref, zn_ref, w_ref, b_ref, p_ref, x0_ref, ext):
    i = pl.program_id(1)
    last = pl.num_programs(1) - 1
    tl = zc_ref.shape[1]
    h = 2 * SUBLANES
    ext[0:h] = jnp.where(i == 0, 0.0, zp_ref[0].astype(F32))
    ext[h:h + tl] = zc_ref[0].astype(F32)
    ext[h + tl:] = jnp.where(i == last, 0.0, zn_ref[0].astype(F32))
    w = w_ref[...]
    u = (ext[pl.ds(h - 1, tl), :] * w[0:1] + ext[pl.ds(h, tl), :] * w[1:2] + ext[pl.ds(h + 1, tl), :] * w[2:3]
         + b_ref[...])
    hw = HY_WIDTH
    x0_ref[0] = u[:, 0:hw]
    p_ref[0] = u[:, 2 * hw:3 * hw] * u[:, hw:2 * hw]


def _hy_prep(zb, conv_w, conv_b):
    bx, lx, w = zb.shape
    tl = min(256, lx)
    h = 2 * SUBLANES
    nh = lx // h
    per = tl // h
    out = jax.ShapeDtypeStruct((bx, lx, HY_WIDTH), F32)
    return pl.pallas_call(
        _hy_prep_kernel,
        grid=(bx, lx // tl),
        in_specs=[pl.BlockSpec((1, h, w), lambda b, i: (b, jnp.maximum(i * per - 1, 0), 0)),
                  pl.BlockSpec((1, tl, w), lambda b, i: (b, i, 0)),
                  pl.BlockSpec((1, h, w), lambda b, i: (b, jnp.minimum((i + 1) * per, nh - 1), 0)),
                  pl.BlockSpec(conv_w.shape, lambda b, i: (0, 0)),
                  pl.BlockSpec((1, w), lambda b, i: (0, 0))],
        out_specs=[pl.BlockSpec((1, tl, HY_WIDTH), lambda b, i: (b, i, 0))] * 2,
        out_shape=[out, out],
        scratch_shapes=[pltpu.VMEM((tl + 2 * h, w), F32)],
        compiler_params=_cp("parallel", "parallel"),
        name="hyena_prep",
    )(zb, zb, zb, conv_w, conv_b.reshape(1, w))


def _hy_filter_kernel(feat_ref, w1, b1, w2, b2, w3, b3, w4, fr_ref, dl_ref, filt_ref, ssq_ref, *, s):
    i = pl.program_id(0)
    tr = feat_ref.shape[0]
    feat = feat_ref[...]
    fr = fr_ref[...]
    dot = lambda a, w: _dot3(*_hi_lo(a), w[...])
    a = jnp.sin(fr * (dot(feat, w1) + b1[...]))
    a = jnp.sin(fr * (dot(a, w2) + b2[...]))
    a = jnp.sin(fr * (dot(a, w3) + b3[...]))
    coef = dot(a, w4)
    n = i * tr + lax.broadcasted_iota(jnp.int32, (tr, 1), 0)
    window = jnp.exp(-feat[:, 0:1] * dl_ref[...]) + HY_SHIFT
    half = jnp.where(n < s, coef[:, :HY_WIDTH], coef[:, HY_WIDTH:])
    filt = jnp.where(n == s, 0.0, half * window)
    filt_ref[...] = filt

    @pl.when(i == 0)
    def _():
        ssq_ref[...] = jnp.zeros_like(ssq_ref)

    ssq_ref[...] += jnp.sum(filt * filt, axis=0, keepdims=True)


def _hy_feat(s):
    t = jnp.linspace(0.0, 1.0, s, dtype=F32)[:, None]
    w = (2.0 * math.pi / s) * jnp.arange(s, dtype=F32)[:, None]
    bands = jnp.linspace(1e-4, HY_BANDS - 1, HY_BANDS, dtype=F32)[None, :]
    feat = jnp.concatenate([t, jnp.cos(w * bands), jnp.sin(w * bands)], axis=-1)
    feat = jnp.concatenate([feat, feat[:1], feat[:0:-1]], axis=0)
    return jnp.pad(feat, ((0, 0), (0, LANES - feat.shape[1])))


def _hy_filter(s, feat, w1, b1, w2, b2, w3, b3, w4, freq):
    n = 2 * s
    tr = min(512, n)
    deltas = jnp.abs(jnp.linspace(math.log(HY_TARGET) / HY_FAST_DECAY, math.log(HY_TARGET) / HY_SLOW_DECAY,
                                  HY_WIDTH, dtype=F32)).reshape(1, HY_WIDTH)
    w1p = jnp.pad(w1, ((0, LANES - w1.shape[0]), (0, 0)))
    row = lambda a: a.reshape(1, -1)
    args = (feat, w1p, row(b1), w2, row(b2), w3, row(b3), w4, row(freq), deltas)
    const = lambda i: (0, 0)
    return pl.pallas_call(
        functools.partial(_hy_filter_kernel, s=s),
        grid=(n // tr,),
        in_specs=[pl.BlockSpec((tr, LANES), lambda i: (i, 0))] + [pl.BlockSpec(a.shape, const) for a in args[1:]],
        out_specs=[pl.BlockSpec((tr, HY_WIDTH), lambda i: (i, 0)), pl.BlockSpec((1, HY_WIDTH), const)],
        out_shape=[jax.ShapeDtypeStruct((n, HY_WIDTH), F32), jax.ShapeDtypeStruct((1, HY_WIDTH), F32)],
        compiler_params=_cp("arbitrary"),
        name="hyena_filter",
    )(*args)


def _dft_factors(n):
    lg = n.bit_length() - 1
    assert 1 << lg == n
    n1 = 1 << (lg // 2)
    return n1, n // n1


def _dft_tables(n):
    n1, n2 = _dft_factors(n)
    ia = jnp.arange(n1, dtype=jnp.int32)
    ang1 = (2.0 * math.pi / n1) * ((ia[:, None] * ia[None, :]) % n1).astype(F32)
    f1 = jnp.concatenate([jnp.cos(ang1), -jnp.sin(ang1)], axis=0)
    c = jnp.arange(n1, dtype=jnp.int32)[:, None, None]
    d = jnp.arange(n2, dtype=jnp.int32)[None, :, None]
    b = jnp.arange(n2, dtype=jnp.int32)[None, None, :]
    ang = (2.0 * math.pi / n) * ((b * (c + n1 * d)) % n).astype(F32)
    re, im = jnp.cos(ang), -jnp.sin(ang)
    m1 = jnp.concatenate([jnp.concatenate([re, -im], axis=2), jnp.concatenate([im, re], axis=2)], axis=1)
    m2 = jnp.swapaxes(m1, 1, 2)
    ang4 = ang1[: n1 // 2]
    f4 = jnp.concatenate([jnp.cos(ang4), -jnp.sin(ang4)], axis=1) * (1.0 / n)
    return f1, _hi_lo(m1), _hi_lo(m2), _hi_lo(_per_offset(f4))


def _dot3(mh, ml, a):
    ah, al = _hi_lo(a)
    return (jnp.dot(mh, ah, preferred_element_type=F32) + jnp.dot(mh, al, preferred_element_type=F32)
            + jnp.dot(ml, ah, preferred_element_type=F32))


HY_TB = SUBLANES


def _per_offset(f):
    return jnp.kron(f, jnp.eye(HY_TB, dtype=f.dtype))


def _hy_stage1_kernel(fh_ref, fl_ref, x_ref, re_ref, im_ref):
    _, n1, tb, w = re_ref.shape
    y = _dot3(fh_ref[...], fl_ref[...], x_ref[0].reshape(-1, w))
    re_ref[0] = y[:n1 * tb].reshape(n1, tb, w)
    im_ref[0] = y[n1 * tb:].reshape(n1, tb, w)


def _hy_stage1(f, x4):
    bx, k, n2, w = x4.shape
    tb = HY_TB
    n1 = f[0].shape[0] // (2 * tb)
    out = jax.ShapeDtypeStruct((bx, n1, n2, w), F32)
    return pl.pallas_call(
        _hy_stage1_kernel,
        grid=(bx, n2 // tb),
        in_specs=[pl.BlockSpec(f[0].shape, lambda b, j: (0, 0))] * 2
                 + [pl.BlockSpec((1, k, tb, w), lambda b, j: (b, 0, j, 0))],
        out_specs=[pl.BlockSpec((1, n1, tb, w), lambda b, j: (b, 0, j, 0))] * 2,
        out_shape=[out, out],
        compiler_params=_cp("parallel", "parallel"),
        name="hyena_dft_stage1",
    )(*f, x4)


def _hy_mid_kernel(m1h_ref, m1l_ref, m2h_ref, m2l_ref, re_ref, im_ref, hre_ref, him_ref, ore_ref, oim_ref):
    n2 = re_ref.shape[2]
    for c in range(re_ref.shape[1]):
        a = jnp.concatenate([re_ref[0, c], im_ref[0, c]], axis=0)
        x = _dot3(m1h_ref[c], m1l_ref[c], a)
        xre, xim = x[:n2], x[n2:]
        hre, him = hre_ref[c], him_ref[c]
        y = jnp.concatenate([xre * hre - xim * him, xre * him + xim * hre], axis=0)
        bb = _dot3(m2h_ref[c], m2l_ref[c], y)
        ore_ref[0, c] = bb[:n2]
        oim_ref[0, c] = bb[n2:]


def _hy_spec_kernel(m1h_ref, m1l_ref, re_ref, im_ref, rs_ref, ore_ref, oim_ref):
    n2 = re_ref.shape[2]
    for c in range(re_ref.shape[1]):
        a = jnp.concatenate([re_ref[0, c], im_ref[0, c]], axis=0)
        x = _dot3(m1h_ref[c], m1l_ref[c], a) * rs_ref[...]
        ore_ref[c] = x[:n2]
        oim_ref[c] = x[n2:]


def _hy_filter_spectrum(m1, are, aim, rs):
    _, n1, n2, w = are.shape
    cb = min(4, n1)
    blk = pl.BlockSpec((1, cb, n2, w), lambda c: (0, c, 0, 0))
    mblk = pl.BlockSpec((cb, 2 * n2, 2 * n2), lambda c: (c, 0, 0))
    oblk = pl.BlockSpec((cb, n2, w), lambda c: (c, 0, 0))
    out = jax.ShapeDtypeStruct((n1, n2, w), F32)
    return pl.pallas_call(
        _hy_spec_kernel,
        grid=(n1 // cb,),
        in_specs=[mblk, mblk, blk, blk, pl.BlockSpec((1, w), lambda c: (0, 0))],
        out_specs=[oblk, oblk],
        out_shape=[out, out],
        compiler_params=_cp("parallel"),
        name="hyena_filter_spectrum",
    )(*m1, are, aim, rs)


def _hy_mid(m1, m2, are, aim, hre, him):
    bx, n1, n2, w = are.shape
    cb = min(4, n1)
    blk = pl.BlockSpec((1, cb, n2, w), lambda c, b: (b, c, 0, 0))
    mblk = pl.BlockSpec((cb, 2 * n2, 2 * n2), lambda c, b: (c, 0, 0))
    hblk = pl.BlockSpec((cb, n2, w), lambda c, b: (c, 0, 0))
    out = jax.ShapeDtypeStruct((bx, n1, n2, w), F32)
    return pl.pallas_call(
        _hy_mid_kernel,
        grid=(n1 // cb, bx),
        in_specs=[mblk, mblk, mblk, mblk, blk, blk, hblk, hblk],
        out_specs=[blk, blk],
        out_shape=[out, out],
        compiler_params=_cp("parallel", "parallel"),
        name="hyena_dft_mid",
    )(*m1, *m2, are, aim, hre, him)


def _hy_last_kernel(fh_ref, fl_ref, re_ref, im_ref, p_ref, bias_ref, o_ref):
    _, k, tb, w = p_ref.shape
    spec = jnp.concatenate([re_ref[0].reshape(-1, w), im_ref[0].reshape(-1, w)], axis=0)
    y = _dot3(fh_ref[...], fl_ref[...], spec)
    o_ref[0] = y.reshape(k, tb, w) + p_ref[0] * bias_ref[...]


def _hy_last(f4, bre, bim, p4, bias):
    bx, n1, n2, w = bre.shape
    k = n1 // 2
    tb = HY_TB
    big = pl.BlockSpec((1, n1, tb, w), lambda b, j: (b, 0, j, 0))
    small = pl.BlockSpec((1, k, tb, w), lambda b, j: (b, 0, j, 0))
    return pl.pallas_call(
        _hy_last_kernel,
        grid=(bx, n2 // tb),
        in_specs=[pl.BlockSpec(f4[0].shape, lambda b, j: (0, 0))] * 2 + [big, big, small,
                  pl.BlockSpec((1, w), lambda b, j: (0, 0))],
        out_specs=small,
        out_shape=jax.ShapeDtypeStruct((bx, k, n2, w), F32),
        compiler_params=_cp("parallel", "parallel"),
        name="hyena_dft_last",
    )(*f4, bre, bim, p4, bias)


def _hyena(zb, conv_w, conv_b, filt_params, hy_bias, consts):
    bx, lx, _ = zb.shape
    feat, (f1, m1, m2, f4) = consts
    n = 2 * lx
    n1, n2 = _dft_factors(n)
    w = HY_WIDTH
    filt, ssq = _hy_filter(lx, feat, *filt_params)
    rs = lax.rsqrt(ssq + EPS)
    fre, fim = _hy_stage1(_hi_lo(_per_offset(f1)), filt.reshape(1, n1, n2, w))
    hre, him = _hy_filter_spectrum(m1, fre, fim, rs)
    p, x0 = _hy_prep(zb, conv_w, conv_b)
    k = n1 // 2
    p4 = p.reshape(bx, k, n2, w)
    are, aim = _hy_stage1(_hi_lo(_per_offset(f1[:, :k])), p4)
    bre, bim = _hy_mid(m1, m2, are, aim, hre, him)
    t = _hy_last(f4, bre, bim, p4, hy_bias.reshape(1, w))
    return x0, t.reshape(bx, lx, w)


WA_SHIFT_MAX = 35.0


def _wattn_kernel(sk_ref, q_ref, bias_ref, *rest, banded, fixed):
    if banded:
        kp_ref, kc_ref, kn_ref, kx_ref, vp_ref, vc_ref, vn_ref, vx_ref, o_ref = rest
    else:
        kx_ref, vx_ref, o_ref = rest
    h = pl.program_id(1)
    i = pl.program_id(2)
    last = pl.num_programs(2) - 1
    qb = q_ref.shape[1]
    q = jnp.concatenate([q_ref[0, :, :LANES], q_ref[0, :, LANES:]], axis=0)
    if banded:
        kk = jnp.concatenate([kp_ref[0], kc_ref[0], kn_ref[0], kx_ref[0]], axis=0)
        vv = jnp.concatenate([vp_ref[0], vc_ref[0], vn_ref[0], vx_ref[0]], axis=0)
    else:
        kk, vv = kx_ref[0], vx_ref[0]
    s = lax.dot_general(q, kk, (((1,), (1,)), ((), ())), preferred_element_type=F32) + bias_ref[0]
    if banded:
        c = lax.broadcasted_iota(jnp.int32, (1, kk.shape[0]), 1) - BLOCK
        outside = ((c < 0) & (i == 0)) | ((c >= qb) & (c < qb + BLOCK) & (i == last))
        s = s + jnp.where(outside, NEG_INF, 0.0)
    top = lax.broadcasted_iota(jnp.int32, (2 * qb, 1), 0) < qb
    sk = jnp.where(top, sk_ref[h * WA_GROUP], sk_ref[h * WA_GROUP + 1])
    if fixed:
        p, sink_term = jnp.exp2(s), sk
    else:
        m = jnp.maximum(jnp.max(s, axis=-1, keepdims=True), sk)
        p, sink_term = jnp.exp2(s - m), jnp.exp2(sk - m)
    acc = jnp.dot(p.astype(BF16), vv, preferred_element_type=F32)
    o = (acc * (1.0 / (acc[:, WA_DIM:WA_DIM + 1] + sink_term))).astype(BF16)
    o_ref[0] = jnp.concatenate([o[:qb], o[qb:]], axis=1)


def _window_attention(q, k, v, kx, vx, sink, bound, banded):
    b, lq, _ = q.shape
    cx = kx.shape[1]
    qb = min(256, lq)
    per = qb // BLOCK
    nblk = lq // BLOCK
    side = lambda f: pl.BlockSpec((1, BLOCK, LANES), f)
    prev = side(lambda b_, h, i: (b_, jnp.maximum(i * per - 1, 0), h))
    nxt = side(lambda b_, h, i: (b_, jnp.minimum((i + 1) * per, nblk - 1), h))
    cur = pl.BlockSpec((1, qb, LANES), lambda b_, h, i: (b_, i, h))
    ctx = pl.BlockSpec((1, cx, LANES), lambda b_, h, i: (b_, 0, h))
    qspec = pl.BlockSpec((1, qb, WA_GROUP * LANES), lambda b_, h, i: (b_, i, h))
    mask = jnp.zeros((WA_GROUP * qb, cx), F32)
    if banded:
        r = jnp.arange(WA_GROUP * qb)[:, None] % qb
        c = jnp.arange(qb + 2 * BLOCK)[None, :] - BLOCK
        mask = jnp.concatenate([jnp.where(jnp.abs(r - c) <= BLOCK, 0.0, NEG_INF).astype(F32), mask], axis=1)
        specs, args = [prev, cur, nxt, ctx, prev, cur, nxt, ctx], (k, k, k, kx, v, v, v, vx)
    else:
        specs, args = [ctx, ctx], (kx, vx)
    fixed = bound <= WA_SHIFT_MAX
    shift = jnp.maximum(bound, sink)
    rows = jnp.repeat(shift.reshape(WA_KV_HEADS, WA_GROUP), qb, axis=1)[:, :, None]
    bias = mask[None] - jnp.where(fixed, LOG2E * rows, 0.0)
    sk = jnp.where(fixed, jnp.exp2(LOG2E * (sink - shift)), LOG2E * sink)

    def call(fixed_, *ops):
        return pl.pallas_call(
            functools.partial(_wattn_kernel, banded=banded, fixed=fixed_),
            grid=(b, WA_KV_HEADS, lq // qb),
            in_specs=[pl.BlockSpec(memory_space=pltpu.SMEM), qspec,
                      pl.BlockSpec((1,) + bias.shape[1:], lambda b_, h, i: (h, 0, 0))] + specs,
            out_specs=qspec,
            out_shape=jax.ShapeDtypeStruct((b, lq, WA_HEADS * LANES), BF16),
            compiler_params=_cp("parallel", "parallel", "arbitrary"),
            name="window_attention" if fixed_ else "window_attention_online",
        )(*ops)

    return lax.cond(fixed, functools.partial(call, True), functools.partial(call, False), sk, q, bias, *args)


def _conf_kernel(zp_ref, zc_ref, zn_ref, w_ref, b_ref, lg_ref, lb_ref, o_ref, ext, sh, *, halo):
    i = pl.program_id(1)
    last = pl.num_programs(1) - 1
    tl = zc_ref.shape[1]
    cw = CF_WIDTH

    def glu(z_ref):
        z = z_ref[0].astype(F32)
        return z[:, :cw] * jax.nn.sigmoid(z[:, cw:])

    ext[0:halo] = jnp.where(i == 0, 0.0, glu(zp_ref))
    ext[halo:halo + tl] = glu(zc_ref)
    ext[halo + tl:] = jnp.where(i == last, 0.0, glu(zn_ref))
    for r in range(1, SUBLANES):
        sh[r - 1] = ext[pl.ds(r, sh.shape[1]), :]
    w = w_ref[...]
    u = jnp.zeros((tl, cw), F32) + b_ref[...]
    for j in range(CF_TAPS):
        off = halo - CF_TAPS // 2 + j
        base, r = off // SUBLANES * SUBLANES, off % SUBLANES
        tap = ext[pl.ds(base, tl), :] if r == 0 else sh[r - 1, pl.ds(base, tl), :]
        u = u + tap * w[j:j + 1]
    uc = u - jnp.mean(u, axis=-1, keepdims=True)
    y = uc * lax.rsqrt(jnp.mean(uc * uc, axis=-1, keepdims=True) + EPS) * lg_ref[...] + lb_ref[...]
    o_ref[0] = _silu(y).astype(BF16)


def _conformer(zd, dw_w, dw_b, ln_g, ln_b):
    bx, lx, w = zd.shape
    tl = min(256, lx)
    halo = 2 * SUBLANES
    nh = lx // halo
    per = tl // halo
    row = lambda a: a.reshape(1, -1)
    const = lambda b, i: (0, 0)
    return pl.pallas_call(
        functools.partial(_conf_kernel, halo=halo),
        grid=(bx, lx // tl),
        in_specs=[pl.BlockSpec((1, halo, w), lambda b, i: (b, jnp.maximum(i * per - 1, 0), 0)),
                  pl.BlockSpec((1, tl, w), lambda b, i: (b, i, 0)),
                  pl.BlockSpec((1, halo, w), lambda b, i: (b, jnp.minimum((i + 1) * per, nh - 1), 0)),
                  pl.BlockSpec(dw_w.shape, const)] + [pl.BlockSpec((1, CF_WIDTH), const)] * 3,
        out_specs=pl.BlockSpec((1, tl, CF_WIDTH), lambda b, i: (b, i, 0)),
        out_shape=jax.ShapeDtypeStruct((bx, lx, CF_WIDTH), BF16),
        scratch_shapes=[pltpu.VMEM((tl + 2 * halo, CF_WIDTH), F32),
                        pltpu.VMEM((SUBLANES - 1, tl + 2 * halo - SUBLANES, CF_WIDTH), F32)],
        compiler_params=_cp("parallel", "parallel"),
        name="conformer_conv",
    )(zd, zd, zd, dw_w, row(dw_b), row(ln_g), row(ln_b))


def _route_rows(x, m_ref, g_ref, wr_ref, br_ref, rows_ref, gi_ref):
    tm, d = x.shape
    lane = lax.broadcasted_iota(jnp.int32, (tm, LANES), 1).astype(F32)
    h = _rms(x) * g_ref[...] * (1.0 + m_ref[0, :, 4 * d:5 * d]) + m_ref[0, :, 3 * d:4 * d]
    rows_ref[0, :, :d] = h
    lg = _dot3(*_hi_lo(h), wr_ref[...]) + br_ref[...]
    isg = lane < N_GROUPS
    gmax = jnp.max(jnp.where(isg, lg, NEG_INF), axis=-1, keepdims=True)
    gi = jnp.min(jnp.where(isg & (lg == gmax), lane, LANES), axis=-1, keepdims=True)
    gw = 1.0 / jnp.sum(jnp.where(isg, jnp.exp(lg - gmax), 0.0), axis=-1, keepdims=True)
    lo = N_GROUPS + gi * EXP_PER_GROUP
    ise = (lane >= lo) & (lane < lo + EXP_PER_GROUP)
    le = jnp.where(ise, lg, NEG_INF)
    m1 = jnp.max(le, axis=-1, keepdims=True)
    i1 = jnp.min(jnp.where(ise & (le == m1), lane, LANES), axis=-1, keepdims=True)
    ise2 = ise & (lane != i1)
    le2 = jnp.where(ise2, lg, NEG_INF)
    m2 = jnp.max(le2, axis=-1, keepdims=True)
    i2 = jnp.min(jnp.where(ise2 & (le2 == m2), lane, LANES), axis=-1, keepdims=True)
    r = jnp.exp(m2 - m1)
    wa = gw / (1.0 + r)
    rows_ref[0, :, d:] = jnp.where(lane == i1 - lo, wa, 0.0) + jnp.where(lane == i2 - lo, wa * r, 0.0)
    gi_ref[0] = gi.astype(jnp.int32)


def _merge_kernel(x_ref, m_ref, ya, yb0, ybt, yc, yd, zg_ref, bg_ref, wa, wb, wc, wd, wo_ref, *rest, goff):
    o_ref = rest[-1] if len(rest) == 1 else rest[3]
    d = x_ref.shape[-1]
    acc = jnp.zeros(x_ref.shape[1:], F32)
    ys = (ya[0], (yb0[0] * ybt[0]).astype(BF16), yc[0], yd[0])
    for i, (y, w) in enumerate(zip(ys, (wa, wb, wc, wd))):
        gate = jax.nn.sigmoid(zg_ref[0, :, i * d:(i + 1) * d].astype(F32) + bg_ref[:, i * d:(i + 1) * d])
        acc = acc + gate * jnp.dot(y, w[...], preferred_element_type=F32)
    out = jnp.dot(acc.astype(BF16), wo_ref[...], preferred_element_type=F32)
    xn = x_ref[0] + m_ref[0, :, goff:goff + d] * out
    o_ref[0] = xn
    if len(rest) > 1:
        g_ref, wr_ref, br_ref, _, rows_ref, gi_ref = rest
        _route_rows(xn, m_ref, g_ref, wr_ref, br_ref, rows_ref, gi_ref)


def _merge(x, mod, ys, zg, b_gate, wbs, w_out, router=None):
    bx, lx, d = x.shape
    tm = min(512, lx)
    row = lambda b, i: (b, i, 0)
    const = lambda b, i: (0, 0)
    once = lambda a: pl.BlockSpec(a.shape, const, pipeline_mode=pl.Buffered(1))
    extra = (router[0].reshape(1, d), router[1], router[2]) if router else ()
    out_specs = [pl.BlockSpec((1, tm, d), row), pl.BlockSpec((1, tm, d + LANES), row), pl.BlockSpec((1, tm, 1), row)]
    out_shape = [jax.ShapeDtypeStruct(x.shape, F32), jax.ShapeDtypeStruct((bx, lx, d + LANES), F32),
                 jax.ShapeDtypeStruct((bx, lx, 1), jnp.int32)]
    return pl.pallas_call(
        functools.partial(_merge_kernel, goff=2 * d),
        grid=(bx, lx // tm),
        in_specs=[pl.BlockSpec((1, tm, d), row), pl.BlockSpec((1, 1, mod.shape[-1]), lambda b, i: (b, 0, 0))]
                 + [pl.BlockSpec((1, tm, y.shape[-1]), row) for y in ys]
                 + [pl.BlockSpec((1, tm, N_BRANCH * d), row), pl.BlockSpec((1, N_BRANCH * d), const)]
                 + [once(w) for w in wbs] + [once(w_out)] + [once(a) for a in extra],
        out_specs=out_specs if router else out_specs[0],
        out_shape=out_shape if router else out_shape[0],
        input_output_aliases={0: 0},
        compiler_params=_cp("parallel", "parallel"),
        name="merge",
    )(x, mod, *ys, zg, b_gate.reshape(1, -1), *wbs, w_out, *extra)


def _slot_rows(w, group):
    n = w.shape[0] // group
    return jnp.pad(w.reshape(n, group, -1), ((0, 0), (0, LANES - group), (0, 0))).reshape(n * LANES, -1)


def _moe_kernel(x_ref, m_ref, g_ref, wr_ref, br_ref, w1_ref, w3_ref, w2_ref, o_ref, h_sc, gate_sc, acc_sc):
    e = pl.program_id(2)
    d = x_ref.shape[-1]
    tm = x_ref.shape[1]
    lane = lax.broadcasted_iota(jnp.int32, (tm, LANES), 1).astype(F32)

    @pl.when(e == 0)
    def _():
        h = _rms(x_ref[0]) * g_ref[...] * (1.0 + m_ref[0, :, 4 * d:5 * d]) + m_ref[0, :, 3 * d:4 * d]
        h_sc[...] = h.astype(BF16)
        lg = _dot3(*_hi_lo(h), wr_ref[...]) + br_ref[...]
        isg = lane < N_GROUPS
        gmax = jnp.max(jnp.where(isg, lg, NEG_INF), axis=-1, keepdims=True)
        gi = jnp.min(jnp.where(isg & (lg == gmax), lane, LANES), axis=-1, keepdims=True)
        gw = 1.0 / jnp.sum(jnp.where(isg, jnp.exp(lg - gmax), 0.0), axis=-1, keepdims=True)
        lo = N_GROUPS + gi * EXP_PER_GROUP
        ise = (lane >= lo) & (lane < lo + EXP_PER_GROUP)
        le = jnp.where(ise, lg, NEG_INF)
        m1 = jnp.max(le, axis=-1, keepdims=True)
        i1 = jnp.min(jnp.where(ise & (le == m1), lane, LANES), axis=-1, keepdims=True)
        ise2 = ise & (lane != i1)
        le2 = jnp.where(ise2, lg, NEG_INF)
        m2 = jnp.max(le2, axis=-1, keepdims=True)
        i2 = jnp.min(jnp.where(ise2 & (le2 == m2), lane, LANES), axis=-1, keepdims=True)
        r = jnp.exp(m2 - m1)
        wa = gw / (1.0 + r)
        gate_sc[...] = jnp.where(lane == i1, wa, 0.0) + jnp.where(lane == i2, wa * r, 0.0)
        acc_sc[...] = jnp.zeros_like(acc_sc)

    hb = h_sc[...]
    u = (_silu(jnp.dot(hb, w1_ref[0].astype(BF16), preferred_element_type=F32))
         * jnp.dot(hb, w3_ref[0].astype(BF16), preferred_element_type=F32))
    ge = jnp.sum(jnp.where(lane == (e + N_GROUPS).astype(F32), gate_sc[...], 0.0), axis=-1, keepdims=True)
    acc_sc[...] += ge * jnp.dot(u.astype(BF16), w2_ref[0].astype(BF16), preferred_element_type=F32)

    @pl.when(e == pl.num_programs(2) - 1)
    def _():
        o_ref[0] = x_ref[0] + m_ref[0, :, 5 * d:6 * d] * acc_sc[...]


def _moe(x, mod, g, w_router, b_router, w1, w3, w2):
    bx, lx, d = x.shape
    tm = min(1024, lx)
    ne, _, f = w1.shape
    row = lambda b, i, e: (b, i, 0)
    const = lambda b, i, e: (0, 0)
    return pl.pallas_call(
        _moe_kernel,
        grid=(bx, lx // tm, ne),
        in_specs=[pl.BlockSpec((1, tm, d), row), pl.BlockSpec((1, 1, mod.shape[-1]), lambda b, i, e: (b, 0, 0)),
                  pl.BlockSpec((1, d), const), pl.BlockSpec((d, LANES), const), pl.BlockSpec((1, LANES), const),
                  pl.BlockSpec((1, d, f), lambda b, i, e: (e, 0, 0)), pl.BlockSpec((1, d, f), lambda b, i, e: (e, 0, 0)),
                  pl.BlockSpec((1, f, d), lambda b, i, e: (e, 0, 0))],
        out_specs=pl.BlockSpec((1, tm, d), row),
        out_shape=jax.ShapeDtypeStruct(x.shape, F32),
        scratch_shapes=[pltpu.VMEM((tm, d), BF16), pltpu.VMEM((tm, LANES), F32), pltpu.VMEM((tm, d), F32)],
        input_output_aliases={0: 0},
        compiler_params=_cp("parallel", "parallel", "arbitrary"),
        name="moe",
    )(x, mod, g.reshape(1, d), w_router, b_router, w1, w3, w2)


SC_CORES = 2
SC_SUBCORES = 16
SC_CHUNK = 64
MOE_ROWS = 1024


def _sc_gather(table, idx):
    n = idx.shape[0]
    w = table.shape[1]
    per = n // (SC_CORES * SC_SUBCORES)
    assert per * SC_CORES * SC_SUBCORES == n and per % SC_CHUNK == 0
    mesh = plsc.VectorSubcoreMesh(core_axis_name="c", subcore_axis_name="s")

    @functools.partial(
        pl.kernel, mesh=mesh, out_type=jax.ShapeDtypeStruct((n, w), table.dtype),
        scratch_types=[pltpu.VMEM((SC_CHUNK,), jnp.int32), pltpu.VMEM((SC_CHUNK, w), table.dtype),
                       pltpu.SemaphoreType.DMA],
        name="sc_row_gather")
    def gather(table_hbm, idx_hbm, out_hbm, idx_v, rows_v, sem):
        base = (lax.axis_index("s") * SC_CORES + lax.axis_index("c")) * per

        @pl.loop(0, per // SC_CHUNK)
        def _(j):
            off = pl.multiple_of(base + j * SC_CHUNK, SC_CHUNK)
            pltpu.sync_copy(idx_hbm.at[pl.ds(off, SC_CHUNK)], idx_v)
            pltpu.async_copy(table_hbm.at[idx_v], rows_v, sem).wait()
            pltpu.sync_copy(rows_v, out_hbm.at[pl.ds(off, SC_CHUNK)])

    return gather(table, idx)


def _sc_scatter(rows, idx, n_out):
    n, w = rows.shape
    per = n // (SC_CORES * SC_SUBCORES)
    assert per * SC_CORES * SC_SUBCORES == n and per % SC_CHUNK == 0
    mesh = plsc.VectorSubcoreMesh(core_axis_name="c", subcore_axis_name="s")

    @functools.partial(
        pl.kernel, mesh=mesh, out_type=jax.ShapeDtypeStruct((n_out, w), rows.dtype),
        scratch_types=[pltpu.VMEM((SC_CHUNK,), jnp.int32), pltpu.VMEM((SC_CHUNK, w), rows.dtype),
                       pltpu.SemaphoreType.DMA],
        name="sc_row_scatter")
    def scatter(rows_hbm, idx_hbm, out_hbm, idx_v, rows_v, sem):
        base = (lax.axis_index("s") * SC_CORES + lax.axis_index("c")) * per

        @pl.loop(0, per // SC_CHUNK)
        def _(j):
            off = pl.multiple_of(base + j * SC_CHUNK, SC_CHUNK)
            pltpu.sync_copy(idx_hbm.at[pl.ds(off, SC_CHUNK)], idx_v)
            pltpu.sync_copy(rows_hbm.at[pl.ds(off, SC_CHUNK)], rows_v)
            pltpu.async_copy(rows_v, out_hbm.at[idx_v], sem).wait()

    return scatter(rows, idx)


def _gmoe_kernel(grp_ref, nv_ref, xs_ref, w1_ref, w3_ref, w2_ref, o_ref, h_sc, acc_sc):
    i = pl.program_id(0)
    e = pl.program_id(1)
    tm, d = h_sc.shape
    valid = lax.broadcasted_iota(jnp.int32, (tm, 1), 0) < nv_ref[i]

    @pl.when(nv_ref[i] > 0)
    def _():
        @pl.when(e == 0)
        def _():
            h_sc[...] = jnp.where(valid, xs_ref[:, :d], 0.0).astype(BF16)
            acc_sc[...] = jnp.zeros_like(acc_sc)

        hb = h_sc[...]
        u = (_silu(jnp.dot(hb, w1_ref[0].astype(BF16), preferred_element_type=F32))
             * jnp.dot(hb, w3_ref[0].astype(BF16), preferred_element_type=F32))
        lane = lax.broadcasted_iota(jnp.int32, (tm, LANES), 1)
        ge = jnp.sum(jnp.where(valid & (lane == e), xs_ref[:, d:], 0.0), axis=-1, keepdims=True)
        acc_sc[...] += ge * jnp.dot(u.astype(BF16), w2_ref[0].astype(BF16), preferred_element_type=F32)

    @pl.when(e == pl.num_programs(1) - 1)
    def _():
        o_ref[...] = jnp.where(nv_ref[i] > 0, acc_sc[...], 0.0)


def _grouped_moe(grp, nv, xs, w1, w3, w2):
    p, dw = xs.shape
    d = dw - LANES
    _, _, f = w1.shape
    wmap = lambda i, e, grp, nv: (grp[i] * EXP_PER_GROUP + e, 0, 0)
    rows = lambda i, e, grp, nv: (i, 0)
    return pl.pallas_call(
        _gmoe_kernel,
        grid_spec=pltpu.PrefetchScalarGridSpec(
            num_scalar_prefetch=2,
            grid=(p // MOE_ROWS, EXP_PER_GROUP),
            in_specs=[pl.BlockSpec((MOE_ROWS, dw), rows),
                      pl.BlockSpec((1, d, f), wmap), pl.BlockSpec((1, d, f), wmap), pl.BlockSpec((1, f, d), wmap)],
            out_specs=pl.BlockSpec((MOE_ROWS, d), rows),
            scratch_shapes=[pltpu.VMEM((MOE_ROWS, d), BF16), pltpu.VMEM((MOE_ROWS, d), F32)]),
        out_shape=jax.ShapeDtypeStruct((p, d), F32),
        compiler_params=_cp("arbitrary", "arbitrary"),
        name="moe_experts",
    )(grp, nv, xs, w1, w3, w2)


def _residual_kernel(x_ref, m_ref, y_ref, o_ref):
    d = x_ref.shape[-1]
    o_ref[0] = x_ref[0] + m_ref[0, :, 5 * d:6 * d] * y_ref[0]


def _residual(x, mod, y):
    bx, lx, d = x.shape
    tm = min(1024, lx)
    row = lambda b, i: (b, i, 0)
    return pl.pallas_call(
        _residual_kernel,
        grid=(bx, lx // tm),
        in_specs=[pl.BlockSpec((1, tm, d), row), pl.BlockSpec((1, 1, mod.shape[-1]), lambda b, i: (b, 0, 0)),
                  pl.BlockSpec((1, tm, d), row)],
        out_specs=pl.BlockSpec((1, tm, d), row),
        out_shape=jax.ShapeDtypeStruct(x.shape, F32),
        input_output_aliases={0: 0},
        compiler_params=_cp("parallel", "parallel"),
        name="moe_residual",
    )(x, mod, y)


def _moe_sorted(x, mod, rows, gi, w1, w3, w2):
    bx, lx, d = x.shape
    t = bx * lx
    gi = gi.reshape(t)
    onehot = (gi[:, None] == jnp.arange(N_GROUPS, dtype=jnp.int32)[None, :]).astype(jnp.int32)
    csum = jnp.cumsum(onehot, axis=0)
    counts = csum[-1]
    rank = jnp.take_along_axis(csum, gi[:, None], axis=1)[:, 0] - 1
    padded = (counts + MOE_ROWS - 1) // MOE_ROWS * MOE_ROWS
    pend = jnp.cumsum(padded)
    pstart = pend - padded
    pos = (pstart[gi] + rank).astype(jnp.int32)
    p = t + N_GROUPS * MOE_ROWS
    bstart = jnp.arange(p // MOE_ROWS, dtype=jnp.int32) * MOE_ROWS
    grp = jnp.minimum(jnp.searchsorted(pend, bstart, side="right"), N_GROUPS - 1).astype(jnp.int32)
    nv = jnp.clip(pstart[grp] + counts[grp] - bstart, 0, MOE_ROWS).astype(jnp.int32)
    xs = _sc_scatter(rows.reshape(t, d + LANES), pos, p)
    ys = _grouped_moe(grp, nv, xs, w1, w3, w2)
    yt = _sc_gather(ys, pos)
    return _residual(x, mod, yt.reshape(bx, lx, d))


def kernel(x, c, ctx, c_ctx, w_mod, b_mod, norm1_g, norm2_g, w_in, b_gate, da_qn, da_kn, da_lam, da_subln, hy_conv_w, hy_conv_b, hf_w1, hf_b1, hf_w2, hf_b2, hf_w3, hf_b3, hf_w4, hf_freq, hy_bias, wa_qn, wa_kn, wa_sink, cf_dw_w, cf_dw_b, cf_ln_g, cf_ln_b, w_branch, w_out, w_rg, b_rg, w_re, b_re, w1, w3, w2):
    b, s, d = x.shape
    cl = ctx.shape[1]
    depth = w_mod.shape[0]
    assert s % 256 == 0 and cl % 256 == 0 and s % GRID_W == 0

    nrow = -(-(b + 1) // SUBLANES) * SUBLANES
    crows = jnp.zeros((nrow, d), F32).at[:b].set(c).at[b].set(c_ctx)
    mods = _mod_vectors(crows, w_mod, b_mod)

    aw, qw, kw = DA_HEADS * DA_DIM, WA_HEADS * WA_DIM, WA_KV_HEADS * WA_DIM
    tab_lat = (*_rope_tables(s, DA_DIM, DA_HEADS), *_rope_tables(s, WA_DIM, WA_HEADS), *_rope_tables(s, WA_DIM, WA_KV_HEADS))
    tab_ctx = (*_unit_tables(cl, aw), *_unit_tables(cl, qw), *_unit_tables(cl, kw))
    hy_lat = (_hy_feat(s), _dft_tables(2 * s))
    hy_ctx = (_hy_feat(cl), _dft_tables(2 * cl))
    gms = (_group_ones(aw, DA_DIM), _group_ones(qw, WA_DIM), _group_ones(kw, WA_DIM))
    qvec = _slot_fill(DA_DIM, DA_DIM + 1, 1.0)
    vvec = _slot_fill(DA_VDIM, DA_VDIM + DA_ONES, 1.0)

    w_in_b = w_in.astype(BF16)
    xc = ctx
    for l in range(depth):
        last = l == depth - 1
        lam_init = 0.8 - 0.6 * math.exp(-0.3 * l)
        mod_x = mods[l, :b][:, None, :]
        mod_c = jnp.broadcast_to(mods[l, b][None, None, :], (b, 1, mods.shape[-1]))
        shift = 1.02 * LOG2E * DA_DIM ** 0.5 * jnp.max(jnp.abs(da_qn[l])) * jnp.max(jnp.abs(da_kn[l]))
        fixed = shift <= DA_SHIFT_MAX
        kvec = _slot_fill(DA_DIM, DA_DIM + 1, jnp.where(fixed, -shift, 0.0))
        tile = lambda a, n: jnp.tile(a, n).reshape(1, -1)
        consts = (*gms, tile(da_qn[l], DA_HEADS), tile(da_kn[l], DA_HEADS), tile(wa_qn[l], WA_HEADS),
                  tile(wa_kn[l], WA_KV_HEADS), qvec, kvec, vvec)
        q1, q2, k1, k2, v, zb, qc, kc, vc, zd, zg = _in_proj(x, mod_x, norm1_g[l], w_in_b, l, tab_lat, consts)
        q1x, q2x, k1x, k2x, vx, zbx, qcx, kcx, vcx, zdx, zgx = _in_proj(xc, mod_c, norm1_g[l], w_in_b, l, tab_ctx, consts)
        filt_params = (hf_w1[l], hf_b1[l], hf_w2[l], hf_b2[l], hf_w3[l], hf_b3[l], hf_w4[l], hf_freq[l])
        wb4 = w_branch[l].astype(BF16)
        wbs = (_slot_rows(wb4[0], DA_VDIM), wb4[1], _slot_rows(wb4[2], WA_DIM), wb4[3])
        wo = w_out[l].astype(BF16)

        ya = _diff_attention(q1, q2, [(k1, k2, v), (k1x, k2x, vx)], fixed, da_lam[l], da_subln[l], lam_init)
        yb = _hyena(zb, hy_conv_w[l], hy_conv_b[l], filt_params, hy_bias[l], hy_lat)
        wbound = 1.02 * WA_DIM ** 0.5 * jnp.max(jnp.abs(wa_qn[l])) * jnp.max(jnp.abs(wa_kn[l]))
        yc_ = _window_attention(qc, kc, vc, kcx, vcx, wa_sink[l], wbound, True)
        yd = _conformer(zd, cf_dw_w[l], cf_dw_b[l], cf_ln_g[l], cf_ln_b[l])
        w_router = jnp.pad(jnp.concatenate([w_rg[l], w_re[l]], axis=1), ((0, 0), (0, LANES - N_GROUPS - N_EXPERTS)))
        b_router = jnp.pad(jnp.concatenate([b_rg[l], b_re[l]]), (0, LANES - N_GROUPS - N_EXPERTS)).reshape(1, LANES)
        ew = (w1[l], w3[l], w2[l])
        x, rows, gi = _merge(x, mod_x, (ya, *yb, yc_, yd), zg, b_gate[l], wbs, wo, (norm2_g[l], w_router, b_router))

        if not last:
            yca = _diff_attention(q1x, q2x, [(k1x, k2x, vx)], fixed, da_lam[l], da_subln[l], lam_init)
            ycb = _hyena(zbx, hy_conv_w[l], hy_conv_b[l], filt_params, hy_bias[l], hy_ctx)
            ycc = _window_attention(qcx, kcx, vcx, kcx, vcx, wa_sink[l], wbound, False)
            ycd = _conformer(zdx, cf_dw_w[l], cf_dw_b[l], cf_ln_g[l], cf_ln_b[l])
            xc = _merge(xc, mod_c, (yca, *ycb, ycc, ycd), zgx, b_gate[l], wbs, wo)
            xc = _moe(xc.reshape(1, b * cl, d), mod_c[:1], norm2_g[l], w_router, b_router, *ew).reshape(b, cl, d)
        x = _moe_sorted(x, mod_x, rows, gi, *ew)
    return x
```

```python
import functools
import math

import jax
import jax.numpy as jnp
from jax import lax
from jax.experimental import pallas as pl
from jax.experimental.pallas import tpu as pltpu
from jax.experimental.pallas import tpu_sc as plsc

F32 = jnp.float32
BF16 = jnp.bfloat16
HI = lax.Precision.HIGHEST

GRID_W = 64
BLOCK = 128
ROPE_BASE = 10000.0
EPS = 1e-6
NEG_INF = -1e30

DA_HEADS = 4
DA_DIM = 32
DA_VDIM = 64
HY_WIDTH = 256
HY_BANDS = 16
HY_FF = 64
HY_SHIFT = 0.05
HY_FAST_DECAY = 0.3
HY_SLOW_DECAY = 1.5
HY_TARGET = 1e-2
WA_HEADS = 4
WA_KV_HEADS = 2
WA_GROUP = 2
WA_DIM = 64
CF_WIDTH = 256
CF_TAPS = 31
N_BRANCH = 4
BRANCH_W = 256
N_GROUPS = 4
EXP_PER_GROUP = 4
N_EXPERTS = 16

W_A = 4 * DA_HEADS * DA_DIM + DA_HEADS * DA_VDIM
W_B = 3 * HY_WIDTH
W_C = (WA_HEADS + 2 * WA_KV_HEADS) * WA_DIM
W_D = 2 * CF_WIDTH
OFF_B = W_A
OFF_C = OFF_B + W_B
OFF_D = OFF_C + W_C
OFF_G = OFF_D + W_D

LOG2E = math.log2(math.e)
LANES = 128
SUBLANES = 8
VMEM_LIMIT = 56 * 1024 * 1024

DA_ONES = 16
DA_SHIFT_MAX = 50.0


def _cp(*sem):
    return pltpu.CompilerParams(dimension_semantics=sem, vmem_limit_bytes=VMEM_LIMIT)


def _rms(xf):
    return xf * lax.rsqrt(jnp.mean(xf * xf, axis=-1, keepdims=True) + EPS)


def _silu(x):
    return x * jax.nn.sigmoid(x)


def _mod_kernel(c_ref, w_ref, b_ref, o_ref):
    s = _silu(c_ref[...])
    o_ref[0] = jnp.dot(s, w_ref[0], precision=HI, preferred_element_type=F32) + b_ref[0]


def _mod_vectors(crows, w_mod, b_mod):
    depth, d, n = w_mod.shape
    r = crows.shape[0]
    tn = 1536
    return pl.pallas_call(
        _mod_kernel,
        grid=(depth, n // tn),
        in_specs=[pl.BlockSpec((r, d), lambda l, j: (0, 0)),
                  pl.BlockSpec((1, d, tn), lambda l, j: (l, 0, j)),
                  pl.BlockSpec((1, 1, tn), lambda l, j: (l, 0, j))],
        out_specs=pl.BlockSpec((1, r, tn), lambda l, j: (l, 0, j)),
        out_shape=jax.ShapeDtypeStruct((depth, r, n), F32),
        compiler_params=_cp("arbitrary", "arbitrary"),
        name="mod_vectors",
    )(crows, w_mod, b_mod.reshape(depth, 1, n))


def _rope_tables(s, d, reps):
    rows = s // GRID_W
    row = jnp.repeat(jnp.arange(rows, dtype=F32), GRID_W)
    col = jnp.tile(jnp.arange(GRID_W, dtype=F32), rows)
    qd = d // 4
    inv = ROPE_BASE ** (-jnp.arange(qd, dtype=F32) / qd)
    ar = row[:, None] * inv[None, :]
    ac = col[:, None] * inv[None, :]
    z = jnp.zeros_like(ar)
    cos = jnp.concatenate([jnp.cos(ar), jnp.cos(ar), jnp.cos(ac), jnp.cos(ac)], axis=-1)
    sin_up = jnp.concatenate([-jnp.sin(ar), z, -jnp.sin(ac), z], axis=-1)
    sin_dn = jnp.concatenate([z, jnp.sin(ar), z, jnp.sin(ac)], axis=-1)
    t = lambda a: jnp.tile(a, (1, reps))
    return t(cos), t(sin_up), t(sin_dn)


def _unit_tables(s, w):
    return jnp.ones((s, w), F32), jnp.zeros((s, w), F32), jnp.zeros((s, w), F32)


def _group_ones(width, group):
    i = jnp.arange(width) // group
    return (i[:, None] == i[None, :]).astype(BF16)


def _hi_lo(a):
    hi = a.astype(BF16)
    return hi, (a - hi.astype(F32)).astype(BF16)


def _slot_fill(lo, hi, value):
    j = jnp.arange(LANES)
    return jnp.where((j >= lo) & (j < hi), value, 0.0).astype(F32).reshape(1, LANES)


def _store_slots(o_ref, y, group, fill):
    per = LANES // group
    lane = lax.broadcasted_iota(jnp.int32, (y.shape[0], LANES), 1)
    for h in range(y.shape[1] // group):
        blk = y[:, h // per * LANES:(h // per + 1) * LANES]
        if h % per:
            blk = pltpu.roll(blk, LANES - h % per * group, 1)
        o_ref[0, :, h * LANES:(h + 1) * LANES] = jnp.where(lane < group, blk, fill).astype(BF16)


def _norm_rope(x, gmat, gain, cos, sup, sdn, group, qd):
    w = x.shape[-1]
    sh, sl = _hi_lo(x * x)
    ss = (jnp.dot(sh, gmat, preferred_element_type=F32) + jnp.dot(sl, gmat, preferred_element_type=F32)) * (1.0 / group)
    xn = x * lax.rsqrt(ss + EPS) * gain
    return xn * cos + pltpu.roll(xn, w - qd, 1) * sup + pltpu.roll(xn, qd, 1) * sdn


def _inproj_kernel(x_ref, m_ref, g_ref, w_ref,
                   ca, ua, da, cq, uq, dq, ck, uk, dk, gma, gmq, gmk, qna, kna, qnc, knc,
                   qvec, kvec, vvec,
                   q1o, q2o, k1o, k2o, vo, zbo, qco, kco, vco, zdo, zgo):
    d = x_ref.shape[-1]
    x = x_ref[0]
    shift = m_ref[0, :, 0:d]
    scale = m_ref[0, :, d:2 * d]
    h = (_rms(x) * g_ref[...] * (1.0 + scale) + shift).astype(BF16)

    za = jnp.dot(h, w_ref[0, :, 0:OFF_B], preferred_element_type=F32)
    hw = DA_HEADS * DA_DIM
    cos, sup, sdn, gm = ca[...], ua[...], da[...], gma[...]
    qscale = DA_DIM ** -0.5 * LOG2E
    for t, (o, gain, sc, vec) in enumerate(((q1o, qna, qscale, qvec), (q2o, qna, qscale, qvec),
                                            (k1o, kna, 1.0, kvec), (k2o, kna, 1.0, kvec))):
        y = _norm_rope(za[:, t * hw:(t + 1) * hw], gm, gain[...], cos, sup, sdn, DA_DIM, DA_DIM // 4) * sc
        _store_slots(o, y, DA_DIM, vec[...])
    _store_slots(vo, za[:, 4 * hw:], DA_VDIM, vvec[...])

    zbo[0] = jnp.dot(h, w_ref[0, :, OFF_B:OFF_C], preferred_element_type=F32).astype(BF16)

    zc = jnp.dot(h, w_ref[0, :, OFF_C:OFF_D], preferred_element_type=F32)
    qw = WA_HEADS * WA_DIM
    kw = WA_KV_HEADS * WA_DIM
    y = _norm_rope(zc[:, 0:qw], gmq[...], qnc[...], cq[...], uq[...], dq[...], WA_DIM, WA_DIM // 4) * (WA_DIM ** -0.5 * LOG2E)
    _store_slots(qco, y, WA_DIM, 0.0)
    y = _norm_rope(zc[:, qw:qw + kw], gmk[...], knc[...], ck[...], uk[...], dk[...], WA_DIM, WA_DIM // 4)
    _store_slots(kco, y, WA_DIM, 0.0)
    _store_slots(vco, zc[:, qw + kw:], WA_DIM, vvec[...])

    zdo[0] = jnp.dot(h, w_ref[0, :, OFF_D:OFF_G], preferred_element_type=F32).astype(BF16)
    for k in range(N_BRANCH):
        zgo[0, :, k * d:(k + 1) * d] = jnp.dot(h, w_ref[0, :, OFF_G + k * d:OFF_G + (k + 1) * d],
                                               preferred_element_type=F32).astype(BF16)


def _in_proj(x, mod, g, w_all, layer, tables, consts):
    bx, lx, d = x.shape
    tm = min(512, lx)
    const = lambda b, i: (0, 0)
    row = lambda b, i: (b, i, 0)
    once = lambda a: pl.BlockSpec(a.shape, const, pipeline_mode=pl.Buffered(1))
    widths = [DA_HEADS * LANES] * 5 + [W_B, WA_HEADS * LANES, WA_KV_HEADS * LANES, WA_KV_HEADS * LANES, W_D, N_BRANCH * d]
    return pl.pallas_call(
        _inproj_kernel,
        grid=(bx, lx // tm),
        in_specs=[pl.BlockSpec((1, tm, d), row), pl.BlockSpec((1, 1, mod.shape[-1]), lambda b, i: (b, 0, 0)),
                  pl.BlockSpec((1, d), const)]
                 + [pl.BlockSpec((1,) + w_all.shape[1:], lambda b, i: (layer, 0, 0), pipeline_mode=pl.Buffered(1))]
                 + [pl.BlockSpec((tm, t.shape[1]), lambda b, i: (i, 0)) for t in tables]
                 + [once(c) for c in consts],
        out_specs=[pl.BlockSpec((1, tm, w), row) for w in widths],
        out_shape=[jax.ShapeDtypeStruct((bx, lx, w), BF16) for w in widths],
        compiler_params=_cp("parallel", "parallel"),
        name="in_proj",
    )(x, mod, g.reshape(1, d), w_all, *tables, *consts)


def _da_lambda(lam_ref, lam_init):
    lv = lam_ref[...]
    return (jnp.exp(jnp.sum(lv[0:1] * lv[1:2], keepdims=True)) - jnp.exp(jnp.sum(lv[2:3] * lv[3:4], keepdims=True))
            + lam_init)


def _dattn_kernel(lam_ref, sg_ref, q1_ref, q2_ref, *rest, lam_init, online, nsrc, tk_max):
    srcs = [rest[3 * s:3 * s + 3] for s in range(nsrc)]
    o_ref, acc1, acc2 = rest[3 * nsrc:]
    dn = (((1,), (1,)), ((), ()))
    q1 = q1_ref[0]
    q2 = q2_ref[0]
    tq = q1.shape[0]
    acc1[...] = jnp.zeros_like(acc1)
    acc2[...] = jnp.zeros_like(acc2)
    carry = (jnp.full((tq, 1), NEG_INF, F32),) * 2 if online else 0

    for k1_ref, k2_ref, v_ref in srcs:
        tk = min(tk_max, k1_ref.shape[1])

        def body(j, c, k1_ref=k1_ref, k2_ref=k2_ref, v_ref=v_ref, tk=tk):
            rows = pl.ds(pl.multiple_of(j * tk, tk), tk)
            vc = v_ref[0, rows, :]
            s1 = lax.dot_general(q1, k1_ref[0, rows, :], dn, preferred_element_type=F32)
            s2 = lax.dot_general(q2, k2_ref[0, rows, :], dn, preferred_element_type=F32)
            if online:
                m1, m2 = c
                n1 = jnp.maximum(m1, jnp.max(s1, axis=-1, keepdims=True))
                n2 = jnp.maximum(m2, jnp.max(s2, axis=-1, keepdims=True))
                acc1[...] = jnp.exp2(m1 - n1) * acc1[...] + jnp.dot(jnp.exp2(s1 - n1).astype(BF16), vc,
                                                                    preferred_element_type=F32)
                acc2[...] = jnp.exp2(m2 - n2) * acc2[...] + jnp.dot(jnp.exp2(s2 - n2).astype(BF16), vc,
                                                                    preferred_element_type=F32)
                return n1, n2
            acc1[...] += jnp.dot(jnp.exp2(s1).astype(BF16), vc, preferred_element_type=F32)
            acc2[...] += jnp.dot(jnp.exp2(s2).astype(BF16), vc, preferred_element_type=F32)
            return c

        carry = lax.fori_loop(0, k1_ref.shape[1] // tk, body, carry)

    dv = DA_VDIM
    a1 = acc1[...]
    a2 = acc2[...]
    lam = _da_lambda(lam_ref, lam_init)
    o = a1 * (1.0 / a1[:, dv:dv + 1]) - a2 * (lam / a2[:, dv:dv + 1])
    o = jnp.where(lax.broadcasted_iota(jnp.int32, o.shape, 1) < dv, o, 0.0)
    o = o * lax.rsqrt(jnp.sum(o * o, axis=-1, keepdims=True) * (1.0 / dv) + EPS)
    o_ref[0] = (o * (sg_ref[...] * (1.0 - lam_init))).astype(BF16)


def _diff_attention(q1, q2, srcs, fixed, lam_p, subln, lam_init):
    b, lq, _ = q1.shape
    h = DA_HEADS
    const = lambda b_, h_, i: (0, 0)
    sg = jnp.pad(subln, (0, LANES - DA_VDIM)).reshape(1, LANES)
    flat = [a for src in srcs for a in src]

    def call(online, *args):
        tq = min(256 if online else 2048, lq)
        qs = pl.BlockSpec((1, tq, LANES), lambda b_, h_, i: (b_, i, h_))
        return pl.pallas_call(
            functools.partial(_dattn_kernel, lam_init=lam_init, online=online, nsrc=len(srcs),
                              tk_max=256 if online else 512),
            grid=(b, h, lq // tq),
            in_specs=[pl.BlockSpec(lam_p.shape, const), pl.BlockSpec((1, LANES), const), qs, qs]
                     + [pl.BlockSpec((1, a.shape[1], LANES), lambda b_, h_, i: (b_, 0, h_)) for a in flat],
            out_specs=qs,
            out_shape=jax.ShapeDtypeStruct((b, lq, h * LANES), BF16),
            scratch_shapes=[pltpu.VMEM((tq, LANES), F32)] * 2,
            compiler_params=_cp("parallel", "parallel", "arbitrary"),
            name="diff_attention_online" if online else "diff_attention",
        )(*args)

    return lax.cond(fixed, functools.partial(call, False), functools.partial(call, True),
                    lam_p, sg, q1, q2, *flat)


def _hy_prep_kernel(zp_ref, zc_ref, zn_ref, w_ref, b_ref, p_ref, x0_ref):
    i = pl.program_id(1)
    last = pl.num_programs(1) - 1
    tl = zc_ref.shape[1]
    h = 2 * SUBLANES
    ext = jnp.concatenate([jnp.where(i == 0, 0.0, zp_ref[0].astype(F32)), zc_ref[0].astype(F32),
                           jnp.where(i == last, 0.0, zn_ref[0].astype(F32))], axis=0)
    n = ext.shape[0]
    w = w_ref[...]
    u = (pltpu.roll(ext, 1, 0) * w[0:1] + ext * w[1:2] + pltpu.roll(ext, n - 1, 0) * w[2:3] + b_ref[...])[h:h + tl]
    hw = HY_WIDTH
    x0_ref[0] = u[:, 0:hw]
    p_ref[0] = u[:, 2 * hw:3 * hw] * u[:, hw:2 * hw]


def _hy_prep(zb, conv_w, conv_b):
    bx, lx, w = zb.shape
    tl = min(256, lx)
    h = 2 * SUBLANES
    nh = lx // h
    per = tl // h
    out = jax.ShapeDtypeStruct((bx, lx, HY_WIDTH), F32)
    return pl.pallas_call(
        _hy_prep_kernel,
        grid=(bx, lx // tl),
        in_specs=[pl.BlockSpec((1, h, w), lambda b, i: (b, jnp.maximum(i * per - 1, 0), 0)),
                  pl.BlockSpec((1, tl, w), lambda b, i: (b, i, 0)),
                  pl.BlockSpec((1, h, w), lambda b, i: (b, jnp.minimum((i + 1) * per, nh - 1), 0)),
                  pl.BlockSpec(conv_w.shape, lambda b, i: (0, 0)),
                  pl.BlockSpec((1, w), lambda b, i: (0, 0))],
        out_specs=[pl.BlockSpec((1, tl, HY_WIDTH), lambda b, i: (b, i, 0))] * 2,
        out_shape=[out, out],
        compiler_params=_cp("parallel", "parallel"),
        name="hyena_prep",
    )(zb, zb, zb, conv_w, conv_b.reshape(1, w))


def _hy_filter_kernel(feat_ref, w1, b1, w2, b2, w3, b3, w4, fr_ref, dl_ref, filt_ref, ssq_ref, *, s):
    i = pl.program_id(0)
    tr = feat_ref.shape[0]
    feat = feat_ref[...]
    fr = fr_ref[...]
    dot = lambda a, w: _dot3(*_hi_lo(a), w[...])
    a = jnp.sin(fr * (dot(feat, w1) + b1[...]))
    a = jnp.sin(fr * (dot(a, w2) + b2[...]))
    a = jnp.sin(fr * (dot(a, w3) + b3[...]))
    coef = dot(a, w4)
    n = i * tr + lax.broadcasted_iota(jnp.int32, (tr, 1), 0)
    window = jnp.exp(-feat[:, 0:1] * dl_ref[...]) + HY_SHIFT
    half = jnp.where(n < s, coef[:, :HY_WIDTH], coef[:, HY_WIDTH:])
    filt = jnp.where(n == s, 0.0, half * window)
    filt_ref[...] = filt

    @pl.when(i == 0)
    def _():
        ssq_ref[...] = jnp.zeros_like(ssq_ref)

    ssq_ref[...] += jnp.sum(filt * filt, axis=0, keepdims=True)


def _hy_feat(s):
    t = jnp.linspace(0.0, 1.0, s, dtype=F32)[:, None]
    w = (2.0 * math.pi / s) * jnp.arange(s, dtype=F32)[:, None]
    bands = jnp.linspace(1e-4, HY_BANDS - 1, HY_BANDS, dtype=F32)[None, :]
    feat = jnp.concatenate([t, jnp.cos(w * bands), jnp.sin(w * bands)], axis=-1)
    feat = jnp.concatenate([feat, feat[:1], feat[:0:-1]], axis=0)
    return jnp.pad(feat, ((0, 0), (0, LANES - feat.shape[1])))


def _hy_filter(s, feat, w1, b1, w2, b2, w3, b3, w4, freq):
    n = 2 * s
    tr = min(512, n)
    deltas = jnp.abs(jnp.linspace(math.log(HY_TARGET) / HY_FAST_DECAY, math.log(HY_TARGET) / HY_SLOW_DECAY,
                                  HY_WIDTH, dtype=F32)).reshape(1, HY_WIDTH)
    w1p = jnp.pad(w1, ((0, LANES - w1.shape[0]), (0, 0)))
    row = lambda a: a.reshape(1, -1)
    args = (feat, w1p, row(b1), w2, row(b2), w3, row(b3), w4, row(freq), deltas)
    const = lambda i: (0, 0)
    return pl.pallas_call(
        functools.partial(_hy_filter_kernel, s=s),
        grid=(n // tr,),
        in_specs=[pl.BlockSpec((tr, LANES), lambda i: (i, 0))] + [pl.BlockSpec(a.shape, const) for a in args[1:]],
        out_specs=[pl.BlockSpec((tr, HY_WIDTH), lambda i: (i, 0)), pl.BlockSpec((1, HY_WIDTH), const)],
        out_shape=[jax.ShapeDtypeStruct((n, HY_WIDTH), F32), jax.ShapeDtypeStruct((1, HY_WIDTH), F32)],
        compiler_params=_cp("arbitrary"),
        name="hyena_filter",
    )(*args)


def _dft_factors(n):
    lg = n.bit_length() - 1
    assert 1 << lg == n
    n1 = 1 << (lg // 2)
    return n1, n // n1


def _dft_tables(n):
    n1, n2 = _dft_factors(n)
    ia = jnp.arange(n1, dtype=jnp.int32)
    ang1 = (2.0 * math.pi / n1) * ((ia[:, None] * ia[None, :]) % n1).astype(F32)
    f1 = jnp.concatenate([jnp.cos(ang1), -jnp.sin(ang1)], axis=0)
    c = jnp.arange(n1, dtype=jnp.int32)[:, None, None]
    d = jnp.arange(n2, dtype=jnp.int32)[None, :, None]
    b = jnp.arange(n2, dtype=jnp.int32)[None, None, :]
    ang = (2.0 * math.pi / n) * ((b * (c + n1 * d)) % n).astype(F32)
    re, im = jnp.cos(ang), -jnp.sin(ang)
    m1 = jnp.concatenate([jnp.concatenate([re, -im], axis=2), jnp.concatenate([im, re], axis=2)], axis=1)
    m2 = jnp.swapaxes(m1, 1, 2)
    ang4 = ang1[: n1 // 2]
    f4 = jnp.concatenate([jnp.cos(ang4), -jnp.sin(ang4)], axis=1) * (1.0 / n)
    return f1, _hi_lo_rows(m1), _hi_lo_rows(m2), _hi_lo_rows(_per_offset(f4))


def _hi_lo_rows(m):
    return jnp.concatenate(_hi_lo(m), axis=-2)


def _dot3s(ms, a):
    m = ms.shape[0] // 2
    ah, al = _hi_lo(a)
    t = jnp.dot(ms, ah, preferred_element_type=F32)
    return t[:m] + t[m:] + jnp.dot(ms[:m], al, preferred_element_type=F32)


def _dot3(mh, ml, a):
    ah, al = _hi_lo(a)
    return (jnp.dot(mh, ah, preferred_element_type=F32) + jnp.dot(mh, al, preferred_element_type=F32)
            + jnp.dot(ml, ah, preferred_element_type=F32))


HY_TB = SUBLANES


def _per_offset(f):
    return jnp.kron(f, jnp.eye(HY_TB, dtype=f.dtype))


def _hy_stage1_kernel(f_ref, x_ref, re_ref, im_ref):
    _, n1, tb, w = re_ref.shape
    y = _dot3s(f_ref[...], x_ref[0].reshape(-1, w))
    re_ref[0] = y[:n1 * tb].reshape(n1, tb, w)
    im_ref[0] = y[n1 * tb:].reshape(n1, tb, w)


def _hy_stage1(f, x4):
    bx, k, n2, w = x4.shape
    tb = HY_TB
    n1 = f.shape[0] // (4 * tb)
    out = jax.ShapeDtypeStruct((bx, n1, n2, w), F32)
    return pl.pallas_call(
        _hy_stage1_kernel,
        grid=(bx, n2 // tb),
        in_specs=[pl.BlockSpec(f.shape, lambda b, j: (0, 0))]
                 + [pl.BlockSpec((1, k, tb, w), lambda b, j: (b, 0, j, 0))],
        out_specs=[pl.BlockSpec((1, n1, tb, w), lambda b, j: (b, 0, j, 0))] * 2,
        out_shape=[out, out],
        compiler_params=_cp("parallel", "parallel"),
        name="hyena_dft_stage1",
    )(f, x4)


def _hy_mid_kernel(m1_ref, m2_ref, re_ref, im_ref, hre_ref, him_ref, ore_ref, oim_ref):
    n2 = re_ref.shape[2]
    for c in range(re_ref.shape[1]):
        a = jnp.concatenate([re_ref[0, c], im_ref[0, c]], axis=0)
        x = _dot3s(m1_ref[c], a)
        xre, xim = x[:n2], x[n2:]
        hre, him = hre_ref[c], him_ref[c]
        y = jnp.concatenate([xre * hre - xim * him, xre * him + xim * hre], axis=0)
        bb = _dot3s(m2_ref[c], y)
        ore_ref[0, c] = bb[:n2]
        oim_ref[0, c] = bb[n2:]


def _hy_spec_kernel(m1_ref, re_ref, im_ref, rs_ref, ore_ref, oim_ref):
    n2 = re_ref.shape[2]
    for c in range(re_ref.shape[1]):
        a = jnp.concatenate([re_ref[0, c], im_ref[0, c]], axis=0)
        x = _dot3s(m1_ref[c], a) * rs_ref[...]
        ore_ref[c] = x[:n2]
        oim_ref[c] = x[n2:]


def _hy_filter_spectrum(m1, are, aim, rs):
    _, n1, n2, w = are.shape
    cb = min(4, n1)
    blk = pl.BlockSpec((1, cb, n2, w), lambda c: (0, c, 0, 0))
    mblk = pl.BlockSpec((cb, 4 * n2, 2 * n2), lambda c: (c, 0, 0))
    oblk = pl.BlockSpec((cb, n2, w), lambda c: (c, 0, 0))
    out = jax.ShapeDtypeStruct((n1, n2, w), F32)
    return pl.pallas_call(
        _hy_spec_kernel,
        grid=(n1 // cb,),
        in_specs=[mblk, blk, blk, pl.BlockSpec((1, w), lambda c: (0, 0))],
        out_specs=[oblk, oblk],
        out_shape=[out, out],
        compiler_params=_cp("parallel"),
        name="hyena_filter_spectrum",
    )(m1, are, aim, rs)


def _hy_mid(m1, m2, are, aim, hre, him):
    bx, n1, n2, w = are.shape
    cb = min(4, n1)
    blk = pl.BlockSpec((1, cb, n2, w), lambda c, b: (b, c, 0, 0))
    mblk = pl.BlockSpec((cb, 4 * n2, 2 * n2), lambda c, b: (c, 0, 0))
    hblk = pl.BlockSpec((cb, n2, w), lambda c, b: (c, 0, 0))
    out = jax.ShapeDtypeStruct((bx, n1, n2, w), F32)
    return pl.pallas_call(
        _hy_mid_kernel,
        grid=(n1 // cb, bx),
        in_specs=[mblk, mblk, blk, blk, hblk, hblk],
        out_specs=[blk, blk],
        out_shape=[out, out],
        compiler_params=_cp("parallel", "parallel"),
        name="hyena_dft_mid",
    )(m1, m2, are, aim, hre, him)


def _hy_last_kernel(f_ref, re_ref, im_ref, p_ref, bias_ref, o_ref):
    _, k, tb, w = p_ref.shape
    spec = jnp.concatenate([re_ref[0].reshape(-1, w), im_ref[0].reshape(-1, w)], axis=0)
    y = _dot3s(f_ref[...], spec)
    o_ref[0] = y.reshape(k, tb, w) + p_ref[0] * bias_ref[...]


def _hy_last(f4, bre, bim, p4, bias):
    bx, n1, n2, w = bre.shape
    k = n1 // 2
    tb = HY_TB
    big = pl.BlockSpec((1, n1, tb, w), lambda b, j: (b, 0, j, 0))
    small = pl.BlockSpec((1, k, tb, w), lambda b, j: (b, 0, j, 0))
    return pl.pallas_call(
        _hy_last_kernel,
        grid=(bx, n2 // tb),
        in_specs=[pl.BlockSpec(f4.shape, lambda b, j: (0, 0)), big, big, small,
                  pl.BlockSpec((1, w), lambda b, j: (0, 0))],
        out_specs=small,
        out_shape=jax.ShapeDtypeStruct((bx, k, n2, w), F32),
        compiler_params=_cp("parallel", "parallel"),
        name="hyena_dft_last",
    )(f4, bre, bim, p4, bias)


def _hyena(zb, conv_w, conv_b, filt_params, hy_bias, consts):
    bx, lx, _ = zb.shape
    feat, (f1, m1, m2, f4) = consts
    n = 2 * lx
    n1, n2 = _dft_factors(n)
    w = HY_WIDTH
    filt, ssq = _hy_filter(lx, feat, *filt_params)
    rs = lax.rsqrt(ssq + EPS)
    fre, fim = _hy_stage1(_hi_lo_rows(_per_offset(f1)), filt.reshape(1, n1, n2, w))
    hre, him = _hy_filter_spectrum(m1, fre, fim, rs)
    p, x0 = _hy_prep(zb, conv_w, conv_b)
    k = n1 // 2
    p4 = p.reshape(bx, k, n2, w)
    are, aim = _hy_stage1(_hi_lo_rows(_per_offset(f1[:, :k])), p4)
    bre, bim = _hy_mid(m1, m2, are, aim, hre, him)
    t = _hy_last(f4, bre, bim, p4, hy_bias.reshape(1, w))
    return x0, t.reshape(bx, lx, w)


WA_SHIFT_MAX = 35.0


def _wattn_kernel(sk_ref, q_ref, bias_ref, *rest, banded, fixed):
    if banded:
        kp_ref, kc_ref, kn_ref, kx_ref, vp_ref, vc_ref, vn_ref, vx_ref, o_ref = rest
    else:
        kx_ref, vx_ref, o_ref = rest
    i = pl.program_id(1)
    last = pl.num_programs(1) - 1
    qb = q_ref.shape[1]
    top = lax.broadcasted_iota(jnp.int32, (2 * qb, 1), 0) < qb
    if banded:
        c = lax.broadcasted_iota(jnp.int32, (1, bias_ref.shape[2]), 1) - BLOCK
        outside = ((c < 0) & (i == 0)) | ((c >= qb) & (c < qb + BLOCK) & (i == last))
        edge = jnp.where(outside, NEG_INF, 0.0)
    for h in range(WA_KV_HEADS):
        kl = slice(h * LANES, (h + 1) * LANES)
        ql = h * WA_GROUP * LANES
        q = jnp.concatenate([q_ref[0, :, ql:ql + LANES], q_ref[0, :, ql + LANES:ql + 2 * LANES]], axis=0)
        if banded:
            kk = jnp.concatenate([kp_ref[0, :, kl], kc_ref[0, :, kl], kn_ref[0, :, kl], kx_ref[0, :, kl]], axis=0)
            vv = jnp.concatenate([vp_ref[0, :, kl], vc_ref[0, :, kl], vn_ref[0, :, kl], vx_ref[0, :, kl]], axis=0)
        else:
            kk, vv = kx_ref[0, :, kl], vx_ref[0, :, kl]
        s = lax.dot_general(q, kk, (((1,), (1,)), ((), ())), preferred_element_type=F32) + bias_ref[h]
        if banded:
            s = s + edge
        sk = jnp.where(top, sk_ref[h * WA_GROUP], sk_ref[h * WA_GROUP + 1])
        if fixed:
            p, sink_term = jnp.exp2(s), sk
        else:
            m = jnp.maximum(jnp.max(s, axis=-1, keepdims=True), sk)
            p, sink_term = jnp.exp2(s - m), jnp.exp2(sk - m)
        acc = jnp.dot(p.astype(BF16), vv, preferred_element_type=F32)
        o = (acc * (1.0 / (acc[:, WA_DIM:WA_DIM + 1] + sink_term))).astype(BF16)
        o_ref[0, :, ql:ql + 2 * LANES] = jnp.concatenate([o[:qb], o[qb:]], axis=1)


def _window_attention(q, k, v, kx, vx, sink, bound, banded):
    b, lq, _ = q.shape
    cx = kx.shape[1]
    qb = min(256, lq)
    per = qb // BLOCK
    nblk = lq // BLOCK
    kvw = WA_KV_HEADS * LANES
    side = lambda f: pl.BlockSpec((1, BLOCK, kvw), f)
    prev = side(lambda b_, i: (b_, jnp.maximum(i * per - 1, 0), 0))
    nxt = side(lambda b_, i: (b_, jnp.minimum((i + 1) * per, nblk - 1), 0))
    cur = pl.BlockSpec((1, qb, kvw), lambda b_, i: (b_, i, 0))
    ctx = pl.BlockSpec((1, cx, kvw), lambda b_, i: (b_, 0, 0))
    qspec = pl.BlockSpec((1, qb, WA_HEADS * LANES), lambda b_, i: (b_, i, 0))
    mask = jnp.zeros((WA_GROUP * qb, cx), F32)
    if banded:
        r = jnp.arange(WA_GROUP * qb)[:, None] % qb
        c = jnp.arange(qb + 2 * BLOCK)[None, :] - BLOCK
        mask = jnp.concatenate([jnp.where(jnp.abs(r - c) <= BLOCK, 0.0, NEG_INF).astype(F32), mask], axis=1)
        specs, args = [prev, cur, nxt, ctx, prev, cur, nxt, ctx], (k, k, k, kx, v, v, v, vx)
    else:
        specs, args = [ctx, ctx], (kx, vx)
    fixed = bound <= WA_SHIFT_MAX
    shift = jnp.maximum(bound, sink)
    rows = jnp.repeat(shift.reshape(WA_KV_HEADS, WA_GROUP), qb, axis=1)[:, :, None]
    bias = mask[None] - jnp.where(fixed, LOG2E * rows, 0.0)
    sk = jnp.where(fixed, jnp.exp2(LOG2E * (sink - shift)), LOG2E * sink)

    def call(fixed_, *ops):
        return pl.pallas_call(
            functools.partial(_wattn_kernel, banded=banded, fixed=fixed_),
            grid=(b, lq // qb),
            in_specs=[pl.BlockSpec(memory_space=pltpu.SMEM), qspec,
                      pl.BlockSpec(bias.shape, lambda b_, i: (0, 0, 0))] + specs,
            out_specs=qspec,
            out_shape=jax.ShapeDtypeStruct((b, lq, WA_HEADS * LANES), BF16),
            compiler_params=_cp("parallel", "arbitrary"),
            name="window_attention" if fixed_ else "window_attention_online",
        )(*ops)

    return lax.cond(fixed, functools.partial(call, True), functools.partial(call, False), sk, q, bias, *args)


def _conf_kernel(zp_ref, zc_ref, zn_ref, w_ref, b_ref, lg_ref, lb_ref, o_ref, ext, sh, *, halo):
    i = pl.program_id(1)
    last = pl.num_programs(1) - 1
    tl = zc_ref.shape[1]
    cw = CF_WIDTH

    def glu(z_ref):
        z = z_ref[0].astype(F32)
        return z[:, :cw] * jax.nn.sigmoid(z[:, cw:])

    ext[0:halo] = jnp.where(i == 0, 0.0, glu(zp_ref))
    ext[halo:halo + tl] = glu(zc_ref)
    ext[halo + tl:] = jnp.where(i == last, 0.0, glu(zn_ref))
    for r in range(1, SUBLANES):
        sh[r - 1] = ext[pl.ds(r, sh.shape[1]), :]
    w = w_ref[...]
    u = jnp.zeros((tl, cw), F32) + b_ref[...]
    for j in range(CF_TAPS):
        off = halo - CF_TAPS // 2 + j
        base, r = off // SUBLANES * SUBLANES, off % SUBLANES
        tap = ext[pl.ds(base, tl), :] if r == 0 else sh[r - 1, pl.ds(base, tl), :]
        u = u + tap * w[j:j + 1]
    uc = u - jnp.mean(u, axis=-1, keepdims=True)
    y = uc * lax.rsqrt(jnp.mean(uc * uc, axis=-1, keepdims=True) + EPS) * lg_ref[...] + lb_ref[...]
    o_ref[0] = _silu(y).astype(BF16)


def _conformer(zd, dw_w, dw_b, ln_g, ln_b):
    bx, lx, w = zd.shape
    tl = min(256, lx)
    halo = 2 * SUBLANES
    nh = lx // halo
    per = tl // halo
    row = lambda a: a.reshape(1, -1)
    const = lambda b, i: (0, 0)
    return pl.pallas_call(
        functools.partial(_conf_kernel, halo=halo),
        grid=(bx, lx // tl),
        in_specs=[pl.BlockSpec((1, halo, w), lambda b, i: (b, jnp.maximum(i * per - 1, 0), 0)),
                  pl.BlockSpec((1, tl, w), lambda b, i: (b, i, 0)),
                  pl.BlockSpec((1, halo, w), lambda b, i: (b, jnp.minimum((i + 1) * per, nh - 1), 0)),
                  pl.BlockSpec(dw_w.shape, const)] + [pl.BlockSpec((1, CF_WIDTH), const)] * 3,
        out_specs=pl.BlockSpec((1, tl, CF_WIDTH), lambda b, i: (b, i, 0)),
        out_shape=jax.ShapeDtypeStruct((bx, lx, CF_WIDTH), BF16),
        scratch_shapes=[pltpu.VMEM((tl + 2 * halo, CF_WIDTH), F32),
                        pltpu.VMEM((SUBLANES - 1, tl + 2 * halo - SUBLANES, CF_WIDTH), F32)],
        compiler_params=_cp("parallel", "parallel"),
        name="conformer_conv",
    )(zd, zd, zd, dw_w, row(dw_b), row(ln_g), row(ln_b))


def _route_rows(x, m_ref, g_ref, wr_ref, br_ref, rows_ref, gi_ref):
    tm, d = x.shape
    half = d // 2
    lane = lax.broadcasted_iota(jnp.int32, (tm, LANES), 1).astype(F32)
    h = _rms(x) * g_ref[...] * (1.0 + m_ref[0, :, 4 * d:5 * d]) + m_ref[0, :, 3 * d:4 * d]
    bits = lax.bitcast_convert_type(h.astype(BF16).astype(F32), jnp.int32)
    rows_ref[0, :, :half] = bits[:, :half] | lax.shift_right_logical(bits[:, half:], 16)
    lg = _dot3(*_hi_lo(h), wr_ref[...]) + br_ref[...]
    isg = lane < N_GROUPS
    gmax = jnp.max(jnp.where(isg, lg, NEG_INF), axis=-1, keepdims=True)
    gi = jnp.min(jnp.where(isg & (lg == gmax), lane, LANES), axis=-1, keepdims=True)
    gw = 1.0 / jnp.sum(jnp.where(isg, jnp.exp(lg - gmax), 0.0), axis=-1, keepdims=True)
    lo = N_GROUPS + gi * EXP_PER_GROUP
    ise = (lane >= lo) & (lane < lo + EXP_PER_GROUP)
    le = jnp.where(ise, lg, NEG_INF)
    m1 = jnp.max(le, axis=-1, keepdims=True)
    i1 = jnp.min(jnp.where(ise & (le == m1), lane, LANES), axis=-1, keepdims=True)
    ise2 = ise & (lane != i1)
    le2 = jnp.where(ise2, lg, NEG_INF)
    m2 = jnp.max(le2, axis=-1, keepdims=True)
    i2 = jnp.min(jnp.where(ise2 & (le2 == m2), lane, LANES), axis=-1, keepdims=True)
    r = jnp.exp(m2 - m1)
    wa = gw / (1.0 + r)
    gates = jnp.where(lane == i1 - lo, wa, 0.0) + jnp.where(lane == i2 - lo, wa * r, 0.0)
    rows_ref[0, :, half:] = lax.bitcast_convert_type(gates, jnp.int32)
    gi_ref[0] = gi.astype(jnp.int32)


def _merge_kernel(x_ref, m_ref, ya, yb0, ybt, yc, yd, zg_ref, bg_ref, wa, wb, wc, wd, wo_ref, *rest, goff):
    o_ref = rest[-1] if len(rest) == 1 else rest[3]
    d = x_ref.shape[-1]
    acc = jnp.zeros(x_ref.shape[1:], F32)
    ys = (ya[0], (yb0[0] * ybt[0]).astype(BF16), yc[0], yd[0])
    for i, (y, w) in enumerate(zip(ys, (wa, wb, wc, wd))):
        gate = jax.nn.sigmoid(zg_ref[0, :, i * d:(i + 1) * d].astype(F32) + bg_ref[:, i * d:(i + 1) * d])
        acc = acc + gate * jnp.dot(y, w[...], preferred_element_type=F32)
    out = jnp.dot(acc.astype(BF16), wo_ref[...], preferred_element_type=F32)
    xn = x_ref[0] + m_ref[0, :, goff:goff + d] * out
    o_ref[0] = xn
    if len(rest) > 1:
        g_ref, wr_ref, br_ref, _, rows_ref, gi_ref = rest
        _route_rows(xn, m_ref, g_ref, wr_ref, br_ref, rows_ref, gi_ref)


def _merge(x, mod, ys, zg, b_gate, wbs, w_out, router=None):
    bx, lx, d = x.shape
    tm = min(512, lx)
    row = lambda b, i: (b, i, 0)
    const = lambda b, i: (0, 0)
    once = lambda a: pl.BlockSpec(a.shape, const, pipeline_mode=pl.Buffered(1))
    extra = (router[0].reshape(1, d), router[1], router[2]) if router else ()
    rw = d // 2 + LANES
    out_specs = [pl.BlockSpec((1, tm, d), row), pl.BlockSpec((1, tm, rw), row), pl.BlockSpec((1, tm, 1), row)]
    out_shape = [jax.ShapeDtypeStruct(x.shape, F32), jax.ShapeDtypeStruct((bx, lx, rw), jnp.int32),
                 jax.ShapeDtypeStruct((bx, lx, 1), jnp.int32)]
    return pl.pallas_call(
        functools.partial(_merge_kernel, goff=2 * d),
        grid=(bx, lx // tm),
        in_specs=[pl.BlockSpec((1, tm, d), row), pl.BlockSpec((1, 1, mod.shape[-1]), lambda b, i: (b, 0, 0))]
                 + [pl.BlockSpec((1, tm, y.shape[-1]), row) for y in ys]
                 + [pl.BlockSpec((1, tm, N_BRANCH * d), row), pl.BlockSpec((1, N_BRANCH * d), const)]
                 + [once(w) for w in wbs] + [once(w_out)] + [once(a) for a in extra],
        out_specs=out_specs if router else out_specs[0],
        out_shape=out_shape if router else out_shape[0],
        input_output_aliases={0: 0},
        compiler_params=_cp("parallel", "parallel"),
        name="merge",
    )(x, mod, *ys, zg, b_gate.reshape(1, -1), *wbs, w_out, *extra)


def _slot_rows(w, group):
    n = w.shape[0] // group
    return jnp.pad(w.reshape(n, group, -1), ((0, 0), (0, LANES - group), (0, 0))).reshape(n * LANES, -1)


def _moe_kernel(x_ref, m_ref, g_ref, wr_ref, br_ref, w1_ref, w3_ref, w2_ref, o_ref, h_sc, gate_sc, acc_sc):
    e = pl.program_id(2)
    d = x_ref.shape[-1]
    tm = x_ref.shape[1]
    lane = lax.broadcasted_iota(jnp.int32, (tm, LANES), 1).astype(F32)

    @pl.when(e == 0)
    def _():
        h = _rms(x_ref[0]) * g_ref[...] * (1.0 + m_ref[0, :, 4 * d:5 * d]) + m_ref[0, :, 3 * d:4 * d]
        h_sc[...] = h.astype(BF16)
        lg = _dot3(*_hi_lo(h), wr_ref[...]) + br_ref[...]
        isg = lane < N_GROUPS
        gmax = jnp.max(jnp.where(isg, lg, NEG_INF), axis=-1, keepdims=True)
        gi = jnp.min(jnp.where(isg & (lg == gmax), lane, LANES), axis=-1, keepdims=True)
        gw = 1.0 / jnp.sum(jnp.where(isg, jnp.exp(lg - gmax), 0.0), axis=-1, keepdims=True)
        lo = N_GROUPS + gi * EXP_PER_GROUP
        ise = (lane >= lo) & (lane < lo + EXP_PER_GROUP)
        le = jnp.where(ise, lg, NEG_INF)
        m1 = jnp.max(le, axis=-1, keepdims=True)
        i1 = jnp.min(jnp.where(ise & (le == m1), lane, LANES), axis=-1, keepdims=True)
        ise2 = ise & (lane != i1)
        le2 = jnp.where(ise2, lg, NEG_INF)
        m2 = jnp.max(le2, axis=-1, keepdims=True)
        i2 = jnp.min(jnp.where(ise2 & (le2 == m2), lane, LANES), axis=-1, keepdims=True)
        r = jnp.exp(m2 - m1)
        wa = gw / (1.0 + r)
        gate_sc[...] = jnp.where(lane == i1, wa, 0.0) + jnp.where(lane == i2, wa * r, 0.0)
        acc_sc[...] = jnp.zeros_like(acc_sc)

    hb = h_sc[...]
    u = (_silu(jnp.dot(hb, w1_ref[0].astype(BF16), preferred_element_type=F32))
         * jnp.dot(hb, w3_ref[0].astype(BF16), preferred_element_type=F32))
    ge = jnp.sum(jnp.where(lane == (e + N_GROUPS).astype(F32), gate_sc[...], 0.0), axis=-1, keepdims=True)
    acc_sc[...] += ge * jnp.dot(u.astype(BF16), w2_ref[0].astype(BF16), preferred_element_type=F32)

    @pl.when(e == pl.num_programs(2) - 1)
    def _():
        o_ref[0] = x_ref[0] + m_ref[0, :, 5 * d:6 * d] * acc_sc[...]


def _moe(x, mod, g, w_router, b_router, w1, w3, w2):
    bx, lx, d = x.shape
    tm = min(1024, lx)
    ne, _, f = w1.shape
    row = lambda b, i, e: (b, i, 0)
    const = lambda b, i, e: (0, 0)
    return pl.pallas_call(
        _moe_kernel,
        grid=(bx, lx // tm, ne),
        in_specs=[pl.BlockSpec((1, tm, d), row), pl.BlockSpec((1, 1, mod.shape[-1]), lambda b, i, e: (b, 0, 0)),
                  pl.BlockSpec((1, d), const), pl.BlockSpec((d, LANES), const), pl.BlockSpec((1, LANES), const),
                  pl.BlockSpec((1, d, f), lambda b, i, e: (e, 0, 0)), pl.BlockSpec((1, d, f), lambda b, i, e: (e, 0, 0)),
                  pl.BlockSpec((1, f, d), lambda b, i, e: (e, 0, 0))],
        out_specs=pl.BlockSpec((1, tm, d), row),
        out_shape=jax.ShapeDtypeStruct(x.shape, F32),
        scratch_shapes=[pltpu.VMEM((tm, d), BF16), pltpu.VMEM((tm, LANES), F32), pltpu.VMEM((tm, d), F32)],
        input_output_aliases={0: 0},
        compiler_params=_cp("parallel", "parallel", "arbitrary"),
        name="moe",
    )(x, mod, g.reshape(1, d), w_router, b_router, w1, w3, w2)


SC_CORES = 2
SC_SUBCORES = 16
SC_CHUNK = 64
MOE_ROWS = 1024


def _sc_gather(table, idx):
    n = idx.shape[0]
    w = table.shape[1]
    per = n // (SC_CORES * SC_SUBCORES)
    assert per * SC_CORES * SC_SUBCORES == n and per % SC_CHUNK == 0
    mesh = plsc.VectorSubcoreMesh(core_axis_name="c", subcore_axis_name="s")

    @functools.partial(
        pl.kernel, mesh=mesh, out_type=jax.ShapeDtypeStruct((n, w), table.dtype),
        scratch_types=[pltpu.VMEM((SC_CHUNK,), jnp.int32), pltpu.VMEM((SC_CHUNK, w), table.dtype),
                       pltpu.SemaphoreType.DMA],
        name="sc_row_gather")
    def gather(table_hbm, idx_hbm, out_hbm, idx_v, rows_v, sem):
        base = (lax.axis_index("s") * SC_CORES + lax.axis_index("c")) * per

        @pl.loop(0, per // SC_CHUNK)
        def _(j):
            off = pl.multiple_of(base + j * SC_CHUNK, SC_CHUNK)
            pltpu.sync_copy(idx_hbm.at[pl.ds(off, SC_CHUNK)], idx_v)
            pltpu.async_copy(table_hbm.at[idx_v], rows_v, sem).wait()
            pltpu.sync_copy(rows_v, out_hbm.at[pl.ds(off, SC_CHUNK)])

    return gather(table, idx)


def _sc_scatter(rows, idx, n_out):
    n, w = rows.shape
    per = n // (SC_CORES * SC_SUBCORES)
    assert per * SC_CORES * SC_SUBCORES == n and per % SC_CHUNK == 0
    mesh = plsc.VectorSubcoreMesh(core_axis_name="c", subcore_axis_name="s")

    @functools.partial(
        pl.kernel, mesh=mesh, out_type=jax.ShapeDtypeStruct((n_out, w), rows.dtype),
        scratch_types=[pltpu.VMEM((SC_CHUNK,), jnp.int32), pltpu.VMEM((SC_CHUNK, w), rows.dtype),
                       pltpu.SemaphoreType.DMA],
        name="sc_row_scatter")
    def scatter(rows_hbm, idx_hbm, out_hbm, idx_v, rows_v, sem):
        base = (lax.axis_index("s") * SC_CORES + lax.axis_index("c")) * per

        @pl.loop(0, per // SC_CHUNK)
        def _(j):
            off = pl.multiple_of(base + j * SC_CHUNK, SC_CHUNK)
            pltpu.sync_copy(idx_hbm.at[pl.ds(off, SC_CHUNK)], idx_v)
            pltpu.sync_copy(rows_hbm.at[pl.ds(off, SC_CHUNK)], rows_v)
            pltpu.async_copy(rows_v, out_hbm.at[idx_v], sem).wait()

    return scatter(rows, idx)


def _gmoe_kernel(grp_ref, nv_ref, xs_ref, w1_ref, w3_ref, w2_ref, o_ref, h_sc, acc_sc):
    i = pl.program_id(0)
    e = pl.program_id(1)
    tm, d = h_sc.shape
    half = d // 2
    valid = lax.broadcasted_iota(jnp.int32, (tm, 1), 0) < nv_ref[i]

    @pl.when(nv_ref[i] > 0)
    def _():
        @pl.when(e == 0)
        def _():
            pk = xs_ref[:, :half]
            h = jnp.concatenate([lax.bitcast_convert_type(pk & -65536, F32),
                                 lax.bitcast_convert_type(lax.shift_left(pk, 16), F32)], axis=1)
            h_sc[...] = jnp.where(valid, h, 0.0).astype(BF16)
            acc_sc[...] = jnp.zeros_like(acc_sc)

        hb = h_sc[...]
        u = (_silu(jnp.dot(hb, w1_ref[0].astype(BF16), preferred_element_type=F32))
             * jnp.dot(hb, w3_ref[0].astype(BF16), preferred_element_type=F32))
        lane = lax.broadcasted_iota(jnp.int32, (tm, LANES), 1)
        gates = lax.bitcast_convert_type(xs_ref[:, half:], F32)
        ge = jnp.sum(jnp.where(valid & (lane == e), gates, 0.0), axis=-1, keepdims=True)
        acc_sc[...] += ge * jnp.dot(u.astype(BF16), w2_ref[0].astype(BF16), preferred_element_type=F32)

    @pl.when(e == pl.num_programs(1) - 1)
    def _():
        o_ref[...] = jnp.where(nv_ref[i] > 0, acc_sc[...], 0.0)


def _grouped_moe(grp, nv, xs, w1, w3, w2):
    p, dw = xs.shape
    d = 2 * (dw - LANES)
    _, _, f = w1.shape
    wmap = lambda i, e, grp, nv: (grp[i] * EXP_PER_GROUP + e, 0, 0)
    rows = lambda i, e, grp, nv: (i, 0)
    return pl.pallas_call(
        _gmoe_kernel,
        grid_spec=pltpu.PrefetchScalarGridSpec(
            num_scalar_prefetch=2,
            grid=(p // MOE_ROWS, EXP_PER_GROUP),
            in_specs=[pl.BlockSpec((MOE_ROWS, dw), rows),
                      pl.BlockSpec((1, d, f), wmap), pl.BlockSpec((1, d, f), wmap), pl.BlockSpec((1, f, d), wmap)],
            out_specs=pl.BlockSpec((MOE_ROWS, d), rows),
            scratch_shapes=[pltpu.VMEM((MOE_ROWS, d), BF16), pltpu.VMEM((MOE_ROWS, d), F32)]),
        out_shape=jax.ShapeDtypeStruct((p, d), F32),
        compiler_params=_cp("arbitrary", "arbitrary"),
        name="moe_experts",
    )(grp, nv, xs, w1, w3, w2)


def _residual_kernel(x_ref, m_ref, y_ref, o_ref):
    d = x_ref.shape[-1]
    o_ref[0] = x_ref[0] + m_ref[0, :, 5 * d:6 * d] * y_ref[0]


def _residual(x, mod, y):
    bx, lx, d = x.shape
    tm = min(1024, lx)
    row = lambda b, i: (b, i, 0)
    return pl.pallas_call(
        _residual_kernel,
        grid=(bx, lx // tm),
        in_specs=[pl.BlockSpec((1, tm, d), row), pl.BlockSpec((1, 1, mod.shape[-1]), lambda b, i: (b, 0, 0)),
                  pl.BlockSpec((1, tm, d), row)],
        out_specs=pl.BlockSpec((1, tm, d), row),
        out_shape=jax.ShapeDtypeStruct(x.shape, F32),
        input_output_aliases={0: 0},
        compiler_params=_cp("parallel", "parallel"),
        name="moe_residual",
    )(x, mod, y)


def _moe_sorted(x, mod, rows, gi, w1, w3, w2):
    bx, lx, d = x.shape
    t = bx * lx
    gi = gi.reshape(t)
    onehot = (gi[:, None] == jnp.arange(N_GROUPS, dtype=jnp.int32)[None, :]).astype(jnp.int32)
    csum = jnp.cumsum(onehot, axis=0)
    counts = csum[-1]
    rank = jnp.take_along_axis(csum, gi[:, None], axis=1)[:, 0] - 1
    padded = (counts + MOE_ROWS - 1) // MOE_ROWS * MOE_ROWS
    pend = jnp.cumsum(padded)
    pstart = pend - padded
    pos = (pstart[gi] + rank).astype(jnp.int32)
    p = t + N_GROUPS * MOE_ROWS
    bstart = jnp.arange(p // MOE_ROWS, dtype=jnp.int32) * MOE_ROWS
    grp = jnp.minimum(jnp.searchsorted(pend, bstart, side="right"), N_GROUPS - 1).astype(jnp.int32)
    nv = jnp.clip(pstart[grp] + counts[grp] - bstart, 0, MOE_ROWS).astype(jnp.int32)
    xs = _sc_scatter(rows.reshape(t, rows.shape[-1]), pos, p)
    ys = _grouped_moe(grp, nv, xs, w1, w3, w2)
    yt = _sc_gather(ys, pos)
    return _residual(x, mod, yt.reshape(bx, lx, d))


def kernel(x, c, ctx, c_ctx, w_mod, b_mod, norm1_g, norm2_g, w_in, b_gate, da_qn, da_kn, da_lam, da_subln, hy_conv_w, hy_conv_b, hf_w1, hf_b1, hf_w2, hf_b2, hf_w3, hf_b3, hf_w4, hf_freq, hy_bias, wa_qn, wa_kn, wa_sink, cf_dw_w, cf_dw_b, cf_ln_g, cf_ln_b, w_branch, w_out, w_rg, b_rg, w_re, b_re, w1, w3, w2):
    b, s, d = x.shape
    cl = ctx.shape[1]
    depth = w_mod.shape[0]
    assert s % 256 == 0 and cl % 256 == 0 and s % GRID_W == 0

    nrow = -(-(b + 1) // SUBLANES) * SUBLANES
    crows = jnp.zeros((nrow, d), F32).at[:b].set(c).at[b].set(c_ctx)
    mods = _mod_vectors(crows, w_mod, b_mod)

    aw, qw, kw = DA_HEADS * DA_DIM, WA_HEADS * WA_DIM, WA_KV_HEADS * WA_DIM
    tab_lat = (*_rope_tables(s, DA_DIM, DA_HEADS), *_rope_tables(s, WA_DIM, WA_HEADS), *_rope_tables(s, WA_DIM, WA_KV_HEADS))
    tab_ctx = (*_unit_tables(cl, aw), *_unit_tables(cl, qw), *_unit_tables(cl, kw))
    hy_lat = (_hy_feat(s), _dft_tables(2 * s))
    hy_ctx = (_hy_feat(cl), _dft_tables(2 * cl))
    gms = (_group_ones(aw, DA_DIM), _group_ones(qw, WA_DIM), _group_ones(kw, WA_DIM))
    qvec = _slot_fill(DA_DIM, DA_DIM + 1, 1.0)
    vvec = _slot_fill(DA_VDIM, DA_VDIM + DA_ONES, 1.0)

    w_in_b = w_in.astype(BF16)
    xc = ctx
    for l in range(depth):
        last = l == depth - 1
        lam_init = 0.8 - 0.6 * math.exp(-0.3 * l)
        mod_x = mods[l, :b][:, None, :]
        mod_c = jnp.broadcast_to(mods[l, b][None, None, :], (b, 1, mods.shape[-1]))
        shift = 1.02 * LOG2E * DA_DIM ** 0.5 * jnp.max(jnp.abs(da_qn[l])) * jnp.max(jnp.abs(da_kn[l]))
        fixed = shift <= DA_SHIFT_MAX
        kvec = _slot_fill(DA_DIM, DA_DIM + 1, jnp.where(fixed, -shift, 0.0))
        tile = lambda a, n: jnp.tile(a, n).reshape(1, -1)
        consts = (*gms, tile(da_qn[l], DA_HEADS), tile(da_kn[l], DA_HEADS), tile(wa_qn[l], WA_HEADS),
                  tile(wa_kn[l], WA_KV_HEADS), qvec, kvec, vvec)
        q1, q2, k1, k2, v, zb, qc, kc, vc, zd, zg = _in_proj(x, mod_x, norm1_g[l], w_in_b, l, tab_lat, consts)
        q1x, q2x, k1x, k2x, vx, zbx, qcx, kcx, vcx, zdx, zgx = _in_proj(xc, mod_c, norm1_g[l], w_in_b, l, tab_ctx, consts)
        filt_params = (hf_w1[l], hf_b1[l], hf_w2[l], hf_b2[l], hf_w3[l], hf_b3[l], hf_w4[l], hf_freq[l])
        wb4 = w_branch[l].astype(BF16)
        wbs = (_slot_rows(wb4[0], DA_VDIM), wb4[1], _slot_rows(wb4[2], WA_DIM), wb4[3])
        wo = w_out[l].astype(BF16)

        ya = _diff_attention(q1, q2, [(k1, k2, v), (k1x, k2x, vx)], fixed, da_lam[l], da_subln[l], lam_init)
        yb = _hyena(zb, hy_conv_w[l], hy_conv_b[l], filt_params, hy_bias[l], hy_lat)
        wbound = 1.02 * WA_DIM ** 0.5 * jnp.max(jnp.abs(wa_qn[l])) * jnp.max(jnp.abs(wa_kn[l]))
        yc_ = _window_attention(qc, kc, vc, kcx, vcx, wa_sink[l], wbound, True)
        yd = _conformer(zd, cf_dw_w[l], cf_dw_b[l], cf_ln_g[l], cf_ln_b[l])
        w_router = jnp.pad(jnp.concatenate([w_rg[l], w_re[l]], axis=1), ((0, 0), (0, LANES - N_GROUPS - N_EXPERTS)))
        b_router = jnp.pad(jnp.concatenate([b_rg[l], b_re[l]]), (0, LANES - N_GROUPS - N_EXPERTS)).reshape(1, LANES)
        ew = (w1[l], w3[l], w2[l])
        x, rows, gi = _merge(x, mod_x, (ya, *yb, yc_, yd), zg, b_gate[l], wbs, wo, (norm2_g[l], w_router, b_router))

        if not last:
            yca = _diff_attention(q1x, q2x, [(k1x, k2x, vx)], fixed, da_lam[l], da_subln[l], lam_init)
            ycb = _hyena(zbx, hy_conv_w[l], hy_conv_b[l], filt_params, hy_bias[l], hy_ctx)
            ycc = _window_attention(qcx, kcx, vcx, kcx, vcx, wa_sink[l], wbound, False)
            ycd = _conformer(zdx, cf_dw_w[l], cf_dw_b[l], cf_ln_g[l], cf_ln_b[l])
            xc = _merge(xc, mod_c, (yca, *ycb, ycc, ycd), zgx, b_gate[l], wbs, wo)
            xc = _moe(xc.reshape(1, b * cl, d), mod_c[:1], norm2_g[l], w_router, b_router, *ew).reshape(b, cl, d)
        x = _moe_sorted(x, mod_x, rows, gi, *ew)
    return x
```

```python
import functools
import math

import jax
import jax.numpy as jnp
from jax import lax
from jax.experimental import pallas as pl
from jax.experimental.pallas import tpu as pltpu
from jax.experimental.pallas import tpu_sc as plsc

F32 = jnp.float32
BF16 = jnp.bfloat16
HI = lax.Precision.HIGHEST

GRID_W = 64
BLOCK = 128
ROPE_BASE = 10000.0
EPS = 1e-6
NEG_INF = -1e30

DA_HEADS = 4
DA_DIM = 32
DA_VDIM = 64
HY_WIDTH = 256
HY_BANDS = 16
HY_FF = 64
HY_SHIFT = 0.05
HY_FAST_DECAY = 0.3
HY_SLOW_DECAY = 1.5
HY_TARGET = 1e-2
WA_HEADS = 4
WA_KV_HEADS = 2
WA_GROUP = 2
WA_DIM = 64
CF_WIDTH = 256
CF_TAPS = 31
N_BRANCH = 4
BRANCH_W = 256
N_GROUPS = 4
EXP_PER_GROUP = 4
N_EXPERTS = 16

W_A = 4 * DA_HEADS * DA_DIM + DA_HEADS * DA_VDIM
W_B = 3 * HY_WIDTH
W_C = (WA_HEADS + 2 * WA_KV_HEADS) * WA_DIM
W_D = 2 * CF_WIDTH
OFF_B = W_A
OFF_C = OFF_B + W_B
OFF_D = OFF_C + W_C
OFF_G = OFF_D + W_D

LOG2E = math.log2(math.e)
LANES = 128
SUBLANES = 8
VMEM_LIMIT = 56 * 1024 * 1024

DA_ONES = 16
DA_SHIFT_MAX = 50.0


def _cp(*sem):
    return pltpu.CompilerParams(dimension_semantics=sem, vmem_limit_bytes=VMEM_LIMIT)


def _rms(xf):
    return xf * lax.rsqrt(jnp.mean(xf * xf, axis=-1, keepdims=True) + EPS)


def _silu(x):
    return x * jax.nn.sigmoid(x)


def _mod_kernel(c_ref, w_ref, b_ref, o_ref):
    s = _silu(c_ref[...])
    o_ref[0] = jnp.dot(s, w_ref[0], precision=HI, preferred_element_type=F32) + b_ref[0]


def _mod_vectors(crows, w_mod, b_mod):
    depth, d, n = w_mod.shape
    r = crows.shape[0]
    tn = 1536
    return pl.pallas_call(
        _mod_kernel,
        grid=(depth, n // tn),
        in_specs=[pl.BlockSpec((r, d), lambda l, j: (0, 0)),
                  pl.BlockSpec((1, d, tn), lambda l, j: (l, 0, j)),
                  pl.BlockSpec((1, 1, tn), lambda l, j: (l, 0, j))],
        out_specs=pl.BlockSpec((1, r, tn), lambda l, j: (l, 0, j)),
        out_shape=jax.ShapeDtypeStruct((depth, r, n), F32),
        compiler_params=_cp("arbitrary", "arbitrary"),
        name="mod_vectors",
    )(crows, w_mod, b_mod.reshape(depth, 1, n))


def _rope_tables(s, d, reps):
    rows = s // GRID_W
    row = jnp.repeat(jnp.arange(rows, dtype=F32), GRID_W)
    col = jnp.tile(jnp.arange(GRID_W, dtype=F32), rows)
    qd = d // 4
    inv = ROPE_BASE ** (-jnp.arange(qd, dtype=F32) / qd)
    ar = row[:, None] * inv[None, :]
    ac = col[:, None] * inv[None, :]
    z = jnp.zeros_like(ar)
    cos = jnp.concatenate([jnp.cos(ar), jnp.cos(ar), jnp.cos(ac), jnp.cos(ac)], axis=-1)
    sin_up = jnp.concatenate([-jnp.sin(ar), z, -jnp.sin(ac), z], axis=-1)
    sin_dn = jnp.concatenate([z, jnp.sin(ar), z, jnp.sin(ac)], axis=-1)
    t = lambda a: jnp.tile(a, (1, reps))
    return t(cos), t(sin_up), t(sin_dn)


def _unit_tables(s, w):
    return jnp.ones((s, w), F32), jnp.zeros((s, w), F32), jnp.zeros((s, w), F32)


def _group_ones(width, group):
    i = jnp.arange(width) // group
    return (i[:, None] == i[None, :]).astype(BF16)


def _hi_lo(a):
    hi = a.astype(BF16)
    return hi, (a - hi.astype(F32)).astype(BF16)


def _slot_fill(lo, hi, value):
    j = jnp.arange(LANES)
    return jnp.where((j >= lo) & (j < hi), value, 0.0).astype(F32).reshape(1, LANES)


def _store_slots(o_ref, y, group, fill):
    per = LANES // group
    lane = lax.broadcasted_iota(jnp.int32, (y.shape[0], LANES), 1)
    for h in range(y.shape[1] // group):
        blk = y[:, h // per * LANES:(h // per + 1) * LANES]
        if h % per:
            blk = pltpu.roll(blk, LANES - h % per * group, 1)
        o_ref[0, :, h * LANES:(h + 1) * LANES] = jnp.where(lane < group, blk, fill).astype(BF16)


def _norm_rope(x, gmat, gain, cos, sup, sdn, group, qd):
    w = x.shape[-1]
    sh, sl = _hi_lo(x * x)
    ss = (jnp.dot(sh, gmat, preferred_element_type=F32) + jnp.dot(sl, gmat, preferred_element_type=F32)) * (1.0 / group)
    xn = x * lax.rsqrt(ss + EPS) * gain
    return xn * cos + pltpu.roll(xn, w - qd, 1) * sup + pltpu.roll(xn, qd, 1) * sdn


def _inproj_kernel(x_ref, m_ref, g_ref, w_ref,
                   ca, ua, da, cq, uq, dq, ck, uk, dk, gma, gmq, gmk, qna, kna, qnc, knc,
                   qvec, kvec, vvec,
                   q1o, q2o, k1o, k2o, vo, zbo, qco, kco, vco, zdo, zgo):
    d = x_ref.shape[-1]
    x = x_ref[0]
    shift = m_ref[0, :, 0:d]
    scale = m_ref[0, :, d:2 * d]
    h = (_rms(x) * g_ref[...] * (1.0 + scale) + shift).astype(BF16)

    za = jnp.dot(h, w_ref[0, :, 0:OFF_B], preferred_element_type=F32)
    hw = DA_HEADS * DA_DIM
    cos, sup, sdn, gm = ca[...], ua[...], da[...], gma[...]
    qscale = DA_DIM ** -0.5 * LOG2E
    for t, (o, gain, sc, vec) in enumerate(((q1o, qna, qscale, qvec), (q2o, qna, qscale, qvec),
                                            (k1o, kna, 1.0, kvec), (k2o, kna, 1.0, kvec))):
        y = _norm_rope(za[:, t * hw:(t + 1) * hw], gm, gain[...], cos, sup, sdn, DA_DIM, DA_DIM // 4) * sc
        _store_slots(o, y, DA_DIM, vec[...])
    _store_slots(vo, za[:, 4 * hw:], DA_VDIM, vvec[...])

    zbo[0] = jnp.dot(h, w_ref[0, :, OFF_B:OFF_C], preferred_element_type=F32).astype(BF16)

    zc = jnp.dot(h, w_ref[0, :, OFF_C:OFF_D], preferred_element_type=F32)
    qw = WA_HEADS * WA_DIM
    kw = WA_KV_HEADS * WA_DIM
    y = _norm_rope(zc[:, 0:qw], gmq[...], qnc[...], cq[...], uq[...], dq[...], WA_DIM, WA_DIM // 4) * (WA_DIM ** -0.5 * LOG2E)
    _store_slots(qco, y, WA_DIM, 0.0)
    y = _norm_rope(zc[:, qw:qw + kw], gmk[...], knc[...], ck[...], uk[...], dk[...], WA_DIM, WA_DIM // 4)
    _store_slots(kco, y, WA_DIM, 0.0)
    _store_slots(vco, zc[:, qw + kw:], WA_DIM, vvec[...])

    zdo[0] = jnp.dot(h, w_ref[0, :, OFF_D:OFF_G], preferred_element_type=F32).astype(BF16)
    for k in range(N_BRANCH):
        zgo[0, :, k * d:(k + 1) * d] = jnp.dot(h, w_ref[0, :, OFF_G + k * d:OFF_G + (k + 1) * d],
                                               preferred_element_type=F32).astype(BF16)


def _in_proj(x, mod, g, w_all, layer, tables, consts):
    bx, lx, d = x.shape
    tm = min(512, lx)
    const = lambda b, i: (0, 0)
    row = lambda b, i: (b, i, 0)
    once = lambda a: pl.BlockSpec(a.shape, const, pipeline_mode=pl.Buffered(1))
    widths = [DA_HEADS * LANES] * 5 + [W_B, WA_HEADS * LANES, WA_KV_HEADS * LANES, WA_KV_HEADS * LANES, W_D, N_BRANCH * d]
    return pl.pallas_call(
        _inproj_kernel,
        grid=(bx, lx // tm),
        in_specs=[pl.BlockSpec((1, tm, d), row), pl.BlockSpec((1, 1, mod.shape[-1]), lambda b, i: (b, 0, 0)),
                  pl.BlockSpec((1, d), const)]
                 + [pl.BlockSpec((1,) + w_all.shape[1:], lambda b, i: (layer, 0, 0), pipeline_mode=pl.Buffered(1))]
                 + [pl.BlockSpec((tm, t.shape[1]), lambda b, i: (i, 0)) for t in tables]
                 + [once(c) for c in consts],
        out_specs=[pl.BlockSpec((1, tm, w), row) for w in widths],
        out_shape=[jax.ShapeDtypeStruct((bx, lx, w), BF16) for w in widths],
        compiler_params=_cp("parallel", "parallel"),
        name="in_proj",
    )(x, mod, g.reshape(1, d), w_all, *tables, *consts)


def _da_lambda(lam_ref, lam_init):
    lv = lam_ref[...]
    return (jnp.exp(jnp.sum(lv[0:1] * lv[1:2], keepdims=True)) - jnp.exp(jnp.sum(lv[2:3] * lv[3:4], keepdims=True))
            + lam_init)


def _dattn_kernel(lam_ref, sg_ref, q1_ref, q2_ref, *rest, lam_init, online, nsrc, tk_max):
    srcs = [rest[3 * s:3 * s + 3] for s in range(nsrc)]
    o_ref, acc1, acc2 = rest[3 * nsrc:]
    dn = (((1,), (1,)), ((), ()))
    q1 = q1_ref[0]
    q2 = q2_ref[0]
    tq = q1.shape[0]
    acc1[...] = jnp.zeros_like(acc1)
    acc2[...] = jnp.zeros_like(acc2)
    carry = (jnp.full((tq, 1), NEG_INF, F32),) * 2 if online else 0

    for k1_ref, k2_ref, v_ref in srcs:
        tk = min(tk_max, k1_ref.shape[1])

        def body(j, c, k1_ref=k1_ref, k2_ref=k2_ref, v_ref=v_ref, tk=tk):
            rows = pl.ds(pl.multiple_of(j * tk, tk), tk)
            vc = v_ref[0, rows, :]
            s1 = lax.dot_general(q1, k1_ref[0, rows, :], dn, preferred_element_type=F32)
            s2 = lax.dot_general(q2, k2_ref[0, rows, :], dn, preferred_element_type=F32)
            if online:
                m1, m2 = c
                n1 = jnp.maximum(m1, jnp.max(s1, axis=-1, keepdims=True))
                n2 = jnp.maximum(m2, jnp.max(s2, axis=-1, keepdims=True))
                acc1[...] = jnp.exp2(m1 - n1) * acc1[...] + jnp.dot(jnp.exp2(s1 - n1).astype(BF16), vc,
                                                                    preferred_element_type=F32)
                acc2[...] = jnp.exp2(m2 - n2) * acc2[...] + jnp.dot(jnp.exp2(s2 - n2).astype(BF16), vc,
                                                                    preferred_element_type=F32)
                return n1, n2
            acc1[...] += jnp.dot(jnp.exp2(s1).astype(BF16), vc, preferred_element_type=F32)
            acc2[...] += jnp.dot(jnp.exp2(s2).astype(BF16), vc, preferred_element_type=F32)
            return c

        carry = lax.fori_loop(0, k1_ref.shape[1] // tk, body, carry)

    dv = DA_VDIM
    a1 = acc1[...]
    a2 = acc2[...]
    lam = _da_lambda(lam_ref, lam_init)
    o = a1 * (1.0 / a1[:, dv:dv + 1]) - a2 * (lam / a2[:, dv:dv + 1])
    o = jnp.where(lax.broadcasted_iota(jnp.int32, o.shape, 1) < dv, o, 0.0)
    o = o * lax.rsqrt(jnp.sum(o * o, axis=-1, keepdims=True) * (1.0 / dv) + EPS)
    o_ref[0] = (o * (sg_ref[...] * (1.0 - lam_init))).astype(BF16)


def _diff_attention(q1, q2, srcs, fixed, lam_p, subln, lam_init):
    b, lq, _ = q1.shape
    h = DA_HEADS
    const = lambda b_, h_, i: (0, 0)
    sg = jnp.pad(subln, (0, LANES - DA_VDIM)).reshape(1, LANES)
    flat = [a for src in srcs for a in src]

    def call(online, *args):
        tq = min(256 if online else 2048, lq)
        qs = pl.BlockSpec((1, tq, LANES), lambda b_, h_, i: (b_, i, h_))
        return pl.pallas_call(
            functools.partial(_dattn_kernel, lam_init=lam_init, online=online, nsrc=len(srcs),
                              tk_max=256 if online else 512),
            grid=(b, h, lq // tq),
            in_specs=[pl.BlockSpec(lam_p.shape, const), pl.BlockSpec((1, LANES), const), qs, qs]
                     + [pl.BlockSpec((1, a.shape[1], LANES), lambda b_, h_, i: (b_, 0, h_)) for a in flat],
            out_specs=qs,
            out_shape=jax.ShapeDtypeStruct((b, lq, h * LANES), BF16),
            scratch_shapes=[pltpu.VMEM((tq, LANES), F32)] * 2,
            compiler_params=_cp("parallel", "parallel", "arbitrary"),
            name="diff_attention_online" if online else "diff_attention",
        )(*args)

    return lax.cond(fixed, functools.partial(call, False), functools.partial(call, True),
                    lam_p, sg, q1, q2, *flat)


def _hy_prep_kernel(zp_ref, zc_ref, zn_ref, w_ref, b_ref, p_ref, x0_ref):
    i = pl.program_id(1)
    last = pl.num_programs(1) - 1
    tl = zc_ref.shape[1]
    h = 2 * SUBLANES
    ext = jnp.concatenate([jnp.where(i == 0, 0.0, zp_ref[0].astype(F32)), zc_ref[0].astype(F32),
                           jnp.where(i == last, 0.0, zn_ref[0].astype(F32))], axis=0)
    n = ext.shape[0]
    w = w_ref[...]
    u = (pltpu.roll(ext, 1, 0) * w[0:1] + ext * w[1:2] + pltpu.roll(ext, n - 1, 0) * w[2:3] + b_ref[...])[h:h + tl]
    hw = HY_WIDTH
    x0_ref[0] = u[:, 0:hw]
    p_ref[0] = u[:, 2 * hw:3 * hw] * u[:, hw:2 * hw]


def _hy_prep(zb, conv_w, conv_b):
    bx, lx, w = zb.shape
    tl = min(512, lx)
    h = 2 * SUBLANES
    nh = lx // h
    per = tl // h
    out = jax.ShapeDtypeStruct((bx, lx, HY_WIDTH), F32)
    return pl.pallas_call(
        _hy_prep_kernel,
        grid=(bx, lx // tl),
        in_specs=[pl.BlockSpec((1, h, w), lambda b, i: (b, jnp.maximum(i * per - 1, 0), 0)),
                  pl.BlockSpec((1, tl, w), lambda b, i: (b, i, 0)),
                  pl.BlockSpec((1, h, w), lambda b, i: (b, jnp.minimum((i + 1) * per, nh - 1), 0)),
                  pl.BlockSpec(conv_w.shape, lambda b, i: (0, 0)),
                  pl.BlockSpec((1, w), lambda b, i: (0, 0))],
        out_specs=[pl.BlockSpec((1, tl, HY_WIDTH), lambda b, i: (b, i, 0))] * 2,
        out_shape=[out, out],
        compiler_params=_cp("parallel", "parallel"),
        name="hyena_prep",
    )(zb, zb, zb, conv_w, conv_b.reshape(1, w))


def _hy_filter_kernel(feat_ref, w1, b1, w2, b2, w3, b3, w4, fr_ref, dl_ref, filt_ref, ssq_ref, *, s):
    i = pl.program_id(0)
    tr = feat_ref.shape[0]
    feat = feat_ref[...]
    fr = fr_ref[...]
    dot = lambda a, w: _dot3(*_hi_lo(a), w[...])
    a = jnp.sin(fr * (dot(feat, w1) + b1[...]))
    a = jnp.sin(fr * (dot(a, w2) + b2[...]))
    a = jnp.sin(fr * (dot(a, w3) + b3[...]))
    coef = dot(a, w4)
    n = i * tr + lax.broadcasted_iota(jnp.int32, (tr, 1), 0)
    window = jnp.exp(-feat[:, 0:1] * dl_ref[...]) + HY_SHIFT
    half = jnp.where(n < s, coef[:, :HY_WIDTH], coef[:, HY_WIDTH:])
    filt = jnp.where(n == s, 0.0, half * window)
    filt_ref[...] = filt

    @pl.when(i == 0)
    def _():
        ssq_ref[...] = jnp.zeros_like(ssq_ref)

    ssq_ref[...] += jnp.sum(filt * filt, axis=0, keepdims=True)


def _hy_feat(s):
    t = jnp.linspace(0.0, 1.0, s, dtype=F32)[:, None]
    w = (2.0 * math.pi / s) * jnp.arange(s, dtype=F32)[:, None]
    bands = jnp.linspace(1e-4, HY_BANDS - 1, HY_BANDS, dtype=F32)[None, :]
    feat = jnp.concatenate([t, jnp.cos(w * bands), jnp.sin(w * bands)], axis=-1)
    feat = jnp.concatenate([feat, feat[:1], feat[:0:-1]], axis=0)
    return jnp.pad(feat, ((0, 0), (0, LANES - feat.shape[1])))


def _hy_filter(s, feat, w1, b1, w2, b2, w3, b3, w4, freq):
    n = 2 * s
    tr = min(512, n)
    deltas = jnp.abs(jnp.linspace(math.log(HY_TARGET) / HY_FAST_DECAY, math.log(HY_TARGET) / HY_SLOW_DECAY,
                                  HY_WIDTH, dtype=F32)).reshape(1, HY_WIDTH)
    w1p = jnp.pad(w1, ((0, LANES - w1.shape[0]), (0, 0)))
    row = lambda a: a.reshape(1, -1)
    args = (feat, w1p, row(b1), w2, row(b2), w3, row(b3), w4, row(freq), deltas)
    const = lambda i: (0, 0)
    return pl.pallas_call(
        functools.partial(_hy_filter_kernel, s=s),
        grid=(n // tr,),
        in_specs=[pl.BlockSpec((tr, LANES), lambda i: (i, 0))] + [pl.BlockSpec(a.shape, const) for a in args[1:]],
        out_specs=[pl.BlockSpec((tr, HY_WIDTH), lambda i: (i, 0)), pl.BlockSpec((1, HY_WIDTH), const)],
        out_shape=[jax.ShapeDtypeStruct((n, HY_WIDTH), F32), jax.ShapeDtypeStruct((1, HY_WIDTH), F32)],
        compiler_params=_cp("arbitrary"),
        name="hyena_filter",
    )(*args)


def _dft_factors(n):
    lg = n.bit_length() - 1
    assert 1 << lg == n
    n1 = 1 << (lg // 2)
    return n1, n // n1


def _dft_tables(n):
    n1, n2 = _dft_factors(n)
    ia = jnp.arange(n1, dtype=jnp.int32)
    ang1 = (2.0 * math.pi / n1) * ((ia[:, None] * ia[None, :]) % n1).astype(F32)
    f1 = jnp.concatenate([jnp.cos(ang1), -jnp.sin(ang1)], axis=0)
    c = jnp.arange(n1, dtype=jnp.int32)[:, None, None]
    d = jnp.arange(n2, dtype=jnp.int32)[None, :, None]
    b = jnp.arange(n2, dtype=jnp.int32)[None, None, :]
    ang = (2.0 * math.pi / n) * ((b * (c + n1 * d)) % n).astype(F32)
    re, im = jnp.cos(ang), -jnp.sin(ang)
    m1 = jnp.concatenate([jnp.concatenate([re, -im], axis=2), jnp.concatenate([im, re], axis=2)], axis=1)
    m2 = jnp.swapaxes(m1, 1, 2)
    ang4 = ang1[: n1 // 2]
    f4 = jnp.concatenate([jnp.cos(ang4), -jnp.sin(ang4)], axis=1) * (1.0 / n)
    return f1, _hi_lo_rows(m1), _hi_lo_rows(m2), _hi_lo_rows(_per_offset(f4))


def _hi_lo_rows(m):
    return jnp.concatenate(_hi_lo(m), axis=-2)


def _dot3s(ms, a):
    m = ms.shape[0] // 2
    ah, al = _hi_lo(a)
    t = jnp.dot(ms, ah, preferred_element_type=F32)
    return t[:m] + t[m:] + jnp.dot(ms[:m], al, preferred_element_type=F32)


def _dot3(mh, ml, a):
    ah, al = _hi_lo(a)
    return (jnp.dot(mh, ah, preferred_element_type=F32) + jnp.dot(mh, al, preferred_element_type=F32)
            + jnp.dot(ml, ah, preferred_element_type=F32))


HY_TB = SUBLANES


def _per_offset(f):
    return jnp.kron(f, jnp.eye(HY_TB, dtype=f.dtype))


def _hy_stage1_kernel(f_ref, x_ref, re_ref, im_ref):
    _, n1, tb, w = re_ref.shape
    y = _dot3s(f_ref[...], x_ref[0].reshape(-1, w))
    re_ref[0] = y[:n1 * tb].reshape(n1, tb, w)
    im_ref[0] = y[n1 * tb:].reshape(n1, tb, w)


def _hy_stage1(f, x4):
    bx, k, n2, w = x4.shape
    tb = HY_TB
    n1 = f.shape[0] // (4 * tb)
    out = jax.ShapeDtypeStruct((bx, n1, n2, w), F32)
    return pl.pallas_call(
        _hy_stage1_kernel,
        grid=(bx, n2 // tb),
        in_specs=[pl.BlockSpec(f.shape, lambda b, j: (0, 0))]
                 + [pl.BlockSpec((1, k, tb, w), lambda b, j: (b, 0, j, 0))],
        out_specs=[pl.BlockSpec((1, n1, tb, w), lambda b, j: (b, 0, j, 0))] * 2,
        out_shape=[out, out],
        compiler_params=_cp("parallel", "parallel"),
        name="hyena_dft_stage1",
    )(f, x4)


def _hy_mid_kernel(m1_ref, m2_ref, re_ref, im_ref, hre_ref, him_ref, ore_ref, oim_ref):
    n2 = re_ref.shape[2]
    for c in range(re_ref.shape[1]):
        a = jnp.concatenate([re_ref[0, c], im_ref[0, c]], axis=0)
        x = _dot3s(m1_ref[c], a)
        xre, xim = x[:n2], x[n2:]
        hre, him = hre_ref[c], him_ref[c]
        y = jnp.concatenate([xre * hre - xim * him, xre * him + xim * hre], axis=0)
        bb = _dot3s(m2_ref[c], y)
        ore_ref[0, c] = bb[:n2]
        oim_ref[0, c] = bb[n2:]


def _hy_spec_kernel(m1_ref, re_ref, im_ref, rs_ref, ore_ref, oim_ref):
    n2 = re_ref.shape[2]
    for c in range(re_ref.shape[1]):
        a = jnp.concatenate([re_ref[0, c], im_ref[0, c]], axis=0)
        x = _dot3s(m1_ref[c], a) * rs_ref[...]
        ore_ref[c] = x[:n2]
        oim_ref[c] = x[n2:]


def _hy_filter_spectrum(m1, are, aim, rs):
    _, n1, n2, w = are.shape
    cb = min(4, n1)
    blk = pl.BlockSpec((1, cb, n2, w), lambda c: (0, c, 0, 0))
    mblk = pl.BlockSpec((cb, 4 * n2, 2 * n2), lambda c: (c, 0, 0))
    oblk = pl.BlockSpec((cb, n2, w), lambda c: (c, 0, 0))
    out = jax.ShapeDtypeStruct((n1, n2, w), F32)
    return pl.pallas_call(
        _hy_spec_kernel,
        grid=(n1 // cb,),
        in_specs=[mblk, blk, blk, pl.BlockSpec((1, w), lambda c: (0, 0))],
        out_specs=[oblk, oblk],
        out_shape=[out, out],
        compiler_params=_cp("parallel"),
        name="hyena_filter_spectrum",
    )(m1, are, aim, rs)


def _hy_mid(m1, m2, are, aim, hre, him):
    bx, n1, n2, w = are.shape
    cb = min(8, n1)
    blk = pl.BlockSpec((1, cb, n2, w), lambda c, b: (b, c, 0, 0))
    mblk = pl.BlockSpec((cb, 4 * n2, 2 * n2), lambda c, b: (c, 0, 0))
    hblk = pl.BlockSpec((cb, n2, w), lambda c, b: (c, 0, 0))
    out = jax.ShapeDtypeStruct((bx, n1, n2, w), F32)
    return pl.pallas_call(
        _hy_mid_kernel,
        grid=(n1 // cb, bx),
        in_specs=[mblk, mblk, blk, blk, hblk, hblk],
        out_specs=[blk, blk],
        out_shape=[out, out],
        compiler_params=_cp("parallel", "parallel"),
        name="hyena_dft_mid",
    )(m1, m2, are, aim, hre, him)


def _hy_last_kernel(f_ref, re_ref, im_ref, p_ref, bias_ref, o_ref):
    _, k, tb, w = p_ref.shape
    spec = jnp.concatenate([re_ref[0].reshape(-1, w), im_ref[0].reshape(-1, w)], axis=0)
    y = _dot3s(f_ref[...], spec)
    o_ref[0] = y.reshape(k, tb, w) + p_ref[0] * bias_ref[...]


def _hy_last(f4, bre, bim, p4, bias):
    bx, n1, n2, w = bre.shape
    k = n1 // 2
    tb = HY_TB
    big = pl.BlockSpec((1, n1, tb, w), lambda b, j: (b, 0, j, 0))
    small = pl.BlockSpec((1, k, tb, w), lambda b, j: (b, 0, j, 0))
    return pl.pallas_call(
        _hy_last_kernel,
        grid=(bx, n2 // tb),
        in_specs=[pl.BlockSpec(f4.shape, lambda b, j: (0, 0)), big, big, small,
                  pl.BlockSpec((1, w), lambda b, j: (0, 0))],
        out_specs=small,
        out_shape=jax.ShapeDtypeStruct((bx, k, n2, w), F32),
        compiler_params=_cp("parallel", "parallel"),
        name="hyena_dft_last",
    )(f4, bre, bim, p4, bias)


def _hyena(zb, conv_w, conv_b, filt_params, hy_bias, consts):
    bx, lx, _ = zb.shape
    feat, (f1, m1, m2, f4) = consts
    n = 2 * lx
    n1, n2 = _dft_factors(n)
    w = HY_WIDTH
    filt, ssq = _hy_filter(lx, feat, *filt_params)
    rs = lax.rsqrt(ssq + EPS)
    fre, fim = _hy_stage1(_hi_lo_rows(_per_offset(f1)), filt.reshape(1, n1, n2, w))
    hre, him = _hy_filter_spectrum(m1, fre, fim, rs)
    p, x0 = _hy_prep(zb, conv_w, conv_b)
    k = n1 // 2
    p4 = p.reshape(bx, k, n2, w)
    are, aim = _hy_stage1(_hi_lo_rows(_per_offset(f1[:, :k])), p4)
    bre, bim = _hy_mid(m1, m2, are, aim, hre, him)
    t = _hy_last(f4, bre, bim, p4, hy_bias.reshape(1, w))
    return x0, t.reshape(bx, lx, w)


WA_SHIFT_MAX = 35.0


def _wattn_kernel(sk_ref, q_ref, bias_ref, *rest, banded, fixed):
    if banded:
        kp_ref, kc_ref, kn_ref, kx_ref, vp_ref, vc_ref, vn_ref, vx_ref, o_ref = rest
    else:
        kx_ref, vx_ref, o_ref = rest
    i = pl.program_id(1)
    last = pl.num_programs(1) - 1
    qb = q_ref.shape[1]
    top = lax.broadcasted_iota(jnp.int32, (2 * qb, 1), 0) < qb
    if banded:
        c = lax.broadcasted_iota(jnp.int32, (1, bias_ref.shape[2]), 1) - BLOCK
        outside = ((c < 0) & (i == 0)) | ((c >= qb) & (c < qb + BLOCK) & (i == last))
        edge = jnp.where(outside, NEG_INF, 0.0)
    for h in range(WA_KV_HEADS):
        kl = slice(h * LANES, (h + 1) * LANES)
        ql = h * WA_GROUP * LANES
        q = jnp.concatenate([q_ref[0, :, ql:ql + LANES], q_ref[0, :, ql + LANES:ql + 2 * LANES]], axis=0)
        if banded:
            kk = jnp.concatenate([kp_ref[0, :, kl], kc_ref[0, :, kl], kn_ref[0, :, kl], kx_ref[0, :, kl]], axis=0)
            vv = jnp.concatenate([vp_ref[0, :, kl], vc_ref[0, :, kl], vn_ref[0, :, kl], vx_ref[0, :, kl]], axis=0)
        else:
            kk, vv = kx_ref[0, :, kl], vx_ref[0, :, kl]
        s = lax.dot_general(q, kk, (((1,), (1,)), ((), ())), preferred_element_type=F32) + bias_ref[h]
        if banded:
            s = s + edge
        sk = jnp.where(top, sk_ref[h * WA_GROUP], sk_ref[h * WA_GROUP + 1])
        if fixed:
            p, sink_term = jnp.exp2(s), sk
        else:
            m = jnp.maximum(jnp.max(s, axis=-1, keepdims=True), sk)
            p, sink_term = jnp.exp2(s - m), jnp.exp2(sk - m)
        acc = jnp.dot(p.astype(BF16), vv, preferred_element_type=F32)
        o = (acc * (1.0 / (acc[:, WA_DIM:WA_DIM + 1] + sink_term))).astype(BF16)
        o_ref[0, :, ql:ql + 2 * LANES] = jnp.concatenate([o[:qb], o[qb:]], axis=1)


def _window_attention(q, k, v, kx, vx, sink, bound, banded):
    b, lq, _ = q.shape
    cx = kx.shape[1]
    qb = min(256, lq)
    per = qb // BLOCK
    nblk = lq // BLOCK
    kvw = WA_KV_HEADS * LANES
    side = lambda f: pl.BlockSpec((1, BLOCK, kvw), f)
    prev = side(lambda b_, i: (b_, jnp.maximum(i * per - 1, 0), 0))
    nxt = side(lambda b_, i: (b_, jnp.minimum((i + 1) * per, nblk - 1), 0))
    cur = pl.BlockSpec((1, qb, kvw), lambda b_, i: (b_, i, 0))
    ctx = pl.BlockSpec((1, cx, kvw), lambda b_, i: (b_, 0, 0))
    qspec = pl.BlockSpec((1, qb, WA_HEADS * LANES), lambda b_, i: (b_, i, 0))
    mask = jnp.zeros((WA_GROUP * qb, cx), F32)
    if banded:
        r = jnp.arange(WA_GROUP * qb)[:, None] % qb
        c = jnp.arange(qb + 2 * BLOCK)[None, :] - BLOCK
        mask = jnp.concatenate([jnp.where(jnp.abs(r - c) <= BLOCK, 0.0, NEG_INF).astype(F32), mask], axis=1)
        specs, args = [prev, cur, nxt, ctx, prev, cur, nxt, ctx], (k, k, k, kx, v, v, v, vx)
    else:
        specs, args = [ctx, ctx], (kx, vx)
    fixed = bound <= WA_SHIFT_MAX
    shift = jnp.maximum(bound, sink)
    rows = jnp.repeat(shift.reshape(WA_KV_HEADS, WA_GROUP), qb, axis=1)[:, :, None]
    bias = mask[None] - jnp.where(fixed, LOG2E * rows, 0.0)
    sk = jnp.where(fixed, jnp.exp2(LOG2E * (sink - shift)), LOG2E * sink)

    def call(fixed_, *ops):
        return pl.pallas_call(
            functools.partial(_wattn_kernel, banded=banded, fixed=fixed_),
            grid=(b, lq // qb),
            in_specs=[pl.BlockSpec(memory_space=pltpu.SMEM), qspec,
                      pl.BlockSpec(bias.shape, lambda b_, i: (0, 0, 0))] + specs,
            out_specs=qspec,
            out_shape=jax.ShapeDtypeStruct((b, lq, WA_HEADS * LANES), BF16),
            compiler_params=_cp("parallel", "arbitrary"),
            name="window_attention" if fixed_ else "window_attention_online",
        )(*ops)

    return lax.cond(fixed, functools.partial(call, True), functools.partial(call, False), sk, q, bias, *args)


def _conf_kernel(zp_ref, zc_ref, zn_ref, w_ref, b_ref, lg_ref, lb_ref, o_ref, ext, sh, *, halo):
    i = pl.program_id(1)
    last = pl.num_programs(1) - 1
    tl = zc_ref.shape[1]
    cw = CF_WIDTH

    def glu(z_ref):
        z = z_ref[0].astype(F32)
        return z[:, :cw] * jax.nn.sigmoid(z[:, cw:])

    ext[0:halo] = jnp.where(i == 0, 0.0, glu(zp_ref))
    ext[halo:halo + tl] = glu(zc_ref)
    ext[halo + tl:] = jnp.where(i == last, 0.0, glu(zn_ref))
    for r in range(1, SUBLANES):
        sh[r - 1] = ext[pl.ds(r, sh.shape[1]), :]
    w = w_ref[...]
    u = jnp.zeros((tl, cw), F32) + b_ref[...]
    for j in range(CF_TAPS):
        off = halo - CF_TAPS // 2 + j
        base, r = off // SUBLANES * SUBLANES, off % SUBLANES
        tap = ext[pl.ds(base, tl), :] if r == 0 else sh[r - 1, pl.ds(base, tl), :]
        u = u + tap * w[j:j + 1]
    uc = u - jnp.mean(u, axis=-1, keepdims=True)
    y = uc * lax.rsqrt(jnp.mean(uc * uc, axis=-1, keepdims=True) + EPS) * lg_ref[...] + lb_ref[...]
    o_ref[0] = _silu(y).astype(BF16)


def _conformer(zd, dw_w, dw_b, ln_g, ln_b):
    bx, lx, w = zd.shape
    tl = min(512, lx)
    halo = 2 * SUBLANES
    nh = lx // halo
    per = tl // halo
    row = lambda a: a.reshape(1, -1)
    const = lambda b, i: (0, 0)
    return pl.pallas_call(
        functools.partial(_conf_kernel, halo=halo),
        grid=(bx, lx // tl),
        in_specs=[pl.BlockSpec((1, halo, w), lambda b, i: (b, jnp.maximum(i * per - 1, 0), 0)),
                  pl.BlockSpec((1, tl, w), lambda b, i: (b, i, 0)),
                  pl.BlockSpec((1, halo, w), lambda b, i: (b, jnp.minimum((i + 1) * per, nh - 1), 0)),
                  pl.BlockSpec(dw_w.shape, const)] + [pl.BlockSpec((1, CF_WIDTH), const)] * 3,
        out_specs=pl.BlockSpec((1, tl, CF_WIDTH), lambda b, i: (b, i, 0)),
        out_shape=jax.ShapeDtypeStruct((bx, lx, CF_WIDTH), BF16),
        scratch_shapes=[pltpu.VMEM((tl + 2 * halo, CF_WIDTH), F32),
                        pltpu.VMEM((SUBLANES - 1, tl + 2 * halo - SUBLANES, CF_WIDTH), F32)],
        compiler_params=_cp("parallel", "parallel"),
        name="conformer_conv",
    )(zd, zd, zd, dw_w, row(dw_b), row(ln_g), row(ln_b))


def _route_rows(x, m_ref, g_ref, wr_ref, br_ref, rows_ref, gi_ref):
    tm, d = x.shape
    half = d // 2
    lane = lax.broadcasted_iota(jnp.int32, (tm, LANES), 1).astype(F32)
    h = _rms(x) * g_ref[...] * (1.0 + m_ref[0, :, 4 * d:5 * d]) + m_ref[0, :, 3 * d:4 * d]
    bits = lax.bitcast_convert_type(h.astype(BF16).astype(F32), jnp.int32)
    rows_ref[0, :, :half] = bits[:, :half] | lax.shift_right_logical(bits[:, half:], 16)
    lg = _dot3(*_hi_lo(h), wr_ref[...]) + br_ref[...]
    isg = lane < N_GROUPS
    gmax = jnp.max(jnp.where(isg, lg, NEG_INF), axis=-1, keepdims=True)
    gi = jnp.min(jnp.where(isg & (lg == gmax), lane, LANES), axis=-1, keepdims=True)
    gw = 1.0 / jnp.sum(jnp.where(isg, jnp.exp(lg - gmax), 0.0), axis=-1, keepdims=True)
    lo = N_GROUPS + gi * EXP_PER_GROUP
    ise = (lane >= lo) & (lane < lo + EXP_PER_GROUP)
    le = jnp.where(ise, lg, NEG_INF)
    m1 = jnp.max(le, axis=-1, keepdims=True)
    i1 = jnp.min(jnp.where(ise & (le == m1), lane, LANES), axis=-1, keepdims=True)
    ise2 = ise & (lane != i1)
    le2 = jnp.where(ise2, lg, NEG_INF)
    m2 = jnp.max(le2, axis=-1, keepdims=True)
    i2 = jnp.min(jnp.where(ise2 & (le2 == m2), lane, LANES), axis=-1, keepdims=True)
    r = jnp.exp(m2 - m1)
    wa = gw / (1.0 + r)
    gates = jnp.where(lane == i1 - lo, wa, 0.0) + jnp.where(lane == i2 - lo, wa * r, 0.0)
    rows_ref[0, :, half:] = lax.bitcast_convert_type(gates, jnp.int32)
    gi_ref[0] = gi.astype(jnp.int32)


def _merge_kernel(x_ref, m_ref, ya, yb0, ybt, yc, yd, zg_ref, bg_ref, wa, wb, wc, wd, wo_ref, *rest, goff):
    o_ref = rest[-1] if len(rest) == 1 else rest[3]
    d = x_ref.shape[-1]
    acc = jnp.zeros(x_ref.shape[1:], F32)
    ys = (ya[0], (yb0[0] * ybt[0]).astype(BF16), yc[0], yd[0])
    for i, (y, w) in enumerate(zip(ys, (wa, wb, wc, wd))):
        gate = jax.nn.sigmoid(zg_ref[0, :, i * d:(i + 1) * d].astype(F32) + bg_ref[:, i * d:(i + 1) * d])
        acc = acc + gate * jnp.dot(y, w[...], preferred_element_type=F32)
    out = jnp.dot(acc.astype(BF16), wo_ref[...], preferred_element_type=F32)
    xn = x_ref[0] + m_ref[0, :, goff:goff + d] * out
    o_ref[0] = xn
    if len(rest) > 1:
        g_ref, wr_ref, br_ref, _, rows_ref, gi_ref = rest
        _route_rows(xn, m_ref, g_ref, wr_ref, br_ref, rows_ref, gi_ref)


def _merge(x, mod, ys, zg, b_gate, wbs, w_out, router=None):
    bx, lx, d = x.shape
    tm = min(512, lx)
    row = lambda b, i: (b, i, 0)
    const = lambda b, i: (0, 0)
    once = lambda a: pl.BlockSpec(a.shape, const, pipeline_mode=pl.Buffered(1))
    extra = (router[0].reshape(1, d), router[1], router[2]) if router else ()
    rw = d // 2 + LANES
    out_specs = [pl.BlockSpec((1, tm, d), row), pl.BlockSpec((1, tm, rw), row), pl.BlockSpec((1, tm, 1), row)]
    out_shape = [jax.ShapeDtypeStruct(x.shape, F32), jax.ShapeDtypeStruct((bx, lx, rw), jnp.int32),
                 jax.ShapeDtypeStruct((bx, lx, 1), jnp.int32)]
    return pl.pallas_call(
        functools.partial(_merge_kernel, goff=2 * d),
        grid=(bx, lx // tm),
        in_specs=[pl.BlockSpec((1, tm, d), row), pl.BlockSpec((1, 1, mod.shape[-1]), lambda b, i: (b, 0, 0))]
                 + [pl.BlockSpec((1, tm, y.shape[-1]), row) for y in ys]
                 + [pl.BlockSpec((1, tm, N_BRANCH * d), row), pl.BlockSpec((1, N_BRANCH * d), const)]
                 + [once(w) for w in wbs] + [once(w_out)] + [once(a) for a in extra],
        out_specs=out_specs if router else out_specs[0],
        out_shape=out_shape if router else out_shape[0],
        input_output_aliases={0: 0},
        compiler_params=_cp("parallel", "parallel"),
        name="merge",
    )(x, mod, *ys, zg, b_gate.reshape(1, -1), *wbs, w_out, *extra)


def _slot_rows(w, group):
    n = w.shape[0] // group
    return jnp.pad(w.reshape(n, group, -1), ((0, 0), (0, LANES - group), (0, 0))).reshape(n * LANES, -1)


def _moe_kernel(x_ref, m_ref, g_ref, wr_ref, br_ref, w1_ref, w3_ref, w2_ref, o_ref, h_sc, gate_sc, acc_sc):
    e = pl.program_id(2)
    d = x_ref.shape[-1]
    tm = x_ref.shape[1]
    lane = lax.broadcasted_iota(jnp.int32, (tm, LANES), 1).astype(F32)

    @pl.when(e == 0)
    def _():
        h = _rms(x_ref[0]) * g_ref[...] * (1.0 + m_ref[0, :, 4 * d:5 * d]) + m_ref[0, :, 3 * d:4 * d]
        h_sc[...] = h.astype(BF16)
        lg = _dot3(*_hi_lo(h), wr_ref[...]) + br_ref[...]
        isg = lane < N_GROUPS
        gmax = jnp.max(jnp.where(isg, lg, NEG_INF), axis=-1, keepdims=True)
        gi = jnp.min(jnp.where(isg & (lg == gmax), lane, LANES), axis=-1, keepdims=True)
        gw = 1.0 / jnp.sum(jnp.where(isg, jnp.exp(lg - gmax), 0.0), axis=-1, keepdims=True)
        lo = N_GROUPS + gi * EXP_PER_GROUP
        ise = (lane >= lo) & (lane < lo + EXP_PER_GROUP)
        le = jnp.where(ise, lg, NEG_INF)
        m1 = jnp.max(le, axis=-1, keepdims=True)
        i1 = jnp.min(jnp.where(ise & (le == m1), lane, LANES), axis=-1, keepdims=True)
        ise2 = ise & (lane != i1)
        le2 = jnp.where(ise2, lg, NEG_INF)
        m2 = jnp.max(le2, axis=-1, keepdims=True)
        i2 = jnp.min(jnp.where(ise2 & (le2 == m2), lane, LANES), axis=-1, keepdims=True)
        r = jnp.exp(m2 - m1)
        wa = gw / (1.0 + r)
        gate_sc[...] = jnp.where(lane == i1, wa, 0.0) + jnp.where(lane == i2, wa * r, 0.0)
        acc_sc[...] = jnp.zeros_like(acc_sc)

    hb = h_sc[...]
    u = (_silu(jnp.dot(hb, w1_ref[0].astype(BF16), preferred_element_type=F32))
         * jnp.dot(hb, w3_ref[0].astype(BF16), preferred_element_type=F32))
    ge = jnp.sum(jnp.where(lane == (e + N_GROUPS).astype(F32), gate_sc[...], 0.0), axis=-1, keepdims=True)
    acc_sc[...] += ge * jnp.dot(u.astype(BF16), w2_ref[0].astype(BF16), preferred_element_type=F32)

    @pl.when(e == pl.num_programs(2) - 1)
    def _():
        o_ref[0] = x_ref[0] + m_ref[0, :, 5 * d:6 * d] * acc_sc[...]


def _moe(x, mod, g, w_router, b_router, w1, w3, w2):
    bx, lx, d = x.shape
    tm = min(1024, lx)
    ne, _, f = w1.shape
    row = lambda b, i, e: (b, i, 0)
    const = lambda b, i, e: (0, 0)
    return pl.pallas_call(
        _moe_kernel,
        grid=(bx, lx // tm, ne),
        in_specs=[pl.BlockSpec((1, tm, d), row), pl.BlockSpec((1, 1, mod.shape[-1]), lambda b, i, e: (b, 0, 0)),
                  pl.BlockSpec((1, d), const), pl.BlockSpec((d, LANES), const), pl.BlockSpec((1, LANES), const),
                  pl.BlockSpec((1, d, f), lambda b, i, e: (e, 0, 0)), pl.BlockSpec((1, d, f), lambda b, i, e: (e, 0, 0)),
                  pl.BlockSpec((1, f, d), lambda b, i, e: (e, 0, 0))],
        out_specs=pl.BlockSpec((1, tm, d), row),
        out_shape=jax.ShapeDtypeStruct(x.shape, F32),
        scratch_shapes=[pltpu.VMEM((tm, d), BF16), pltpu.VMEM((tm, LANES), F32), pltpu.VMEM((tm, d), F32)],
        input_output_aliases={0: 0},
        compiler_params=_cp("parallel", "parallel", "arbitrary"),
        name="moe",
    )(x, mod, g.reshape(1, d), w_router, b_router, w1, w3, w2)


SC_CORES = 2
SC_SUBCORES = 16
SC_CHUNK = 64
MOE_ROWS = 1024


def _sc_gather(table, idx):
    n = idx.shape[0]
    w = table.shape[1]
    per = n // (SC_CORES * SC_SUBCORES)
    assert per * SC_CORES * SC_SUBCORES == n and per % SC_CHUNK == 0
    mesh = plsc.VectorSubcoreMesh(core_axis_name="c", subcore_axis_name="s")

    @functools.partial(
        pl.kernel, mesh=mesh, out_type=jax.ShapeDtypeStruct((n, w), table.dtype),
        scratch_types=[pltpu.VMEM((SC_CHUNK,), jnp.int32), pltpu.VMEM((SC_CHUNK, w), table.dtype),
                       pltpu.SemaphoreType.DMA],
        name="sc_row_gather")
    def gather(table_hbm, idx_hbm, out_hbm, idx_v, rows_v, sem):
        base = (lax.axis_index("s") * SC_CORES + lax.axis_index("c")) * per

        @pl.loop(0, per // SC_CHUNK)
        def _(j):
            off = pl.multiple_of(base + j * SC_CHUNK, SC_CHUNK)
            pltpu.sync_copy(idx_hbm.at[pl.ds(off, SC_CHUNK)], idx_v)
            pltpu.async_copy(table_hbm.at[idx_v], rows_v, sem).wait()
            pltpu.sync_copy(rows_v, out_hbm.at[pl.ds(off, SC_CHUNK)])

    return gather(table, idx)


def _sc_scatter(rows, idx, n_out):
    n, w = rows.shape
    per = n // (SC_CORES * SC_SUBCORES)
    assert per * SC_CORES * SC_SUBCORES == n and per % SC_CHUNK == 0
    mesh = plsc.VectorSubcoreMesh(core_axis_name="c", subcore_axis_name="s")

    @functools.partial(
        pl.kernel, mesh=mesh, out_type=jax.ShapeDtypeStruct((n_out, w), rows.dtype),
        scratch_types=[pltpu.VMEM((SC_CHUNK,), jnp.int32), pltpu.VMEM((SC_CHUNK, w), rows.dtype),
                       pltpu.SemaphoreType.DMA],
        name="sc_row_scatter")
    def scatter(rows_hbm, idx_hbm, out_hbm, idx_v, rows_v, sem):
        base = (lax.axis_index("s") * SC_CORES + lax.axis_index("c")) * per

        @pl.loop(0, per // SC_CHUNK)
        def _(j):
            off = pl.multiple_of(base + j * SC_CHUNK, SC_CHUNK)
            pltpu.sync_copy(idx_hbm.at[pl.ds(off, SC_CHUNK)], idx_v)
            pltpu.sync_copy(rows_hbm.at[pl.ds(off, SC_CHUNK)], rows_v)
            pltpu.async_copy(rows_v, out_hbm.at[idx_v], sem).wait()

    return scatter(rows, idx)


def _gmoe_kernel(grp_ref, nv_ref, xs_ref, w1_ref, w3_ref, w2_ref, o_ref, h_sc, acc_sc):
    i = pl.program_id(0)
    e = pl.program_id(1)
    tm, d = h_sc.shape
    half = d // 2
    valid = lax.broadcasted_iota(jnp.int32, (tm, 1), 0) < nv_ref[i]

    @pl.when(nv_ref[i] > 0)
    def _():
        @pl.when(e == 0)
        def _():
            pk = xs_ref[:, :half]
            h = jnp.concatenate([lax.bitcast_convert_type(pk & -65536, F32),
                                 lax.bitcast_convert_type(lax.shift_left(pk, 16), F32)], axis=1)
            h_sc[...] = jnp.where(valid, h, 0.0).astype(BF16)
            acc_sc[...] = jnp.zeros_like(acc_sc)

        hb = h_sc[...]
        u = (_silu(jnp.dot(hb, w1_ref[0].astype(BF16), preferred_element_type=F32))
             * jnp.dot(hb, w3_ref[0].astype(BF16), preferred_element_type=F32))
        lane = lax.broadcasted_iota(jnp.int32, (tm, LANES), 1)
        gates = lax.bitcast_convert_type(xs_ref[:, half:], F32)
        ge = jnp.sum(jnp.where(valid & (lane == e), gates, 0.0), axis=-1, keepdims=True)
        acc_sc[...] += ge * jnp.dot(u.astype(BF16), w2_ref[0].astype(BF16), preferred_element_type=F32)

    @pl.when(e == pl.num_programs(1) - 1)
    def _():
        o_ref[...] = jnp.where(nv_ref[i] > 0, acc_sc[...], 0.0)


def _grouped_moe(grp, nv, xs, w1, w3, w2):
    p, dw = xs.shape
    d = 2 * (dw - LANES)
    _, _, f = w1.shape
    wmap = lambda i, e, grp, nv: (grp[i] * EXP_PER_GROUP + e, 0, 0)
    rows = lambda i, e, grp, nv: (i, 0)
    return pl.pallas_call(
        _gmoe_kernel,
        grid_spec=pltpu.PrefetchScalarGridSpec(
            num_scalar_prefetch=2,
            grid=(p // MOE_ROWS, EXP_PER_GROUP),
            in_specs=[pl.BlockSpec((MOE_ROWS, dw), rows),
                      pl.BlockSpec((1, d, f), wmap), pl.BlockSpec((1, d, f), wmap), pl.BlockSpec((1, f, d), wmap)],
            out_specs=pl.BlockSpec((MOE_ROWS, d), rows),
            scratch_shapes=[pltpu.VMEM((MOE_ROWS, d), BF16), pltpu.VMEM((MOE_ROWS, d), F32)]),
        out_shape=jax.ShapeDtypeStruct((p, d), F32),
        compiler_params=_cp("arbitrary", "arbitrary"),
        name="moe_experts",
    )(grp, nv, xs, w1, w3, w2)


def _residual_kernel(x_ref, m_ref, y_ref, o_ref):
    d = x_ref.shape[-1]
    o_ref[0] = x_ref[0] + m_ref[0, :, 5 * d:6 * d] * y_ref[0]


def _residual(x, mod, y):
    bx, lx, d = x.shape
    tm = min(1024, lx)
    row = lambda b, i: (b, i, 0)
    return pl.pallas_call(
        _residual_kernel,
        grid=(bx, lx // tm),
        in_specs=[pl.BlockSpec((1, tm, d), row), pl.BlockSpec((1, 1, mod.shape[-1]), lambda b, i: (b, 0, 0)),
                  pl.BlockSpec((1, tm, d), row)],
        out_specs=pl.BlockSpec((1, tm, d), row),
        out_shape=jax.ShapeDtypeStruct(x.shape, F32),
        input_output_aliases={0: 0},
        compiler_params=_cp("parallel", "parallel"),
        name="moe_residual",
    )(x, mod, y)


def _moe_sorted(x, mod, rows, gi, w1, w3, w2):
    bx, lx, d = x.shape
    t = bx * lx
    gi = gi.reshape(t)
    onehot = (gi[:, None] == jnp.arange(N_GROUPS, dtype=jnp.int32)[None, :]).astype(jnp.int32)
    csum = jnp.cumsum(onehot, axis=0)
    counts = csum[-1]
    rank = jnp.take_along_axis(csum, gi[:, None], axis=1)[:, 0] - 1
    padded = (counts + MOE_ROWS - 1) // MOE_ROWS * MOE_ROWS
    pend = jnp.cumsum(padded)
    pstart = pend - padded
    pos = (pstart[gi] + rank).astype(jnp.int32)
    p = t + N_GROUPS * MOE_ROWS
    bstart = jnp.arange(p // MOE_ROWS, dtype=jnp.int32) * MOE_ROWS
    grp = jnp.minimum(jnp.searchsorted(pend, bstart, side="right"), N_GROUPS - 1).astype(jnp.int32)
    nv = jnp.clip(pstart[grp] + counts[grp] - bstart, 0, MOE_ROWS).astype(jnp.int32)
    xs = _sc_scatter(rows.reshape(t, rows.shape[-1]), pos, p)
    ys = _grouped_moe(grp, nv, xs, w1, w3, w2)
    yt = _sc_gather(ys, pos)
    return _residual(x, mod, yt.reshape(bx, lx, d))


def kernel(x, c, ctx, c_ctx, w_mod, b_mod, norm1_g, norm2_g, w_in, b_gate, da_qn, da_kn, da_lam, da_subln, hy_conv_w, hy_conv_b, hf_w1, hf_b1, hf_w2, hf_b2, hf_w3, hf_b3, hf_w4, hf_freq, hy_bias, wa_qn, wa_kn, wa_sink, cf_dw_w, cf_dw_b, cf_ln_g, cf_ln_b, w_branch, w_out, w_rg, b_rg, w_re, b_re, w1, w3, w2):
    b, s, d = x.shape
    cl = ctx.shape[1]
    depth = w_mod.shape[0]
    assert s % 256 == 0 and cl % 256 == 0 and s % GRID_W == 0

    nrow = -(-(b + 1) // SUBLANES) * SUBLANES
    crows = jnp.zeros((nrow, d), F32).at[:b].set(c).at[b].set(c_ctx)
    mods = _mod_vectors(crows, w_mod, b_mod)

    aw, qw, kw = DA_HEADS * DA_DIM, WA_HEADS * WA_DIM, WA_KV_HEADS * WA_DIM
    tab_lat = (*_rope_tables(s, DA_DIM, DA_HEADS), *_rope_tables(s, WA_DIM, WA_HEADS), *_rope_tables(s, WA_DIM, WA_KV_HEADS))
    tab_ctx = (*_unit_tables(cl, aw), *_unit_tables(cl, qw), *_unit_tables(cl, kw))
    hy_lat = (_hy_feat(s), _dft_tables(2 * s))
    hy_ctx = (_hy_feat(cl), _dft_tables(2 * cl))
    gms = (_group_ones(aw, DA_DIM), _group_ones(qw, WA_DIM), _group_ones(kw, WA_DIM))
    qvec = _slot_fill(DA_DIM, DA_DIM + 1, 1.0)
    vvec = _slot_fill(DA_VDIM, DA_VDIM + DA_ONES, 1.0)

    w_in_b = w_in.astype(BF16)
    xc = ctx
    for l in range(depth):
        last = l == depth - 1
        lam_init = 0.8 - 0.6 * math.exp(-0.3 * l)
        mod_x = mods[l, :b][:, None, :]
        mod_c = jnp.broadcast_to(mods[l, b][None, None, :], (b, 1, mods.shape[-1]))
        shift = 1.02 * LOG2E * DA_DIM ** 0.5 * jnp.max(jnp.abs(da_qn[l])) * jnp.max(jnp.abs(da_kn[l]))
        fixed = shift <= DA_SHIFT_MAX
        kvec = _slot_fill(DA_DIM, DA_DIM + 1, jnp.where(fixed, -shift, 0.0))
        tile = lambda a, n: jnp.tile(a, n).reshape(1, -1)
        consts = (*gms, tile(da_qn[l], DA_HEADS), tile(da_kn[l], DA_HEADS), tile(wa_qn[l], WA_HEADS),
                  tile(wa_kn[l], WA_KV_HEADS), qvec, kvec, vvec)
        q1, q2, k1, k2, v, zb, qc, kc, vc, zd, zg = _in_proj(x, mod_x, norm1_g[l], w_in_b, l, tab_lat, consts)
        q1x, q2x, k1x, k2x, vx, zbx, qcx, kcx, vcx, zdx, zgx = _in_proj(xc, mod_c, norm1_g[l], w_in_b, l, tab_ctx, consts)
        filt_params = (hf_w1[l], hf_b1[l], hf_w2[l], hf_b2[l], hf_w3[l], hf_b3[l], hf_w4[l], hf_freq[l])
        wb4 = w_branch[l].astype(BF16)
        wbs = (_slot_rows(wb4[0], DA_VDIM), wb4[1], _slot_rows(wb4[2], WA_DIM), wb4[3])
        wo = w_out[l].astype(BF16)

        ya = _diff_attention(q1, q2, [(k1, k2, v), (k1x, k2x, vx)], fixed, da_lam[l], da_subln[l], lam_init)
        yb = _hyena(zb, hy_conv_w[l], hy_conv_b[l], filt_params, hy_bias[l], hy_lat)
        wbound = 1.02 * WA_DIM ** 0.5 * jnp.max(jnp.abs(wa_qn[l])) * jnp.max(jnp.abs(wa_kn[l]))
        yc_ = _window_attention(qc, kc, vc, kcx, vcx, wa_sink[l], wbound, True)
        yd = _conformer(zd, cf_dw_w[l], cf_dw_b[l], cf_ln_g[l], cf_ln_b[l])
        w_router = jnp.pad(jnp.concatenate([w_rg[l], w_re[l]], axis=1), ((0, 0), (0, LANES - N_GROUPS - N_EXPERTS)))
        b_router = jnp.pad(jnp.concatenate([b_rg[l], b_re[l]]), (0, LANES - N_GROUPS - N_EXPERTS)).reshape(1, LANES)
        ew = (w1[l], w3[l], w2[l])
        x, rows, gi = _merge(x, mod_x, (ya, *yb, yc_, yd), zg, b_gate[l], wbs, wo, (norm2_g[l], w_router, b_router))

        if not last:
            yca = _diff_attention(q1x, q2x, [(k1x, k2x, vx)], fixed, da_lam[l], da_subln[l], lam_init)
            ycb = _hyena(zbx, hy_conv_w[l], hy_conv_b[l], filt_params, hy_bias[l], hy_ctx)
            ycc = _window_attention(qcx, kcx, vcx, kcx, vcx, wa_sink[l], wbound, False)
            ycd = _conformer(zdx, cf_dw_w[l], cf_dw_b[l], cf_ln_g[l], cf_ln_b[l])
            xc = _merge(xc, mod_c, (yca, *ycb, ycc, ycd), zgx, b_gate[l], wbs, wo)
            xc = _moe(xc.reshape(1, b * cl, d), mod_c[:1], norm2_g[l], w_router, b_router, *ew).reshape(b, cl, d)
        x = _moe_sorted(x, mod_x, rows, gi, *ew)
    return x
```

```python
import functools
import math

import jax
import jax.numpy as jnp
from jax import lax
from jax.experimental import pallas as pl
from jax.experimental.pallas import tpu as pltpu
from jax.experimental.pallas import tpu_sc as plsc

F32 = jnp.float32
BF16 = jnp.bfloat16
HI = lax.Precision.HIGHEST

GRID_W = 64
BLOCK = 128
ROPE_BASE = 10000.0
EPS = 1e-6
NEG_INF = -1e30

DA_HEADS = 4
DA_DIM = 32
DA_VDIM = 64
HY_WIDTH = 256
HY_BANDS = 16
HY_FF = 64
HY_SHIFT = 0.05
HY_FAST_DECAY = 0.3
HY_SLOW_DECAY = 1.5
HY_TARGET = 1e-2
WA_HEADS = 4
WA_KV_HEADS = 2
WA_GROUP = 2
WA_DIM = 64
CF_WIDTH = 256
CF_TAPS = 31
N_BRANCH = 4
BRANCH_W = 256
N_GROUPS = 4
EXP_PER_GROUP = 4
N_EXPERTS = 16

W_A = 4 * DA_HEADS * DA_DIM + DA_HEADS * DA_VDIM
W_B = 3 * HY_WIDTH
W_C = (WA_HEADS + 2 * WA_KV_HEADS) * WA_DIM
W_D = 2 * CF_WIDTH
OFF_B = W_A
OFF_C = OFF_B + W_B
OFF_D = OFF_C + W_C
OFF_G = OFF_D + W_D

LOG2E = math.log2(math.e)
LANES = 128
SUBLANES = 8
VMEM_LIMIT = 56 * 1024 * 1024

DA_ONES = 16
DA_SHIFT_MAX = 50.0


def _cp(*sem):
    return pltpu.CompilerParams(dimension_semantics=sem, vmem_limit_bytes=VMEM_LIMIT)


def _rms(xf):
    return xf * lax.rsqrt(jnp.mean(xf * xf, axis=-1, keepdims=True) + EPS)


def _silu(x):
    return x * jax.nn.sigmoid(x)


def _mod_kernel(c_ref, w_ref, b_ref, o_ref):
    s = _silu(c_ref[...])
    o_ref[0] = jnp.dot(s, w_ref[0], precision=HI, preferred_element_type=F32) + b_ref[0]


def _mod_vectors(crows, w_mod, b_mod):
    depth, d, n = w_mod.shape
    r = crows.shape[0]
    tn = 1536
    return pl.pallas_call(
        _mod_kernel,
        grid=(depth, n // tn),
        in_specs=[pl.BlockSpec((r, d), lambda l, j: (0, 0)),
                  pl.BlockSpec((1, d, tn), lambda l, j: (l, 0, j)),
                  pl.BlockSpec((1, 1, tn), lambda l, j: (l, 0, j))],
        out_specs=pl.BlockSpec((1, r, tn), lambda l, j: (l, 0, j)),
        out_shape=jax.ShapeDtypeStruct((depth, r, n), F32),
        compiler_params=_cp("arbitrary", "arbitrary"),
        name="mod_vectors",
    )(crows, w_mod, b_mod.reshape(depth, 1, n))


def _rope_tables(s, d, reps):
    rows = s // GRID_W
    row = jnp.repeat(jnp.arange(rows, dtype=F32), GRID_W)
    col = jnp.tile(jnp.arange(GRID_W, dtype=F32), rows)
    qd = d // 4
    inv = ROPE_BASE ** (-jnp.arange(qd, dtype=F32) / qd)
    ar = row[:, None] * inv[None, :]
    ac = col[:, None] * inv[None, :]
    z = jnp.zeros_like(ar)
    cos = jnp.concatenate([jnp.cos(ar), jnp.cos(ar), jnp.cos(ac), jnp.cos(ac)], axis=-1)
    sin_up = jnp.concatenate([-jnp.sin(ar), z, -jnp.sin(ac), z], axis=-1)
    sin_dn = jnp.concatenate([z, jnp.sin(ar), z, jnp.sin(ac)], axis=-1)
    t = lambda a: jnp.tile(a, (1, reps))
    return t(cos), t(sin_up), t(sin_dn)


def _unit_tables(s, w):
    return jnp.ones((s, w), F32), jnp.zeros((s, w), F32), jnp.zeros((s, w), F32)


def _group_ones(width, group):
    i = jnp.arange(width) // group
    return (i[:, None] == i[None, :]).astype(BF16)


def _hi_lo(a):
    hi = a.astype(BF16)
    return hi, (a - hi.astype(F32)).astype(BF16)


def _slot_fill(lo, hi, value):
    j = jnp.arange(LANES)
    return jnp.where((j >= lo) & (j < hi), value, 0.0).astype(F32).reshape(1, LANES)


def _store_slots(o_ref, y, group, fill):
    per = LANES // group
    lane = lax.broadcasted_iota(jnp.int32, (y.shape[0], LANES), 1)
    for h in range(y.shape[1] // group):
        blk = y[:, h // per * LANES:(h // per + 1) * LANES]
        if h % per:
            blk = pltpu.roll(blk, LANES - h % per * group, 1)
        o_ref[0, :, h * LANES:(h + 1) * LANES] = jnp.where(lane < group, blk, fill).astype(BF16)


def _norm_rope(x, gmat, gain, cos, sup, sdn, group, qd):
    w = x.shape[-1]
    sh, sl = _hi_lo(x * x)
    ss = (jnp.dot(sh, gmat, preferred_element_type=F32) + jnp.dot(sl, gmat, preferred_element_type=F32)) * (1.0 / group)
    xn = x * lax.rsqrt(ss + EPS) * gain
    return xn * cos + pltpu.roll(xn, w - qd, 1) * sup + pltpu.roll(xn, qd, 1) * sdn


def _inproj_kernel(x_ref, m_ref, g_ref, w_ref,
                   ca, ua, da, cq, uq, dq, ck, uk, dk, gma, gmq, gmk, qna, kna, qnc, knc,
                   qvec, kvec, vvec,
                   q1o, q2o, k1o, k2o, vo, zbo, qco, kco, vco, zdo, zgo):
    d = x_ref.shape[-1]
    x = x_ref[0]
    shift = m_ref[0, :, 0:d]
    scale = m_ref[0, :, d:2 * d]
    h = (_rms(x) * g_ref[...] * (1.0 + scale) + shift).astype(BF16)

    za = jnp.dot(h, w_ref[0, :, 0:OFF_B], preferred_element_type=F32)
    hw = DA_HEADS * DA_DIM
    cos, sup, sdn, gm = ca[...], ua[...], da[...], gma[...]
    qscale = DA_DIM ** -0.5 * LOG2E
    for t, (o, gain, sc, vec) in enumerate(((q1o, qna, qscale, qvec), (q2o, qna, qscale, qvec),
                                            (k1o, kna, 1.0, kvec), (k2o, kna, 1.0, kvec))):
        y = _norm_rope(za[:, t * hw:(t + 1) * hw], gm, gain[...], cos, sup, sdn, DA_DIM, DA_DIM // 4) * sc
        _store_slots(o, y, DA_DIM, vec[...])
    _store_slots(vo, za[:, 4 * hw:], DA_VDIM, vvec[...])

    zbo[0] = jnp.dot(h, w_ref[0, :, OFF_B:OFF_C], preferred_element_type=F32).astype(BF16)

    zc = jnp.dot(h, w_ref[0, :, OFF_C:OFF_D], preferred_element_type=F32)
    qw = WA_HEADS * WA_DIM
    kw = WA_KV_HEADS * WA_DIM
    y = _norm_rope(zc[:, 0:qw], gmq[...], qnc[...], cq[...], uq[...], dq[...], WA_DIM, WA_DIM // 4) * (WA_DIM ** -0.5 * LOG2E)
    _store_slots(qco, y, WA_DIM, 0.0)
    y = _norm_rope(zc[:, qw:qw + kw], gmk[...], knc[...], ck[...], uk[...], dk[...], WA_DIM, WA_DIM // 4)
    _store_slots(kco, y, WA_DIM, 0.0)
    _store_slots(vco, zc[:, qw + kw:], WA_DIM, vvec[...])

    zdo[0] = jnp.dot(h, w_ref[0, :, OFF_D:OFF_G], preferred_element_type=F32).astype(BF16)
    for k in range(N_BRANCH):
        zgo[0, :, k * d:(k + 1) * d] = jnp.dot(h, w_ref[0, :, OFF_G + k * d:OFF_G + (k + 1) * d],
                                               preferred_element_type=F32).astype(BF16)


def _in_proj(x, mod, g, w_all, layer, tables, consts):
    bx, lx, d = x.shape
    tm = min(512, lx)
    const = lambda b, i: (0, 0)
    row = lambda b, i: (b, i, 0)
    once = lambda a: pl.BlockSpec(a.shape, const, pipeline_mode=pl.Buffered(1))
    widths = [DA_HEADS * LANES] * 5 + [W_B, WA_HEADS * LANES, WA_KV_HEADS * LANES, WA_KV_HEADS * LANES, W_D, N_BRANCH * d]
    return pl.pallas_call(
        _inproj_kernel,
        grid=(bx, lx // tm),
        in_specs=[pl.BlockSpec((1, tm, d), row), pl.BlockSpec((1, 1, mod.shape[-1]), lambda b, i: (b, 0, 0)),
                  pl.BlockSpec((1, d), const)]
                 + [pl.BlockSpec((1,) + w_all.shape[1:], lambda b, i: (layer, 0, 0), pipeline_mode=pl.Buffered(1))]
                 + [pl.BlockSpec((tm, t.shape[1]), lambda b, i: (i, 0)) for t in tables]
                 + [once(c) for c in consts],
        out_specs=[pl.BlockSpec((1, tm, w), row) for w in widths],
        out_shape=[jax.ShapeDtypeStruct((bx, lx, w), BF16) for w in widths],
        compiler_params=_cp("parallel", "parallel"),
        name="in_proj",
    )(x, mod, g.reshape(1, d), w_all, *tables, *consts)


def _da_lambda(lam_ref, lam_init):
    lv = lam_ref[...]
    return (jnp.exp(jnp.sum(lv[0:1] * lv[1:2], keepdims=True)) - jnp.exp(jnp.sum(lv[2:3] * lv[3:4], keepdims=True))
            + lam_init)


def _dattn_kernel(lam_ref, sg_ref, q1_ref, q2_ref, *rest, lam_init, online, nsrc, tk_max):
    srcs = [rest[3 * s:3 * s + 3] for s in range(nsrc)]
    o_ref, acc1, acc2 = rest[3 * nsrc:]
    dn = (((1,), (1,)), ((), ()))
    q1 = q1_ref[0]
    q2 = q2_ref[0]
    tq = q1.shape[0]
    acc1[...] = jnp.zeros_like(acc1)
    acc2[...] = jnp.zeros_like(acc2)
    carry = (jnp.full((tq, 1), NEG_INF, F32),) * 2 if online else 0

    for k1_ref, k2_ref, v_ref in srcs:
        tk = min(tk_max, k1_ref.shape[1])

        def body(j, c, k1_ref=k1_ref, k2_ref=k2_ref, v_ref=v_ref, tk=tk):
            rows = pl.ds(pl.multiple_of(j * tk, tk), tk)
            vc = v_ref[0, rows, :]
            s1 = lax.dot_general(q1, k1_ref[0, rows, :], dn, preferred_element_type=F32)
            s2 = lax.dot_general(q2, k2_ref[0, rows, :], dn, preferred_element_type=F32)
            if online:
                m1, m2 = c
                n1 = jnp.maximum(m1, jnp.max(s1, axis=-1, keepdims=True))
                n2 = jnp.maximum(m2, jnp.max(s2, axis=-1, keepdims=True))
                acc1[...] = jnp.exp2(m1 - n1) * acc1[...] + jnp.dot(jnp.exp2(s1 - n1).astype(BF16), vc,
                                                                    preferred_element_type=F32)
                acc2[...] = jnp.exp2(m2 - n2) * acc2[...] + jnp.dot(jnp.exp2(s2 - n2).astype(BF16), vc,
                                                                    preferred_element_type=F32)
                return n1, n2
            acc1[...] += jnp.dot(jnp.exp2(s1).astype(BF16), vc, preferred_element_type=F32)
            acc2[...] += jnp.dot(jnp.exp2(s2).astype(BF16), vc, preferred_element_type=F32)
            return c

        carry = lax.fori_loop(0, k1_ref.shape[1] // tk, body, carry)

    dv = DA_VDIM
    a1 = acc1[...]
    a2 = acc2[...]
    lam = _da_lambda(lam_ref, lam_init)
    o = a1 * (1.0 / a1[:, dv:dv + 1]) - a2 * (lam / a2[:, dv:dv + 1])
    o = jnp.where(lax.broadcasted_iota(jnp.int32, o.shape, 1) < dv, o, 0.0)
    o = o * lax.rsqrt(jnp.sum(o * o, axis=-1, keepdims=True) * (1.0 / dv) + EPS)
    o_ref[0] = (o * (sg_ref[...] * (1.0 - lam_init))).astype(BF16)


def _diff_attention(q1, q2, srcs, fixed, lam_p, subln, lam_init):
    b, lq, _ = q1.shape
    h = DA_HEADS
    const = lambda b_, h_, i: (0, 0)
    sg = jnp.pad(subln, (0, LANES - DA_VDIM)).reshape(1, LANES)
    flat = [a for src in srcs for a in src]

    def call(online, *args):
        tq = min(256 if online else 2048, lq)
        qs = pl.BlockSpec((1, tq, LANES), lambda b_, h_, i: (b_, i, h_))
        return pl.pallas_call(
            functools.partial(_dattn_kernel, lam_init=lam_init, online=online, nsrc=len(srcs),
                              tk_max=256 if online else 512),
            grid=(b, h, lq // tq),
            in_specs=[pl.BlockSpec(lam_p.shape, const), pl.BlockSpec((1, LANES), const), qs, qs]
                     + [pl.BlockSpec((1, a.shape[1], LANES), lambda b_, h_, i: (b_, 0, h_)) for a in flat],
            out_specs=qs,
            out_shape=jax.ShapeDtypeStruct((b, lq, h * LANES), BF16),
            scratch_shapes=[pltpu.VMEM((tq, LANES), F32)] * 2,
            compiler_params=_cp("parallel", "parallel", "arbitrary"),
            name="diff_attention_online" if online else "diff_attention",
        )(*args)

    return lax.cond(fixed, functools.partial(call, False), functools.partial(call, True),
                    lam_p, sg, q1, q2, *flat)


def _hy_prep_kernel(zp_ref, zc_ref, zn_ref, w_ref, b_ref, p_ref, x0_ref):
    i = pl.program_id(1)
    last = pl.num_programs(1) - 1
    tl = zc_ref.shape[1]
    h = 2 * SUBLANES
    ext = jnp.concatenate([jnp.where(i == 0, 0.0, zp_ref[0].astype(F32)), zc_ref[0].astype(F32),
                           jnp.where(i == last, 0.0, zn_ref[0].astype(F32))], axis=0)
    n = ext.shape[0]
    w = w_ref[...]
    u = (pltpu.roll(ext, 1, 0) * w[0:1] + ext * w[1:2] + pltpu.roll(ext, n - 1, 0) * w[2:3] + b_ref[...])[h:h + tl]
    hw = HY_WIDTH
    x0_ref[0] = u[:, 0:hw]
    p_ref[0] = u[:, 2 * hw:3 * hw] * u[:, hw:2 * hw]


def _hy_prep(zb, conv_w, conv_b):
    bx, lx, w = zb.shape
    tl = min(1024, lx)
    h = 2 * SUBLANES
    nh = lx // h
    per = tl // h
    out = jax.ShapeDtypeStruct((bx, lx, HY_WIDTH), F32)
    return pl.pallas_call(
        _hy_prep_kernel,
        grid=(bx, lx // tl),
        in_specs=[pl.BlockSpec((1, h, w), lambda b, i: (b, jnp.maximum(i * per - 1, 0), 0)),
                  pl.BlockSpec((1, tl, w), lambda b, i: (b, i, 0)),
                  pl.BlockSpec((1, h, w), lambda b, i: (b, jnp.minimum((i + 1) * per, nh - 1), 0)),
                  pl.BlockSpec(conv_w.shape, lambda b, i: (0, 0)),
                  pl.BlockSpec((1, w), lambda b, i: (0, 0))],
        out_specs=[pl.BlockSpec((1, tl, HY_WIDTH), lambda b, i: (b, i, 0))] * 2,
        out_shape=[out, out],
        compiler_params=_cp("parallel", "parallel"),
        name="hyena_prep",
    )(zb, zb, zb, conv_w, conv_b.reshape(1, w))


def _hy_filter_kernel(feat_ref, w1, b1, w2, b2, w3, b3, w4, fr_ref, dl_ref, filt_ref, ssq_ref, *, s):
    i = pl.program_id(0)
    tr = feat_ref.shape[0]
    feat = feat_ref[...]
    fr = fr_ref[...]
    dot = lambda a, w: _dot3(*_hi_lo(a), w[...])
    a = jnp.sin(fr * (dot(feat, w1) + b1[...]))
    a = jnp.sin(fr * (dot(a, w2) + b2[...]))
    a = jnp.sin(fr * (dot(a, w3) + b3[...]))
    coef = dot(a, w4)
    n = i * tr + lax.broadcasted_iota(jnp.int32, (tr, 1), 0)
    window = jnp.exp(-feat[:, 0:1] * dl_ref[...]) + HY_SHIFT
    half = jnp.where(n < s, coef[:, :HY_WIDTH], coef[:, HY_WIDTH:])
    filt = jnp.where(n == s, 0.0, half * window)
    filt_ref[...] = filt

    @pl.when(i == 0)
    def _():
        ssq_ref[...] = jnp.zeros_like(ssq_ref)

    ssq_ref[...] += jnp.sum(filt * filt, axis=0, keepdims=True)


def _hy_feat(s):
    t = jnp.linspace(0.0, 1.0, s, dtype=F32)[:, None]
    w = (2.0 * math.pi / s) * jnp.arange(s, dtype=F32)[:, None]
    bands = jnp.linspace(1e-4, HY_BANDS - 1, HY_BANDS, dtype=F32)[None, :]
    feat = jnp.concatenate([t, jnp.cos(w * bands), jnp.sin(w * bands)], axis=-1)
    feat = jnp.concatenate([feat, feat[:1], feat[:0:-1]], axis=0)
    return jnp.pad(feat, ((0, 0), (0, LANES - feat.shape[1])))


def _hy_filter(s, feat, w1, b1, w2, b2, w3, b3, w4, freq):
    n = 2 * s
    tr = min(512, n)
    deltas = jnp.abs(jnp.linspace(math.log(HY_TARGET) / HY_FAST_DECAY, math.log(HY_TARGET) / HY_SLOW_DECAY,
                                  HY_WIDTH, dtype=F32)).reshape(1, HY_WIDTH)
    w1p = jnp.pad(w1, ((0, LANES - w1.shape[0]), (0, 0)))
    row = lambda a: a.reshape(1, -1)
    args = (feat, w1p, row(b1), w2, row(b2), w3, row(b3), w4, row(freq), deltas)
    const = lambda i: (0, 0)
    return pl.pallas_call(
        functools.partial(_hy_filter_kernel, s=s),
        grid=(n // tr,),
        in_specs=[pl.BlockSpec((tr, LANES), lambda i: (i, 0))] + [pl.BlockSpec(a.shape, const) for a in args[1:]],
        out_specs=[pl.BlockSpec((tr, HY_WIDTH), lambda i: (i, 0)), pl.BlockSpec((1, HY_WIDTH), const)],
        out_shape=[jax.ShapeDtypeStruct((n, HY_WIDTH), F32), jax.ShapeDtypeStruct((1, HY_WIDTH), F32)],
        compiler_params=_cp("arbitrary"),
        name="hyena_filter",
    )(*args)


def _dft_factors(n):
    lg = n.bit_length() - 1
    assert 1 << lg == n
    n1 = 1 << (lg // 2)
    return n1, n // n1


def _dft_tables(n):
    n1, n2 = _dft_factors(n)
    ia = jnp.arange(n1, dtype=jnp.int32)
    ang1 = (2.0 * math.pi / n1) * ((ia[:, None] * ia[None, :]) % n1).astype(F32)
    f1 = jnp.concatenate([jnp.cos(ang1), -jnp.sin(ang1)], axis=0)
    c = jnp.arange(n1, dtype=jnp.int32)[:, None, None]
    d = jnp.arange(n2, dtype=jnp.int32)[None, :, None]
    b = jnp.arange(n2, dtype=jnp.int32)[None, None, :]
    ang = (2.0 * math.pi / n) * ((b * (c + n1 * d)) % n).astype(F32)
    re, im = jnp.cos(ang), -jnp.sin(ang)
    m1 = jnp.concatenate([jnp.concatenate([re, -im], axis=2), jnp.concatenate([im, re], axis=2)], axis=1)
    m2 = jnp.swapaxes(m1, 1, 2)
    ang4 = ang1[: n1 // 2]
    f4 = jnp.concatenate([jnp.cos(ang4), -jnp.sin(ang4)], axis=1) * (1.0 / n)
    return f1, _hi_lo_rows(m1), _hi_lo_rows(m2), _hi_lo_rows(_per_offset(f4))


def _hi_lo_rows(m):
    return jnp.concatenate(_hi_lo(m), axis=-2)


def _dot3s(ms, a):
    m = ms.shape[0] // 2
    ah, al = _hi_lo(a)
    t = jnp.dot(ms, ah, preferred_element_type=F32)
    return t[:m] + t[m:] + jnp.dot(ms[:m], al, preferred_element_type=F32)


def _dot3(mh, ml, a):
    ah, al = _hi_lo(a)
    return (jnp.dot(mh, ah, preferred_element_type=F32) + jnp.dot(mh, al, preferred_element_type=F32)
            + jnp.dot(ml, ah, preferred_element_type=F32))


HY_TB = SUBLANES


def _per_offset(f):
    return jnp.kron(f, jnp.eye(HY_TB, dtype=f.dtype))


def _hy_stage1_kernel(f_ref, x_ref, re_ref, im_ref):
    _, n1, tb, w = re_ref.shape
    y = _dot3s(f_ref[...], x_ref[0].reshape(-1, w))
    re_ref[0] = y[:n1 * tb].reshape(n1, tb, w)
    im_ref[0] = y[n1 * tb:].reshape(n1, tb, w)


def _hy_stage1(f, x4):
    bx, k, n2, w = x4.shape
    tb = HY_TB
    n1 = f.shape[0] // (4 * tb)
    out = jax.ShapeDtypeStruct((bx, n1, n2, w), F32)
    return pl.pallas_call(
        _hy_stage1_kernel,
        grid=(bx, n2 // tb),
        in_specs=[pl.BlockSpec(f.shape, lambda b, j: (0, 0))]
                 + [pl.BlockSpec((1, k, tb, w), lambda b, j: (b, 0, j, 0))],
        out_specs=[pl.BlockSpec((1, n1, tb, w), lambda b, j: (b, 0, j, 0))] * 2,
        out_shape=[out, out],
        compiler_params=_cp("parallel", "parallel"),
        name="hyena_dft_stage1",
    )(f, x4)


def _hy_mid_kernel(m1_ref, m2_ref, re_ref, im_ref, hre_ref, him_ref, ore_ref, oim_ref):
    n2 = re_ref.shape[2]
    for c in range(re_ref.shape[1]):
        a = jnp.concatenate([re_ref[0, c], im_ref[0, c]], axis=0)
        x = _dot3s(m1_ref[c], a)
        xre, xim = x[:n2], x[n2:]
        hre, him = hre_ref[c], him_ref[c]
        y = jnp.concatenate([xre * hre - xim * him, xre * him + xim * hre], axis=0)
        bb = _dot3s(m2_ref[c], y)
        ore_ref[0, c] = bb[:n2]
        oim_ref[0, c] = bb[n2:]


def _hy_spec_kernel(m1_ref, re_ref, im_ref, rs_ref, ore_ref, oim_ref):
    n2 = re_ref.shape[2]
    for c in range(re_ref.shape[1]):
        a = jnp.concatenate([re_ref[0, c], im_ref[0, c]], axis=0)
        x = _dot3s(m1_ref[c], a) * rs_ref[...]
        ore_ref[c] = x[:n2]
        oim_ref[c] = x[n2:]


def _hy_filter_spectrum(m1, are, aim, rs):
    _, n1, n2, w = are.shape
    cb = min(4, n1)
    blk = pl.BlockSpec((1, cb, n2, w), lambda c: (0, c, 0, 0))
    mblk = pl.BlockSpec((cb, 4 * n2, 2 * n2), lambda c: (c, 0, 0))
    oblk = pl.BlockSpec((cb, n2, w), lambda c: (c, 0, 0))
    out = jax.ShapeDtypeStruct((n1, n2, w), F32)
    return pl.pallas_call(
        _hy_spec_kernel,
        grid=(n1 // cb,),
        in_specs=[mblk, blk, blk, pl.BlockSpec((1, w), lambda c: (0, 0))],
        out_specs=[oblk, oblk],
        out_shape=[out, out],
        compiler_params=_cp("parallel"),
        name="hyena_filter_spectrum",
    )(m1, are, aim, rs)


def _hy_mid(m1, m2, are, aim, hre, him):
    bx, n1, n2, w = are.shape
    cb = min(16, n1)
    blk = pl.BlockSpec((1, cb, n2, w), lambda c, b: (b, c, 0, 0))
    mblk = pl.BlockSpec((cb, 4 * n2, 2 * n2), lambda c, b: (c, 0, 0))
    hblk = pl.BlockSpec((cb, n2, w), lambda c, b: (c, 0, 0))
    out = jax.ShapeDtypeStruct((bx, n1, n2, w), F32)
    return pl.pallas_call(
        _hy_mid_kernel,
        grid=(n1 // cb, bx),
        in_specs=[mblk, mblk, blk, blk, hblk, hblk],
        out_specs=[blk, blk],
        out_shape=[out, out],
        compiler_params=_cp("parallel", "parallel"),
        name="hyena_dft_mid",
    )(m1, m2, are, aim, hre, him)


def _hy_last_kernel(f_ref, re_ref, im_ref, p_ref, bias_ref, o_ref):
    _, k, tb, w = p_ref.shape
    spec = jnp.concatenate([re_ref[0].reshape(-1, w), im_ref[0].reshape(-1, w)], axis=0)
    y = _dot3s(f_ref[...], spec)
    o_ref[0] = y.reshape(k, tb, w) + p_ref[0] * bias_ref[...]


def _hy_last(f4, bre, bim, p4, bias):
    bx, n1, n2, w = bre.shape
    k = n1 // 2
    tb = HY_TB
    big = pl.BlockSpec((1, n1, tb, w), lambda b, j: (b, 0, j, 0))
    small = pl.BlockSpec((1, k, tb, w), lambda b, j: (b, 0, j, 0))
    return pl.pallas_call(
        _hy_last_kernel,
        grid=(bx, n2 // tb),
        in_specs=[pl.BlockSpec(f4.shape, lambda b, j: (0, 0)), big, big, small,
                  pl.BlockSpec((1, w), lambda b, j: (0, 0))],
        out_specs=small,
        out_shape=jax.ShapeDtypeStruct((bx, k, n2, w), F32),
        compiler_params=_cp("parallel", "parallel"),
        name="hyena_dft_last",
    )(f4, bre, bim, p4, bias)


def _hyena(zb, conv_w, conv_b, filt_params, hy_bias, consts):
    bx, lx, _ = zb.shape
    feat, (f1, m1, m2, f4) = consts
    n = 2 * lx
    n1, n2 = _dft_factors(n)
    w = HY_WIDTH
    filt, ssq = _hy_filter(lx, feat, *filt_params)
    rs = lax.rsqrt(ssq + EPS)
    fre, fim = _hy_stage1(_hi_lo_rows(_per_offset(f1)), filt.reshape(1, n1, n2, w))
    hre, him = _hy_filter_spectrum(m1, fre, fim, rs)
    p, x0 = _hy_prep(zb, conv_w, conv_b)
    k = n1 // 2
    p4 = p.reshape(bx, k, n2, w)
    are, aim = _hy_stage1(_hi_lo_rows(_per_offset(f1[:, :k])), p4)
    bre, bim = _hy_mid(m1, m2, are, aim, hre, him)
    t = _hy_last(f4, bre, bim, p4, hy_bias.reshape(1, w))
    return x0, t.reshape(bx, lx, w)


WA_SHIFT_MAX = 35.0


def _wattn_kernel(sk_ref, q_ref, bias_ref, *rest, banded, fixed):
    if banded:
        kp_ref, kc_ref, kn_ref, kx_ref, vp_ref, vc_ref, vn_ref, vx_ref, o_ref = rest
    else:
        kx_ref, vx_ref, o_ref = rest
    i = pl.program_id(1)
    last = pl.num_programs(1) - 1
    qb = q_ref.shape[1]
    top = lax.broadcasted_iota(jnp.int32, (2 * qb, 1), 0) < qb
    if banded:
        c = lax.broadcasted_iota(jnp.int32, (1, bias_ref.shape[2]), 1) - BLOCK
        outside = ((c < 0) & (i == 0)) | ((c >= qb) & (c < qb + BLOCK) & (i == last))
        edge = jnp.where(outside, NEG_INF, 0.0)
    for h in range(WA_KV_HEADS):
        kl = slice(h * LANES, (h + 1) * LANES)
        ql = h * WA_GROUP * LANES
        q = jnp.concatenate([q_ref[0, :, ql:ql + LANES], q_ref[0, :, ql + LANES:ql + 2 * LANES]], axis=0)
        if banded:
            kk = jnp.concatenate([kp_ref[0, :, kl], kc_ref[0, :, kl], kn_ref[0, :, kl], kx_ref[0, :, kl]], axis=0)
            vv = jnp.concatenate([vp_ref[0, :, kl], vc_ref[0, :, kl], vn_ref[0, :, kl], vx_ref[0, :, kl]], axis=0)
        else:
            kk, vv = kx_ref[0, :, kl], vx_ref[0, :, kl]
        s = lax.dot_general(q, kk, (((1,), (1,)), ((), ())), preferred_element_type=F32) + bias_ref[h]
        if banded:
            s = s + edge
        sk = jnp.where(top, sk_ref[h * WA_GROUP], sk_ref[h * WA_GROUP + 1])
        if fixed:
            p, sink_term = jnp.exp2(s), sk
        else:
            m = jnp.maximum(jnp.max(s, axis=-1, keepdims=True), sk)
            p, sink_term = jnp.exp2(s - m), jnp.exp2(sk - m)
        acc = jnp.dot(p.astype(BF16), vv, preferred_element_type=F32)
        o = (acc * (1.0 / (acc[:, WA_DIM:WA_DIM + 1] + sink_term))).astype(BF16)
        o_ref[0, :, ql:ql + 2 * LANES] = jnp.concatenate([o[:qb], o[qb:]], axis=1)


def _window_attention(q, k, v, kx, vx, sink, bound, banded):
    b, lq, _ = q.shape
    cx = kx.shape[1]
    qb = min(256, lq)
    per = qb // BLOCK
    nblk = lq // BLOCK
    kvw = WA_KV_HEADS * LANES
    side = lambda f: pl.BlockSpec((1, BLOCK, kvw), f)
    prev = side(lambda b_, i: (b_, jnp.maximum(i * per - 1, 0), 0))
    nxt = side(lambda b_, i: (b_, jnp.minimum((i + 1) * per, nblk - 1), 0))
    cur = pl.BlockSpec((1, qb, kvw), lambda b_, i: (b_, i, 0))
    ctx = pl.BlockSpec((1, cx, kvw), lambda b_, i: (b_, 0, 0))
    qspec = pl.BlockSpec((1, qb, WA_HEADS * LANES), lambda b_, i: (b_, i, 0))
    mask = jnp.zeros((WA_GROUP * qb, cx), F32)
    if banded:
        r = jnp.arange(WA_GROUP * qb)[:, None] % qb
        c = jnp.arange(qb + 2 * BLOCK)[None, :] - BLOCK
        mask = jnp.concatenate([jnp.where(jnp.abs(r - c) <= BLOCK, 0.0, NEG_INF).astype(F32), mask], axis=1)
        specs, args = [prev, cur, nxt, ctx, prev, cur, nxt, ctx], (k, k, k, kx, v, v, v, vx)
    else:
        specs, args = [ctx, ctx], (kx, vx)
    fixed = bound <= WA_SHIFT_MAX
    shift = jnp.maximum(bound, sink)
    rows = jnp.repeat(shift.reshape(WA_KV_HEADS, WA_GROUP), qb, axis=1)[:, :, None]
    bias = mask[None] - jnp.where(fixed, LOG2E * rows, 0.0)
    sk = jnp.where(fixed, jnp.exp2(LOG2E * (sink - shift)), LOG2E * sink)

    def call(fixed_, *ops):
        return pl.pallas_call(
            functools.partial(_wattn_kernel, banded=banded, fixed=fixed_),
            grid=(b, lq // qb),
            in_specs=[pl.BlockSpec(memory_space=pltpu.SMEM), qspec,
                      pl.BlockSpec(bias.shape, lambda b_, i: (0, 0, 0))] + specs,
            out_specs=qspec,
            out_shape=jax.ShapeDtypeStruct((b, lq, WA_HEADS * LANES), BF16),
            compiler_params=_cp("parallel", "arbitrary"),
            name="window_attention" if fixed_ else "window_attention_online",
        )(*ops)

    return lax.cond(fixed, functools.partial(call, True), functools.partial(call, False), sk, q, bias, *args)


def _conf_kernel(zp_ref, zc_ref, zn_ref, w_ref, b_ref, lg_ref, lb_ref, o_ref, ext, sh, *, halo):
    i = pl.program_id(1)
    last = pl.num_programs(1) - 1
    tl = zc_ref.shape[1]
    cw = CF_WIDTH

    def glu(z_ref):
        z = z_ref[0].astype(F32)
        return z[:, :cw] * jax.nn.sigmoid(z[:, cw:])

    ext[0:halo] = jnp.where(i == 0, 0.0, glu(zp_ref))
    ext[halo:halo + tl] = glu(zc_ref)
    ext[halo + tl:] = jnp.where(i == last, 0.0, glu(zn_ref))
    for r in range(1, SUBLANES):
        sh[r - 1] = ext[pl.ds(r, sh.shape[1]), :]
    w = w_ref[...]
    u = jnp.zeros((tl, cw), F32) + b_ref[...]
    for j in range(CF_TAPS):
        off = halo - CF_TAPS // 2 + j
        base, r = off // SUBLANES * SUBLANES, off % SUBLANES
        tap = ext[pl.ds(base, tl), :] if r == 0 else sh[r - 1, pl.ds(base, tl), :]
        u = u + tap * w[j:j + 1]
    uc = u - jnp.mean(u, axis=-1, keepdims=True)
    y = uc * lax.rsqrt(jnp.mean(uc * uc, axis=-1, keepdims=True) + EPS) * lg_ref[...] + lb_ref[...]
    o_ref[0] = _silu(y).astype(BF16)


def _conformer(zd, dw_w, dw_b, ln_g, ln_b):
    bx, lx, w = zd.shape
    tl = min(1024, lx)
    halo = 2 * SUBLANES
    nh = lx // halo
    per = tl // halo
    row = lambda a: a.reshape(1, -1)
    const = lambda b, i: (0, 0)
    return pl.pallas_call(
        functools.partial(_conf_kernel, halo=halo),
        grid=(bx, lx // tl),
        in_specs=[pl.BlockSpec((1, halo, w), lambda b, i: (b, jnp.maximum(i * per - 1, 0), 0)),
                  pl.BlockSpec((1, tl, w), lambda b, i: (b, i, 0)),
                  pl.BlockSpec((1, halo, w), lambda b, i: (b, jnp.minimum((i + 1) * per, nh - 1), 0)),
                  pl.BlockSpec(dw_w.shape, const)] + [pl.BlockSpec((1, CF_WIDTH), const)] * 3,
        out_specs=pl.BlockSpec((1, tl, CF_WIDTH), lambda b, i: (b, i, 0)),
        out_shape=jax.ShapeDtypeStruct((bx, lx, CF_WIDTH), BF16),
        scratch_shapes=[pltpu.VMEM((tl + 2 * halo, CF_WIDTH), F32),
                        pltpu.VMEM((SUBLANES - 1, tl + 2 * halo - SUBLANES, CF_WIDTH), F32)],
        compiler_params=_cp("parallel", "parallel"),
        name="conformer_conv",
    )(zd, zd, zd, dw_w, row(dw_b), row(ln_g), row(ln_b))


def _route_rows(x, m_ref, g_ref, wr_ref, br_ref, rows_ref, gi_ref):
    tm, d = x.shape
    half = d // 2
    lane = lax.broadcasted_iota(jnp.int32, (tm, LANES), 1).astype(F32)
    h = _rms(x) * g_ref[...] * (1.0 + m_ref[0, :, 4 * d:5 * d]) + m_ref[0, :, 3 * d:4 * d]
    bits = lax.bitcast_convert_type(h.astype(BF16).astype(F32), jnp.int32)
    rows_ref[0, :, :half] = bits[:, :half] | lax.shift_right_logical(bits[:, half:], 16)
    lg = _dot3(*_hi_lo(h), wr_ref[...]) + br_ref[...]
    isg = lane < N_GROUPS
    gmax = jnp.max(jnp.where(isg, lg, NEG_INF), axis=-1, keepdims=True)
    gi = jnp.min(jnp.where(isg & (lg == gmax), lane, LANES), axis=-1, keepdims=True)
    gw = 1.0 / jnp.sum(jnp.where(isg, jnp.exp(lg - gmax), 0.0), axis=-1, keepdims=True)
    lo = N_GROUPS + gi * EXP_PER_GROUP
    ise = (lane >= lo) & (lane < lo + EXP_PER_GROUP)
    le = jnp.where(ise, lg, NEG_INF)
    m1 = jnp.max(le, axis=-1, keepdims=True)
    i1 = jnp.min(jnp.where(ise & (le == m1), lane, LANES), axis=-1, keepdims=True)
    ise2 = ise & (lane != i1)
    le2 = jnp.where(ise2, lg, NEG_INF)
    m2 = jnp.max(le2, axis=-1, keepdims=True)
    i2 = jnp.min(jnp.where(ise2 & (le2 == m2), lane, LANES), axis=-1, keepdims=True)
    r = jnp.exp(m2 - m1)
    wa = gw / (1.0 + r)
    gates = jnp.where(lane == i1 - lo, wa, 0.0) + jnp.where(lane == i2 - lo, wa * r, 0.0)
    rows_ref[0, :, half:] = lax.bitcast_convert_type(gates, jnp.int32)
    gi_ref[0] = gi.astype(jnp.int32)


def _merge_kernel(x_ref, m_ref, ya, yb0, ybt, yc, yd, zg_ref, bg_ref, wa, wb, wc, wd, wo_ref, *rest, goff):
    o_ref = rest[-1] if len(rest) == 1 else rest[3]
    d = x_ref.shape[-1]
    acc = jnp.zeros(x_ref.shape[1:], F32)
    ys = (ya[0], (yb0[0] * ybt[0]).astype(BF16), yc[0], yd[0])
    for i, (y, w) in enumerate(zip(ys, (wa, wb, wc, wd))):
        gate = jax.nn.sigmoid(zg_ref[0, :, i * d:(i + 1) * d].astype(F32) + bg_ref[:, i * d:(i + 1) * d])
        acc = acc + gate * jnp.dot(y, w[...], preferred_element_type=F32)
    out = jnp.dot(acc.astype(BF16), wo_ref[...], preferred_element_type=F32)
    xn = x_ref[0] + m_ref[0, :, goff:goff + d] * out
    o_ref[0] = xn
    if len(rest) > 1:
        g_ref, wr_ref, br_ref, _, rows_ref, gi_ref = rest
        _route_rows(xn, m_ref, g_ref, wr_ref, br_ref, rows_ref, gi_ref)


def _merge(x, mod, ys, zg, b_gate, wbs, w_out, router=None):
    bx, lx, d = x.shape
    tm = min(512, lx)
    row = lambda b, i: (b, i, 0)
    const = lambda b, i: (0, 0)
    once = lambda a: pl.BlockSpec(a.shape, const, pipeline_mode=pl.Buffered(1))
    extra = (router[0].reshape(1, d), router[1], router[2]) if router else ()
    rw = d // 2 + LANES
    out_specs = [pl.BlockSpec((1, tm, d), row), pl.BlockSpec((1, tm, rw), row), pl.BlockSpec((1, tm, 1), row)]
    out_shape = [jax.ShapeDtypeStruct(x.shape, F32), jax.ShapeDtypeStruct((bx, lx, rw), jnp.int32),
                 jax.ShapeDtypeStruct((bx, lx, 1), jnp.int32)]
    return pl.pallas_call(
        functools.partial(_merge_kernel, goff=2 * d),
        grid=(bx, lx // tm),
        in_specs=[pl.BlockSpec((1, tm, d), row), pl.BlockSpec((1, 1, mod.shape[-1]), lambda b, i: (b, 0, 0))]
                 + [pl.BlockSpec((1, tm, y.shape[-1]), row) for y in ys]
                 + [pl.BlockSpec((1, tm, N_BRANCH * d), row), pl.BlockSpec((1, N_BRANCH * d), const)]
                 + [once(w) for w in wbs] + [once(w_out)] + [once(a) for a in extra],
        out_specs=out_specs if router else out_specs[0],
        out_shape=out_shape if router else out_shape[0],
        input_output_aliases={0: 0},
        compiler_params=_cp("parallel", "parallel"),
        name="merge",
    )(x, mod, *ys, zg, b_gate.reshape(1, -1), *wbs, w_out, *extra)


def _slot_rows(w, group):
    n = w.shape[0] // group
    return jnp.pad(w.reshape(n, group, -1), ((0, 0), (0, LANES - group), (0, 0))).reshape(n * LANES, -1)


def _moe_kernel(x_ref, m_ref, g_ref, wr_ref, br_ref, w1_ref, w3_ref, w2_ref, o_ref, h_sc, gate_sc, acc_sc):
    e = pl.program_id(2)
    d = x_ref.shape[-1]
    tm = x_ref.shape[1]
    lane = lax.broadcasted_iota(jnp.int32, (tm, LANES), 1).astype(F32)

    @pl.when(e == 0)
    def _():
        h = _rms(x_ref[0]) * g_ref[...] * (1.0 + m_ref[0, :, 4 * d:5 * d]) + m_ref[0, :, 3 * d:4 * d]
        h_sc[...] = h.astype(BF16)
        lg = _dot3(*_hi_lo(h), wr_ref[...]) + br_ref[...]
        isg = lane < N_GROUPS
        gmax = jnp.max(jnp.where(isg, lg, NEG_INF), axis=-1, keepdims=True)
        gi = jnp.min(jnp.where(isg & (lg == gmax), lane, LANES), axis=-1, keepdims=True)
        gw = 1.0 / jnp.sum(jnp.where(isg, jnp.exp(lg - gmax), 0.0), axis=-1, keepdims=True)
        lo = N_GROUPS + gi * EXP_PER_GROUP
        ise = (lane >= lo) & (lane < lo + EXP_PER_GROUP)
        le = jnp.where(ise, lg, NEG_INF)
        m1 = jnp.max(le, axis=-1, keepdims=True)
        i1 = jnp.min(jnp.where(ise & (le == m1), lane, LANES), axis=-1, keepdims=True)
        ise2 = ise & (lane != i1)
        le2 = jnp.where(ise2, lg, NEG_INF)
        m2 = jnp.max(le2, axis=-1, keepdims=True)
        i2 = jnp.min(jnp.where(ise2 & (le2 == m2), lane, LANES), axis=-1, keepdims=True)
        r = jnp.exp(m2 - m1)
        wa = gw / (1.0 + r)
        gate_sc[...] = jnp.where(lane == i1, wa, 0.0) + jnp.where(lane == i2, wa * r, 0.0)
        acc_sc[...] = jnp.zeros_like(acc_sc)

    hb = h_sc[...]
    u = (_silu(jnp.dot(hb, w1_ref[0].astype(BF16), preferred_element_type=F32))
         * jnp.dot(hb, w3_ref[0].astype(BF16), preferred_element_type=F32))
    ge = jnp.sum(jnp.where(lane == (e + N_GROUPS).astype(F32), gate_sc[...], 0.0), axis=-1, keepdims=True)
    acc_sc[...] += ge * jnp.dot(u.astype(BF16), w2_ref[0].astype(BF16), preferred_element_type=F32)

    @pl.when(e == pl.num_programs(2) - 1)
    def _():
        o_ref[0] = x_ref[0] + m_ref[0, :, 5 * d:6 * d] * acc_sc[...]


def _moe(x, mod, g, w_router, b_router, w1, w3, w2):
    bx, lx, d = x.shape
    tm = min(1024, lx)
    ne, _, f = w1.shape
    row = lambda b, i, e: (b, i, 0)
    const = lambda b, i, e: (0, 0)
    return pl.pallas_call(
        _moe_kernel,
        grid=(bx, lx // tm, ne),
        in_specs=[pl.BlockSpec((1, tm, d), row), pl.BlockSpec((1, 1, mod.shape[-1]), lambda b, i, e: (b, 0, 0)),
                  pl.BlockSpec((1, d), const), pl.BlockSpec((d, LANES), const), pl.BlockSpec((1, LANES), const),
                  pl.BlockSpec((1, d, f), lambda b, i, e: (e, 0, 0)), pl.BlockSpec((1, d, f), lambda b, i, e: (e, 0, 0)),
                  pl.BlockSpec((1, f, d), lambda b, i, e: (e, 0, 0))],
        out_specs=pl.BlockSpec((1, tm, d), row),
        out_shape=jax.ShapeDtypeStruct(x.shape, F32),
        scratch_shapes=[pltpu.VMEM((tm, d), BF16), pltpu.VMEM((tm, LANES), F32), pltpu.VMEM((tm, d), F32)],
        input_output_aliases={0: 0},
        compiler_params=_cp("parallel", "parallel", "arbitrary"),
        name="moe",
    )(x, mod, g.reshape(1, d), w_router, b_router, w1, w3, w2)


SC_CORES = 2
SC_SUBCORES = 16
SC_CHUNK = 64
MOE_ROWS = 1024


def _sc_gather(table, idx):
    n = idx.shape[0]
    w = table.shape[1]
    per = n // (SC_CORES * SC_SUBCORES)
    assert per * SC_CORES * SC_SUBCORES == n and per % SC_CHUNK == 0
    mesh = plsc.VectorSubcoreMesh(core_axis_name="c", subcore_axis_name="s")

    @functools.partial(
        pl.kernel, mesh=mesh, out_type=jax.ShapeDtypeStruct((n, w), table.dtype),
        scratch_types=[pltpu.VMEM((SC_CHUNK,), jnp.int32), pltpu.VMEM((SC_CHUNK, w), table.dtype),
                       pltpu.SemaphoreType.DMA],
        name="sc_row_gather")
    def gather(table_hbm, idx_hbm, out_hbm, idx_v, rows_v, sem):
        base = (lax.axis_index("s") * SC_CORES + lax.axis_index("c")) * per

        @pl.loop(0, per // SC_CHUNK)
        def _(j):
            off = pl.multiple_of(base + j * SC_CHUNK, SC_CHUNK)
            pltpu.sync_copy(idx_hbm.at[pl.ds(off, SC_CHUNK)], idx_v)
            pltpu.async_copy(table_hbm.at[idx_v], rows_v, sem).wait()
            pltpu.sync_copy(rows_v, out_hbm.at[pl.ds(off, SC_CHUNK)])

    return gather(table, idx)


def _sc_scatter(rows, idx, n_out):
    n, w = rows.shape
    per = n // (SC_CORES * SC_SUBCORES)
    assert per * SC_CORES * SC_SUBCORES == n and per % SC_CHUNK == 0
    mesh = plsc.VectorSubcoreMesh(core_axis_name="c", subcore_axis_name="s")

    @functools.partial(
        pl.kernel, mesh=mesh, out_type=jax.ShapeDtypeStruct((n_out, w), rows.dtype),
        scratch_types=[pltpu.VMEM((SC_CHUNK,), jnp.int32), pltpu.VMEM((SC_CHUNK, w), rows.dtype),
                       pltpu.SemaphoreType.DMA],
        name="sc_row_scatter")
    def scatter(rows_hbm, idx_hbm, out_hbm, idx_v, rows_v, sem):
        base = (lax.axis_index("s") * SC_CORES + lax.axis_index("c")) * per

        @pl.loop(0, per // SC_CHUNK)
        def _(j):
            off = pl.multiple_of(base + j * SC_CHUNK, SC_CHUNK)
            pltpu.sync_copy(idx_hbm.at[pl.ds(off, SC_CHUNK)], idx_v)
            pltpu.sync_copy(rows_hbm.at[pl.ds(off, SC_CHUNK)], rows_v)
            pltpu.async_copy(rows_v, out_hbm.at[idx_v], sem).wait()

    return scatter(rows, idx)


def _gmoe_kernel(grp_ref, nv_ref, xs_ref, w1_ref, w3_ref, w2_ref, o_ref, h_sc, acc_sc):
    i = pl.program_id(0)
    e = pl.program_id(1)
    tm, d = h_sc.shape
    half = d // 2
    valid = lax.broadcasted_iota(jnp.int32, (tm, 1), 0) < nv_ref[i]

    @pl.when(nv_ref[i] > 0)
    def _():
        @pl.when(e == 0)
        def _():
            pk = xs_ref[:, :half]
            h = jnp.concatenate([lax.bitcast_convert_type(pk & -65536, F32),
                                 lax.bitcast_convert_type(lax.shift_left(pk, 16), F32)], axis=1)
            h_sc[...] = jnp.where(valid, h, 0.0).astype(BF16)
            acc_sc[...] = jnp.zeros_like(acc_sc)

        hb = h_sc[...]
        u = (_silu(jnp.dot(hb, w1_ref[0].astype(BF16), preferred_element_type=F32))
             * jnp.dot(hb, w3_ref[0].astype(BF16), preferred_element_type=F32))
        lane = lax.broadcasted_iota(jnp.int32, (tm, LANES), 1)
        gates = lax.bitcast_convert_type(xs_ref[:, half:], F32)
        ge = jnp.sum(jnp.where(valid & (lane == e), gates, 0.0), axis=-1, keepdims=True)
        acc_sc[...] += ge * jnp.dot(u.astype(BF16), w2_ref[0].astype(BF16), preferred_element_type=F32)

    @pl.when(e == pl.num_programs(1) - 1)
    def _():
        o_ref[...] = jnp.where(nv_ref[i] > 0, acc_sc[...], 0.0)


def _grouped_moe(grp, nv, xs, w1, w3, w2):
    p, dw = xs.shape
    d = 2 * (dw - LANES)
    _, _, f = w1.shape
    wmap = lambda i, e, grp, nv: (grp[i] * EXP_PER_GROUP + e, 0, 0)
    rows = lambda i, e, grp, nv: (i, 0)
    return pl.pallas_call(
        _gmoe_kernel,
        grid_spec=pltpu.PrefetchScalarGridSpec(
            num_scalar_prefetch=2,
            grid=(p // MOE_ROWS, EXP_PER_GROUP),
            in_specs=[pl.BlockSpec((MOE_ROWS, dw), rows),
                      pl.BlockSpec((1, d, f), wmap), pl.BlockSpec((1, d, f), wmap), pl.BlockSpec((1, f, d), wmap)],
            out_specs=pl.BlockSpec((MOE_ROWS, d), rows),
            scratch_shapes=[pltpu.VMEM((MOE_ROWS, d), BF16), pltpu.VMEM((MOE_ROWS, d), F32)]),
        out_shape=jax.ShapeDtypeStruct((p, d), F32),
        compiler_params=_cp("arbitrary", "arbitrary"),
        name="moe_experts",
    )(grp, nv, xs, w1, w3, w2)


def _residual_kernel(x_ref, m_ref, y_ref, o_ref):
    d = x_ref.shape[-1]
    o_ref[0] = x_ref[0] + m_ref[0, :, 5 * d:6 * d] * y_ref[0]


def _residual(x, mod, y):
    bx, lx, d = x.shape
    tm = min(1024, lx)
    row = lambda b, i: (b, i, 0)
    return pl.pallas_call(
        _residual_kernel,
        grid=(bx, lx // tm),
        in_specs=[pl.BlockSpec((1, tm, d), row), pl.BlockSpec((1, 1, mod.shape[-1]), lambda b, i: (b, 0, 0)),
                  pl.BlockSpec((1, tm, d), row)],
        out_specs=pl.BlockSpec((1, tm, d), row),
        out_shape=jax.ShapeDtypeStruct(x.shape, F32),
        input_output_aliases={0: 0},
        compiler_params=_cp("parallel", "parallel"),
        name="moe_residual",
    )(x, mod, y)


def _moe_sorted(x, mod, rows, gi, w1, w3, w2):
    bx, lx, d = x.shape
    t = bx * lx
    gi = gi.reshape(t)
    onehot = (gi[:, None] == jnp.arange(N_GROUPS, dtype=jnp.int32)[None, :]).astype(jnp.int32)
    csum = jnp.cumsum(onehot, axis=0)
    counts = csum[-1]
    rank = jnp.take_along_axis(csum, gi[:, None], axis=1)[:, 0] - 1
    padded = (counts + MOE_ROWS - 1) // MOE_ROWS * MOE_ROWS
    pend = jnp.cumsum(padded)
    pstart = pend - padded
    pos = (pstart[gi] + rank).astype(jnp.int32)
    p = t + N_GROUPS * MOE_ROWS
    bstart = jnp.arange(p // MOE_ROWS, dtype=jnp.int32) * MOE_ROWS
    grp = jnp.minimum(jnp.searchsorted(pend, bstart, side="right"), N_GROUPS - 1).astype(jnp.int32)
    nv = jnp.clip(pstart[grp] + counts[grp] - bstart, 0, MOE_ROWS).astype(jnp.int32)
    xs = _sc_scatter(rows.reshape(t, rows.shape[-1]), pos, p)
    ys = _grouped_moe(grp, nv, xs, w1, w3, w2)
    yt = _sc_gather(ys, pos)
    return _residual(x, mod, yt.reshape(bx, lx, d))


def kernel(x, c, ctx, c_ctx, w_mod, b_mod, norm1_g, norm2_g, w_in, b_gate, da_qn, da_kn, da_lam, da_subln, hy_conv_w, hy_conv_b, hf_w1, hf_b1, hf_w2, hf_b2, hf_w3, hf_b3, hf_w4, hf_freq, hy_bias, wa_qn, wa_kn, wa_sink, cf_dw_w, cf_dw_b, cf_ln_g, cf_ln_b, w_branch, w_out, w_rg, b_rg, w_re, b_re, w1, w3, w2):
    b, s, d = x.shape
    cl = ctx.shape[1]
    depth = w_mod.shape[0]
    assert s % 256 == 0 and cl % 256 == 0 and s % GRID_W == 0

    nrow = -(-(b + 1) // SUBLANES) * SUBLANES
    crows = jnp.zeros((nrow, d), F32).at[:b].set(c).at[b].set(c_ctx)
    mods = _mod_vectors(crows, w_mod, b_mod)

    aw, qw, kw = DA_HEADS * DA_DIM, WA_HEADS * WA_DIM, WA_KV_HEADS * WA_DIM
    tab_lat = (*_rope_tables(s, DA_DIM, DA_HEADS), *_rope_tables(s, WA_DIM, WA_HEADS), *_rope_tables(s, WA_DIM, WA_KV_HEADS))
    tab_ctx = (*_unit_tables(cl, aw), *_unit_tables(cl, qw), *_unit_tables(cl, kw))
    hy_lat = (_hy_feat(s), _dft_tables(2 * s))
    hy_ctx = (_hy_feat(cl), _dft_tables(2 * cl))
    gms = (_group_ones(aw, DA_DIM), _group_ones(qw, WA_DIM), _group_ones(kw, WA_DIM))
    qvec = _slot_fill(DA_DIM, DA_DIM + 1, 1.0)
    vvec = _slot_fill(DA_VDIM, DA_VDIM + DA_ONES, 1.0)

    w_in_b = w_in.astype(BF16)
    xc = ctx
    for l in range(depth):
        last = l == depth - 1
        lam_init = 0.8 - 0.6 * math.exp(-0.3 * l)
        mod_x = mods[l, :b][:, None, :]
        mod_c = jnp.broadcast_to(mods[l, b][None, None, :], (b, 1, mods.shape[-1]))
        shift = 1.02 * LOG2E * DA_DIM ** 0.5 * jnp.max(jnp.abs(da_qn[l])) * jnp.max(jnp.abs(da_kn[l]))
        fixed = shift <= DA_SHIFT_MAX
        kvec = _slot_fill(DA_DIM, DA_DIM + 1, jnp.where(fixed, -shift, 0.0))
        tile = lambda a, n: jnp.tile(a, n).reshape(1, -1)
        consts = (*gms, tile(da_qn[l], DA_HEADS), tile(da_kn[l], DA_HEADS), tile(wa_qn[l], WA_HEADS),
                  tile(wa_kn[l], WA_KV_HEADS), qvec, kvec, vvec)
        q1, q2, k1, k2, v, zb, qc, kc, vc, zd, zg = _in_proj(x, mod_x, norm1_g[l], w_in_b, l, tab_lat, consts)
        q1x, q2x, k1x, k2x, vx, zbx, qcx, kcx, vcx, zdx, zgx = _in_proj(xc, mod_c, norm1_g[l], w_in_b, l, tab_ctx, consts)
        filt_params = (hf_w1[l], hf_b1[l], hf_w2[l], hf_b2[l], hf_w3[l], hf_b3[l], hf_w4[l], hf_freq[l])
        wb4 = w_branch[l].astype(BF16)
        wbs = (_slot_rows(wb4[0], DA_VDIM), wb4[1], _slot_rows(wb4[2], WA_DIM), wb4[3])
        wo = w_out[l].astype(BF16)

        ya = _diff_attention(q1, q2, [(k1, k2, v), (k1x, k2x, vx)], fixed, da_lam[l], da_subln[l], lam_init)
        yb = _hyena(zb, hy_conv_w[l], hy_conv_b[l], filt_params, hy_bias[l], hy_lat)
        wbound = 1.02 * WA_DIM ** 0.5 * jnp.max(jnp.abs(wa_qn[l])) * jnp.max(jnp.abs(wa_kn[l]))
        yc_ = _window_attention(qc, kc, vc, kcx, vcx, wa_sink[l], wbound, True)
        yd = _conformer(zd, cf_dw_w[l], cf_dw_b[l], cf_ln_g[l], cf_ln_b[l])
        w_router = jnp.pad(jnp.concatenate([w_rg[l], w_re[l]], axis=1), ((0, 0), (0, LANES - N_GROUPS - N_EXPERTS)))
        b_router = jnp.pad(jnp.concatenate([b_rg[l], b_re[l]]), (0, LANES - N_GROUPS - N_EXPERTS)).reshape(1, LANES)
        ew = (w1[l], w3[l], w2[l])
        x, rows, gi = _merge(x, mod_x, (ya, *yb, yc_, yd), zg, b_gate[l], wbs, wo, (norm2_g[l], w_router, b_router))

        if not last:
            yca = _diff_attention(q1x, q2x, [(k1x, k2x, vx)], fixed, da_lam[l], da_subln[l], lam_init)
            ycb = _hyena(zbx, hy_conv_w[l], hy_conv_b[l], filt_params, hy_bias[l], hy_ctx)
            ycc = _window_attention(qcx, kcx, vcx, kcx, vcx, wa_sink[l], wbound, False)
            ycd = _conformer(zdx, cf_dw_w[l], cf_dw_b[l], cf_ln_g[l], cf_ln_b[l])
            xc = _merge(xc, mod_c, (yca, *ycb, ycc, ycd), zgx, b_gate[l], wbs, wo)
            xc = _moe(xc.reshape(1, b * cl, d), mod_c[:1], norm2_g[l], w_router, b_router, *ew).reshape(b, cl, d)
        x = _moe_sorted(x, mod_x, rows, gi, *ew)
    return x
```

```python
import functools
import math

import jax
import jax.numpy as jnp
from jax import lax
from jax.experimental import pallas as pl
from jax.experimental.pallas import tpu as pltpu
from jax.experimental.pallas import tpu_sc as plsc

F32 = jnp.float32
BF16 = jnp.bfloat16
HI = lax.Precision.HIGHEST

GRID_W = 64
BLOCK = 128
ROPE_BASE = 10000.0
EPS = 1e-6
NEG_INF = -1e30

DA_HEADS = 4
DA_DIM = 32
DA_VDIM = 64
HY_WIDTH = 256
HY_BANDS = 16
HY_FF = 64
HY_SHIFT = 0.05
HY_FAST_DECAY = 0.3
HY_SLOW_DECAY = 1.5
HY_TARGET = 1e-2
WA_HEADS = 4
WA_KV_HEADS = 2
WA_GROUP = 2
WA_DIM = 64
CF_WIDTH = 256
CF_TAPS = 31
N_BRANCH = 4
BRANCH_W = 256
N_GROUPS = 4
EXP_PER_GROUP = 4
N_EXPERTS = 16

W_A = 4 * DA_HEADS * DA_DIM + DA_HEADS * DA_VDIM
W_B = 3 * HY_WIDTH
W_C = (WA_HEADS + 2 * WA_KV_HEADS) * WA_DIM
W_D = 2 * CF_WIDTH
OFF_B = W_A
OFF_C = OFF_B + W_B
OFF_D = OFF_C + W_C
OFF_G = OFF_D + W_D

LOG2E = math.log2(math.e)
LANES = 128
SUBLANES = 8
VMEM_LIMIT = 56 * 1024 * 1024

DA_ONES = 16
DA_SHIFT_MAX = 50.0


def _cp(*sem):
    return pltpu.CompilerParams(dimension_semantics=sem, vmem_limit_bytes=VMEM_LIMIT)


def _rms(xf):
    return xf * lax.rsqrt(jnp.mean(xf * xf, axis=-1, keepdims=True) + EPS)


def _silu(x):
    return x * jax.nn.sigmoid(x)


def _mod_kernel(c_ref, w_ref, b_ref, o_ref):
    s = _silu(c_ref[...])
    o_ref[0] = jnp.dot(s, w_ref[0], precision=HI, preferred_element_type=F32) + b_ref[0]


def _mod_vectors(crows, w_mod, b_mod):
    depth, d, n = w_mod.shape
    r = crows.shape[0]
    tn = 1536
    return pl.pallas_call(
        _mod_kernel,
        grid=(depth, n // tn),
        in_specs=[pl.BlockSpec((r, d), lambda l, j: (0, 0)),
                  pl.BlockSpec((1, d, tn), lambda l, j: (l, 0, j)),
                  pl.BlockSpec((1, 1, tn), lambda l, j: (l, 0, j))],
        out_specs=pl.BlockSpec((1, r, tn), lambda l, j: (l, 0, j)),
        out_shape=jax.ShapeDtypeStruct((depth, r, n), F32),
        compiler_params=_cp("arbitrary", "arbitrary"),
        name="mod_vectors",
    )(crows, w_mod, b_mod.reshape(depth, 1, n))


def _rope_tables(s, d, reps):
    rows = s // GRID_W
    row = jnp.repeat(jnp.arange(rows, dtype=F32), GRID_W)
    col = jnp.tile(jnp.arange(GRID_W, dtype=F32), rows)
    qd = d // 4
    inv = ROPE_BASE ** (-jnp.arange(qd, dtype=F32) / qd)
    ar = row[:, None] * inv[None, :]
    ac = col[:, None] * inv[None, :]
    z = jnp.zeros_like(ar)
    cos = jnp.concatenate([jnp.cos(ar), jnp.cos(ar), jnp.cos(ac), jnp.cos(ac)], axis=-1)
    sin_up = jnp.concatenate([-jnp.sin(ar), z, -jnp.sin(ac), z], axis=-1)
    sin_dn = jnp.concatenate([z, jnp.sin(ar), z, jnp.sin(ac)], axis=-1)
    t = lambda a: jnp.tile(a, (1, reps))
    return t(cos), t(sin_up), t(sin_dn)


def _unit_tables(s, w):
    return jnp.ones((s, w), F32), jnp.zeros((s, w), F32), jnp.zeros((s, w), F32)


def _group_ones(width, group):
    i = jnp.arange(width) // group
    return (i[:, None] == i[None, :]).astype(BF16)


def _hi_lo(a):
    hi = a.astype(BF16)
    return hi, (a - hi.astype(F32)).astype(BF16)


def _slot_fill(lo, hi, value):
    j = jnp.arange(LANES)
    return jnp.where((j >= lo) & (j < hi), value, 0.0).astype(F32).reshape(1, LANES)


def _store_slots(o_ref, y, group, fill):
    per = LANES // group
    lane = lax.broadcasted_iota(jnp.int32, (y.shape[0], LANES), 1)
    for h in range(y.shape[1] // group):
        blk = y[:, h // per * LANES:(h // per + 1) * LANES]
        if h % per:
            blk = pltpu.roll(blk, LANES - h % per * group, 1)
        o_ref[0, :, h * LANES:(h + 1) * LANES] = jnp.where(lane < group, blk, fill).astype(BF16)


def _norm_rope(x, gmat, gain, cos, sup, sdn, group, qd):
    w = x.shape[-1]
    sh, sl = _hi_lo(x * x)
    ss = (jnp.dot(sh, gmat, preferred_element_type=F32) + jnp.dot(sl, gmat, preferred_element_type=F32)) * (1.0 / group)
    xn = x * lax.rsqrt(ss + EPS) * gain
    return xn * cos + pltpu.roll(xn, w - qd, 1) * sup + pltpu.roll(xn, qd, 1) * sdn


def _inproj_kernel(x_ref, m_ref, g_ref, w_ref,
                   ca, ua, da, cq, uq, dq, ck, uk, dk, gma, gmq, gmk, qna, kna, qnc, knc,
                   qvec, kvec, vvec,
                   q1o, q2o, k1o, k2o, vo, zbo, qco, kco, vco, zdo, zgo):
    d = x_ref.shape[-1]
    x = x_ref[0]
    shift = m_ref[0, :, 0:d]
    scale = m_ref[0, :, d:2 * d]
    h = (_rms(x) * g_ref[...] * (1.0 + scale) + shift).astype(BF16)

    za = jnp.dot(h, w_ref[0, :, 0:OFF_B], preferred_element_type=F32)
    hw = DA_HEADS * DA_DIM
    cos, sup, sdn, gm = ca[...], ua[...], da[...], gma[...]
    qscale = DA_DIM ** -0.5 * LOG2E
    for t, (o, gain, sc, vec) in enumerate(((q1o, qna, qscale, qvec), (q2o, qna, qscale, qvec),
                                            (k1o, kna, 1.0, kvec), (k2o, kna, 1.0, kvec))):
        y = _norm_rope(za[:, t * hw:(t + 1) * hw], gm, gain[...], cos, sup, sdn, DA_DIM, DA_DIM // 4) * sc
        _store_slots(o, y, DA_DIM, vec[...])
    _store_slots(vo, za[:, 4 * hw:], DA_VDIM, vvec[...])

    zbo[0] = jnp.dot(h, w_ref[0, :, OFF_B:OFF_C], preferred_element_type=F32).astype(BF16)

    zc = jnp.dot(h, w_ref[0, :, OFF_C:OFF_D], preferred_element_type=F32)
    qw = WA_HEADS * WA_DIM
    kw = WA_KV_HEADS * WA_DIM
    y = _norm_rope(zc[:, 0:qw], gmq[...], qnc[...], cq[...], uq[...], dq[...], WA_DIM, WA_DIM // 4) * (WA_DIM ** -0.5 * LOG2E)
    _store_slots(qco, y, WA_DIM, 0.0)
    y = _norm_rope(zc[:, qw:qw + kw], gmk[...], knc[...], ck[...], uk[...], dk[...], WA_DIM, WA_DIM // 4)
    _store_slots(kco, y, WA_DIM, 0.0)
    _store_slots(vco, zc[:, qw + kw:], WA_DIM, vvec[...])

    zdo[0] = jnp.dot(h, w_ref[0, :, OFF_D:OFF_G], preferred_element_type=F32).astype(BF16)
    for k in range(N_BRANCH):
        zgo[0, :, k * d:(k + 1) * d] = jnp.dot(h, w_ref[0, :, OFF_G + k * d:OFF_G + (k + 1) * d],
                                               preferred_element_type=F32).astype(BF16)


def _in_proj(x, mod, g, w_all, layer, tables, consts):
    bx, lx, d = x.shape
    tm = min(512, lx)
    const = lambda b, i: (0, 0)
    row = lambda b, i: (b, i, 0)
    once = lambda a: pl.BlockSpec(a.shape, const, pipeline_mode=pl.Buffered(1))
    widths = [DA_HEADS * LANES] * 5 + [W_B, WA_HEADS * LANES, WA_KV_HEADS * LANES, WA_KV_HEADS * LANES, W_D, N_BRANCH * d]
    return pl.pallas_call(
        _inproj_kernel,
        grid=(bx, lx // tm),
        in_specs=[pl.BlockSpec((1, tm, d), row), pl.BlockSpec((1, 1, mod.shape[-1]), lambda b, i: (b, 0, 0)),
                  pl.BlockSpec((1, d), const)]
                 + [pl.BlockSpec((1,) + w_all.shape[1:], lambda b, i: (layer, 0, 0), pipeline_mode=pl.Buffered(1))]
                 + [pl.BlockSpec((tm, t.shape[1]), lambda b, i: (i, 0)) for t in tables]
                 + [once(c) for c in consts],
        out_specs=[pl.BlockSpec((1, tm, w), row) for w in widths],
        out_shape=[jax.ShapeDtypeStruct((bx, lx, w), BF16) for w in widths],
        compiler_params=_cp("parallel", "parallel"),
        name="in_proj",
    )(x, mod, g.reshape(1, d), w_all, *tables, *consts)


def _da_lambda(lam_ref, lam_init):
    lv = lam_ref[...]
    return (jnp.exp(jnp.sum(lv[0:1] * lv[1:2], keepdims=True)) - jnp.exp(jnp.sum(lv[2:3] * lv[3:4], keepdims=True))
            + lam_init)


def _dattn_kernel(lam_ref, sg_ref, q1_ref, q2_ref, *rest, lam_init, online, nsrc, tk_max):
    srcs = [rest[3 * s:3 * s + 3] for s in range(nsrc)]
    o_ref, acc1, acc2 = rest[3 * nsrc:]
    dn = (((1,), (1,)), ((), ()))
    q1 = q1_ref[0]
    q2 = q2_ref[0]
    tq = q1.shape[0]
    acc1[...] = jnp.zeros_like(acc1)
    acc2[...] = jnp.zeros_like(acc2)
    carry = (jnp.full((tq, 1), NEG_INF, F32),) * 2 if online else 0

    for k1_ref, k2_ref, v_ref in srcs:
        tk = min(tk_max, k1_ref.shape[1])

        def body(j, c, k1_ref=k1_ref, k2_ref=k2_ref, v_ref=v_ref, tk=tk):
            rows = pl.ds(pl.multiple_of(j * tk, tk), tk)
            vc = v_ref[0, rows, :]
            s1 = lax.dot_general(q1, k1_ref[0, rows, :], dn, preferred_element_type=F32)
            s2 = lax.dot_general(q2, k2_ref[0, rows, :], dn, preferred_element_type=F32)
            if online:
                m1, m2 = c
                n1 = jnp.maximum(m1, jnp.max(s1, axis=-1, keepdims=True))
                n2 = jnp.maximum(m2, jnp.max(s2, axis=-1, keepdims=True))
                acc1[...] = jnp.exp2(m1 - n1) * acc1[...] + jnp.dot(jnp.exp2(s1 - n1).astype(BF16), vc,
                                                                    preferred_element_type=F32)
                acc2[...] = jnp.exp2(m2 - n2) * acc2[...] + jnp.dot(jnp.exp2(s2 - n2).astype(BF16), vc,
                                                                    preferred_element_type=F32)
                return n1, n2
            acc1[...] += jnp.dot(jnp.exp2(s1).astype(BF16), vc, preferred_element_type=F32)
            acc2[...] += jnp.dot(jnp.exp2(s2).astype(BF16), vc, preferred_element_type=F32)
            return c

        carry = lax.fori_loop(0, k1_ref.shape[1] // tk, body, carry)

    dv = DA_VDIM
    a1 = acc1[...]
    a2 = acc2[...]
    lam = _da_lambda(lam_ref, lam_init)
    o = a1 * (1.0 / a1[:, dv:dv + 1]) - a2 * (lam / a2[:, dv:dv + 1])
    o = jnp.where(lax.broadcasted_iota(jnp.int32, o.shape, 1) < dv, o, 0.0)
    o = o * lax.rsqrt(jnp.sum(o * o, axis=-1, keepdims=True) * (1.0 / dv) + EPS)
    o_ref[0] = (o * (sg_ref[...] * (1.0 - lam_init))).astype(BF16)


def _diff_attention(q1, q2, srcs, fixed, lam_p, subln, lam_init):
    b, lq, _ = q1.shape
    h = DA_HEADS
    const = lambda b_, h_, i: (0, 0)
    sg = jnp.pad(subln, (0, LANES - DA_VDIM)).reshape(1, LANES)
    flat = [a for src in srcs for a in src]

    def call(online, *args):
        tq = min(256 if online else 2048, lq)
        qs = pl.BlockSpec((1, tq, LANES), lambda b_, h_, i: (b_, i, h_))
        return pl.pallas_call(
            functools.partial(_dattn_kernel, lam_init=lam_init, online=online, nsrc=len(srcs),
                              tk_max=256 if online else 512),
            grid=(b, h, lq // tq),
            in_specs=[pl.BlockSpec(lam_p.shape, const), pl.BlockSpec((1, LANES), const), qs, qs]
                     + [pl.BlockSpec((1, a.shape[1], LANES), lambda b_, h_, i: (b_, 0, h_)) for a in flat],
            out_specs=qs,
            out_shape=jax.ShapeDtypeStruct((b, lq, h * LANES), BF16),
            scratch_shapes=[pltpu.VMEM((tq, LANES), F32)] * 2,
            compiler_params=_cp("parallel", "parallel", "arbitrary"),
            name="diff_attention_online" if online else "diff_attention",
        )(*args)

    return lax.cond(fixed, functools.partial(call, False), functools.partial(call, True),
                    lam_p, sg, q1, q2, *flat)


def _hy_prep_kernel(zp_ref, zc_ref, zn_ref, w_ref, b_ref, p_ref, x0_ref):
    i = pl.program_id(1)
    last = pl.num_programs(1) - 1
    tl = zc_ref.shape[1]
    h = 2 * SUBLANES
    ext = jnp.concatenate([jnp.where(i == 0, 0.0, zp_ref[0].astype(F32)), zc_ref[0].astype(F32),
                           jnp.where(i == last, 0.0, zn_ref[0].astype(F32))], axis=0)
    n = ext.shape[0]
    w = w_ref[...]
    u = (pltpu.roll(ext, 1, 0) * w[0:1] + ext * w[1:2] + pltpu.roll(ext, n - 1, 0) * w[2:3] + b_ref[...])[h:h + tl]
    hw = HY_WIDTH
    x0_ref[0] = u[:, 0:hw]
    p_ref[0] = u[:, 2 * hw:3 * hw] * u[:, hw:2 * hw]


def _hy_prep(zb, conv_w, conv_b):
    bx, lx, w = zb.shape
    tl = min(1024, lx)
    h = 2 * SUBLANES
    nh = lx // h
    per = tl // h
    out = jax.ShapeDtypeStruct((bx, lx, HY_WIDTH), F32)
    return pl.pallas_call(
        _hy_prep_kernel,
        grid=(bx, lx // tl),
        in_specs=[pl.BlockSpec((1, h, w), lambda b, i: (b, jnp.maximum(i * per - 1, 0), 0)),
                  pl.BlockSpec((1, tl, w), lambda b, i: (b, i, 0)),
                  pl.BlockSpec((1, h, w), lambda b, i: (b, jnp.minimum((i + 1) * per, nh - 1), 0)),
                  pl.BlockSpec(conv_w.shape, lambda b, i: (0, 0)),
                  pl.BlockSpec((1, w), lambda b, i: (0, 0))],
        out_specs=[pl.BlockSpec((1, tl, HY_WIDTH), lambda b, i: (b, i, 0))] * 2,
        out_shape=[out, out],
        compiler_params=_cp("parallel", "parallel"),
        name="hyena_prep",
    )(zb, zb, zb, conv_w, conv_b.reshape(1, w))


def _hy_filter_kernel(feat_ref, w1, b1, w2, b2, w3, b3, w4, fr_ref, dl_ref, filt_ref, ssq_ref, *, s):
    i = pl.program_id(0)
    tr = feat_ref.shape[0]
    feat = feat_ref[...]
    fr = fr_ref[...]
    dot = lambda a, w: _dot3(*_hi_lo(a), w[...])
    a = jnp.sin(fr * (dot(feat, w1) + b1[...]))
    a = jnp.sin(fr * (dot(a, w2) + b2[...]))
    a = jnp.sin(fr * (dot(a, w3) + b3[...]))
    coef = dot(a, w4)
    n = i * tr + lax.broadcasted_iota(jnp.int32, (tr, 1), 0)
    window = jnp.exp(-feat[:, 0:1] * dl_ref[...]) + HY_SHIFT
    half = jnp.where(n < s, coef[:, :HY_WIDTH], coef[:, HY_WIDTH:])
    filt = jnp.where(n == s, 0.0, half * window)
    filt_ref[...] = filt

    @pl.when(i == 0)
    def _():
        ssq_ref[...] = jnp.zeros_like(ssq_ref)

    ssq_ref[...] += jnp.sum(filt * filt, axis=0, keepdims=True)


def _hy_feat(s):
    t = jnp.linspace(0.0, 1.0, s, dtype=F32)[:, None]
    w = (2.0 * math.pi / s) * jnp.arange(s, dtype=F32)[:, None]
    bands = jnp.linspace(1e-4, HY_BANDS - 1, HY_BANDS, dtype=F32)[None, :]
    feat = jnp.concatenate([t, jnp.cos(w * bands), jnp.sin(w * bands)], axis=-1)
    feat = jnp.concatenate([feat, feat[:1], feat[:0:-1]], axis=0)
    return jnp.pad(feat, ((0, 0), (0, LANES - feat.shape[1])))


def _hy_filter(s, feat, w1, b1, w2, b2, w3, b3, w4, freq):
    n = 2 * s
    tr = min(512, n)
    deltas = jnp.abs(jnp.linspace(math.log(HY_TARGET) / HY_FAST_DECAY, math.log(HY_TARGET) / HY_SLOW_DECAY,
                                  HY_WIDTH, dtype=F32)).reshape(1, HY_WIDTH)
    w1p = jnp.pad(w1, ((0, LANES - w1.shape[0]), (0, 0)))
    row = lambda a: a.reshape(1, -1)
    args = (feat, w1p, row(b1), w2, row(b2), w3, row(b3), w4, row(freq), deltas)
    const = lambda i: (0, 0)
    return pl.pallas_call(
        functools.partial(_hy_filter_kernel, s=s),
        grid=(n // tr,),
        in_specs=[pl.BlockSpec((tr, LANES), lambda i: (i, 0))] + [pl.BlockSpec(a.shape, const) for a in args[1:]],
        out_specs=[pl.BlockSpec((tr, HY_WIDTH), lambda i: (i, 0)), pl.BlockSpec((1, HY_WIDTH), const)],
        out_shape=[jax.ShapeDtypeStruct((n, HY_WIDTH), F32), jax.ShapeDtypeStruct((1, HY_WIDTH), F32)],
        compiler_params=_cp("arbitrary"),
        name="hyena_filter",
    )(*args)


def _dft_factors(n):
    lg = n.bit_length() - 1
    assert 1 << lg == n
    n1 = 1 << (lg // 2)
    return n1, n // n1


def _dft_tables(n):
    n1, n2 = _dft_factors(n)
    ia = jnp.arange(n1, dtype=jnp.int32)
    ang1 = (2.0 * math.pi / n1) * ((ia[:, None] * ia[None, :]) % n1).astype(F32)
    f1 = jnp.concatenate([jnp.cos(ang1), -jnp.sin(ang1)], axis=0)
    c = jnp.arange(n1, dtype=jnp.int32)[:, None, None]
    d = jnp.arange(n2, dtype=jnp.int32)[None, :, None]
    b = jnp.arange(n2, dtype=jnp.int32)[None, None, :]
    ang = (2.0 * math.pi / n) * ((b * (c + n1 * d)) % n).astype(F32)
    re, im = jnp.cos(ang), -jnp.sin(ang)
    m1 = jnp.concatenate([jnp.concatenate([re, -im], axis=2), jnp.concatenate([im, re], axis=2)], axis=1)
    m2 = jnp.swapaxes(m1, 1, 2)
    ang4 = ang1[: n1 // 2]
    f4 = jnp.concatenate([jnp.cos(ang4), -jnp.sin(ang4)], axis=1) * (1.0 / n)
    return f1, _hi_lo_rows(m1), _hi_lo_rows(m2), _hi_lo_rows(_per_offset(f4))


def _hi_lo_rows(m):
    return jnp.concatenate(_hi_lo(m), axis=-2)


def _dot3s(ms, a):
    m = ms.shape[0] // 2
    ah, al = _hi_lo(a)
    t = jnp.dot(ms, ah, preferred_element_type=F32)
    return t[:m] + t[m:] + jnp.dot(ms[:m], al, preferred_element_type=F32)


def _dot3(mh, ml, a):
    ah, al = _hi_lo(a)
    return (jnp.dot(mh, ah, preferred_element_type=F32) + jnp.dot(mh, al, preferred_element_type=F32)
            + jnp.dot(ml, ah, preferred_element_type=F32))


HY_TB = SUBLANES


def _per_offset(f):
    return jnp.kron(f, jnp.eye(HY_TB, dtype=f.dtype))


def _hy_stage1_kernel(f_ref, x_ref, re_ref, im_ref):
    _, n1, tb, w = re_ref.shape
    y = _dot3s(f_ref[...], x_ref[0].reshape(-1, w))
    re_ref[0] = y[:n1 * tb].reshape(n1, tb, w)
    im_ref[0] = y[n1 * tb:].reshape(n1, tb, w)


def _hy_stage1(f, x4):
    bx, k, n2, w = x4.shape
    tb = HY_TB
    n1 = f.shape[0] // (4 * tb)
    out = jax.ShapeDtypeStruct((bx, n1, n2, w), F32)
    return pl.pallas_call(
        _hy_stage1_kernel,
        grid=(bx, n2 // tb),
        in_specs=[pl.BlockSpec(f.shape, lambda b, j: (0, 0))]
                 + [pl.BlockSpec((1, k, tb, w), lambda b, j: (b, 0, j, 0))],
        out_specs=[pl.BlockSpec((1, n1, tb, w), lambda b, j: (b, 0, j, 0))] * 2,
        out_shape=[out, out],
        compiler_params=_cp("parallel", "parallel"),
        name="hyena_dft_stage1",
    )(f, x4)


def _hy_mid_kernel(m1_ref, m2_ref, re_ref, im_ref, hre_ref, him_ref, ore_ref, oim_ref):
    n2 = re_ref.shape[2]
    for c in range(re_ref.shape[1]):
        a = jnp.concatenate([re_ref[0, c], im_ref[0, c]], axis=0)
        x = _dot3s(m1_ref[c], a)
        xre, xim = x[:n2], x[n2:]
        hre, him = hre_ref[c], him_ref[c]
        y = jnp.concatenate([xre * hre - xim * him, xre * him + xim * hre], axis=0)
        bb = _dot3s(m2_ref[c], y)
        ore_ref[0, c] = bb[:n2]
        oim_ref[0, c] = bb[n2:]


def _hy_spec_kernel(m1_ref, re_ref, im_ref, rs_ref, ore_ref, oim_ref):
    n2 = re_ref.shape[2]
    for c in range(re_ref.shape[1]):
        a = jnp.concatenate([re_ref[0, c], im_ref[0, c]], axis=0)
        x = _dot3s(m1_ref[c], a) * rs_ref[...]
        ore_ref[c] = x[:n2]
        oim_ref[c] = x[n2:]


def _hy_filter_spectrum(m1, are, aim, rs):
    _, n1, n2, w = are.shape
    cb = min(4, n1)
    blk = pl.BlockSpec((1, cb, n2, w), lambda c: (0, c, 0, 0))
    mblk = pl.BlockSpec((cb, 4 * n2, 2 * n2), lambda c: (c, 0, 0))
    oblk = pl.BlockSpec((cb, n2, w), lambda c: (c, 0, 0))
    out = jax.ShapeDtypeStruct((n1, n2, w), F32)
    return pl.pallas_call(
        _hy_spec_kernel,
        grid=(n1 // cb,),
        in_specs=[mblk, blk, blk, pl.BlockSpec((1, w), lambda c: (0, 0))],
        out_specs=[oblk, oblk],
        out_shape=[out, out],
        compiler_params=_cp("parallel"),
        name="hyena_filter_spectrum",
    )(m1, are, aim, rs)


def _hy_mid(m1, m2, are, aim, hre, him):
    bx, n1, n2, w = are.shape
    cb = min(16, n1)
    blk = pl.BlockSpec((1, cb, n2, w), lambda c, b: (b, c, 0, 0))
    mblk = pl.BlockSpec((cb, 4 * n2, 2 * n2), lambda c, b: (c, 0, 0))
    hblk = pl.BlockSpec((cb, n2, w), lambda c, b: (c, 0, 0))
    out = jax.ShapeDtypeStruct((bx, n1, n2, w), F32)
    return pl.pallas_call(
        _hy_mid_kernel,
        grid=(n1 // cb, bx),
        in_specs=[mblk, mblk, blk, blk, hblk, hblk],
        out_specs=[blk, blk],
        out_shape=[out, out],
        compiler_params=_cp("parallel", "parallel"),
        name="hyena_dft_mid",
    )(m1, m2, are, aim, hre, him)


def _hy_last_kernel(f_ref, re_ref, im_ref, p_ref, bias_ref, o_ref):
    _, k, tb, w = p_ref.shape
    spec = jnp.concatenate([re_ref[0].reshape(-1, w), im_ref[0].reshape(-1, w)], axis=0)
    y = _dot3s(f_ref[...], spec)
    o_ref[0] = y.reshape(k, tb, w) + p_ref[0] * bias_ref[...]


def _hy_last(f4, bre, bim, p4, bias):
    bx, n1, n2, w = bre.shape
    k = n1 // 2
    tb = HY_TB
    big = pl.BlockSpec((1, n1, tb, w), lambda b, j: (b, 0, j, 0))
    small = pl.BlockSpec((1, k, tb, w), lambda b, j: (b, 0, j, 0))
    return pl.pallas_call(
        _hy_last_kernel,
        grid=(bx, n2 // tb),
        in_specs=[pl.BlockSpec(f4.shape, lambda b, j: (0, 0)), big, big, small,
                  pl.BlockSpec((1, w), lambda b, j: (0, 0))],
        out_specs=small,
        out_shape=jax.ShapeDtypeStruct((bx, k, n2, w), F32),
        compiler_params=_cp("parallel", "parallel"),
        name="hyena_dft_last",
    )(f4, bre, bim, p4, bias)


def _hyena(zb, conv_w, conv_b, filt_params, hy_bias, consts):
    bx, lx, _ = zb.shape
    feat, (f1, m1, m2, f4) = consts
    n = 2 * lx
    n1, n2 = _dft_factors(n)
    w = HY_WIDTH
    filt, ssq = _hy_filter(lx, feat, *filt_params)
    rs = lax.rsqrt(ssq + EPS)
    fre, fim = _hy_stage1(_hi_lo_rows(_per_offset(f1)), filt.reshape(1, n1, n2, w))
    hre, him = _hy_filter_spectrum(m1, fre, fim, rs)
    p, x0 = _hy_prep(zb, conv_w, conv_b)
    k = n1 // 2
    p4 = p.reshape(bx, k, n2, w)
    are, aim = _hy_stage1(_hi_lo_rows(_per_offset(f1[:, :k])), p4)
    bre, bim = _hy_mid(m1, m2, are, aim, hre, him)
    t = _hy_last(f4, bre, bim, p4, hy_bias.reshape(1, w))
    return x0, t.reshape(bx, lx, w)


WA_SHIFT_MAX = 35.0


def _wattn_kernel(sk_ref, q_ref, bias_ref, *rest, banded, fixed):
    if banded:
        kp_ref, kc_ref, kn_ref, kx_ref, vp_ref, vc_ref, vn_ref, vx_ref, o_ref = rest
    else:
        kx_ref, vx_ref, o_ref = rest
    i = pl.program_id(1)
    last = pl.num_programs(1) - 1
    qb = q_ref.shape[1]
    top = lax.broadcasted_iota(jnp.int32, (2 * qb, 1), 0) < qb
    if banded:
        c = lax.broadcasted_iota(jnp.int32, (1, bias_ref.shape[2]), 1) - BLOCK
        outside = ((c < 0) & (i == 0)) | ((c >= qb) & (c < qb + BLOCK) & (i == last))
        edge = jnp.where(outside, NEG_INF, 0.0)
    for h in range(WA_KV_HEADS):
        kl = slice(h * LANES, (h + 1) * LANES)
        ql = h * WA_GROUP * LANES
        q = jnp.concatenate([q_ref[0, :, ql:ql + LANES], q_ref[0, :, ql + LANES:ql + 2 * LANES]], axis=0)
        if banded:
            kk = jnp.concatenate([kp_ref[0, :, kl], kc_ref[0, :, kl], kn_ref[0, :, kl], kx_ref[0, :, kl]], axis=0)
            vv = jnp.concatenate([vp_ref[0, :, kl], vc_ref[0, :, kl], vn_ref[0, :, kl], vx_ref[0, :, kl]], axis=0)
        else:
            kk, vv = kx_ref[0, :, kl], vx_ref[0, :, kl]
        s = lax.dot_general(q, kk, (((1,), (1,)), ((), ())), preferred_element_type=F32) + bias_ref[h]
        if banded:
            s = s + edge
        sk = jnp.where(top, sk_ref[h * WA_GROUP], sk_ref[h * WA_GROUP + 1])
        if fixed:
            p, sink_term = jnp.exp2(s), sk
        else:
            m = jnp.maximum(jnp.max(s, axis=-1, keepdims=True), sk)
            p, sink_term = jnp.exp2(s - m), jnp.exp2(sk - m)
        acc = jnp.dot(p.astype(BF16), vv, preferred_element_type=F32)
        o = (acc * (1.0 / (acc[:, WA_DIM:WA_DIM + 1] + sink_term))).astype(BF16)
        o_ref[0, :, ql:ql + 2 * LANES] = jnp.concatenate([o[:qb], o[qb:]], axis=1)


def _window_attention(q, k, v, kx, vx, sink, bound, banded):
    b, lq, _ = q.shape
    cx = kx.shape[1]
    qb = min(256, lq)
    per = qb // BLOCK
    nblk = lq // BLOCK
    kvw = WA_KV_HEADS * LANES
    side = lambda f: pl.BlockSpec((1, BLOCK, kvw), f)
    prev = side(lambda b_, i: (b_, jnp.maximum(i * per - 1, 0), 0))
    nxt = side(lambda b_, i: (b_, jnp.minimum((i + 1) * per, nblk - 1), 0))
    cur = pl.BlockSpec((1, qb, kvw), lambda b_, i: (b_, i, 0))
    ctx = pl.BlockSpec((1, cx, kvw), lambda b_, i: (b_, 0, 0))
    qspec = pl.BlockSpec((1, qb, WA_HEADS * LANES), lambda b_, i: (b_, i, 0))
    mask = jnp.zeros((WA_GROUP * qb, cx), F32)
    if banded:
        r = jnp.arange(WA_GROUP * qb)[:, None] % qb
        c = jnp.arange(qb + 2 * BLOCK)[None, :] - BLOCK
        mask = jnp.concatenate([jnp.where(jnp.abs(r - c) <= BLOCK, 0.0, NEG_INF).astype(F32), mask], axis=1)
        specs, args = [prev, cur, nxt, ctx, prev, cur, nxt, ctx], (k, k, k, kx, v, v, v, vx)
    else:
        specs, args = [ctx, ctx], (kx, vx)
    fixed = bound <= WA_SHIFT_MAX
    shift = jnp.maximum(bound, sink)
    rows = jnp.repeat(shift.reshape(WA_KV_HEADS, WA_GROUP), qb, axis=1)[:, :, None]
    bias = mask[None] - jnp.where(fixed, LOG2E * rows, 0.0)
    sk = jnp.where(fixed, jnp.exp2(LOG2E * (sink - shift)), LOG2E * sink)

    def call(fixed_, *ops):
        return pl.pallas_call(
            functools.partial(_wattn_kernel, banded=banded, fixed=fixed_),
            grid=(b, lq // qb),
            in_specs=[pl.BlockSpec(memory_space=pltpu.SMEM), qspec,
                      pl.BlockSpec(bias.shape, lambda b_, i: (0, 0, 0))] + specs,
            out_specs=qspec,
            out_shape=jax.ShapeDtypeStruct((b, lq, WA_HEADS * LANES), BF16),
            compiler_params=_cp("parallel", "arbitrary"),
            name="window_attention" if fixed_ else "window_attention_online",
        )(*ops)

    return lax.cond(fixed, functools.partial(call, True), functools.partial(call, False), sk, q, bias, *args)


def _conf_kernel(zp_ref, zc_ref, zn_ref, w_ref, b_ref, lg_ref, lb_ref, o_ref, ext, sh, *, halo):
    i = pl.program_id(1)
    last = pl.num_programs(1) - 1
    tl = zc_ref.shape[1]
    cw = CF_WIDTH

    def glu(z_ref):
        z = z_ref[0].astype(F32)
        return z[:, :cw] * jax.nn.sigmoid(z[:, cw:])

    ext[0:halo] = jnp.where(i == 0, 0.0, glu(zp_ref))
    ext[halo:halo + tl] = glu(zc_ref)
    ext[halo + tl:] = jnp.where(i == last, 0.0, glu(zn_ref))
    for r in range(1, SUBLANES):
        sh[r - 1] = ext[pl.ds(r, sh.shape[1]), :]
    w = w_ref[...]
    u = jnp.zeros((tl, cw), F32) + b_ref[...]
    for j in range(CF_TAPS):
        off = halo - CF_TAPS // 2 + j
        base, r = off // SUBLANES * SUBLANES, off % SUBLANES
        tap = ext[pl.ds(base, tl), :] if r == 0 else sh[r - 1, pl.ds(base, tl), :]
        u = u + tap * w[j:j + 1]
    uc = u - jnp.mean(u, axis=-1, keepdims=True)
    y = uc * lax.rsqrt(jnp.mean(uc * uc, axis=-1, keepdims=True) + EPS) * lg_ref[...] + lb_ref[...]
    o_ref[0] = _silu(y).astype(BF16)


def _conformer(zd, dw_w, dw_b, ln_g, ln_b):
    bx, lx, w = zd.shape
    tl = min(1024, lx)
    halo = 2 * SUBLANES
    nh = lx // halo
    per = tl // halo
    row = lambda a: a.reshape(1, -1)
    const = lambda b, i: (0, 0)
    return pl.pallas_call(
        functools.partial(_conf_kernel, halo=halo),
        grid=(bx, lx // tl),
        in_specs=[pl.BlockSpec((1, halo, w), lambda b, i: (b, jnp.maximum(i * per - 1, 0), 0)),
                  pl.BlockSpec((1, tl, w), lambda b, i: (b, i, 0)),
                  pl.BlockSpec((1, halo, w), lambda b, i: (b, jnp.minimum((i + 1) * per, nh - 1), 0)),
                  pl.BlockSpec(dw_w.shape, const)] + [pl.BlockSpec((1, CF_WIDTH), const)] * 3,
        out_specs=pl.BlockSpec((1, tl, CF_WIDTH), lambda b, i: (b, i, 0)),
        out_shape=jax.ShapeDtypeStruct((bx, lx, CF_WIDTH), BF16),
        scratch_shapes=[pltpu.VMEM((tl + 2 * halo, CF_WIDTH), F32),
                        pltpu.VMEM((SUBLANES - 1, tl + 2 * halo - SUBLANES, CF_WIDTH), F32)],
        compiler_params=_cp("parallel", "parallel"),
        name="conformer_conv",
    )(zd, zd, zd, dw_w, row(dw_b), row(ln_g), row(ln_b))


def _route_rows(x, m_ref, g_ref, wr_ref, br_ref, rows_ref, gi_ref):
    tm, d = x.shape
    half = d // 2
    lane = lax.broadcasted_iota(jnp.int32, (tm, LANES), 1).astype(F32)
    h = _rms(x) * g_ref[...] * (1.0 + m_ref[0, :, 4 * d:5 * d]) + m_ref[0, :, 3 * d:4 * d]
    bits = lax.bitcast_convert_type(h.astype(BF16).astype(F32), jnp.int32)
    rows_ref[0, :, :half] = bits[:, :half] | lax.shift_right_logical(bits[:, half:], 16)
    lg = _dot3(*_hi_lo(h), wr_ref[...]) + br_ref[...]
    isg = lane < N_GROUPS
    gmax = jnp.max(jnp.where(isg, lg, NEG_INF), axis=-1, keepdims=True)
    gi = jnp.min(jnp.where(isg & (lg == gmax), lane, LANES), axis=-1, keepdims=True)
    gw = 1.0 / jnp.sum(jnp.where(isg, jnp.exp(lg - gmax), 0.0), axis=-1, keepdims=True)
    lo = N_GROUPS + gi * EXP_PER_GROUP
    ise = (lane >= lo) & (lane < lo + EXP_PER_GROUP)
    le = jnp.where(ise, lg, NEG_INF)
    m1 = jnp.max(le, axis=-1, keepdims=True)
    i1 = jnp.min(jnp.where(ise & (le == m1), lane, LANES), axis=-1, keepdims=True)
    ise2 = ise & (lane != i1)
    le2 = jnp.where(ise2, lg, NEG_INF)
    m2 = jnp.max(le2, axis=-1, keepdims=True)
    i2 = jnp.min(jnp.where(ise2 & (le2 == m2), lane, LANES), axis=-1, keepdims=True)
    r = jnp.exp(m2 - m1)
    wa = gw / (1.0 + r)
    gates = jnp.where(lane == i1 - lo, wa, 0.0) + jnp.where(lane == i2 - lo, wa * r, 0.0)
    rows_ref[0, :, half:] = lax.bitcast_convert_type(gates, jnp.int32)
    gi_ref[0] = gi.astype(jnp.int32)


def _merge_kernel(x_ref, m_ref, ya, yb0, ybt, yc, yd, zg_ref, bg_ref, wa, wb, wc, wd, wo_ref, *rest, goff):
    o_ref = rest[-1] if len(rest) == 1 else rest[3]
    d = x_ref.shape[-1]
    ys = (ya[0], (yb0[0] * ybt[0]).astype(BF16), yc[0], yd[0])
    hw = d // 2
    out = jnp.zeros(x_ref.shape[1:], F32)
    for j in range(2):
        acc = jnp.zeros((x_ref.shape[1], hw), F32)
        for i, (y, w) in enumerate(zip(ys, (wa, wb, wc, wd))):
            c0 = i * d + j * hw
            gate = jax.nn.sigmoid(zg_ref[0, :, c0:c0 + hw].astype(F32) + bg_ref[:, c0:c0 + hw])
            acc = acc + gate * jnp.dot(y, w[:, j * hw:(j + 1) * hw], preferred_element_type=F32)
        out = out + jnp.dot(acc.astype(BF16), wo_ref[j * hw:(j + 1) * hw, :], preferred_element_type=F32)
    xn = x_ref[0] + m_ref[0, :, goff:goff + d] * out
    o_ref[0] = xn
    if len(rest) > 1:
        g_ref, wr_ref, br_ref, _, rows_ref, gi_ref = rest
        _route_rows(xn, m_ref, g_ref, wr_ref, br_ref, rows_ref, gi_ref)


def _merge(x, mod, ys, zg, b_gate, wbs, w_out, router=None):
    bx, lx, d = x.shape
    tm = min(512, lx)
    row = lambda b, i: (b, i, 0)
    const = lambda b, i: (0, 0)
    once = lambda a: pl.BlockSpec(a.shape, const, pipeline_mode=pl.Buffered(1))
    extra = (router[0].reshape(1, d), router[1], router[2]) if router else ()
    rw = d // 2 + LANES
    out_specs = [pl.BlockSpec((1, tm, d), row), pl.BlockSpec((1, tm, rw), row), pl.BlockSpec((1, tm, 1), row)]
    out_shape = [jax.ShapeDtypeStruct(x.shape, F32), jax.ShapeDtypeStruct((bx, lx, rw), jnp.int32),
                 jax.ShapeDtypeStruct((bx, lx, 1), jnp.int32)]
    return pl.pallas_call(
        functools.partial(_merge_kernel, goff=2 * d),
        grid=(bx, lx // tm),
        in_specs=[pl.BlockSpec((1, tm, d), row), pl.BlockSpec((1, 1, mod.shape[-1]), lambda b, i: (b, 0, 0))]
                 + [pl.BlockSpec((1, tm, y.shape[-1]), row) for y in ys]
                 + [pl.BlockSpec((1, tm, N_BRANCH * d), row), pl.BlockSpec((1, N_BRANCH * d), const)]
                 + [once(w) for w in wbs] + [once(w_out)] + [once(a) for a in extra],
        out_specs=out_specs if router else out_specs[0],
        out_shape=out_shape if router else out_shape[0],
        input_output_aliases={0: 0},
        compiler_params=_cp("parallel", "parallel"),
        name="merge",
    )(x, mod, *ys, zg, b_gate.reshape(1, -1), *wbs, w_out, *extra)


def _slot_rows(w, group):
    n = w.shape[0] // group
    return jnp.pad(w.reshape(n, group, -1), ((0, 0), (0, LANES - group), (0, 0))).reshape(n * LANES, -1)


def _moe_kernel(x_ref, m_ref, g_ref, wr_ref, br_ref, w1_ref, w3_ref, w2_ref, o_ref, h_sc, gate_sc, acc_sc):
    e = pl.program_id(2)
    d = x_ref.shape[-1]
    tm = x_ref.shape[1]
    lane = lax.broadcasted_iota(jnp.int32, (tm, LANES), 1).astype(F32)

    @pl.when(e == 0)
    def _():
        h = _rms(x_ref[0]) * g_ref[...] * (1.0 + m_ref[0, :, 4 * d:5 * d]) + m_ref[0, :, 3 * d:4 * d]
        h_sc[...] = h.astype(BF16)
        lg = _dot3(*_hi_lo(h), wr_ref[...]) + br_ref[...]
        isg = lane < N_GROUPS
        gmax = jnp.max(jnp.where(isg, lg, NEG_INF), axis=-1, keepdims=True)
        gi = jnp.min(jnp.where(isg & (lg == gmax), lane, LANES), axis=-1, keepdims=True)
        gw = 1.0 / jnp.sum(jnp.where(isg, jnp.exp(lg - gmax), 0.0), axis=-1, keepdims=True)
        lo = N_GROUPS + gi * EXP_PER_GROUP
        ise = (lane >= lo) & (lane < lo + EXP_PER_GROUP)
        le = jnp.where(ise, lg, NEG_INF)
        m1 = jnp.max(le, axis=-1, keepdims=True)
        i1 = jnp.min(jnp.where(ise & (le == m1), lane, LANES), axis=-1, keepdims=True)
        ise2 = ise & (lane != i1)
        le2 = jnp.where(ise2, lg, NEG_INF)
        m2 = jnp.max(le2, axis=-1, keepdims=True)
        i2 = jnp.min(jnp.where(ise2 & (le2 == m2), lane, LANES), axis=-1, keepdims=True)
        r = jnp.exp(m2 - m1)
        wa = gw / (1.0 + r)
        gate_sc[...] = jnp.where(lane == i1, wa, 0.0) + jnp.where(lane == i2, wa * r, 0.0)
        acc_sc[...] = jnp.zeros_like(acc_sc)

    hb = h_sc[...]
    u = (_silu(jnp.dot(hb, w1_ref[0].astype(BF16), preferred_element_type=F32))
         * jnp.dot(hb, w3_ref[0].astype(BF16), preferred_element_type=F32))
    ge = jnp.sum(jnp.where(lane == (e + N_GROUPS).astype(F32), gate_sc[...], 0.0), axis=-1, keepdims=True)
    acc_sc[...] += ge * jnp.dot(u.astype(BF16), w2_ref[0].astype(BF16), preferred_element_type=F32)

    @pl.when(e == pl.num_programs(2) - 1)
    def _():
        o_ref[0] = x_ref[0] + m_ref[0, :, 5 * d:6 * d] * acc_sc[...]


def _moe(x, mod, g, w_router, b_router, w1, w3, w2):
    bx, lx, d = x.shape
    tm = min(1024, lx)
    ne, _, f = w1.shape
    row = lambda b, i, e: (b, i, 0)
    const = lambda b, i, e: (0, 0)
    return pl.pallas_call(
        _moe_kernel,
        grid=(bx, lx // tm, ne),
        in_specs=[pl.BlockSpec((1, tm, d), row), pl.BlockSpec((1, 1, mod.shape[-1]), lambda b, i, e: (b, 0, 0)),
                  pl.BlockSpec((1, d), const), pl.BlockSpec((d, LANES), const), pl.BlockSpec((1, LANES), const),
                  pl.BlockSpec((1, d, f), lambda b, i, e: (e, 0, 0)), pl.BlockSpec((1, d, f), lambda b, i, e: (e, 0, 0)),
                  pl.BlockSpec((1, f, d), lambda b, i, e: (e, 0, 0))],
        out_specs=pl.BlockSpec((1, tm, d), row),
        out_shape=jax.ShapeDtypeStruct(x.shape, F32),
        scratch_shapes=[pltpu.VMEM((tm, d), BF16), pltpu.VMEM((tm, LANES), F32), pltpu.VMEM((tm, d), F32)],
        input_output_aliases={0: 0},
        compiler_params=_cp("parallel", "parallel", "arbitrary"),
        name="moe",
    )(x, mod, g.reshape(1, d), w_router, b_router, w1, w3, w2)


SC_CORES = 2
SC_SUBCORES = 16
SC_CHUNK = 64
MOE_ROWS = 1024


def _sc_gather(table, idx):
    n = idx.shape[0]
    w = table.shape[1]
    per = n // (SC_CORES * SC_SUBCORES)
    assert per * SC_CORES * SC_SUBCORES == n and per % SC_CHUNK == 0
    mesh = plsc.VectorSubcoreMesh(core_axis_name="c", subcore_axis_name="s")

    @functools.partial(
        pl.kernel, mesh=mesh, out_type=jax.ShapeDtypeStruct((n, w), table.dtype),
        scratch_types=[pltpu.VMEM((SC_CHUNK,), jnp.int32), pltpu.VMEM((SC_CHUNK, w), table.dtype),
                       pltpu.SemaphoreType.DMA],
        name="sc_row_gather")
    def gather(table_hbm, idx_hbm, out_hbm, idx_v, rows_v, sem):
        base = (lax.axis_index("s") * SC_CORES + lax.axis_index("c")) * per

        @pl.loop(0, per // SC_CHUNK)
        def _(j):
            off = pl.multiple_of(base + j * SC_CHUNK, SC_CHUNK)
            pltpu.sync_copy(idx_hbm.at[pl.ds(off, SC_CHUNK)], idx_v)
            pltpu.async_copy(table_hbm.at[idx_v], rows_v, sem).wait()
            pltpu.sync_copy(rows_v, out_hbm.at[pl.ds(off, SC_CHUNK)])

    return gather(table, idx)


def _sc_scatter(rows, idx, n_out):
    n, w = rows.shape
    per = n // (SC_CORES * SC_SUBCORES)
    assert per * SC_CORES * SC_SUBCORES == n and per % SC_CHUNK == 0
    mesh = plsc.VectorSubcoreMesh(core_axis_name="c", subcore_axis_name="s")

    @functools.partial(
        pl.kernel, mesh=mesh, out_type=jax.ShapeDtypeStruct((n_out, w), rows.dtype),
        scratch_types=[pltpu.VMEM((SC_CHUNK,), jnp.int32), pltpu.VMEM((SC_CHUNK, w), rows.dtype),
                       pltpu.SemaphoreType.DMA],
        name="sc_row_scatter")
    def scatter(rows_hbm, idx_hbm, out_hbm, idx_v, rows_v, sem):
        base = (lax.axis_index("s") * SC_CORES + lax.axis_index("c")) * per

        @pl.loop(0, per // SC_CHUNK)
        def _(j):
            off = pl.multiple_of(base + j * SC_CHUNK, SC_CHUNK)
            pltpu.sync_copy(idx_hbm.at[pl.ds(off, SC_CHUNK)], idx_v)
            pltpu.sync_copy(rows_hbm.at[pl.ds(off, SC_CHUNK)], rows_v)
            pltpu.async_copy(rows_v, out_hbm.at[idx_v], sem).wait()

    return scatter(rows, idx)


def _gmoe_kernel(grp_ref, nv_ref, xs_ref, w1_ref, w3_ref, w2_ref, o_ref, h_sc, acc_sc):
    i = pl.program_id(0)
    e = pl.program_id(1)
    tm, d = h_sc.shape
    half = d // 2
    valid = lax.broadcasted_iota(jnp.int32, (tm, 1), 0) < nv_ref[i]

    @pl.when(nv_ref[i] > 0)
    def _():
        @pl.when(e == 0)
        def _():
            pk = xs_ref[:, :half]
            h = jnp.concatenate([lax.bitcast_convert_type(pk & -65536, F32),
                                 lax.bitcast_convert_type(lax.shift_left(pk, 16), F32)], axis=1)
            h_sc[...] = jnp.where(valid, h, 0.0).astype(BF16)
            acc_sc[...] = jnp.zeros_like(acc_sc)

        hb = h_sc[...]
        u = (_silu(jnp.dot(hb, w1_ref[0].astype(BF16), preferred_element_type=F32))
             * jnp.dot(hb, w3_ref[0].astype(BF16), preferred_element_type=F32))
        lane = lax.broadcasted_iota(jnp.int32, (tm, LANES), 1)
        gates = lax.bitcast_convert_type(xs_ref[:, half:], F32)
        ge = jnp.sum(jnp.where(valid & (lane == e), gates, 0.0), axis=-1, keepdims=True)
        acc_sc[...] += ge * jnp.dot(u.astype(BF16), w2_ref[0].astype(BF16), preferred_element_type=F32)

    @pl.when(e == pl.num_programs(1) - 1)
    def _():
        o_ref[...] = jnp.where(nv_ref[i] > 0, acc_sc[...], 0.0)


def _grouped_moe(grp, nv, xs, w1, w3, w2):
    p, dw = xs.shape
    d = 2 * (dw - LANES)
    _, _, f = w1.shape
    wmap = lambda i, e, grp, nv: (grp[i] * EXP_PER_GROUP + e, 0, 0)
    rows = lambda i, e, grp, nv: (i, 0)
    return pl.pallas_call(
        _gmoe_kernel,
        grid_spec=pltpu.PrefetchScalarGridSpec(
            num_scalar_prefetch=2,
            grid=(p // MOE_ROWS, EXP_PER_GROUP),
            in_specs=[pl.BlockSpec((MOE_ROWS, dw), rows),
                      pl.BlockSpec((1, d, f), wmap), pl.BlockSpec((1, d, f), wmap), pl.BlockSpec((1, f, d), wmap)],
            out_specs=pl.BlockSpec((MOE_ROWS, d), rows),
            scratch_shapes=[pltpu.VMEM((MOE_ROWS, d), BF16), pltpu.VMEM((MOE_ROWS, d), F32)]),
        out_shape=jax.ShapeDtypeStruct((p, d), F32),
        compiler_params=_cp("arbitrary", "arbitrary"),
        name="moe_experts",
    )(grp, nv, xs, w1, w3, w2)


def _residual_kernel(x_ref, m_ref, y_ref, o_ref):
    d = x_ref.shape[-1]
    o_ref[0] = x_ref[0] + m_ref[0, :, 5 * d:6 * d] * y_ref[0]


def _residual(x, mod, y):
    bx, lx, d = x.shape
    tm = min(1024, lx)
    row = lambda b, i: (b, i, 0)
    return pl.pallas_call(
        _residual_kernel,
        grid=(bx, lx // tm),
        in_specs=[pl.BlockSpec((1, tm, d), row), pl.BlockSpec((1, 1, mod.shape[-1]), lambda b, i: (b, 0, 0)),
                  pl.BlockSpec((1, tm, d), row)],
        out_specs=pl.BlockSpec((1, tm, d), row),
        out_shape=jax.ShapeDtypeStruct(x.shape, F32),
        input_output_aliases={0: 0},
        compiler_params=_cp("parallel", "parallel"),
        name="moe_residual",
    )(x, mod, y)


def _moe_sorted(x, mod, rows, gi, w1, w3, w2):
    bx, lx, d = x.shape
    t = bx * lx
    gi = gi.reshape(t)
    onehot = (gi[:, None] == jnp.arange(N_GROUPS, dtype=jnp.int32)[None, :]).astype(jnp.int32)
    csum = jnp.cumsum(onehot, axis=0)
    counts = csum[-1]
    rank = jnp.take_along_axis(csum, gi[:, None], axis=1)[:, 0] - 1
    padded = (counts + MOE_ROWS - 1) // MOE_ROWS * MOE_ROWS
    pend = jnp.cumsum(padded)
    pstart = pend - padded
    pos = (pstart[gi] + rank).astype(jnp.int32)
    p = t + N_GROUPS * MOE_ROWS
    bstart = jnp.arange(p // MOE_ROWS, dtype=jnp.int32) * MOE_ROWS
    grp = jnp.minimum(jnp.searchsorted(pend, bstart, side="right"), N_GROUPS - 1).astype(jnp.int32)
    nv = jnp.clip(pstart[grp] + counts[grp] - bstart, 0, MOE_ROWS).astype(jnp.int32)
    xs = _sc_scatter(rows.reshape(t, rows.shape[-1]), pos, p)
    ys = _grouped_moe(grp, nv, xs, w1, w3, w2)
    yt = _sc_gather(ys, pos)
    return _residual(x, mod, yt.reshape(bx, lx, d))


def kernel(x, c, ctx, c_ctx, w_mod, b_mod, norm1_g, norm2_g, w_in, b_gate, da_qn, da_kn, da_lam, da_subln, hy_conv_w, hy_conv_b, hf_w1, hf_b1, hf_w2, hf_b2, hf_w3, hf_b3, hf_w4, hf_freq, hy_bias, wa_qn, wa_kn, wa_sink, cf_dw_w, cf_dw_b, cf_ln_g, cf_ln_b, w_branch, w_out, w_rg, b_rg, w_re, b_re, w1, w3, w2):
    b, s, d = x.shape
    cl = ctx.shape[1]
    depth = w_mod.shape[0]
    assert s % 256 == 0 and cl % 256 == 0 and s % GRID_W == 0

    nrow = -(-(b + 1) // SUBLANES) * SUBLANES
    crows = jnp.zeros((nrow, d), F32).at[:b].set(c).at[b].set(c_ctx)
    mods = _mod_vectors(crows, w_mod, b_mod)

    aw, qw, kw = DA_HEADS * DA_DIM, WA_HEADS * WA_DIM, WA_KV_HEADS * WA_DIM
    tab_lat = (*_rope_tables(s, DA_DIM, DA_HEADS), *_rope_tables(s, WA_DIM, WA_HEADS), *_rope_tables(s, WA_DIM, WA_KV_HEADS))
    tab_ctx = (*_unit_tables(cl, aw), *_unit_tables(cl, qw), *_unit_tables(cl, kw))
    hy_lat = (_hy_feat(s), _dft_tables(2 * s))
    hy_ctx = (_hy_feat(cl), _dft_tables(2 * cl))
    gms = (_group_ones(aw, DA_DIM), _group_ones(qw, WA_DIM), _group_ones(kw, WA_DIM))
    qvec = _slot_fill(DA_DIM, DA_DIM + 1, 1.0)
    vvec = _slot_fill(DA_VDIM, DA_VDIM + DA_ONES, 1.0)

    w_in_b = w_in.astype(BF16)
    xc = ctx
    for l in range(depth):
        last = l == depth - 1
        lam_init = 0.8 - 0.6 * math.exp(-0.3 * l)
        mod_x = mods[l, :b][:, None, :]
        mod_c = jnp.broadcast_to(mods[l, b][None, None, :], (b, 1, mods.shape[-1]))
        shift = 1.02 * LOG2E * DA_DIM ** 0.5 * jnp.max(jnp.abs(da_qn[l])) * jnp.max(jnp.abs(da_kn[l]))
        fixed = shift <= DA_SHIFT_MAX
        kvec = _slot_fill(DA_DIM, DA_DIM + 1, jnp.where(fixed, -shift, 0.0))
        tile = lambda a, n: jnp.tile(a, n).reshape(1, -1)
        consts = (*gms, tile(da_qn[l], DA_HEADS), tile(da_kn[l], DA_HEADS), tile(wa_qn[l], WA_HEADS),
                  tile(wa_kn[l], WA_KV_HEADS), qvec, kvec, vvec)
        q1, q2, k1, k2, v, zb, qc, kc, vc, zd, zg = _in_proj(x, mod_x, norm1_g[l], w_in_b, l, tab_lat, consts)
        q1x, q2x, k1x, k2x, vx, zbx, qcx, kcx, vcx, zdx, zgx = _in_proj(xc, mod_c, norm1_g[l], w_in_b, l, tab_ctx, consts)
        filt_params = (hf_w1[l], hf_b1[l], hf_w2[l], hf_b2[l], hf_w3[l], hf_b3[l], hf_w4[l], hf_freq[l])
        wb4 = w_branch[l].astype(BF16)
        wbs = (_slot_rows(wb4[0], DA_VDIM), wb4[1], _slot_rows(wb4[2], WA_DIM), wb4[3])
        wo = w_out[l].astype(BF16)

        ya = _diff_attention(q1, q2, [(k1, k2, v), (k1x, k2x, vx)], fixed, da_lam[l], da_subln[l], lam_init)
        yb = _hyena(zb, hy_conv_w[l], hy_conv_b[l], filt_params, hy_bias[l], hy_lat)
        wbound = 1.02 * WA_DIM ** 0.5 * jnp.max(jnp.abs(wa_qn[l])) * jnp.max(jnp.abs(wa_kn[l]))
        yc_ = _window_attention(qc, kc, vc, kcx, vcx, wa_sink[l], wbound, True)
        yd = _conformer(zd, cf_dw_w[l], cf_dw_b[l], cf_ln_g[l], cf_ln_b[l])
        w_router = jnp.pad(jnp.concatenate([w_rg[l], w_re[l]], axis=1), ((0, 0), (0, LANES - N_GROUPS - N_EXPERTS)))
        b_router = jnp.pad(jnp.concatenate([b_rg[l], b_re[l]]), (0, LANES - N_GROUPS - N_EXPERTS)).reshape(1, LANES)
        ew = (w1[l], w3[l], w2[l])
        x, rows, gi = _merge(x, mod_x, (ya, *yb, yc_, yd), zg, b_gate[l], wbs, wo, (norm2_g[l], w_router, b_router))

        if not last:
            yca = _diff_attention(q1x, q2x, [(k1x, k2x, vx)], fixed, da_lam[l], da_subln[l], lam_init)
            ycb = _hyena(zbx, hy_conv_w[l], hy_conv_b[l], filt_params, hy_bias[l], hy_ctx)
            ycc = _window_attention(qcx, kcx, vcx, kcx, vcx, wa_sink[l], wbound, False)
            ycd = _conformer(zdx, cf_dw_w[l], cf_dw_b[l], cf_ln_g[l], cf_ln_b[l])
            xc = _merge(xc, mod_c, (yca, *ycb, ycc, ycd), zgx, b_gate[l], wbs, wo)
            xc = _moe(xc.reshape(1, b * cl, d), mod_c[:1], norm2_g[l], w_router, b_router, *ew).reshape(b, cl, d)
        x = _moe_sorted(x, mod_x, rows, gi, *ew)
    return x
```
